```python
import math
import jax
import jax.numpy as jnp
from jax import lax
import numpy as np


D_MODEL = 1024
BATCH = 16
SEQ = 4096
DEPTH = 2

PLE_DIM = 256
MLP_HIDDEN = 4 * D_MODEL
EPS = 1e-6
LRU_HEADS = 8
LRU_HEAD_DIM = 64
LRU_WIDTH = LRU_HEADS * LRU_HEAD_DIM
LRU_CONV = 4
LRU_C = 8.0
SB_HEADS = 8
SB_HEAD_DIM = 64
SB_WIDTH = SB_HEADS * SB_HEAD_DIM
SB_BLOCK = 128
EVEN_IN = 2 * LRU_WIDTH + 3 * SB_WIDTH
EVEN_OUT = LRU_WIDTH + SB_WIDTH
SSM_HEADS = 16
SSM_HEAD_DIM = 64
SSM_WIDTH = SSM_HEADS * SSM_HEAD_DIM
SSM_GROUPS = 2
SSM_STATE = 128
SSM_CONV = 4
SSM_CHUNK = 128
SSM_XBC = SSM_WIDTH + 2 * SSM_GROUPS * SSM_STATE
CONF_WIDTH = 512
CONF_KERNEL = 31
ODD_IN = SSM_WIDTH + SSM_XBC + SSM_HEADS + 2 * CONF_WIDTH
ODD_OUT = SSM_WIDTH + CONF_WIDTH
N_EVEN = (DEPTH + 1) // 2
N_ODD = DEPTH // 2

kernel_name = 'hybrid_rglru_stickbreak_ssd_conformer'


def rmsnorm(x, g):
    xf = x.astype(jnp.float32)
    y = xf * lax.rsqrt(jnp.mean(xf * xf, axis=-1, keepdims=True) + EPS)
    return (y * g.astype(jnp.float32)).astype(x.dtype)


def group_rmsnorm(x, g, groups):
    xf = x.astype(jnp.float32)
    xg = xf.reshape(xf.shape[:-1] + (groups, xf.shape[-1] // groups))
    xg = xg * lax.rsqrt(jnp.mean(xg * xg, axis=-1, keepdims=True) + EPS)
    return (xg.reshape(xf.shape) * g.astype(jnp.float32)).astype(x.dtype)


def layernorm(x, g, b):
    xf = x.astype(jnp.float32)
    xc = xf - jnp.mean(xf, axis=-1, keepdims=True)
    y = xc * lax.rsqrt(jnp.mean(xc * xc, axis=-1, keepdims=True) + EPS)
    return (y * g.astype(jnp.float32) + b.astype(jnp.float32)).astype(x.dtype)


def causal_dwconv(x, w, b):
    k = w.shape[0]
    y = lax.conv_general_dilated(
        x, w[:, None, :].astype(x.dtype), window_strides=(1,), padding=[(k - 1, 0)],
        dimension_numbers=('NWC', 'WIO', 'NWC'), feature_group_count=x.shape[-1])
    return y + b.astype(x.dtype)


def rg_lru(x, ga_w, ga_b, gx_w, gx_b, lam):
    nb, ns, w = x.shape
    xf = x.astype(jnp.float32)
    xh = xf.reshape(nb, ns, LRU_HEADS, LRU_HEAD_DIM)
    r = jax.nn.sigmoid(jnp.einsum('bshi,hij->bshj', xh, ga_w).reshape(nb, ns, w) + ga_b)
    i = jax.nn.sigmoid(jnp.einsum('bshi,hij->bshj', xh, gx_w).reshape(nb, ns, w) + gx_b)
    log_a = -LRU_C * r * jax.nn.softplus(-lam.astype(jnp.float32))
    a = jnp.exp(log_a)
    u = jnp.sqrt(-jnp.expm1(2.0 * log_a)) * (i * xf)

    def combine(c1, c2):
        return c1[0] * c2[0], c2[0] * c1[1] + c2[1]

    _, h = lax.associative_scan(combine, (a, u), axis=1)
    return h


def stick_breaking(q, k, v):
    nb, ns, nh, dh = q.shape
    scale = 1.0 / math.sqrt(dh)
    qh = q.astype(jnp.float32).transpose(0, 2, 1, 3) * scale
    kh = k.astype(jnp.float32).transpose(0, 2, 1, 3)
    vh = v.astype(jnp.float32).transpose(0, 2, 1, 3)
    outs = []
    for blk in range(ns // SB_BLOCK):
        s0 = blk * SB_BLOCK
        end = s0 + SB_BLOCK
        logits = jnp.einsum('bhqd,bhkd->bhqk', qh[:, :, s0:end], kh[:, :, :end])
        strict = jnp.arange(end)[None, :] < (s0 + jnp.arange(SB_BLOCK))[:, None]
        log_skip = jnp.where(strict, jax.nn.log_sigmoid(-logits), 0.0)
        log_after = lax.cumsum(log_skip, axis=3, reverse=True) - log_skip
        wts = jnp.where(strict, jnp.exp(jax.nn.log_sigmoid(logits) + log_after), 0.0)
        outs.append(jnp.einsum('bhqk,bhkd->bhqd', wts, vh[:, :, :end]))
    o = jnp.concatenate(outs, axis=2)
    return o.transpose(0, 2, 1, 3).reshape(nb, ns, nh * dh)


def segsum_exp(a_cs):
    n = a_cs.shape[-1]
    mask = jnp.tril(jnp.ones((n, n), dtype=bool))
    diff = a_cs[..., :, None] - a_cs[..., None, :]
    return jnp.where(mask, jnp.exp(jnp.where(mask, diff, 0.0)), 0.0)


def ssd(x, dt, a, bm, cm):
    nb, ns, nh, hp = x.shape
    g = bm.shape[2]
    kpg = nh // g
    n = bm.shape[-1]
    cl = SSM_CHUNK
    nc = ns // cl
    xdt = (x * dt[..., None]).reshape(nb, nc, cl, g, kpg, hp)
    da = (dt * a).reshape(nb, nc, cl, g, kpg).transpose(0, 3, 4, 1, 2)
    a_cs = jnp.cumsum(da, axis=-1)
    bc = bm.reshape(nb, nc, cl, g, n)
    cc = cm.reshape(nb, nc, cl, g, n)
    cb = jnp.einsum('bclgn,bcsgn->bcgls', cc, bc)
    y_diag = jnp.einsum('bcgls,bgkcls,bcsgkp->bclgkp', cb, segsum_exp(a_cs), xdt)
    decay_to_end = jnp.exp(a_cs[..., -1:] - a_cs)
    states = jnp.einsum('bclgn,bgkcl,bclgkp->bcgkpn', bc, decay_to_end, xdt)
    chunk_decay = jnp.exp(a_cs[..., -1])

    def step(h, inp):
        st, dec = inp
        return dec[..., None, None] * h + st, h

    h0 = jnp.zeros((nb, g, kpg, hp, n), states.dtype)
    _, prev = lax.scan(step, h0, (jnp.moveaxis(states, 1, 0), jnp.moveaxis(chunk_decay, -1, 0)))
    prev = jnp.moveaxis(prev, 0, 1)
    y_off = jnp.einsum('bclgn,bcgkpn,bgkcl->bclgkp', cc, prev, jnp.exp(a_cs))
    return (y_diag + y_off).reshape(nb, ns, nh, hp)


def even_mixer(h, w_in, conv_w, conv_b, ga_w, ga_b, gx_w, gx_b, lam, w_out):
    nb, ns, _ = h.shape
    proj = h @ w_in
    lru_x, lru_gate, q, k, v = jnp.split(
        proj, [LRU_WIDTH, 2 * LRU_WIDTH, 2 * LRU_WIDTH + SB_WIDTH, 2 * LRU_WIDTH + 2 * SB_WIDTH], axis=-1)
    lru_x = causal_dwconv(lru_x, conv_w, conv_b)
    y_a = rg_lru(lru_x, ga_w, ga_b, gx_w, gx_b, lam) * jax.nn.gelu(lru_gate.astype(jnp.float32))
    shp = (nb, ns, SB_HEADS, SB_HEAD_DIM)
    y_b = stick_breaking(q.reshape(shp), k.reshape(shp), v.reshape(shp))
    y = jnp.concatenate([y_a, y_b], axis=-1).astype(h.dtype)
    return y @ w_out


def odd_mixer(h, w_in, conv_w, conv_b, dt_bias, a_log, d_skip, ssm_norm,
              cm_conv_w, cm_conv_b, cm_ln_g, cm_ln_b, w_out):
    nb, ns, _ = h.shape
    proj = h @ w_in
    z, xbc, dt_raw, glu_in = jnp.split(
        proj, [SSM_WIDTH, SSM_WIDTH + SSM_XBC, SSM_WIDTH + SSM_XBC + SSM_HEADS], axis=-1)
    xbc = jax.nn.silu(causal_dwconv(xbc, conv_w, conv_b).astype(jnp.float32))
    xs, bm, cm = jnp.split(xbc, [SSM_WIDTH, SSM_WIDTH + SSM_GROUPS * SSM_STATE], axis=-1)
    xs = xs.reshape(nb, ns, SSM_HEADS, SSM_HEAD_DIM)
    dt = jax.nn.softplus(dt_raw.astype(jnp.float32) + dt_bias.astype(jnp.float32))
    a = -jnp.exp(a_log.astype(jnp.float32))
    y = ssd(xs, dt, a, bm.reshape(nb, ns, SSM_GROUPS, SSM_STATE), cm.reshape(nb, ns, SSM_GROUPS, SSM_STATE))
    y = y + d_skip.astype(jnp.float32)[:, None] * xs
    y = y.reshape(nb, ns, SSM_WIDTH) * jax.nn.silu(z.astype(jnp.float32))
    y_c = group_rmsnorm(y, ssm_norm, SSM_GROUPS)
    glu_a, glu_b = jnp.split(glu_in, 2, axis=-1)
    u = glu_a * jax.nn.sigmoid(glu_b)
    u = causal_dwconv(u, cm_conv_w, cm_conv_b)
    y_d = jax.nn.silu(layernorm(u, cm_ln_g, cm_ln_b).astype(jnp.float32))
    y = jnp.concatenate([y_c, y_d], axis=-1).astype(h.dtype)
    return y @ w_out


def _fwd_setup_inputs(seed: int = 0) -> dict:
    key = jax.random.key(seed)
    ks = iter(jax.random.split(key, 40))

    def nrm(shape, scale):
        return jax.random.normal(next(ks), shape, jnp.float32) * scale

    def gain(shape):
        return 1.0 + nrm(shape, 0.05)

    def unif(shape, lo, hi):
        return jax.random.uniform(next(ks), shape, jnp.float32, lo, hi)

    e, o = N_EVEN, N_ODD
    x = nrm((BATCH, SEQ, D_MODEL), 1.0)
    p = nrm((DEPTH, BATCH, SEQ, PLE_DIM), 1.0)
    a0 = unif((e, LRU_WIDTH), 0.9, 0.999)
    dt0 = jnp.exp(unif((o, SSM_HEADS), math.log(1e-3), math.log(1e-1)))
    return {
        'x': x,
        'p': p,
        'ev_w_in': nrm((e, D_MODEL, EVEN_IN), D_MODEL ** -0.5),
        'ev_lru_conv_w': nrm((e, LRU_CONV, LRU_WIDTH), LRU_CONV ** -0.5),
        'ev_lru_conv_b': nrm((e, LRU_WIDTH), 0.02),
        'ev_lru_gate_a_w': nrm((e, LRU_HEADS, LRU_HEAD_DIM, LRU_HEAD_DIM), LRU_HEAD_DIM ** -0.5),
        'ev_lru_gate_a_b': nrm((e, LRU_WIDTH), 0.02),
        'ev_lru_gate_x_w': nrm((e, LRU_HEADS, LRU_HEAD_DIM, LRU_HEAD_DIM), LRU_HEAD_DIM ** -0.5),
        'ev_lru_gate_x_b': nrm((e, LRU_WIDTH), 0.02),
        'ev_lru_lambda': jnp.log(a0) - jnp.log1p(-a0),
        'ev_w_out': nrm((e, EVEN_OUT, D_MODEL), EVEN_OUT ** -0.5),
        'od_w_in': nrm((o, D_MODEL, ODD_IN), D_MODEL ** -0.5),
        'od_ssm_conv_w': nrm((o, SSM_CONV, SSM_XBC), SSM_CONV ** -0.5),
        'od_ssm_conv_b': nrm((o, SSM_XBC), 0.02),
        'od_ssm_dt_bias': dt0 + jnp.log(-jnp.expm1(-dt0)),
        'od_ssm_a_log': jnp.log(unif((o, SSM_HEADS), 1.0, 16.0)),
        'od_ssm_d': gain((o, SSM_HEADS)),
        'od_ssm_norm': gain((o, SSM_WIDTH)),
        'od_cm_conv_w': nrm((o, CONF_KERNEL, CONF_WIDTH), CONF_KERNEL ** -0.5),
        'od_cm_conv_b': nrm((o, CONF_WIDTH), 0.02),
        'od_cm_ln_g': gain((o, CONF_WIDTH)),
        'od_cm_ln_b': nrm((o, CONF_WIDTH), 0.02),
        'od_w_out': nrm((o, ODD_OUT, D_MODEL), ODD_OUT ** -0.5),
        'norm_mix_pre': gain((DEPTH, D_MODEL)),
        'norm_mix_post': gain((DEPTH, D_MODEL)),
        'norm_mlp_pre': gain((DEPTH, D_MODEL)),
        'norm_mlp_post': gain((DEPTH, D_MODEL)),
        'norm_ple': gain((DEPTH, D_MODEL)),
        'mlp_w1': nrm((DEPTH, D_MODEL, MLP_HIDDEN), D_MODEL ** -0.5),
        'mlp_w2': nrm((DEPTH, MLP_HIDDEN, D_MODEL), MLP_HIDDEN ** -0.5),
        'ple_w_proj': nrm((DEPTH, PLE_DIM, D_MODEL), PLE_DIM ** -0.5),
        'ple_w_gate': nrm((DEPTH, D_MODEL, D_MODEL), D_MODEL ** -0.5),
    }


def _fwd_reference(x, p, ev_w_in, ev_lru_conv_w, ev_lru_conv_b, ev_lru_gate_a_w, ev_lru_gate_a_b,
              ev_lru_gate_x_w, ev_lru_gate_x_b, ev_lru_lambda, ev_w_out,
              od_w_in, od_ssm_conv_w, od_ssm_conv_b, od_ssm_dt_bias, od_ssm_a_log, od_ssm_d,
              od_ssm_norm, od_cm_conv_w, od_cm_conv_b, od_cm_ln_g, od_cm_ln_b, od_w_out,
              norm_mix_pre, norm_mix_post, norm_mlp_pre, norm_mlp_post, norm_ple,
              mlp_w1, mlp_w2, ple_w_proj, ple_w_gate):
    h = x
    for i in range(DEPTH):
        j = i // 2
        hn = rmsnorm(h, norm_mix_pre[i])
        if i % 2 == 0:
            m = even_mixer(hn, ev_w_in[j], ev_lru_conv_w[j], ev_lru_conv_b[j],
                           ev_lru_gate_a_w[j], ev_lru_gate_a_b[j], ev_lru_gate_x_w[j],
                           ev_lru_gate_x_b[j], ev_lru_lambda[j], ev_w_out[j])
        else:
            m = odd_mixer(hn, od_w_in[j], od_ssm_conv_w[j], od_ssm_conv_b[j], od_ssm_dt_bias[j],
                          od_ssm_a_log[j], od_ssm_d[j], od_ssm_norm[j], od_cm_conv_w[j],
                          od_cm_conv_b[j], od_cm_ln_g[j], od_cm_ln_b[j], od_w_out[j])
        h = h + rmsnorm(m, norm_mix_post[i])
        hn = rmsnorm(h, norm_mlp_pre[i])
        f = jnp.square(jax.nn.relu(hn @ mlp_w1[i])) @ mlp_w2[i]
        h = h + rmsnorm(f, norm_mlp_post[i])
        gate = jax.nn.sigmoid(h @ ple_w_gate[i])
        emb = p[i] @ ple_w_proj[i]
        h = h + rmsnorm(gate * emb, norm_ple[i])
    return h


import jax as _jax
import jax.numpy as _jnp

TWIN_FORMAT = 'train_step'
FWD_PARAMS = ['x', 'p', 'ev_w_in', 'ev_lru_conv_w', 'ev_lru_conv_b', 'ev_lru_gate_a_w', 'ev_lru_gate_a_b', 'ev_lru_gate_x_w', 'ev_lru_gate_x_b', 'ev_lru_lambda', 'ev_w_out', 'od_w_in', 'od_ssm_conv_w', 'od_ssm_conv_b', 'od_ssm_dt_bias', 'od_ssm_a_log', 'od_ssm_d', 'od_ssm_norm', 'od_cm_conv_w', 'od_cm_conv_b', 'od_cm_ln_g', 'od_cm_ln_b', 'od_w_out', 'norm_mix_pre', 'norm_mix_post', 'norm_mlp_pre', 'norm_mlp_post', 'norm_ple', 'mlp_w1', 'mlp_w2', 'ple_w_proj', 'ple_w_gate']
TWIN_WEIGHTS = ['ev_w_in', 'ev_lru_conv_w', 'ev_lru_conv_b', 'ev_lru_gate_a_w', 'ev_lru_gate_a_b', 'ev_lru_gate_x_w', 'ev_lru_gate_x_b', 'ev_lru_lambda', 'ev_w_out', 'od_w_in', 'od_ssm_conv_w', 'od_ssm_conv_b', 'od_ssm_dt_bias', 'od_ssm_a_log', 'od_ssm_d', 'od_ssm_norm', 'od_cm_conv_w', 'od_cm_conv_b', 'od_cm_ln_g', 'od_cm_ln_b', 'od_w_out', 'norm_mix_pre', 'norm_mix_post', 'norm_mlp_pre', 'norm_mlp_post', 'norm_ple', 'mlp_w1', 'mlp_w2', 'ple_w_proj', 'ple_w_gate']
TWIN_DIFF_INPUT = 'x'
TWIN_INPUTS = ['x', 'p', 'ev_w_in', 'ev_lru_conv_w', 'ev_lru_conv_b', 'ev_lru_gate_a_w', 'ev_lru_gate_a_b', 'ev_lru_gate_x_w', 'ev_lru_gate_x_b', 'ev_lru_lambda', 'ev_w_out', 'od_w_in', 'od_ssm_conv_w', 'od_ssm_conv_b', 'od_ssm_dt_bias', 'od_ssm_a_log', 'od_ssm_d', 'od_ssm_norm', 'od_cm_conv_w', 'od_cm_conv_b', 'od_cm_ln_g', 'od_cm_ln_b', 'od_w_out', 'norm_mix_pre', 'norm_mix_post', 'norm_mlp_pre', 'norm_mlp_post', 'norm_ple', 'mlp_w1', 'mlp_w2', 'ple_w_proj', 'ple_w_gate', 'loss_target', 'm_ev_w_in', 'm_ev_lru_conv_w', 'm_ev_lru_conv_b', 'm_ev_lru_gate_a_w', 'm_ev_lru_gate_a_b', 'm_ev_lru_gate_x_w', 'm_ev_lru_gate_x_b', 'm_ev_lru_lambda', 'm_ev_w_out', 'm_od_w_in', 'm_od_ssm_conv_w', 'm_od_ssm_conv_b', 'm_od_ssm_dt_bias', 'm_od_ssm_a_log', 'm_od_ssm_d', 'm_od_ssm_norm', 'm_od_cm_conv_w', 'm_od_cm_conv_b', 'm_od_cm_ln_g', 'm_od_cm_ln_b', 'm_od_w_out', 'm_norm_mix_pre', 'm_norm_mix_post', 'm_norm_mlp_pre', 'm_norm_mlp_post', 'm_norm_ple', 'm_mlp_w1', 'm_mlp_w2', 'm_ple_w_proj', 'm_ple_w_gate', 'v_ev_w_in', 'v_ev_lru_conv_w', 'v_ev_lru_conv_b', 'v_ev_lru_gate_a_w', 'v_ev_lru_gate_a_b', 'v_ev_lru_gate_x_w', 'v_ev_lru_gate_x_b', 'v_ev_lru_lambda', 'v_ev_w_out', 'v_od_w_in', 'v_od_ssm_conv_w', 'v_od_ssm_conv_b', 'v_od_ssm_dt_bias', 'v_od_ssm_a_log', 'v_od_ssm_d', 'v_od_ssm_norm', 'v_od_cm_conv_w', 'v_od_cm_conv_b', 'v_od_cm_ln_g', 'v_od_cm_ln_b', 'v_od_w_out', 'v_norm_mix_pre', 'v_norm_mix_post', 'v_norm_mlp_pre', 'v_norm_mlp_post', 'v_norm_ple', 'v_mlp_w1', 'v_mlp_w2', 'v_ple_w_proj', 'v_ple_w_gate']
TWIN_OUTPUTS = ['loss', 'grad_x', 'grad_ev_w_in', 'grad_ev_lru_conv_w', 'grad_ev_lru_conv_b', 'grad_ev_lru_gate_a_w', 'grad_ev_lru_gate_a_b', 'grad_ev_lru_gate_x_w', 'grad_ev_lru_gate_x_b', 'grad_ev_lru_lambda', 'grad_ev_w_out', 'grad_od_w_in', 'grad_od_ssm_conv_w', 'grad_od_ssm_conv_b', 'grad_od_ssm_dt_bias', 'grad_od_ssm_a_log', 'grad_od_ssm_d', 'grad_od_ssm_norm', 'grad_od_cm_conv_w', 'grad_od_cm_conv_b', 'grad_od_cm_ln_g', 'grad_od_cm_ln_b', 'grad_od_w_out', 'grad_norm_mix_pre', 'grad_norm_mix_post', 'grad_norm_mlp_pre', 'grad_norm_mlp_post', 'grad_norm_ple', 'grad_mlp_w1', 'grad_mlp_w2', 'grad_ple_w_proj', 'grad_ple_w_gate', 'delta_ev_w_in', 'delta_ev_lru_conv_w', 'delta_ev_lru_conv_b', 'delta_ev_lru_gate_a_w', 'delta_ev_lru_gate_a_b', 'delta_ev_lru_gate_x_w', 'delta_ev_lru_gate_x_b', 'delta_ev_lru_lambda', 'delta_ev_w_out', 'delta_od_w_in', 'delta_od_ssm_conv_w', 'delta_od_ssm_conv_b', 'delta_od_ssm_dt_bias', 'delta_od_ssm_a_log', 'delta_od_ssm_d', 'delta_od_ssm_norm', 'delta_od_cm_conv_w', 'delta_od_cm_conv_b', 'delta_od_cm_ln_g', 'delta_od_cm_ln_b', 'delta_od_w_out', 'delta_norm_mix_pre', 'delta_norm_mix_post', 'delta_norm_mlp_pre', 'delta_norm_mlp_post', 'delta_norm_ple', 'delta_mlp_w1', 'delta_mlp_w2', 'delta_ple_w_proj', 'delta_ple_w_gate', 'new_m_ev_w_in', 'new_m_ev_lru_conv_w', 'new_m_ev_lru_conv_b', 'new_m_ev_lru_gate_a_w', 'new_m_ev_lru_gate_a_b', 'new_m_ev_lru_gate_x_w', 'new_m_ev_lru_gate_x_b', 'new_m_ev_lru_lambda', 'new_m_ev_w_out', 'new_m_od_w_in', 'new_m_od_ssm_conv_w', 'new_m_od_ssm_conv_b', 'new_m_od_ssm_dt_bias', 'new_m_od_ssm_a_log', 'new_m_od_ssm_d', 'new_m_od_ssm_norm', 'new_m_od_cm_conv_w', 'new_m_od_cm_conv_b', 'new_m_od_cm_ln_g', 'new_m_od_cm_ln_b', 'new_m_od_w_out', 'new_m_norm_mix_pre', 'new_m_norm_mix_post', 'new_m_norm_mlp_pre', 'new_m_norm_mlp_post', 'new_m_norm_ple', 'new_m_mlp_w1', 'new_m_mlp_w2', 'new_m_ple_w_proj', 'new_m_ple_w_gate', 'new_v_ev_w_in', 'new_v_ev_lru_conv_w', 'new_v_ev_lru_conv_b', 'new_v_ev_lru_gate_a_w', 'new_v_ev_lru_gate_a_b', 'new_v_ev_lru_gate_x_w', 'new_v_ev_lru_gate_x_b', 'new_v_ev_lru_lambda', 'new_v_ev_w_out', 'new_v_od_w_in', 'new_v_od_ssm_conv_w', 'new_v_od_ssm_conv_b', 'new_v_od_ssm_dt_bias', 'new_v_od_ssm_a_log', 'new_v_od_ssm_d', 'new_v_od_ssm_norm', 'new_v_od_cm_conv_w', 'new_v_od_cm_conv_b', 'new_v_od_cm_ln_g', 'new_v_od_cm_ln_b', 'new_v_od_w_out', 'new_v_norm_mix_pre', 'new_v_norm_mix_post', 'new_v_norm_mlp_pre', 'new_v_norm_mlp_post', 'new_v_norm_ple', 'new_v_mlp_w1', 'new_v_mlp_w2', 'new_v_ple_w_proj', 'new_v_ple_w_gate']
TWIN_LEAF_KINDS = {'loss': 'loss', 'grad_x': 'grad_x', 'grad_ev_w_in': 'grad_w', 'grad_ev_lru_conv_w': 'grad_w', 'grad_ev_lru_conv_b': 'grad_w', 'grad_ev_lru_gate_a_w': 'grad_w', 'grad_ev_lru_gate_a_b': 'grad_w', 'grad_ev_lru_gate_x_w': 'grad_w', 'grad_ev_lru_gate_x_b': 'grad_w', 'grad_ev_lru_lambda': 'grad_w', 'grad_ev_w_out': 'grad_w', 'grad_od_w_in': 'grad_w', 'grad_od_ssm_conv_w': 'grad_w', 'grad_od_ssm_conv_b': 'grad_w', 'grad_od_ssm_dt_bias': 'grad_w', 'grad_od_ssm_a_log': 'grad_w', 'grad_od_ssm_d': 'grad_w', 'grad_od_ssm_norm': 'grad_w', 'grad_od_cm_conv_w': 'grad_w', 'grad_od_cm_conv_b': 'grad_w', 'grad_od_cm_ln_g': 'grad_w', 'grad_od_cm_ln_b': 'grad_w', 'grad_od_w_out': 'grad_w', 'grad_norm_mix_pre': 'grad_w', 'grad_norm_mix_post': 'grad_w', 'grad_norm_mlp_pre': 'grad_w', 'grad_norm_mlp_post': 'grad_w', 'grad_norm_ple': 'grad_w', 'grad_mlp_w1': 'grad_w', 'grad_mlp_w2': 'grad_w', 'grad_ple_w_proj': 'grad_w', 'grad_ple_w_gate': 'grad_w', 'delta_ev_w_in': 'delta_w', 'delta_ev_lru_conv_w': 'delta_w', 'delta_ev_lru_conv_b': 'delta_w', 'delta_ev_lru_gate_a_w': 'delta_w', 'delta_ev_lru_gate_a_b': 'delta_w', 'delta_ev_lru_gate_x_w': 'delta_w', 'delta_ev_lru_gate_x_b': 'delta_w', 'delta_ev_lru_lambda': 'delta_w', 'delta_ev_w_out': 'delta_w', 'delta_od_w_in': 'delta_w', 'delta_od_ssm_conv_w': 'delta_w', 'delta_od_ssm_conv_b': 'delta_w', 'delta_od_ssm_dt_bias': 'delta_w', 'delta_od_ssm_a_log': 'delta_w', 'delta_od_ssm_d': 'delta_w', 'delta_od_ssm_norm': 'delta_w', 'delta_od_cm_conv_w': 'delta_w', 'delta_od_cm_conv_b': 'delta_w', 'delta_od_cm_ln_g': 'delta_w', 'delta_od_cm_ln_b': 'delta_w', 'delta_od_w_out': 'delta_w', 'delta_norm_mix_pre': 'delta_w', 'delta_norm_mix_post': 'delta_w', 'delta_norm_mlp_pre': 'delta_w', 'delta_norm_mlp_post': 'delta_w', 'delta_norm_ple': 'delta_w', 'delta_mlp_w1': 'delta_w', 'delta_mlp_w2': 'delta_w', 'delta_ple_w_proj': 'delta_w', 'delta_ple_w_gate': 'delta_w', 'new_m_ev_w_in': 'new_m', 'new_m_ev_lru_conv_w': 'new_m', 'new_m_ev_lru_conv_b': 'new_m', 'new_m_ev_lru_gate_a_w': 'new_m', 'new_m_ev_lru_gate_a_b': 'new_m', 'new_m_ev_lru_gate_x_w': 'new_m', 'new_m_ev_lru_gate_x_b': 'new_m', 'new_m_ev_lru_lambda': 'new_m', 'new_m_ev_w_out': 'new_m', 'new_m_od_w_in': 'new_m', 'new_m_od_ssm_conv_w': 'new_m', 'new_m_od_ssm_conv_b': 'new_m', 'new_m_od_ssm_dt_bias': 'new_m', 'new_m_od_ssm_a_log': 'new_m', 'new_m_od_ssm_d': 'new_m', 'new_m_od_ssm_norm': 'new_m', 'new_m_od_cm_conv_w': 'new_m', 'new_m_od_cm_conv_b': 'new_m', 'new_m_od_cm_ln_g': 'new_m', 'new_m_od_cm_ln_b': 'new_m', 'new_m_od_w_out': 'new_m', 'new_m_norm_mix_pre': 'new_m', 'new_m_norm_mix_post': 'new_m', 'new_m_norm_mlp_pre': 'new_m', 'new_m_norm_mlp_post': 'new_m', 'new_m_norm_ple': 'new_m', 'new_m_mlp_w1': 'new_m', 'new_m_mlp_w2': 'new_m', 'new_m_ple_w_proj': 'new_m', 'new_m_ple_w_gate': 'new_m', 'new_v_ev_w_in': 'new_v', 'new_v_ev_lru_conv_w': 'new_v', 'new_v_ev_lru_conv_b': 'new_v', 'new_v_ev_lru_gate_a_w': 'new_v', 'new_v_ev_lru_gate_a_b': 'new_v', 'new_v_ev_lru_gate_x_w': 'new_v', 'new_v_ev_lru_gate_x_b': 'new_v', 'new_v_ev_lru_lambda': 'new_v', 'new_v_ev_w_out': 'new_v', 'new_v_od_w_in': 'new_v', 'new_v_od_ssm_conv_w': 'new_v', 'new_v_od_ssm_conv_b': 'new_v', 'new_v_od_ssm_dt_bias': 'new_v', 'new_v_od_ssm_a_log': 'new_v', 'new_v_od_ssm_d': 'new_v', 'new_v_od_ssm_norm': 'new_v', 'new_v_od_cm_conv_w': 'new_v', 'new_v_od_cm_conv_b': 'new_v', 'new_v_od_cm_ln_g': 'new_v', 'new_v_od_cm_ln_b': 'new_v', 'new_v_od_w_out': 'new_v', 'new_v_norm_mix_pre': 'new_v', 'new_v_norm_mix_post': 'new_v', 'new_v_norm_mlp_pre': 'new_v', 'new_v_norm_mlp_post': 'new_v', 'new_v_norm_ple': 'new_v', 'new_v_mlp_w1': 'new_v', 'new_v_mlp_w2': 'new_v', 'new_v_ple_w_proj': 'new_v', 'new_v_ple_w_gate': 'new_v'}


def _forward(args):
    return _fwd_reference(*[args[k] for k in FWD_PARAMS])


def _output_shape():
    out = _jax.eval_shape(lambda: _forward(_fwd_setup_inputs(0)))
    return out.shape, out.dtype

N_MICROBATCH = 1
ADAM_LR = 0.001
ADAM_B1 = 0.9
ADAM_B2 = 0.999
ADAM_EPS = 1e-08
ADAM_WD = 0.01
ADAM_STEP = 10
PER_EXAMPLE_BATCH_AXIS = {'x': 0, 'p': 1, 'loss_target': 0}
SHARED_INPUTS = []
_WEIGHT_DTYPES = {'ev_w_in': _jnp.float32, 'ev_lru_conv_w': _jnp.float32, 'ev_lru_conv_b': _jnp.float32, 'ev_lru_gate_a_w': _jnp.float32, 'ev_lru_gate_a_b': _jnp.float32, 'ev_lru_gate_x_w': _jnp.float32, 'ev_lru_gate_x_b': _jnp.float32, 'ev_lru_lambda': _jnp.float32, 'ev_w_out': _jnp.float32, 'od_w_in': _jnp.float32, 'od_ssm_conv_w': _jnp.float32, 'od_ssm_conv_b': _jnp.float32, 'od_ssm_dt_bias': _jnp.float32, 'od_ssm_a_log': _jnp.float32, 'od_ssm_d': _jnp.float32, 'od_ssm_norm': _jnp.float32, 'od_cm_conv_w': _jnp.float32, 'od_cm_conv_b': _jnp.float32, 'od_cm_ln_g': _jnp.float32, 'od_cm_ln_b': _jnp.float32, 'od_w_out': _jnp.float32, 'norm_mix_pre': _jnp.float32, 'norm_mix_post': _jnp.float32, 'norm_mlp_pre': _jnp.float32, 'norm_mlp_post': _jnp.float32, 'norm_ple': _jnp.float32, 'mlp_w1': _jnp.float32, 'mlp_w2': _jnp.float32, 'ple_w_proj': _jnp.float32, 'ple_w_gate': _jnp.float32}
MOMENT_SCALE = {'ev_w_in': 1.134611e+00, 'ev_lru_conv_w': 1.801020e+00, 'ev_lru_conv_b': 5.969275e+01, 'ev_lru_gate_a_w': 1.245861e+00, 'ev_lru_gate_a_b': 8.846330e-01, 'ev_lru_gate_x_w': 2.577480e+00, 'ev_lru_gate_x_b': 6.968174e-01, 'ev_lru_lambda': 1.069010e+00, 'ev_w_out': 2.251694e+00, 'od_w_in': 1.876172e+00, 'od_ssm_conv_w': 3.513553e+00, 'od_ssm_conv_b': 1.209981e+01, 'od_ssm_dt_bias': 3.656920e+00, 'od_ssm_a_log': 2.162395e+01, 'od_ssm_d': 2.620941e+01, 'od_ssm_norm': 6.194175e+00, 'od_cm_conv_w': 3.394857e+00, 'od_cm_conv_b': 3.113759e+01, 'od_cm_ln_g': 1.251181e+01, 'od_cm_ln_b': 1.851917e+01, 'od_w_out': 8.480421e+00, 'norm_mix_pre': 2.380719e+00, 'norm_mix_post': 6.414138e+01, 'norm_mlp_pre': 2.953488e+00, 'norm_mlp_post': 6.764572e+01, 'norm_ple': 6.508870e+01, 'mlp_w1': 1.372794e+00, 'mlp_w2': 1.266579e+01, 'ple_w_proj': 9.762421e-01, 'ple_w_gate': 5.581966e-01}


def _to_microbatches(a, axis):
    t = _jnp.moveaxis(a, axis, 0)
    t = t.reshape((N_MICROBATCH, t.shape[0] // N_MICROBATCH) + t.shape[1:])
    return _jnp.moveaxis(t, 1, axis + 1)


def setup_inputs(seed: int = 0) -> dict:
    inp = _fwd_setup_inputs(seed)
    key = _jax.random.fold_in(_jax.random.key(seed), 7919)
    shape, _ = _output_shape()
    out = dict(inp)
    out["loss_target"] = _jax.random.normal(_jax.random.fold_in(key, 0), shape, _jnp.float32)
    for i, name in enumerate(TWIN_WEIGHTS):
        w = inp[name].astype(_jnp.float32)
        if MOMENT_SCALE is None:
            s = _jnp.sqrt(_jnp.mean(_jnp.square(w)) + 1e-30)
        else:
            s = MOMENT_SCALE[name]
        km, kv = _jax.random.split(_jax.random.fold_in(key, i + 1))
        out[name] = w
        out["m_" + name] = s * _jax.random.normal(km, w.shape, _jnp.float32)
        out["v_" + name] = (s * s) * _jax.random.uniform(kv, w.shape, _jnp.float32, 0.5, 1.5)
    if N_MICROBATCH > 1:
        for name, axis in PER_EXAMPLE_BATCH_AXIS.items():
            out[name] = _to_microbatches(out[name], axis)
    return {'x': out['x'], 'p': out['p'], 'ev_w_in': out['ev_w_in'], 'ev_lru_conv_w': out['ev_lru_conv_w'], 'ev_lru_conv_b': out['ev_lru_conv_b'], 'ev_lru_gate_a_w': out['ev_lru_gate_a_w'], 'ev_lru_gate_a_b': out['ev_lru_gate_a_b'], 'ev_lru_gate_x_w': out['ev_lru_gate_x_w'], 'ev_lru_gate_x_b': out['ev_lru_gate_x_b'], 'ev_lru_lambda': out['ev_lru_lambda'], 'ev_w_out': out['ev_w_out'], 'od_w_in': out['od_w_in'], 'od_ssm_conv_w': out['od_ssm_conv_w'], 'od_ssm_conv_b': out['od_ssm_conv_b'], 'od_ssm_dt_bias': out['od_ssm_dt_bias'], 'od_ssm_a_log': out['od_ssm_a_log'], 'od_ssm_d': out['od_ssm_d'], 'od_ssm_norm': out['od_ssm_norm'], 'od_cm_conv_w': out['od_cm_conv_w'], 'od_cm_conv_b': out['od_cm_conv_b'], 'od_cm_ln_g': out['od_cm_ln_g'], 'od_cm_ln_b': out['od_cm_ln_b'], 'od_w_out': out['od_w_out'], 'norm_mix_pre': out['norm_mix_pre'], 'norm_mix_post': out['norm_mix_post'], 'norm_mlp_pre': out['norm_mlp_pre'], 'norm_mlp_post': out['norm_mlp_post'], 'norm_ple': out['norm_ple'], 'mlp_w1': out['mlp_w1'], 'mlp_w2': out['mlp_w2'], 'ple_w_proj': out['ple_w_proj'], 'ple_w_gate': out['ple_w_gate'], 'loss_target': out['loss_target'], 'm_ev_w_in': out['m_ev_w_in'], 'm_ev_lru_conv_w': out['m_ev_lru_conv_w'], 'm_ev_lru_conv_b': out['m_ev_lru_conv_b'], 'm_ev_lru_gate_a_w': out['m_ev_lru_gate_a_w'], 'm_ev_lru_gate_a_b': out['m_ev_lru_gate_a_b'], 'm_ev_lru_gate_x_w': out['m_ev_lru_gate_x_w'], 'm_ev_lru_gate_x_b': out['m_ev_lru_gate_x_b'], 'm_ev_lru_lambda': out['m_ev_lru_lambda'], 'm_ev_w_out': out['m_ev_w_out'], 'm_od_w_in': out['m_od_w_in'], 'm_od_ssm_conv_w': out['m_od_ssm_conv_w'], 'm_od_ssm_conv_b': out['m_od_ssm_conv_b'], 'm_od_ssm_dt_bias': out['m_od_ssm_dt_bias'], 'm_od_ssm_a_log': out['m_od_ssm_a_log'], 'm_od_ssm_d': out['m_od_ssm_d'], 'm_od_ssm_norm': out['m_od_ssm_norm'], 'm_od_cm_conv_w': out['m_od_cm_conv_w'], 'm_od_cm_conv_b': out['m_od_cm_conv_b'], 'm_od_cm_ln_g': out['m_od_cm_ln_g'], 'm_od_cm_ln_b': out['m_od_cm_ln_b'], 'm_od_w_out': out['m_od_w_out'], 'm_norm_mix_pre': out['m_norm_mix_pre'], 'm_norm_mix_post': out['m_norm_mix_post'], 'm_norm_mlp_pre': out['m_norm_mlp_pre'], 'm_norm_mlp_post': out['m_norm_mlp_post'], 'm_norm_ple': out['m_norm_ple'], 'm_mlp_w1': out['m_mlp_w1'], 'm_mlp_w2': out['m_mlp_w2'], 'm_ple_w_proj': out['m_ple_w_proj'], 'm_ple_w_gate': out['m_ple_w_gate'], 'v_ev_w_in': out['v_ev_w_in'], 'v_ev_lru_conv_w': out['v_ev_lru_conv_w'], 'v_ev_lru_conv_b': out['v_ev_lru_conv_b'], 'v_ev_lru_gate_a_w': out['v_ev_lru_gate_a_w'], 'v_ev_lru_gate_a_b': out['v_ev_lru_gate_a_b'], 'v_ev_lru_gate_x_w': out['v_ev_lru_gate_x_w'], 'v_ev_lru_gate_x_b': out['v_ev_lru_gate_x_b'], 'v_ev_lru_lambda': out['v_ev_lru_lambda'], 'v_ev_w_out': out['v_ev_w_out'], 'v_od_w_in': out['v_od_w_in'], 'v_od_ssm_conv_w': out['v_od_ssm_conv_w'], 'v_od_ssm_conv_b': out['v_od_ssm_conv_b'], 'v_od_ssm_dt_bias': out['v_od_ssm_dt_bias'], 'v_od_ssm_a_log': out['v_od_ssm_a_log'], 'v_od_ssm_d': out['v_od_ssm_d'], 'v_od_ssm_norm': out['v_od_ssm_norm'], 'v_od_cm_conv_w': out['v_od_cm_conv_w'], 'v_od_cm_conv_b': out['v_od_cm_conv_b'], 'v_od_cm_ln_g': out['v_od_cm_ln_g'], 'v_od_cm_ln_b': out['v_od_cm_ln_b'], 'v_od_w_out': out['v_od_w_out'], 'v_norm_mix_pre': out['v_norm_mix_pre'], 'v_norm_mix_post': out['v_norm_mix_post'], 'v_norm_mlp_pre': out['v_norm_mlp_pre'], 'v_norm_mlp_post': out['v_norm_mlp_post'], 'v_norm_ple': out['v_norm_ple'], 'v_mlp_w1': out['v_mlp_w1'], 'v_mlp_w2': out['v_mlp_w2'], 'v_ple_w_proj': out['v_ple_w_proj'], 'v_ple_w_gate': out['v_ple_w_gate']}


def _loss(weights, diff, rest, loss_target):
    with _jax.named_scope("forward"):
        args = {**rest, TWIN_DIFF_INPUT: diff, **{k: w.astype(_WEIGHT_DTYPES[k]) for k, w in weights.items()}}
        y = _forward(args)
    with _jax.named_scope("loss_head"):
        err = _jnp.square(y.astype(_jnp.float32) - loss_target)
        return 0.5 * _jnp.sum(_jnp.mean(err, axis=-1)) if err.ndim else 0.5 * err


def _adamw(w, g, m, v):
    m = ADAM_B1 * m + (1.0 - ADAM_B1) * g
    v = ADAM_B2 * v + (1.0 - ADAM_B2) * _jnp.square(g)
    m_hat = m / (1.0 - ADAM_B1 ** ADAM_STEP)
    v_hat = v / (1.0 - ADAM_B2 ** ADAM_STEP)
    delta = -ADAM_LR * (m_hat / (_jnp.sqrt(v_hat) + ADAM_EPS) + ADAM_WD * w)
    return delta, m, v


def reference(x, p, ev_w_in, ev_lru_conv_w, ev_lru_conv_b, ev_lru_gate_a_w, ev_lru_gate_a_b, ev_lru_gate_x_w, ev_lru_gate_x_b, ev_lru_lambda, ev_w_out, od_w_in, od_ssm_conv_w, od_ssm_conv_b, od_ssm_dt_bias, od_ssm_a_log, od_ssm_d, od_ssm_norm, od_cm_conv_w, od_cm_conv_b, od_cm_ln_g, od_cm_ln_b, od_w_out, norm_mix_pre, norm_mix_post, norm_mlp_pre, norm_mlp_post, norm_ple, mlp_w1, mlp_w2, ple_w_proj, ple_w_gate, loss_target, m_ev_w_in, m_ev_lru_conv_w, m_ev_lru_conv_b, m_ev_lru_gate_a_w, m_ev_lru_gate_a_b, m_ev_lru_gate_x_w, m_ev_lru_gate_x_b, m_ev_lru_lambda, m_ev_w_out, m_od_w_in, m_od_ssm_conv_w, m_od_ssm_conv_b, m_od_ssm_dt_bias, m_od_ssm_a_log, m_od_ssm_d, m_od_ssm_norm, m_od_cm_conv_w, m_od_cm_conv_b, m_od_cm_ln_g, m_od_cm_ln_b, m_od_w_out, m_norm_mix_pre, m_norm_mix_post, m_norm_mlp_pre, m_norm_mlp_post, m_norm_ple, m_mlp_w1, m_mlp_w2, m_ple_w_proj, m_ple_w_gate, v_ev_w_in, v_ev_lru_conv_w, v_ev_lru_conv_b, v_ev_lru_gate_a_w, v_ev_lru_gate_a_b, v_ev_lru_gate_x_w, v_ev_lru_gate_x_b, v_ev_lru_lambda, v_ev_w_out, v_od_w_in, v_od_ssm_conv_w, v_od_ssm_conv_b, v_od_ssm_dt_bias, v_od_ssm_a_log, v_od_ssm_d, v_od_ssm_norm, v_od_cm_conv_w, v_od_cm_conv_b, v_od_cm_ln_g, v_od_cm_ln_b, v_od_w_out, v_norm_mix_pre, v_norm_mix_post, v_norm_mlp_pre, v_norm_mlp_post, v_norm_ple, v_mlp_w1, v_mlp_w2, v_ple_w_proj, v_ple_w_gate):
    given = dict(x=x, p=p, ev_w_in=ev_w_in, ev_lru_conv_w=ev_lru_conv_w, ev_lru_conv_b=ev_lru_conv_b, ev_lru_gate_a_w=ev_lru_gate_a_w, ev_lru_gate_a_b=ev_lru_gate_a_b, ev_lru_gate_x_w=ev_lru_gate_x_w, ev_lru_gate_x_b=ev_lru_gate_x_b, ev_lru_lambda=ev_lru_lambda, ev_w_out=ev_w_out, od_w_in=od_w_in, od_ssm_conv_w=od_ssm_conv_w, od_ssm_conv_b=od_ssm_conv_b, od_ssm_dt_bias=od_ssm_dt_bias, od_ssm_a_log=od_ssm_a_log, od_ssm_d=od_ssm_d, od_ssm_norm=od_ssm_norm, od_cm_conv_w=od_cm_conv_w, od_cm_conv_b=od_cm_conv_b, od_cm_ln_g=od_cm_ln_g, od_cm_ln_b=od_cm_ln_b, od_w_out=od_w_out, norm_mix_pre=norm_mix_pre, norm_mix_post=norm_mix_post, norm_mlp_pre=norm_mlp_pre, norm_mlp_post=norm_mlp_post, norm_ple=norm_ple, mlp_w1=mlp_w1, mlp_w2=mlp_w2, ple_w_proj=ple_w_proj, ple_w_gate=ple_w_gate, loss_target=loss_target, m_ev_w_in=m_ev_w_in, m_ev_lru_conv_w=m_ev_lru_conv_w, m_ev_lru_conv_b=m_ev_lru_conv_b, m_ev_lru_gate_a_w=m_ev_lru_gate_a_w, m_ev_lru_gate_a_b=m_ev_lru_gate_a_b, m_ev_lru_gate_x_w=m_ev_lru_gate_x_w, m_ev_lru_gate_x_b=m_ev_lru_gate_x_b, m_ev_lru_lambda=m_ev_lru_lambda, m_ev_w_out=m_ev_w_out, m_od_w_in=m_od_w_in, m_od_ssm_conv_w=m_od_ssm_conv_w, m_od_ssm_conv_b=m_od_ssm_conv_b, m_od_ssm_dt_bias=m_od_ssm_dt_bias, m_od_ssm_a_log=m_od_ssm_a_log, m_od_ssm_d=m_od_ssm_d, m_od_ssm_norm=m_od_ssm_norm, m_od_cm_conv_w=m_od_cm_conv_w, m_od_cm_conv_b=m_od_cm_conv_b, m_od_cm_ln_g=m_od_cm_ln_g, m_od_cm_ln_b=m_od_cm_ln_b, m_od_w_out=m_od_w_out, m_norm_mix_pre=m_norm_mix_pre, m_norm_mix_post=m_norm_mix_post, m_norm_mlp_pre=m_norm_mlp_pre, m_norm_mlp_post=m_norm_mlp_post, m_norm_ple=m_norm_ple, m_mlp_w1=m_mlp_w1, m_mlp_w2=m_mlp_w2, m_ple_w_proj=m_ple_w_proj, m_ple_w_gate=m_ple_w_gate, v_ev_w_in=v_ev_w_in, v_ev_lru_conv_w=v_ev_lru_conv_w, v_ev_lru_conv_b=v_ev_lru_conv_b, v_ev_lru_gate_a_w=v_ev_lru_gate_a_w, v_ev_lru_gate_a_b=v_ev_lru_gate_a_b, v_ev_lru_gate_x_w=v_ev_lru_gate_x_w, v_ev_lru_gate_x_b=v_ev_lru_gate_x_b, v_ev_lru_lambda=v_ev_lru_lambda, v_ev_w_out=v_ev_w_out, v_od_w_in=v_od_w_in, v_od_ssm_conv_w=v_od_ssm_conv_w, v_od_ssm_conv_b=v_od_ssm_conv_b, v_od_ssm_dt_bias=v_od_ssm_dt_bias, v_od_ssm_a_log=v_od_ssm_a_log, v_od_ssm_d=v_od_ssm_d, v_od_ssm_norm=v_od_ssm_norm, v_od_cm_conv_w=v_od_cm_conv_w, v_od_cm_conv_b=v_od_cm_conv_b, v_od_cm_ln_g=v_od_cm_ln_g, v_od_cm_ln_b=v_od_cm_ln_b, v_od_w_out=v_od_w_out, v_norm_mix_pre=v_norm_mix_pre, v_norm_mix_post=v_norm_mix_post, v_norm_mlp_pre=v_norm_mlp_pre, v_norm_mlp_post=v_norm_mlp_post, v_norm_ple=v_norm_ple, v_mlp_w1=v_mlp_w1, v_mlp_w2=v_mlp_w2, v_ple_w_proj=v_ple_w_proj, v_ple_w_gate=v_ple_w_gate)
    weights = {n: given[n] for n in TWIN_WEIGHTS}
    shared = {n: given[n] for n in SHARED_INPUTS}
    per_example = {n: given[n] for n in ['x', 'p']}
    grad_fn = _jax.value_and_grad(_loss, argnums=(0, 1))

    def one_microbatch(ex, loss_target):
        ex = dict(ex)
        diff = ex.pop(TWIN_DIFF_INPUT)
        return grad_fn(weights, diff, {**shared, **ex}, loss_target)

    if N_MICROBATCH == 1:
        loss, (grad_w, grad_x) = one_microbatch(per_example, given["loss_target"])
    else:
        def body(carry, xs):
            loss_sum, grad_sum = carry
            l_k, (gw_k, gx_k) = one_microbatch(xs[0], xs[1])
            with _jax.named_scope("update"):
                return (loss_sum + l_k, _jax.tree.map(_jnp.add, grad_sum, gw_k)), gx_k

        init = (_jnp.zeros((), _jnp.float32), _jax.tree.map(_jnp.zeros_like, weights))
        (loss, grad_w), grad_x = _jax.lax.scan(body, init, (per_example, given["loss_target"]))
    with _jax.named_scope("update"):
        delta_w, new_m, new_v = {}, {}, {}
        for n in TWIN_WEIGHTS:
            delta_w[n], new_m[n], new_v[n] = _adamw(weights[n], grad_w[n], given["m_" + n], given["v_" + n])
    return (loss, grad_x, *[grad_w[n] for n in TWIN_WEIGHTS], *[delta_w[n] for n in TWIN_WEIGHTS],
            *[new_m[n] for n in TWIN_WEIGHTS], *[new_v[n] for n in TWIN_WEIGHTS])
```

```python
import functools
import math

import jax
import jax.numpy as jnp
from jax import lax
from jax.experimental import pallas as pl
from jax.experimental.pallas import tpu as pltpu

f32 = jnp.float32
bf16 = jnp.bfloat16
MXU_DTYPE = jnp.bfloat16

D_MODEL = 1024
EPS = 1e-6
LRU_WIDTH = 512
LRU_HEADS = 8
LRU_CONV = 4
LRU_C = 8.0
SB_WIDTH = 512
SB_HEAD_DIM = 64
SSM_WIDTH = 1024
SSM_HEADS = 16
SSM_HEAD_DIM = 64
SSM_GROUPS = 2
SSM_STATE = 128
SSM_CONV = 4
SSM_CHUNK = 128
SSM_XBC = SSM_WIDTH + 2 * SSM_GROUPS * SSM_STATE
CONF_WIDTH = 512
CONF_KERNEL = 31
MLP_HIDDEN = 4096
PLE_DIM = 256
LANES = 128
N_DEV = 8

ADAM_LR = 0.001
ADAM_B1 = 0.9
ADAM_B2 = 0.999
ADAM_EPS = 1e-08
ADAM_WD = 0.01
ADAM_STEP = 10

VMEM_LIMIT = 56 * 1024 * 1024


def _cparams(*sem):
    return pltpu.CompilerParams(dimension_semantics=sem, vmem_limit_bytes=VMEM_LIMIT)


def _mm(a, b):
    return jnp.dot(a.astype(MXU_DTYPE), b.astype(MXU_DTYPE), preferred_element_type=f32)


def _mm_nt(a, b):
    return lax.dot_general(a.astype(MXU_DTYPE), b.astype(MXU_DTYPE), (((1,), (1,)), ((), ())),
                           preferred_element_type=f32)


def _mm_tn(a, b):
    return lax.dot_general(a.astype(MXU_DTYPE), b.astype(MXU_DTYPE), (((0,), (0,)), ((), ())),
                           preferred_element_type=f32)


def _mm_exact(a, b):
    return jnp.dot(a, b, preferred_element_type=f32, precision=lax.Precision.HIGHEST)


@jax.custom_vjp
def dmm(a, b):
    return _mm(a, b)


def _dmm_fwd(a, b):
    return _mm(a, b), (a, b)


def _dmm_bwd(res, g):
    a, b = res
    return _mm_nt(g, b), _mm_tn(a, g)


dmm.defvjp(_dmm_fwd, _dmm_bwd)


@jax.custom_vjp
def dmm_nt(a, b):
    return _mm_nt(a, b)


def _dmm_nt_fwd(a, b):
    return _mm_nt(a, b), (a, b)


def _dmm_nt_bwd(res, g):
    a, b = res
    return _mm(g, b), _mm_tn(g, a)


dmm_nt.defvjp(_dmm_nt_fwd, _dmm_nt_bwd)


@jax.custom_vjp
def dmm_tn(a, b):
    return _mm_tn(a, b)


def _dmm_tn_fwd(a, b):
    return _mm_tn(a, b), (a, b)


def _dmm_tn_bwd(res, g):
    a, b = res
    return _mm_nt(b, g), _mm(a, g)


dmm_tn.defvjp(_dmm_tn_fwd, _dmm_tn_bwd)


def _rms(x, g):
    r = lax.rsqrt(jnp.mean(x * x, axis=-1, keepdims=True) + EPS)
    return x * r * g


def _rms_bwd(dy, x, g):
    r = lax.rsqrt(jnp.mean(x * x, axis=-1, keepdims=True) + EPS)
    dyg = dy * g
    dx = r * dyg - x * (r * r * r * jnp.mean(dyg * x, axis=-1, keepdims=True))
    return dx, dy * x * r


def _tok(tm, n):
    return pl.BlockSpec((tm, n), lambda i: (i, 0))


def _whole(shape):
    nd = len(shape)
    return pl.BlockSpec(tuple(shape), lambda i: (0,) * nd)


def _acc_rows(ref, val):
    s = jnp.sum(val, axis=0, keepdims=True)

    @pl.when(pl.program_id(0) == 0)
    def _():
        ref[...] = s

    @pl.when(pl.program_id(0) != 0)
    def _():
        ref[...] += s


TOKEN_TILE = 256


def norm_matmul(h, g, ws, out_dtypes, name):
    t, d = h.shape
    tm = TOKEN_TILE
    nw = len(ws)

    def body(h_ref, g_ref, *refs):
        hn = _rms(h_ref[...], g_ref[...]).astype(MXU_DTYPE)
        for w_ref, o_ref in zip(refs[:nw], refs[nw:]):
            o_ref[...] = jnp.dot(hn, w_ref[...], preferred_element_type=f32).astype(o_ref.dtype)

    return pl.pallas_call(
        body, name=name, grid=(t // tm,),
        in_specs=[_tok(tm, d), _whole(g.shape)] + [_whole(w.shape) for w in ws],
        out_specs=[_tok(tm, w.shape[1]) for w in ws],
        out_shape=[jax.ShapeDtypeStruct((t, w.shape[1]), dt) for w, dt in zip(ws, out_dtypes)],
        compiler_params=_cparams("parallel"),
    )(h, g, *ws)


def matmul_residual_norm(xs, ws, h, g, name, relu2=False):
    t, d = h.shape
    tm = TOKEN_TILE
    nx = len(xs)

    def body(*refs):
        x_refs, w_refs = refs[:nx], refs[nx:2 * nx]
        h_ref, g_ref, ho_ref, m_ref = refs[2 * nx:]
        m = None
        for x_ref, w_ref in zip(x_refs, w_refs):
            x = x_ref[...]
            if relu2:
                x = jnp.square(jnp.maximum(x.astype(f32), 0.0))
            part = jnp.dot(x.astype(MXU_DTYPE), w_ref[...], preferred_element_type=f32)
            m = part if m is None else m + part
        m_ref[...] = m
        ho_ref[...] = h_ref[...] + _rms(m, g_ref[...])

    return pl.pallas_call(
        body, name=name, grid=(t // tm,),
        in_specs=[_tok(tm, x.shape[1]) for x in xs] + [_whole(w.shape) for w in ws] + [_tok(tm, d), _whole(g.shape)],
        out_specs=[_tok(tm, d), _tok(tm, d)],
        out_shape=[jax.ShapeDtypeStruct((t, d), f32), jax.ShapeDtypeStruct((t, d), f32)],
        compiler_params=_cparams("parallel"),
    )(*xs, *ws, h, g)


def ple_forward(h, p, w_gate, w_proj, g, name):
    t, d = h.shape
    tm = TOKEN_TILE

    def body(h_ref, p_ref, wg_ref, wp_ref, g_ref, ho_ref, gl_ref, emb_ref):
        hh = h_ref[...]
        gl = jnp.dot(hh.astype(MXU_DTYPE), wg_ref[...], preferred_element_type=f32)
        emb = jnp.dot(p_ref[...].astype(MXU_DTYPE), wp_ref[...], preferred_element_type=f32)
        gl_ref[...] = gl
        emb_ref[...] = emb
        ho_ref[...] = hh + _rms(jax.nn.sigmoid(gl) * emb, g_ref[...])

    return pl.pallas_call(
        body, name=name, grid=(t // tm,),
        in_specs=[_tok(tm, d), _tok(tm, p.shape[1]), _whole(w_gate.shape), _whole(w_proj.shape), _whole(g.shape)],
        out_specs=[_tok(tm, d)] * 3,
        out_shape=[jax.ShapeDtypeStruct((t, d), f32)] * 3,
        compiler_params=_cparams("parallel"),
    )(h, p, w_gate, w_proj, g)


def loss_and_grad(h, target, name):
    t, d = h.shape
    tm = TOKEN_TILE

    def body(h_ref, t_ref, l_ref, dh_ref):
        e = h_ref[...] - t_ref[...]
        dh_ref[...] = e * (1.0 / d)
        part = jnp.sum(jnp.sum(e * e, axis=1, keepdims=True), axis=0, keepdims=True) * (0.5 / d)
        _acc_rows(l_ref, jnp.broadcast_to(part, (1, LANES)))

    return pl.pallas_call(
        body, name=name, grid=(t // tm,),
        in_specs=[_tok(tm, d), _tok(tm, d)],
        out_specs=[_whole((1, LANES)), _tok(tm, d)],
        out_shape=[jax.ShapeDtypeStruct((1, LANES), f32), jax.ShapeDtypeStruct((t, d), f32)],
        compiler_params=_cparams("arbitrary"),
    )(h, target)


def bwd_through_norm_in(dh, gs, wts, h, g, name):
    t, d = h.shape
    tm = TOKEN_TILE
    ng = len(gs)

    def body(*refs):
        dh_ref = refs[0]
        g_refs, w_refs = refs[1:1 + ng], refs[1 + ng:1 + 2 * ng]
        h_ref, gain_ref, dho_ref, dg_ref = refs[1 + 2 * ng:]
        dhn = None
        for g_ref, w_ref in zip(g_refs, w_refs):
            part = jnp.dot(g_ref[...].astype(MXU_DTYPE), w_ref[...], preferred_element_type=f32)
            dhn = part if dhn is None else dhn + part
        dx, dgr = _rms_bwd(dhn, h_ref[...], gain_ref[...])
        dho_ref[...] = dh_ref[...] + dx
        _acc_rows(dg_ref, dgr)

    return pl.pallas_call(
        body, name=name, grid=(t // tm,),
        in_specs=[_tok(tm, d)] + [_tok(tm, x.shape[1]) for x in gs] + [_whole(w.shape) for w in wts]
        + [_tok(tm, d), _whole(g.shape)],
        out_specs=[_tok(tm, d), _whole((1, d))],
        out_shape=[jax.ShapeDtypeStruct((t, d), f32), jax.ShapeDtypeStruct((1, d), f32)],
        compiler_params=_cparams("arbitrary"),
    )(dh, *gs, *wts, h, g)


def bwd_through_norm_out(dh, n, g, wts, out_dtypes, name, relu2_of=None):
    t, d = n.shape
    tm = TOKEN_TILE
    nw = len(wts)
    has_a = relu2_of is not None

    def body(*refs):
        dh_ref, n_ref, gain_ref = refs[:3]
        w_refs = refs[3:3 + nw]
        rest = refs[3 + nw:]
        if has_a:
            a_ref, rest = rest[0], rest[1:]
        dn_ref, dx_refs, dg_ref = rest[0], rest[1:1 + nw], rest[1 + nw]
        dn, dgr = _rms_bwd(dh_ref[...], n_ref[...], gain_ref[...])
        dnb = dn.astype(MXU_DTYPE)
        dn_ref[...] = dnb.astype(dn_ref.dtype)
        for w_ref, dx_ref in zip(w_refs, dx_refs):
            dx = jnp.dot(dnb, w_ref[...], preferred_element_type=f32)
            if has_a:
                dx = dx * (2.0 * jnp.maximum(a_ref[...].astype(f32), 0.0))
            dx_ref[...] = dx.astype(dx_ref.dtype)
        _acc_rows(dg_ref, dgr)

    ins = [dh, n, g, *wts] + ([relu2_of] if has_a else [])
    in_specs = [_tok(tm, d), _tok(tm, d), _whole(g.shape)] + [_whole(w.shape) for w in wts]
    if has_a:
        in_specs.append(_tok(tm, relu2_of.shape[1]))
    outs = pl.pallas_call(
        body, name=name, grid=(t // tm,),
        in_specs=in_specs,
        out_specs=[_tok(tm, d)] + [_tok(tm, w.shape[1]) for w in wts] + [_whole((1, d))],
        out_shape=[jax.ShapeDtypeStruct((t, d), MXU_DTYPE)]
        + [jax.ShapeDtypeStruct((t, w.shape[1]), dt) for w, dt in zip(wts, out_dtypes)]
        + [jax.ShapeDtypeStruct((1, d), f32)],
        compiler_params=_cparams("arbitrary"),
    )(*ins)
    return outs[0], list(outs[1:1 + nw]), outs[1 + nw]


def ple_backward(dh3, h2, gl, emb, g, w_gate_t, name):
    t, d = h2.shape
    tm = TOKEN_TILE

    def body(dh_ref, gl_ref, emb_ref, gain_ref, wt_ref, dho_ref, dgl_ref, demb_ref, dg_ref):
        gate = jax.nn.sigmoid(gl_ref[...])
        emb = emb_ref[...]
        dge, dgr = _rms_bwd(dh_ref[...], gate * emb, gain_ref[...])
        demb_ref[...] = (dge * gate).astype(demb_ref.dtype)
        dgl = (dge * emb * gate * (1.0 - gate)).astype(MXU_DTYPE)
        dgl_ref[...] = dgl.astype(dgl_ref.dtype)
        dho_ref[...] = dh_ref[...] + jnp.dot(dgl, wt_ref[...], preferred_element_type=f32)
        _acc_rows(dg_ref, dgr)

    return pl.pallas_call(
        body, name=name, grid=(t // tm,),
        in_specs=[_tok(tm, d), _tok(tm, d), _tok(tm, d), _whole(g.shape), _whole(w_gate_t.shape)],
        out_specs=[_tok(tm, d), _tok(tm, d), _tok(tm, d), _whole((1, d))],
        out_shape=[jax.ShapeDtypeStruct((t, d), f32), jax.ShapeDtypeStruct((t, d), MXU_DTYPE),
                   jax.ShapeDtypeStruct((t, d), MXU_DTYPE), jax.ShapeDtypeStruct((1, d), f32)],
        compiler_params=_cparams("arbitrary"),
    )(dh3, gl, emb, g, w_gate_t)


def _largest_tile(n, cap):
    if n <= cap:
        return n
    return max(c for c in range(LANES, cap + 1, LANES) if n % c == 0)


def weight_grad(x, gout, name, prologue="none", gain=None):
    t, k = x.shape
    n = gout.shape[1]
    tt = 512
    tn = _largest_tile(n, 1024)
    tk = k if prologue == "rms" else _largest_tile(k, 1024)
    has_gain = prologue == "rms"

    def body(*refs):
        if has_gain:
            x_ref, gain_ref, g_ref, o_ref = refs
        else:
            x_ref, g_ref, o_ref = refs
        x = x_ref[...].astype(f32)
        if prologue == "relu2":
            x = jnp.square(jnp.maximum(x, 0.0))
        elif prologue == "rms":
            x = _rms(x, gain_ref[...])
        part = _mm_tn(x, g_ref[...])

        @pl.when(pl.program_id(2) == 0)
        def _():
            o_ref[...] = part

        @pl.when(pl.program_id(2) != 0)
        def _():
            o_ref[...] += part

    in_specs = [pl.BlockSpec((tt, tk), lambda i, j, s: (s, i))]
    ins = [x]
    if has_gain:
        in_specs.append(pl.BlockSpec(gain.shape, lambda i, j, s: (0, 0)))
        ins.append(gain)
    in_specs.append(pl.BlockSpec((tt, tn), lambda i, j, s: (s, j)))
    ins.append(gout)
    return pl.pallas_call(
        body, name=name, grid=(k // tk, n // tn, t // tt),
        in_specs=in_specs,
        out_specs=pl.BlockSpec((tk, tn), lambda i, j, s: (i, j)),
        out_shape=jax.ShapeDtypeStruct((k, n), f32),
        compiler_params=_cparams("parallel", "parallel", "arbitrary"),
    )(*ins)


SEQ_TILE = 256
HALO = 8


def _first_step():
    return jnp.logical_and(pl.program_id(0) == 0, pl.program_id(1) == 0)


def _accum(ref, val, first):
    @pl.when(first)
    def _():
        ref[...] = val

    @pl.when(jnp.logical_not(first))
    def _():
        ref[...] += val


def _softplus(x):
    return jnp.maximum(x, 0.0) + jnp.log1p(jnp.exp(-jnp.abs(x)))


def _neg_expm1(z):
    series = -z * (1.0 + z * (0.5 + z * (1.0 / 6.0 + z * (1.0 / 24.0 + z * (1.0 / 120.0)))))
    return jnp.where(z > -0.05, series, 1.0 - jnp.exp(z))


def _lru_gates(xc, ga, gab, gx, gxb, lam):
    r = jax.nn.sigmoid(dmm(xc, ga) + gab)
    i = jax.nn.sigmoid(dmm(xc, gx) + gxb)
    log_a = -LRU_C * r * _softplus(-lam)
    a = jnp.exp(log_a)
    u = jnp.sqrt(_neg_expm1(2.0 * log_a)) * (i * xc)
    return a, u


def _scan_down(a, u):
    n = a.shape[0]
    rows = lax.broadcasted_iota(jnp.int32, a.shape, 0)
    d = 1
    while d < n:
        keep = rows >= d
        a_s = jnp.where(keep, pltpu.roll(a, d, 0), 1.0)
        u_s = jnp.where(keep, pltpu.roll(u, d, 0), 0.0)
        u = a * u_s + u
        a = a * a_s
        d *= 2
    return a, u


def _scan_up(b, g):
    n = b.shape[0]
    rows = lax.broadcasted_iota(jnp.int32, b.shape, 0)
    d = 1
    while d < n:
        keep = rows < n - d
        b_s = jnp.where(keep, pltpu.roll(b, n - d, 0), 1.0)
        g_s = jnp.where(keep, pltpu.roll(g, n - d, 0), 0.0)
        g = g + b * g_s
        b = b * b_s
        d *= 2
    return g


def _seq_specs(ts, c, nt, reverse=False):
    per = ts // HALO

    def jj(j):
        return (nt - 1 - j) if reverse else j

    tile = pl.BlockSpec((1, ts, c), lambda b, j: (b, jj(j), 0))
    before = pl.BlockSpec((1, HALO, c), lambda b, j: (b, jnp.maximum(jj(j) * per - 1, 0), 0))
    after = pl.BlockSpec((1, HALO, c), lambda b, j: (b, jnp.minimum((jj(j) + 1) * per, nt * per - 1), 0))
    return tile, before, after


def _const2(shape):
    nd = len(shape)
    return pl.BlockSpec(tuple(shape), lambda b, j: (0,) * nd)


def lru_forward(xpre, gate, cw, cb, ga, gab, gx, gxb, lam, name):
    nb, ns, w = xpre.shape
    ts = SEQ_TILE
    nt = ns // ts
    tile, _, _ = _seq_specs(ts, w, nt)

    def body(xp_ref, gt_ref, cw_ref, cb_ref, ga_ref, gab_ref, gx_ref, gxb_ref, lam_ref,
             y_ref, xc_ref, hs_ref, xin, hcar):
        @pl.when(pl.program_id(1) == 0)
        def _():
            xin[0:HALO, :] = jnp.zeros((HALO, w), f32)
            hcar[...] = jnp.zeros_like(hcar)

        xin[HALO:HALO + ts, :] = xp_ref[0]
        xc = jnp.broadcast_to(cb_ref[...], (ts, w))
        for k in range(LRU_CONV):
            xc = xc + cw_ref[k:k + 1, :] * xin[pl.ds(HALO - LRU_CONV + 1 + k, ts), :]
        xin[0:HALO, :] = xin[ts:ts + HALO, :]
        a, u = _lru_gates(xc, ga_ref[...], gab_ref[...], gx_ref[...], gxb_ref[...], lam_ref[...])
        acum, h = _scan_down(a, u)
        h = h + acum * hcar[0:1, :]
        hcar[0:1, :] = h[ts - 1:ts, :]
        xc_ref[0] = xc
        hs_ref[0] = h
        y_ref[0] = (h * jax.nn.gelu(gt_ref[0])).astype(y_ref.dtype)

    params = [cw, cb, ga, gab, gx, gxb, lam]
    return pl.pallas_call(
        body, name=name, grid=(nb, nt),
        in_specs=[tile, tile] + [_const2(p.shape) for p in params],
        out_specs=[tile, tile, tile],
        out_shape=[jax.ShapeDtypeStruct((nb, ns, w), MXU_DTYPE), jax.ShapeDtypeStruct((nb, ns, w), f32),
                   jax.ShapeDtypeStruct((nb, ns, w), f32)],
        scratch_shapes=[pltpu.VMEM((ts + HALO, w), f32), pltpu.VMEM((HALO, w), f32)],
        compiler_params=_cparams("arbitrary", "arbitrary"),
    )(xpre, gate, *params)


def lru_backward(dy, xpre, gate, xc, hs, cw, cb, ga, gab, gx, gxb, lam, name):
    nb, ns, w = xpre.shape
    ts = SEQ_TILE
    nt = ns // ts
    tile, before, _ = _seq_specs(ts, w, nt, reverse=True)

    def body(dy_ref, xp_ref, xpb_ref, gt_ref, xc_ref, hs_ref, hsb_ref,
             cw_ref, cb_ref, ga_ref, gab_ref, gx_ref, gxb_ref, lam_ref,
             dxp_ref, dgt_ref, dcw_ref, dcb_ref, dga_ref, dgab_ref, dgx_ref, dgxb_ref, dlam_ref,
             dxc_ext, gcar, xin):
        j = pl.program_id(1)
        first = _first_step()
        at_seq_start = j == nt - 1

        @pl.when(j == 0)
        def _():
            dxc_ext[ts:ts + HALO, :] = jnp.zeros((HALO, w), f32)
            gcar[...] = jnp.zeros_like(gcar)

        gt = gt_ref[0]
        h = hs_ref[0]
        dyv = dy_ref[0].astype(f32)
        gl, gelu_vjp = jax.vjp(jax.nn.gelu, gt)
        dgt_ref[0] = gelu_vjp(dyv * h)[0].astype(dgt_ref.dtype)
        dh = dyv * gl

        (a, _), gates_vjp = jax.vjp(_lru_gates, xc_ref[0], ga_ref[...], gab_ref[...], gx_ref[...], gxb_ref[...],
                                    lam_ref[...])
        rows = lax.broadcasted_iota(jnp.int32, (ts, w), 0)
        dh = dh + jnp.where(rows == ts - 1, gcar[0:1, :], 0.0)
        b = pltpu.roll(a, ts - 1, 0)
        g = _scan_up(b, dh)
        gcar[0:1, :] = a[0:1, :] * g[0:1, :]
        hprev_row = jnp.where(at_seq_start, 0.0, hsb_ref[0][HALO - 1:HALO, :])
        hprev = jnp.where(rows == 0, hprev_row, pltpu.roll(h, 1, 0))
        dxc, dga, dgab, dgx, dgxb, dlam = gates_vjp((g * hprev, g))

        _accum(dga_ref, dga, first)
        _accum(dgx_ref, dgx, first)
        _accum(dgab_ref, dgab, first)
        _accum(dgxb_ref, dgxb, first)
        _accum(dlam_ref, dlam, first)
        _accum(dcb_ref, jnp.sum(dxc, axis=0, keepdims=True), first)

        dxc_ext[0:ts, :] = dxc
        dxp = jnp.zeros((ts, w), f32)
        for k in range(LRU_CONV):
            dxp = dxp + cw_ref[k:k + 1, :] * dxc_ext[pl.ds(LRU_CONV - 1 - k, ts), :]
        dxp_ref[0] = dxp.astype(dxp_ref.dtype)
        dxc_ext[ts:ts + HALO, :] = dxc[0:HALO, :]

        xin[0:HALO, :] = jnp.where(at_seq_start, 0.0, xpb_ref[0])
        xin[HALO:HALO + ts, :] = xp_ref[0]
        dcw_rows = [jnp.sum(dxc * xin[pl.ds(HALO - LRU_CONV + 1 + k, ts), :], axis=0, keepdims=True)
                    for k in range(LRU_CONV)]
        dcw_rows += [jnp.zeros((1, w), f32)] * (HALO - LRU_CONV)
        _accum(dcw_ref, jnp.concatenate(dcw_rows, axis=0), first)

    params = [cw, cb, ga, gab, gx, gxb, lam]
    pshape = lambda p: jax.ShapeDtypeStruct(p.shape, f32)
    outs = pl.pallas_call(
        body, name=name, grid=(nb, nt),
        in_specs=[tile, tile, before, tile, tile, tile, before] + [_const2(p.shape) for p in params],
        out_specs=[tile, tile, _const2((HALO, w))] + [_const2(p.shape) for p in params[1:]],
        out_shape=[jax.ShapeDtypeStruct((nb, ns, w), MXU_DTYPE), jax.ShapeDtypeStruct((nb, ns, w), MXU_DTYPE),
                   jax.ShapeDtypeStruct((HALO, w), f32)] + [pshape(p) for p in params[1:]],
        scratch_shapes=[pltpu.VMEM((ts + HALO, w), f32), pltpu.VMEM((HALO, w), f32),
                        pltpu.VMEM((ts + HALO, w), f32)],
        compiler_params=_cparams("arbitrary", "arbitrary"),
    )(dy, xpre, xpre, gate, xc, hs, hs, *params)
    return outs


SB_TILE = 256


def _split_dot(x, m):
    hi = x.astype(MXU_DTYPE)
    lo = (x - hi.astype(f32)).astype(MXU_DTYPE)
    return jnp.dot(hi, m, preferred_element_type=f32) + jnp.dot(lo, m, preferred_element_type=f32)


def _suffix_matrices(n):
    r = lax.broadcasted_iota(jnp.int32, (n, n), 0)
    c = lax.broadcasted_iota(jnp.int32, (n, n), 1)
    return (r > c).astype(MXU_DTYPE), (r >= c).astype(MXU_DTYPE)


def _sb_logits(qh, kb, strict):
    z = _mm_nt(qh, kb)
    ls = jnp.minimum(z, 0.0) - jnp.log1p(jnp.exp(-jnp.abs(z)))
    lk = ls - z
    if strict is not None:
        lk = jnp.where(strict, lk, 0.0)
    return ls, lk


def _head_masked(x, dtype):
    lane = lax.broadcasted_iota(jnp.int32, x.shape, 1)
    return (jnp.where(lane < SB_HEAD_DIM, x, 0.0).astype(dtype), jnp.where(lane >= SB_HEAD_DIM, x, 0.0).astype(dtype))


def _sb_specs(ns):
    npair = SB_WIDTH // LANES
    q = pl.BlockSpec((1, ns, LANES), lambda b, p: (b, 0, p))
    k = pl.BlockSpec((1, ns, LANES), lambda b, p: (b, 0, npair + p))
    v = pl.BlockSpec((1, ns, LANES), lambda b, p: (b, 0, 2 * npair + p))
    return q, k, v, npair


def sb_forward(qkv, name):
    nb, ns, _ = qkv.shape
    tq = SB_TILE
    nq = ns // tq
    qspec, kspec, vspec, npair = _sb_specs(ns)

    def body(q_ref, k_ref, v_ref, o_ref, qs, ks, vs, acc):
        scale = 1.0 / math.sqrt(SB_HEAD_DIM)
        q0m, q1m = _head_masked(q_ref[0] * scale, MXU_DTYPE)
        qs[0], qs[1] = q0m, q1m
        ks[...] = k_ref[0].astype(MXU_DTYPE)
        v0m, v1m = _head_masked(v_ref[0], MXU_DTYPE)
        vs[0], vs[1] = v0m, v1m
        mx, _ = _suffix_matrices(tq)
        rr = lax.broadcasted_iota(jnp.int32, (tq, tq), 0)
        cc = lax.broadcasted_iota(jnp.int32, (tq, tq), 1)
        strict = cc < rr

        def pair(hh, qh, k0, r, masked):
            kb = ks[pl.ds(k0, tq), :]
            ls, lk = _sb_logits(qh, kb, strict if masked else None)
            a = r + _split_dot(lk, mx)
            w = jnp.exp(ls + a)
            if masked:
                w = jnp.where(strict, w, 0.0)
            acc[...] += _split_dot(w, vs[hh, pl.ds(k0, tq), :])
            return a[:, 0:1] + lk[:, 0:1]

        def q_block(qi, carry):
            q0 = pl.multiple_of(qi * tq, tq)
            acc[...] = jnp.zeros_like(acc)
            for hh in range(2):
                qh = qs[hh, pl.ds(q0, tq), :]
                r = pair(hh, qh, q0, jnp.zeros((tq, 1), f32), True)

                def k_block(i, r):
                    k0 = pl.multiple_of((qi - 1 - i) * tq, tq)
                    return pair(hh, qh, k0, r, False)

                lax.fori_loop(0, qi, k_block, r)
            o_ref[0, pl.ds(q0, tq), :] = acc[...]
            return carry

        lax.fori_loop(0, nq, q_block, 0)

    return pl.pallas_call(
        body, name=name, grid=(nb, npair),
        in_specs=[qspec, kspec, vspec],
        out_specs=pl.BlockSpec((1, ns, LANES), lambda b, p: (b, 0, p)),
        out_shape=jax.ShapeDtypeStruct((nb, ns, SB_WIDTH), f32),
        scratch_shapes=[pltpu.VMEM((2, ns, LANES), MXU_DTYPE), pltpu.VMEM((ns, LANES), MXU_DTYPE),
                        pltpu.VMEM((2, ns, LANES), MXU_DTYPE), pltpu.VMEM((tq, LANES), f32)],
        compiler_params=_cparams("parallel", "parallel"),
    )(qkv, qkv, qkv)


def sb_backward(qkv, o, do, name):
    nb, ns, _ = qkv.shape
    tq = SB_TILE
    nq = ns // tq
    qspec, kspec, vspec, npair = _sb_specs(ns)
    ospec = pl.BlockSpec((1, ns, LANES), lambda b, p: (b, 0, p))

    def body(q_ref, k_ref, v_ref, o_ref, do_ref, dq_ref, dk_ref, dv_ref, qs, ks, vs, dos, dqacc, dkacc, dvacc):
        scale = 1.0 / math.sqrt(SB_HEAD_DIM)
        q0m, q1m = _head_masked(q_ref[0] * scale, MXU_DTYPE)
        qs[0], qs[1] = q0m, q1m
        k0m, k1m = _head_masked(k_ref[0], MXU_DTYPE)
        ks[0], ks[1] = k0m, k1m
        v0m, v1m = _head_masked(v_ref[0], MXU_DTYPE)
        vs[0], vs[1] = v0m, v1m
        d0m, d1m = _head_masked(do_ref[0].astype(f32), MXU_DTYPE)
        dos[0], dos[1] = d0m, d1m
        dkacc[...] = jnp.zeros_like(dkacc)
        dvacc[...] = jnp.zeros_like(dvacc)
        mx, mi = _suffix_matrices(tq)
        rr = lax.broadcasted_iota(jnp.int32, (tq, tq), 0)
        cc = lax.broadcasted_iota(jnp.int32, (tq, tq), 1)
        strict = cc < rr

        def pair(hh, qh, doh, dtot, k0, r, gsum, masked):
            kb = ks[hh, pl.ds(k0, tq), :]
            ls, lk = _sb_logits(qh, kb, strict if masked else None)
            a = r + _split_dot(lk, mx)
            w = jnp.exp(ls + a)
            if masked:
                w = jnp.where(strict, w, 0.0)
            e = w * _mm_nt(doh, vs[hh, pl.ds(k0, tq), :])
            esuf = gsum + _split_dot(e, mi)
            beta = jnp.exp(ls)
            dz = e * (1.0 - beta) - beta * (dtot - esuf)
            if masked:
                dz = jnp.where(strict, dz, 0.0)
            dqacc[...] += _mm(dz, kb)
            dkacc[pl.ds(k0, tq), :] += _mm_tn(dz, qh)
            dvacc[pl.ds(k0, tq), :] += _mm_tn(w, doh)
            return a[:, 0:1] + lk[:, 0:1], esuf[:, 0:1]

        def q_block(qi, carry):
            q0 = pl.multiple_of(qi * tq, tq)
            dqacc[...] = jnp.zeros_like(dqacc)
            for hh in range(2):
                qh = qs[hh, pl.ds(q0, tq), :]
                doh = dos[hh, pl.ds(q0, tq), :]
                dtot = jnp.sum(doh.astype(f32) * o_ref[0, pl.ds(q0, tq), :], axis=1, keepdims=True)
                zero = jnp.zeros((tq, 1), f32)
                r, gsum = pair(hh, qh, doh, dtot, q0, zero, zero, True)

                def k_block(i, c):
                    k0 = pl.multiple_of((qi - 1 - i) * tq, tq)
                    return pair(hh, qh, doh, dtot, k0, c[0], c[1], False)

                lax.fori_loop(0, qi, k_block, (r, gsum))
            dq_ref[0, pl.ds(q0, tq), :] = (dqacc[...] * scale).astype(dq_ref.dtype)
            return carry

        lax.fori_loop(0, nq, q_block, 0)
        dk_ref[0] = dkacc[...].astype(dk_ref.dtype)
        dv_ref[0] = dvacc[...].astype(dv_ref.dtype)

    dshape = jax.ShapeDtypeStruct((nb, ns, SB_WIDTH), MXU_DTYPE)
    return pl.pallas_call(
        body, name=name, grid=(nb, npair),
        in_specs=[qspec, kspec, vspec, ospec, ospec],
        out_specs=[ospec, ospec, ospec],
        out_shape=[dshape, dshape, dshape],
        scratch_shapes=[pltpu.VMEM((2, ns, LANES), MXU_DTYPE)] * 4
        + [pltpu.VMEM((tq, LANES), f32), pltpu.VMEM((ns, LANES), f32), pltpu.VMEM((ns, LANES), f32)],
        compiler_params=_cparams("parallel", "parallel"),
    )(qkv, qkv, qkv, o, do)


SSM_PAIRS = SSM_HEADS // 2
PAIRS_PER_GROUP = SSM_PAIRS // SSM_GROUPS
GROUP_WIDTH = SSM_WIDTH // SSM_GROUPS


def _silu(x):
    return x * jax.nn.sigmoid(x)


def _pair_expand(p):
    hrow = lax.broadcasted_iota(jnp.int32, (LANES, LANES), 0)
    lane = lax.broadcasted_iota(jnp.int32, (LANES, LANES), 1)
    return (hrow == jnp.where(lane >= SSM_HEAD_DIM, 2 * p + 1, 2 * p)).astype(f32)


def _row_expand(v, ex):
    return jnp.sum(_mm_exact(jnp.broadcast_to(v, (8, LANES)), ex), axis=0, keepdims=True) * 0.125


def _ssd_chunk(xs_pre, b_pre, c_pre, dt_raw, dt_raw_t, z, st, dt_bias_r, dt_bias_c, a_log_r, a_log_c, d_skip,
               gains):
    n = dt_raw.shape[0]
    rows = lax.broadcasted_iota(jnp.int32, (n, n), 0)
    cols = lax.broadcasted_iota(jnp.int32, (n, n), 1)
    tril = cols <= rows
    tri_l = tril.astype(f32)
    tri_u = (rows <= cols).astype(f32)
    lane = lax.broadcasted_iota(jnp.int32, (n, LANES), 1)
    sub = lax.broadcasted_iota(jnp.int32, (LANES, n), 0)

    dt = _softplus(dt_raw + dt_bias_r)
    a_r = -jnp.exp(a_log_r)
    da = dt * a_r
    acs = _mm_exact(tri_l, da)
    dt_t = _softplus(dt_raw_t + dt_bias_c)
    acs_t = _mm_exact(dt_t * (-jnp.exp(a_log_c)), tri_u)

    bs = [_silu(b) for b in b_pre]
    cs = [_silu(c) for c in c_pre]
    cb = [dmm_nt(cs[g], bs[g]) for g in range(SSM_GROUPS)]

    ys, st_new = [], []
    for p in range(SSM_PAIRS):
        g = p // PAIRS_PER_GROUP
        ex = _pair_expand(p)
        xs = _silu(xs_pre[p])
        dt_p = _mm_exact(dt, ex)
        da_p = dt_p * _row_expand(a_r, ex)
        acs_p = _mm_exact(tri_l, da_p)
        end_p = jnp.sum(da_p, axis=0, keepdims=True)
        xdt = xs * dt_p
        y = jnp.exp(acs_p) * dmm(cs[g], st[p])
        for hh in range(2):
            h = 2 * p + hh
            col = jnp.sum(jnp.where(lane == h, acs, 0.0), axis=1, keepdims=True)
            row = jnp.sum(jnp.where(sub == h, acs_t, 0.0), axis=0, keepdims=True)
            decay = jnp.where(tril, jnp.exp(jnp.where(tril, col - row, 0.0)), 0.0)
            head = (lane >= SSM_HEAD_DIM) if hh else (lane < SSM_HEAD_DIM)
            y = y + dmm(cb[g] * decay, jnp.where(head, xdt, 0.0))
        st_new.append(jnp.exp(end_p) * st[p] + dmm_tn(bs[g], xdt * jnp.exp(end_p - acs_p)))
        ys.append(y + _row_expand(d_skip, ex) * xs)
    out = []
    for g in range(SSM_GROUPS):
        yg = jnp.concatenate(ys[g * PAIRS_PER_GROUP:(g + 1) * PAIRS_PER_GROUP], axis=1) * _silu(z[g])
        out.append(_rms(yg, gains[g]))
    return out, st_new


def _ssd_chunk_inputs(xconv, dtr, z, st_ref, gain):
    xs_pre = [xconv[:, LANES * p:LANES * (p + 1)] for p in range(SSM_PAIRS)]
    b0 = SSM_WIDTH
    c0 = SSM_WIDTH + SSM_GROUPS * SSM_STATE
    b_pre = [xconv[:, b0 + SSM_STATE * g:b0 + SSM_STATE * (g + 1)] for g in range(SSM_GROUPS)]
    c_pre = [xconv[:, c0 + SSM_STATE * g:c0 + SSM_STATE * (g + 1)] for g in range(SSM_GROUPS)]
    zs = [z[:, GROUP_WIDTH * g:GROUP_WIDTH * (g + 1)] for g in range(SSM_GROUPS)]
    sts = [st_ref[p] for p in range(SSM_PAIRS)]
    gains = [gain[:, GROUP_WIDTH * g:GROUP_WIDTH * (g + 1)] for g in range(SSM_GROUPS)]
    return xs_pre, b_pre, c_pre, dtr, dtr.T, zs, sts, gains


def ssd_forward(xbc, dt_raw, z, cw, cb, dbr, dbc, alr, alc, dsk, gain, name):
    nb, ns, wx = xbc.shape
    ln = SSM_CHUNK
    nt = ns // ln
    tile = lambda c: pl.BlockSpec((1, ln, c), lambda b, j: (b, j, 0))
    st_spec = pl.BlockSpec((1, 1, SSM_PAIRS, SSM_STATE, LANES), lambda b, j: (b, j, 0, 0, 0))

    def body(xbc_ref, dt_ref, z_ref, cw_ref, cb_ref, dbr_ref, dbc_ref, alr_ref, alc_ref, dsk_ref, gain_ref,
             y_ref, xconv_ref, stp_ref, xin, st):
        @pl.when(pl.program_id(1) == 0)
        def _():
            xin[0:HALO, :] = jnp.zeros((HALO, wx), f32)
            st[...] = jnp.zeros_like(st)

        xin[HALO:HALO + ln, :] = xbc_ref[0]
        xconv = jnp.broadcast_to(cb_ref[...], (ln, wx))
        for k in range(SSM_CONV):
            xconv = xconv + cw_ref[k:k + 1, :] * xin[pl.ds(HALO - SSM_CONV + 1 + k, ln), :]
        xin[0:HALO, :] = xin[ln:ln + HALO, :]
        xconv_ref[0] = xconv
        stp_ref[0, 0] = st[...]
        xs_pre, b_pre, c_pre, dtr, dtr_t, zs, sts, gains = _ssd_chunk_inputs(xconv, dt_ref[0], z_ref[0], st,
                                                                             gain_ref[...])
        out, st_new = _ssd_chunk(xs_pre, b_pre, c_pre, dtr, dtr_t, zs, sts, dbr_ref[...], dbc_ref[...],
                                 alr_ref[...], alc_ref[...], dsk_ref[...], gains)
        y_ref[0] = jnp.concatenate(out, axis=1).astype(y_ref.dtype)
        for p in range(SSM_PAIRS):
            st[p] = st_new[p]

    params = [cw, cb, dbr, dbc, alr, alc, dsk, gain]
    return pl.pallas_call(
        body, name=name, grid=(nb, nt),
        in_specs=[tile(wx), tile(LANES), tile(SSM_WIDTH)] + [_const2(p.shape) for p in params],
        out_specs=[tile(SSM_WIDTH), tile(wx), st_spec],
        out_shape=[jax.ShapeDtypeStruct((nb, ns, SSM_WIDTH), MXU_DTYPE), jax.ShapeDtypeStruct((nb, ns, wx), f32),
                   jax.ShapeDtypeStruct((nb, nt, SSM_PAIRS, SSM_STATE, LANES), f32)],
        scratch_shapes=[pltpu.VMEM((ln + HALO, wx), f32), pltpu.VMEM((SSM_PAIRS, SSM_STATE, LANES), f32)],
        compiler_params=_cparams("arbitrary", "arbitrary"),
    )(xbc, dt_raw, z, *params)


def ssd_backward(dy, xbc, xconv, dt_raw, z, stp, cw, cb, dbr, dbc, alr, alc, dsk, gain, name):
    nb, ns, wx = xbc.shape
    ln = SSM_CHUNK
    nt = ns // ln
    per = ln // HALO
    rj = lambda j: nt - 1 - j
    tile = lambda c: pl.BlockSpec((1, ln, c), lambda b, j: (b, rj(j), 0))
    before = pl.BlockSpec((1, HALO, wx), lambda b, j: (b, jnp.maximum(rj(j) * per - 1, 0), 0))
    st_spec = pl.BlockSpec((1, 1, SSM_PAIRS, SSM_STATE, LANES), lambda b, j: (b, rj(j), 0, 0, 0))

    def body(dy_ref, xbc_ref, xbcb_ref, xconv_ref, dt_ref, z_ref, stp_ref,
             cw_ref, cb_ref, dbr_ref, dbc_ref, alr_ref, alc_ref, dsk_ref, gain_ref,
             dxbc_ref, ddt_ref, dz_ref, dcw_ref, dcb_ref, ddbr_ref, ddbc_ref, dalr_ref, dalc_ref, ddsk_ref, dgain_ref,
             dxc_ext, dst, xin):
        j = pl.program_id(1)
        first = _first_step()
        at_seq_start = j == nt - 1

        @pl.when(j == 0)
        def _():
            dxc_ext[ln:ln + HALO, :] = jnp.zeros((HALO, wx), f32)
            dst[...] = jnp.zeros_like(dst)

        xs_pre, b_pre, c_pre, dtr, dtr_t, zs, sts, gains = _ssd_chunk_inputs(xconv_ref[0], dt_ref[0], z_ref[0],
                                                                             stp_ref.at[0, 0], gain_ref[...])
        _, vjp = jax.vjp(_ssd_chunk, xs_pre, b_pre, c_pre, dtr, dtr_t, zs, sts, dbr_ref[...], dbc_ref[...],
                         alr_ref[...], alc_ref[...], dsk_ref[...], gains)
        dyv = dy_ref[0].astype(f32)
        cot = ([dyv[:, GROUP_WIDTH * g:GROUP_WIDTH * (g + 1)] for g in range(SSM_GROUPS)],
               [dst[p] for p in range(SSM_PAIRS)])
        dxs, db, dc, ddt, ddt_t, dzs, dsts, ddbr, ddbc, dalr, dalc, ddsk, dgains = vjp(cot)
        for p in range(SSM_PAIRS):
            dst[p] = dsts[p]
        ddt_ref[0] = (ddt + ddt_t.T).astype(ddt_ref.dtype)
        dz_ref[0] = jnp.concatenate(dzs, axis=1).astype(dz_ref.dtype)
        _accum(ddbr_ref, ddbr, first)
        _accum(ddbc_ref, ddbc, first)
        _accum(dalr_ref, dalr, first)
        _accum(dalc_ref, dalc, first)
        _accum(ddsk_ref, ddsk, first)
        _accum(dgain_ref, jnp.concatenate(dgains, axis=1), first)

        dxc = jnp.concatenate(dxs + db + dc, axis=1)
        _accum(dcb_ref, jnp.sum(dxc, axis=0, keepdims=True), first)
        dxc_ext[0:ln, :] = dxc
        dxp = jnp.zeros((ln, wx), f32)
        for k in range(SSM_CONV):
            dxp = dxp + cw_ref[k:k + 1, :] * dxc_ext[pl.ds(SSM_CONV - 1 - k, ln), :]
        dxbc_ref[0] = dxp.astype(dxbc_ref.dtype)
        dxc_ext[ln:ln + HALO, :] = dxc[0:HALO, :]

        xin[0:HALO, :] = jnp.where(at_seq_start, 0.0, xbcb_ref[0])
        xin[HALO:HALO + ln, :] = xbc_ref[0]
        dcw_rows = [jnp.sum(dxc * xin[pl.ds(HALO - SSM_CONV + 1 + k, ln), :], axis=0, keepdims=True)
                    for k in range(SSM_CONV)]
        dcw_rows += [jnp.zeros((1, wx), f32)] * (HALO - SSM_CONV)
        _accum(dcw_ref, jnp.concatenate(dcw_rows, axis=0), first)

    params = [cw, cb, dbr, dbc, alr, alc, dsk, gain]
    pshape = lambda p: jax.ShapeDtypeStruct(p.shape, f32)
    return pl.pallas_call(
        body, name=name, grid=(nb, nt),
        in_specs=[tile(SSM_WIDTH), tile(wx), before, tile(wx), tile(LANES), tile(SSM_WIDTH), st_spec]
        + [_const2(p.shape) for p in params],
        out_specs=[tile(wx), tile(LANES), tile(SSM_WIDTH), _const2((HALO, wx))] + [_const2(p.shape) for p in params[1:]],
        out_shape=[jax.ShapeDtypeStruct((nb, ns, wx), MXU_DTYPE), jax.ShapeDtypeStruct((nb, ns, LANES), MXU_DTYPE),
                   jax.ShapeDtypeStruct((nb, ns, SSM_WIDTH), MXU_DTYPE), jax.ShapeDtypeStruct((HALO, wx), f32)]
        + [pshape(p) for p in params[1:]],
        scratch_shapes=[pltpu.VMEM((ln + HALO, wx), f32), pltpu.VMEM((SSM_PAIRS, SSM_STATE, LANES), f32),
                        pltpu.VMEM((ln + HALO, wx), f32)],
        compiler_params=_cparams("arbitrary", "arbitrary"),
    )(dy, xbc, xbc, xconv, dt_raw, z, stp, *params)


CONF_HALO = 32
CONF_OFF = CONF_HALO - CONF_KERNEL + 1


def _conf_specs(ts, c, nt):
    per = ts // CONF_HALO
    tile = pl.BlockSpec((1, ts, c), lambda b, j: (b, j, 0))
    before = pl.BlockSpec((1, CONF_HALO, c), lambda b, j: (b, jnp.maximum(j * per - 1, 0), 0))
    after = pl.BlockSpec((1, CONF_HALO, c), lambda b, j: (b, jnp.minimum((j + 1) * per, nt * per - 1), 0))
    return tile, before, after


def _glu(x):
    return x[:, :CONF_WIDTH] * jax.nn.sigmoid(x[:, CONF_WIDTH:])


def _layernorm_parts(c):
    xc = c - jnp.mean(c, axis=-1, keepdims=True)
    r = lax.rsqrt(jnp.mean(xc * xc, axis=-1, keepdims=True) + EPS)
    return xc * r, r


def conf_forward(glu, cw, cb, ln_g, ln_b, name):
    nb, ns, wg = glu.shape
    w = CONF_WIDTH
    ts = SEQ_TILE
    nt = ns // ts
    tile, before, _ = _conf_specs(ts, wg, nt)

    def body(x_ref, xb_ref, cw_ref, cb_ref, g_ref, b_ref, y_ref, u_ext):
        u_ext[0:CONF_HALO, :] = jnp.where(pl.program_id(1) == 0, 0.0, _glu(xb_ref[0]))
        u_ext[CONF_HALO:CONF_HALO + ts, :] = _glu(x_ref[0])
        conv = jnp.broadcast_to(cb_ref[...], (ts, w))
        for k in range(CONF_KERNEL):
            conv = conv + cw_ref[k:k + 1, :] * u_ext[pl.ds(CONF_OFF + k, ts), :]
        xhat, _ = _layernorm_parts(conv)
        y_ref[0] = _silu(xhat * g_ref[...] + b_ref[...]).astype(y_ref.dtype)

    params = [cw, cb, ln_g, ln_b]
    return pl.pallas_call(
        body, name=name, grid=(nb, nt),
        in_specs=[tile, before] + [_const2(p.shape) for p in params],
        out_specs=pl.BlockSpec((1, ts, w), lambda b, j: (b, j, 0)),
        out_shape=jax.ShapeDtypeStruct((nb, ns, w), MXU_DTYPE),
        scratch_shapes=[pltpu.VMEM((ts + CONF_HALO, w), f32)],
        compiler_params=_cparams("parallel", "parallel"),
    )(glu, glu, *params)


def conf_backward(dy, glu, cw, cb, ln_g, ln_b, name):
    nb, ns, wg = glu.shape
    w = CONF_WIDTH
    ts = SEQ_TILE
    nt = ns // ts
    te = ts + CONF_HALO
    tile, before, after = _conf_specs(ts, wg, nt)
    dtile, _, dafter = _conf_specs(ts, w, nt)

    def body(dy_ref, dya_ref, x_ref, xb_ref, xa_ref, cw_ref, cb_ref, g_ref, b_ref,
             dx_ref, dcw_ref, dcb_ref, dg_ref, db_ref, u_ext, dc_ext):
        j = pl.program_id(1)
        first = _first_step()
        x = x_ref[0]
        u_ext[0:CONF_HALO, :] = jnp.where(j == 0, 0.0, _glu(xb_ref[0]))
        u_ext[CONF_HALO:CONF_HALO + ts, :] = _glu(x)
        u_ext[CONF_HALO + ts:CONF_HALO + te, :] = _glu(xa_ref[0])
        conv = jnp.broadcast_to(cb_ref[...], (te, w))
        for k in range(CONF_KERNEL):
            conv = conv + cw_ref[k:k + 1, :] * u_ext[pl.ds(CONF_OFF + k, te), :]
        xhat, r = _layernorm_parts(conv)
        lnout = xhat * g_ref[...] + b_ref[...]
        sg = jax.nn.sigmoid(lnout)
        rows = lax.broadcasted_iota(jnp.int32, (te, w), 0)
        dyv = jnp.concatenate([dy_ref[0].astype(f32), dya_ref[0].astype(f32)], axis=0)
        dyv = jnp.where(jnp.logical_and(j == nt - 1, rows >= ts), 0.0, dyv)
        dln = dyv * sg * (1.0 + lnout * (1.0 - sg))
        in_tile = rows < ts
        _accum(dg_ref, jnp.sum(jnp.where(in_tile, dln * xhat, 0.0), axis=0, keepdims=True), first)
        _accum(db_ref, jnp.sum(jnp.where(in_tile, dln, 0.0), axis=0, keepdims=True), first)
        dxh = dln * g_ref[...]
        dconv = r * (dxh - jnp.mean(dxh, axis=-1, keepdims=True) - xhat * jnp.mean(dxh * xhat, axis=-1, keepdims=True))
        dc_ext[...] = dconv
        dct = dconv[0:ts, :]
        _accum(dcb_ref, jnp.sum(dct, axis=0, keepdims=True), first)
        du = jnp.zeros((ts, w), f32)
        dcw_rows = []
        for k in range(CONF_KERNEL):
            du = du + cw_ref[k:k + 1, :] * dc_ext[pl.ds(CONF_KERNEL - 1 - k, ts), :]
            dcw_rows.append(jnp.sum(dct * u_ext[pl.ds(CONF_OFF + k, ts), :], axis=0, keepdims=True))
        dcw_rows.append(jnp.zeros((1, w), f32))
        _accum(dcw_ref, jnp.concatenate(dcw_rows, axis=0), first)
        sb = jax.nn.sigmoid(x[:, w:])
        dx_ref[0] = jnp.concatenate([du * sb, du * x[:, :w] * sb * (1.0 - sb)], axis=1).astype(dx_ref.dtype)

    params = [cw, cb, ln_g, ln_b]
    return pl.pallas_call(
        body, name=name, grid=(nb, nt),
        in_specs=[dtile, dafter, tile, before, after] + [_const2(p.shape) for p in params],
        out_specs=[tile] + [_const2(p.shape) for p in params],
        out_shape=[jax.ShapeDtypeStruct((nb, ns, wg), MXU_DTYPE)] + [jax.ShapeDtypeStruct(p.shape, f32) for p in params],
        scratch_shapes=[pltpu.VMEM((te + CONF_HALO, w), f32), pltpu.VMEM((te, w), f32)],
        compiler_params=_cparams("arbitrary", "arbitrary"),
    )(dy, dy, glu, glu, glu, *params)


def _row(v):
    return v.reshape(1, -1).astype(f32)


def _pad_to(v, n, axis):
    pads = [(0, 0)] * v.ndim
    pads[axis] = (0, n - v.shape[axis])
    return jnp.pad(v, pads)


def _block_diag(w):
    nh, d, _ = w.shape
    eye = jnp.eye(nh, dtype=w.dtype)
    return (eye[:, None, :, None] * w[:, :, None, :]).reshape(nh * d, nh * d)


def _diag_blocks(m, nh):
    d = m.shape[0] // nh
    idx = jnp.arange(nh)
    return m.reshape(nh, d, nh, d)[idx, :, idx, :]


def _mix_even_fwd(h, gpre, w, nb, ns):
    t = nb * ns
    w_in = w["ev_w_in"][0]
    w_lx, w_lg, w_qkv = w_in[:, :LRU_WIDTH], w_in[:, LRU_WIDTH:2 * LRU_WIDTH], w_in[:, 2 * LRU_WIDTH:]
    xpre, gate, qkv = norm_matmul(h, gpre, [w_lx, w_lg, w_qkv], [f32, f32, f32], name="ev_in_proj")
    lru_p = [w["ev_lru_conv_w"][0], _row(w["ev_lru_conv_b"][0]),
             _block_diag(w["ev_lru_gate_a_w"][0]).astype(MXU_DTYPE), _row(w["ev_lru_gate_a_b"][0]),
             _block_diag(w["ev_lru_gate_x_w"][0]).astype(MXU_DTYPE), _row(w["ev_lru_gate_x_b"][0]),
             _row(w["ev_lru_lambda"][0])]
    xpre3, gate3, qkv3 = xpre.reshape(nb, ns, -1), gate.reshape(nb, ns, -1), qkv.reshape(nb, ns, -1)
    y_a, xc, hs = lru_forward(xpre3, gate3, *lru_p, name="ev_lru_fwd")
    o = sb_forward(qkv3, name="ev_sb_fwd")
    ys = [y_a.reshape(t, -1), o.reshape(t, -1)]
    saved = dict(xpre=xpre3, gate=gate3, qkv=qkv3, xc=xc, hs=hs, o=o, lru_p=lru_p)
    w_out = w["ev_w_out"][0]
    return ys, [w_out[:LRU_WIDTH], w_out[LRU_WIDTH:]], saved


def _mix_even_bwd(dys, saved, w, nb, ns):
    t = nb * ns
    dy_a, dy_b = [d.reshape(nb, ns, -1) for d in dys]
    outs = lru_backward(dy_a, saved["xpre"], saved["gate"], saved["xc"], saved["hs"], *saved["lru_p"],
                        name="ev_lru_bwd")
    dxp, dgt, dcw, dcb, dga, dgab, dgx, dgxb, dlam = outs
    dq, dk, dv = sb_backward(saved["qkv"], saved["o"], dy_b, name="ev_sb_bwd")
    w_in_t = w["ev_w_in"][0].T
    pieces = [dxp, dgt, dq, dk, dv]
    gs = [d.reshape(t, -1) for d in pieces]
    wts = [w_in_t[LRU_WIDTH * i:LRU_WIDTH * (i + 1)] for i in range(5)]
    grads = {
        "ev_lru_conv_w": dcw[:LRU_CONV][None], "ev_lru_conv_b": dcb,
        "ev_lru_gate_a_w": _diag_blocks(dga, LRU_HEADS)[None], "ev_lru_gate_a_b": dgab,
        "ev_lru_gate_x_w": _diag_blocks(dgx, LRU_HEADS)[None], "ev_lru_gate_x_b": dgxb,
        "ev_lru_lambda": dlam,
    }
    return gs, wts, grads


def _odd_params(w):
    ssd_p = [w["od_ssm_conv_w"][0], _row(w["od_ssm_conv_b"][0]),
             _pad_to(_row(w["od_ssm_dt_bias"][0]), LANES, 1), _pad_to(_row(w["od_ssm_dt_bias"][0]), LANES, 1).T,
             _pad_to(_row(w["od_ssm_a_log"][0]), LANES, 1), _pad_to(_row(w["od_ssm_a_log"][0]), LANES, 1).T,
             _pad_to(_row(w["od_ssm_d"][0]), LANES, 1), _row(w["od_ssm_norm"][0])]
    conf_p = [_pad_to(w["od_cm_conv_w"][0], CONF_HALO, 0), _row(w["od_cm_conv_b"][0]),
              _row(w["od_cm_ln_g"][0]), _row(w["od_cm_ln_b"][0])]
    return ssd_p, conf_p


ODD_SPLITS = (SSM_WIDTH, SSM_WIDTH + SSM_XBC, SSM_WIDTH + SSM_XBC + SSM_HEADS)


def _mix_odd_fwd(h, gpre, w, nb, ns):
    t = nb * ns
    w_in = w["od_w_in"][0]
    s0, s1, s2 = ODD_SPLITS
    w_z, w_xbc, w_dt, w_glu = w_in[:, :s0], w_in[:, s0:s1], _pad_to(w_in[:, s1:s2], LANES, 1), w_in[:, s2:]
    zz, xbc, dtr, glu = norm_matmul(h, gpre, [w_z, w_xbc, w_dt, w_glu], [f32] * 4, name="od_in_proj")
    ssd_p, conf_p = _odd_params(w)
    zz3, xbc3, dtr3, glu3 = [a.reshape(nb, ns, -1) for a in (zz, xbc, dtr, glu)]
    y_c, xconv, stp = ssd_forward(xbc3, dtr3, zz3, *ssd_p, name="od_ssd_fwd")
    y_d = conf_forward(glu3, *conf_p, name="od_conf_fwd")
    ys = [y_c.reshape(t, -1), y_d.reshape(t, -1)]
    saved = dict(z=zz3, xbc=xbc3, dtr=dtr3, glu=glu3, xconv=xconv, stp=stp, ssd_p=ssd_p, conf_p=conf_p)
    w_out = w["od_w_out"][0]
    return ys, [w_out[:SSM_WIDTH], w_out[SSM_WIDTH:]], saved


def _mix_odd_bwd(dys, saved, w, nb, ns):
    t = nb * ns
    dy_c, dy_d = [d.reshape(nb, ns, -1) for d in dys]
    outs = ssd_backward(dy_c, saved["xbc"], saved["xconv"], saved["dtr"], saved["z"], saved["stp"], *saved["ssd_p"],
                        name="od_ssd_bwd")
    dxbc, ddt, dz, dcw, dcb, ddbr, ddbc, dalr, dalc, ddsk, dgain = outs
    dglu, ccw, ccb, clg, clb = conf_backward(dy_d, saved["glu"], *saved["conf_p"], name="od_conf_bwd")
    w_in_t = w["od_w_in"][0].T
    s0, s1, s2 = ODD_SPLITS
    gs = [d.reshape(t, -1) for d in (dz, dxbc, ddt, dglu)]
    wts = [w_in_t[:s0], w_in_t[s0:s1], _pad_to(w_in_t[s1:s2], LANES, 0), w_in_t[s2:]]
    nh = SSM_HEADS
    grads = {
        "od_ssm_conv_w": dcw[:SSM_CONV][None], "od_ssm_conv_b": dcb,
        "od_ssm_dt_bias": ddbr[:, :nh] + ddbc[:nh, 0][None], "od_ssm_a_log": dalr[:, :nh] + dalc[:nh, 0][None],
        "od_ssm_d": ddsk[:, :nh], "od_ssm_norm": dgain,
        "od_cm_conv_w": ccw[:CONF_KERNEL][None], "od_cm_conv_b": ccb, "od_cm_ln_g": clg, "od_cm_ln_b": clb,
    }
    return gs, wts, grads


def local_step(x, p, target, w):
    nb, ns, d = x.shape
    t = nb * ns
    h = x.reshape(t, d)
    depth = p.shape[0]
    tapes = []
    for i in range(depth):
        even = i % 2 == 0
        tag = f"l{i}_"
        gpre = _row(w["norm_mix_pre"][i])
        ys, w_outs, saved = (_mix_even_fwd if even else _mix_odd_fwd)(h, gpre, w, nb, ns)
        h1, m = matmul_residual_norm(ys, w_outs, h, _row(w["norm_mix_post"][i]), name=tag + "out_proj")
        a1, = norm_matmul(h1, _row(w["norm_mlp_pre"][i]), [w["mlp_w1"][i]], [f32], name=tag + "mlp_up")
        h2, f = matmul_residual_norm([a1], [w["mlp_w2"][i]], h1, _row(w["norm_mlp_post"][i]), name=tag + "mlp_down",
                                     relu2=True)
        pi = p[i].reshape(t, -1)
        h3, gl, emb = ple_forward(h2, pi, w["ple_w_gate"][i], w["ple_w_proj"][i], _row(w["norm_ple"][i]),
                                  name=tag + "ple")
        tapes.append(dict(h=h, ys=ys, w_outs=w_outs, saved=saved, h1=h1, m=m, a1=a1, h2=h2, f=f, pi=pi, gl=gl,
                          emb=emb))
        h = h3

    loss_row, dh = loss_and_grad(h, target.reshape(t, d), name="loss")
    grads = {}
    stacked = {k: [None] * depth for k in ("norm_mix_pre", "norm_mix_post", "norm_mlp_pre", "norm_mlp_post", "norm_ple",
                                           "mlp_w1", "mlp_w2", "ple_w_proj", "ple_w_gate")}
    for i in reversed(range(depth)):
        even = i % 2 == 0
        tag = f"l{i}_"
        tp = tapes[i]
        dh2, dgl, demb, dg = ple_backward(dh, tp["h2"], tp["gl"], tp["emb"], _row(w["norm_ple"][i]),
                                          w["ple_w_gate"][i].T, name=tag + "ple_bwd")
        stacked["norm_ple"][i] = dg
        stacked["ple_w_gate"][i] = weight_grad(tp["h2"], dgl, name=tag + "dw_gate")
        stacked["ple_w_proj"][i] = weight_grad(tp["pi"], demb, name=tag + "dw_proj")
        d_f, (da1,), dg = bwd_through_norm_out(dh2, tp["f"], _row(w["norm_mlp_post"][i]), [w["mlp_w2"][i].T],
                                               [MXU_DTYPE], name=tag + "mlp_down_bwd", relu2_of=tp["a1"])
        stacked["norm_mlp_post"][i] = dg
        stacked["mlp_w2"][i] = weight_grad(tp["a1"], d_f, name=tag + "dw2", prologue="relu2")
        gpre = _row(w["norm_mlp_pre"][i])
        dh1, dg = bwd_through_norm_in(dh2, [da1], [w["mlp_w1"][i].T], tp["h1"], gpre, name=tag + "mlp_up_bwd")
        stacked["norm_mlp_pre"][i] = dg
        stacked["mlp_w1"][i] = weight_grad(tp["h1"], da1, name=tag + "dw1", prologue="rms", gain=gpre)
        dm, dys, dg = bwd_through_norm_out(dh1, tp["m"], _row(w["norm_mix_post"][i]), [wo.T for wo in tp["w_outs"]],
                                           [f32] * len(tp["w_outs"]), name=tag + "out_proj_bwd")
        stacked["norm_mix_post"][i] = dg
        dw_out = jnp.concatenate([weight_grad(y, dm, name=tag + f"dw_out{k}") for k, y in enumerate(tp["ys"])], axis=0)
        gs, wts, mix_grads = (_mix_even_bwd if even else _mix_odd_bwd)(dys, tp["saved"], w, nb, ns)
        grads.update(mix_grads)
        gpre = _row(w["norm_mix_pre"][i])
        dh, dg = bwd_through_norm_in(dh1, gs, wts, tp["h"], gpre, name=tag + "in_proj_bwd")
        stacked["norm_mix_pre"][i] = dg
        dw_in = [weight_grad(tp["h"], g, name=tag + f"dw_in{k}", prologue="rms", gain=gpre) for k, g in enumerate(gs)]
        if even:
            grads["ev_w_in"] = jnp.concatenate(dw_in, axis=1)[None]
            grads["ev_w_out"] = dw_out[None]
        else:
            dw_in[2] = dw_in[2][:, :SSM_HEADS]
            grads["od_w_in"] = jnp.concatenate(dw_in, axis=1)[None]
            grads["od_w_out"] = dw_out[None]
    for k, v in stacked.items():
        grads[k] = jnp.concatenate(v, axis=0) if v[0].shape[0] == 1 and v[0].ndim == 2 and k.startswith("norm") \
            else jnp.stack(v, axis=0)
    return loss_row[0, 0], dh.reshape(nb, ns, d), grads


MESH_ID = pl.DeviceIdType.MESH
ANY = pl.BlockSpec(memory_space=pl.ANY)


def _mesh_pos():
    return lax.axis_index("x"), lax.axis_index("y"), lax.axis_index("c")


def all_gather(shard, name):
    r, n = shard.shape

    def body(x_ref, out_ref, send_sems, recv_sems, local_sem):
        x, y, c = _mesh_pos()
        me, sibling = (x, y, c), (x, y, 1 - c)
        chips = [(1 - x, y), (x, 1 - y), (1 - x, 1 - y)]

        def slot(px, py, pc):
            return out_ref.at[4 * px + 2 * py + pc]

        def copy(k, block, to, src=None):
            return pltpu.make_async_remote_copy(
                src_ref=slot(*block) if src is None else src, dst_ref=slot(*block),
                send_sem=send_sems.at[k], recv_sem=recv_sems.at[k], device_id=to, device_id_type=MESH_ID)

        mine = pltpu.make_async_copy(x_ref, slot(*me), local_sem)
        mine.start()
        first = [copy(0, me, sibling, src=x_ref)]
        first += [copy(1 + j, me, (*chip, c), src=x_ref) for j, chip in enumerate(chips)]
        for cp in first:
            cp.start()
        passed = [copy(4 + j, (*chip, c), sibling) for j, chip in enumerate(chips)]
        for j, chip in enumerate(chips):
            copy(1 + j, (*chip, c), me).wait_recv()
            passed[j].start()
        copy(0, sibling, me).wait_recv()
        for j, chip in enumerate(chips):
            copy(4 + j, (*chip, 1 - c), me).wait_recv()
        for cp in first + passed:
            cp.wait_send()
        mine.wait()

    return pl.pallas_call(
        body, name=name,
        out_shape=jax.ShapeDtypeStruct((N_DEV, r, n), shard.dtype),
        in_specs=[ANY], out_specs=ANY,
        scratch_shapes=[pltpu.SemaphoreType.DMA((7,)), pltpu.SemaphoreType.DMA((7,)), pltpu.SemaphoreType.DMA],
    )(shard)


def scatter_to_sibling(parts, name):
    _, r, n = parts.shape

    def body(g_ref, out_ref, send_sems, recv_sems):
        x, y, c = _mesh_pos()
        sibling = (x, y, 1 - c)
        copies = []
        for chip in range(4):
            copies.append(pltpu.make_async_remote_copy(
                src_ref=g_ref.at[2 * chip + (1 - c)], dst_ref=out_ref.at[chip],
                send_sem=send_sems.at[chip], recv_sem=recv_sems.at[chip], device_id=sibling, device_id_type=MESH_ID))
        for cp in copies:
            cp.start()
        for cp in copies:
            cp.wait_recv()
        for cp in copies:
            cp.wait_send()

    return pl.pallas_call(
        body, name=name,
        out_shape=jax.ShapeDtypeStruct((4, r, n), parts.dtype),
        in_specs=[ANY], out_specs=ANY,
        scratch_shapes=[pltpu.SemaphoreType.DMA((4,)), pltpu.SemaphoreType.DMA((4,))],
    )(parts)


def scatter_to_chips(partial, name):
    _, r, n = partial.shape

    def body(p_ref, out_ref, send_sems, recv_sems):
        x, y, c = _mesh_pos()
        chips = [(1 - x, y), (x, 1 - y), (1 - x, 1 - y)]
        copies = []
        for j, (px, py) in enumerate(chips):
            copies.append(pltpu.make_async_remote_copy(
                src_ref=p_ref.at[2 * px + py], dst_ref=out_ref.at[j],
                send_sem=send_sems.at[j], recv_sem=recv_sems.at[j], device_id=(px, py, c), device_id_type=MESH_ID))
        for cp in copies:
            cp.start()
        for cp in copies:
            cp.wait_recv()
        for cp in copies:
            cp.wait_send()

    return pl.pallas_call(
        body, name=name,
        out_shape=jax.ShapeDtypeStruct((3, r, n), partial.dtype),
        in_specs=[ANY], out_specs=ANY,
        scratch_shapes=[pltpu.SemaphoreType.DMA((3,)), pltpu.SemaphoreType.DMA((3,))],
    )(partial)


ROW_TILE = 1024


def add_sibling_parts(parts, received, core, name):
    _, r, n = parts.shape

    def body(c_ref, a_ref, b_ref, o_ref):
        o_ref[...] = a_ref[...] + b_ref[...]

    return pl.pallas_call(
        body, name=name,
        grid_spec=pltpu.PrefetchScalarGridSpec(
            num_scalar_prefetch=1, grid=(4, r // ROW_TILE),
            in_specs=[pl.BlockSpec((1, ROW_TILE, n), lambda i, j, c_ref: (2 * i + c_ref[0], j, 0)),
                      pl.BlockSpec((1, ROW_TILE, n), lambda i, j, c_ref: (i, j, 0))],
            out_specs=pl.BlockSpec((1, ROW_TILE, n), lambda i, j, c_ref: (i, j, 0))),
        out_shape=jax.ShapeDtypeStruct((4, r, n), parts.dtype),
        compiler_params=_cparams("parallel", "parallel"),
    )(core, parts, received)


def _adamw(w, g, m, v):
    m = ADAM_B1 * m + (1.0 - ADAM_B1) * g
    v = ADAM_B2 * v + (1.0 - ADAM_B2) * jnp.square(g)
    m_hat = m / (1.0 - ADAM_B1 ** ADAM_STEP)
    v_hat = v / (1.0 - ADAM_B2 ** ADAM_STEP)
    delta = -ADAM_LR * (m_hat / (jnp.sqrt(v_hat) + ADAM_EPS) + ADAM_WD * w)
    return delta, m, v


def adamw_sharded(partial, received, chip, w, m, v, name):
    _, r, n = partial.shape

    def body(k_ref, p_ref, r_ref, w_ref, m_ref, v_ref, g_out, d_out, m_out, v_out):
        g = p_ref[0] + r_ref[0]
        g = g + r_ref[1]
        g = g + r_ref[2]
        delta, mn, vn = _adamw(w_ref[...], g, m_ref[...], v_ref[...])
        g_out[...] = g
        d_out[...] = delta
        m_out[...] = mn
        v_out[...] = vn

    flat = pl.BlockSpec((ROW_TILE, n), lambda j, k_ref: (j, 0))
    return pl.pallas_call(
        body, name=name,
        grid_spec=pltpu.PrefetchScalarGridSpec(
            num_scalar_prefetch=1, grid=(r // ROW_TILE,),
            in_specs=[pl.BlockSpec((1, ROW_TILE, n), lambda j, k_ref: (k_ref[0], j, 0)),
                      pl.BlockSpec((3, ROW_TILE, n), lambda j, k_ref: (0, j, 0)), flat, flat, flat],
            out_specs=[flat] * 4),
        out_shape=[jax.ShapeDtypeStruct((r, n), f32)] * 4,
        compiler_params=_cparams("parallel"),
    )(chip, partial, received, w, m, v)


def adamw_replicated(gathered, w, m, v, name):
    _, r, n = gathered.shape

    def body(g_ref, w_ref, m_ref, v_ref, g_out, d_out, m_out, v_out):
        g = g_ref[0]
        for k in range(1, N_DEV):
            g = g + g_ref[k]
        delta, mn, vn = _adamw(w_ref[...], g, m_ref[...], v_ref[...])
        g_out[...] = g
        d_out[...] = delta
        m_out[...] = mn
        v_out[...] = vn

    return pl.pallas_call(
        body, name=name,
        out_shape=[jax.ShapeDtypeStruct((r, n), f32)] * 4,
        compiler_params=pltpu.CompilerParams(vmem_limit_bytes=VMEM_LIMIT),
    )(gathered, w, m, v)


W_NAMES = ['ev_w_in', 'ev_lru_conv_w', 'ev_lru_conv_b', 'ev_lru_gate_a_w', 'ev_lru_gate_a_b', 'ev_lru_gate_x_w',
           'ev_lru_gate_x_b', 'ev_lru_lambda', 'ev_w_out', 'od_w_in', 'od_ssm_conv_w', 'od_ssm_conv_b',
           'od_ssm_dt_bias', 'od_ssm_a_log', 'od_ssm_d', 'od_ssm_norm', 'od_cm_conv_w', 'od_cm_conv_b', 'od_cm_ln_g',
           'od_cm_ln_b', 'od_w_out', 'norm_mix_pre', 'norm_mix_post', 'norm_mlp_pre', 'norm_mlp_post', 'norm_ple',
           'mlp_w1', 'mlp_w2', 'ple_w_proj', 'ple_w_gate']
BIG_SHARDED = {'ev_w_in': 2, 'ev_w_out': 1, 'od_w_in': 2, 'od_w_out': 1, 'mlp_w1': 2, 'mlp_w2': 1, 'ple_w_proj': 2,
               'ple_w_gate': 1}
SMALL_SHARDED = {'ev_lru_conv_w': 2, 'od_ssm_conv_w': 2, 'od_ssm_conv_b': 1, 'od_ssm_norm': 1, 'od_cm_conv_w': 2,
                 'od_cm_conv_b': 1, 'od_cm_ln_g': 1, 'od_cm_ln_b': 1}
SHARDED = {**BIG_SHARDED, **SMALL_SHARDED}
REPLICATED = [n for n in W_NAMES if n not in SHARDED]


def _round_up(n, k):
    return -(-n // k) * k


def _pack_rows(flat, rows_multiple):
    n = flat.shape[0]
    total = _round_up(n, LANES * rows_multiple)
    return jnp.pad(flat, (0, total - n)).reshape(-1, LANES)


def _unpack(flat, shapes):
    out, off = {}, 0
    for name, shape in shapes.items():
        size = math.prod(shape)
        out[name] = flat[off:off + size].reshape(shape)
        off += size
    return out


def _gather_weights(w):
    big = jnp.concatenate([w[k].astype(MXU_DTYPE).reshape(-1) for k in BIG_SHARDED])
    small = jnp.concatenate([w[k].reshape(-1) for k in SMALL_SHARDED])
    terms, rest = [], small
    for _ in range(3):
        term = rest.astype(MXU_DTYPE)
        terms.append(term)
        rest = rest - term.astype(f32)
    n_big, n_small = big.shape[0], small.shape[0]
    buf = _pack_rows(jnp.concatenate([big] + terms), 16)
    gathered = all_gather(buf, name="gather_weights").reshape(N_DEV, -1)
    full = {k: w[k] for k in REPLICATED}
    pieces = {k: [] for k in SHARDED}
    for d in range(N_DEV):
        row = gathered[d]
        for k, v in _unpack(row[:n_big], {k: w[k].shape for k in BIG_SHARDED}).items():
            pieces[k].append(v)
        t0, t1, t2 = [row[n_big + i * n_small:n_big + (i + 1) * n_small].astype(f32) for i in range(3)]
        for k, v in _unpack(t0 + t1 + t2, {k: w[k].shape for k in SMALL_SHARDED}).items():
            pieces[k].append(v)
    for k, axis in SHARDED.items():
        full[k] = jnp.concatenate(pieces[k], axis=axis)
    return full


def _pack_sharded(tree):
    return _pack_rows(jnp.concatenate([tree[k].astype(f32).reshape(-1) for k in SHARDED]), ROW_TILE)


def _pack_replicated(tree):
    return _pack_rows(jnp.concatenate([tree[k].astype(f32).reshape(-1) for k in REPLICATED]), 8)


def kernel(x, p, ev_w_in, ev_lru_conv_w, ev_lru_conv_b, ev_lru_gate_a_w, ev_lru_gate_a_b, ev_lru_gate_x_w, ev_lru_gate_x_b, ev_lru_lambda, ev_w_out, od_w_in, od_ssm_conv_w, od_ssm_conv_b, od_ssm_dt_bias, od_ssm_a_log, od_ssm_d, od_ssm_norm, od_cm_conv_w, od_cm_conv_b, od_cm_ln_g, od_cm_ln_b, od_w_out, norm_mix_pre, norm_mix_post, norm_mlp_pre, norm_mlp_post, norm_ple, mlp_w1, mlp_w2, ple_w_proj, ple_w_gate, loss_target, m_ev_w_in, m_ev_lru_conv_w, m_ev_lru_conv_b, m_ev_lru_gate_a_w, m_ev_lru_gate_a_b, m_ev_lru_gate_x_w, m_ev_lru_gate_x_b, m_ev_lru_lambda, m_ev_w_out, m_od_w_in, m_od_ssm_conv_w, m_od_ssm_conv_b, m_od_ssm_dt_bias, m_od_ssm_a_log, m_od_ssm_d, m_od_ssm_norm, m_od_cm_conv_w, m_od_cm_conv_b, m_od_cm_ln_g, m_od_cm_ln_b, m_od_w_out, m_norm_mix_pre, m_norm_mix_post, m_norm_mlp_pre, m_norm_mlp_post, m_norm_ple, m_mlp_w1, m_mlp_w2, m_ple_w_proj, m_ple_w_gate, v_ev_w_in, v_ev_lru_conv_w, v_ev_lru_conv_b, v_ev_lru_gate_a_w, v_ev_lru_gate_a_b, v_ev_lru_gate_x_w, v_ev_lru_gate_x_b, v_ev_lru_lambda, v_ev_w_out, v_od_w_in, v_od_ssm_conv_w, v_od_ssm_conv_b, v_od_ssm_dt_bias, v_od_ssm_a_log, v_od_ssm_d, v_od_ssm_norm, v_od_cm_conv_w, v_od_cm_conv_b, v_od_cm_ln_g, v_od_cm_ln_b, v_od_w_out, v_norm_mix_pre, v_norm_mix_post, v_norm_mlp_pre, v_norm_mlp_post, v_norm_ple, v_mlp_w1, v_mlp_w2, v_ple_w_proj, v_ple_w_gate):
    ws = [ev_w_in, ev_lru_conv_w, ev_lru_conv_b, ev_lru_gate_a_w, ev_lru_gate_a_b, ev_lru_gate_x_w, ev_lru_gate_x_b, ev_lru_lambda, ev_w_out, od_w_in, od_ssm_conv_w, od_ssm_conv_b, od_ssm_dt_bias, od_ssm_a_log, od_ssm_d, od_ssm_norm, od_cm_conv_w, od_cm_conv_b, od_cm_ln_g, od_cm_ln_b, od_w_out, norm_mix_pre, norm_mix_post, norm_mlp_pre, norm_mlp_post, norm_ple, mlp_w1, mlp_w2, ple_w_proj, ple_w_gate]
    ms = [m_ev_w_in, m_ev_lru_conv_w, m_ev_lru_conv_b, m_ev_lru_gate_a_w, m_ev_lru_gate_a_b, m_ev_lru_gate_x_w, m_ev_lru_gate_x_b, m_ev_lru_lambda, m_ev_w_out, m_od_w_in, m_od_ssm_conv_w, m_od_ssm_conv_b, m_od_ssm_dt_bias, m_od_ssm_a_log, m_od_ssm_d, m_od_ssm_norm, m_od_cm_conv_w, m_od_cm_conv_b, m_od_cm_ln_g, m_od_cm_ln_b, m_od_w_out, m_norm_mix_pre, m_norm_mix_post, m_norm_mlp_pre, m_norm_mlp_post, m_norm_ple, m_mlp_w1, m_mlp_w2, m_ple_w_proj, m_ple_w_gate]
    vs = [v_ev_w_in, v_ev_lru_conv_w, v_ev_lru_conv_b, v_ev_lru_gate_a_w, v_ev_lru_gate_a_b, v_ev_lru_gate_x_w, v_ev_lru_gate_x_b, v_ev_lru_lambda, v_ev_w_out, v_od_w_in, v_od_ssm_conv_w, v_od_ssm_conv_b, v_od_ssm_dt_bias, v_od_ssm_a_log, v_od_ssm_d, v_od_ssm_norm, v_od_cm_conv_w, v_od_cm_conv_b, v_od_cm_ln_g, v_od_cm_ln_b, v_od_w_out, v_norm_mix_pre, v_norm_mix_post, v_norm_mlp_pre, v_norm_mlp_post, v_norm_ple, v_mlp_w1, v_mlp_w2, v_ple_w_proj, v_ple_w_gate]
    w = dict(zip(W_NAMES, ws))
    m = dict(zip(W_NAMES, ms))
    v = dict(zip(W_NAMES, vs))
    full = _gather_weights(w)
    loss_local, grad_x, grads = local_step(x, p, loss_target, full)
    loss = lax.psum(loss_local, ("x", "y", "c"))
    return (loss, grad_x, *_reduce_and_update(grads, w, m, v))


def _reduce_and_update(grads, w, m, v):
    mx, my, mc = _mesh_pos()

    parts = []
    for k, axis in SHARDED.items():
        g = grads[k]
        shard = g.shape[axis] // N_DEV
        g = g.reshape(g.shape[:axis] + (N_DEV, shard) + g.shape[axis + 1:])
        parts.append(jnp.moveaxis(g, axis, 0).reshape(N_DEV, -1))
    parts = jnp.concatenate(parts, axis=1)
    rows = _round_up(parts.shape[1], LANES * ROW_TILE) // LANES
    parts = jnp.pad(parts, ((0, 0), (0, rows * LANES - parts.shape[1]))).reshape(N_DEV, rows, LANES)
    from_sibling = scatter_to_sibling(parts, name="scatter_sibling")
    core = jnp.reshape(mc, (1,)).astype(jnp.int32)
    chip_sums = add_sibling_parts(parts, from_sibling, core, name="add_sibling")
    from_chips = scatter_to_chips(chip_sums, name="scatter_chips")
    chip = jnp.reshape(2 * mx + my, (1,)).astype(jnp.int32)
    outs = adamw_sharded(chip_sums, from_chips, chip, _pack_sharded(w), _pack_sharded(m), _pack_sharded(v),
                         name="adamw_sharded")
    shard_shapes = {k: w[k].shape for k in SHARDED}
    g_sh, d_sh, m_sh, v_sh = [_unpack(o.reshape(-1), shard_shapes) for o in outs]

    rep_parts = all_gather(_pack_replicated(grads), name="gather_replicated_grads")
    outs = adamw_replicated(rep_parts, _pack_replicated(w), _pack_replicated(m), _pack_replicated(v),
                            name="adamw_replicated")
    rep_shapes = {k: w[k].shape for k in REPLICATED}
    g_rp, d_rp, m_rp, v_rp = [_unpack(o.reshape(-1), rep_shapes) for o in outs]

    pick = lambda sh, rp: [sh[k] if k in SHARDED else rp[k] for k in W_NAMES]
    return [*pick(g_sh, g_rp), *pick(d_sh, d_rp), *pick(m_sh, m_rp), *pick(v_sh, v_rp)]
```

```python
import functools
import math

import jax
import jax.numpy as jnp
from jax import lax
from jax.experimental import pallas as pl
from jax.experimental.pallas import tpu as pltpu

f32 = jnp.float32
bf16 = jnp.bfloat16
MXU_DTYPE = jnp.bfloat16

D_MODEL = 1024
EPS = 1e-6
LRU_WIDTH = 512
LRU_HEADS = 8
LRU_CONV = 4
LRU_C = 8.0
SB_WIDTH = 512
SB_HEAD_DIM = 64
SSM_WIDTH = 1024
SSM_HEADS = 16
SSM_HEAD_DIM = 64
SSM_GROUPS = 2
SSM_STATE = 128
SSM_CONV = 4
SSM_CHUNK = 128
SSM_XBC = SSM_WIDTH + 2 * SSM_GROUPS * SSM_STATE
CONF_WIDTH = 512
CONF_KERNEL = 31
MLP_HIDDEN = 4096
PLE_DIM = 256
LANES = 128
N_DEV = 8

ADAM_LR = 0.001
ADAM_B1 = 0.9
ADAM_B2 = 0.999
ADAM_EPS = 1e-08
ADAM_WD = 0.01
ADAM_STEP = 10

VMEM_LIMIT = 56 * 1024 * 1024


def _cparams(*sem):
    return pltpu.CompilerParams(dimension_semantics=sem, vmem_limit_bytes=VMEM_LIMIT)


def _mm(a, b):
    return jnp.dot(a.astype(MXU_DTYPE), b.astype(MXU_DTYPE), preferred_element_type=f32)


def _mm_nt(a, b):
    return lax.dot_general(a.astype(MXU_DTYPE), b.astype(MXU_DTYPE), (((1,), (1,)), ((), ())),
                           preferred_element_type=f32)


def _mm_tn(a, b):
    return lax.dot_general(a.astype(MXU_DTYPE), b.astype(MXU_DTYPE), (((0,), (0,)), ((), ())),
                           preferred_element_type=f32)


def _mm_exact(a, b):
    return jnp.dot(a, b, preferred_element_type=f32, precision=lax.Precision.HIGHEST)


@jax.custom_vjp
def dmm(a, b):
    return _mm(a, b)


def _dmm_fwd(a, b):
    return _mm(a, b), (a, b)


def _dmm_bwd(res, g):
    a, b = res
    return _mm_nt(g, b), _mm_tn(a, g)


dmm.defvjp(_dmm_fwd, _dmm_bwd)


@jax.custom_vjp
def dmm_nt(a, b):
    return _mm_nt(a, b)


def _dmm_nt_fwd(a, b):
    return _mm_nt(a, b), (a, b)


def _dmm_nt_bwd(res, g):
    a, b = res
    return _mm(g, b), _mm_tn(g, a)


dmm_nt.defvjp(_dmm_nt_fwd, _dmm_nt_bwd)


@jax.custom_vjp
def dmm_tn(a, b):
    return _mm_tn(a, b)


def _dmm_tn_fwd(a, b):
    return _mm_tn(a, b), (a, b)


def _dmm_tn_bwd(res, g):
    a, b = res
    return _mm_nt(b, g), _mm(a, g)


dmm_tn.defvjp(_dmm_tn_fwd, _dmm_tn_bwd)


def _rms(x, g):
    r = lax.rsqrt(jnp.mean(x * x, axis=-1, keepdims=True) + EPS)
    return x * r * g


def _rms_bwd(dy, x, g):
    r = lax.rsqrt(jnp.mean(x * x, axis=-1, keepdims=True) + EPS)
    dyg = dy * g
    dx = r * dyg - x * (r * r * r * jnp.mean(dyg * x, axis=-1, keepdims=True))
    return dx, dy * x * r


def _tok(tm, n):
    return pl.BlockSpec((tm, n), lambda i: (i, 0))


def _whole(shape):
    nd = len(shape)
    return pl.BlockSpec(tuple(shape), lambda i: (0,) * nd)


def _acc_rows(ref, val):
    s = jnp.sum(val, axis=0, keepdims=True)

    @pl.when(pl.program_id(0) == 0)
    def _():
        ref[...] = s

    @pl.when(pl.program_id(0) != 0)
    def _():
        ref[...] += s


TOKEN_TILE = 256


def norm_matmul(h, g, ws, out_dtypes, name):
    t, d = h.shape
    tm = TOKEN_TILE
    nw = len(ws)

    def body(h_ref, g_ref, *refs):
        hn = _rms(h_ref[...], g_ref[...]).astype(MXU_DTYPE)
        for w_ref, o_ref in zip(refs[:nw], refs[nw:]):
            o_ref[...] = jnp.dot(hn, w_ref[...], preferred_element_type=f32).astype(o_ref.dtype)

    return pl.pallas_call(
        body, name=name, grid=(t // tm,),
        in_specs=[_tok(tm, d), _whole(g.shape)] + [_whole(w.shape) for w in ws],
        out_specs=[_tok(tm, w.shape[1]) for w in ws],
        out_shape=[jax.ShapeDtypeStruct((t, w.shape[1]), dt) for w, dt in zip(ws, out_dtypes)],
        compiler_params=_cparams("parallel"),
    )(h, g, *ws)


def matmul_residual_norm(xs, ws, h, g, name, relu2=False):
    t, d = h.shape
    tm = TOKEN_TILE
    nx = len(xs)

    def body(*refs):
        x_refs, w_refs = refs[:nx], refs[nx:2 * nx]
        h_ref, g_ref, ho_ref, m_ref = refs[2 * nx:]
        m = None
        for x_ref, w_ref in zip(x_refs, w_refs):
            x = x_ref[...]
            if relu2:
                x = jnp.square(jnp.maximum(x.astype(f32), 0.0))
            part = jnp.dot(x.astype(MXU_DTYPE), w_ref[...], preferred_element_type=f32)
            m = part if m is None else m + part
        m_ref[...] = m
        ho_ref[...] = h_ref[...] + _rms(m, g_ref[...])

    return pl.pallas_call(
        body, name=name, grid=(t // tm,),
        in_specs=[_tok(tm, x.shape[1]) for x in xs] + [_whole(w.shape) for w in ws] + [_tok(tm, d), _whole(g.shape)],
        out_specs=[_tok(tm, d), _tok(tm, d)],
        out_shape=[jax.ShapeDtypeStruct((t, d), f32), jax.ShapeDtypeStruct((t, d), f32)],
        compiler_params=_cparams("parallel"),
    )(*xs, *ws, h, g)


def ple_forward(h, p, w_gate, w_proj, g, name):
    t, d = h.shape
    tm = TOKEN_TILE

    def body(h_ref, p_ref, wg_ref, wp_ref, g_ref, ho_ref, gl_ref, emb_ref):
        hh = h_ref[...]
        gl = jnp.dot(hh.astype(MXU_DTYPE), wg_ref[...], preferred_element_type=f32)
        emb = jnp.dot(p_ref[...].astype(MXU_DTYPE), wp_ref[...], preferred_element_type=f32)
        gl_ref[...] = gl
        emb_ref[...] = emb
        ho_ref[...] = hh + _rms(jax.nn.sigmoid(gl) * emb, g_ref[...])

    return pl.pallas_call(
        body, name=name, grid=(t // tm,),
        in_specs=[_tok(tm, d), _tok(tm, p.shape[1]), _whole(w_gate.shape), _whole(w_proj.shape), _whole(g.shape)],
        out_specs=[_tok(tm, d)] * 3,
        out_shape=[jax.ShapeDtypeStruct((t, d), f32)] * 3,
        compiler_params=_cparams("parallel"),
    )(h, p, w_gate, w_proj, g)


def loss_and_grad(h, target, name):
    t, d = h.shape
    tm = TOKEN_TILE

    def body(h_ref, t_ref, l_ref, dh_ref):
        e = h_ref[...] - t_ref[...]
        dh_ref[...] = e * (1.0 / d)
        part = jnp.sum(jnp.sum(e * e, axis=1, keepdims=True), axis=0, keepdims=True) * (0.5 / d)
        _acc_rows(l_ref, jnp.broadcast_to(part, (1, LANES)))

    return pl.pallas_call(
        body, name=name, grid=(t // tm,),
        in_specs=[_tok(tm, d), _tok(tm, d)],
        out_specs=[_whole((1, LANES)), _tok(tm, d)],
        out_shape=[jax.ShapeDtypeStruct((1, LANES), f32), jax.ShapeDtypeStruct((t, d), f32)],
        compiler_params=_cparams("arbitrary"),
    )(h, target)


def bwd_through_norm_in(dh, gs, wts, h, g, name):
    t, d = h.shape
    tm = TOKEN_TILE
    ng = len(gs)

    def body(*refs):
        dh_ref = refs[0]
        g_refs, w_refs = refs[1:1 + ng], refs[1 + ng:1 + 2 * ng]
        h_ref, gain_ref, dho_ref, dg_ref = refs[1 + 2 * ng:]
        dhn = None
        for g_ref, w_ref in zip(g_refs, w_refs):
            part = jnp.dot(g_ref[...].astype(MXU_DTYPE), w_ref[...], preferred_element_type=f32)
            dhn = part if dhn is None else dhn + part
        dx, dgr = _rms_bwd(dhn, h_ref[...], gain_ref[...])
        dho_ref[...] = dh_ref[...] + dx
        _acc_rows(dg_ref, dgr)

    return pl.pallas_call(
        body, name=name, grid=(t // tm,),
        in_specs=[_tok(tm, d)] + [_tok(tm, x.shape[1]) for x in gs] + [_whole(w.shape) for w in wts]
        + [_tok(tm, d), _whole(g.shape)],
        out_specs=[_tok(tm, d), _whole((1, d))],
        out_shape=[jax.ShapeDtypeStruct((t, d), f32), jax.ShapeDtypeStruct((1, d), f32)],
        compiler_params=_cparams("arbitrary"),
    )(dh, *gs, *wts, h, g)


def bwd_through_norm_out(dh, n, g, wts, out_dtypes, name, relu2_of=None):
    t, d = n.shape
    tm = TOKEN_TILE
    nw = len(wts)
    has_a = relu2_of is not None

    def body(*refs):
        dh_ref, n_ref, gain_ref = refs[:3]
        w_refs = refs[3:3 + nw]
        rest = refs[3 + nw:]
        if has_a:
            a_ref, rest = rest[0], rest[1:]
        dn_ref, dx_refs, dg_ref = rest[0], rest[1:1 + nw], rest[1 + nw]
        dn, dgr = _rms_bwd(dh_ref[...], n_ref[...], gain_ref[...])
        dnb = dn.astype(MXU_DTYPE)
        dn_ref[...] = dnb.astype(dn_ref.dtype)
        for w_ref, dx_ref in zip(w_refs, dx_refs):
            dx = jnp.dot(dnb, w_ref[...], preferred_element_type=f32)
            if has_a:
                dx = dx * (2.0 * jnp.maximum(a_ref[...].astype(f32), 0.0))
            dx_ref[...] = dx.astype(dx_ref.dtype)
        _acc_rows(dg_ref, dgr)

    ins = [dh, n, g, *wts] + ([relu2_of] if has_a else [])
    in_specs = [_tok(tm, d), _tok(tm, d), _whole(g.shape)] + [_whole(w.shape) for w in wts]
    if has_a:
        in_specs.append(_tok(tm, relu2_of.shape[1]))
    outs = pl.pallas_call(
        body, name=name, grid=(t // tm,),
        in_specs=in_specs,
        out_specs=[_tok(tm, d)] + [_tok(tm, w.shape[1]) for w in wts] + [_whole((1, d))],
        out_shape=[jax.ShapeDtypeStruct((t, d), MXU_DTYPE)]
        + [jax.ShapeDtypeStruct((t, w.shape[1]), dt) for w, dt in zip(wts, out_dtypes)]
        + [jax.ShapeDtypeStruct((1, d), f32)],
        compiler_params=_cparams("arbitrary"),
    )(*ins)
    return outs[0], list(outs[1:1 + nw]), outs[1 + nw]


def ple_backward(dh3, h2, gl, emb, g, w_gate_t, name):
    t, d = h2.shape
    tm = TOKEN_TILE

    def body(dh_ref, gl_ref, emb_ref, gain_ref, wt_ref, dho_ref, dgl_ref, demb_ref, dg_ref):
        gate = jax.nn.sigmoid(gl_ref[...])
        emb = emb_ref[...]
        dge, dgr = _rms_bwd(dh_ref[...], gate * emb, gain_ref[...])
        demb_ref[...] = (dge * gate).astype(demb_ref.dtype)
        dgl = (dge * emb * gate * (1.0 - gate)).astype(MXU_DTYPE)
        dgl_ref[...] = dgl.astype(dgl_ref.dtype)
        dho_ref[...] = dh_ref[...] + jnp.dot(dgl, wt_ref[...], preferred_element_type=f32)
        _acc_rows(dg_ref, dgr)

    return pl.pallas_call(
        body, name=name, grid=(t // tm,),
        in_specs=[_tok(tm, d), _tok(tm, d), _tok(tm, d), _whole(g.shape), _whole(w_gate_t.shape)],
        out_specs=[_tok(tm, d), _tok(tm, d), _tok(tm, d), _whole((1, d))],
        out_shape=[jax.ShapeDtypeStruct((t, d), f32), jax.ShapeDtypeStruct((t, d), MXU_DTYPE),
                   jax.ShapeDtypeStruct((t, d), MXU_DTYPE), jax.ShapeDtypeStruct((1, d), f32)],
        compiler_params=_cparams("arbitrary"),
    )(dh3, gl, emb, g, w_gate_t)


def _largest_tile(n, cap):
    if n <= cap:
        return n
    return max(c for c in range(LANES, cap + 1, LANES) if n % c == 0)


def weight_grad(x, gout, name, prologue="none", gain=None):
    t, k = x.shape
    n = gout.shape[1]
    tt = 512
    tn = _largest_tile(n, 1024)
    tk = k if prologue == "rms" else _largest_tile(k, 1024)
    has_gain = prologue == "rms"

    def body(*refs):
        if has_gain:
            x_ref, gain_ref, g_ref, o_ref = refs
        else:
            x_ref, g_ref, o_ref = refs
        x = x_ref[...].astype(f32)
        if prologue == "relu2":
            x = jnp.square(jnp.maximum(x, 0.0))
        elif prologue == "rms":
            x = _rms(x, gain_ref[...])
        part = _mm_tn(x, g_ref[...])

        @pl.when(pl.program_id(2) == 0)
        def _():
            o_ref[...] = part

        @pl.when(pl.program_id(2) != 0)
        def _():
            o_ref[...] += part

    in_specs = [pl.BlockSpec((tt, tk), lambda i, j, s: (s, i))]
    ins = [x]
    if has_gain:
        in_specs.append(pl.BlockSpec(gain.shape, lambda i, j, s: (0, 0)))
        ins.append(gain)
    in_specs.append(pl.BlockSpec((tt, tn), lambda i, j, s: (s, j)))
    ins.append(gout)
    return pl.pallas_call(
        body, name=name, grid=(k // tk, n // tn, t // tt),
        in_specs=in_specs,
        out_specs=pl.BlockSpec((tk, tn), lambda i, j, s: (i, j)),
        out_shape=jax.ShapeDtypeStruct((k, n), f32),
        compiler_params=_cparams("parallel", "parallel", "arbitrary"),
    )(*ins)


SEQ_TILE = 256
HALO = 8


def _first_step():
    return jnp.logical_and(pl.program_id(0) == 0, pl.program_id(1) == 0)


def _accum(ref, val, first):
    @pl.when(first)
    def _():
        ref[...] = val

    @pl.when(jnp.logical_not(first))
    def _():
        ref[...] += val


def _softplus(x):
    return jnp.maximum(x, 0.0) + jnp.log1p(jnp.exp(-jnp.abs(x)))


def _neg_expm1(z):
    series = -z * (1.0 + z * (0.5 + z * (1.0 / 6.0 + z * (1.0 / 24.0 + z * (1.0 / 120.0)))))
    return jnp.where(z > -0.05, series, 1.0 - jnp.exp(z))


def _lru_gates(xc, ga, gab, gx, gxb, lam):
    r = jax.nn.sigmoid(dmm(xc, ga) + gab)
    i = jax.nn.sigmoid(dmm(xc, gx) + gxb)
    log_a = -LRU_C * r * _softplus(-lam)
    a = jnp.exp(log_a)
    u = jnp.sqrt(_neg_expm1(2.0 * log_a)) * (i * xc)
    return a, u


def _scan_down(a, u):
    n = a.shape[0]
    rows = lax.broadcasted_iota(jnp.int32, a.shape, 0)
    d = 1
    while d < n:
        keep = rows >= d
        a_s = jnp.where(keep, pltpu.roll(a, d, 0), 1.0)
        u_s = jnp.where(keep, pltpu.roll(u, d, 0), 0.0)
        u = a * u_s + u
        a = a * a_s
        d *= 2
    return a, u


def _scan_up(b, g):
    n = b.shape[0]
    rows = lax.broadcasted_iota(jnp.int32, b.shape, 0)
    d = 1
    while d < n:
        keep = rows < n - d
        b_s = jnp.where(keep, pltpu.roll(b, n - d, 0), 1.0)
        g_s = jnp.where(keep, pltpu.roll(g, n - d, 0), 0.0)
        g = g + b * g_s
        b = b * b_s
        d *= 2
    return g


def _seq_specs(ts, c, nt, reverse=False):
    per = ts // HALO

    def jj(j):
        return (nt - 1 - j) if reverse else j

    tile = pl.BlockSpec((1, ts, c), lambda b, j: (b, jj(j), 0))
    before = pl.BlockSpec((1, HALO, c), lambda b, j: (b, jnp.maximum(jj(j) * per - 1, 0), 0))
    after = pl.BlockSpec((1, HALO, c), lambda b, j: (b, jnp.minimum((jj(j) + 1) * per, nt * per - 1), 0))
    return tile, before, after


def _const2(shape):
    nd = len(shape)
    return pl.BlockSpec(tuple(shape), lambda b, j: (0,) * nd)


def lru_forward(xpre, gate, cw, cb, ga, gab, gx, gxb, lam, name):
    nb, ns, w = xpre.shape
    ts = SEQ_TILE
    nt = ns // ts
    tile, _, _ = _seq_specs(ts, w, nt)

    def body(xp_ref, gt_ref, cw_ref, cb_ref, ga_ref, gab_ref, gx_ref, gxb_ref, lam_ref,
             y_ref, xc_ref, hs_ref, xin, hcar):
        @pl.when(pl.program_id(1) == 0)
        def _():
            xin[0:HALO, :] = jnp.zeros((HALO, w), f32)
            hcar[...] = jnp.zeros_like(hcar)

        xin[HALO:HALO + ts, :] = xp_ref[0]
        xc = jnp.broadcast_to(cb_ref[...], (ts, w))
        for k in range(LRU_CONV):
            xc = xc + cw_ref[k:k + 1, :] * xin[pl.ds(HALO - LRU_CONV + 1 + k, ts), :]
        xin[0:HALO, :] = xin[ts:ts + HALO, :]
        a, u = _lru_gates(xc, ga_ref[...], gab_ref[...], gx_ref[...], gxb_ref[...], lam_ref[...])
        acum, h = _scan_down(a, u)
        h = h + acum * hcar[0:1, :]
        hcar[0:1, :] = h[ts - 1:ts, :]
        xc_ref[0] = xc
        hs_ref[0] = h
        y_ref[0] = (h * jax.nn.gelu(gt_ref[0])).astype(y_ref.dtype)

    params = [cw, cb, ga, gab, gx, gxb, lam]
    return pl.pallas_call(
        body, name=name, grid=(nb, nt),
        in_specs=[tile, tile] + [_const2(p.shape) for p in params],
        out_specs=[tile, tile, tile],
        out_shape=[jax.ShapeDtypeStruct((nb, ns, w), MXU_DTYPE), jax.ShapeDtypeStruct((nb, ns, w), f32),
                   jax.ShapeDtypeStruct((nb, ns, w), f32)],
        scratch_shapes=[pltpu.VMEM((ts + HALO, w), f32), pltpu.VMEM((HALO, w), f32)],
        compiler_params=_cparams("arbitrary", "arbitrary"),
    )(xpre, gate, *params)


def lru_backward(dy, xpre, gate, xc, hs, cw, cb, ga, gab, gx, gxb, lam, name):
    nb, ns, w = xpre.shape
    ts = SEQ_TILE
    nt = ns // ts
    tile, before, _ = _seq_specs(ts, w, nt, reverse=True)

    def body(dy_ref, xp_ref, xpb_ref, gt_ref, xc_ref, hs_ref, hsb_ref,
             cw_ref, cb_ref, ga_ref, gab_ref, gx_ref, gxb_ref, lam_ref,
             dxp_ref, dgt_ref, dcw_ref, dcb_ref, dga_ref, dgab_ref, dgx_ref, dgxb_ref, dlam_ref,
             dxc_ext, gcar, xin):
        j = pl.program_id(1)
        first = _first_step()
        at_seq_start = j == nt - 1

        @pl.when(j == 0)
        def _():
            dxc_ext[ts:ts + HALO, :] = jnp.zeros((HALO, w), f32)
            gcar[...] = jnp.zeros_like(gcar)

        gt = gt_ref[0]
        h = hs_ref[0]
        dyv = dy_ref[0].astype(f32)
        gl, gelu_vjp = jax.vjp(jax.nn.gelu, gt)
        dgt_ref[0] = gelu_vjp(dyv * h)[0].astype(dgt_ref.dtype)
        dh = dyv * gl

        (a, _), gates_vjp = jax.vjp(_lru_gates, xc_ref[0], ga_ref[...], gab_ref[...], gx_ref[...], gxb_ref[...],
                                    lam_ref[...])
        rows = lax.broadcasted_iota(jnp.int32, (ts, w), 0)
        dh = dh + jnp.where(rows == ts - 1, gcar[0:1, :], 0.0)
        b = pltpu.roll(a, ts - 1, 0)
        g = _scan_up(b, dh)
        gcar[0:1, :] = a[0:1, :] * g[0:1, :]
        hprev_row = jnp.where(at_seq_start, 0.0, hsb_ref[0][HALO - 1:HALO, :])
        hprev = jnp.where(rows == 0, hprev_row, pltpu.roll(h, 1, 0))
        dxc, dga, dgab, dgx, dgxb, dlam = gates_vjp((g * hprev, g))

        _accum(dga_ref, dga, first)
        _accum(dgx_ref, dgx, first)
        _accum(dgab_ref, dgab, first)
        _accum(dgxb_ref, dgxb, first)
        _accum(dlam_ref, dlam, first)
        _accum(dcb_ref, jnp.sum(dxc, axis=0, keepdims=True), first)

        dxc_ext[0:ts, :] = dxc
        dxp = jnp.zeros((ts, w), f32)
        for k in range(LRU_CONV):
            dxp = dxp + cw_ref[k:k + 1, :] * dxc_ext[pl.ds(LRU_CONV - 1 - k, ts), :]
        dxp_ref[0] = dxp.astype(dxp_ref.dtype)
        dxc_ext[ts:ts + HALO, :] = dxc[0:HALO, :]

        xin[0:HALO, :] = jnp.where(at_seq_start, 0.0, xpb_ref[0])
        xin[HALO:HALO + ts, :] = xp_ref[0]
        dcw_rows = [jnp.sum(dxc * xin[pl.ds(HALO - LRU_CONV + 1 + k, ts), :], axis=0, keepdims=True)
                    for k in range(LRU_CONV)]
        dcw_rows += [jnp.zeros((1, w), f32)] * (HALO - LRU_CONV)
        _accum(dcw_ref, jnp.concatenate(dcw_rows, axis=0), first)

    params = [cw, cb, ga, gab, gx, gxb, lam]
    pshape = lambda p: jax.ShapeDtypeStruct(p.shape, f32)
    outs = pl.pallas_call(
        body, name=name, grid=(nb, nt),
        in_specs=[tile, tile, before, tile, tile, tile, before] + [_const2(p.shape) for p in params],
        out_specs=[tile, tile, _const2((HALO, w))] + [_const2(p.shape) for p in params[1:]],
        out_shape=[jax.ShapeDtypeStruct((nb, ns, w), MXU_DTYPE), jax.ShapeDtypeStruct((nb, ns, w), MXU_DTYPE),
                   jax.ShapeDtypeStruct((HALO, w), f32)] + [pshape(p) for p in params[1:]],
        scratch_shapes=[pltpu.VMEM((ts + HALO, w), f32), pltpu.VMEM((HALO, w), f32),
                        pltpu.VMEM((ts + HALO, w), f32)],
        compiler_params=_cparams("arbitrary", "arbitrary"),
    )(dy, xpre, xpre, gate, xc, hs, hs, *params)
    return outs


SB_TILE = 256


def _split_dot(x, m):
    hi = x.astype(MXU_DTYPE)
    lo = (x - hi.astype(f32)).astype(MXU_DTYPE)
    return jnp.dot(hi, m, preferred_element_type=f32) + jnp.dot(lo, m, preferred_element_type=f32)


def _suffix_matrices(n):
    r = lax.broadcasted_iota(jnp.int32, (n, n), 0)
    c = lax.broadcasted_iota(jnp.int32, (n, n), 1)
    return (r > c).astype(MXU_DTYPE), (r >= c).astype(MXU_DTYPE)


def _sb_logits(qh, kb, strict):
    z = _mm_nt(qh, kb)
    ls = jnp.minimum(z, 0.0) - jnp.log(1.0 + jnp.exp(-jnp.abs(z)))
    lk = ls - z
    if strict is not None:
        lk = jnp.where(strict, lk, 0.0)
    return ls, lk


def _head_masked(x, dtype):
    lane = lax.broadcasted_iota(jnp.int32, x.shape, 1)
    return (jnp.where(lane < SB_HEAD_DIM, x, 0.0).astype(dtype), jnp.where(lane >= SB_HEAD_DIM, x, 0.0).astype(dtype))


def _stack_heads(dst, x, tq):
    x0, x1 = _head_masked(x, dst.dtype)
    for blk in range(dst.shape[0]):
        dst[blk, 0:tq, :] = x0[blk * tq:(blk + 1) * tq]
        dst[blk, tq:2 * tq, :] = x1[blk * tq:(blk + 1) * tq]


def _strict_mask(tq):
    rr = lax.broadcasted_iota(jnp.int32, (2 * tq, tq), 0)
    cc = lax.broadcasted_iota(jnp.int32, (2 * tq, tq), 1)
    return cc < jnp.where(rr >= tq, rr - tq, rr)


def _sb_specs(ns):
    npair = SB_WIDTH // LANES
    q = pl.BlockSpec((1, ns, LANES), lambda b, p: (b, 0, p))
    k = pl.BlockSpec((1, ns, LANES), lambda b, p: (b, 0, npair + p))
    v = pl.BlockSpec((1, ns, LANES), lambda b, p: (b, 0, 2 * npair + p))
    return q, k, v, npair


def sb_forward(qkv, name):
    nb, ns, _ = qkv.shape
    tq = SB_TILE
    nq = ns // tq
    qspec, kspec, vspec, npair = _sb_specs(ns)

    def body(q_ref, k_ref, v_ref, o_ref, qs, ks, vs, acc):
        scale = 1.0 / math.sqrt(SB_HEAD_DIM)
        _stack_heads(qs, q_ref[0] * scale, tq)
        ks[...] = k_ref[0].astype(MXU_DTYPE)
        _stack_heads(vs, v_ref[0], tq)
        mx, _ = _suffix_matrices(tq)
        strict = _strict_mask(tq)

        def step(q2, blks, r2, masked):
            kbs = [ks[pl.ds(pl.multiple_of(b * tq, tq), tq), :] for b in blks]
            lg = [_sb_logits(q2, kb, strict if masked else None) for kb in kbs]
            sums = [_split_dot(lk, mx) for _, lk in lg]
            total = None
            for (ls, lk), s, b in zip(lg, sums, blks):
                a = r2 + s
                w = jnp.exp(ls + a)
                if masked:
                    w = jnp.where(strict, w, 0.0)
                wb = w.astype(MXU_DTYPE)
                part = jnp.dot(jnp.concatenate([wb[:tq], wb[tq:]], axis=1), vs[b], preferred_element_type=f32)
                total = part if total is None else total + part
                r2 = a[:, 0:1] + lk[:, 0:1]
            acc[...] += total
            return r2

        def q_block(qi, carry):
            acc[...] = jnp.zeros_like(acc)
            q2 = qs[qi]
            r2 = step(q2, [qi], jnp.zeros((2 * tq, 1), f32), True)
            r2 = lax.fori_loop(0, lax.shift_right_logical(qi, 1),
                               lambda i, r: step(q2, [qi - 1 - 2 * i, qi - 2 - 2 * i], r, False), r2)
            lax.cond(jnp.bitwise_and(qi, 1) == 1, lambda r: step(q2, [0], r, False), lambda r: r, r2)
            o_ref[0, pl.ds(pl.multiple_of(qi * tq, tq), tq), :] = acc[...]
            return carry

        lax.fori_loop(0, nq, q_block, 0)

    return pl.pallas_call(
        body, name=name, grid=(nb, npair),
        in_specs=[qspec, kspec, vspec],
        out_specs=pl.BlockSpec((1, ns, LANES), lambda b, p: (b, 0, p)),
        out_shape=jax.ShapeDtypeStruct((nb, ns, SB_WIDTH), f32),
        scratch_shapes=[pltpu.VMEM((nq, 2 * tq, LANES), MXU_DTYPE), pltpu.VMEM((ns, LANES), MXU_DTYPE),
                        pltpu.VMEM((nq, 2 * tq, LANES), MXU_DTYPE), pltpu.VMEM((tq, LANES), f32)],
        compiler_params=_cparams("parallel", "parallel"),
    )(qkv, qkv, qkv)


def sb_backward(qkv, o, do, name):
    nb, ns, _ = qkv.shape
    tq = SB_TILE
    nq = ns // tq
    qspec, kspec, vspec, npair = _sb_specs(ns)
    ospec = pl.BlockSpec((1, ns, LANES), lambda b, p: (b, 0, p))

    def body(q_ref, k_ref, v_ref, o_ref, do_ref, dq_ref, dk_ref, dv_ref, qs, ks, kcat, vs, dos, dqacc, dkacc, dvacc):
        scale = 1.0 / math.sqrt(SB_HEAD_DIM)
        _stack_heads(qs, q_ref[0] * scale, tq)
        ks[...] = k_ref[0].astype(MXU_DTYPE)
        _stack_heads(kcat, k_ref[0], tq)
        vs[...] = v_ref[0].astype(MXU_DTYPE)
        _stack_heads(dos, do_ref[0].astype(f32), tq)
        dkacc[...] = jnp.zeros_like(dkacc)
        dvacc[...] = jnp.zeros_like(dvacc)
        mx, mi = _suffix_matrices(tq)
        strict = _strict_mask(tq)

        def step(q2, do2, dtot2, blks, carry, masked):
            r2, g2 = carry
            k0s = [pl.multiple_of(b * tq, tq) for b in blks]
            lg = [_sb_logits(q2, ks[pl.ds(k0, tq), :], strict if masked else None) for k0 in k0s]
            dws = [_mm_nt(do2, vs[pl.ds(k0, tq), :]) for k0 in k0s]
            sums = [_split_dot(lk, mx) for _, lk in lg]
            wbs, es = [], []
            for (ls, lk), s in zip(lg, sums):
                a = r2 + s
                w = jnp.exp(ls + a)
                if masked:
                    w = jnp.where(strict, w, 0.0)
                wbs.append(w.astype(MXU_DTYPE))
                r2 = a[:, 0:1] + lk[:, 0:1]
            es = [wb.astype(f32) * dw for wb, dw in zip(wbs, dws)]
            esums = [_split_dot(e, mi) for e in es]
            dq = None
            for (ls, _), e, esum, wb, b, k0 in zip(lg, es, esums, wbs, blks, k0s):
                esuf = g2 + esum
                beta = jnp.exp(ls)
                dz = e * (1.0 - beta) - beta * (dtot2 - esuf)
                if masked:
                    dz = jnp.where(strict, dz, 0.0)
                dzb = dz.astype(MXU_DTYPE)
                part = jnp.dot(jnp.concatenate([dzb[:tq], dzb[tq:]], axis=1), kcat[b], preferred_element_type=f32)
                dq = part if dq is None else dq + part
                dkacc[pl.ds(k0, tq), :] += _mm_tn(dzb, q2)
                dvacc[pl.ds(k0, tq), :] += _mm_tn(wb, do2)
                g2 = esuf[:, 0:1]
            dqacc[...] += dq
            return r2, g2

        def q_block(qi, carry):
            dqacc[...] = jnp.zeros_like(dqacc)
            q2, do2 = qs[qi], dos[qi]
            ov = o_ref[0, pl.ds(pl.multiple_of(qi * tq, tq), tq), :]
            dtot2 = jnp.sum(do2.astype(f32) * jnp.concatenate([ov, ov], axis=0), axis=1, keepdims=True)
            zero = jnp.zeros((2 * tq, 1), f32)
            c = step(q2, do2, dtot2, [qi], (zero, zero), True)
            c = lax.fori_loop(0, lax.shift_right_logical(qi, 1),
                              lambda i, c: step(q2, do2, dtot2, [qi - 1 - 2 * i, qi - 2 - 2 * i], c, False), c)
            lax.cond(jnp.bitwise_and(qi, 1) == 1, lambda c: step(q2, do2, dtot2, [0], c, False), lambda c: c, c)
            dq_ref[0, pl.ds(pl.multiple_of(qi * tq, tq), tq), :] = (dqacc[...] * scale).astype(dq_ref.dtype)
            return carry

        lax.fori_loop(0, nq, q_block, 0)
        dk_ref[0] = dkacc[...].astype(dk_ref.dtype)
        dv_ref[0] = dvacc[...].astype(dv_ref.dtype)

    dshape = jax.ShapeDtypeStruct((nb, ns, SB_WIDTH), MXU_DTYPE)
    stacked = pltpu.VMEM((nq, 2 * tq, LANES), MXU_DTYPE)
    flat = pltpu.VMEM((ns, LANES), MXU_DTYPE)
    return pl.pallas_call(
        body, name=name, grid=(nb, npair),
        in_specs=[qspec, kspec, vspec, ospec, ospec],
        out_specs=[ospec, ospec, ospec],
        out_shape=[dshape, dshape, dshape],
        scratch_shapes=[stacked, flat, stacked, flat, stacked,
                        pltpu.VMEM((tq, LANES), f32), pltpu.VMEM((ns, LANES), f32), pltpu.VMEM((ns, LANES), f32)],
        compiler_params=_cparams("parallel", "parallel"),
    )(qkv, qkv, qkv, o, do)


SSM_PAIRS = SSM_HEADS // 2
PAIRS_PER_GROUP = SSM_PAIRS // SSM_GROUPS
GROUP_WIDTH = SSM_WIDTH // SSM_GROUPS


def _silu(x):
    return x * jax.nn.sigmoid(x)


def _pair_expand(p):
    hrow = lax.broadcasted_iota(jnp.int32, (LANES, LANES), 0)
    lane = lax.broadcasted_iota(jnp.int32, (LANES, LANES), 1)
    return (hrow == jnp.where(lane >= SSM_HEAD_DIM, 2 * p + 1, 2 * p)).astype(f32)


def _row_expand(v, ex):
    return jnp.sum(_mm_exact(jnp.broadcast_to(v, (8, LANES)), ex), axis=0, keepdims=True) * 0.125


def _ssd_chunk(xs_pre, b_pre, c_pre, dt_raw, dt_raw_t, z, st, dt_bias_r, dt_bias_c, a_log_r, a_log_c, d_skip,
               gains):
    n = dt_raw.shape[0]
    rows = lax.broadcasted_iota(jnp.int32, (n, n), 0)
    cols = lax.broadcasted_iota(jnp.int32, (n, n), 1)
    tril = cols <= rows
    tri_l = tril.astype(f32)
    tri_u = (rows <= cols).astype(f32)
    lane = lax.broadcasted_iota(jnp.int32, (n, LANES), 1)
    sub = lax.broadcasted_iota(jnp.int32, (LANES, n), 0)

    dt = _softplus(dt_raw + dt_bias_r)
    a_r = -jnp.exp(a_log_r)
    da = dt * a_r
    acs = _mm_exact(tri_l, da)
    dt_t = _softplus(dt_raw_t + dt_bias_c)
    acs_t = _mm_exact(dt_t * (-jnp.exp(a_log_c)), tri_u)

    bs = [_silu(b) for b in b_pre]
    cs = [_silu(c) for c in c_pre]
    cb = [dmm_nt(cs[g], bs[g]) for g in range(SSM_GROUPS)]

    ys, st_new = [], []
    for p in range(SSM_PAIRS):
        g = p // PAIRS_PER_GROUP
        ex = _pair_expand(p)
        xs = _silu(xs_pre[p])
        dt_p = _mm_exact(dt, ex)
        da_p = dt_p * _row_expand(a_r, ex)
        acs_p = _mm_exact(tri_l, da_p)
        end_p = jnp.sum(da_p, axis=0, keepdims=True)
        xdt = xs * dt_p
        y = jnp.exp(acs_p) * dmm(cs[g], st[p])
        for hh in range(2):
            h = 2 * p + hh
            col = jnp.sum(jnp.where(lane == h, acs, 0.0), axis=1, keepdims=True)
            row = jnp.sum(jnp.where(sub == h, acs_t, 0.0), axis=0, keepdims=True)
            decay = jnp.where(tril, jnp.exp(jnp.where(tril, col - row, 0.0)), 0.0)
            head = (lane >= SSM_HEAD_DIM) if hh else (lane < SSM_HEAD_DIM)
            y = y + dmm(cb[g] * decay, jnp.where(head, xdt, 0.0))
        st_new.append(jnp.exp(end_p) * st[p] + dmm_tn(bs[g], xdt * jnp.exp(end_p - acs_p)))
        ys.append(y + _row_expand(d_skip, ex) * xs)
    out = []
    for g in range(SSM_GROUPS):
        yg = jnp.concatenate(ys[g * PAIRS_PER_GROUP:(g + 1) * PAIRS_PER_GROUP], axis=1) * _silu(z[g])
        out.append(_rms(yg, gains[g]))
    return out, st_new


def _ssd_chunk_inputs(xconv, dtr, z, st_ref, gain):
    xs_pre = [xconv[:, LANES * p:LANES * (p + 1)] for p in range(SSM_PAIRS)]
    b0 = SSM_WIDTH
    c0 = SSM_WIDTH + SSM_GROUPS * SSM_STATE
    b_pre = [xconv[:, b0 + SSM_STATE * g:b0 + SSM_STATE * (g + 1)] for g in range(SSM_GROUPS)]
    c_pre = [xconv[:, c0 + SSM_STATE * g:c0 + SSM_STATE * (g + 1)] for g in range(SSM_GROUPS)]
    zs = [z[:, GROUP_WIDTH * g:GROUP_WIDTH * (g + 1)] for g in range(SSM_GROUPS)]
    sts = [st_ref[p] for p in range(SSM_PAIRS)]
    gains = [gain[:, GROUP_WIDTH * g:GROUP_WIDTH * (g + 1)] for g in range(SSM_GROUPS)]
    return xs_pre, b_pre, c_pre, dtr, dtr.T, zs, sts, gains


def ssd_forward(xbc, dt_raw, z, cw, cb, dbr, dbc, alr, alc, dsk, gain, name):
    nb, ns, wx = xbc.shape
    ln = SSM_CHUNK
    nt = ns // ln
    tile = lambda c: pl.BlockSpec((1, ln, c), lambda b, j: (b, j, 0))
    st_spec = pl.BlockSpec((1, 1, SSM_PAIRS, SSM_STATE, LANES), lambda b, j: (b, j, 0, 0, 0))

    def body(xbc_ref, dt_ref, z_ref, cw_ref, cb_ref, dbr_ref, dbc_ref, alr_ref, alc_ref, dsk_ref, gain_ref,
             y_ref, xconv_ref, stp_ref, xin, st):
        @pl.when(pl.program_id(1) == 0)
        def _():
            xin[0:HALO, :] = jnp.zeros((HALO, wx), f32)
            st[...] = jnp.zeros_like(st)

        xin[HALO:HALO + ln, :] = xbc_ref[0]
        xconv = jnp.broadcast_to(cb_ref[...], (ln, wx))
        for k in range(SSM_CONV):
            xconv = xconv + cw_ref[k:k + 1, :] * xin[pl.ds(HALO - SSM_CONV + 1 + k, ln), :]
        xin[0:HALO, :] = xin[ln:ln + HALO, :]
        xconv_ref[0] = xconv
        stp_ref[0, 0] = st[...]
        xs_pre, b_pre, c_pre, dtr, dtr_t, zs, sts, gains = _ssd_chunk_inputs(xconv, dt_ref[0], z_ref[0], st,
                                                                             gain_ref[...])
        out, st_new = _ssd_chunk(xs_pre, b_pre, c_pre, dtr, dtr_t, zs, sts, dbr_ref[...], dbc_ref[...],
                                 alr_ref[...], alc_ref[...], dsk_ref[...], gains)
        y_ref[0] = jnp.concatenate(out, axis=1).astype(y_ref.dtype)
        for p in range(SSM_PAIRS):
            st[p] = st_new[p]

    params = [cw, cb, dbr, dbc, alr, alc, dsk, gain]
    return pl.pallas_call(
        body, name=name, grid=(nb, nt),
        in_specs=[tile(wx), tile(LANES), tile(SSM_WIDTH)] + [_const2(p.shape) for p in params],
        out_specs=[tile(SSM_WIDTH), tile(wx), st_spec],
        out_shape=[jax.ShapeDtypeStruct((nb, ns, SSM_WIDTH), MXU_DTYPE), jax.ShapeDtypeStruct((nb, ns, wx), f32),
                   jax.ShapeDtypeStruct((nb, nt, SSM_PAIRS, SSM_STATE, LANES), f32)],
        scratch_shapes=[pltpu.VMEM((ln + HALO, wx), f32), pltpu.VMEM((SSM_PAIRS, SSM_STATE, LANES), f32)],
        compiler_params=_cparams("arbitrary", "arbitrary"),
    )(xbc, dt_raw, z, *params)


def ssd_backward(dy, xbc, xconv, dt_raw, z, stp, cw, cb, dbr, dbc, alr, alc, dsk, gain, name):
    nb, ns, wx = xbc.shape
    ln = SSM_CHUNK
    nt = ns // ln
    per = ln // HALO
    rj = lambda j: nt - 1 - j
    tile = lambda c: pl.BlockSpec((1, ln, c), lambda b, j: (b, rj(j), 0))
    before = pl.BlockSpec((1, HALO, wx), lambda b, j: (b, jnp.maximum(rj(j) * per - 1, 0), 0))
    st_spec = pl.BlockSpec((1, 1, SSM_PAIRS, SSM_STATE, LANES), lambda b, j: (b, rj(j), 0, 0, 0))

    def body(dy_ref, xbc_ref, xbcb_ref, xconv_ref, dt_ref, z_ref, stp_ref,
             cw_ref, cb_ref, dbr_ref, dbc_ref, alr_ref, alc_ref, dsk_ref, gain_ref,
             dxbc_ref, ddt_ref, dz_ref, dcw_ref, dcb_ref, ddbr_ref, ddbc_ref, dalr_ref, dalc_ref, ddsk_ref, dgain_ref,
             dxc_ext, dst, xin):
        j = pl.program_id(1)
        first = _first_step()
        at_seq_start = j == nt - 1

        @pl.when(j == 0)
        def _():
            dxc_ext[ln:ln + HALO, :] = jnp.zeros((HALO, wx), f32)
            dst[...] = jnp.zeros_like(dst)

        xs_pre, b_pre, c_pre, dtr, dtr_t, zs, sts, gains = _ssd_chunk_inputs(xconv_ref[0], dt_ref[0], z_ref[0],
                                                                             stp_ref.at[0, 0], gain_ref[...])
        _, vjp = jax.vjp(_ssd_chunk, xs_pre, b_pre, c_pre, dtr, dtr_t, zs, sts, dbr_ref[...], dbc_ref[...],
                         alr_ref[...], alc_ref[...], dsk_ref[...], gains)
        dyv = dy_ref[0].astype(f32)
        cot = ([dyv[:, GROUP_WIDTH * g:GROUP_WIDTH * (g + 1)] for g in range(SSM_GROUPS)],
               [dst[p] for p in range(SSM_PAIRS)])
        dxs, db, dc, ddt, ddt_t, dzs, dsts, ddbr, ddbc, dalr, dalc, ddsk, dgains = vjp(cot)
        for p in range(SSM_PAIRS):
            dst[p] = dsts[p]
        ddt_ref[0] = (ddt + ddt_t.T).astype(ddt_ref.dtype)
        dz_ref[0] = jnp.concatenate(dzs, axis=1).astype(dz_ref.dtype)
        _accum(ddbr_ref, ddbr, first)
        _accum(ddbc_ref, ddbc, first)
        _accum(dalr_ref, dalr, first)
        _accum(dalc_ref, dalc, first)
        _accum(ddsk_ref, ddsk, first)
        _accum(dgain_ref, jnp.concatenate(dgains, axis=1), first)

        dxc = jnp.concatenate(dxs + db + dc, axis=1)
        _accum(dcb_ref, jnp.sum(dxc, axis=0, keepdims=True), first)
        dxc_ext[0:ln, :] = dxc
        dxp = jnp.zeros((ln, wx), f32)
        for k in range(SSM_CONV):
            dxp = dxp + cw_ref[k:k + 1, :] * dxc_ext[pl.ds(SSM_CONV - 1 - k, ln), :]
        dxbc_ref[0] = dxp.astype(dxbc_ref.dtype)
        dxc_ext[ln:ln + HALO, :] = dxc[0:HALO, :]

        xin[0:HALO, :] = jnp.where(at_seq_start, 0.0, xbcb_ref[0])
        xin[HALO:HALO + ln, :] = xbc_ref[0]
        dcw_rows = [jnp.sum(dxc * xin[pl.ds(HALO - SSM_CONV + 1 + k, ln), :], axis=0, keepdims=True)
                    for k in range(SSM_CONV)]
        dcw_rows += [jnp.zeros((1, wx), f32)] * (HALO - SSM_CONV)
        _accum(dcw_ref, jnp.concatenate(dcw_rows, axis=0), first)

    params = [cw, cb, dbr, dbc, alr, alc, dsk, gain]
    pshape = lambda p: jax.ShapeDtypeStruct(p.shape, f32)
    return pl.pallas_call(
        body, name=name, grid=(nb, nt),
        in_specs=[tile(SSM_WIDTH), tile(wx), before, tile(wx), tile(LANES), tile(SSM_WIDTH), st_spec]
        + [_const2(p.shape) for p in params],
        out_specs=[tile(wx), tile(LANES), tile(SSM_WIDTH), _const2((HALO, wx))] + [_const2(p.shape) for p in params[1:]],
        out_shape=[jax.ShapeDtypeStruct((nb, ns, wx), MXU_DTYPE), jax.ShapeDtypeStruct((nb, ns, LANES), MXU_DTYPE),
                   jax.ShapeDtypeStruct((nb, ns, SSM_WIDTH), MXU_DTYPE), jax.ShapeDtypeStruct((HALO, wx), f32)]
        + [pshape(p) for p in params[1:]],
        scratch_shapes=[pltpu.VMEM((ln + HALO, wx), f32), pltpu.VMEM((SSM_PAIRS, SSM_STATE, LANES), f32),
                        pltpu.VMEM((ln + HALO, wx), f32)],
        compiler_params=_cparams("arbitrary", "arbitrary"),
    )(dy, xbc, xbc, xconv, dt_raw, z, stp, *params)


CONF_HALO = 32
CONF_OFF = CONF_HALO - CONF_KERNEL + 1


def _conf_specs(ts, c, nt):
    per = ts // CONF_HALO
    tile = pl.BlockSpec((1, ts, c), lambda b, j: (b, j, 0))
    before = pl.BlockSpec((1, CONF_HALO, c), lambda b, j: (b, jnp.maximum(j * per - 1, 0), 0))
    after = pl.BlockSpec((1, CONF_HALO, c), lambda b, j: (b, jnp.minimum((j + 1) * per, nt * per - 1), 0))
    return tile, before, after


def _glu(x):
    return x[:, :CONF_WIDTH] * jax.nn.sigmoid(x[:, CONF_WIDTH:])


def _layernorm_parts(c):
    xc = c - jnp.mean(c, axis=-1, keepdims=True)
    r = lax.rsqrt(jnp.mean(xc * xc, axis=-1, keepdims=True) + EPS)
    return xc * r, r


def conf_forward(glu, cw, cb, ln_g, ln_b, name):
    nb, ns, wg = glu.shape
    w = CONF_WIDTH
    ts = SEQ_TILE
    nt = ns // ts
    tile, before, _ = _conf_specs(ts, wg, nt)

    def body(x_ref, xb_ref, cw_ref, cb_ref, g_ref, b_ref, y_ref, u_ext):
        u_ext[0:CONF_HALO, :] = jnp.where(pl.program_id(1) == 0, 0.0, _glu(xb_ref[0]))
        u_ext[CONF_HALO:CONF_HALO + ts, :] = _glu(x_ref[0])
        conv = jnp.broadcast_to(cb_ref[...], (ts, w))
        for k in range(CONF_KERNEL):
            conv = conv + cw_ref[k:k + 1, :] * u_ext[pl.ds(CONF_OFF + k, ts), :]
        xhat, _ = _layernorm_parts(conv)
        y_ref[0] = _silu(xhat * g_ref[...] + b_ref[...]).astype(y_ref.dtype)

    params = [cw, cb, ln_g, ln_b]
    return pl.pallas_call(
        body, name=name, grid=(nb, nt),
        in_specs=[tile, before] + [_const2(p.shape) for p in params],
        out_specs=pl.BlockSpec((1, ts, w), lambda b, j: (b, j, 0)),
        out_shape=jax.ShapeDtypeStruct((nb, ns, w), MXU_DTYPE),
        scratch_shapes=[pltpu.VMEM((ts + CONF_HALO, w), f32)],
        compiler_params=_cparams("parallel", "parallel"),
    )(glu, glu, *params)


def conf_backward(dy, glu, cw, cb, ln_g, ln_b, name):
    nb, ns, wg = glu.shape
    w = CONF_WIDTH
    ts = SEQ_TILE
    nt = ns // ts
    te = ts + CONF_HALO
    tile, before, after = _conf_specs(ts, wg, nt)
    dtile, _, dafter = _conf_specs(ts, w, nt)

    def body(dy_ref, dya_ref, x_ref, xb_ref, xa_ref, cw_ref, cb_ref, g_ref, b_ref,
             dx_ref, dcw_ref, dcb_ref, dg_ref, db_ref, u_ext, dc_ext):
        j = pl.program_id(1)
        first = _first_step()
        x = x_ref[0]
        u_ext[0:CONF_HALO, :] = jnp.where(j == 0, 0.0, _glu(xb_ref[0]))
        u_ext[CONF_HALO:CONF_HALO + ts, :] = _glu(x)
        u_ext[CONF_HALO + ts:CONF_HALO + te, :] = _glu(xa_ref[0])
        conv = jnp.broadcast_to(cb_ref[...], (te, w))
        for k in range(CONF_KERNEL):
            conv = conv + cw_ref[k:k + 1, :] * u_ext[pl.ds(CONF_OFF + k, te), :]
        xhat, r = _layernorm_parts(conv)
        lnout = xhat * g_ref[...] + b_ref[...]
        sg = jax.nn.sigmoid(lnout)
        rows = lax.broadcasted_iota(jnp.int32, (te, w), 0)
        dyv = jnp.concatenate([dy_ref[0].astype(f32), dya_ref[0].astype(f32)], axis=0)
        dyv = jnp.where(jnp.logical_and(j == nt - 1, rows >= ts), 0.0, dyv)
        dln = dyv * sg * (1.0 + lnout * (1.0 - sg))
        in_tile = rows < ts
        _accum(dg_ref, jnp.sum(jnp.where(in_tile, dln * xhat, 0.0), axis=0, keepdims=True), first)
        _accum(db_ref, jnp.sum(jnp.where(in_tile, dln, 0.0), axis=0, keepdims=True), first)
        dxh = dln * g_ref[...]
        dconv = r * (dxh - jnp.mean(dxh, axis=-1, keepdims=True) - xhat * jnp.mean(dxh * xhat, axis=-1, keepdims=True))
        dc_ext[...] = dconv
        dct = dconv[0:ts, :]
        _accum(dcb_ref, jnp.sum(dct, axis=0, keepdims=True), first)
        du = jnp.zeros((ts, w), f32)
        dcw_rows = []
        for k in range(CONF_KERNEL):
            du = du + cw_ref[k:k + 1, :] * dc_ext[pl.ds(CONF_KERNEL - 1 - k, ts), :]
            dcw_rows.append(jnp.sum(dct * u_ext[pl.ds(CONF_OFF + k, ts), :], axis=0, keepdims=True))
        dcw_rows.append(jnp.zeros((1, w), f32))
        _accum(dcw_ref, jnp.concatenate(dcw_rows, axis=0), first)
        sb = jax.nn.sigmoid(x[:, w:])
        dx_ref[0] = jnp.concatenate([du * sb, du * x[:, :w] * sb * (1.0 - sb)], axis=1).astype(dx_ref.dtype)

    params = [cw, cb, ln_g, ln_b]
    return pl.pallas_call(
        body, name=name, grid=(nb, nt),
        in_specs=[dtile, dafter, tile, before, after] + [_const2(p.shape) for p in params],
        out_specs=[tile] + [_const2(p.shape) for p in params],
        out_shape=[jax.ShapeDtypeStruct((nb, ns, wg), MXU_DTYPE)] + [jax.ShapeDtypeStruct(p.shape, f32) for p in params],
        scratch_shapes=[pltpu.VMEM((te + CONF_HALO, w), f32), pltpu.VMEM((te, w), f32)],
        compiler_params=_cparams("arbitrary", "arbitrary"),
    )(dy, dy, glu, glu, glu, *params)


def _row(v):
    return v.reshape(1, -1).astype(f32)


def _pad_to(v, n, axis):
    pads = [(0, 0)] * v.ndim
    pads[axis] = (0, n - v.shape[axis])
    return jnp.pad(v, pads)


def _block_diag(w):
    nh, d, _ = w.shape
    eye = jnp.eye(nh, dtype=w.dtype)
    return (eye[:, None, :, None] * w[:, :, None, :]).reshape(nh * d, nh * d)


def _diag_blocks(m, nh):
    d = m.shape[0] // nh
    idx = jnp.arange(nh)
    return m.reshape(nh, d, nh, d)[idx, :, idx, :]


def _mix_even_fwd(h, gpre, w, nb, ns):
    t = nb * ns
    w_in = w["ev_w_in"][0]
    w_lx, w_lg, w_qkv = w_in[:, :LRU_WIDTH], w_in[:, LRU_WIDTH:2 * LRU_WIDTH], w_in[:, 2 * LRU_WIDTH:]
    xpre, gate, qkv = norm_matmul(h, gpre, [w_lx, w_lg, w_qkv], [f32, f32, f32], name="ev_in_proj")
    lru_p = [w["ev_lru_conv_w"][0], _row(w["ev_lru_conv_b"][0]),
             _block_diag(w["ev_lru_gate_a_w"][0]).astype(MXU_DTYPE), _row(w["ev_lru_gate_a_b"][0]),
             _block_diag(w["ev_lru_gate_x_w"][0]).astype(MXU_DTYPE), _row(w["ev_lru_gate_x_b"][0]),
             _row(w["ev_lru_lambda"][0])]
    xpre3, gate3, qkv3 = xpre.reshape(nb, ns, -1), gate.reshape(nb, ns, -1), qkv.reshape(nb, ns, -1)
    y_a, xc, hs = lru_forward(xpre3, gate3, *lru_p, name="ev_lru_fwd")
    o = sb_forward(qkv3, name="ev_sb_fwd")
    ys = [y_a.reshape(t, -1), o.reshape(t, -1)]
    saved = dict(xpre=xpre3, gate=gate3, qkv=qkv3, xc=xc, hs=hs, o=o, lru_p=lru_p)
    w_out = w["ev_w_out"][0]
    return ys, [w_out[:LRU_WIDTH], w_out[LRU_WIDTH:]], saved


def _mix_even_bwd(dys, saved, w, nb, ns):
    t = nb * ns
    dy_a, dy_b = [d.reshape(nb, ns, -1) for d in dys]
    outs = lru_backward(dy_a, saved["xpre"], saved["gate"], saved["xc"], saved["hs"], *saved["lru_p"],
                        name="ev_lru_bwd")
    dxp, dgt, dcw, dcb, dga, dgab, dgx, dgxb, dlam = outs
    dq, dk, dv = sb_backward(saved["qkv"], saved["o"], dy_b, name="ev_sb_bwd")
    w_in_t = w["ev_w_in"][0].T
    pieces = [dxp, dgt, dq, dk, dv]
    gs = [d.reshape(t, -1) for d in pieces]
    wts = [w_in_t[LRU_WIDTH * i:LRU_WIDTH * (i + 1)] for i in range(5)]
    grads = {
        "ev_lru_conv_w": dcw[:LRU_CONV][None], "ev_lru_conv_b": dcb,
        "ev_lru_gate_a_w": _diag_blocks(dga, LRU_HEADS)[None], "ev_lru_gate_a_b": dgab,
        "ev_lru_gate_x_w": _diag_blocks(dgx, LRU_HEADS)[None], "ev_lru_gate_x_b": dgxb,
        "ev_lru_lambda": dlam,
    }
    return gs, wts, grads


def _odd_params(w):
    ssd_p = [w["od_ssm_conv_w"][0], _row(w["od_ssm_conv_b"][0]),
             _pad_to(_row(w["od_ssm_dt_bias"][0]), LANES, 1), _pad_to(_row(w["od_ssm_dt_bias"][0]), LANES, 1).T,
             _pad_to(_row(w["od_ssm_a_log"][0]), LANES, 1), _pad_to(_row(w["od_ssm_a_log"][0]), LANES, 1).T,
             _pad_to(_row(w["od_ssm_d"][0]), LANES, 1), _row(w["od_ssm_norm"][0])]
    conf_p = [_pad_to(w["od_cm_conv_w"][0], CONF_HALO, 0), _row(w["od_cm_conv_b"][0]),
              _row(w["od_cm_ln_g"][0]), _row(w["od_cm_ln_b"][0])]
    return ssd_p, conf_p


ODD_SPLITS = (SSM_WIDTH, SSM_WIDTH + SSM_XBC, SSM_WIDTH + SSM_XBC + SSM_HEADS)


def _mix_odd_fwd(h, gpre, w, nb, ns):
    t = nb * ns
    w_in = w["od_w_in"][0]
    s0, s1, s2 = ODD_SPLITS
    w_z, w_xbc, w_dt, w_glu = w_in[:, :s0], w_in[:, s0:s1], _pad_to(w_in[:, s1:s2], LANES, 1), w_in[:, s2:]
    zz, xbc, dtr, glu = norm_matmul(h, gpre, [w_z, w_xbc, w_dt, w_glu], [f32] * 4, name="od_in_proj")
    ssd_p, conf_p = _odd_params(w)
    zz3, xbc3, dtr3, glu3 = [a.reshape(nb, ns, -1) for a in (zz, xbc, dtr, glu)]
    y_c, xconv, stp = ssd_forward(xbc3, dtr3, zz3, *ssd_p, name="od_ssd_fwd")
    y_d = conf_forward(glu3, *conf_p, name="od_conf_fwd")
    ys = [y_c.reshape(t, -1), y_d.reshape(t, -1)]
    saved = dict(z=zz3, xbc=xbc3, dtr=dtr3, glu=glu3, xconv=xconv, stp=stp, ssd_p=ssd_p, conf_p=conf_p)
    w_out = w["od_w_out"][0]
    return ys, [w_out[:SSM_WIDTH], w_out[SSM_WIDTH:]], saved


def _mix_odd_bwd(dys, saved, w, nb, ns):
    t = nb * ns
    dy_c, dy_d = [d.reshape(nb, ns, -1) for d in dys]
    outs = ssd_backward(dy_c, saved["xbc"], saved["xconv"], saved["dtr"], saved["z"], saved["stp"], *saved["ssd_p"],
                        name="od_ssd_bwd")
    dxbc, ddt, dz, dcw, dcb, ddbr, ddbc, dalr, dalc, ddsk, dgain = outs
    dglu, ccw, ccb, clg, clb = conf_backward(dy_d, saved["glu"], *saved["conf_p"], name="od_conf_bwd")
    w_in_t = w["od_w_in"][0].T
    s0, s1, s2 = ODD_SPLITS
    gs = [d.reshape(t, -1) for d in (dz, dxbc, ddt, dglu)]
    wts = [w_in_t[:s0], w_in_t[s0:s1], _pad_to(w_in_t[s1:s2], LANES, 0), w_in_t[s2:]]
    nh = SSM_HEADS
    grads = {
        "od_ssm_conv_w": dcw[:SSM_CONV][None], "od_ssm_conv_b": dcb,
        "od_ssm_dt_bias": ddbr[:, :nh] + ddbc[:nh, 0][None], "od_ssm_a_log": dalr[:, :nh] + dalc[:nh, 0][None],
        "od_ssm_d": ddsk[:, :nh], "od_ssm_norm": dgain,
        "od_cm_conv_w": ccw[:CONF_KERNEL][None], "od_cm_conv_b": ccb, "od_cm_ln_g": clg, "od_cm_ln_b": clb,
    }
    return gs, wts, grads


def local_step(x, p, target, w):
    nb, ns, d = x.shape
    t = nb * ns
    h = x.reshape(t, d)
    depth = p.shape[0]
    tapes = []
    for i in range(depth):
        even = i % 2 == 0
        tag = f"l{i}_"
        gpre = _row(w["norm_mix_pre"][i])
        ys, w_outs, saved = (_mix_even_fwd if even else _mix_odd_fwd)(h, gpre, w, nb, ns)
        h1, m = matmul_residual_norm(ys, w_outs, h, _row(w["norm_mix_post"][i]), name=tag + "out_proj")
        a1, = norm_matmul(h1, _row(w["norm_mlp_pre"][i]), [w["mlp_w1"][i]], [f32], name=tag + "mlp_up")
        h2, f = matmul_residual_norm([a1], [w["mlp_w2"][i]], h1, _row(w["norm_mlp_post"][i]), name=tag + "mlp_down",
                                     relu2=True)
        pi = p[i].reshape(t, -1)
        h3, gl, emb = ple_forward(h2, pi, w["ple_w_gate"][i], w["ple_w_proj"][i], _row(w["norm_ple"][i]),
                                  name=tag + "ple")
        tapes.append(dict(h=h, ys=ys, w_outs=w_outs, saved=saved, h1=h1, m=m, a1=a1, h2=h2, f=f, pi=pi, gl=gl,
                          emb=emb))
        h = h3

    loss_row, dh = loss_and_grad(h, target.reshape(t, d), name="loss")
    grads = {}
    stacked = {k: [None] * depth for k in ("norm_mix_pre", "norm_mix_post", "norm_mlp_pre", "norm_mlp_post", "norm_ple",
                                           "mlp_w1", "mlp_w2", "ple_w_proj", "ple_w_gate")}
    for i in reversed(range(depth)):
        even = i % 2 == 0
        tag = f"l{i}_"
        tp = tapes[i]
        dh2, dgl, demb, dg = ple_backward(dh, tp["h2"], tp["gl"], tp["emb"], _row(w["norm_ple"][i]),
                                          w["ple_w_gate"][i].T, name=tag + "ple_bwd")
        stacked["norm_ple"][i] = dg
        stacked["ple_w_gate"][i] = weight_grad(tp["h2"], dgl, name=tag + "dw_gate")
        stacked["ple_w_proj"][i] = weight_grad(tp["pi"], demb, name=tag + "dw_proj")
        d_f, (da1,), dg = bwd_through_norm_out(dh2, tp["f"], _row(w["norm_mlp_post"][i]), [w["mlp_w2"][i].T],
                                               [MXU_DTYPE], name=tag + "mlp_down_bwd", relu2_of=tp["a1"])
        stacked["norm_mlp_post"][i] = dg
        stacked["mlp_w2"][i] = weight_grad(tp["a1"], d_f, name=tag + "dw2", prologue="relu2")
        gpre = _row(w["norm_mlp_pre"][i])
        dh1, dg = bwd_through_norm_in(dh2, [da1], [w["mlp_w1"][i].T], tp["h1"], gpre, name=tag + "mlp_up_bwd")
        stacked["norm_mlp_pre"][i] = dg
        stacked["mlp_w1"][i] = weight_grad(tp["h1"], da1, name=tag + "dw1", prologue="rms", gain=gpre)
        dm, dys, dg = bwd_through_norm_out(dh1, tp["m"], _row(w["norm_mix_post"][i]), [wo.T for wo in tp["w_outs"]],
                                           [f32] * len(tp["w_outs"]), name=tag + "out_proj_bwd")
        stacked["norm_mix_post"][i] = dg
        dw_out = jnp.concatenate([weight_grad(y, dm, name=tag + f"dw_out{k}") for k, y in enumerate(tp["ys"])], axis=0)
        gs, wts, mix_grads = (_mix_even_bwd if even else _mix_odd_bwd)(dys, tp["saved"], w, nb, ns)
        grads.update(mix_grads)
        gpre = _row(w["norm_mix_pre"][i])
        dh, dg = bwd_through_norm_in(dh1, gs, wts, tp["h"], gpre, name=tag + "in_proj_bwd")
        stacked["norm_mix_pre"][i] = dg
        dw_in = [weight_grad(tp["h"], g, name=tag + f"dw_in{k}", prologue="rms", gain=gpre) for k, g in enumerate(gs)]
        if even:
            grads["ev_w_in"] = jnp.concatenate(dw_in, axis=1)[None]
            grads["ev_w_out"] = dw_out[None]
        else:
            dw_in[2] = dw_in[2][:, :SSM_HEADS]
            grads["od_w_in"] = jnp.concatenate(dw_in, axis=1)[None]
            grads["od_w_out"] = dw_out[None]
    for k, v in stacked.items():
        grads[k] = jnp.concatenate(v, axis=0) if v[0].shape[0] == 1 and v[0].ndim == 2 and k.startswith("norm") \
            else jnp.stack(v, axis=0)
    return loss_row[0, 0], dh.reshape(nb, ns, d), grads


MESH_ID = pl.DeviceIdType.MESH
ANY = pl.BlockSpec(memory_space=pl.ANY)


def _mesh_pos():
    return lax.axis_index("x"), lax.axis_index("y"), lax.axis_index("c")


def all_gather(shard, name):
    r, n = shard.shape

    def body(x_ref, out_ref, send_sems, recv_sems, local_sem):
        x, y, c = _mesh_pos()
        me, sibling = (x, y, c), (x, y, 1 - c)
        chips = [(1 - x, y), (x, 1 - y), (1 - x, 1 - y)]

        def slot(px, py, pc):
            return out_ref.at[4 * px + 2 * py + pc]

        def copy(k, block, to, src=None):
            return pltpu.make_async_remote_copy(
                src_ref=slot(*block) if src is None else src, dst_ref=slot(*block),
                send_sem=send_sems.at[k], recv_sem=recv_sems.at[k], device_id=to, device_id_type=MESH_ID)

        mine = pltpu.make_async_copy(x_ref, slot(*me), local_sem)
        mine.start()
        first = [copy(0, me, sibling, src=x_ref)]
        first += [copy(1 + j, me, (*chip, c), src=x_ref) for j, chip in enumerate(chips)]
        for cp in first:
            cp.start()
        passed = [copy(4 + j, (*chip, c), sibling) for j, chip in enumerate(chips)]
        for j, chip in enumerate(chips):
            copy(1 + j, (*chip, c), me).wait_recv()
            passed[j].start()
        copy(0, sibling, me).wait_recv()
        for j, chip in enumerate(chips):
            copy(4 + j, (*chip, 1 - c), me).wait_recv()
        for cp in first + passed:
            cp.wait_send()
        mine.wait()

    return pl.pallas_call(
        body, name=name,
        out_shape=jax.ShapeDtypeStruct((N_DEV, r, n), shard.dtype),
        in_specs=[ANY], out_specs=ANY,
        scratch_shapes=[pltpu.SemaphoreType.DMA((7,)), pltpu.SemaphoreType.DMA((7,)), pltpu.SemaphoreType.DMA],
    )(shard)


def scatter_to_sibling(parts, name):
    _, r, n = parts.shape

    def body(g_ref, out_ref, send_sems, recv_sems):
        x, y, c = _mesh_pos()
        sibling = (x, y, 1 - c)
        copies = []
        for chip in range(4):
            copies.append(pltpu.make_async_remote_copy(
                src_ref=g_ref.at[2 * chip + (1 - c)], dst_ref=out_ref.at[chip],
                send_sem=send_sems.at[chip], recv_sem=recv_sems.at[chip], device_id=sibling, device_id_type=MESH_ID))
        for cp in copies:
            cp.start()
        for cp in copies:
            cp.wait_recv()
        for cp in copies:
            cp.wait_send()

    return pl.pallas_call(
        body, name=name,
        out_shape=jax.ShapeDtypeStruct((4, r, n), parts.dtype),
        in_specs=[ANY], out_specs=ANY,
        scratch_shapes=[pltpu.SemaphoreType.DMA((4,)), pltpu.SemaphoreType.DMA((4,))],
    )(parts)


def scatter_to_chips(partial, name):
    _, r, n = partial.shape

    def body(p_ref, out_ref, send_sems, recv_sems):
        x, y, c = _mesh_pos()
        chips = [(1 - x, y), (x, 1 - y), (1 - x, 1 - y)]
        copies = []
        for j, (px, py) in enumerate(chips):
            copies.append(pltpu.make_async_remote_copy(
                src_ref=p_ref.at[2 * px + py], dst_ref=out_ref.at[j],
                send_sem=send_sems.at[j], recv_sem=recv_sems.at[j], device_id=(px, py, c), device_id_type=MESH_ID))
        for cp in copies:
            cp.start()
        for cp in copies:
            cp.wait_recv()
        for cp in copies:
            cp.wait_send()

    return pl.pallas_call(
        body, name=name,
        out_shape=jax.ShapeDtypeStruct((3, r, n), partial.dtype),
        in_specs=[ANY], out_specs=ANY,
        scratch_shapes=[pltpu.SemaphoreType.DMA((3,)), pltpu.SemaphoreType.DMA((3,))],
    )(partial)


ROW_TILE = 1024
ICI_DTYPE = jnp.bfloat16


def add_sibling_parts(parts, received, core, name):
    _, r, n = parts.shape

    def body(c_ref, a_ref, b_ref, o_ref, ob_ref):
        s = a_ref[...] + b_ref[...]
        o_ref[...] = s
        ob_ref[...] = s.astype(ob_ref.dtype)

    blk = pl.BlockSpec((1, ROW_TILE, n), lambda i, j, c_ref: (i, j, 0))
    return pl.pallas_call(
        body, name=name,
        grid_spec=pltpu.PrefetchScalarGridSpec(
            num_scalar_prefetch=1, grid=(4, r // ROW_TILE),
            in_specs=[pl.BlockSpec((1, ROW_TILE, n), lambda i, j, c_ref: (2 * i + c_ref[0], j, 0)), blk],
            out_specs=[blk, blk]),
        out_shape=[jax.ShapeDtypeStruct((4, r, n), f32), jax.ShapeDtypeStruct((4, r, n), ICI_DTYPE)],
        compiler_params=_cparams("parallel", "parallel"),
    )(core, parts, received)


def _adamw(w, g, m, v):
    m = ADAM_B1 * m + (1.0 - ADAM_B1) * g
    v = ADAM_B2 * v + (1.0 - ADAM_B2) * jnp.square(g)
    m_hat = m / (1.0 - ADAM_B1 ** ADAM_STEP)
    v_hat = v / (1.0 - ADAM_B2 ** ADAM_STEP)
    delta = -ADAM_LR * (m_hat / (jnp.sqrt(v_hat) + ADAM_EPS) + ADAM_WD * w)
    return delta, m, v


def adamw_sharded(partial, received, chip, w, m, v, name):
    _, r, n = partial.shape

    def body(k_ref, p_ref, r_ref, w_ref, m_ref, v_ref, g_out, d_out, m_out, v_out):
        g = p_ref[0] + r_ref[0].astype(f32)
        g = g + r_ref[1].astype(f32)
        g = g + r_ref[2].astype(f32)
        delta, mn, vn = _adamw(w_ref[...], g, m_ref[...], v_ref[...])
        g_out[...] = g
        d_out[...] = delta
        m_out[...] = mn
        v_out[...] = vn

    flat = pl.BlockSpec((ROW_TILE, n), lambda j, k_ref: (j, 0))
    return pl.pallas_call(
        body, name=name,
        grid_spec=pltpu.PrefetchScalarGridSpec(
            num_scalar_prefetch=1, grid=(r // ROW_TILE,),
            in_specs=[pl.BlockSpec((1, ROW_TILE, n), lambda j, k_ref: (k_ref[0], j, 0)),
                      pl.BlockSpec((3, ROW_TILE, n), lambda j, k_ref: (0, j, 0)), flat, flat, flat],
            out_specs=[flat] * 4),
        out_shape=[jax.ShapeDtypeStruct((r, n), f32)] * 4,
        compiler_params=_cparams("parallel"),
    )(chip, partial, received, w, m, v)


def adamw_replicated(gathered, w, m, v, name):
    _, r, n = gathered.shape

    def body(g_ref, w_ref, m_ref, v_ref, g_out, d_out, m_out, v_out):
        g = g_ref[0]
        for k in range(1, N_DEV):
            g = g + g_ref[k]
        delta, mn, vn = _adamw(w_ref[...], g, m_ref[...], v_ref[...])
        g_out[...] = g
        d_out[...] = delta
        m_out[...] = mn
        v_out[...] = vn

    return pl.pallas_call(
        body, name=name,
        out_shape=[jax.ShapeDtypeStruct((r, n), f32)] * 4,
        compiler_params=pltpu.CompilerParams(vmem_limit_bytes=VMEM_LIMIT),
    )(gathered, w, m, v)


W_NAMES = ['ev_w_in', 'ev_lru_conv_w', 'ev_lru_conv_b', 'ev_lru_gate_a_w', 'ev_lru_gate_a_b', 'ev_lru_gate_x_w',
           'ev_lru_gate_x_b', 'ev_lru_lambda', 'ev_w_out', 'od_w_in', 'od_ssm_conv_w', 'od_ssm_conv_b',
           'od_ssm_dt_bias', 'od_ssm_a_log', 'od_ssm_d', 'od_ssm_norm', 'od_cm_conv_w', 'od_cm_conv_b', 'od_cm_ln_g',
           'od_cm_ln_b', 'od_w_out', 'norm_mix_pre', 'norm_mix_post', 'norm_mlp_pre', 'norm_mlp_post', 'norm_ple',
           'mlp_w1', 'mlp_w2', 'ple_w_proj', 'ple_w_gate']
BIG_SHARDED = {'ev_w_in': 2, 'ev_w_out': 1, 'od_w_in': 2, 'od_w_out': 1, 'mlp_w1': 2, 'mlp_w2': 1, 'ple_w_proj': 2,
               'ple_w_gate': 1}
SMALL_SHARDED = {'ev_lru_conv_w': 2, 'od_ssm_conv_w': 2, 'od_ssm_conv_b': 1, 'od_ssm_norm': 1, 'od_cm_conv_w': 2,
                 'od_cm_conv_b': 1, 'od_cm_ln_g': 1, 'od_cm_ln_b': 1}
SHARDED = {**BIG_SHARDED, **SMALL_SHARDED}
REPLICATED = [n for n in W_NAMES if n not in SHARDED]


def _round_up(n, k):
    return -(-n // k) * k


def _pack_rows(flat, rows_multiple):
    n = flat.shape[0]
    total = _round_up(n, LANES * rows_multiple)
    return jnp.pad(flat, (0, total - n)).reshape(-1, LANES)


def _unpack(flat, shapes):
    out, off = {}, 0
    for name, shape in shapes.items():
        size = math.prod(shape)
        out[name] = flat[off:off + size].reshape(shape)
        off += size
    return out


def _gather_weights(w):
    big = jnp.concatenate([w[k].astype(MXU_DTYPE).reshape(-1) for k in BIG_SHARDED])
    small = jnp.concatenate([w[k].reshape(-1) for k in SMALL_SHARDED])
    terms, rest = [], small
    for _ in range(3):
        term = rest.astype(MXU_DTYPE)
        terms.append(term)
        rest = rest - term.astype(f32)
    n_big, n_small = big.shape[0], small.shape[0]
    buf = _pack_rows(jnp.concatenate([big] + terms), 16)
    gathered = all_gather(buf, name="gather_weights").reshape(N_DEV, -1)
    full = {k: w[k] for k in REPLICATED}
    pieces = {k: [] for k in SHARDED}
    for d in range(N_DEV):
        row = gathered[d]
        for k, v in _unpack(row[:n_big], {k: w[k].shape for k in BIG_SHARDED}).items():
            pieces[k].append(v)
        t0, t1, t2 = [row[n_big + i * n_small:n_big + (i + 1) * n_small].astype(f32) for i in range(3)]
        for k, v in _unpack(t0 + t1 + t2, {k: w[k].shape for k in SMALL_SHARDED}).items():
            pieces[k].append(v)
    for k, axis in SHARDED.items():
        full[k] = jnp.concatenate(pieces[k], axis=axis)
    return full


def _pack_sharded(tree):
    return _pack_rows(jnp.concatenate([tree[k].astype(f32).reshape(-1) for k in SHARDED]), ROW_TILE)


def _pack_replicated(tree):
    return _pack_rows(jnp.concatenate([tree[k].astype(f32).reshape(-1) for k in REPLICATED]), 8)


def kernel(x, p, ev_w_in, ev_lru_conv_w, ev_lru_conv_b, ev_lru_gate_a_w, ev_lru_gate_a_b, ev_lru_gate_x_w, ev_lru_gate_x_b, ev_lru_lambda, ev_w_out, od_w_in, od_ssm_conv_w, od_ssm_conv_b, od_ssm_dt_bias, od_ssm_a_log, od_ssm_d, od_ssm_norm, od_cm_conv_w, od_cm_conv_b, od_cm_ln_g, od_cm_ln_b, od_w_out, norm_mix_pre, norm_mix_post, norm_mlp_pre, norm_mlp_post, norm_ple, mlp_w1, mlp_w2, ple_w_proj, ple_w_gate, loss_target, m_ev_w_in, m_ev_lru_conv_w, m_ev_lru_conv_b, m_ev_lru_gate_a_w, m_ev_lru_gate_a_b, m_ev_lru_gate_x_w, m_ev_lru_gate_x_b, m_ev_lru_lambda, m_ev_w_out, m_od_w_in, m_od_ssm_conv_w, m_od_ssm_conv_b, m_od_ssm_dt_bias, m_od_ssm_a_log, m_od_ssm_d, m_od_ssm_norm, m_od_cm_conv_w, m_od_cm_conv_b, m_od_cm_ln_g, m_od_cm_ln_b, m_od_w_out, m_norm_mix_pre, m_norm_mix_post, m_norm_mlp_pre, m_norm_mlp_post, m_norm_ple, m_mlp_w1, m_mlp_w2, m_ple_w_proj, m_ple_w_gate, v_ev_w_in, v_ev_lru_conv_w, v_ev_lru_conv_b, v_ev_lru_gate_a_w, v_ev_lru_gate_a_b, v_ev_lru_gate_x_w, v_ev_lru_gate_x_b, v_ev_lru_lambda, v_ev_w_out, v_od_w_in, v_od_ssm_conv_w, v_od_ssm_conv_b, v_od_ssm_dt_bias, v_od_ssm_a_log, v_od_ssm_d, v_od_ssm_norm, v_od_cm_conv_w, v_od_cm_conv_b, v_od_cm_ln_g, v_od_cm_ln_b, v_od_w_out, v_norm_mix_pre, v_norm_mix_post, v_norm_mlp_pre, v_norm_mlp_post, v_norm_ple, v_mlp_w1, v_mlp_w2, v_ple_w_proj, v_ple_w_gate):
    ws = [ev_w_in, ev_lru_conv_w, ev_lru_conv_b, ev_lru_gate_a_w, ev_lru_gate_a_b, ev_lru_gate_x_w, ev_lru_gate_x_b, ev_lru_lambda, ev_w_out, od_w_in, od_ssm_conv_w, od_ssm_conv_b, od_ssm_dt_bias, od_ssm_a_log, od_ssm_d, od_ssm_norm, od_cm_conv_w, od_cm_conv_b, od_cm_ln_g, od_cm_ln_b, od_w_out, norm_mix_pre, norm_mix_post, norm_mlp_pre, norm_mlp_post, norm_ple, mlp_w1, mlp_w2, ple_w_proj, ple_w_gate]
    ms = [m_ev_w_in, m_ev_lru_conv_w, m_ev_lru_conv_b, m_ev_lru_gate_a_w, m_ev_lru_gate_a_b, m_ev_lru_gate_x_w, m_ev_lru_gate_x_b, m_ev_lru_lambda, m_ev_w_out, m_od_w_in, m_od_ssm_conv_w, m_od_ssm_conv_b, m_od_ssm_dt_bias, m_od_ssm_a_log, m_od_ssm_d, m_od_ssm_norm, m_od_cm_conv_w, m_od_cm_conv_b, m_od_cm_ln_g, m_od_cm_ln_b, m_od_w_out, m_norm_mix_pre, m_norm_mix_post, m_norm_mlp_pre, m_norm_mlp_post, m_norm_ple, m_mlp_w1, m_mlp_w2, m_ple_w_proj, m_ple_w_gate]
    vs = [v_ev_w_in, v_ev_lru_conv_w, v_ev_lru_conv_b, v_ev_lru_gate_a_w, v_ev_lru_gate_a_b, v_ev_lru_gate_x_w, v_ev_lru_gate_x_b, v_ev_lru_lambda, v_ev_w_out, v_od_w_in, v_od_ssm_conv_w, v_od_ssm_conv_b, v_od_ssm_dt_bias, v_od_ssm_a_log, v_od_ssm_d, v_od_ssm_norm, v_od_cm_conv_w, v_od_cm_conv_b, v_od_cm_ln_g, v_od_cm_ln_b, v_od_w_out, v_norm_mix_pre, v_norm_mix_post, v_norm_mlp_pre, v_norm_mlp_post, v_norm_ple, v_mlp_w1, v_mlp_w2, v_ple_w_proj, v_ple_w_gate]
    w = dict(zip(W_NAMES, ws))
    m = dict(zip(W_NAMES, ms))
    v = dict(zip(W_NAMES, vs))
    full = _gather_weights(w)
    loss_local, grad_x, grads = local_step(x, p, loss_target, full)
    loss = lax.psum(loss_local, ("x", "y", "c"))
    return (loss, grad_x, *_reduce_and_update(grads, w, m, v))


def _reduce_and_update(grads, w, m, v):
    mx, my, mc = _mesh_pos()

    parts = []
    for k, axis in SHARDED.items():
        g = grads[k]
        shard = g.shape[axis] // N_DEV
        g = g.reshape(g.shape[:axis] + (N_DEV, shard) + g.shape[axis + 1:])
        parts.append(jnp.moveaxis(g, axis, 0).reshape(N_DEV, -1))
    parts = jnp.concatenate(parts, axis=1)
    rows = _round_up(parts.shape[1], LANES * ROW_TILE) // LANES
    parts = jnp.pad(parts, ((0, 0), (0, rows * LANES - parts.shape[1]))).reshape(N_DEV, rows, LANES)
    from_sibling = scatter_to_sibling(parts, name="scatter_sibling")
    core = jnp.reshape(mc, (1,)).astype(jnp.int32)
    chip_sums, chip_sums_ici = add_sibling_parts(parts, from_sibling, core, name="add_sibling")
    from_chips = scatter_to_chips(chip_sums_ici, name="scatter_chips")
    chip = jnp.reshape(2 * mx + my, (1,)).astype(jnp.int32)
    outs = adamw_sharded(chip_sums, from_chips, chip, _pack_sharded(w), _pack_sharded(m), _pack_sharded(v),
                         name="adamw_sharded")
    shard_shapes = {k: w[k].shape for k in SHARDED}
    g_sh, d_sh, m_sh, v_sh = [_unpack(o.reshape(-1), shard_shapes) for o in outs]

    rep_parts = all_gather(_pack_replicated(grads), name="gather_replicated_grads")
    outs = adamw_replicated(rep_parts, _pack_replicated(w), _pack_replicated(m), _pack_replicated(v),
                            name="adamw_replicated")
    rep_shapes = {k: w[k].shape for k in REPLICATED}
    g_rp, d_rp, m_rp, v_rp = [_unpack(o.reshape(-1), rep_shapes) for o in outs]

    pick = lambda sh, rp: [sh[k] if k in SHARDED else rp[k] for k in W_NAMES]
    return [*pick(g_sh, g_rp), *pick(d_sh, d_rp), *pick(m_sh, m_rp), *pick(v_sh, v_rp)]
```

```python
import functools
import math

import jax
import jax.numpy as jnp
from jax import lax
from jax.experimental import pallas as pl
from jax.experimental.pallas import tpu as pltpu

f32 = jnp.float32
bf16 = jnp.bfloat16
MXU_DTYPE = jnp.bfloat16

D_MODEL = 1024
EPS = 1e-6
LRU_WIDTH = 512
LRU_HEADS = 8
LRU_CONV = 4
LRU_C = 8.0
SB_WIDTH = 512
SB_HEAD_DIM = 64
SSM_WIDTH = 1024
SSM_HEADS = 16
SSM_HEAD_DIM = 64
SSM_GROUPS = 2
SSM_STATE = 128
SSM_CONV = 4
SSM_CHUNK = 128
SSM_XBC = SSM_WIDTH + 2 * SSM_GROUPS * SSM_STATE
CONF_WIDTH = 512
CONF_KERNEL = 31
MLP_HIDDEN = 4096
PLE_DIM = 256
LANES = 128
N_DEV = 8

ADAM_LR = 0.001
ADAM_B1 = 0.9
ADAM_B2 = 0.999
ADAM_EPS = 1e-08
ADAM_WD = 0.01
ADAM_STEP = 10

VMEM_LIMIT = 56 * 1024 * 1024


def _cparams(*sem):
    return pltpu.CompilerParams(dimension_semantics=sem, vmem_limit_bytes=VMEM_LIMIT)


def _mm(a, b):
    return jnp.dot(a.astype(MXU_DTYPE), b.astype(MXU_DTYPE), preferred_element_type=f32)


def _mm_nt(a, b):
    return lax.dot_general(a.astype(MXU_DTYPE), b.astype(MXU_DTYPE), (((1,), (1,)), ((), ())),
                           preferred_element_type=f32)


def _mm_tn(a, b):
    return lax.dot_general(a.astype(MXU_DTYPE), b.astype(MXU_DTYPE), (((0,), (0,)), ((), ())),
                           preferred_element_type=f32)


def _mm_exact(a, b):
    return jnp.dot(a, b, preferred_element_type=f32, precision=lax.Precision.HIGHEST)


@jax.custom_vjp
def dmm(a, b):
    return _mm(a, b)


def _dmm_fwd(a, b):
    return _mm(a, b), (a, b)


def _dmm_bwd(res, g):
    a, b = res
    return _mm_nt(g, b), _mm_tn(a, g)


dmm.defvjp(_dmm_fwd, _dmm_bwd)


@jax.custom_vjp
def dmm_nt(a, b):
    return _mm_nt(a, b)


def _dmm_nt_fwd(a, b):
    return _mm_nt(a, b), (a, b)


def _dmm_nt_bwd(res, g):
    a, b = res
    return _mm(g, b), _mm_tn(g, a)


dmm_nt.defvjp(_dmm_nt_fwd, _dmm_nt_bwd)


@jax.custom_vjp
def dmm_tn(a, b):
    return _mm_tn(a, b)


def _dmm_tn_fwd(a, b):
    return _mm_tn(a, b), (a, b)


def _dmm_tn_bwd(res, g):
    a, b = res
    return _mm_nt(b, g), _mm(a, g)


dmm_tn.defvjp(_dmm_tn_fwd, _dmm_tn_bwd)


def _rms(x, g):
    r = lax.rsqrt(jnp.mean(x * x, axis=-1, keepdims=True) + EPS)
    return x * r * g


def _rms_bwd(dy, x, g):
    r = lax.rsqrt(jnp.mean(x * x, axis=-1, keepdims=True) + EPS)
    dyg = dy * g
    dx = r * dyg - x * (r * r * r * jnp.mean(dyg * x, axis=-1, keepdims=True))
    return dx, dy * x * r


def _tok(tm, n):
    return pl.BlockSpec((tm, n), lambda i: (i, 0))


def _whole(shape):
    nd = len(shape)
    return pl.BlockSpec(tuple(shape), lambda i: (0,) * nd)


def _acc_rows(ref, val):
    s = jnp.sum(val, axis=0, keepdims=True)

    @pl.when(pl.program_id(0) == 0)
    def _():
        ref[...] = s

    @pl.when(pl.program_id(0) != 0)
    def _():
        ref[...] += s


TOKEN_TILE = 256


def norm_matmul(h, g, ws, out_dtypes, name):
    t, d = h.shape
    tm = TOKEN_TILE
    nw = len(ws)

    def body(h_ref, g_ref, *refs):
        hn = _rms(h_ref[...], g_ref[...]).astype(MXU_DTYPE)
        for w_ref, o_ref in zip(refs[:nw], refs[nw:]):
            o_ref[...] = jnp.dot(hn, w_ref[...], preferred_element_type=f32).astype(o_ref.dtype)

    return pl.pallas_call(
        body, name=name, grid=(t // tm,),
        in_specs=[_tok(tm, d), _whole(g.shape)] + [_whole(w.shape) for w in ws],
        out_specs=[_tok(tm, w.shape[1]) for w in ws],
        out_shape=[jax.ShapeDtypeStruct((t, w.shape[1]), dt) for w, dt in zip(ws, out_dtypes)],
        compiler_params=_cparams("parallel"),
    )(h, g, *ws)


def matmul_residual_norm(xs, ws, h, g, name, relu2=False):
    t, d = h.shape
    tm = TOKEN_TILE
    nx = len(xs)

    def body(*refs):
        x_refs, w_refs = refs[:nx], refs[nx:2 * nx]
        h_ref, g_ref, ho_ref, m_ref = refs[2 * nx:]
        m = None
        for x_ref, w_ref in zip(x_refs, w_refs):
            x = x_ref[...]
            if relu2:
                x = jnp.square(jnp.maximum(x.astype(f32), 0.0))
            part = jnp.dot(x.astype(MXU_DTYPE), w_ref[...], preferred_element_type=f32)
            m = part if m is None else m + part
        m_ref[...] = m
        ho_ref[...] = h_ref[...] + _rms(m, g_ref[...])

    return pl.pallas_call(
        body, name=name, grid=(t // tm,),
        in_specs=[_tok(tm, x.shape[1]) for x in xs] + [_whole(w.shape) for w in ws] + [_tok(tm, d), _whole(g.shape)],
        out_specs=[_tok(tm, d), _tok(tm, d)],
        out_shape=[jax.ShapeDtypeStruct((t, d), f32), jax.ShapeDtypeStruct((t, d), f32)],
        compiler_params=_cparams("parallel"),
    )(*xs, *ws, h, g)


def ple_forward(h, p, w_gate, w_proj, g, name):
    t, d = h.shape
    tm = TOKEN_TILE

    def body(h_ref, p_ref, wg_ref, wp_ref, g_ref, ho_ref, gl_ref, emb_ref):
        hh = h_ref[...]
        gl = jnp.dot(hh.astype(MXU_DTYPE), wg_ref[...], preferred_element_type=f32)
        emb = jnp.dot(p_ref[...].astype(MXU_DTYPE), wp_ref[...], preferred_element_type=f32)
        gl_ref[...] = gl
        emb_ref[...] = emb
        ho_ref[...] = hh + _rms(jax.nn.sigmoid(gl) * emb, g_ref[...])

    return pl.pallas_call(
        body, name=name, grid=(t // tm,),
        in_specs=[_tok(tm, d), _tok(tm, p.shape[1]), _whole(w_gate.shape), _whole(w_proj.shape), _whole(g.shape)],
        out_specs=[_tok(tm, d)] * 3,
        out_shape=[jax.ShapeDtypeStruct((t, d), f32)] * 3,
        compiler_params=_cparams("parallel"),
    )(h, p, w_gate, w_proj, g)


def loss_and_grad(h, target, name):
    t, d = h.shape
    tm = TOKEN_TILE

    def body(h_ref, t_ref, l_ref, dh_ref):
        e = h_ref[...] - t_ref[...]
        dh_ref[...] = e * (1.0 / d)
        part = jnp.sum(jnp.sum(e * e, axis=1, keepdims=True), axis=0, keepdims=True) * (0.5 / d)
        _acc_rows(l_ref, jnp.broadcast_to(part, (1, LANES)))

    return pl.pallas_call(
        body, name=name, grid=(t // tm,),
        in_specs=[_tok(tm, d), _tok(tm, d)],
        out_specs=[_whole((1, LANES)), _tok(tm, d)],
        out_shape=[jax.ShapeDtypeStruct((1, LANES), f32), jax.ShapeDtypeStruct((t, d), f32)],
        compiler_params=_cparams("arbitrary"),
    )(h, target)


def bwd_through_norm_in(dh, gs, wts, h, g, name):
    t, d = h.shape
    tm = TOKEN_TILE
    ng = len(gs)

    def body(*refs):
        dh_ref = refs[0]
        g_refs, w_refs = refs[1:1 + ng], refs[1 + ng:1 + 2 * ng]
        h_ref, gain_ref, dho_ref, dg_ref = refs[1 + 2 * ng:]
        dhn = None
        for g_ref, w_ref in zip(g_refs, w_refs):
            part = jnp.dot(g_ref[...].astype(MXU_DTYPE), w_ref[...], preferred_element_type=f32)
            dhn = part if dhn is None else dhn + part
        dx, dgr = _rms_bwd(dhn, h_ref[...], gain_ref[...])
        dho_ref[...] = dh_ref[...] + dx
        _acc_rows(dg_ref, dgr)

    return pl.pallas_call(
        body, name=name, grid=(t // tm,),
        in_specs=[_tok(tm, d)] + [_tok(tm, x.shape[1]) for x in gs] + [_whole(w.shape) for w in wts]
        + [_tok(tm, d), _whole(g.shape)],
        out_specs=[_tok(tm, d), _whole((1, d))],
        out_shape=[jax.ShapeDtypeStruct((t, d), f32), jax.ShapeDtypeStruct((1, d), f32)],
        compiler_params=_cparams("arbitrary"),
    )(dh, *gs, *wts, h, g)


def bwd_through_norm_out(dh, n, g, wts, out_dtypes, name, relu2_of=None):
    t, d = n.shape
    tm = TOKEN_TILE
    nw = len(wts)
    has_a = relu2_of is not None

    def body(*refs):
        dh_ref, n_ref, gain_ref = refs[:3]
        w_refs = refs[3:3 + nw]
        rest = refs[3 + nw:]
        if has_a:
            a_ref, rest = rest[0], rest[1:]
        dn_ref, dx_refs, dg_ref = rest[0], rest[1:1 + nw], rest[1 + nw]
        dn, dgr = _rms_bwd(dh_ref[...], n_ref[...], gain_ref[...])
        dnb = dn.astype(MXU_DTYPE)
        dn_ref[...] = dnb.astype(dn_ref.dtype)
        for w_ref, dx_ref in zip(w_refs, dx_refs):
            dx = jnp.dot(dnb, w_ref[...], preferred_element_type=f32)
            if has_a:
                dx = dx * (2.0 * jnp.maximum(a_ref[...].astype(f32), 0.0))
            dx_ref[...] = dx.astype(dx_ref.dtype)
        _acc_rows(dg_ref, dgr)

    ins = [dh, n, g, *wts] + ([relu2_of] if has_a else [])
    in_specs = [_tok(tm, d), _tok(tm, d), _whole(g.shape)] + [_whole(w.shape) for w in wts]
    if has_a:
        in_specs.append(_tok(tm, relu2_of.shape[1]))
    outs = pl.pallas_call(
        body, name=name, grid=(t // tm,),
        in_specs=in_specs,
        out_specs=[_tok(tm, d)] + [_tok(tm, w.shape[1]) for w in wts] + [_whole((1, d))],
        out_shape=[jax.ShapeDtypeStruct((t, d), MXU_DTYPE)]
        + [jax.ShapeDtypeStruct((t, w.shape[1]), dt) for w, dt in zip(wts, out_dtypes)]
        + [jax.ShapeDtypeStruct((1, d), f32)],
        compiler_params=_cparams("arbitrary"),
    )(*ins)
    return outs[0], list(outs[1:1 + nw]), outs[1 + nw]


def ple_backward(dh3, h2, gl, emb, g, w_gate_t, name):
    t, d = h2.shape
    tm = TOKEN_TILE

    def body(dh_ref, gl_ref, emb_ref, gain_ref, wt_ref, dho_ref, dgl_ref, demb_ref, dg_ref):
        gate = jax.nn.sigmoid(gl_ref[...])
        emb = emb_ref[...]
        dge, dgr = _rms_bwd(dh_ref[...], gate * emb, gain_ref[...])
        demb_ref[...] = (dge * gate).astype(demb_ref.dtype)
        dgl = (dge * emb * gate * (1.0 - gate)).astype(MXU_DTYPE)
        dgl_ref[...] = dgl.astype(dgl_ref.dtype)
        dho_ref[...] = dh_ref[...] + jnp.dot(dgl, wt_ref[...], preferred_element_type=f32)
        _acc_rows(dg_ref, dgr)

    return pl.pallas_call(
        body, name=name, grid=(t // tm,),
        in_specs=[_tok(tm, d), _tok(tm, d), _tok(tm, d), _whole(g.shape), _whole(w_gate_t.shape)],
        out_specs=[_tok(tm, d), _tok(tm, d), _tok(tm, d), _whole((1, d))],
        out_shape=[jax.ShapeDtypeStruct((t, d), f32), jax.ShapeDtypeStruct((t, d), MXU_DTYPE),
                   jax.ShapeDtypeStruct((t, d), MXU_DTYPE), jax.ShapeDtypeStruct((1, d), f32)],
        compiler_params=_cparams("arbitrary"),
    )(dh3, gl, emb, g, w_gate_t)


def _largest_tile(n, cap):
    if n <= cap:
        return n
    return max(c for c in range(LANES, cap + 1, LANES) if n % c == 0)


def weight_grad(x, gout, name, prologue="none", gain=None):
    t, k = x.shape
    n = gout.shape[1]
    tt = 512
    tn = _largest_tile(n, 1024)
    tk = k if prologue == "rms" else _largest_tile(k, 1024)
    has_gain = prologue == "rms"

    def body(*refs):
        if has_gain:
            x_ref, gain_ref, g_ref, o_ref = refs
        else:
            x_ref, g_ref, o_ref = refs
        x = x_ref[...].astype(f32)
        if prologue == "relu2":
            x = jnp.square(jnp.maximum(x, 0.0))
        elif prologue == "rms":
            x = _rms(x, gain_ref[...])
        part = _mm_tn(x, g_ref[...])

        @pl.when(pl.program_id(2) == 0)
        def _():
            o_ref[...] = part

        @pl.when(pl.program_id(2) != 0)
        def _():
            o_ref[...] += part

    in_specs = [pl.BlockSpec((tt, tk), lambda i, j, s: (s, i))]
    ins = [x]
    if has_gain:
        in_specs.append(pl.BlockSpec(gain.shape, lambda i, j, s: (0, 0)))
        ins.append(gain)
    in_specs.append(pl.BlockSpec((tt, tn), lambda i, j, s: (s, j)))
    ins.append(gout)
    return pl.pallas_call(
        body, name=name, grid=(k // tk, n // tn, t // tt),
        in_specs=in_specs,
        out_specs=pl.BlockSpec((tk, tn), lambda i, j, s: (i, j)),
        out_shape=jax.ShapeDtypeStruct((k, n), f32),
        compiler_params=_cparams("parallel", "parallel", "arbitrary"),
    )(*ins)


SEQ_TILE = 256
HALO = 8


def _first_step():
    return jnp.logical_and(pl.program_id(0) == 0, pl.program_id(1) == 0)


def _accum(ref, val, first):
    @pl.when(first)
    def _():
        ref[...] = val

    @pl.when(jnp.logical_not(first))
    def _():
        ref[...] += val


def _softplus(x):
    return jnp.maximum(x, 0.0) + jnp.log1p(jnp.exp(-jnp.abs(x)))


def _neg_expm1(z):
    series = -z * (1.0 + z * (0.5 + z * (1.0 / 6.0 + z * (1.0 / 24.0 + z * (1.0 / 120.0)))))
    return jnp.where(z > -0.05, series, 1.0 - jnp.exp(z))


def _lru_gates(xc, ga, gab, gx, gxb, lam):
    r = jax.nn.sigmoid(dmm(xc, ga) + gab)
    i = jax.nn.sigmoid(dmm(xc, gx) + gxb)
    log_a = -LRU_C * r * _softplus(-lam)
    a = jnp.exp(log_a)
    u = jnp.sqrt(_neg_expm1(2.0 * log_a)) * (i * xc)
    return a, u


def _scan_down(a, u):
    n = a.shape[0]
    rows = lax.broadcasted_iota(jnp.int32, a.shape, 0)
    d = 1
    while d < n:
        keep = rows >= d
        a_s = jnp.where(keep, pltpu.roll(a, d, 0), 1.0)
        u_s = jnp.where(keep, pltpu.roll(u, d, 0), 0.0)
        u = a * u_s + u
        a = a * a_s
        d *= 2
    return a, u


def _scan_up(b, g):
    n = b.shape[0]
    rows = lax.broadcasted_iota(jnp.int32, b.shape, 0)
    d = 1
    while d < n:
        keep = rows < n - d
        b_s = jnp.where(keep, pltpu.roll(b, n - d, 0), 1.0)
        g_s = jnp.where(keep, pltpu.roll(g, n - d, 0), 0.0)
        g = g + b * g_s
        b = b * b_s
        d *= 2
    return g


def _seq_specs(ts, c, nt, reverse=False):
    per = ts // HALO

    def jj(j):
        return (nt - 1 - j) if reverse else j

    tile = pl.BlockSpec((1, ts, c), lambda b, j: (b, jj(j), 0))
    before = pl.BlockSpec((1, HALO, c), lambda b, j: (b, jnp.maximum(jj(j) * per - 1, 0), 0))
    after = pl.BlockSpec((1, HALO, c), lambda b, j: (b, jnp.minimum((jj(j) + 1) * per, nt * per - 1), 0))
    return tile, before, after


def _const2(shape):
    nd = len(shape)
    return pl.BlockSpec(tuple(shape), lambda b, j: (0,) * nd)


def lru_forward(xpre, gate, cw, cb, ga, gab, gx, gxb, lam, name):
    nb, ns, w = xpre.shape
    ts = SEQ_TILE
    nt = ns // ts
    tile, _, _ = _seq_specs(ts, w, nt)

    def body(xp_ref, gt_ref, cw_ref, cb_ref, ga_ref, gab_ref, gx_ref, gxb_ref, lam_ref,
             y_ref, xc_ref, hs_ref, xin, hcar):
        @pl.when(pl.program_id(1) == 0)
        def _():
            xin[0:HALO, :] = jnp.zeros((HALO, w), f32)
            hcar[...] = jnp.zeros_like(hcar)

        xin[HALO:HALO + ts, :] = xp_ref[0]
        xc = jnp.broadcast_to(cb_ref[...], (ts, w))
        for k in range(LRU_CONV):
            xc = xc + cw_ref[k:k + 1, :] * xin[pl.ds(HALO - LRU_CONV + 1 + k, ts), :]
        xin[0:HALO, :] = xin[ts:ts + HALO, :]
        a, u = _lru_gates(xc, ga_ref[...], gab_ref[...], gx_ref[...], gxb_ref[...], lam_ref[...])
        acum, h = _scan_down(a, u)
        h = h + acum * hcar[0:1, :]
        hcar[0:1, :] = h[ts - 1:ts, :]
        xc_ref[0] = xc
        hs_ref[0] = h
        y_ref[0] = (h * jax.nn.gelu(gt_ref[0])).astype(y_ref.dtype)

    params = [cw, cb, ga, gab, gx, gxb, lam]
    return pl.pallas_call(
        body, name=name, grid=(nb, nt),
        in_specs=[tile, tile] + [_const2(p.shape) for p in params],
        out_specs=[tile, tile, tile],
        out_shape=[jax.ShapeDtypeStruct((nb, ns, w), MXU_DTYPE), jax.ShapeDtypeStruct((nb, ns, w), f32),
                   jax.ShapeDtypeStruct((nb, ns, w), f32)],
        scratch_shapes=[pltpu.VMEM((ts + HALO, w), f32), pltpu.VMEM((HALO, w), f32)],
        compiler_params=_cparams("arbitrary", "arbitrary"),
    )(xpre, gate, *params)


def lru_backward(dy, xpre, gate, xc, hs, cw, cb, ga, gab, gx, gxb, lam, name):
    nb, ns, w = xpre.shape
    ts = SEQ_TILE
    nt = ns // ts
    tile, before, _ = _seq_specs(ts, w, nt, reverse=True)

    def body(dy_ref, xp_ref, xpb_ref, gt_ref, xc_ref, hs_ref, hsb_ref,
             cw_ref, cb_ref, ga_ref, gab_ref, gx_ref, gxb_ref, lam_ref,
             dxp_ref, dgt_ref, dcw_ref, dcb_ref, dga_ref, dgab_ref, dgx_ref, dgxb_ref, dlam_ref,
             dxc_ext, gcar, xin):
        j = pl.program_id(1)
        first = _first_step()
        at_seq_start = j == nt - 1

        @pl.when(j == 0)
        def _():
            dxc_ext[ts:ts + HALO, :] = jnp.zeros((HALO, w), f32)
            gcar[...] = jnp.zeros_like(gcar)

        gt = gt_ref[0]
        h = hs_ref[0]
        dyv = dy_ref[0].astype(f32)
        gl, gelu_vjp = jax.vjp(jax.nn.gelu, gt)
        dgt_ref[0] = gelu_vjp(dyv * h)[0].astype(dgt_ref.dtype)
        dh = dyv * gl

        (a, _), gates_vjp = jax.vjp(_lru_gates, xc_ref[0], ga_ref[...], gab_ref[...], gx_ref[...], gxb_ref[...],
                                    lam_ref[...])
        rows = lax.broadcasted_iota(jnp.int32, (ts, w), 0)
        dh = dh + jnp.where(rows == ts - 1, gcar[0:1, :], 0.0)
        b = pltpu.roll(a, ts - 1, 0)
        g = _scan_up(b, dh)
        gcar[0:1, :] = a[0:1, :] * g[0:1, :]
        hprev_row = jnp.where(at_seq_start, 0.0, hsb_ref[0][HALO - 1:HALO, :])
        hprev = jnp.where(rows == 0, hprev_row, pltpu.roll(h, 1, 0))
        dxc, dga, dgab, dgx, dgxb, dlam = gates_vjp((g * hprev, g))

        _accum(dga_ref, dga, first)
        _accum(dgx_ref, dgx, first)
        _accum(dgab_ref, dgab, first)
        _accum(dgxb_ref, dgxb, first)
        _accum(dlam_ref, dlam, first)
        _accum(dcb_ref, jnp.sum(dxc, axis=0, keepdims=True), first)

        dxc_ext[0:ts, :] = dxc
        dxp = jnp.zeros((ts, w), f32)
        for k in range(LRU_CONV):
            dxp = dxp + cw_ref[k:k + 1, :] * dxc_ext[pl.ds(LRU_CONV - 1 - k, ts), :]
        dxp_ref[0] = dxp.astype(dxp_ref.dtype)
        dxc_ext[ts:ts + HALO, :] = dxc[0:HALO, :]

        xin[0:HALO, :] = jnp.where(at_seq_start, 0.0, xpb_ref[0])
        xin[HALO:HALO + ts, :] = xp_ref[0]
        dcw_rows = [jnp.sum(dxc * xin[pl.ds(HALO - LRU_CONV + 1 + k, ts), :], axis=0, keepdims=True)
                    for k in range(LRU_CONV)]
        dcw_rows += [jnp.zeros((1, w), f32)] * (HALO - LRU_CONV)
        _accum(dcw_ref, jnp.concatenate(dcw_rows, axis=0), first)

    params = [cw, cb, ga, gab, gx, gxb, lam]
    pshape = lambda p: jax.ShapeDtypeStruct(p.shape, f32)
    outs = pl.pallas_call(
        body, name=name, grid=(nb, nt),
        in_specs=[tile, tile, before, tile, tile, tile, before] + [_const2(p.shape) for p in params],
        out_specs=[tile, tile, _const2((HALO, w))] + [_const2(p.shape) for p in params[1:]],
        out_shape=[jax.ShapeDtypeStruct((nb, ns, w), MXU_DTYPE), jax.ShapeDtypeStruct((nb, ns, w), MXU_DTYPE),
                   jax.ShapeDtypeStruct((HALO, w), f32)] + [pshape(p) for p in params[1:]],
        scratch_shapes=[pltpu.VMEM((ts + HALO, w), f32), pltpu.VMEM((HALO, w), f32),
                        pltpu.VMEM((ts + HALO, w), f32)],
        compiler_params=_cparams("arbitrary", "arbitrary"),
    )(dy, xpre, xpre, gate, xc, hs, hs, *params)
    return outs


SB_TILE = 256


def _split_dot(x, m):
    hi = x.astype(MXU_DTYPE)
    lo = (x - hi.astype(f32)).astype(MXU_DTYPE)
    return jnp.dot(hi, m, preferred_element_type=f32) + jnp.dot(lo, m, preferred_element_type=f32)


def _suffix_matrices(n):
    r = lax.broadcasted_iota(jnp.int32, (n, n), 0)
    c = lax.broadcasted_iota(jnp.int32, (n, n), 1)
    return (r > c).astype(MXU_DTYPE), (r >= c).astype(MXU_DTYPE)


def _sb_logits(qh, kb, strict):
    z = _mm_nt(qh, kb)
    ls = jnp.minimum(z, 0.0) - jnp.log(1.0 + jnp.exp(-jnp.abs(z)))
    lk = ls - z
    if strict is not None:
        lk = jnp.where(strict, lk, 0.0)
    return ls, lk


def _head_masked(x, dtype):
    lane = lax.broadcasted_iota(jnp.int32, x.shape, 1)
    return (jnp.where(lane < SB_HEAD_DIM, x, 0.0).astype(dtype), jnp.where(lane >= SB_HEAD_DIM, x, 0.0).astype(dtype))


def _stack_heads(dst, x, tq):
    x0, x1 = _head_masked(x, dst.dtype)
    for blk in range(dst.shape[0]):
        dst[blk, 0:tq, :] = x0[blk * tq:(blk + 1) * tq]
        dst[blk, tq:2 * tq, :] = x1[blk * tq:(blk + 1) * tq]


def _strict_mask(tq):
    rr = lax.broadcasted_iota(jnp.int32, (2 * tq, tq), 0)
    cc = lax.broadcasted_iota(jnp.int32, (2 * tq, tq), 1)
    return cc < jnp.where(rr >= tq, rr - tq, rr)


def _sb_specs(ns):
    npair = SB_WIDTH // LANES
    q = pl.BlockSpec((1, ns, LANES), lambda b, p: (b, 0, p))
    k = pl.BlockSpec((1, ns, LANES), lambda b, p: (b, 0, npair + p))
    v = pl.BlockSpec((1, ns, LANES), lambda b, p: (b, 0, 2 * npair + p))
    return q, k, v, npair


def sb_forward(qkv, name):
    nb, ns, _ = qkv.shape
    tq = SB_TILE
    nq = ns // tq
    qspec, kspec, vspec, npair = _sb_specs(ns)

    def body(q_ref, k_ref, v_ref, o_ref, qs, ks, vs, acc):
        scale = 1.0 / math.sqrt(SB_HEAD_DIM)
        _stack_heads(qs, q_ref[0] * scale, tq)
        ks[...] = k_ref[0].astype(MXU_DTYPE)
        _stack_heads(vs, v_ref[0], tq)
        mx, _ = _suffix_matrices(tq)
        strict = _strict_mask(tq)

        def step(q2, blks, r2, masked):
            kbs = [ks[pl.ds(pl.multiple_of(b * tq, tq), tq), :] for b in blks]
            lg = [_sb_logits(q2, kb, strict if masked else None) for kb in kbs]
            sums = [jnp.dot(lk.astype(MXU_DTYPE), mx, preferred_element_type=f32) for _, lk in lg]
            total = None
            for (ls, lk), s, b in zip(lg, sums, blks):
                a = r2 + s
                w = jnp.exp(ls + a)
                if masked:
                    w = jnp.where(strict, w, 0.0)
                wb = w.astype(MXU_DTYPE)
                part = jnp.dot(jnp.concatenate([wb[:tq], wb[tq:]], axis=1), vs[b], preferred_element_type=f32)
                total = part if total is None else total + part
                r2 = a[:, 0:1] + lk[:, 0:1]
            acc[...] += total
            return r2

        def q_block(qi, carry):
            acc[...] = jnp.zeros_like(acc)
            q2 = qs[qi]
            r2 = step(q2, [qi], jnp.zeros((2 * tq, 1), f32), True)
            r2 = lax.fori_loop(0, lax.shift_right_logical(qi, 1),
                               lambda i, r: step(q2, [qi - 1 - 2 * i, qi - 2 - 2 * i], r, False), r2)
            lax.cond(jnp.bitwise_and(qi, 1) == 1, lambda r: step(q2, [0], r, False), lambda r: r, r2)
            o_ref[0, pl.ds(pl.multiple_of(qi * tq, tq), tq), :] = acc[...]
            return carry

        lax.fori_loop(0, nq, q_block, 0)

    return pl.pallas_call(
        body, name=name, grid=(nb, npair),
        in_specs=[qspec, kspec, vspec],
        out_specs=pl.BlockSpec((1, ns, LANES), lambda b, p: (b, 0, p)),
        out_shape=jax.ShapeDtypeStruct((nb, ns, SB_WIDTH), f32),
        scratch_shapes=[pltpu.VMEM((nq, 2 * tq, LANES), MXU_DTYPE), pltpu.VMEM((ns, LANES), MXU_DTYPE),
                        pltpu.VMEM((nq, 2 * tq, LANES), MXU_DTYPE), pltpu.VMEM((tq, LANES), f32)],
        compiler_params=_cparams("parallel", "parallel"),
    )(qkv, qkv, qkv)


def sb_backward(qkv, o, do, name):
    nb, ns, _ = qkv.shape
    tq = SB_TILE
    nq = ns // tq
    qspec, kspec, vspec, npair = _sb_specs(ns)
    ospec = pl.BlockSpec((1, ns, LANES), lambda b, p: (b, 0, p))

    def body(q_ref, k_ref, v_ref, o_ref, do_ref, dq_ref, dk_ref, dv_ref, qs, ks, kcat, vs, dos, dqacc, dkacc, dvacc):
        scale = 1.0 / math.sqrt(SB_HEAD_DIM)
        _stack_heads(qs, q_ref[0] * scale, tq)
        ks[...] = k_ref[0].astype(MXU_DTYPE)
        _stack_heads(kcat, k_ref[0], tq)
        vs[...] = v_ref[0].astype(MXU_DTYPE)
        _stack_heads(dos, do_ref[0].astype(f32), tq)
        dkacc[...] = jnp.zeros_like(dkacc)
        dvacc[...] = jnp.zeros_like(dvacc)
        mx, mi = _suffix_matrices(tq)
        strict = _strict_mask(tq)

        def step(q2, do2, dtot2, blks, carry, masked):
            r2, g2 = carry
            k0s = [pl.multiple_of(b * tq, tq) for b in blks]
            lg = [_sb_logits(q2, ks[pl.ds(k0, tq), :], strict if masked else None) for k0 in k0s]
            dws = [_mm_nt(do2, vs[pl.ds(k0, tq), :]) for k0 in k0s]
            sums = [jnp.dot(lk.astype(MXU_DTYPE), mx, preferred_element_type=f32) for _, lk in lg]
            wbs, es = [], []
            for (ls, lk), s in zip(lg, sums):
                a = r2 + s
                w = jnp.exp(ls + a)
                if masked:
                    w = jnp.where(strict, w, 0.0)
                wbs.append(w.astype(MXU_DTYPE))
                r2 = a[:, 0:1] + lk[:, 0:1]
            es = [wb.astype(f32) * dw for wb, dw in zip(wbs, dws)]
            esums = [_split_dot(e, mi) for e in es]
            dq = None
            for (ls, _), e, esum, wb, b, k0 in zip(lg, es, esums, wbs, blks, k0s):
                esuf = g2 + esum
                beta = jnp.exp(ls)
                dz = e * (1.0 - beta) - beta * (dtot2 - esuf)
                if masked:
                    dz = jnp.where(strict, dz, 0.0)
                dzb = dz.astype(MXU_DTYPE)
                part = jnp.dot(jnp.concatenate([dzb[:tq], dzb[tq:]], axis=1), kcat[b], preferred_element_type=f32)
                dq = part if dq is None else dq + part
                dkacc[pl.ds(k0, tq), :] += _mm_tn(dzb, q2)
                dvacc[pl.ds(k0, tq), :] += _mm_tn(wb, do2)
                g2 = esuf[:, 0:1]
            dqacc[...] += dq
            return r2, g2

        def q_block(qi, carry):
            dqacc[...] = jnp.zeros_like(dqacc)
            q2, do2 = qs[qi], dos[qi]
            ov = o_ref[0, pl.ds(pl.multiple_of(qi * tq, tq), tq), :]
            dtot2 = jnp.sum(do2.astype(f32) * jnp.concatenate([ov, ov], axis=0), axis=1, keepdims=True)
            zero = jnp.zeros((2 * tq, 1), f32)
            c = step(q2, do2, dtot2, [qi], (zero, zero), True)
            c = lax.fori_loop(0, lax.shift_right_logical(qi, 1),
                              lambda i, c: step(q2, do2, dtot2, [qi - 1 - 2 * i, qi - 2 - 2 * i], c, False), c)
            lax.cond(jnp.bitwise_and(qi, 1) == 1, lambda c: step(q2, do2, dtot2, [0], c, False), lambda c: c, c)
            dq_ref[0, pl.ds(pl.multiple_of(qi * tq, tq), tq), :] = (dqacc[...] * scale).astype(dq_ref.dtype)
            return carry

        lax.fori_loop(0, nq, q_block, 0)
        dk_ref[0] = dkacc[...].astype(dk_ref.dtype)
        dv_ref[0] = dvacc[...].astype(dv_ref.dtype)

    dshape = jax.ShapeDtypeStruct((nb, ns, SB_WIDTH), MXU_DTYPE)
    stacked = pltpu.VMEM((nq, 2 * tq, LANES), MXU_DTYPE)
    flat = pltpu.VMEM((ns, LANES), MXU_DTYPE)
    return pl.pallas_call(
        body, name=name, grid=(nb, npair),
        in_specs=[qspec, kspec, vspec, ospec, ospec],
        out_specs=[ospec, ospec, ospec],
        out_shape=[dshape, dshape, dshape],
        scratch_shapes=[stacked, flat, stacked, flat, stacked,
                        pltpu.VMEM((tq, LANES), f32), pltpu.VMEM((ns, LANES), f32), pltpu.VMEM((ns, LANES), f32)],
        compiler_params=_cparams("parallel", "parallel"),
    )(qkv, qkv, qkv, o, do)


SSM_PAIRS = SSM_HEADS // 2
PAIRS_PER_GROUP = SSM_PAIRS // SSM_GROUPS
GROUP_WIDTH = SSM_WIDTH // SSM_GROUPS


def _silu(x):
    return x * jax.nn.sigmoid(x)


def _ssd_chunk(xs_pre, b_pre, c_pre, dt_raw, dt_raw_t, z, st, dt_bias_r, dt_bias_c, a_log_r, a_log_c, d_skip,
               gains):
    n = dt_raw.shape[0]
    rows = lax.broadcasted_iota(jnp.int32, (n, n), 0)
    cols = lax.broadcasted_iota(jnp.int32, (n, n), 1)
    tril = cols <= rows
    tri_l = tril.astype(f32)
    tri_u = (rows <= cols).astype(f32)
    lane = lax.broadcasted_iota(jnp.int32, (n, LANES), 1)
    sub = lax.broadcasted_iota(jnp.int32, (LANES, n), 0)

    dt = _softplus(dt_raw + dt_bias_r)
    a_r = -jnp.exp(a_log_r)
    da = dt * a_r
    acs = _mm_exact(tri_l, da)
    dt_t = _softplus(dt_raw_t + dt_bias_c)
    acs_t = _mm_exact(dt_t * (-jnp.exp(a_log_c)), tri_u)

    bs = [_silu(b) for b in b_pre]
    cs = [_silu(c) for c in c_pre]
    cb = [dmm_nt(cs[g], bs[g]) for g in range(SSM_GROUPS)]

    end = jnp.sum(da, axis=0, keepdims=True)
    lane_row = lax.broadcasted_iota(jnp.int32, (1, LANES), 1)
    first_head = lane < SSM_HEAD_DIM
    first_head_row = lane_row < SSM_HEAD_DIM

    def head_col(v, h):
        return jnp.sum(jnp.where((lane if v.shape[0] == n else lane_row) == h, v, 0.0), axis=1, keepdims=True)

    ys, st_new = [], []
    for p in range(SSM_PAIRS):
        g = p // PAIRS_PER_GROUP
        h0, h1 = 2 * p, 2 * p + 1
        xs = _silu(xs_pre[p])
        acols = [head_col(acs, h0), head_col(acs, h1)]
        dt_p = jnp.where(first_head, head_col(dt, h0), head_col(dt, h1))
        acs_p = jnp.where(first_head, acols[0], acols[1])
        end_p = jnp.where(first_head_row, head_col(end, h0), head_col(end, h1))
        dsk_p = jnp.where(first_head_row, head_col(d_skip, h0), head_col(d_skip, h1))
        xdt = xs * dt_p
        y = jnp.exp(acs_p) * dmm(cs[g], st[p])
        for hh in range(2):
            row = jnp.sum(jnp.where(sub == 2 * p + hh, acs_t, 0.0), axis=0, keepdims=True)
            decay = jnp.where(tril, jnp.exp(jnp.where(tril, acols[hh] - row, 0.0)), 0.0)
            head = first_head if hh == 0 else jnp.logical_not(first_head)
            y = y + dmm(cb[g] * decay, jnp.where(head, xdt, 0.0))
        st_new.append(jnp.exp(end_p) * st[p] + dmm_tn(bs[g], xdt * jnp.exp(end_p - acs_p)))
        ys.append(y + dsk_p * xs)
    out = []
    for g in range(SSM_GROUPS):
        yg = jnp.concatenate(ys[g * PAIRS_PER_GROUP:(g + 1) * PAIRS_PER_GROUP], axis=1) * _silu(z[g])
        out.append(_rms(yg, gains[g]))
    return out, st_new


def _ssd_chunk_inputs(xconv, dtr, z, st_ref, gain):
    xs_pre = [xconv[:, LANES * p:LANES * (p + 1)] for p in range(SSM_PAIRS)]
    b0 = SSM_WIDTH
    c0 = SSM_WIDTH + SSM_GROUPS * SSM_STATE
    b_pre = [xconv[:, b0 + SSM_STATE * g:b0 + SSM_STATE * (g + 1)] for g in range(SSM_GROUPS)]
    c_pre = [xconv[:, c0 + SSM_STATE * g:c0 + SSM_STATE * (g + 1)] for g in range(SSM_GROUPS)]
    zs = [z[:, GROUP_WIDTH * g:GROUP_WIDTH * (g + 1)] for g in range(SSM_GROUPS)]
    sts = [st_ref[p] for p in range(SSM_PAIRS)]
    gains = [gain[:, GROUP_WIDTH * g:GROUP_WIDTH * (g + 1)] for g in range(SSM_GROUPS)]
    return xs_pre, b_pre, c_pre, dtr, dtr.T, zs, sts, gains


def ssd_forward(xbc, dt_raw, z, cw, cb, dbr, dbc, alr, alc, dsk, gain, name):
    nb, ns, wx = xbc.shape
    ln = SSM_CHUNK
    nt = ns // ln
    tile = lambda c: pl.BlockSpec((1, ln, c), lambda b, j: (b, j, 0))
    st_spec = pl.BlockSpec((1, 1, SSM_PAIRS, SSM_STATE, LANES), lambda b, j: (b, j, 0, 0, 0))

    def body(xbc_ref, dt_ref, z_ref, cw_ref, cb_ref, dbr_ref, dbc_ref, alr_ref, alc_ref, dsk_ref, gain_ref,
             y_ref, xconv_ref, stp_ref, xin, st):
        @pl.when(pl.program_id(1) == 0)
        def _():
            xin[0:HALO, :] = jnp.zeros((HALO, wx), f32)
            st[...] = jnp.zeros_like(st)

        xin[HALO:HALO + ln, :] = xbc_ref[0]
        xconv = jnp.broadcast_to(cb_ref[...], (ln, wx))
        for k in range(SSM_CONV):
            xconv = xconv + cw_ref[k:k + 1, :] * xin[pl.ds(HALO - SSM_CONV + 1 + k, ln), :]
        xin[0:HALO, :] = xin[ln:ln + HALO, :]
        xconv_ref[0] = xconv
        stp_ref[0, 0] = st[...]
        xs_pre, b_pre, c_pre, dtr, dtr_t, zs, sts, gains = _ssd_chunk_inputs(xconv, dt_ref[0], z_ref[0], st,
                                                                             gain_ref[...])
        out, st_new = _ssd_chunk(xs_pre, b_pre, c_pre, dtr, dtr_t, zs, sts, dbr_ref[...], dbc_ref[...],
                                 alr_ref[...], alc_ref[...], dsk_ref[...], gains)
        y_ref[0] = jnp.concatenate(out, axis=1).astype(y_ref.dtype)
        for p in range(SSM_PAIRS):
            st[p] = st_new[p]

    params = [cw, cb, dbr, dbc, alr, alc, dsk, gain]
    return pl.pallas_call(
        body, name=name, grid=(nb, nt),
        in_specs=[tile(wx), tile(LANES), tile(SSM_WIDTH)] + [_const2(p.shape) for p in params],
        out_specs=[tile(SSM_WIDTH), tile(wx), st_spec],
        out_shape=[jax.ShapeDtypeStruct((nb, ns, SSM_WIDTH), MXU_DTYPE), jax.ShapeDtypeStruct((nb, ns, wx), f32),
                   jax.ShapeDtypeStruct((nb, nt, SSM_PAIRS, SSM_STATE, LANES), f32)],
        scratch_shapes=[pltpu.VMEM((ln + HALO, wx), f32), pltpu.VMEM((SSM_PAIRS, SSM_STATE, LANES), f32)],
        compiler_params=_cparams("arbitrary", "arbitrary"),
    )(xbc, dt_raw, z, *params)


def ssd_backward(dy, xbc, xconv, dt_raw, z, stp, cw, cb, dbr, dbc, alr, alc, dsk, gain, name):
    nb, ns, wx = xbc.shape
    ln = SSM_CHUNK
    nt = ns // ln
    per = ln // HALO
    rj = lambda j: nt - 1 - j
    tile = lambda c: pl.BlockSpec((1, ln, c), lambda b, j: (b, rj(j), 0))
    before = pl.BlockSpec((1, HALO, wx), lambda b, j: (b, jnp.maximum(rj(j) * per - 1, 0), 0))
    st_spec = pl.BlockSpec((1, 1, SSM_PAIRS, SSM_STATE, LANES), lambda b, j: (b, rj(j), 0, 0, 0))

    def body(dy_ref, xbc_ref, xbcb_ref, xconv_ref, dt_ref, z_ref, stp_ref,
             cw_ref, cb_ref, dbr_ref, dbc_ref, alr_ref, alc_ref, dsk_ref, gain_ref,
             dxbc_ref, ddt_ref, dz_ref, dcw_ref, dcb_ref, ddbr_ref, ddbc_ref, dalr_ref, dalc_ref, ddsk_ref, dgain_ref,
             dxc_ext, dst, xin):
        j = pl.program_id(1)
        first = _first_step()
        at_seq_start = j == nt - 1

        @pl.when(j == 0)
        def _():
            dxc_ext[ln:ln + HALO, :] = jnp.zeros((HALO, wx), f32)
            dst[...] = jnp.zeros_like(dst)

        xs_pre, b_pre, c_pre, dtr, dtr_t, zs, sts, gains = _ssd_chunk_inputs(xconv_ref[0], dt_ref[0], z_ref[0],
                                                                             stp_ref.at[0, 0], gain_ref[...])
        _, vjp = jax.vjp(_ssd_chunk, xs_pre, b_pre, c_pre, dtr, dtr_t, zs, sts, dbr_ref[...], dbc_ref[...],
                         alr_ref[...], alc_ref[...], dsk_ref[...], gains)
        dyv = dy_ref[0].astype(f32)
        cot = ([dyv[:, GROUP_WIDTH * g:GROUP_WIDTH * (g + 1)] for g in range(SSM_GROUPS)],
               [dst[p] for p in range(SSM_PAIRS)])
        dxs, db, dc, ddt, ddt_t, dzs, dsts, ddbr, ddbc, dalr, dalc, ddsk, dgains = vjp(cot)
        for p in range(SSM_PAIRS):
            dst[p] = dsts[p]
        ddt_ref[0] = (ddt + ddt_t.T).astype(ddt_ref.dtype)
        dz_ref[0] = jnp.concatenate(dzs, axis=1).astype(dz_ref.dtype)
        _accum(ddbr_ref, ddbr, first)
        _accum(ddbc_ref, ddbc, first)
        _accum(dalr_ref, dalr, first)
        _accum(dalc_ref, dalc, first)
        _accum(ddsk_ref, ddsk, first)
        _accum(dgain_ref, jnp.concatenate(dgains, axis=1), first)

        dxc = jnp.concatenate(dxs + db + dc, axis=1)
        _accum(dcb_ref, jnp.sum(dxc, axis=0, keepdims=True), first)
        dxc_ext[0:ln, :] = dxc
        dxp = jnp.zeros((ln, wx), f32)
        for k in range(SSM_CONV):
            dxp = dxp + cw_ref[k:k + 1, :] * dxc_ext[pl.ds(SSM_CONV - 1 - k, ln), :]
        dxbc_ref[0] = dxp.astype(dxbc_ref.dtype)
        dxc_ext[ln:ln + HALO, :] = dxc[0:HALO, :]

        xin[0:HALO, :] = jnp.where(at_seq_start, 0.0, xbcb_ref[0])
        xin[HALO:HALO + ln, :] = xbc_ref[0]
        dcw_rows = [jnp.sum(dxc * xin[pl.ds(HALO - SSM_CONV + 1 + k, ln), :], axis=0, keepdims=True)
                    for k in range(SSM_CONV)]
        dcw_rows += [jnp.zeros((1, wx), f32)] * (HALO - SSM_CONV)
        _accum(dcw_ref, jnp.concatenate(dcw_rows, axis=0), first)

    params = [cw, cb, dbr, dbc, alr, alc, dsk, gain]
    pshape = lambda p: jax.ShapeDtypeStruct(p.shape, f32)
    return pl.pallas_call(
        body, name=name, grid=(nb, nt),
        in_specs=[tile(SSM_WIDTH), tile(wx), before, tile(wx), tile(LANES), tile(SSM_WIDTH), st_spec]
        + [_const2(p.shape) for p in params],
        out_specs=[tile(wx), tile(LANES), tile(SSM_WIDTH), _const2((HALO, wx))] + [_const2(p.shape) for p in params[1:]],
        out_shape=[jax.ShapeDtypeStruct((nb, ns, wx), MXU_DTYPE), jax.ShapeDtypeStruct((nb, ns, LANES), MXU_DTYPE),
                   jax.ShapeDtypeStruct((nb, ns, SSM_WIDTH), MXU_DTYPE), jax.ShapeDtypeStruct((HALO, wx), f32)]
        + [pshape(p) for p in params[1:]],
        scratch_shapes=[pltpu.VMEM((ln + HALO, wx), f32), pltpu.VMEM((SSM_PAIRS, SSM_STATE, LANES), f32),
                        pltpu.VMEM((ln + HALO, wx), f32)],
        compiler_params=_cparams("arbitrary", "arbitrary"),
    )(dy, xbc, xbc, xconv, dt_raw, z, stp, *params)


CONF_HALO = 32
CONF_OFF = CONF_HALO - CONF_KERNEL + 1


def _conf_specs(ts, c, nt):
    per = ts // CONF_HALO
    tile = pl.BlockSpec((1, ts, c), lambda b, j: (b, j, 0))
    before = pl.BlockSpec((1, CONF_HALO, c), lambda b, j: (b, jnp.maximum(j * per - 1, 0), 0))
    after = pl.BlockSpec((1, CONF_HALO, c), lambda b, j: (b, jnp.minimum((j + 1) * per, nt * per - 1), 0))
    return tile, before, after


SUBLANES = 8


def _shifted_copies(dst, x):
    rows = x.shape[0]
    dst[0] = x
    for b in range(1, SUBLANES):
        dst[b] = pltpu.roll(x, rows - b, 0)


def _window(copies, off, size):
    b = off % SUBLANES
    return copies[b, pl.ds(off - b, size), :]


def _glu(x):
    return x[:, :CONF_WIDTH] * jax.nn.sigmoid(x[:, CONF_WIDTH:])


def _layernorm_parts(c):
    xc = c - jnp.mean(c, axis=-1, keepdims=True)
    r = lax.rsqrt(jnp.mean(xc * xc, axis=-1, keepdims=True) + EPS)
    return xc * r, r


def conf_forward(glu, cw, cb, ln_g, ln_b, name):
    nb, ns, wg = glu.shape
    w = CONF_WIDTH
    ts = SEQ_TILE
    nt = ns // ts
    tile, before, _ = _conf_specs(ts, wg, nt)

    def body(x_ref, xb_ref, cw_ref, cb_ref, g_ref, b_ref, y_ref, u_rot):
        _shifted_copies(u_rot, jnp.concatenate(
            [jnp.where(pl.program_id(1) == 0, 0.0, _glu(xb_ref[0])), _glu(x_ref[0])], axis=0))
        conv = jnp.broadcast_to(cb_ref[...], (ts, w))
        for k in range(CONF_KERNEL):
            conv = conv + cw_ref[k:k + 1, :] * _window(u_rot, CONF_OFF + k, ts)
        xhat, _ = _layernorm_parts(conv)
        y_ref[0] = _silu(xhat * g_ref[...] + b_ref[...]).astype(y_ref.dtype)

    params = [cw, cb, ln_g, ln_b]
    return pl.pallas_call(
        body, name=name, grid=(nb, nt),
        in_specs=[tile, before] + [_const2(p.shape) for p in params],
        out_specs=pl.BlockSpec((1, ts, w), lambda b, j: (b, j, 0)),
        out_shape=jax.ShapeDtypeStruct((nb, ns, w), MXU_DTYPE),
        scratch_shapes=[pltpu.VMEM((SUBLANES, ts + CONF_HALO, w), f32)],
        compiler_params=_cparams("parallel", "parallel"),
    )(glu, glu, *params)


def conf_backward(dy, glu, cw, cb, ln_g, ln_b, name):
    nb, ns, wg = glu.shape
    w = CONF_WIDTH
    ts = SEQ_TILE
    nt = ns // ts
    te = ts + CONF_HALO
    tile, before, after = _conf_specs(ts, wg, nt)
    dtile, _, dafter = _conf_specs(ts, w, nt)

    def body(dy_ref, dya_ref, x_ref, xb_ref, xa_ref, cw_ref, cb_ref, g_ref, b_ref,
             dx_ref, dcw_ref, dcb_ref, dg_ref, db_ref, u_ext, dc_ext):
        j = pl.program_id(1)
        first = _first_step()
        x = x_ref[0]
        _shifted_copies(u_ext, jnp.concatenate(
            [jnp.where(j == 0, 0.0, _glu(xb_ref[0])), _glu(x), _glu(xa_ref[0])], axis=0))
        conv = jnp.broadcast_to(cb_ref[...], (te, w))
        for k in range(CONF_KERNEL):
            conv = conv + cw_ref[k:k + 1, :] * _window(u_ext, CONF_OFF + k, te)
        xhat, r = _layernorm_parts(conv)
        lnout = xhat * g_ref[...] + b_ref[...]
        sg = jax.nn.sigmoid(lnout)
        rows = lax.broadcasted_iota(jnp.int32, (te, w), 0)
        dyv = jnp.concatenate([dy_ref[0].astype(f32), dya_ref[0].astype(f32)], axis=0)
        dyv = jnp.where(jnp.logical_and(j == nt - 1, rows >= ts), 0.0, dyv)
        dln = dyv * sg * (1.0 + lnout * (1.0 - sg))
        in_tile = rows < ts
        _accum(dg_ref, jnp.sum(jnp.where(in_tile, dln * xhat, 0.0), axis=0, keepdims=True), first)
        _accum(db_ref, jnp.sum(jnp.where(in_tile, dln, 0.0), axis=0, keepdims=True), first)
        dxh = dln * g_ref[...]
        dconv = r * (dxh - jnp.mean(dxh, axis=-1, keepdims=True) - xhat * jnp.mean(dxh * xhat, axis=-1, keepdims=True))
        _shifted_copies(dc_ext, dconv)
        dct = dconv[0:ts, :]
        _accum(dcb_ref, jnp.sum(dct, axis=0, keepdims=True), first)
        du = jnp.zeros((ts, w), f32)
        dcw_rows = []
        for k in range(CONF_KERNEL):
            du = du + cw_ref[k:k + 1, :] * _window(dc_ext, CONF_KERNEL - 1 - k, ts)
            dcw_rows.append(jnp.sum(dct * _window(u_ext, CONF_OFF + k, ts), axis=0, keepdims=True))
        dcw_rows.append(jnp.zeros((1, w), f32))
        _accum(dcw_ref, jnp.concatenate(dcw_rows, axis=0), first)
        sb = jax.nn.sigmoid(x[:, w:])
        dx_ref[0] = jnp.concatenate([du * sb, du * x[:, :w] * sb * (1.0 - sb)], axis=1).astype(dx_ref.dtype)

    params = [cw, cb, ln_g, ln_b]
    return pl.pallas_call(
        body, name=name, grid=(nb, nt),
        in_specs=[dtile, dafter, tile, before, after] + [_const2(p.shape) for p in params],
        out_specs=[tile] + [_const2(p.shape) for p in params],
        out_shape=[jax.ShapeDtypeStruct((nb, ns, wg), MXU_DTYPE)] + [jax.ShapeDtypeStruct(p.shape, f32) for p in params],
        scratch_shapes=[pltpu.VMEM((SUBLANES, te + CONF_HALO, w), f32), pltpu.VMEM((SUBLANES, te, w), f32)],
        compiler_params=_cparams("arbitrary", "arbitrary"),
    )(dy, dy, glu, glu, glu, *params)


def _row(v):
    return v.reshape(1, -1).astype(f32)


def _pad_to(v, n, axis):
    pads = [(0, 0)] * v.ndim
    pads[axis] = (0, n - v.shape[axis])
    return jnp.pad(v, pads)


def _block_diag(w):
    nh, d, _ = w.shape
    eye = jnp.eye(nh, dtype=w.dtype)
    return (eye[:, None, :, None] * w[:, :, None, :]).reshape(nh * d, nh * d)


def _diag_blocks(m, nh):
    d = m.shape[0] // nh
    idx = jnp.arange(nh)
    return m.reshape(nh, d, nh, d)[idx, :, idx, :]


def _mix_even_fwd(h, gpre, w, nb, ns):
    t = nb * ns
    w_in = w["ev_w_in"][0]
    w_lx, w_lg, w_qkv = w_in[:, :LRU_WIDTH], w_in[:, LRU_WIDTH:2 * LRU_WIDTH], w_in[:, 2 * LRU_WIDTH:]
    xpre, gate, qkv = norm_matmul(h, gpre, [w_lx, w_lg, w_qkv], [f32, f32, f32], name="ev_in_proj")
    lru_p = [w["ev_lru_conv_w"][0], _row(w["ev_lru_conv_b"][0]),
             _block_diag(w["ev_lru_gate_a_w"][0]).astype(MXU_DTYPE), _row(w["ev_lru_gate_a_b"][0]),
             _block_diag(w["ev_lru_gate_x_w"][0]).astype(MXU_DTYPE), _row(w["ev_lru_gate_x_b"][0]),
             _row(w["ev_lru_lambda"][0])]
    xpre3, gate3, qkv3 = xpre.reshape(nb, ns, -1), gate.reshape(nb, ns, -1), qkv.reshape(nb, ns, -1)
    y_a, xc, hs = lru_forward(xpre3, gate3, *lru_p, name="ev_lru_fwd")
    o = sb_forward(qkv3, name="ev_sb_fwd")
    ys = [y_a.reshape(t, -1), o.reshape(t, -1)]
    saved = dict(xpre=xpre3, gate=gate3, qkv=qkv3, xc=xc, hs=hs, o=o, lru_p=lru_p)
    w_out = w["ev_w_out"][0]
    return ys, [w_out[:LRU_WIDTH], w_out[LRU_WIDTH:]], saved


def _mix_even_bwd(dys, saved, wt, nb, ns):
    t = nb * ns
    dy_a, dy_b = [d.reshape(nb, ns, -1) for d in dys]
    outs = lru_backward(dy_a, saved["xpre"], saved["gate"], saved["xc"], saved["hs"], *saved["lru_p"],
                        name="ev_lru_bwd")
    dxp, dgt, dcw, dcb, dga, dgab, dgx, dgxb, dlam = outs
    dq, dk, dv = sb_backward(saved["qkv"], saved["o"], dy_b, name="ev_sb_bwd")
    w_in_t = wt["ev_w_in"][0]
    pieces = [dxp, dgt, dq, dk, dv]
    gs = [d.reshape(t, -1) for d in pieces]
    wts = [w_in_t[LRU_WIDTH * i:LRU_WIDTH * (i + 1)] for i in range(5)]
    grads = {
        "ev_lru_conv_w": dcw[:LRU_CONV][None], "ev_lru_conv_b": dcb,
        "ev_lru_gate_a_w": _diag_blocks(dga, LRU_HEADS)[None], "ev_lru_gate_a_b": dgab,
        "ev_lru_gate_x_w": _diag_blocks(dgx, LRU_HEADS)[None], "ev_lru_gate_x_b": dgxb,
        "ev_lru_lambda": dlam,
    }
    return gs, wts, grads


def _odd_params(w):
    ssd_p = [w["od_ssm_conv_w"][0], _row(w["od_ssm_conv_b"][0]),
             _pad_to(_row(w["od_ssm_dt_bias"][0]), LANES, 1), _pad_to(_row(w["od_ssm_dt_bias"][0]), LANES, 1).T,
             _pad_to(_row(w["od_ssm_a_log"][0]), LANES, 1), _pad_to(_row(w["od_ssm_a_log"][0]), LANES, 1).T,
             _pad_to(_row(w["od_ssm_d"][0]), LANES, 1), _row(w["od_ssm_norm"][0])]
    conf_p = [_pad_to(w["od_cm_conv_w"][0], CONF_HALO, 0), _row(w["od_cm_conv_b"][0]),
              _row(w["od_cm_ln_g"][0]), _row(w["od_cm_ln_b"][0])]
    return ssd_p, conf_p


ODD_SPLITS = (SSM_WIDTH, SSM_WIDTH + SSM_XBC, SSM_WIDTH + SSM_XBC + SSM_HEADS)


def _mix_odd_fwd(h, gpre, w, nb, ns):
    t = nb * ns
    w_in = w["od_w_in"][0]
    s0, s1, s2 = ODD_SPLITS
    w_z, w_xbc, w_dt, w_glu = w_in[:, :s0], w_in[:, s0:s1], _pad_to(w_in[:, s1:s2], LANES, 1), w_in[:, s2:]
    zz, xbc, dtr, glu = norm_matmul(h, gpre, [w_z, w_xbc, w_dt, w_glu], [f32] * 4, name="od_in_proj")
    ssd_p, conf_p = _odd_params(w)
    zz3, xbc3, dtr3, glu3 = [a.reshape(nb, ns, -1) for a in (zz, xbc, dtr, glu)]
    y_c, xconv, stp = ssd_forward(xbc3, dtr3, zz3, *ssd_p, name="od_ssd_fwd")
    y_d = conf_forward(glu3, *conf_p, name="od_conf_fwd")
    ys = [y_c.reshape(t, -1), y_d.reshape(t, -1)]
    saved = dict(z=zz3, xbc=xbc3, dtr=dtr3, glu=glu3, xconv=xconv, stp=stp, ssd_p=ssd_p, conf_p=conf_p)
    w_out = w["od_w_out"][0]
    return ys, [w_out[:SSM_WIDTH], w_out[SSM_WIDTH:]], saved


def _mix_odd_bwd(dys, saved, wt, nb, ns):
    t = nb * ns
    dy_c, dy_d = [d.reshape(nb, ns, -1) for d in dys]
    outs = ssd_backward(dy_c, saved["xbc"], saved["xconv"], saved["dtr"], saved["z"], saved["stp"], *saved["ssd_p"],
                        name="od_ssd_bwd")
    dxbc, ddt, dz, dcw, dcb, ddbr, ddbc, dalr, dalc, ddsk, dgain = outs
    dglu, ccw, ccb, clg, clb = conf_backward(dy_d, saved["glu"], *saved["conf_p"], name="od_conf_bwd")
    w_in_t = wt["od_w_in"][0]
    s0, s1, s2 = ODD_SPLITS
    gs = [d.reshape(t, -1) for d in (dz, dxbc, ddt, dglu)]
    wts = [w_in_t[:s0], w_in_t[s0:s1], _pad_to(w_in_t[s1:s2], LANES, 0), w_in_t[s2:]]
    nh = SSM_HEADS
    grads = {
        "od_ssm_conv_w": dcw[:SSM_CONV][None], "od_ssm_conv_b": dcb,
        "od_ssm_dt_bias": ddbr[:, :nh] + ddbc[:nh, 0][None], "od_ssm_a_log": dalr[:, :nh] + dalc[:nh, 0][None],
        "od_ssm_d": ddsk[:, :nh], "od_ssm_norm": dgain,
        "od_cm_conv_w": ccw[:CONF_KERNEL][None], "od_cm_conv_b": ccb, "od_cm_ln_g": clg, "od_cm_ln_b": clb,
    }
    return gs, wts, grads


def local_step(x, p, target, w, wt=None):
    if wt is None:
        wt = {k: jnp.swapaxes(w[k], 1, 2) for k in BIG_SHARDED}
    nb, ns, d = x.shape
    t = nb * ns
    h = x.reshape(t, d)
    depth = p.shape[0]
    tapes = []
    for i in range(depth):
        even = i % 2 == 0
        tag = f"l{i}_"
        gpre = _row(w["norm_mix_pre"][i])
        ys, w_outs, saved = (_mix_even_fwd if even else _mix_odd_fwd)(h, gpre, w, nb, ns)
        h1, m = matmul_residual_norm(ys, w_outs, h, _row(w["norm_mix_post"][i]), name=tag + "out_proj")
        a1, = norm_matmul(h1, _row(w["norm_mlp_pre"][i]), [w["mlp_w1"][i]], [f32], name=tag + "mlp_up")
        h2, f = matmul_residual_norm([a1], [w["mlp_w2"][i]], h1, _row(w["norm_mlp_post"][i]), name=tag + "mlp_down",
                                     relu2=True)
        pi = p[i].reshape(t, -1)
        h3, gl, emb = ple_forward(h2, pi, w["ple_w_gate"][i], w["ple_w_proj"][i], _row(w["norm_ple"][i]),
                                  name=tag + "ple")
        tapes.append(dict(h=h, ys=ys, w_outs=w_outs, saved=saved, h1=h1, m=m, a1=a1, h2=h2, f=f, pi=pi, gl=gl,
                          emb=emb))
        h = h3

    loss_row, dh = loss_and_grad(h, target.reshape(t, d), name="loss")
    grads = {}
    stacked = {k: [None] * depth for k in ("norm_mix_pre", "norm_mix_post", "norm_mlp_pre", "norm_mlp_post", "norm_ple",
                                           "mlp_w1", "mlp_w2", "ple_w_proj", "ple_w_gate")}
    for i in reversed(range(depth)):
        even = i % 2 == 0
        tag = f"l{i}_"
        tp = tapes[i]
        dh2, dgl, demb, dg = ple_backward(dh, tp["h2"], tp["gl"], tp["emb"], _row(w["norm_ple"][i]),
                                          wt["ple_w_gate"][i], name=tag + "ple_bwd")
        stacked["norm_ple"][i] = dg
        stacked["ple_w_gate"][i] = weight_grad(tp["h2"], dgl, name=tag + "dw_gate")
        stacked["ple_w_proj"][i] = weight_grad(tp["pi"], demb, name=tag + "dw_proj")
        d_f, (da1,), dg = bwd_through_norm_out(dh2, tp["f"], _row(w["norm_mlp_post"][i]), [wt["mlp_w2"][i]],
                                               [MXU_DTYPE], name=tag + "mlp_down_bwd", relu2_of=tp["a1"])
        stacked["norm_mlp_post"][i] = dg
        stacked["mlp_w2"][i] = weight_grad(tp["a1"], d_f, name=tag + "dw2", prologue="relu2")
        gpre = _row(w["norm_mlp_pre"][i])
        dh1, dg = bwd_through_norm_in(dh2, [da1], [wt["mlp_w1"][i]], tp["h1"], gpre, name=tag + "mlp_up_bwd")
        stacked["norm_mlp_pre"][i] = dg
        stacked["mlp_w1"][i] = weight_grad(tp["h1"], da1, name=tag + "dw1", prologue="rms", gain=gpre)
        wt_out = wt["ev_w_out" if even else "od_w_out"][0]
        split = tp["w_outs"][0].shape[0]
        dm, dys, dg = bwd_through_norm_out(dh1, tp["m"], _row(w["norm_mix_post"][i]),
                                           [wt_out[:, :split], wt_out[:, split:]], [f32, f32],
                                           name=tag + "out_proj_bwd")
        stacked["norm_mix_post"][i] = dg
        dw_out = jnp.concatenate([weight_grad(y, dm, name=tag + f"dw_out{k}") for k, y in enumerate(tp["ys"])], axis=0)
        gs, wts, mix_grads = (_mix_even_bwd if even else _mix_odd_bwd)(dys, tp["saved"], wt, nb, ns)
        grads.update(mix_grads)
        gpre = _row(w["norm_mix_pre"][i])
        dh, dg = bwd_through_norm_in(dh1, gs, wts, tp["h"], gpre, name=tag + "in_proj_bwd")
        stacked["norm_mix_pre"][i] = dg
        dw_in = [weight_grad(tp["h"], g, name=tag + f"dw_in{k}", prologue="rms", gain=gpre) for k, g in enumerate(gs)]
        if even:
            grads["ev_w_in"] = jnp.concatenate(dw_in, axis=1)[None]
            grads["ev_w_out"] = dw_out[None]
        else:
            dw_in[2] = dw_in[2][:, :SSM_HEADS]
            grads["od_w_in"] = jnp.concatenate(dw_in, axis=1)[None]
            grads["od_w_out"] = dw_out[None]
    for k, v in stacked.items():
        grads[k] = jnp.concatenate(v, axis=0) if v[0].shape[0] == 1 and v[0].ndim == 2 and k.startswith("norm") \
            else jnp.stack(v, axis=0)
    return loss_row[0, 0], dh.reshape(nb, ns, d), grads


MESH_ID = pl.DeviceIdType.MESH
ANY = pl.BlockSpec(memory_space=pl.ANY)


def _mesh_pos():
    return lax.axis_index("x"), lax.axis_index("y"), lax.axis_index("c")


def all_gather(shards, name):
    n = len(shards)

    def body(*refs):
        x_refs, out_refs = refs[:n], refs[n:2 * n]
        send_sems, recv_sems, local_sems = refs[2 * n:]
        x, y, c = _mesh_pos()
        me, sibling = (x, y, c), (x, y, 1 - c)
        chips = [(1 - x, y), (x, 1 - y), (1 - x, 1 - y)]

        def slot(a, px, py, pc):
            return out_refs[a].at[4 * px + 2 * py + pc]

        def copy(a, k, block, to, src=None):
            return pltpu.make_async_remote_copy(
                src_ref=slot(a, *block) if src is None else src, dst_ref=slot(a, *block),
                send_sem=send_sems.at[7 * a + k], recv_sem=recv_sems.at[7 * a + k], device_id=to,
                device_id_type=MESH_ID)

        mine = [pltpu.make_async_copy(x_refs[a], slot(a, *me), local_sems.at[a]) for a in range(n)]
        for cp in mine:
            cp.start()
        first = []
        for j, chip in enumerate(chips):
            first += [copy(a, 1 + j, me, (*chip, c), src=x_refs[a]) for a in range(n)]
        first += [copy(a, 0, me, sibling, src=x_refs[a]) for a in range(n)]
        for cp in first:
            cp.start()
        passed = []
        for j, chip in enumerate(chips):
            for a in range(n):
                copy(a, 1 + j, (*chip, c), me).wait_recv()
                fwd = copy(a, 4 + j, (*chip, c), sibling)
                fwd.start()
                passed.append(fwd)
        for a in range(n):
            copy(a, 0, sibling, me).wait_recv()
        for j, chip in enumerate(chips):
            for a in range(n):
                copy(a, 4 + j, (*chip, 1 - c), me).wait_recv()
        for cp in first + passed:
            cp.wait_send()
        for cp in mine:
            cp.wait()

    return pl.pallas_call(
        body, name=name,
        out_shape=[jax.ShapeDtypeStruct((N_DEV,) + s.shape, s.dtype) for s in shards],
        in_specs=[ANY] * n, out_specs=[ANY] * n,
        scratch_shapes=[pltpu.SemaphoreType.DMA((7 * n,)), pltpu.SemaphoreType.DMA((7 * n,)),
                        pltpu.SemaphoreType.DMA((n,))],
    )(*shards)


def scatter_to_sibling(parts, name):
    n = len(parts)

    def body(*refs):
        g_refs, out_refs = refs[:n], refs[n:2 * n]
        send_sems, recv_sems = refs[2 * n:]
        x, y, c = _mesh_pos()
        sibling = (x, y, 1 - c)
        copies = []
        for a in range(n):
            for chip in range(4):
                copies.append(pltpu.make_async_remote_copy(
                    src_ref=g_refs[a].at[2 * chip + (1 - c)], dst_ref=out_refs[a].at[chip],
                    send_sem=send_sems.at[4 * a + chip], recv_sem=recv_sems.at[4 * a + chip], device_id=sibling,
                    device_id_type=MESH_ID))
        for cp in copies:
            cp.start()
        for cp in copies:
            cp.wait_recv()
        for cp in copies:
            cp.wait_send()

    return pl.pallas_call(
        body, name=name,
        out_shape=[jax.ShapeDtypeStruct((4,) + p.shape[1:], p.dtype) for p in parts],
        in_specs=[ANY] * n, out_specs=[ANY] * n,
        scratch_shapes=[pltpu.SemaphoreType.DMA((4 * n,)), pltpu.SemaphoreType.DMA((4 * n,))],
    )(*parts)


def scatter_to_chips(partials, name):
    n = len(partials)

    def body(*refs):
        p_refs, out_refs = refs[:n], refs[n:2 * n]
        send_sems, recv_sems = refs[2 * n:]
        x, y, c = _mesh_pos()
        chips = [(1 - x, y), (x, 1 - y), (1 - x, 1 - y)]
        copies = []
        for a in range(n):
            for j, (px, py) in enumerate(chips):
                copies.append(pltpu.make_async_remote_copy(
                    src_ref=p_refs[a].at[2 * px + py], dst_ref=out_refs[a].at[j],
                    send_sem=send_sems.at[3 * a + j], recv_sem=recv_sems.at[3 * a + j], device_id=(px, py, c),
                    device_id_type=MESH_ID))
        for cp in copies:
            cp.start()
        for cp in copies:
            cp.wait_recv()
        for cp in copies:
            cp.wait_send()

    return pl.pallas_call(
        body, name=name,
        out_shape=[jax.ShapeDtypeStruct((3,) + p.shape[1:], p.dtype) for p in partials],
        in_specs=[ANY] * n, out_specs=[ANY] * n,
        scratch_shapes=[pltpu.SemaphoreType.DMA((3 * n,)), pltpu.SemaphoreType.DMA((3 * n,))],
    )(*partials)


ICI_DTYPE = jnp.bfloat16
ELEMENTWISE_BLOCK_BYTES = 1 << 20


def _row_tile(rows, cols):
    cap = max(16, ELEMENTWISE_BLOCK_BYTES // (4 * cols))
    best = [t for t in range(16, min(rows, cap) + 1, 16) if rows % t == 0]
    return best[-1] if best else rows


def add_sibling_parts(parts, received, core, name):
    _, r, n = parts.shape
    tr = _row_tile(r, n)

    def body(c_ref, a_ref, b_ref, o_ref, ob_ref):
        s = a_ref[...] + b_ref[...]
        o_ref[...] = s
        ob_ref[...] = s.astype(ob_ref.dtype)

    blk = pl.BlockSpec((1, tr, n), lambda i, j, c_ref: (i, j, 0))
    return pl.pallas_call(
        body, name=name,
        grid_spec=pltpu.PrefetchScalarGridSpec(
            num_scalar_prefetch=1, grid=(4, r // tr),
            in_specs=[pl.BlockSpec((1, tr, n), lambda i, j, c_ref: (2 * i + c_ref[0], j, 0)), blk],
            out_specs=[blk, blk]),
        out_shape=[jax.ShapeDtypeStruct((4, r, n), f32), jax.ShapeDtypeStruct((4, r, n), ICI_DTYPE)],
        compiler_params=_cparams("parallel", "parallel"),
    )(core, parts, received)


def _adamw(w, g, m, v):
    m = ADAM_B1 * m + (1.0 - ADAM_B1) * g
    v = ADAM_B2 * v + (1.0 - ADAM_B2) * jnp.square(g)
    m_hat = m / (1.0 - ADAM_B1 ** ADAM_STEP)
    v_hat = v / (1.0 - ADAM_B2 ** ADAM_STEP)
    delta = -ADAM_LR * (m_hat / (jnp.sqrt(v_hat) + ADAM_EPS) + ADAM_WD * w)
    return delta, m, v


def adamw_sharded(partial, received, chip, w, m, v, name):
    _, r, n = partial.shape

    def body(k_ref, p_ref, r_ref, w_ref, m_ref, v_ref, g_out, d_out, m_out, v_out):
        g = p_ref[0] + r_ref[0].astype(f32)
        g = g + r_ref[1].astype(f32)
        g = g + r_ref[2].astype(f32)
        delta, mn, vn = _adamw(w_ref[...], g, m_ref[...], v_ref[...])
        g_out[...] = g
        d_out[...] = delta
        m_out[...] = mn
        v_out[...] = vn

    tr = _row_tile(r, n)
    flat = pl.BlockSpec((tr, n), lambda j, k_ref: (j, 0))
    return pl.pallas_call(
        body, name=name,
        grid_spec=pltpu.PrefetchScalarGridSpec(
            num_scalar_prefetch=1, grid=(r // tr,),
            in_specs=[pl.BlockSpec((1, tr, n), lambda j, k_ref: (k_ref[0], j, 0)),
                      pl.BlockSpec((3, tr, n), lambda j, k_ref: (0, j, 0)), flat, flat, flat],
            out_specs=[flat] * 4),
        out_shape=[jax.ShapeDtypeStruct((r, n), f32)] * 4,
        compiler_params=_cparams("parallel"),
    )(chip, partial, received, w, m, v)


def adamw_replicated(gathered, w, m, v, name):
    _, r, n = gathered.shape

    def body(g_ref, w_ref, m_ref, v_ref, g_out, d_out, m_out, v_out):
        g = g_ref[0]
        for k in range(1, N_DEV):
            g = g + g_ref[k]
        delta, mn, vn = _adamw(w_ref[...], g, m_ref[...], v_ref[...])
        g_out[...] = g
        d_out[...] = delta
        m_out[...] = mn
        v_out[...] = vn

    return pl.pallas_call(
        body, name=name,
        out_shape=[jax.ShapeDtypeStruct((r, n), f32)] * 4,
        compiler_params=pltpu.CompilerParams(vmem_limit_bytes=VMEM_LIMIT),
    )(gathered, w, m, v)


W_NAMES = ['ev_w_in', 'ev_lru_conv_w', 'ev_lru_conv_b', 'ev_lru_gate_a_w', 'ev_lru_gate_a_b', 'ev_lru_gate_x_w',
           'ev_lru_gate_x_b', 'ev_lru_lambda', 'ev_w_out', 'od_w_in', 'od_ssm_conv_w', 'od_ssm_conv_b',
           'od_ssm_dt_bias', 'od_ssm_a_log', 'od_ssm_d', 'od_ssm_norm', 'od_cm_conv_w', 'od_cm_conv_b', 'od_cm_ln_g',
           'od_cm_ln_b', 'od_w_out', 'norm_mix_pre', 'norm_mix_post', 'norm_mlp_pre', 'norm_mlp_post', 'norm_ple',
           'mlp_w1', 'mlp_w2', 'ple_w_proj', 'ple_w_gate']
BIG_SHARDED = {'ev_w_in': 2, 'ev_w_out': 1, 'od_w_in': 2, 'od_w_out': 1, 'mlp_w1': 2, 'mlp_w2': 1, 'ple_w_proj': 2,
               'ple_w_gate': 1}
SMALL_SHARDED = {'ev_lru_conv_w': 2, 'od_ssm_conv_w': 2, 'od_ssm_conv_b': 1, 'od_ssm_norm': 1, 'od_cm_conv_w': 2,
                 'od_cm_conv_b': 1, 'od_cm_ln_g': 1, 'od_cm_ln_b': 1}
SHARDED = {**BIG_SHARDED, **SMALL_SHARDED}
REPLICATED = [n for n in W_NAMES if n not in SHARDED]


def _round_up(n, k):
    return -(-n // k) * k


def _pack_rows(flat, rows_multiple):
    n = flat.shape[0]
    total = _round_up(n, LANES * rows_multiple)
    return jnp.pad(flat, (0, total - n)).reshape(-1, LANES)


def _unpack(flat, shapes):
    out, off = {}, 0
    for name, shape in shapes.items():
        size = math.prod(shape)
        out[name] = flat[off:off + size].reshape(shape)
        off += size
    return out


def _rows(a):
    return a.reshape(-1, a.shape[-1])


def _unshard(g8, shape, axis):
    g = jnp.moveaxis(g8.reshape((N_DEV,) + tuple(shape)), 0, axis)
    return g.reshape(tuple(shape[:axis]) + (N_DEV * shape[axis],) + tuple(shape[axis + 1:]))


def _to_shards(g, axis):
    shard = g.shape[axis] // N_DEV
    g = g.reshape(g.shape[:axis] + (N_DEV, shard) + g.shape[axis + 1:])
    return jnp.moveaxis(g, axis, 0)


def _pack_small(tree):
    return _pack_rows(jnp.concatenate([tree[k].astype(f32).reshape(-1) for k in SMALL_SHARDED]), 16)


def _gather_weights(w):
    small = _pack_small(w)
    terms, rest = [], small
    for _ in range(3):
        term = rest.astype(MXU_DTYPE)
        terms.append(term)
        rest = rest - term.astype(f32)
    outs = all_gather([_rows(w[k].astype(MXU_DTYPE)) for k in BIG_SHARDED] + [jnp.concatenate(terms, axis=0)],
                      name="gather_weights")
    full = {k: w[k] for k in REPLICATED}
    full_t = {}
    for (k, axis), g8 in zip(BIG_SHARDED.items(), outs):
        full[k] = _unshard(g8, w[k].shape, axis)
        full_t[k] = jnp.swapaxes(full[k], 1, 2)
    t = outs[-1].astype(f32)
    nr = small.shape[0]
    vals = (t[:, :nr] + t[:, nr:2 * nr] + t[:, 2 * nr:]).reshape(N_DEV, -1)
    off = 0
    for k, axis in SMALL_SHARDED.items():
        size = math.prod(w[k].shape)
        full[k] = _unshard(vals[:, off:off + size], w[k].shape, axis)
        off += size
    return full, full_t


def _pack_replicated(tree):
    return _pack_rows(jnp.concatenate([tree[k].astype(f32).reshape(-1) for k in REPLICATED]), 8)


def kernel(x, p, ev_w_in, ev_lru_conv_w, ev_lru_conv_b, ev_lru_gate_a_w, ev_lru_gate_a_b, ev_lru_gate_x_w, ev_lru_gate_x_b, ev_lru_lambda, ev_w_out, od_w_in, od_ssm_conv_w, od_ssm_conv_b, od_ssm_dt_bias, od_ssm_a_log, od_ssm_d, od_ssm_norm, od_cm_conv_w, od_cm_conv_b, od_cm_ln_g, od_cm_ln_b, od_w_out, norm_mix_pre, norm_mix_post, norm_mlp_pre, norm_mlp_post, norm_ple, mlp_w1, mlp_w2, ple_w_proj, ple_w_gate, loss_target, m_ev_w_in, m_ev_lru_conv_w, m_ev_lru_conv_b, m_ev_lru_gate_a_w, m_ev_lru_gate_a_b, m_ev_lru_gate_x_w, m_ev_lru_gate_x_b, m_ev_lru_lambda, m_ev_w_out, m_od_w_in, m_od_ssm_conv_w, m_od_ssm_conv_b, m_od_ssm_dt_bias, m_od_ssm_a_log, m_od_ssm_d, m_od_ssm_norm, m_od_cm_conv_w, m_od_cm_conv_b, m_od_cm_ln_g, m_od_cm_ln_b, m_od_w_out, m_norm_mix_pre, m_norm_mix_post, m_norm_mlp_pre, m_norm_mlp_post, m_norm_ple, m_mlp_w1, m_mlp_w2, m_ple_w_proj, m_ple_w_gate, v_ev_w_in, v_ev_lru_conv_w, v_ev_lru_conv_b, v_ev_lru_gate_a_w, v_ev_lru_gate_a_b, v_ev_lru_gate_x_w, v_ev_lru_gate_x_b, v_ev_lru_lambda, v_ev_w_out, v_od_w_in, v_od_ssm_conv_w, v_od_ssm_conv_b, v_od_ssm_dt_bias, v_od_ssm_a_log, v_od_ssm_d, v_od_ssm_norm, v_od_cm_conv_w, v_od_cm_conv_b, v_od_cm_ln_g, v_od_cm_ln_b, v_od_w_out, v_norm_mix_pre, v_norm_mix_post, v_norm_mlp_pre, v_norm_mlp_post, v_norm_ple, v_mlp_w1, v_mlp_w2, v_ple_w_proj, v_ple_w_gate):
    ws = [ev_w_in, ev_lru_conv_w, ev_lru_conv_b, ev_lru_gate_a_w, ev_lru_gate_a_b, ev_lru_gate_x_w, ev_lru_gate_x_b, ev_lru_lambda, ev_w_out, od_w_in, od_ssm_conv_w, od_ssm_conv_b, od_ssm_dt_bias, od_ssm_a_log, od_ssm_d, od_ssm_norm, od_cm_conv_w, od_cm_conv_b, od_cm_ln_g, od_cm_ln_b, od_w_out, norm_mix_pre, norm_mix_post, norm_mlp_pre, norm_mlp_post, norm_ple, mlp_w1, mlp_w2, ple_w_proj, ple_w_gate]
    ms = [m_ev_w_in, m_ev_lru_conv_w, m_ev_lru_conv_b, m_ev_lru_gate_a_w, m_ev_lru_gate_a_b, m_ev_lru_gate_x_w, m_ev_lru_gate_x_b, m_ev_lru_lambda, m_ev_w_out, m_od_w_in, m_od_ssm_conv_w, m_od_ssm_conv_b, m_od_ssm_dt_bias, m_od_ssm_a_log, m_od_ssm_d, m_od_ssm_norm, m_od_cm_conv_w, m_od_cm_conv_b, m_od_cm_ln_g, m_od_cm_ln_b, m_od_w_out, m_norm_mix_pre, m_norm_mix_post, m_norm_mlp_pre, m_norm_mlp_post, m_norm_ple, m_mlp_w1, m_mlp_w2, m_ple_w_proj, m_ple_w_gate]
    vs = [v_ev_w_in, v_ev_lru_conv_w, v_ev_lru_conv_b, v_ev_lru_gate_a_w, v_ev_lru_gate_a_b, v_ev_lru_gate_x_w, v_ev_lru_gate_x_b, v_ev_lru_lambda, v_ev_w_out, v_od_w_in, v_od_ssm_conv_w, v_od_ssm_conv_b, v_od_ssm_dt_bias, v_od_ssm_a_log, v_od_ssm_d, v_od_ssm_norm, v_od_cm_conv_w, v_od_cm_conv_b, v_od_cm_ln_g, v_od_cm_ln_b, v_od_w_out, v_norm_mix_pre, v_norm_mix_post, v_norm_mlp_pre, v_norm_mlp_post, v_norm_ple, v_mlp_w1, v_mlp_w2, v_ple_w_proj, v_ple_w_gate]
    w = dict(zip(W_NAMES, ws))
    m = dict(zip(W_NAMES, ms))
    v = dict(zip(W_NAMES, vs))
    full, full_t = _gather_weights(w)
    loss_local, grad_x, grads = local_step(x, p, loss_target, full, full_t)
    loss = lax.psum(loss_local, ("x", "y", "c"))
    return (loss, grad_x, *_reduce_and_update(grads, w, m, v))


def _reduce_and_update(grads, w, m, v):
    mx, my, mc = _mesh_pos()

    parts = [_to_shards(grads[k], axis) for k, axis in BIG_SHARDED.items()]
    parts = [g.reshape(N_DEV, -1, g.shape[-1]) for g in parts]
    small = jnp.concatenate([_to_shards(grads[k], axis).reshape(N_DEV, -1) for k, axis in SMALL_SHARDED.items()],
                            axis=1)
    small_rows = _pack_small(w).shape[0]
    small = jnp.pad(small, ((0, 0), (0, small_rows * LANES - small.shape[1]))).reshape(N_DEV, small_rows, LANES)
    parts.append(small)
    from_sibling = scatter_to_sibling(parts, name="scatter_sibling")
    core = jnp.reshape(mc, (1,)).astype(jnp.int32)
    sums = [add_sibling_parts(a, b, core, name=f"add_sibling_{i}") for i, (a, b) in enumerate(zip(parts, from_sibling))]
    from_chips = scatter_to_chips([s[1] for s in sums], name="scatter_chips")
    chip = jnp.reshape(2 * mx + my, (1,)).astype(jnp.int32)
    g_sh, d_sh, m_sh, v_sh = {}, {}, {}, {}
    for i, k in enumerate(BIG_SHARDED):
        outs = adamw_sharded(sums[i][0], from_chips[i], chip, _rows(w[k]), _rows(m[k]), _rows(v[k]),
                             name=f"adamw_{k}")
        for tree, o in zip((g_sh, d_sh, m_sh, v_sh), outs):
            tree[k] = o.reshape(w[k].shape)
    outs = adamw_sharded(sums[-1][0], from_chips[-1], chip, _pack_small(w), _pack_small(m), _pack_small(v),
                         name="adamw_small")
    small_shapes = {k: w[k].shape for k in SMALL_SHARDED}
    for tree, o in zip((g_sh, d_sh, m_sh, v_sh), outs):
        tree.update(_unpack(o.reshape(-1), small_shapes))

    rep_parts, = all_gather([_pack_replicated(grads)], name="gather_replicated_grads")
    outs = adamw_replicated(rep_parts, _pack_replicated(w), _pack_replicated(m), _pack_replicated(v),
                            name="adamw_replicated")
    rep_shapes = {k: w[k].shape for k in REPLICATED}
    g_rp, d_rp, m_rp, v_rp = [_unpack(o.reshape(-1), rep_shapes) for o in outs]

    pick = lambda sh, rp: [sh[k] if k in SHARDED else rp[k] for k in W_NAMES]
    return [*pick(g_sh, g_rp), *pick(d_sh, d_rp), *pick(m_sh, m_rp), *pick(v_sh, v_rp)]
```

```python
import functools
import math

import jax
import jax.numpy as jnp
from jax import lax
from jax.experimental import pallas as pl
from jax.experimental.pallas import tpu as pltpu

f32 = jnp.float32
bf16 = jnp.bfloat16
MXU_DTYPE = jnp.bfloat16

D_MODEL = 1024
EPS = 1e-6
LRU_WIDTH = 512
LRU_HEADS = 8
LRU_CONV = 4
LRU_C = 8.0
SB_WIDTH = 512
SB_HEAD_DIM = 64
SSM_WIDTH = 1024
SSM_HEADS = 16
SSM_HEAD_DIM = 64
SSM_GROUPS = 2
SSM_STATE = 128
SSM_CONV = 4
SSM_CHUNK = 128
SSM_XBC = SSM_WIDTH + 2 * SSM_GROUPS * SSM_STATE
CONF_WIDTH = 512
CONF_KERNEL = 31
MLP_HIDDEN = 4096
PLE_DIM = 256
LANES = 128
N_DEV = 8

ADAM_LR = 0.001
ADAM_B1 = 0.9
ADAM_B2 = 0.999
ADAM_EPS = 1e-08
ADAM_WD = 0.01
ADAM_STEP = 10

VMEM_LIMIT = 56 * 1024 * 1024


def _cparams(*sem):
    return pltpu.CompilerParams(dimension_semantics=sem, vmem_limit_bytes=VMEM_LIMIT)


def _mm(a, b):
    return jnp.dot(a.astype(MXU_DTYPE), b.astype(MXU_DTYPE), preferred_element_type=f32)


def _mm_nt(a, b):
    return lax.dot_general(a.astype(MXU_DTYPE), b.astype(MXU_DTYPE), (((1,), (1,)), ((), ())),
                           preferred_element_type=f32)


def _mm_tn(a, b):
    return lax.dot_general(a.astype(MXU_DTYPE), b.astype(MXU_DTYPE), (((0,), (0,)), ((), ())),
                           preferred_element_type=f32)


def _mm_exact(a, b):
    return jnp.dot(a, b, preferred_element_type=f32, precision=lax.Precision.HIGHEST)


@jax.custom_vjp
def dmm(a, b):
    return _mm(a, b)


def _dmm_fwd(a, b):
    return _mm(a, b), (a, b)


def _dmm_bwd(res, g):
    a, b = res
    return _mm_nt(g, b), _mm_tn(a, g)


dmm.defvjp(_dmm_fwd, _dmm_bwd)


@jax.custom_vjp
def dmm_nt(a, b):
    return _mm_nt(a, b)


def _dmm_nt_fwd(a, b):
    return _mm_nt(a, b), (a, b)


def _dmm_nt_bwd(res, g):
    a, b = res
    return _mm(g, b), _mm_tn(g, a)


dmm_nt.defvjp(_dmm_nt_fwd, _dmm_nt_bwd)


@jax.custom_vjp
def dmm_tn(a, b):
    return _mm_tn(a, b)


def _dmm_tn_fwd(a, b):
    return _mm_tn(a, b), (a, b)


def _dmm_tn_bwd(res, g):
    a, b = res
    return _mm_nt(b, g), _mm(a, g)


dmm_tn.defvjp(_dmm_tn_fwd, _dmm_tn_bwd)


def _rms(x, g):
    r = lax.rsqrt(jnp.mean(x * x, axis=-1, keepdims=True) + EPS)
    return x * r * g


def _rms_bwd(dy, x, g):
    r = lax.rsqrt(jnp.mean(x * x, axis=-1, keepdims=True) + EPS)
    dyg = dy * g
    dx = r * dyg - x * (r * r * r * jnp.mean(dyg * x, axis=-1, keepdims=True))
    return dx, dy * x * r


def _tok(tm, n):
    return pl.BlockSpec((tm, n), lambda i: (i, 0))


def _whole(shape):
    nd = len(shape)
    return pl.BlockSpec(tuple(shape), lambda i: (0,) * nd)


def _acc_rows(ref, val):
    s = jnp.sum(val, axis=0, keepdims=True)

    @pl.when(pl.program_id(0) == 0)
    def _():
        ref[...] = s

    @pl.when(pl.program_id(0) != 0)
    def _():
        ref[...] += s


TOKEN_TILE = 256


def norm_matmul(h, g, ws, out_dtypes, name):
    t, d = h.shape
    tm = TOKEN_TILE
    nw = len(ws)

    def body(h_ref, g_ref, *refs):
        hn = _rms(h_ref[...], g_ref[...]).astype(MXU_DTYPE)
        for w_ref, o_ref in zip(refs[:nw], refs[nw:]):
            o_ref[...] = jnp.dot(hn, w_ref[...], preferred_element_type=f32).astype(o_ref.dtype)

    return pl.pallas_call(
        body, name=name, grid=(t // tm,),
        in_specs=[_tok(tm, d), _whole(g.shape)] + [_whole(w.shape) for w in ws],
        out_specs=[_tok(tm, w.shape[1]) for w in ws],
        out_shape=[jax.ShapeDtypeStruct((t, w.shape[1]), dt) for w, dt in zip(ws, out_dtypes)],
        compiler_params=_cparams("parallel"),
    )(h, g, *ws)


def matmul_residual_norm(xs, ws, h, g, name, relu2=False):
    t, d = h.shape
    tm = TOKEN_TILE
    nx = len(xs)

    def body(*refs):
        x_refs, w_refs = refs[:nx], refs[nx:2 * nx]
        h_ref, g_ref, ho_ref, m_ref = refs[2 * nx:]
        m = None
        for x_ref, w_ref in zip(x_refs, w_refs):
            x = x_ref[...]
            if relu2:
                x = jnp.square(jnp.maximum(x.astype(f32), 0.0))
            part = jnp.dot(x.astype(MXU_DTYPE), w_ref[...], preferred_element_type=f32)
            m = part if m is None else m + part
        m_ref[...] = m
        ho_ref[...] = h_ref[...] + _rms(m, g_ref[...])

    return pl.pallas_call(
        body, name=name, grid=(t // tm,),
        in_specs=[_tok(tm, x.shape[1]) for x in xs] + [_whole(w.shape) for w in ws] + [_tok(tm, d), _whole(g.shape)],
        out_specs=[_tok(tm, d), _tok(tm, d)],
        out_shape=[jax.ShapeDtypeStruct((t, d), f32), jax.ShapeDtypeStruct((t, d), f32)],
        compiler_params=_cparams("parallel"),
    )(*xs, *ws, h, g)


def ple_forward(h, p, w_gate, w_proj, g, name):
    t, d = h.shape
    tm = TOKEN_TILE

    def body(h_ref, p_ref, wg_ref, wp_ref, g_ref, ho_ref, gl_ref, emb_ref):
        hh = h_ref[...]
        gl = jnp.dot(hh.astype(MXU_DTYPE), wg_ref[...], preferred_element_type=f32)
        emb = jnp.dot(p_ref[...].astype(MXU_DTYPE), wp_ref[...], preferred_element_type=f32)
        gl_ref[...] = gl
        emb_ref[...] = emb
        ho_ref[...] = hh + _rms(jax.nn.sigmoid(gl) * emb, g_ref[...])

    return pl.pallas_call(
        body, name=name, grid=(t // tm,),
        in_specs=[_tok(tm, d), _tok(tm, p.shape[1]), _whole(w_gate.shape), _whole(w_proj.shape), _whole(g.shape)],
        out_specs=[_tok(tm, d)] * 3,
        out_shape=[jax.ShapeDtypeStruct((t, d), f32)] * 3,
        compiler_params=_cparams("parallel"),
    )(h, p, w_gate, w_proj, g)


def loss_and_grad(h, target, name):
    t, d = h.shape
    tm = TOKEN_TILE

    def body(h_ref, t_ref, l_ref, dh_ref):
        e = h_ref[...] - t_ref[...]
        dh_ref[...] = e * (1.0 / d)
        part = jnp.sum(jnp.sum(e * e, axis=1, keepdims=True), axis=0, keepdims=True) * (0.5 / d)
        _acc_rows(l_ref, jnp.broadcast_to(part, (1, LANES)))

    return pl.pallas_call(
        body, name=name, grid=(t // tm,),
        in_specs=[_tok(tm, d), _tok(tm, d)],
        out_specs=[_whole((1, LANES)), _tok(tm, d)],
        out_shape=[jax.ShapeDtypeStruct((1, LANES), f32), jax.ShapeDtypeStruct((t, d), f32)],
        compiler_params=_cparams("arbitrary"),
    )(h, target)


def bwd_through_norm_in(dh, gs, wts, h, g, name):
    t, d = h.shape
    tm = TOKEN_TILE
    ng = len(gs)

    def body(*refs):
        dh_ref = refs[0]
        g_refs, w_refs = refs[1:1 + ng], refs[1 + ng:1 + 2 * ng]
        h_ref, gain_ref, dho_ref, dg_ref = refs[1 + 2 * ng:]
        dhn = None
        for g_ref, w_ref in zip(g_refs, w_refs):
            part = jnp.dot(g_ref[...].astype(MXU_DTYPE), w_ref[...], preferred_element_type=f32)
            dhn = part if dhn is None else dhn + part
        dx, dgr = _rms_bwd(dhn, h_ref[...], gain_ref[...])
        dho_ref[...] = dh_ref[...] + dx
        _acc_rows(dg_ref, dgr)

    return pl.pallas_call(
        body, name=name, grid=(t // tm,),
        in_specs=[_tok(tm, d)] + [_tok(tm, x.shape[1]) for x in gs] + [_whole(w.shape) for w in wts]
        + [_tok(tm, d), _whole(g.shape)],
        out_specs=[_tok(tm, d), _whole((1, d))],
        out_shape=[jax.ShapeDtypeStruct((t, d), f32), jax.ShapeDtypeStruct((1, d), f32)],
        compiler_params=_cparams("arbitrary"),
    )(dh, *gs, *wts, h, g)


def bwd_through_norm_out(dh, n, g, wts, out_dtypes, name, relu2_of=None, rider=None):
    t, d = n.shape
    tm = TOKEN_TILE
    nw = len(wts)
    has_a = relu2_of is not None

    def body(*refs):
        dh_ref, n_ref, gain_ref = refs[:3]
        w_refs = refs[3:3 + nw]
        rest = refs[3 + nw:]
        if has_a:
            a_ref, rest = rest[0], rest[1:]
        dn_ref, dx_refs, dg_ref = rest[0], rest[1:1 + nw], rest[1 + nw]
        dn, dgr = _rms_bwd(dh_ref[...], n_ref[...], gain_ref[...])
        dnb = dn.astype(MXU_DTYPE)
        dn_ref[...] = dnb.astype(dn_ref.dtype)
        for w_ref, dx_ref in zip(w_refs, dx_refs):
            dx = jnp.dot(dnb, w_ref[...], preferred_element_type=f32)
            if has_a:
                dx = dx * (2.0 * jnp.maximum(a_ref[...].astype(f32), 0.0))
            dx_ref[...] = dx.astype(dx_ref.dtype)
        _acc_rows(dg_ref, dgr)

    ins = [dh, n, g, *wts] + ([relu2_of] if has_a else [])
    in_specs = [_tok(tm, d), _tok(tm, d), _whole(g.shape)] + [_whole(w.shape) for w in wts]
    if has_a:
        in_specs.append(_tok(tm, relu2_of.shape[1]))
    grid = (t // tm,)
    body, in_specs, out_specs, out_shape, scratch = _ride(
        rider, body, in_specs, [_tok(tm, d)] + [_tok(tm, w.shape[1]) for w in wts] + [_whole((1, d))],
        [jax.ShapeDtypeStruct((t, d), MXU_DTYPE)]
        + [jax.ShapeDtypeStruct((t, w.shape[1]), dt) for w, dt in zip(wts, out_dtypes)]
        + [jax.ShapeDtypeStruct((1, d), f32)], [], grid)
    outs = pl.pallas_call(
        body, name=name, grid=grid, in_specs=in_specs, out_specs=out_specs, out_shape=out_shape,
        scratch_shapes=scratch, compiler_params=_cparams("arbitrary"),
    )(*ins, *(rider.inputs if rider else []))
    if rider:
        return outs[0], list(outs[1:1 + nw]), outs[1 + nw], list(outs[2 + nw:])
    return outs[0], list(outs[1:1 + nw]), outs[1 + nw]


def ple_backward(dh3, h2, gl, emb, g, w_gate_t, name):
    t, d = h2.shape
    tm = TOKEN_TILE

    def body(dh_ref, gl_ref, emb_ref, gain_ref, wt_ref, dho_ref, dgl_ref, demb_ref, dg_ref):
        gate = jax.nn.sigmoid(gl_ref[...])
        emb = emb_ref[...]
        dge, dgr = _rms_bwd(dh_ref[...], gate * emb, gain_ref[...])
        demb_ref[...] = (dge * gate).astype(demb_ref.dtype)
        dgl = (dge * emb * gate * (1.0 - gate)).astype(MXU_DTYPE)
        dgl_ref[...] = dgl.astype(dgl_ref.dtype)
        dho_ref[...] = dh_ref[...] + jnp.dot(dgl, wt_ref[...], preferred_element_type=f32)
        _acc_rows(dg_ref, dgr)

    return pl.pallas_call(
        body, name=name, grid=(t // tm,),
        in_specs=[_tok(tm, d), _tok(tm, d), _tok(tm, d), _whole(g.shape), _whole(w_gate_t.shape)],
        out_specs=[_tok(tm, d), _tok(tm, d), _tok(tm, d), _whole((1, d))],
        out_shape=[jax.ShapeDtypeStruct((t, d), f32), jax.ShapeDtypeStruct((t, d), MXU_DTYPE),
                   jax.ShapeDtypeStruct((t, d), MXU_DTYPE), jax.ShapeDtypeStruct((1, d), f32)],
        compiler_params=_cparams("arbitrary"),
    )(dh3, gl, emb, g, w_gate_t)


def _largest_tile(n, cap):
    if n <= cap:
        return n
    return max(c for c in range(LANES, cap + 1, LANES) if n % c == 0)


def weight_grad(x, gout, name, prologue="none", gain=None):
    t, k = x.shape
    n = gout.shape[1]
    tt = 512
    tn = _largest_tile(n, 1024)
    tk = k if prologue == "rms" else _largest_tile(k, 1024)
    has_gain = prologue == "rms"

    def body(*refs):
        if has_gain:
            x_ref, gain_ref, g_ref, o_ref = refs
        else:
            x_ref, g_ref, o_ref = refs
        x = x_ref[...].astype(f32)
        if prologue == "relu2":
            x = jnp.square(jnp.maximum(x, 0.0))
        elif prologue == "rms":
            x = _rms(x, gain_ref[...])
        part = _mm_tn(x, g_ref[...])

        @pl.when(pl.program_id(2) == 0)
        def _():
            o_ref[...] = part

        @pl.when(pl.program_id(2) != 0)
        def _():
            o_ref[...] += part

    in_specs = [pl.BlockSpec((tt, tk), lambda i, j, s: (s, i))]
    ins = [x]
    if has_gain:
        in_specs.append(pl.BlockSpec(gain.shape, lambda i, j, s: (0, 0)))
        ins.append(gain)
    in_specs.append(pl.BlockSpec((tt, tn), lambda i, j, s: (s, j)))
    ins.append(gout)
    return pl.pallas_call(
        body, name=name, grid=(k // tk, n // tn, t // tt),
        in_specs=in_specs,
        out_specs=pl.BlockSpec((tk, tn), lambda i, j, s: (i, j)),
        out_shape=jax.ShapeDtypeStruct((k, n), f32),
        compiler_params=_cparams("parallel", "parallel", "arbitrary"),
    )(*ins)


SEQ_TILE = 256
HALO = 8


def _first_step():
    return jnp.logical_and(pl.program_id(0) == 0, pl.program_id(1) == 0)


def _accum(ref, val, first):
    @pl.when(first)
    def _():
        ref[...] = val

    @pl.when(jnp.logical_not(first))
    def _():
        ref[...] += val


def _softplus(x):
    return jnp.maximum(x, 0.0) + jnp.log1p(jnp.exp(-jnp.abs(x)))


def _neg_expm1(z):
    series = -z * (1.0 + z * (0.5 + z * (1.0 / 6.0 + z * (1.0 / 24.0 + z * (1.0 / 120.0)))))
    return jnp.where(z > -0.05, series, 1.0 - jnp.exp(z))


def _lru_gates(xc, ga, gab, gx, gxb, lam):
    r = jax.nn.sigmoid(dmm(xc, ga) + gab)
    i = jax.nn.sigmoid(dmm(xc, gx) + gxb)
    log_a = -LRU_C * r * _softplus(-lam)
    a = jnp.exp(log_a)
    u = jnp.sqrt(_neg_expm1(2.0 * log_a)) * (i * xc)
    return a, u


def _scan_down(a, u):
    n = a.shape[0]
    rows = lax.broadcasted_iota(jnp.int32, a.shape, 0)
    d = 1
    while d < n:
        keep = rows >= d
        a_s = jnp.where(keep, pltpu.roll(a, d, 0), 1.0)
        u_s = jnp.where(keep, pltpu.roll(u, d, 0), 0.0)
        u = a * u_s + u
        a = a * a_s
        d *= 2
    return a, u


def _scan_up(b, g):
    n = b.shape[0]
    rows = lax.broadcasted_iota(jnp.int32, b.shape, 0)
    d = 1
    while d < n:
        keep = rows < n - d
        b_s = jnp.where(keep, pltpu.roll(b, n - d, 0), 1.0)
        g_s = jnp.where(keep, pltpu.roll(g, n - d, 0), 0.0)
        g = g + b * g_s
        b = b * b_s
        d *= 2
    return g


def _seq_specs(ts, c, nt, reverse=False):
    per = ts // HALO

    def jj(j):
        return (nt - 1 - j) if reverse else j

    tile = pl.BlockSpec((1, ts, c), lambda b, j: (b, jj(j), 0))
    before = pl.BlockSpec((1, HALO, c), lambda b, j: (b, jnp.maximum(jj(j) * per - 1, 0), 0))
    after = pl.BlockSpec((1, HALO, c), lambda b, j: (b, jnp.minimum((jj(j) + 1) * per, nt * per - 1), 0))
    return tile, before, after


def _const2(shape):
    nd = len(shape)
    return pl.BlockSpec(tuple(shape), lambda b, j: (0,) * nd)


def lru_forward(xpre, gate, cw, cb, ga, gab, gx, gxb, lam, name):
    nb, ns, w = xpre.shape
    ts = SEQ_TILE
    nt = ns // ts
    tile, _, _ = _seq_specs(ts, w, nt)

    def body(xp_ref, gt_ref, cw_ref, cb_ref, ga_ref, gab_ref, gx_ref, gxb_ref, lam_ref,
             y_ref, xc_ref, hs_ref, xin, hcar):
        @pl.when(pl.program_id(1) == 0)
        def _():
            xin[0:HALO, :] = jnp.zeros((HALO, w), f32)
            hcar[...] = jnp.zeros_like(hcar)

        xin[HALO:HALO + ts, :] = xp_ref[0]
        xc = jnp.broadcast_to(cb_ref[...], (ts, w))
        for k in range(LRU_CONV):
            xc = xc + cw_ref[k:k + 1, :] * xin[pl.ds(HALO - LRU_CONV + 1 + k, ts), :]
        xin[0:HALO, :] = xin[ts:ts + HALO, :]
        a, u = _lru_gates(xc, ga_ref[...], gab_ref[...], gx_ref[...], gxb_ref[...], lam_ref[...])
        acum, h = _scan_down(a, u)
        h = h + acum * hcar[0:1, :]
        hcar[0:1, :] = h[ts - 1:ts, :]
        xc_ref[0] = xc
        hs_ref[0] = h
        y_ref[0] = (h * jax.nn.gelu(gt_ref[0])).astype(y_ref.dtype)

    params = [cw, cb, ga, gab, gx, gxb, lam]
    return pl.pallas_call(
        body, name=name, grid=(nb, nt),
        in_specs=[tile, tile] + [_const2(p.shape) for p in params],
        out_specs=[tile, tile, tile],
        out_shape=[jax.ShapeDtypeStruct((nb, ns, w), MXU_DTYPE), jax.ShapeDtypeStruct((nb, ns, w), f32),
                   jax.ShapeDtypeStruct((nb, ns, w), f32)],
        scratch_shapes=[pltpu.VMEM((ts + HALO, w), f32), pltpu.VMEM((HALO, w), f32)],
        compiler_params=_cparams("arbitrary", "arbitrary"),
    )(xpre, gate, *params)


def lru_backward(dy, xpre, gate, xc, hs, cw, cb, ga, gab, gx, gxb, lam, name):
    nb, ns, w = xpre.shape
    ts = SEQ_TILE
    nt = ns // ts
    tile, before, _ = _seq_specs(ts, w, nt, reverse=True)

    def body(dy_ref, xp_ref, xpb_ref, gt_ref, xc_ref, hs_ref, hsb_ref,
             cw_ref, cb_ref, ga_ref, gab_ref, gx_ref, gxb_ref, lam_ref,
             dxp_ref, dgt_ref, dcw_ref, dcb_ref, dga_ref, dgab_ref, dgx_ref, dgxb_ref, dlam_ref,
             dxc_ext, gcar, xin):
        j = pl.program_id(1)
        first = _first_step()
        at_seq_start = j == nt - 1

        @pl.when(j == 0)
        def _():
            dxc_ext[ts:ts + HALO, :] = jnp.zeros((HALO, w), f32)
            gcar[...] = jnp.zeros_like(gcar)

        gt = gt_ref[0]
        h = hs_ref[0]
        dyv = dy_ref[0].astype(f32)
        gl, gelu_vjp = jax.vjp(jax.nn.gelu, gt)
        dgt_ref[0] = gelu_vjp(dyv * h)[0].astype(dgt_ref.dtype)
        dh = dyv * gl

        (a, _), gates_vjp = jax.vjp(_lru_gates, xc_ref[0], ga_ref[...], gab_ref[...], gx_ref[...], gxb_ref[...],
                                    lam_ref[...])
        rows = lax.broadcasted_iota(jnp.int32, (ts, w), 0)
        dh = dh + jnp.where(rows == ts - 1, gcar[0:1, :], 0.0)
        b = pltpu.roll(a, ts - 1, 0)
        g = _scan_up(b, dh)
        gcar[0:1, :] = a[0:1, :] * g[0:1, :]
        hprev_row = jnp.where(at_seq_start, 0.0, hsb_ref[0][HALO - 1:HALO, :])
        hprev = jnp.where(rows == 0, hprev_row, pltpu.roll(h, 1, 0))
        dxc, dga, dgab, dgx, dgxb, dlam = gates_vjp((g * hprev, g))

        _accum(dga_ref, dga, first)
        _accum(dgx_ref, dgx, first)
        _accum(dgab_ref, dgab, first)
        _accum(dgxb_ref, dgxb, first)
        _accum(dlam_ref, dlam, first)
        _accum(dcb_ref, jnp.sum(dxc, axis=0, keepdims=True), first)

        dxc_ext[0:ts, :] = dxc
        dxp = jnp.zeros((ts, w), f32)
        for k in range(LRU_CONV):
            dxp = dxp + cw_ref[k:k + 1, :] * dxc_ext[pl.ds(LRU_CONV - 1 - k, ts), :]
        dxp_ref[0] = dxp.astype(dxp_ref.dtype)
        dxc_ext[ts:ts + HALO, :] = dxc[0:HALO, :]

        xin[0:HALO, :] = jnp.where(at_seq_start, 0.0, xpb_ref[0])
        xin[HALO:HALO + ts, :] = xp_ref[0]
        dcw_rows = [jnp.sum(dxc * xin[pl.ds(HALO - LRU_CONV + 1 + k, ts), :], axis=0, keepdims=True)
                    for k in range(LRU_CONV)]
        dcw_rows += [jnp.zeros((1, w), f32)] * (HALO - LRU_CONV)
        _accum(dcw_ref, jnp.concatenate(dcw_rows, axis=0), first)

    params = [cw, cb, ga, gab, gx, gxb, lam]
    pshape = lambda p: jax.ShapeDtypeStruct(p.shape, f32)
    outs = pl.pallas_call(
        body, name=name, grid=(nb, nt),
        in_specs=[tile, tile, before, tile, tile, tile, before] + [_const2(p.shape) for p in params],
        out_specs=[tile, tile, _const2((HALO, w))] + [_const2(p.shape) for p in params[1:]],
        out_shape=[jax.ShapeDtypeStruct((nb, ns, w), MXU_DTYPE), jax.ShapeDtypeStruct((nb, ns, w), MXU_DTYPE),
                   jax.ShapeDtypeStruct((HALO, w), f32)] + [pshape(p) for p in params[1:]],
        scratch_shapes=[pltpu.VMEM((ts + HALO, w), f32), pltpu.VMEM((HALO, w), f32),
                        pltpu.VMEM((ts + HALO, w), f32)],
        compiler_params=_cparams("arbitrary", "arbitrary"),
    )(dy, xpre, xpre, gate, xc, hs, hs, *params)
    return outs


SB_TILE = 256


def _split_dot(x, m):
    hi = x.astype(MXU_DTYPE)
    lo = (x - hi.astype(f32)).astype(MXU_DTYPE)
    return jnp.dot(hi, m, preferred_element_type=f32) + jnp.dot(lo, m, preferred_element_type=f32)


def _suffix_matrices(n):
    r = lax.broadcasted_iota(jnp.int32, (n, n), 0)
    c = lax.broadcasted_iota(jnp.int32, (n, n), 1)
    return (r > c).astype(MXU_DTYPE), (r >= c).astype(MXU_DTYPE)


def _sb_logits(qh, kb, strict):
    z = _mm_nt(qh, kb)
    ls = jnp.minimum(z, 0.0) - jnp.log(1.0 + jnp.exp(-jnp.abs(z)))
    lk = ls - z
    if strict is not None:
        lk = jnp.where(strict, lk, 0.0)
    return ls, lk


def _head_masked(x, dtype):
    lane = lax.broadcasted_iota(jnp.int32, x.shape, 1)
    return (jnp.where(lane < SB_HEAD_DIM, x, 0.0).astype(dtype), jnp.where(lane >= SB_HEAD_DIM, x, 0.0).astype(dtype))


def _stack_heads(dst, x, tq):
    x0, x1 = _head_masked(x, dst.dtype)
    for blk in range(dst.shape[0]):
        dst[blk, 0:tq, :] = x0[blk * tq:(blk + 1) * tq]
        dst[blk, tq:2 * tq, :] = x1[blk * tq:(blk + 1) * tq]


def _strict_mask(tq):
    rr = lax.broadcasted_iota(jnp.int32, (2 * tq, tq), 0)
    cc = lax.broadcasted_iota(jnp.int32, (2 * tq, tq), 1)
    return cc < jnp.where(rr >= tq, rr - tq, rr)


def _sb_specs(ns):
    npair = SB_WIDTH // LANES
    q = pl.BlockSpec((1, ns, LANES), lambda b, p: (b, 0, p))
    k = pl.BlockSpec((1, ns, LANES), lambda b, p: (b, 0, npair + p))
    v = pl.BlockSpec((1, ns, LANES), lambda b, p: (b, 0, 2 * npair + p))
    return q, k, v, npair


def sb_forward(qkv, name, rider=None):
    nb, ns, _ = qkv.shape
    tq = SB_TILE
    nq = ns // tq
    qspec, kspec, vspec, npair = _sb_specs(ns)

    def body(q_ref, k_ref, v_ref, o_ref, qs, ks, vs, acc):
        scale = 1.0 / math.sqrt(SB_HEAD_DIM)
        _stack_heads(qs, q_ref[0] * scale, tq)
        ks[...] = k_ref[0].astype(MXU_DTYPE)
        _stack_heads(vs, v_ref[0], tq)
        mx, _ = _suffix_matrices(tq)
        strict = _strict_mask(tq)

        def step(q2, blks, r2, masked):
            kbs = [ks[pl.ds(pl.multiple_of(b * tq, tq), tq), :] for b in blks]
            lg = [_sb_logits(q2, kb, strict if masked else None) for kb in kbs]
            sums = [jnp.dot(lk.astype(MXU_DTYPE), mx, preferred_element_type=f32) for _, lk in lg]
            total = None
            for (ls, lk), s, b in zip(lg, sums, blks):
                a = r2 + s
                w = jnp.exp(ls + a)
                if masked:
                    w = jnp.where(strict, w, 0.0)
                wb = w.astype(MXU_DTYPE)
                part = jnp.dot(jnp.concatenate([wb[:tq], wb[tq:]], axis=1), vs[b], preferred_element_type=f32)
                total = part if total is None else total + part
                r2 = a[:, 0:1] + lk[:, 0:1]
            acc[...] += total
            return r2

        def q_block(qi, carry):
            acc[...] = jnp.zeros_like(acc)
            q2 = qs[qi]
            r2 = step(q2, [qi], jnp.zeros((2 * tq, 1), f32), True)
            r2 = lax.fori_loop(0, lax.shift_right_logical(qi, 1),
                               lambda i, r: step(q2, [qi - 1 - 2 * i, qi - 2 - 2 * i], r, False), r2)
            lax.cond(jnp.bitwise_and(qi, 1) == 1, lambda r: step(q2, [0], r, False), lambda r: r, r2)
            o_ref[0, pl.ds(pl.multiple_of(qi * tq, tq), tq), :] = acc[...]
            return carry

        lax.fori_loop(0, nq, q_block, 0)

    grid = (nb, npair)
    body, in_specs, out_specs, out_shape, scratch = _ride(
        rider, body, [qspec, kspec, vspec], [pl.BlockSpec((1, ns, LANES), lambda b, p: (b, 0, p))],
        [jax.ShapeDtypeStruct((nb, ns, SB_WIDTH), f32)],
        [pltpu.VMEM((nq, 2 * tq, LANES), MXU_DTYPE), pltpu.VMEM((ns, LANES), MXU_DTYPE),
         pltpu.VMEM((nq, 2 * tq, LANES), MXU_DTYPE), pltpu.VMEM((tq, LANES), f32)], grid)
    outs = pl.pallas_call(
        body, name=name, grid=grid, in_specs=in_specs, out_specs=out_specs, out_shape=out_shape,
        scratch_shapes=scratch,
        compiler_params=_cparams("arbitrary", "arbitrary") if rider else _cparams("parallel", "parallel"),
    )(qkv, qkv, qkv, *(rider.inputs if rider else []))
    return (outs[0], list(outs[1:])) if rider else outs[0]


def sb_backward(qkv, o, do, name, rider=None):
    nb, ns, _ = qkv.shape
    tq = SB_TILE
    nq = ns // tq
    qspec, kspec, vspec, npair = _sb_specs(ns)
    ospec = pl.BlockSpec((1, ns, LANES), lambda b, p: (b, 0, p))

    def body(q_ref, k_ref, v_ref, o_ref, do_ref, dq_ref, dk_ref, dv_ref, qs, ks, kcat, vs, dos, dqacc, dkacc, dvacc):
        scale = 1.0 / math.sqrt(SB_HEAD_DIM)
        _stack_heads(qs, q_ref[0] * scale, tq)
        ks[...] = k_ref[0].astype(MXU_DTYPE)
        _stack_heads(kcat, k_ref[0], tq)
        vs[...] = v_ref[0].astype(MXU_DTYPE)
        _stack_heads(dos, do_ref[0].astype(f32), tq)
        dkacc[...] = jnp.zeros_like(dkacc)
        dvacc[...] = jnp.zeros_like(dvacc)
        mx, mi = _suffix_matrices(tq)
        strict = _strict_mask(tq)

        def step(q2, do2, dtot2, blks, carry, masked):
            r2, g2 = carry
            k0s = [pl.multiple_of(b * tq, tq) for b in blks]
            lg = [_sb_logits(q2, ks[pl.ds(k0, tq), :], strict if masked else None) for k0 in k0s]
            dws = [_mm_nt(do2, vs[pl.ds(k0, tq), :]) for k0 in k0s]
            sums = [jnp.dot(lk.astype(MXU_DTYPE), mx, preferred_element_type=f32) for _, lk in lg]
            wbs, es = [], []
            for (ls, lk), s in zip(lg, sums):
                a = r2 + s
                w = jnp.exp(ls + a)
                if masked:
                    w = jnp.where(strict, w, 0.0)
                wbs.append(w.astype(MXU_DTYPE))
                r2 = a[:, 0:1] + lk[:, 0:1]
            es = [wb.astype(f32) * dw for wb, dw in zip(wbs, dws)]
            esums = [_split_dot(e, mi) for e in es]
            dq = None
            for (ls, _), e, esum, wb, b, k0 in zip(lg, es, esums, wbs, blks, k0s):
                esuf = g2 + esum
                beta = jnp.exp(ls)
                dz = e * (1.0 - beta) - beta * (dtot2 - esuf)
                if masked:
                    dz = jnp.where(strict, dz, 0.0)
                dzb = dz.astype(MXU_DTYPE)
                part = jnp.dot(jnp.concatenate([dzb[:tq], dzb[tq:]], axis=1), kcat[b], preferred_element_type=f32)
                dq = part if dq is None else dq + part
                dkacc[pl.ds(k0, tq), :] += _mm_tn(dzb, q2)
                dvacc[pl.ds(k0, tq), :] += _mm_tn(wb, do2)
                g2 = esuf[:, 0:1]
            dqacc[...] += dq
            return r2, g2

        def q_block(qi, carry):
            dqacc[...] = jnp.zeros_like(dqacc)
            q2, do2 = qs[qi], dos[qi]
            ov = o_ref[0, pl.ds(pl.multiple_of(qi * tq, tq), tq), :]
            dtot2 = jnp.sum(do2.astype(f32) * jnp.concatenate([ov, ov], axis=0), axis=1, keepdims=True)
            zero = jnp.zeros((2 * tq, 1), f32)
            c = step(q2, do2, dtot2, [qi], (zero, zero), True)
            c = lax.fori_loop(0, lax.shift_right_logical(qi, 1),
                              lambda i, c: step(q2, do2, dtot2, [qi - 1 - 2 * i, qi - 2 - 2 * i], c, False), c)
            lax.cond(jnp.bitwise_and(qi, 1) == 1, lambda c: step(q2, do2, dtot2, [0], c, False), lambda c: c, c)
            dq_ref[0, pl.ds(pl.multiple_of(qi * tq, tq), tq), :] = (dqacc[...] * scale).astype(dq_ref.dtype)
            return carry

        lax.fori_loop(0, nq, q_block, 0)
        dk_ref[0] = dkacc[...].astype(dk_ref.dtype)
        dv_ref[0] = dvacc[...].astype(dv_ref.dtype)

    dshape = jax.ShapeDtypeStruct((nb, ns, SB_WIDTH), MXU_DTYPE)
    stacked = pltpu.VMEM((nq, 2 * tq, LANES), MXU_DTYPE)
    flat = pltpu.VMEM((ns, LANES), MXU_DTYPE)
    grid = (nb, npair)
    body, in_specs, out_specs, out_shape, scratch = _ride(
        rider, body, [qspec, kspec, vspec, ospec, ospec], [ospec, ospec, ospec], [dshape, dshape, dshape],
        [stacked, flat, stacked, flat, stacked,
         pltpu.VMEM((tq, LANES), f32), pltpu.VMEM((ns, LANES), f32), pltpu.VMEM((ns, LANES), f32)], grid)
    outs = pl.pallas_call(
        body, name=name, grid=grid, in_specs=in_specs, out_specs=out_specs, out_shape=out_shape,
        scratch_shapes=scratch,
        compiler_params=_cparams("arbitrary", "arbitrary") if rider else _cparams("parallel", "parallel"),
    )(qkv, qkv, qkv, o, do, *(rider.inputs if rider else []))
    return (list(outs[:3]), list(outs[3:])) if rider else list(outs)


SSM_PAIRS = SSM_HEADS // 2
PAIRS_PER_GROUP = SSM_PAIRS // SSM_GROUPS
GROUP_WIDTH = SSM_WIDTH // SSM_GROUPS


def _silu(x):
    return x * jax.nn.sigmoid(x)


def _ssd_chunk(xs_pre, b_pre, c_pre, dt_raw, dt_raw_t, z, st, dt_bias_r, dt_bias_c, a_log_r, a_log_c, d_skip,
               gains):
    n = dt_raw.shape[0]
    rows = lax.broadcasted_iota(jnp.int32, (n, n), 0)
    cols = lax.broadcasted_iota(jnp.int32, (n, n), 1)
    tril = cols <= rows
    tri_l = tril.astype(f32)
    tri_u = (rows <= cols).astype(f32)
    lane = lax.broadcasted_iota(jnp.int32, (n, LANES), 1)
    sub = lax.broadcasted_iota(jnp.int32, (LANES, n), 0)

    dt = _softplus(dt_raw + dt_bias_r)
    a_r = -jnp.exp(a_log_r)
    da = dt * a_r
    acs = _mm_exact(tri_l, da)
    dt_t = _softplus(dt_raw_t + dt_bias_c)
    acs_t = _mm_exact(dt_t * (-jnp.exp(a_log_c)), tri_u)

    bs = [_silu(b) for b in b_pre]
    cs = [_silu(c) for c in c_pre]
    cb = [dmm_nt(cs[g], bs[g]) for g in range(SSM_GROUPS)]

    end = jnp.sum(da, axis=0, keepdims=True)
    lane_row = lax.broadcasted_iota(jnp.int32, (1, LANES), 1)
    first_head = lane < SSM_HEAD_DIM
    first_head_row = lane_row < SSM_HEAD_DIM

    def head_col(v, h):
        return jnp.sum(jnp.where((lane if v.shape[0] == n else lane_row) == h, v, 0.0), axis=1, keepdims=True)

    ys, st_new = [], []
    for p in range(SSM_PAIRS):
        g = p // PAIRS_PER_GROUP
        h0, h1 = 2 * p, 2 * p + 1
        xs = _silu(xs_pre[p])
        acols = [head_col(acs, h0), head_col(acs, h1)]
        dt_p = jnp.where(first_head, head_col(dt, h0), head_col(dt, h1))
        acs_p = jnp.where(first_head, acols[0], acols[1])
        end_p = jnp.where(first_head_row, head_col(end, h0), head_col(end, h1))
        dsk_p = jnp.where(first_head_row, head_col(d_skip, h0), head_col(d_skip, h1))
        xdt = xs * dt_p
        y = jnp.exp(acs_p) * dmm(cs[g], st[p])
        for hh in range(2):
            row = jnp.sum(jnp.where(sub == 2 * p + hh, acs_t, 0.0), axis=0, keepdims=True)
            decay = jnp.where(tril, jnp.exp(jnp.where(tril, acols[hh] - row, 0.0)), 0.0)
            head = first_head if hh == 0 else jnp.logical_not(first_head)
            y = y + dmm(cb[g] * decay, jnp.where(head, xdt, 0.0))
        st_new.append(jnp.exp(end_p) * st[p] + dmm_tn(bs[g], xdt * jnp.exp(end_p - acs_p)))
        ys.append(y + dsk_p * xs)
    out = []
    for g in range(SSM_GROUPS):
        yg = jnp.concatenate(ys[g * PAIRS_PER_GROUP:(g + 1) * PAIRS_PER_GROUP], axis=1) * _silu(z[g])
        out.append(_rms(yg, gains[g]))
    return out, st_new


def _ssd_chunk_inputs(xconv, dtr, z, st_ref, gain):
    xs_pre = [xconv[:, LANES * p:LANES * (p + 1)] for p in range(SSM_PAIRS)]
    b0 = SSM_WIDTH
    c0 = SSM_WIDTH + SSM_GROUPS * SSM_STATE
    b_pre = [xconv[:, b0 + SSM_STATE * g:b0 + SSM_STATE * (g + 1)] for g in range(SSM_GROUPS)]
    c_pre = [xconv[:, c0 + SSM_STATE * g:c0 + SSM_STATE * (g + 1)] for g in range(SSM_GROUPS)]
    zs = [z[:, GROUP_WIDTH * g:GROUP_WIDTH * (g + 1)] for g in range(SSM_GROUPS)]
    sts = [st_ref[p] for p in range(SSM_PAIRS)]
    gains = [gain[:, GROUP_WIDTH * g:GROUP_WIDTH * (g + 1)] for g in range(SSM_GROUPS)]
    return xs_pre, b_pre, c_pre, dtr, dtr.T, zs, sts, gains


def ssd_forward(xbc, dt_raw, z, cw, cb, dbr, dbc, alr, alc, dsk, gain, name):
    nb, ns, wx = xbc.shape
    ln = SSM_CHUNK
    nt = ns // ln
    tile = lambda c: pl.BlockSpec((1, ln, c), lambda b, j: (b, j, 0))
    st_spec = pl.BlockSpec((1, 1, SSM_PAIRS, SSM_STATE, LANES), lambda b, j: (b, j, 0, 0, 0))

    def body(xbc_ref, dt_ref, z_ref, cw_ref, cb_ref, dbr_ref, dbc_ref, alr_ref, alc_ref, dsk_ref, gain_ref,
             y_ref, xconv_ref, stp_ref, xin, st):
        @pl.when(pl.program_id(1) == 0)
        def _():
            xin[0:HALO, :] = jnp.zeros((HALO, wx), f32)
            st[...] = jnp.zeros_like(st)

        xin[HALO:HALO + ln, :] = xbc_ref[0]
        xconv = jnp.broadcast_to(cb_ref[...], (ln, wx))
        for k in range(SSM_CONV):
            xconv = xconv + cw_ref[k:k + 1, :] * xin[pl.ds(HALO - SSM_CONV + 1 + k, ln), :]
        xin[0:HALO, :] = xin[ln:ln + HALO, :]
        xconv_ref[0] = xconv
        stp_ref[0, 0] = st[...]
        xs_pre, b_pre, c_pre, dtr, dtr_t, zs, sts, gains = _ssd_chunk_inputs(xconv, dt_ref[0], z_ref[0], st,
                                                                             gain_ref[...])
        out, st_new = _ssd_chunk(xs_pre, b_pre, c_pre, dtr, dtr_t, zs, sts, dbr_ref[...], dbc_ref[...],
                                 alr_ref[...], alc_ref[...], dsk_ref[...], gains)
        y_ref[0] = jnp.concatenate(out, axis=1).astype(y_ref.dtype)
        for p in range(SSM_PAIRS):
            st[p] = st_new[p]

    params = [cw, cb, dbr, dbc, alr, alc, dsk, gain]
    return pl.pallas_call(
        body, name=name, grid=(nb, nt),
        in_specs=[tile(wx), tile(LANES), tile(SSM_WIDTH)] + [_const2(p.shape) for p in params],
        out_specs=[tile(SSM_WIDTH), tile(wx), st_spec],
        out_shape=[jax.ShapeDtypeStruct((nb, ns, SSM_WIDTH), MXU_DTYPE), jax.ShapeDtypeStruct((nb, ns, wx), f32),
                   jax.ShapeDtypeStruct((nb, nt, SSM_PAIRS, SSM_STATE, LANES), f32)],
        scratch_shapes=[pltpu.VMEM((ln + HALO, wx), f32), pltpu.VMEM((SSM_PAIRS, SSM_STATE, LANES), f32)],
        compiler_params=_cparams("arbitrary", "arbitrary"),
    )(xbc, dt_raw, z, *params)


def ssd_backward(dy, xbc, xconv, dt_raw, z, stp, cw, cb, dbr, dbc, alr, alc, dsk, gain, name):
    nb, ns, wx = xbc.shape
    ln = SSM_CHUNK
    nt = ns // ln
    per = ln // HALO
    rj = lambda j: nt - 1 - j
    tile = lambda c: pl.BlockSpec((1, ln, c), lambda b, j: (b, rj(j), 0))
    before = pl.BlockSpec((1, HALO, wx), lambda b, j: (b, jnp.maximum(rj(j) * per - 1, 0), 0))
    st_spec = pl.BlockSpec((1, 1, SSM_PAIRS, SSM_STATE, LANES), lambda b, j: (b, rj(j), 0, 0, 0))

    def body(dy_ref, xbc_ref, xbcb_ref, xconv_ref, dt_ref, z_ref, stp_ref,
             cw_ref, cb_ref, dbr_ref, dbc_ref, alr_ref, alc_ref, dsk_ref, gain_ref,
             dxbc_ref, ddt_ref, dz_ref, dcw_ref, dcb_ref, ddbr_ref, ddbc_ref, dalr_ref, dalc_ref, ddsk_ref, dgain_ref,
             dxc_ext, dst, xin):
        j = pl.program_id(1)
        first = _first_step()
        at_seq_start = j == nt - 1

        @pl.when(j == 0)
        def _():
            dxc_ext[ln:ln + HALO, :] = jnp.zeros((HALO, wx), f32)
            dst[...] = jnp.zeros_like(dst)

        xs_pre, b_pre, c_pre, dtr, dtr_t, zs, sts, gains = _ssd_chunk_inputs(xconv_ref[0], dt_ref[0], z_ref[0],
                                                                             stp_ref.at[0, 0], gain_ref[...])
        _, vjp = jax.vjp(_ssd_chunk, xs_pre, b_pre, c_pre, dtr, dtr_t, zs, sts, dbr_ref[...], dbc_ref[...],
                         alr_ref[...], alc_ref[...], dsk_ref[...], gains)
        dyv = dy_ref[0].astype(f32)
        cot = ([dyv[:, GROUP_WIDTH * g:GROUP_WIDTH * (g + 1)] for g in range(SSM_GROUPS)],
               [dst[p] for p in range(SSM_PAIRS)])
        dxs, db, dc, ddt, ddt_t, dzs, dsts, ddbr, ddbc, dalr, dalc, ddsk, dgains = vjp(cot)
        for p in range(SSM_PAIRS):
            dst[p] = dsts[p]
        ddt_ref[0] = (ddt + ddt_t.T).astype(ddt_ref.dtype)
        dz_ref[0] = jnp.concatenate(dzs, axis=1).astype(dz_ref.dtype)
        _accum(ddbr_ref, ddbr, first)
        _accum(ddbc_ref, ddbc, first)
        _accum(dalr_ref, dalr, first)
        _accum(dalc_ref, dalc, first)
        _accum(ddsk_ref, ddsk, first)
        _accum(dgain_ref, jnp.concatenate(dgains, axis=1), first)

        dxc = jnp.concatenate(dxs + db + dc, axis=1)
        _accum(dcb_ref, jnp.sum(dxc, axis=0, keepdims=True), first)
        dxc_ext[0:ln, :] = dxc
        dxp = jnp.zeros((ln, wx), f32)
        for k in range(SSM_CONV):
            dxp = dxp + cw_ref[k:k + 1, :] * dxc_ext[pl.ds(SSM_CONV - 1 - k, ln), :]
        dxbc_ref[0] = dxp.astype(dxbc_ref.dtype)
        dxc_ext[ln:ln + HALO, :] = dxc[0:HALO, :]

        xin[0:HALO, :] = jnp.where(at_seq_start, 0.0, xbcb_ref[0])
        xin[HALO:HALO + ln, :] = xbc_ref[0]
        dcw_rows = [jnp.sum(dxc * xin[pl.ds(HALO - SSM_CONV + 1 + k, ln), :], axis=0, keepdims=True)
                    for k in range(SSM_CONV)]
        dcw_rows += [jnp.zeros((1, wx), f32)] * (HALO - SSM_CONV)
        _accum(dcw_ref, jnp.concatenate(dcw_rows, axis=0), first)

    params = [cw, cb, dbr, dbc, alr, alc, dsk, gain]
    pshape = lambda p: jax.ShapeDtypeStruct(p.shape, f32)
    return pl.pallas_call(
        body, name=name, grid=(nb, nt),
        in_specs=[tile(SSM_WIDTH), tile(wx), before, tile(wx), tile(LANES), tile(SSM_WIDTH), st_spec]
        + [_const2(p.shape) for p in params],
        out_specs=[tile(wx), tile(LANES), tile(SSM_WIDTH), _const2((HALO, wx))] + [_const2(p.shape) for p in params[1:]],
        out_shape=[jax.ShapeDtypeStruct((nb, ns, wx), MXU_DTYPE), jax.ShapeDtypeStruct((nb, ns, LANES), MXU_DTYPE),
                   jax.ShapeDtypeStruct((nb, ns, SSM_WIDTH), MXU_DTYPE), jax.ShapeDtypeStruct((HALO, wx), f32)]
        + [pshape(p) for p in params[1:]],
        scratch_shapes=[pltpu.VMEM((ln + HALO, wx), f32), pltpu.VMEM((SSM_PAIRS, SSM_STATE, LANES), f32),
                        pltpu.VMEM((ln + HALO, wx), f32)],
        compiler_params=_cparams("arbitrary", "arbitrary"),
    )(dy, xbc, xbc, xconv, dt_raw, z, stp, *params)


CONF_HALO = 32
CONF_OFF = CONF_HALO - CONF_KERNEL + 1


def _conf_specs(ts, c, nt):
    per = ts // CONF_HALO
    tile = pl.BlockSpec((1, ts, c), lambda b, j: (b, j, 0))
    before = pl.BlockSpec((1, CONF_HALO, c), lambda b, j: (b, jnp.maximum(j * per - 1, 0), 0))
    after = pl.BlockSpec((1, CONF_HALO, c), lambda b, j: (b, jnp.minimum((j + 1) * per, nt * per - 1), 0))
    return tile, before, after


SUBLANES = 8


def _shifted_copies(dst, x):
    rows = x.shape[0]
    dst[0] = x
    for b in range(1, SUBLANES):
        dst[b] = pltpu.roll(x, rows - b, 0)


def _window(copies, off, size):
    b = off % SUBLANES
    return copies[b, pl.ds(off - b, size), :]


def _glu(x):
    return x[:, :CONF_WIDTH] * jax.nn.sigmoid(x[:, CONF_WIDTH:])


def _layernorm_parts(c):
    xc = c - jnp.mean(c, axis=-1, keepdims=True)
    r = lax.rsqrt(jnp.mean(xc * xc, axis=-1, keepdims=True) + EPS)
    return xc * r, r


def conf_forward(glu, cw, cb, ln_g, ln_b, name):
    nb, ns, wg = glu.shape
    w = CONF_WIDTH
    ts = SEQ_TILE
    nt = ns // ts
    tile, before, _ = _conf_specs(ts, wg, nt)

    def body(x_ref, xb_ref, cw_ref, cb_ref, g_ref, b_ref, y_ref, u_rot):
        _shifted_copies(u_rot, jnp.concatenate(
            [jnp.where(pl.program_id(1) == 0, 0.0, _glu(xb_ref[0])), _glu(x_ref[0])], axis=0))
        conv = jnp.broadcast_to(cb_ref[...], (ts, w))
        for k in range(CONF_KERNEL):
            conv = conv + cw_ref[k:k + 1, :] * _window(u_rot, CONF_OFF + k, ts)
        xhat, _ = _layernorm_parts(conv)
        y_ref[0] = _silu(xhat * g_ref[...] + b_ref[...]).astype(y_ref.dtype)

    params = [cw, cb, ln_g, ln_b]
    return pl.pallas_call(
        body, name=name, grid=(nb, nt),
        in_specs=[tile, before] + [_const2(p.shape) for p in params],
        out_specs=pl.BlockSpec((1, ts, w), lambda b, j: (b, j, 0)),
        out_shape=jax.ShapeDtypeStruct((nb, ns, w), MXU_DTYPE),
        scratch_shapes=[pltpu.VMEM((SUBLANES, ts + CONF_HALO, w), f32)],
        compiler_params=_cparams("parallel", "parallel"),
    )(glu, glu, *params)


def conf_backward(dy, glu, cw, cb, ln_g, ln_b, name):
    nb, ns, wg = glu.shape
    w = CONF_WIDTH
    ts = SEQ_TILE
    nt = ns // ts
    te = ts + CONF_HALO
    tile, before, after = _conf_specs(ts, wg, nt)
    dtile, _, dafter = _conf_specs(ts, w, nt)

    def body(dy_ref, dya_ref, x_ref, xb_ref, xa_ref, cw_ref, cb_ref, g_ref, b_ref,
             dx_ref, dcw_ref, dcb_ref, dg_ref, db_ref, u_ext, dc_ext):
        j = pl.program_id(1)
        first = _first_step()
        x = x_ref[0]
        _shifted_copies(u_ext, jnp.concatenate(
            [jnp.where(j == 0, 0.0, _glu(xb_ref[0])), _glu(x), _glu(xa_ref[0])], axis=0))
        conv = jnp.broadcast_to(cb_ref[...], (te, w))
        for k in range(CONF_KERNEL):
            conv = conv + cw_ref[k:k + 1, :] * _window(u_ext, CONF_OFF + k, te)
        xhat, r = _layernorm_parts(conv)
        lnout = xhat * g_ref[...] + b_ref[...]
        sg = jax.nn.sigmoid(lnout)
        rows = lax.broadcasted_iota(jnp.int32, (te, w), 0)
        dyv = jnp.concatenate([dy_ref[0].astype(f32), dya_ref[0].astype(f32)], axis=0)
        dyv = jnp.where(jnp.logical_and(j == nt - 1, rows >= ts), 0.0, dyv)
        dln = dyv * sg * (1.0 + lnout * (1.0 - sg))
        in_tile = rows < ts
        _accum(dg_ref, jnp.sum(jnp.where(in_tile, dln * xhat, 0.0), axis=0, keepdims=True), first)
        _accum(db_ref, jnp.sum(jnp.where(in_tile, dln, 0.0), axis=0, keepdims=True), first)
        dxh = dln * g_ref[...]
        dconv = r * (dxh - jnp.mean(dxh, axis=-1, keepdims=True) - xhat * jnp.mean(dxh * xhat, axis=-1, keepdims=True))
        _shifted_copies(dc_ext, dconv)
        dct = dconv[0:ts, :]
        _accum(dcb_ref, jnp.sum(dct, axis=0, keepdims=True), first)
        du = jnp.zeros((ts, w), f32)
        dcw_rows = []
        for k in range(CONF_KERNEL):
            du = du + cw_ref[k:k + 1, :] * _window(dc_ext, CONF_KERNEL - 1 - k, ts)
            dcw_rows.append(jnp.sum(dct * _window(u_ext, CONF_OFF + k, ts), axis=0, keepdims=True))
        dcw_rows.append(jnp.zeros((1, w), f32))
        _accum(dcw_ref, jnp.concatenate(dcw_rows, axis=0), first)
        sb = jax.nn.sigmoid(x[:, w:])
        dx_ref[0] = jnp.concatenate([du * sb, du * x[:, :w] * sb * (1.0 - sb)], axis=1).astype(dx_ref.dtype)

    params = [cw, cb, ln_g, ln_b]
    return pl.pallas_call(
        body, name=name, grid=(nb, nt),
        in_specs=[dtile, dafter, tile, before, after] + [_const2(p.shape) for p in params],
        out_specs=[tile] + [_const2(p.shape) for p in params],
        out_shape=[jax.ShapeDtypeStruct((nb, ns, wg), MXU_DTYPE)] + [jax.ShapeDtypeStruct(p.shape, f32) for p in params],
        scratch_shapes=[pltpu.VMEM((SUBLANES, te + CONF_HALO, w), f32), pltpu.VMEM((SUBLANES, te, w), f32)],
        compiler_params=_cparams("arbitrary", "arbitrary"),
    )(dy, dy, glu, glu, glu, *params)


def _row(v):
    return v.reshape(1, -1).astype(f32)


def _pad_to(v, n, axis):
    pads = [(0, 0)] * v.ndim
    pads[axis] = (0, n - v.shape[axis])
    return jnp.pad(v, pads)


def _block_diag(w):
    nh, d, _ = w.shape
    eye = jnp.eye(nh, dtype=w.dtype)
    return (eye[:, None, :, None] * w[:, :, None, :]).reshape(nh * d, nh * d)


def _diag_blocks(m, nh):
    d = m.shape[0] // nh
    idx = jnp.arange(nh)
    return m.reshape(nh, d, nh, d)[idx, :, idx, :]


def _mix_even_fwd(h, gpre, w, wl, nb, ns, rider=None):
    t = nb * ns
    w_in = wl["w_in"]
    w_lx, w_lg, w_qkv = w_in[:, :LRU_WIDTH], w_in[:, LRU_WIDTH:2 * LRU_WIDTH], w_in[:, 2 * LRU_WIDTH:]
    xpre, gate, qkv = norm_matmul(h, gpre, [w_lx, w_lg, w_qkv], [f32, f32, f32], name="ev_in_proj")
    lru_p = [w["ev_lru_conv_w"][0], _row(w["ev_lru_conv_b"][0]),
             _block_diag(w["ev_lru_gate_a_w"][0]).astype(MXU_DTYPE), _row(w["ev_lru_gate_a_b"][0]),
             _block_diag(w["ev_lru_gate_x_w"][0]).astype(MXU_DTYPE), _row(w["ev_lru_gate_x_b"][0]),
             _row(w["ev_lru_lambda"][0])]
    xpre3, gate3, qkv3 = xpre.reshape(nb, ns, -1), gate.reshape(nb, ns, -1), qkv.reshape(nb, ns, -1)
    y_a, xc, hs = lru_forward(xpre3, gate3, *lru_p, name="ev_lru_fwd")
    o = sb_forward(qkv3, name="ev_sb_fwd", rider=rider)
    carried = None
    if rider is not None:
        o, carried = o
    ys = [y_a.reshape(t, -1), o.reshape(t, -1)]
    saved = dict(xpre=xpre3, gate=gate3, qkv=qkv3, xc=xc, hs=hs, o=o, lru_p=lru_p)
    w_out = wl["w_out"]
    return ys, [w_out[:LRU_WIDTH], w_out[LRU_WIDTH:]], saved, carried


def _mix_even_bwd(dys, saved, wtl, nb, ns, rider=None):
    t = nb * ns
    dy_a, dy_b = [d.reshape(nb, ns, -1) for d in dys]
    outs = lru_backward(dy_a, saved["xpre"], saved["gate"], saved["xc"], saved["hs"], *saved["lru_p"],
                        name="ev_lru_bwd")
    dxp, dgt, dcw, dcb, dga, dgab, dgx, dgxb, dlam = outs
    carried = None
    if rider is None:
        dq, dk, dv = sb_backward(saved["qkv"], saved["o"], dy_b, name="ev_sb_bwd")
    else:
        (dq, dk, dv), carried = sb_backward(saved["qkv"], saved["o"], dy_b, name="ev_sb_bwd", rider=rider)
    w_in_t = wtl["w_in"]
    pieces = [dxp, dgt, dq, dk, dv]
    gs = [d.reshape(t, -1) for d in pieces]
    wts = [w_in_t[LRU_WIDTH * i:LRU_WIDTH * (i + 1)] for i in range(5)]
    grads = {
        "ev_lru_conv_w": dcw[:LRU_CONV][None], "ev_lru_conv_b": dcb,
        "ev_lru_gate_a_w": _diag_blocks(dga, LRU_HEADS)[None], "ev_lru_gate_a_b": dgab,
        "ev_lru_gate_x_w": _diag_blocks(dgx, LRU_HEADS)[None], "ev_lru_gate_x_b": dgxb,
        "ev_lru_lambda": dlam,
    }
    return gs, wts, grads, carried


def _odd_params(w):
    ssd_p = [w["od_ssm_conv_w"][0], _row(w["od_ssm_conv_b"][0]),
             _pad_to(_row(w["od_ssm_dt_bias"][0]), LANES, 1), _pad_to(_row(w["od_ssm_dt_bias"][0]), LANES, 1).T,
             _pad_to(_row(w["od_ssm_a_log"][0]), LANES, 1), _pad_to(_row(w["od_ssm_a_log"][0]), LANES, 1).T,
             _pad_to(_row(w["od_ssm_d"][0]), LANES, 1), _row(w["od_ssm_norm"][0])]
    conf_p = [_pad_to(w["od_cm_conv_w"][0], CONF_HALO, 0), _row(w["od_cm_conv_b"][0]),
              _row(w["od_cm_ln_g"][0]), _row(w["od_cm_ln_b"][0])]
    return ssd_p, conf_p


ODD_SPLITS = (SSM_WIDTH, SSM_WIDTH + SSM_XBC, SSM_WIDTH + SSM_XBC + SSM_HEADS)


def _mix_odd_fwd(h, gpre, w, wl, nb, ns, rider=None):
    assert rider is None
    t = nb * ns
    w_in = wl["w_in"]
    s0, s1, s2 = ODD_SPLITS
    w_z, w_xbc, w_dt, w_glu = w_in[:, :s0], w_in[:, s0:s1], _pad_to(w_in[:, s1:s2], LANES, 1), w_in[:, s2:]
    zz, xbc, dtr, glu = norm_matmul(h, gpre, [w_z, w_xbc, w_dt, w_glu], [f32] * 4, name="od_in_proj")
    ssd_p, conf_p = _odd_params(w)
    zz3, xbc3, dtr3, glu3 = [a.reshape(nb, ns, -1) for a in (zz, xbc, dtr, glu)]
    y_c, xconv, stp = ssd_forward(xbc3, dtr3, zz3, *ssd_p, name="od_ssd_fwd")
    y_d = conf_forward(glu3, *conf_p, name="od_conf_fwd")
    ys = [y_c.reshape(t, -1), y_d.reshape(t, -1)]
    saved = dict(z=zz3, xbc=xbc3, dtr=dtr3, glu=glu3, xconv=xconv, stp=stp, ssd_p=ssd_p, conf_p=conf_p)
    w_out = wl["w_out"]
    return ys, [w_out[:SSM_WIDTH], w_out[SSM_WIDTH:]], saved, None


def _mix_odd_bwd(dys, saved, wtl, nb, ns, rider=None):
    assert rider is None
    t = nb * ns
    dy_c, dy_d = [d.reshape(nb, ns, -1) for d in dys]
    outs = ssd_backward(dy_c, saved["xbc"], saved["xconv"], saved["dtr"], saved["z"], saved["stp"], *saved["ssd_p"],
                        name="od_ssd_bwd")
    dxbc, ddt, dz, dcw, dcb, ddbr, ddbc, dalr, dalc, ddsk, dgain = outs
    dglu, ccw, ccb, clg, clb = conf_backward(dy_d, saved["glu"], *saved["conf_p"], name="od_conf_bwd")
    w_in_t = wtl["w_in"]
    s0, s1, s2 = ODD_SPLITS
    carried = None
    gs = [d.reshape(t, -1) for d in (dz, dxbc, ddt, dglu)]
    wts = [w_in_t[:s0], w_in_t[s0:s1], _pad_to(w_in_t[s1:s2], LANES, 0), w_in_t[s2:]]
    nh = SSM_HEADS
    grads = {
        "od_ssm_conv_w": dcw[:SSM_CONV][None], "od_ssm_conv_b": dcb,
        "od_ssm_dt_bias": ddbr[:, :nh] + ddbc[:nh, 0][None], "od_ssm_a_log": dalr[:, :nh] + dalc[:nh, 0][None],
        "od_ssm_d": ddsk[:, :nh], "od_ssm_norm": dgain,
        "od_cm_conv_w": ccw[:CONF_KERNEL][None], "od_cm_conv_b": ccb, "od_cm_ln_g": clg, "od_cm_ln_b": clb,
    }
    return gs, wts, grads, carried


LAYER_MATRICES = ("w_in", "w_out", "mlp_w1", "mlp_w2", "ple_w_proj", "ple_w_gate")
NORM_NAMES = ("norm_mix_pre", "norm_mix_post", "norm_mlp_pre", "norm_mlp_post", "norm_ple")


class NoOverlap:
    def late_weights_rider(self):
        return None

    def late_weights(self, carried):
        raise NotImplementedError

    def late_grads_to_sibling(self, layer_grads):
        return None

    def late_grads_to_chips(self, carried):
        return None

    def late_grads_arrived(self, carried):
        pass


def local_step(x, p, target, w, wl, wtl, comm=NoOverlap()):
    nb, ns, d = x.shape
    t = nb * ns
    h = x.reshape(t, d)
    depth = p.shape[0]
    wl, wtl = list(wl), list(wtl)
    tapes = []
    for i in range(depth):
        even = i % 2 == 0
        tag = f"l{i}_"
        gpre = _row(w["norm_mix_pre"][i])
        rider = comm.late_weights_rider() if i == 0 else None
        ys, w_outs, saved, carried = (_mix_even_fwd if even else _mix_odd_fwd)(h, gpre, w, wl[i], nb, ns, rider)
        if rider is not None:
            wl[1], wtl[1] = comm.late_weights(carried)
        h1, m = matmul_residual_norm(ys, w_outs, h, _row(w["norm_mix_post"][i]), name=tag + "out_proj")
        a1, = norm_matmul(h1, _row(w["norm_mlp_pre"][i]), [wl[i]["mlp_w1"]], [f32], name=tag + "mlp_up")
        h2, f = matmul_residual_norm([a1], [wl[i]["mlp_w2"]], h1, _row(w["norm_mlp_post"][i]), name=tag + "mlp_down",
                                     relu2=True)
        pi = p[i].reshape(t, -1)
        h3, gl, emb = ple_forward(h2, pi, wl[i]["ple_w_gate"], wl[i]["ple_w_proj"], _row(w["norm_ple"][i]),
                                  name=tag + "ple")
        tapes.append(dict(h=h, ys=ys, w_outs=w_outs, saved=saved, h1=h1, m=m, a1=a1, h2=h2, f=f, pi=pi, gl=gl,
                          emb=emb))
        h = h3

    loss_row, dh = loss_and_grad(h, target.reshape(t, d), name="loss")
    grads = {}
    norm_grads = {k: [None] * depth for k in NORM_NAMES}
    layer_grads = [None] * depth
    for i in reversed(range(depth)):
        even = i % 2 == 0
        tag = f"l{i}_"
        tp = tapes[i]
        lg = {}
        to_sibling = comm.late_grads_to_sibling(layer_grads[1]) if i == 0 else None
        dh2, dgl, demb, dg = ple_backward(dh, tp["h2"], tp["gl"], tp["emb"], _row(w["norm_ple"][i]),
                                          wtl[i]["ple_w_gate"], name=tag + "ple_bwd")
        norm_grads["norm_ple"][i] = dg
        lg["ple_w_gate"] = weight_grad(tp["h2"], dgl, name=tag + "dw_gate")
        lg["ple_w_proj"] = weight_grad(tp["pi"], demb, name=tag + "dw_proj")
        outs = bwd_through_norm_out(dh2, tp["f"], _row(w["norm_mlp_post"][i]), [wtl[i]["mlp_w2"]], [MXU_DTYPE],
                                    name=tag + "mlp_down_bwd", relu2_of=tp["a1"], rider=to_sibling)
        d_f, (da1,), dg = outs[:3]
        to_chips = comm.late_grads_to_chips(outs[3]) if to_sibling is not None else None
        norm_grads["norm_mlp_post"][i] = dg
        lg["mlp_w2"] = weight_grad(tp["a1"], d_f, name=tag + "dw2", prologue="relu2")
        gpre = _row(w["norm_mlp_pre"][i])
        dh1, dg = bwd_through_norm_in(dh2, [da1], [wtl[i]["mlp_w1"]], tp["h1"], gpre, name=tag + "mlp_up_bwd")
        norm_grads["norm_mlp_pre"][i] = dg
        lg["mlp_w1"] = weight_grad(tp["h1"], da1, name=tag + "dw1", prologue="rms", gain=gpre)
        wt_out = wtl[i]["w_out"]
        split = tp["w_outs"][0].shape[0]
        dm, dys, dg = bwd_through_norm_out(dh1, tp["m"], _row(w["norm_mix_post"][i]),
                                           [wt_out[:, :split], wt_out[:, split:]], [f32, f32],
                                           name=tag + "out_proj_bwd")
        norm_grads["norm_mix_post"][i] = dg
        lg["w_out"] = jnp.concatenate([weight_grad(y, dm, name=tag + f"dw_out{k}") for k, y in enumerate(tp["ys"])],
                                      axis=0)
        gs, wts, mix_grads, carried = (_mix_even_bwd if even else _mix_odd_bwd)(dys, tp["saved"], wtl[i], nb, ns,
                                                                                to_chips)
        if to_chips is not None:
            comm.late_grads_arrived(carried)
        grads.update(mix_grads)
        gpre = _row(w["norm_mix_pre"][i])
        dh, dg = bwd_through_norm_in(dh1, gs, wts, tp["h"], gpre, name=tag + "in_proj_bwd")
        norm_grads["norm_mix_pre"][i] = dg
        dw_in = [weight_grad(tp["h"], g, name=tag + f"dw_in{k}", prologue="rms", gain=gpre) for k, g in enumerate(gs)]
        if not even:
            dw_in[2] = dw_in[2][:, :SSM_HEADS]
        lg["w_in"] = jnp.concatenate(dw_in, axis=1)
        layer_grads[i] = lg
    for k, v in norm_grads.items():
        grads[k] = jnp.concatenate(v, axis=0)
    return loss_row[0, 0], dh.reshape(nb, ns, d), grads, layer_grads


MESH_ID = pl.DeviceIdType.MESH
ANY = pl.BlockSpec(memory_space=pl.ANY)


def _mesh_pos():
    return lax.axis_index("x"), lax.axis_index("y"), lax.axis_index("c")


def all_gather(shards, name):
    return _run_alone(gather_rider(shards), name)


class Rider:
    def __init__(self, inputs, out_shapes, scratch_shapes, start, finish, middle=None):
        self.inputs, self.out_shapes, self.scratch_shapes = list(inputs), list(out_shapes), list(scratch_shapes)
        self.start, self.finish, self.middle = start, finish, middle


def _run_alone(rider, name):
    ni, no = len(rider.inputs), len(rider.out_shapes)

    def body(*refs):
        args = (refs[:ni], refs[ni:ni + no], refs[ni + no:])
        rider.start(*args)
        if rider.middle is not None:
            rider.middle(*args)
        rider.finish(*args)

    return pl.pallas_call(
        body, name=name, out_shape=rider.out_shapes, in_specs=[ANY] * ni, out_specs=[ANY] * no,
        scratch_shapes=rider.scratch_shapes,
    )(*rider.inputs)


def _ride(rider, body, in_specs, out_specs, out_shape, scratch_shapes, grid):
    in_specs, out_specs, out_shape = list(in_specs), list(out_specs), list(out_shape)
    scratch_shapes = list(scratch_shapes)
    if rider is None:
        return body, in_specs, out_specs, out_shape, scratch_shapes
    n_in, n_out, n_scr = len(in_specs), len(out_specs), len(scratch_shapes)
    ri, ro = len(rider.inputs), len(rider.out_shapes)
    total = math.prod(grid)

    def carrying(*refs):
        ins, r_ins = refs[:n_in], refs[n_in:n_in + ri]
        o0 = n_in + ri
        outs, r_outs = refs[o0:o0 + n_out], refs[o0 + n_out:o0 + n_out + ro]
        s0 = o0 + n_out + ro
        scr, r_scr = refs[s0:s0 + n_scr], refs[s0 + n_scr:]
        step = pl.program_id(0)
        for ax in range(1, len(grid)):
            step = step * grid[ax] + pl.program_id(ax)
        args = (r_ins, r_outs, r_scr)
        pl.when(step == 0)(lambda: rider.start(*args))
        if rider.middle is not None:
            pl.when(step == total // 2)(lambda: rider.middle(*args))
        body(*ins, *outs, *scr)
        pl.when(step == total - 1)(lambda: rider.finish(*args))

    return (carrying, in_specs + [ANY] * ri, out_specs + [ANY] * ro, out_shape + rider.out_shapes,
            scratch_shapes + rider.scratch_shapes)


def gather_rider(shards):
    n = len(shards)

    def parts(x_refs, out_refs, scr):
        send_sems, recv_sems, local_sems = scr
        x, y, c = _mesh_pos()
        chips = [(1 - x, y), (x, 1 - y), (1 - x, 1 - y)]

        def slot(a, px, py, pc):
            return out_refs[a].at[4 * px + 2 * py + pc]

        def copy(a, k, block, to, src=None):
            return pltpu.make_async_remote_copy(
                src_ref=slot(a, *block) if src is None else src, dst_ref=slot(a, *block),
                send_sem=send_sems.at[7 * a + k], recv_sem=recv_sems.at[7 * a + k], device_id=to,
                device_id_type=MESH_ID)

        me, sibling = (x, y, c), (x, y, 1 - c)
        def mine():
            return [pltpu.make_async_copy(x_refs[a], slot(a, *me), local_sems.at[a]) for a in range(n)]

        def first():
            out = []
            for j, chip in enumerate(chips):
                out += [copy(a, 1 + j, me, (*chip, c), src=x_refs[a]) for a in range(n)]
            return out + [copy(a, 0, me, sibling, src=x_refs[a]) for a in range(n)]

        def passed(j):
            return [copy(a, 4 + j, (*chips[j], c), sibling) for a in range(n)]

        return me, sibling, chips, c, copy, mine, first, passed

    def start(x_refs, out_refs, scr):
        _, _, _, _, _, mine, first, _ = parts(x_refs, out_refs, scr)
        for cp in mine() + first():
            cp.start()

    def middle(x_refs, out_refs, scr):
        me, _, chips, c, copy, _, _, passed = parts(x_refs, out_refs, scr)
        for j, chip in enumerate(chips):
            for a, fwd in enumerate(passed(j)):
                copy(a, 1 + j, (*chip, c), me).wait_recv()
                fwd.start()

    def finish(x_refs, out_refs, scr):
        me, sibling, chips, c, copy, mine, first, passed = parts(x_refs, out_refs, scr)
        for a in range(n):
            copy(a, 0, sibling, me).wait_recv()
        for j, chip in enumerate(chips):
            for a in range(n):
                copy(a, 4 + j, (*chip, 1 - c), me).wait_recv()
        for cp in first() + [cp for j in range(len(chips)) for cp in passed(j)]:
            cp.wait_send()
        for cp in mine():
            cp.wait()

    return Rider(shards, [jax.ShapeDtypeStruct((N_DEV,) + s.shape, s.dtype) for s in shards],
                 [pltpu.SemaphoreType.DMA((7 * n,)), pltpu.SemaphoreType.DMA((7 * n,)), pltpu.SemaphoreType.DMA((n,))],
                 start, finish, middle)


def scatter_to_sibling(parts, name):
    return _run_alone(sibling_rider(parts), name)


def sibling_rider(parts):
    n = len(parts)

    def copies(g_refs, out_refs, scr):
        send_sems, recv_sems = scr
        x, y, c = _mesh_pos()
        return [pltpu.make_async_remote_copy(
            src_ref=g_refs[a].at[2 * chip + (1 - c)], dst_ref=out_refs[a].at[chip],
            send_sem=send_sems.at[4 * a + chip], recv_sem=recv_sems.at[4 * a + chip], device_id=(x, y, 1 - c),
            device_id_type=MESH_ID) for a in range(n) for chip in range(4)]

    def start(*refs):
        for cp in copies(*refs):
            cp.start()

    def finish(*refs):
        cps = copies(*refs)
        for cp in cps:
            cp.wait_recv()
        for cp in cps:
            cp.wait_send()

    return Rider(parts, [jax.ShapeDtypeStruct((4,) + p.shape[1:], p.dtype) for p in parts],
                 [pltpu.SemaphoreType.DMA((4 * n,)), pltpu.SemaphoreType.DMA((4 * n,))], start, finish)


def scatter_to_chips(partials, name):
    return _run_alone(chips_rider(partials), name)


def chips_rider(partials):
    n = len(partials)

    def copies(p_refs, out_refs, scr):
        send_sems, recv_sems = scr
        x, y, c = _mesh_pos()
        chips = [(1 - x, y), (x, 1 - y), (1 - x, 1 - y)]
        return [pltpu.make_async_remote_copy(
            src_ref=p_refs[a].at[2 * px + py], dst_ref=out_refs[a].at[j],
            send_sem=send_sems.at[3 * a + j], recv_sem=recv_sems.at[3 * a + j], device_id=(px, py, c),
            device_id_type=MESH_ID) for a in range(n) for j, (px, py) in enumerate(chips)]

    def start(*refs):
        for cp in copies(*refs):
            cp.start()

    def finish(*refs):
        cps = copies(*refs)
        for cp in cps:
            cp.wait_recv()
        for cp in cps:
            cp.wait_send()

    return Rider(partials, [jax.ShapeDtypeStruct((3,) + p.shape[1:], p.dtype) for p in partials],
                 [pltpu.SemaphoreType.DMA((3 * n,)), pltpu.SemaphoreType.DMA((3 * n,))], start, finish)


ICI_DTYPE = jnp.bfloat16
ELEMENTWISE_BLOCK_BYTES = 1 << 20


def _row_tile(rows, cols):
    cap = max(16, ELEMENTWISE_BLOCK_BYTES // (4 * cols))
    best = [t for t in range(16, min(rows, cap) + 1, 16) if rows % t == 0]
    return best[-1] if best else rows


def add_sibling_parts(parts, received, core, name):
    _, r, n = parts.shape
    tr = _row_tile(r, n)

    def body(c_ref, a_ref, b_ref, o_ref, ob_ref):
        s = a_ref[...] + b_ref[...]
        o_ref[...] = s
        ob_ref[...] = s.astype(ob_ref.dtype)

    blk = pl.BlockSpec((1, tr, n), lambda i, j, c_ref: (i, j, 0))
    return pl.pallas_call(
        body, name=name,
        grid_spec=pltpu.PrefetchScalarGridSpec(
            num_scalar_prefetch=1, grid=(4, r // tr),
            in_specs=[pl.BlockSpec((1, tr, n), lambda i, j, c_ref: (2 * i + c_ref[0], j, 0)), blk],
            out_specs=[blk, blk]),
        out_shape=[jax.ShapeDtypeStruct((4, r, n), f32), jax.ShapeDtypeStruct((4, r, n), ICI_DTYPE)],
        compiler_params=_cparams("parallel", "parallel"),
    )(core, parts, received)


def _adamw(w, g, m, v):
    m = ADAM_B1 * m + (1.0 - ADAM_B1) * g
    v = ADAM_B2 * v + (1.0 - ADAM_B2) * jnp.square(g)
    m_hat = m / (1.0 - ADAM_B1 ** ADAM_STEP)
    v_hat = v / (1.0 - ADAM_B2 ** ADAM_STEP)
    delta = -ADAM_LR * (m_hat / (jnp.sqrt(v_hat) + ADAM_EPS) + ADAM_WD * w)
    return delta, m, v


def adamw_sharded(partial, received, chip, w, m, v, name):
    _, r, n = partial.shape

    def body(k_ref, p_ref, r_ref, w_ref, m_ref, v_ref, g_out, d_out, m_out, v_out):
        g = p_ref[0] + r_ref[0].astype(f32)
        g = g + r_ref[1].astype(f32)
        g = g + r_ref[2].astype(f32)
        delta, mn, vn = _adamw(w_ref[...], g, m_ref[...], v_ref[...])
        g_out[...] = g
        d_out[...] = delta
        m_out[...] = mn
        v_out[...] = vn

    tr = _row_tile(r, n)
    flat = pl.BlockSpec((tr, n), lambda j, k_ref: (j, 0))
    return pl.pallas_call(
        body, name=name,
        grid_spec=pltpu.PrefetchScalarGridSpec(
            num_scalar_prefetch=1, grid=(r // tr,),
            in_specs=[pl.BlockSpec((1, tr, n), lambda j, k_ref: (k_ref[0], j, 0)),
                      pl.BlockSpec((3, tr, n), lambda j, k_ref: (0, j, 0)), flat, flat, flat],
            out_specs=[flat] * 4),
        out_shape=[jax.ShapeDtypeStruct((r, n), f32)] * 4,
        compiler_params=_cparams("parallel"),
    )(chip, partial, received, w, m, v)


def adamw_replicated(gathered, w, m, v, name):
    _, r, n = gathered.shape

    def body(g_ref, w_ref, m_ref, v_ref, g_out, d_out, m_out, v_out):
        g = g_ref[0]
        for k in range(1, N_DEV):
            g = g + g_ref[k]
        delta, mn, vn = _adamw(w_ref[...], g, m_ref[...], v_ref[...])
        g_out[...] = g
        d_out[...] = delta
        m_out[...] = mn
        v_out[...] = vn

    return pl.pallas_call(
        body, name=name,
        out_shape=[jax.ShapeDtypeStruct((r, n), f32)] * 4,
        compiler_params=pltpu.CompilerParams(vmem_limit_bytes=VMEM_LIMIT),
    )(gathered, w, m, v)


W_NAMES = ['ev_w_in', 'ev_lru_conv_w', 'ev_lru_conv_b', 'ev_lru_gate_a_w', 'ev_lru_gate_a_b', 'ev_lru_gate_x_w',
           'ev_lru_gate_x_b', 'ev_lru_lambda', 'ev_w_out', 'od_w_in', 'od_ssm_conv_w', 'od_ssm_conv_b',
           'od_ssm_dt_bias', 'od_ssm_a_log', 'od_ssm_d', 'od_ssm_norm', 'od_cm_conv_w', 'od_cm_conv_b', 'od_cm_ln_g',
           'od_cm_ln_b', 'od_w_out', 'norm_mix_pre', 'norm_mix_post', 'norm_mlp_pre', 'norm_mlp_post', 'norm_ple',
           'mlp_w1', 'mlp_w2', 'ple_w_proj', 'ple_w_gate']
BIG_SHARDED = {'ev_w_in': 2, 'ev_w_out': 1, 'od_w_in': 2, 'od_w_out': 1, 'mlp_w1': 2, 'mlp_w2': 1, 'ple_w_proj': 2,
               'ple_w_gate': 1}
SMALL_SHARDED = {'ev_lru_conv_w': 2, 'od_ssm_conv_w': 2, 'od_ssm_conv_b': 1, 'od_ssm_norm': 1, 'od_cm_conv_w': 2,
                 'od_cm_conv_b': 1, 'od_cm_ln_g': 1, 'od_cm_ln_b': 1}
SHARDED = {**BIG_SHARDED, **SMALL_SHARDED}
REPLICATED = [n for n in W_NAMES if n not in SHARDED]


def _round_up(n, k):
    return -(-n // k) * k


def _pack_rows(flat, rows_multiple):
    n = flat.shape[0]
    total = _round_up(n, LANES * rows_multiple)
    return jnp.pad(flat, (0, total - n)).reshape(-1, LANES)


def _unpack(flat, shapes):
    out, off = {}, 0
    for name, shape in shapes.items():
        size = math.prod(shape)
        out[name] = flat[off:off + size].reshape(shape)
        off += size
    return out


def _rows(a):
    return a.reshape(-1, a.shape[-1])


def _unshard(g8, shape, axis):
    g = jnp.moveaxis(g8.reshape((N_DEV,) + tuple(shape)), 0, axis)
    return g.reshape(tuple(shape[:axis]) + (N_DEV * shape[axis],) + tuple(shape[axis + 1:]))


def _to_shards(g, axis):
    shard = g.shape[axis] // N_DEV
    g = g.reshape(g.shape[:axis] + (N_DEV, shard) + g.shape[axis + 1:])
    return jnp.moveaxis(g, axis, 0)


def _pack_small(tree):
    return _pack_rows(jnp.concatenate([tree[k].astype(f32).reshape(-1) for k in SMALL_SHARDED]), 16)


def _layer_entry(i, key):
    mixer = "ev" if i % 2 == 0 else "od"
    return {"w_in": (mixer + "_w_in", i // 2, 1), "w_out": (mixer + "_w_out", i // 2, 0),
            "mlp_w1": ("mlp_w1", i, 1), "mlp_w2": ("mlp_w2", i, 0),
            "ple_w_proj": ("ple_w_proj", i, 1), "ple_w_gate": ("ple_w_gate", i, 0)}[key]


def _layer_shards(tree, i):
    out = {}
    for key in LAYER_MATRICES:
        name, idx, _ = _layer_entry(i, key)
        out[key] = tree[name][idx]
    return out


def _assemble_layer(i, gathered):
    wl = {key: _unshard(g8, g8.shape[1:], _layer_entry(i, key)[2]) for key, g8 in zip(LAYER_MATRICES, gathered)}
    return wl, {key: v.T for key, v in wl.items()}


def _gather_early(w):
    small = _pack_small(w)
    terms, rest = [], small
    for _ in range(3):
        term = rest.astype(MXU_DTYPE)
        terms.append(term)
        rest = rest - term.astype(f32)
    shards = _layer_shards(w, 0)
    outs = all_gather([shards[key].astype(MXU_DTYPE) for key in LAYER_MATRICES] + [jnp.concatenate(terms, axis=0)],
                      name="gather_weights")
    wl, wtl = _assemble_layer(0, outs[:-1])
    full = {k: w[k] for k in REPLICATED}
    t = outs[-1].astype(f32)
    nr = small.shape[0]
    vals = (t[:, :nr] + t[:, nr:2 * nr] + t[:, 2 * nr:]).reshape(N_DEV, -1)
    off = 0
    for k, axis in SMALL_SHARDED.items():
        size = math.prod(w[k].shape)
        full[k] = _unshard(vals[:, off:off + size], w[k].shape, axis)
        off += size
    return full, wl, wtl


class Overlap:
    def __init__(self, w):
        self.w = w
        self.sums = self.from_chips = None

    def late_weights_rider(self):
        shards = _layer_shards(self.w, 1)
        return gather_rider([shards[key].astype(MXU_DTYPE) for key in LAYER_MATRICES])

    def late_weights(self, carried):
        return _assemble_layer(1, carried)

    def late_grads_to_sibling(self, layer_grads):
        self.parts = [_to_shards(layer_grads[key], _layer_entry(1, key)[2]) for key in LAYER_MATRICES]
        return sibling_rider(self.parts)

    def late_grads_to_chips(self, carried):
        core = jnp.reshape(lax.axis_index("c"), (1,)).astype(jnp.int32)
        self.sums = [add_sibling_parts(a, b, core, name=f"add_sibling_l1_{key}")
                     for key, a, b in zip(LAYER_MATRICES, self.parts, carried)]
        return chips_rider([s[1] for s in self.sums])

    def late_grads_arrived(self, carried):
        self.from_chips = carried


def _pack_replicated(tree):
    return _pack_rows(jnp.concatenate([tree[k].astype(f32).reshape(-1) for k in REPLICATED]), 8)


def kernel(x, p, ev_w_in, ev_lru_conv_w, ev_lru_conv_b, ev_lru_gate_a_w, ev_lru_gate_a_b, ev_lru_gate_x_w, ev_lru_gate_x_b, ev_lru_lambda, ev_w_out, od_w_in, od_ssm_conv_w, od_ssm_conv_b, od_ssm_dt_bias, od_ssm_a_log, od_ssm_d, od_ssm_norm, od_cm_conv_w, od_cm_conv_b, od_cm_ln_g, od_cm_ln_b, od_w_out, norm_mix_pre, norm_mix_post, norm_mlp_pre, norm_mlp_post, norm_ple, mlp_w1, mlp_w2, ple_w_proj, ple_w_gate, loss_target, m_ev_w_in, m_ev_lru_conv_w, m_ev_lru_conv_b, m_ev_lru_gate_a_w, m_ev_lru_gate_a_b, m_ev_lru_gate_x_w, m_ev_lru_gate_x_b, m_ev_lru_lambda, m_ev_w_out, m_od_w_in, m_od_ssm_conv_w, m_od_ssm_conv_b, m_od_ssm_dt_bias, m_od_ssm_a_log, m_od_ssm_d, m_od_ssm_norm, m_od_cm_conv_w, m_od_cm_conv_b, m_od_cm_ln_g, m_od_cm_ln_b, m_od_w_out, m_norm_mix_pre, m_norm_mix_post, m_norm_mlp_pre, m_norm_mlp_post, m_norm_ple, m_mlp_w1, m_mlp_w2, m_ple_w_proj, m_ple_w_gate, v_ev_w_in, v_ev_lru_conv_w, v_ev_lru_conv_b, v_ev_lru_gate_a_w, v_ev_lru_gate_a_b, v_ev_lru_gate_x_w, v_ev_lru_gate_x_b, v_ev_lru_lambda, v_ev_w_out, v_od_w_in, v_od_ssm_conv_w, v_od_ssm_conv_b, v_od_ssm_dt_bias, v_od_ssm_a_log, v_od_ssm_d, v_od_ssm_norm, v_od_cm_conv_w, v_od_cm_conv_b, v_od_cm_ln_g, v_od_cm_ln_b, v_od_w_out, v_norm_mix_pre, v_norm_mix_post, v_norm_mlp_pre, v_norm_mlp_post, v_norm_ple, v_mlp_w1, v_mlp_w2, v_ple_w_proj, v_ple_w_gate):
    ws = [ev_w_in, ev_lru_conv_w, ev_lru_conv_b, ev_lru_gate_a_w, ev_lru_gate_a_b, ev_lru_gate_x_w, ev_lru_gate_x_b, ev_lru_lambda, ev_w_out, od_w_in, od_ssm_conv_w, od_ssm_conv_b, od_ssm_dt_bias, od_ssm_a_log, od_ssm_d, od_ssm_norm, od_cm_conv_w, od_cm_conv_b, od_cm_ln_g, od_cm_ln_b, od_w_out, norm_mix_pre, norm_mix_post, norm_mlp_pre, norm_mlp_post, norm_ple, mlp_w1, mlp_w2, ple_w_proj, ple_w_gate]
    ms = [m_ev_w_in, m_ev_lru_conv_w, m_ev_lru_conv_b, m_ev_lru_gate_a_w, m_ev_lru_gate_a_b, m_ev_lru_gate_x_w, m_ev_lru_gate_x_b, m_ev_lru_lambda, m_ev_w_out, m_od_w_in, m_od_ssm_conv_w, m_od_ssm_conv_b, m_od_ssm_dt_bias, m_od_ssm_a_log, m_od_ssm_d, m_od_ssm_norm, m_od_cm_conv_w, m_od_cm_conv_b, m_od_cm_ln_g, m_od_cm_ln_b, m_od_w_out, m_norm_mix_pre, m_norm_mix_post, m_norm_mlp_pre, m_norm_mlp_post, m_norm_ple, m_mlp_w1, m_mlp_w2, m_ple_w_proj, m_ple_w_gate]
    vs = [v_ev_w_in, v_ev_lru_conv_w, v_ev_lru_conv_b, v_ev_lru_gate_a_w, v_ev_lru_gate_a_b, v_ev_lru_gate_x_w, v_ev_lru_gate_x_b, v_ev_lru_lambda, v_ev_w_out, v_od_w_in, v_od_ssm_conv_w, v_od_ssm_conv_b, v_od_ssm_dt_bias, v_od_ssm_a_log, v_od_ssm_d, v_od_ssm_norm, v_od_cm_conv_w, v_od_cm_conv_b, v_od_cm_ln_g, v_od_cm_ln_b, v_od_w_out, v_norm_mix_pre, v_norm_mix_post, v_norm_mlp_pre, v_norm_mlp_post, v_norm_ple, v_mlp_w1, v_mlp_w2, v_ple_w_proj, v_ple_w_gate]
    w = dict(zip(W_NAMES, ws))
    m = dict(zip(W_NAMES, ms))
    v = dict(zip(W_NAMES, vs))
    full, wl0, wtl0 = _gather_early(w)
    comm = Overlap(w)
    loss_local, grad_x, grads, layer_grads = local_step(x, p, loss_target, full, [wl0, None], [wtl0, None], comm)
    loss = lax.psum(loss_local, ("x", "y", "c"))
    return (loss, grad_x, *_reduce_and_update(grads, layer_grads, w, m, v, comm))


def _reduce_and_update(grads, layer_grads, w, m, v, comm):
    mx, my, mc = _mesh_pos()

    parts = [_to_shards(layer_grads[0][key], _layer_entry(0, key)[2]) for key in LAYER_MATRICES]
    small = jnp.concatenate([_to_shards(grads[k], axis).reshape(N_DEV, -1) for k, axis in SMALL_SHARDED.items()],
                            axis=1)
    small_rows = _pack_small(w).shape[0]
    small = jnp.pad(small, ((0, 0), (0, small_rows * LANES - small.shape[1]))).reshape(N_DEV, small_rows, LANES)
    parts.append(small)
    from_sibling = scatter_to_sibling(parts, name="scatter_sibling")
    core = jnp.reshape(mc, (1,)).astype(jnp.int32)
    sums = [add_sibling_parts(a, b, core, name=f"add_sibling_{i}") for i, (a, b) in enumerate(zip(parts, from_sibling))]
    from_chips = scatter_to_chips([s[1] for s in sums], name="scatter_chips")
    chip = jnp.reshape(2 * mx + my, (1,)).astype(jnp.int32)
    per_layer = []
    for i, (layer_sums, layer_from_chips) in enumerate(((sums, from_chips), (comm.sums, comm.from_chips))):
        ws, ms, vs = _layer_shards(w, i), _layer_shards(m, i), _layer_shards(v, i)
        per_layer.append({key: adamw_sharded(layer_sums[j][0], layer_from_chips[j], chip, ws[key], ms[key], vs[key],
                                             name=f"adamw_l{i}_{key}") for j, key in enumerate(LAYER_MATRICES)})
    g_sh, d_sh, m_sh, v_sh = {}, {}, {}, {}
    for which, tree in enumerate((g_sh, d_sh, m_sh, v_sh)):
        for i in range(len(per_layer)):
            for key in LAYER_MATRICES:
                name, idx, _ = _layer_entry(i, key)
                tree.setdefault(name, {})[idx] = per_layer[i][key][which]
        for name in BIG_SHARDED:
            tree[name] = jnp.stack([tree[name][idx] for idx in sorted(tree[name])], axis=0)
    outs = adamw_sharded(sums[-1][0], from_chips[-1], chip, _pack_small(w), _pack_small(m), _pack_small(v),
                         name="adamw_small")
    small_shapes = {k: w[k].shape for k in SMALL_SHARDED}
    for tree, o in zip((g_sh, d_sh, m_sh, v_sh), outs):
        tree.update(_unpack(o.reshape(-1), small_shapes))

    rep_parts, = all_gather([_pack_replicated(grads)], name="gather_replicated_grads")
    outs = adamw_replicated(rep_parts, _pack_replicated(w), _pack_replicated(m), _pack_replicated(v),
                            name="adamw_replicated")
    rep_shapes = {k: w[k].shape for k in REPLICATED}
    g_rp, d_rp, m_rp, v_rp = [_unpack(o.reshape(-1), rep_shapes) for o in outs]

    pick = lambda sh, rp: [sh[k] if k in SHARDED else rp[k] for k in W_NAMES]
    return [*pick(g_sh, g_rp), *pick(d_sh, d_rp), *pick(m_sh, m_rp), *pick(v_sh, v_rp)]
```

```python
import functools
import math

import jax
import jax.numpy as jnp
from jax import lax
from jax.experimental import pallas as pl
from jax.experimental.pallas import tpu as pltpu

f32 = jnp.float32
bf16 = jnp.bfloat16
MXU_DTYPE = jnp.bfloat16

D_MODEL = 1024
EPS = 1e-6
LRU_WIDTH = 512
LRU_HEADS = 8
LRU_CONV = 4
LRU_C = 8.0
SB_WIDTH = 512
SB_HEAD_DIM = 64
SSM_WIDTH = 1024
SSM_HEADS = 16
SSM_HEAD_DIM = 64
SSM_GROUPS = 2
SSM_STATE = 128
SSM_CONV = 4
SSM_CHUNK = 128
SSM_XBC = SSM_WIDTH + 2 * SSM_GROUPS * SSM_STATE
CONF_WIDTH = 512
CONF_KERNEL = 31
MLP_HIDDEN = 4096
PLE_DIM = 256
LANES = 128
N_DEV = 8

ADAM_LR = 0.001
ADAM_B1 = 0.9
ADAM_B2 = 0.999
ADAM_EPS = 1e-08
ADAM_WD = 0.01
ADAM_STEP = 10

VMEM_LIMIT = 56 * 1024 * 1024


def _cparams(*sem):
    return pltpu.CompilerParams(dimension_semantics=sem, vmem_limit_bytes=VMEM_LIMIT)


def _mm(a, b):
    return jnp.dot(a.astype(MXU_DTYPE), b.astype(MXU_DTYPE), preferred_element_type=f32)


def _mm_nt(a, b):
    return lax.dot_general(a.astype(MXU_DTYPE), b.astype(MXU_DTYPE), (((1,), (1,)), ((), ())),
                           preferred_element_type=f32)


def _mm_tn(a, b):
    return lax.dot_general(a.astype(MXU_DTYPE), b.astype(MXU_DTYPE), (((0,), (0,)), ((), ())),
                           preferred_element_type=f32)


def _mm_exact(a, b):
    return jnp.dot(a, b, preferred_element_type=f32, precision=lax.Precision.HIGHEST)


@jax.custom_vjp
def dmm(a, b):
    return _mm(a, b)


def _dmm_fwd(a, b):
    return _mm(a, b), (a, b)


def _dmm_bwd(res, g):
    a, b = res
    return _mm_nt(g, b), _mm_tn(a, g)


dmm.defvjp(_dmm_fwd, _dmm_bwd)


@jax.custom_vjp
def dmm_nt(a, b):
    return _mm_nt(a, b)


def _dmm_nt_fwd(a, b):
    return _mm_nt(a, b), (a, b)


def _dmm_nt_bwd(res, g):
    a, b = res
    return _mm(g, b), _mm_tn(g, a)


dmm_nt.defvjp(_dmm_nt_fwd, _dmm_nt_bwd)


@jax.custom_vjp
def dmm_tn(a, b):
    return _mm_tn(a, b)


def _dmm_tn_fwd(a, b):
    return _mm_tn(a, b), (a, b)


def _dmm_tn_bwd(res, g):
    a, b = res
    return _mm_nt(b, g), _mm(a, g)


dmm_tn.defvjp(_dmm_tn_fwd, _dmm_tn_bwd)


def _rms(x, g):
    r = lax.rsqrt(jnp.mean(x * x, axis=-1, keepdims=True) + EPS)
    return x * r * g


def _rms_bwd(dy, x, g):
    r = lax.rsqrt(jnp.mean(x * x, axis=-1, keepdims=True) + EPS)
    dyg = dy * g
    dx = r * dyg - x * (r * r * r * jnp.mean(dyg * x, axis=-1, keepdims=True))
    return dx, dy * x * r


def _tok(tm, n):
    return pl.BlockSpec((tm, n), lambda i: (i, 0))


def _whole(shape):
    nd = len(shape)
    return pl.BlockSpec(tuple(shape), lambda i: (0,) * nd)


def _acc_rows(ref, val):
    s = jnp.sum(val, axis=0, keepdims=True)

    @pl.when(pl.program_id(0) == 0)
    def _():
        ref[...] = s

    @pl.when(pl.program_id(0) != 0)
    def _():
        ref[...] += s


TOKEN_TILE = 256


def norm_matmul(h, g, ws, out_dtypes, name):
    t, d = h.shape
    tm = TOKEN_TILE
    nw = len(ws)

    def body(h_ref, g_ref, *refs):
        hn = _rms(h_ref[...], g_ref[...]).astype(MXU_DTYPE)
        for w_ref, o_ref in zip(refs[:nw], refs[nw:]):
            o_ref[...] = jnp.dot(hn, w_ref[...], preferred_element_type=f32).astype(o_ref.dtype)

    return pl.pallas_call(
        body, name=name, grid=(t // tm,),
        in_specs=[_tok(tm, d), _whole(g.shape)] + [_whole(w.shape) for w in ws],
        out_specs=[_tok(tm, w.shape[1]) for w in ws],
        out_shape=[jax.ShapeDtypeStruct((t, w.shape[1]), dt) for w, dt in zip(ws, out_dtypes)],
        compiler_params=_cparams("parallel"),
    )(h, g, *ws)


def matmul_residual_norm(xs, ws, h, g, name, relu2=False):
    t, d = h.shape
    tm = TOKEN_TILE
    nx = len(xs)

    def body(*refs):
        x_refs, w_refs = refs[:nx], refs[nx:2 * nx]
        h_ref, g_ref, ho_ref, m_ref = refs[2 * nx:]
        m = None
        for x_ref, w_ref in zip(x_refs, w_refs):
            x = x_ref[...]
            if relu2:
                x = jnp.square(jnp.maximum(x.astype(f32), 0.0))
            part = jnp.dot(x.astype(MXU_DTYPE), w_ref[...], preferred_element_type=f32)
            m = part if m is None else m + part
        m_ref[...] = m
        ho_ref[...] = h_ref[...] + _rms(m, g_ref[...])

    return pl.pallas_call(
        body, name=name, grid=(t // tm,),
        in_specs=[_tok(tm, x.shape[1]) for x in xs] + [_whole(w.shape) for w in ws] + [_tok(tm, d), _whole(g.shape)],
        out_specs=[_tok(tm, d), _tok(tm, d)],
        out_shape=[jax.ShapeDtypeStruct((t, d), f32), jax.ShapeDtypeStruct((t, d), f32)],
        compiler_params=_cparams("parallel"),
    )(*xs, *ws, h, g)


def ple_forward(h, p, w_gate, w_proj, g, name):
    t, d = h.shape
    tm = TOKEN_TILE

    def body(h_ref, p_ref, wg_ref, wp_ref, g_ref, ho_ref, gl_ref, emb_ref):
        hh = h_ref[...]
        gl = jnp.dot(hh.astype(MXU_DTYPE), wg_ref[...], preferred_element_type=f32)
        emb = jnp.dot(p_ref[...].astype(MXU_DTYPE), wp_ref[...], preferred_element_type=f32)
        gl_ref[...] = gl
        emb_ref[...] = emb
        ho_ref[...] = hh + _rms(jax.nn.sigmoid(gl) * emb, g_ref[...])

    return pl.pallas_call(
        body, name=name, grid=(t // tm,),
        in_specs=[_tok(tm, d), _tok(tm, p.shape[1]), _whole(w_gate.shape), _whole(w_proj.shape), _whole(g.shape)],
        out_specs=[_tok(tm, d)] * 3,
        out_shape=[jax.ShapeDtypeStruct((t, d), f32)] * 3,
        compiler_params=_cparams("parallel"),
    )(h, p, w_gate, w_proj, g)


def loss_and_grad(h, target, name):
    t, d = h.shape
    tm = TOKEN_TILE

    def body(h_ref, t_ref, l_ref, dh_ref):
        e = h_ref[...] - t_ref[...]
        dh_ref[...] = e * (1.0 / d)
        part = jnp.sum(jnp.sum(e * e, axis=1, keepdims=True), axis=0, keepdims=True) * (0.5 / d)
        _acc_rows(l_ref, jnp.broadcast_to(part, (1, LANES)))

    return pl.pallas_call(
        body, name=name, grid=(t // tm,),
        in_specs=[_tok(tm, d), _tok(tm, d)],
        out_specs=[_whole((1, LANES)), _tok(tm, d)],
        out_shape=[jax.ShapeDtypeStruct((1, LANES), f32), jax.ShapeDtypeStruct((t, d), f32)],
        compiler_params=_cparams("arbitrary"),
    )(h, target)


def bwd_through_norm_in(dh, gs, wts, h, g, name):
    t, d = h.shape
    tm = TOKEN_TILE
    ng = len(gs)

    def body(*refs):
        dh_ref = refs[0]
        g_refs, w_refs = refs[1:1 + ng], refs[1 + ng:1 + 2 * ng]
        h_ref, gain_ref, dho_ref, dg_ref = refs[1 + 2 * ng:]
        dhn = None
        for g_ref, w_ref in zip(g_refs, w_refs):
            part = jnp.dot(g_ref[...].astype(MXU_DTYPE), w_ref[...], preferred_element_type=f32)
            dhn = part if dhn is None else dhn + part
        dx, dgr = _rms_bwd(dhn, h_ref[...], gain_ref[...])
        dho_ref[...] = dh_ref[...] + dx
        _acc_rows(dg_ref, dgr)

    return pl.pallas_call(
        body, name=name, grid=(t // tm,),
        in_specs=[_tok(tm, d)] + [_tok(tm, x.shape[1]) for x in gs] + [_whole(w.shape) for w in wts]
        + [_tok(tm, d), _whole(g.shape)],
        out_specs=[_tok(tm, d), _whole((1, d))],
        out_shape=[jax.ShapeDtypeStruct((t, d), f32), jax.ShapeDtypeStruct((1, d), f32)],
        compiler_params=_cparams("arbitrary"),
    )(dh, *gs, *wts, h, g)


def bwd_through_norm_out(dh, n, g, wts, out_dtypes, name, relu2_of=None, rider=None):
    t, d = n.shape
    tm = TOKEN_TILE
    nw = len(wts)
    has_a = relu2_of is not None

    def body(*refs):
        dh_ref, n_ref, gain_ref = refs[:3]
        w_refs = refs[3:3 + nw]
        rest = refs[3 + nw:]
        if has_a:
            a_ref, rest = rest[0], rest[1:]
        dn_ref, dx_refs, dg_ref = rest[0], rest[1:1 + nw], rest[1 + nw]
        dn, dgr = _rms_bwd(dh_ref[...], n_ref[...], gain_ref[...])
        dnb = dn.astype(MXU_DTYPE)
        dn_ref[...] = dnb.astype(dn_ref.dtype)
        for w_ref, dx_ref in zip(w_refs, dx_refs):
            dx = jnp.dot(dnb, w_ref[...], preferred_element_type=f32)
            if has_a:
                dx = dx * (2.0 * jnp.maximum(a_ref[...].astype(f32), 0.0))
            dx_ref[...] = dx.astype(dx_ref.dtype)
        _acc_rows(dg_ref, dgr)

    ins = [dh, n, g, *wts] + ([relu2_of] if has_a else [])
    in_specs = [_tok(tm, d), _tok(tm, d), _whole(g.shape)] + [_whole(w.shape) for w in wts]
    if has_a:
        in_specs.append(_tok(tm, relu2_of.shape[1]))
    grid = (t // tm,)
    body, in_specs, out_specs, out_shape, scratch = _ride(
        rider, body, in_specs, [_tok(tm, d)] + [_tok(tm, w.shape[1]) for w in wts] + [_whole((1, d))],
        [jax.ShapeDtypeStruct((t, d), MXU_DTYPE)]
        + [jax.ShapeDtypeStruct((t, w.shape[1]), dt) for w, dt in zip(wts, out_dtypes)]
        + [jax.ShapeDtypeStruct((1, d), f32)], [], grid)
    outs = pl.pallas_call(
        body, name=name, grid=grid, in_specs=in_specs, out_specs=out_specs, out_shape=out_shape,
        scratch_shapes=scratch, compiler_params=_cparams("arbitrary"),
    )(*ins, *(rider.inputs if rider else []))
    if rider:
        return outs[0], list(outs[1:1 + nw]), outs[1 + nw], list(outs[2 + nw:])
    return outs[0], list(outs[1:1 + nw]), outs[1 + nw]


def ple_backward(dh3, h2, gl, emb, g, w_gate_t, name):
    t, d = h2.shape
    tm = TOKEN_TILE

    def body(dh_ref, gl_ref, emb_ref, gain_ref, wt_ref, dho_ref, dgl_ref, demb_ref, dg_ref):
        gate = jax.nn.sigmoid(gl_ref[...])
        emb = emb_ref[...]
        dge, dgr = _rms_bwd(dh_ref[...], gate * emb, gain_ref[...])
        demb_ref[...] = (dge * gate).astype(demb_ref.dtype)
        dgl = (dge * emb * gate * (1.0 - gate)).astype(MXU_DTYPE)
        dgl_ref[...] = dgl.astype(dgl_ref.dtype)
        dho_ref[...] = dh_ref[...] + jnp.dot(dgl, wt_ref[...], preferred_element_type=f32)
        _acc_rows(dg_ref, dgr)

    return pl.pallas_call(
        body, name=name, grid=(t // tm,),
        in_specs=[_tok(tm, d), _tok(tm, d), _tok(tm, d), _whole(g.shape), _whole(w_gate_t.shape)],
        out_specs=[_tok(tm, d), _tok(tm, d), _tok(tm, d), _whole((1, d))],
        out_shape=[jax.ShapeDtypeStruct((t, d), f32), jax.ShapeDtypeStruct((t, d), MXU_DTYPE),
                   jax.ShapeDtypeStruct((t, d), MXU_DTYPE), jax.ShapeDtypeStruct((1, d), f32)],
        compiler_params=_cparams("arbitrary"),
    )(dh3, gl, emb, g, w_gate_t)


def _largest_tile(n, cap):
    if n <= cap:
        return n
    return max(c for c in range(LANES, cap + 1, LANES) if n % c == 0)


def weight_grad(x, gout, name, prologue="none", gain=None):
    t, k = x.shape
    n = gout.shape[1]
    tt = 1024
    tn = _largest_tile(n, 1024)
    tk = k if prologue == "rms" else _largest_tile(k, 1024)
    has_gain = prologue == "rms"

    def body(*refs):
        if has_gain:
            x_ref, gain_ref, g_ref, o_ref = refs
        else:
            x_ref, g_ref, o_ref = refs
        x = x_ref[...].astype(f32)
        if prologue == "relu2":
            x = jnp.square(jnp.maximum(x, 0.0))
        elif prologue == "rms":
            x = _rms(x, gain_ref[...])
        part = _mm_tn(x, g_ref[...])

        @pl.when(pl.program_id(2) == 0)
        def _():
            o_ref[...] = part

        @pl.when(pl.program_id(2) != 0)
        def _():
            o_ref[...] += part

    in_specs = [pl.BlockSpec((tt, tk), lambda i, j, s: (s, i))]
    ins = [x]
    if has_gain:
        in_specs.append(pl.BlockSpec(gain.shape, lambda i, j, s: (0, 0)))
        ins.append(gain)
    in_specs.append(pl.BlockSpec((tt, tn), lambda i, j, s: (s, j)))
    ins.append(gout)
    return pl.pallas_call(
        body, name=name, grid=(k // tk, n // tn, t // tt),
        in_specs=in_specs,
        out_specs=pl.BlockSpec((tk, tn), lambda i, j, s: (i, j)),
        out_shape=jax.ShapeDtypeStruct((k, n), f32),
        compiler_params=_cparams("parallel", "parallel", "arbitrary"),
    )(*ins)


SEQ_TILE = 256
HALO = 8


def _first_step():
    return jnp.logical_and(pl.program_id(0) == 0, pl.program_id(1) == 0)


def _accum(ref, val, first):
    @pl.when(first)
    def _():
        ref[...] = val

    @pl.when(jnp.logical_not(first))
    def _():
        ref[...] += val


def _softplus(x):
    return jnp.maximum(x, 0.0) + jnp.log1p(jnp.exp(-jnp.abs(x)))


def _neg_expm1(z):
    series = -z * (1.0 + z * (0.5 + z * (1.0 / 6.0 + z * (1.0 / 24.0 + z * (1.0 / 120.0)))))
    return jnp.where(z > -0.05, series, 1.0 - jnp.exp(z))


def _lru_gates(xc, ga, gab, gx, gxb, lam):
    r = jax.nn.sigmoid(dmm(xc, ga) + gab)
    i = jax.nn.sigmoid(dmm(xc, gx) + gxb)
    log_a = -LRU_C * r * _softplus(-lam)
    a = jnp.exp(log_a)
    u = jnp.sqrt(_neg_expm1(2.0 * log_a)) * (i * xc)
    return a, u


def _scan_down(a, u):
    n = a.shape[0]
    rows = lax.broadcasted_iota(jnp.int32, a.shape, 0)
    d = 1
    while d < n:
        keep = rows >= d
        a_s = jnp.where(keep, pltpu.roll(a, d, 0), 1.0)
        u_s = jnp.where(keep, pltpu.roll(u, d, 0), 0.0)
        u = a * u_s + u
        a = a * a_s
        d *= 2
    return a, u


def _scan_up(b, g):
    n = b.shape[0]
    rows = lax.broadcasted_iota(jnp.int32, b.shape, 0)
    d = 1
    while d < n:
        keep = rows < n - d
        b_s = jnp.where(keep, pltpu.roll(b, n - d, 0), 1.0)
        g_s = jnp.where(keep, pltpu.roll(g, n - d, 0), 0.0)
        g = g + b * g_s
        b = b * b_s
        d *= 2
    return g


def _seq_specs(ts, c, nt, reverse=False):
    per = ts // HALO

    def jj(j):
        return (nt - 1 - j) if reverse else j

    tile = pl.BlockSpec((1, ts, c), lambda b, j: (b, jj(j), 0))
    before = pl.BlockSpec((1, HALO, c), lambda b, j: (b, jnp.maximum(jj(j) * per - 1, 0), 0))
    after = pl.BlockSpec((1, HALO, c), lambda b, j: (b, jnp.minimum((jj(j) + 1) * per, nt * per - 1), 0))
    return tile, before, after


def _const2(shape):
    nd = len(shape)
    return pl.BlockSpec(tuple(shape), lambda b, j: (0,) * nd)


def lru_forward(xpre, gate, cw, cb, ga, gab, gx, gxb, lam, name):
    nb, ns, w = xpre.shape
    ts = SEQ_TILE
    nt = ns // ts
    tile, _, _ = _seq_specs(ts, w, nt)

    def body(xp_ref, gt_ref, cw_ref, cb_ref, ga_ref, gab_ref, gx_ref, gxb_ref, lam_ref,
             y_ref, xc_ref, hs_ref, xin, hcar):
        @pl.when(pl.program_id(1) == 0)
        def _():
            xin[0:HALO, :] = jnp.zeros((HALO, w), f32)
            hcar[...] = jnp.zeros_like(hcar)

        xin[HALO:HALO + ts, :] = xp_ref[0]
        xc = jnp.broadcast_to(cb_ref[...], (ts, w))
        for k in range(LRU_CONV):
            xc = xc + cw_ref[k:k + 1, :] * xin[pl.ds(HALO - LRU_CONV + 1 + k, ts), :]
        xin[0:HALO, :] = xin[ts:ts + HALO, :]
        a, u = _lru_gates(xc, ga_ref[...], gab_ref[...], gx_ref[...], gxb_ref[...], lam_ref[...])
        acum, h = _scan_down(a, u)
        h = h + acum * hcar[0:1, :]
        hcar[0:1, :] = h[ts - 1:ts, :]
        xc_ref[0] = xc
        hs_ref[0] = h
        y_ref[0] = (h * jax.nn.gelu(gt_ref[0])).astype(y_ref.dtype)

    params = [cw, cb, ga, gab, gx, gxb, lam]
    return pl.pallas_call(
        body, name=name, grid=(nb, nt),
        in_specs=[tile, tile] + [_const2(p.shape) for p in params],
        out_specs=[tile, tile, tile],
        out_shape=[jax.ShapeDtypeStruct((nb, ns, w), MXU_DTYPE), jax.ShapeDtypeStruct((nb, ns, w), f32),
                   jax.ShapeDtypeStruct((nb, ns, w), f32)],
        scratch_shapes=[pltpu.VMEM((ts + HALO, w), f32), pltpu.VMEM((HALO, w), f32)],
        compiler_params=_cparams("arbitrary", "arbitrary"),
    )(xpre, gate, *params)


def lru_backward(dy, xpre, gate, xc, hs, cw, cb, ga, gab, gx, gxb, lam, name, rider=None):
    nb, ns, w = xpre.shape
    ts = SEQ_TILE
    nt = ns // ts
    tile, before, _ = _seq_specs(ts, w, nt, reverse=True)

    def body(dy_ref, xp_ref, xpb_ref, gt_ref, xc_ref, hs_ref, hsb_ref,
             cw_ref, cb_ref, ga_ref, gab_ref, gx_ref, gxb_ref, lam_ref,
             dxp_ref, dgt_ref, dcw_ref, dcb_ref, dga_ref, dgab_ref, dgx_ref, dgxb_ref, dlam_ref,
             dxc_ext, gcar, xin):
        j = pl.program_id(1)
        first = _first_step()
        at_seq_start = j == nt - 1

        @pl.when(j == 0)
        def _():
            dxc_ext[ts:ts + HALO, :] = jnp.zeros((HALO, w), f32)
            gcar[...] = jnp.zeros_like(gcar)

        gt = gt_ref[0]
        h = hs_ref[0]
        dyv = dy_ref[0].astype(f32)
        gl, gelu_vjp = jax.vjp(jax.nn.gelu, gt)
        dgt_ref[0] = gelu_vjp(dyv * h)[0].astype(dgt_ref.dtype)
        dh = dyv * gl

        (a, _), gates_vjp = jax.vjp(_lru_gates, xc_ref[0], ga_ref[...], gab_ref[...], gx_ref[...], gxb_ref[...],
                                    lam_ref[...])
        rows = lax.broadcasted_iota(jnp.int32, (ts, w), 0)
        dh = dh + jnp.where(rows == ts - 1, gcar[0:1, :], 0.0)
        b = pltpu.roll(a, ts - 1, 0)
        g = _scan_up(b, dh)
        gcar[0:1, :] = a[0:1, :] * g[0:1, :]
        hprev_row = jnp.where(at_seq_start, 0.0, hsb_ref[0][HALO - 1:HALO, :])
        hprev = jnp.where(rows == 0, hprev_row, pltpu.roll(h, 1, 0))
        dxc, dga, dgab, dgx, dgxb, dlam = gates_vjp((g * hprev, g))

        _accum(dga_ref, dga, first)
        _accum(dgx_ref, dgx, first)
        _accum(dgab_ref, dgab, first)
        _accum(dgxb_ref, dgxb, first)
        _accum(dlam_ref, dlam, first)
        _accum(dcb_ref, jnp.sum(dxc, axis=0, keepdims=True), first)

        dxc_ext[0:ts, :] = dxc
        dxp = jnp.zeros((ts, w), f32)
        for k in range(LRU_CONV):
            dxp = dxp + cw_ref[k:k + 1, :] * dxc_ext[pl.ds(LRU_CONV - 1 - k, ts), :]
        dxp_ref[0] = dxp.astype(dxp_ref.dtype)
        dxc_ext[ts:ts + HALO, :] = dxc[0:HALO, :]

        xin[0:HALO, :] = jnp.where(at_seq_start, 0.0, xpb_ref[0])
        xin[HALO:HALO + ts, :] = xp_ref[0]
        dcw_rows = [jnp.sum(dxc * xin[pl.ds(HALO - LRU_CONV + 1 + k, ts), :], axis=0, keepdims=True)
                    for k in range(LRU_CONV)]
        dcw_rows += [jnp.zeros((1, w), f32)] * (HALO - LRU_CONV)
        _accum(dcw_ref, jnp.concatenate(dcw_rows, axis=0), first)

    params = [cw, cb, ga, gab, gx, gxb, lam]
    pshape = lambda p: jax.ShapeDtypeStruct(p.shape, f32)
    grid = (nb, nt)
    n_main = 3 + len(params) - 1
    body, in_specs, out_specs, out_shape, scratch = _ride(
        rider, body, [tile, tile, before, tile, tile, tile, before] + [_const2(p.shape) for p in params],
        [tile, tile, _const2((HALO, w))] + [_const2(p.shape) for p in params[1:]],
        [jax.ShapeDtypeStruct((nb, ns, w), MXU_DTYPE), jax.ShapeDtypeStruct((nb, ns, w), MXU_DTYPE),
         jax.ShapeDtypeStruct((HALO, w), f32)] + [pshape(p) for p in params[1:]],
        [pltpu.VMEM((ts + HALO, w), f32), pltpu.VMEM((HALO, w), f32), pltpu.VMEM((ts + HALO, w), f32)], grid)
    outs = pl.pallas_call(
        body, name=name, grid=grid, in_specs=in_specs, out_specs=out_specs, out_shape=out_shape,
        scratch_shapes=scratch, compiler_params=_cparams("arbitrary", "arbitrary"),
    )(dy, xpre, xpre, gate, xc, hs, hs, *params, *(rider.inputs if rider else []))
    return (list(outs[:n_main]), list(outs[n_main:])) if rider else outs


SB_TILE = 256


def _split_dot(x, m):
    hi = x.astype(MXU_DTYPE)
    lo = (x - hi.astype(f32)).astype(MXU_DTYPE)
    return jnp.dot(hi, m, preferred_element_type=f32) + jnp.dot(lo, m, preferred_element_type=f32)


def _suffix_matrices(n):
    r = lax.broadcasted_iota(jnp.int32, (n, n), 0)
    c = lax.broadcasted_iota(jnp.int32, (n, n), 1)
    return (r > c).astype(MXU_DTYPE), (r >= c).astype(MXU_DTYPE)


def _sb_logits(qh, kb, strict):
    z = _mm_nt(qh, kb)
    ls = jnp.minimum(z, 0.0) - jnp.log(1.0 + jnp.exp(-jnp.abs(z)))
    lk = ls - z
    if strict is not None:
        lk = jnp.where(strict, lk, 0.0)
    return ls, lk


def _head_masked(x, dtype):
    lane = lax.broadcasted_iota(jnp.int32, x.shape, 1)
    return (jnp.where(lane < SB_HEAD_DIM, x, 0.0).astype(dtype), jnp.where(lane >= SB_HEAD_DIM, x, 0.0).astype(dtype))


def _stack_heads(dst, x, tq):
    x0, x1 = _head_masked(x, dst.dtype)
    for blk in range(dst.shape[0]):
        dst[blk, 0:tq, :] = x0[blk * tq:(blk + 1) * tq]
        dst[blk, tq:2 * tq, :] = x1[blk * tq:(blk + 1) * tq]


def _strict_mask(tq):
    rr = lax.broadcasted_iota(jnp.int32, (2 * tq, tq), 0)
    cc = lax.broadcasted_iota(jnp.int32, (2 * tq, tq), 1)
    return cc < jnp.where(rr >= tq, rr - tq, rr)


def _sb_specs(ns):
    npair = SB_WIDTH // LANES
    q = pl.BlockSpec((1, ns, LANES), lambda b, p: (b, 0, p))
    k = pl.BlockSpec((1, ns, LANES), lambda b, p: (b, 0, npair + p))
    v = pl.BlockSpec((1, ns, LANES), lambda b, p: (b, 0, 2 * npair + p))
    return q, k, v, npair


def sb_forward(qkv, name, rider=None):
    nb, ns, _ = qkv.shape
    tq = SB_TILE
    nq = ns // tq
    qspec, kspec, vspec, npair = _sb_specs(ns)

    def body(q_ref, k_ref, v_ref, o_ref, qs, ks, vs, acc):
        scale = 1.0 / math.sqrt(SB_HEAD_DIM)
        _stack_heads(qs, q_ref[0] * scale, tq)
        ks[...] = k_ref[0].astype(MXU_DTYPE)
        _stack_heads(vs, v_ref[0], tq)
        mx, _ = _suffix_matrices(tq)
        strict = _strict_mask(tq)

        def step(q2, blks, r2, masked):
            kbs = [ks[pl.ds(pl.multiple_of(b * tq, tq), tq), :] for b in blks]
            lg = [_sb_logits(q2, kb, strict if masked else None) for kb in kbs]
            sums = [jnp.dot(lk.astype(MXU_DTYPE), mx, preferred_element_type=f32) for _, lk in lg]
            total = None
            for (ls, lk), s, b in zip(lg, sums, blks):
                a = r2 + s
                w = jnp.exp(ls + a)
                if masked:
                    w = jnp.where(strict, w, 0.0)
                wb = w.astype(MXU_DTYPE)
                part = (jnp.dot(wb[:tq], vs[b, 0:tq, :], preferred_element_type=f32)
                        + jnp.dot(wb[tq:], vs[b, tq:2 * tq, :], preferred_element_type=f32))
                total = part if total is None else total + part
                r2 = a[:, 0:1] + lk[:, 0:1]
            acc[...] += total
            return r2

        def q_block(qi, carry):
            acc[...] = jnp.zeros_like(acc)
            q2 = qs[qi]
            r2 = step(q2, [qi], jnp.zeros((2 * tq, 1), f32), True)
            r2 = lax.fori_loop(0, lax.shift_right_logical(qi, 1),
                               lambda i, r: step(q2, [qi - 1 - 2 * i, qi - 2 - 2 * i], r, False), r2)
            lax.cond(jnp.bitwise_and(qi, 1) == 1, lambda r: step(q2, [0], r, False), lambda r: r, r2)
            o_ref[0, pl.ds(pl.multiple_of(qi * tq, tq), tq), :] = acc[...]
            return carry

        lax.fori_loop(0, nq, q_block, 0)

    grid = (nb, npair)
    body, in_specs, out_specs, out_shape, scratch = _ride(
        rider, body, [qspec, kspec, vspec], [pl.BlockSpec((1, ns, LANES), lambda b, p: (b, 0, p))],
        [jax.ShapeDtypeStruct((nb, ns, SB_WIDTH), f32)],
        [pltpu.VMEM((nq, 2 * tq, LANES), MXU_DTYPE), pltpu.VMEM((ns, LANES), MXU_DTYPE),
         pltpu.VMEM((nq, 2 * tq, LANES), MXU_DTYPE), pltpu.VMEM((tq, LANES), f32)], grid)
    outs = pl.pallas_call(
        body, name=name, grid=grid, in_specs=in_specs, out_specs=out_specs, out_shape=out_shape,
        scratch_shapes=scratch,
        compiler_params=_cparams("arbitrary", "arbitrary") if rider else _cparams("parallel", "parallel"),
    )(qkv, qkv, qkv, *(rider.inputs if rider else []))
    return (outs[0], list(outs[1:])) if rider else outs[0]


def sb_backward(qkv, o, do, name, rider=None):
    nb, ns, _ = qkv.shape
    tq = SB_TILE
    nq = ns // tq
    qspec, kspec, vspec, npair = _sb_specs(ns)
    ospec = pl.BlockSpec((1, ns, LANES), lambda b, p: (b, 0, p))

    def body(q_ref, k_ref, v_ref, o_ref, do_ref, dq_ref, dk_ref, dv_ref, qs, ks, kcat, vs, dos, dqacc, dkacc, dvacc):
        scale = 1.0 / math.sqrt(SB_HEAD_DIM)
        _stack_heads(qs, q_ref[0] * scale, tq)
        ks[...] = k_ref[0].astype(MXU_DTYPE)
        _stack_heads(kcat, k_ref[0], tq)
        vs[...] = v_ref[0].astype(MXU_DTYPE)
        _stack_heads(dos, do_ref[0].astype(f32), tq)
        dkacc[...] = jnp.zeros_like(dkacc)
        dvacc[...] = jnp.zeros_like(dvacc)
        mx, mi = _suffix_matrices(tq)
        strict = _strict_mask(tq)

        def step(q2, do2, dtot2, blks, carry, masked):
            r2, g2 = carry
            k0s = [pl.multiple_of(b * tq, tq) for b in blks]
            lg = [_sb_logits(q2, ks[pl.ds(k0, tq), :], strict if masked else None) for k0 in k0s]
            dws = [_mm_nt(do2, vs[pl.ds(k0, tq), :]) for k0 in k0s]
            sums = [jnp.dot(lk.astype(MXU_DTYPE), mx, preferred_element_type=f32) for _, lk in lg]
            wbs, es = [], []
            for (ls, lk), s in zip(lg, sums):
                a = r2 + s
                w = jnp.exp(ls + a)
                if masked:
                    w = jnp.where(strict, w, 0.0)
                wbs.append(w.astype(MXU_DTYPE))
                r2 = a[:, 0:1] + lk[:, 0:1]
            es = [wb.astype(f32) * dw for wb, dw in zip(wbs, dws)]
            esums = [_split_dot(e, mi) for e in es]
            dq = None
            for (ls, _), e, esum, wb, b, k0 in zip(lg, es, esums, wbs, blks, k0s):
                esuf = g2 + esum
                beta = jnp.exp(ls)
                dz = e * (1.0 - beta) - beta * (dtot2 - esuf)
                if masked:
                    dz = jnp.where(strict, dz, 0.0)
                dzb = dz.astype(MXU_DTYPE)
                part = (jnp.dot(dzb[:tq], kcat[b, 0:tq, :], preferred_element_type=f32)
                        + jnp.dot(dzb[tq:], kcat[b, tq:2 * tq, :], preferred_element_type=f32))
                dq = part if dq is None else dq + part
                dkacc[pl.ds(k0, tq), :] += _mm_tn(dzb, q2)
                dvacc[pl.ds(k0, tq), :] += _mm_tn(wb, do2)
                g2 = esuf[:, 0:1]
            dqacc[...] += dq
            return r2, g2

        def q_block(qi, carry):
            dqacc[...] = jnp.zeros_like(dqacc)
            q2, do2 = qs[qi], dos[qi]
            ov = o_ref[0, pl.ds(pl.multiple_of(qi * tq, tq), tq), :]
            dtot2 = jnp.sum(do2.astype(f32) * jnp.concatenate([ov, ov], axis=0), axis=1, keepdims=True)
            zero = jnp.zeros((2 * tq, 1), f32)
            c = step(q2, do2, dtot2, [qi], (zero, zero), True)
            c = lax.fori_loop(0, lax.shift_right_logical(qi, 1),
                              lambda i, c: step(q2, do2, dtot2, [qi - 1 - 2 * i, qi - 2 - 2 * i], c, False), c)
            lax.cond(jnp.bitwise_and(qi, 1) == 1, lambda c: step(q2, do2, dtot2, [0], c, False), lambda c: c, c)
            dq_ref[0, pl.ds(pl.multiple_of(qi * tq, tq), tq), :] = (dqacc[...] * scale).astype(dq_ref.dtype)
            return carry

        lax.fori_loop(0, nq, q_block, 0)
        dk_ref[0] = dkacc[...].astype(dk_ref.dtype)
        dv_ref[0] = dvacc[...].astype(dv_ref.dtype)

    dshape = jax.ShapeDtypeStruct((nb, ns, SB_WIDTH), MXU_DTYPE)
    stacked = pltpu.VMEM((nq, 2 * tq, LANES), MXU_DTYPE)
    flat = pltpu.VMEM((ns, LANES), MXU_DTYPE)
    grid = (nb, npair)
    body, in_specs, out_specs, out_shape, scratch = _ride(
        rider, body, [qspec, kspec, vspec, ospec, ospec], [ospec, ospec, ospec], [dshape, dshape, dshape],
        [stacked, flat, stacked, flat, stacked,
         pltpu.VMEM((tq, LANES), f32), pltpu.VMEM((ns, LANES), f32), pltpu.VMEM((ns, LANES), f32)], grid)
    outs = pl.pallas_call(
        body, name=name, grid=grid, in_specs=in_specs, out_specs=out_specs, out_shape=out_shape,
        scratch_shapes=scratch,
        compiler_params=_cparams("arbitrary", "arbitrary") if rider else _cparams("parallel", "parallel"),
    )(qkv, qkv, qkv, o, do, *(rider.inputs if rider else []))
    return (list(outs[:3]), list(outs[3:])) if rider else list(outs)


SSM_PAIRS = SSM_HEADS // 2
PAIRS_PER_GROUP = SSM_PAIRS // SSM_GROUPS
GROUP_WIDTH = SSM_WIDTH // SSM_GROUPS


def _silu(x):
    return x * jax.nn.sigmoid(x)


def _ssd_chunk(xs_pre, b_pre, c_pre, dt_raw, dt_raw_t, z, st, dt_bias_r, dt_bias_c, a_log_r, a_log_c, d_skip,
               gains):
    n = dt_raw.shape[0]
    rows = lax.broadcasted_iota(jnp.int32, (n, n), 0)
    cols = lax.broadcasted_iota(jnp.int32, (n, n), 1)
    tril = cols <= rows
    tri_l = tril.astype(f32)
    tri_u = (rows <= cols).astype(f32)
    lane = lax.broadcasted_iota(jnp.int32, (n, LANES), 1)
    sub = lax.broadcasted_iota(jnp.int32, (LANES, n), 0)

    dt = _softplus(dt_raw + dt_bias_r)
    a_r = -jnp.exp(a_log_r)
    da = dt * a_r
    acs = _mm_exact(tri_l, da)
    dt_t = _softplus(dt_raw_t + dt_bias_c)
    acs_t = _mm_exact(dt_t * (-jnp.exp(a_log_c)), tri_u)

    bs = [_silu(b) for b in b_pre]
    cs = [_silu(c) for c in c_pre]
    cb = [dmm_nt(cs[g], bs[g]) for g in range(SSM_GROUPS)]

    end = jnp.sum(da, axis=0, keepdims=True)
    lane_row = lax.broadcasted_iota(jnp.int32, (1, LANES), 1)
    first_head = lane < SSM_HEAD_DIM
    first_head_row = lane_row < SSM_HEAD_DIM

    def head_col(v, h):
        return jnp.sum(jnp.where((lane if v.shape[0] == n else lane_row) == h, v, 0.0), axis=1, keepdims=True)

    ys, st_new = [], []
    for p in range(SSM_PAIRS):
        g = p // PAIRS_PER_GROUP
        h0, h1 = 2 * p, 2 * p + 1
        xs = _silu(xs_pre[p])
        acols = [head_col(acs, h0), head_col(acs, h1)]
        dt_p = jnp.where(first_head, head_col(dt, h0), head_col(dt, h1))
        acs_p = jnp.where(first_head, acols[0], acols[1])
        end_p = jnp.where(first_head_row, head_col(end, h0), head_col(end, h1))
        dsk_p = jnp.where(first_head_row, head_col(d_skip, h0), head_col(d_skip, h1))
        xdt = xs * dt_p
        y = jnp.exp(acs_p) * dmm(cs[g], st[p])
        for hh in range(2):
            row = jnp.sum(jnp.where(sub == 2 * p + hh, acs_t, 0.0), axis=0, keepdims=True)
            decay = jnp.where(tril, jnp.exp(jnp.where(tril, acols[hh] - row, 0.0)), 0.0)
            head = first_head if hh == 0 else jnp.logical_not(first_head)
            y = y + dmm(cb[g] * decay, jnp.where(head, xdt, 0.0))
        st_new.append(jnp.exp(end_p) * st[p] + dmm_tn(bs[g], xdt * jnp.exp(end_p - acs_p)))
        ys.append(y + dsk_p * xs)
    out = []
    for g in range(SSM_GROUPS):
        yg = jnp.concatenate(ys[g * PAIRS_PER_GROUP:(g + 1) * PAIRS_PER_GROUP], axis=1) * _silu(z[g])
        out.append(_rms(yg, gains[g]))
    return out, st_new


def _ssd_chunk_inputs(xconv, dtr, z, st_ref, gain):
    xs_pre = [xconv[:, LANES * p:LANES * (p + 1)] for p in range(SSM_PAIRS)]
    b0 = SSM_WIDTH
    c0 = SSM_WIDTH + SSM_GROUPS * SSM_STATE
    b_pre = [xconv[:, b0 + SSM_STATE * g:b0 + SSM_STATE * (g + 1)] for g in range(SSM_GROUPS)]
    c_pre = [xconv[:, c0 + SSM_STATE * g:c0 + SSM_STATE * (g + 1)] for g in range(SSM_GROUPS)]
    zs = [z[:, GROUP_WIDTH * g:GROUP_WIDTH * (g + 1)] for g in range(SSM_GROUPS)]
    sts = [st_ref[p] for p in range(SSM_PAIRS)]
    gains = [gain[:, GROUP_WIDTH * g:GROUP_WIDTH * (g + 1)] for g in range(SSM_GROUPS)]
    return xs_pre, b_pre, c_pre, dtr, dtr.T, zs, sts, gains


def ssd_forward(xbc, dt_raw, z, cw, cb, dbr, dbc, alr, alc, dsk, gain, name):
    nb, ns, wx = xbc.shape
    ln = SSM_CHUNK
    nt = ns // ln
    tile = lambda c: pl.BlockSpec((1, ln, c), lambda b, j: (b, j, 0))
    st_spec = pl.BlockSpec((1, 1, SSM_PAIRS, SSM_STATE, LANES), lambda b, j: (b, j, 0, 0, 0))

    def body(xbc_ref, dt_ref, z_ref, cw_ref, cb_ref, dbr_ref, dbc_ref, alr_ref, alc_ref, dsk_ref, gain_ref,
             y_ref, xconv_ref, stp_ref, xin, st):
        @pl.when(pl.program_id(1) == 0)
        def _():
            xin[0:HALO, :] = jnp.zeros((HALO, wx), f32)
            st[...] = jnp.zeros_like(st)

        xin[HALO:HALO + ln, :] = xbc_ref[0]
        xconv = jnp.broadcast_to(cb_ref[...], (ln, wx))
        for k in range(SSM_CONV):
            xconv = xconv + cw_ref[k:k + 1, :] * xin[pl.ds(HALO - SSM_CONV + 1 + k, ln), :]
        xin[0:HALO, :] = xin[ln:ln + HALO, :]
        xconv_ref[0] = xconv
        stp_ref[0, 0] = st[...]
        xs_pre, b_pre, c_pre, dtr, dtr_t, zs, sts, gains = _ssd_chunk_inputs(xconv, dt_ref[0], z_ref[0], st,
                                                                             gain_ref[...])
        out, st_new = _ssd_chunk(xs_pre, b_pre, c_pre, dtr, dtr_t, zs, sts, dbr_ref[...], dbc_ref[...],
                                 alr_ref[...], alc_ref[...], dsk_ref[...], gains)
        y_ref[0] = jnp.concatenate(out, axis=1).astype(y_ref.dtype)
        for p in range(SSM_PAIRS):
            st[p] = st_new[p]

    params = [cw, cb, dbr, dbc, alr, alc, dsk, gain]
    return pl.pallas_call(
        body, name=name, grid=(nb, nt),
        in_specs=[tile(wx), tile(LANES), tile(SSM_WIDTH)] + [_const2(p.shape) for p in params],
        out_specs=[tile(SSM_WIDTH), tile(wx), st_spec],
        out_shape=[jax.ShapeDtypeStruct((nb, ns, SSM_WIDTH), MXU_DTYPE), jax.ShapeDtypeStruct((nb, ns, wx), f32),
                   jax.ShapeDtypeStruct((nb, nt, SSM_PAIRS, SSM_STATE, LANES), f32)],
        scratch_shapes=[pltpu.VMEM((ln + HALO, wx), f32), pltpu.VMEM((SSM_PAIRS, SSM_STATE, LANES), f32)],
        compiler_params=_cparams("arbitrary", "arbitrary"),
    )(xbc, dt_raw, z, *params)


def ssd_backward(dy, xbc, xconv, dt_raw, z, stp, cw, cb, dbr, dbc, alr, alc, dsk, gain, name):
    nb, ns, wx = xbc.shape
    ln = SSM_CHUNK
    nt = ns // ln
    per = ln // HALO
    rj = lambda j: nt - 1 - j
    tile = lambda c: pl.BlockSpec((1, ln, c), lambda b, j: (b, rj(j), 0))
    before = pl.BlockSpec((1, HALO, wx), lambda b, j: (b, jnp.maximum(rj(j) * per - 1, 0), 0))
    st_spec = pl.BlockSpec((1, 1, SSM_PAIRS, SSM_STATE, LANES), lambda b, j: (b, rj(j), 0, 0, 0))

    def body(dy_ref, xbc_ref, xbcb_ref, xconv_ref, dt_ref, z_ref, stp_ref,
             cw_ref, cb_ref, dbr_ref, dbc_ref, alr_ref, alc_ref, dsk_ref, gain_ref,
             dxbc_ref, ddt_ref, dz_ref, dcw_ref, dcb_ref, ddbr_ref, ddbc_ref, dalr_ref, dalc_ref, ddsk_ref, dgain_ref,
             dxc_ext, dst, xin):
        j = pl.program_id(1)
        first = _first_step()
        at_seq_start = j == nt - 1

        @pl.when(j == 0)
        def _():
            dxc_ext[ln:ln + HALO, :] = jnp.zeros((HALO, wx), f32)
            dst[...] = jnp.zeros_like(dst)

        xs_pre, b_pre, c_pre, dtr, dtr_t, zs, sts, gains = _ssd_chunk_inputs(xconv_ref[0], dt_ref[0], z_ref[0],
                                                                             stp_ref.at[0, 0], gain_ref[...])
        _, vjp = jax.vjp(_ssd_chunk, xs_pre, b_pre, c_pre, dtr, dtr_t, zs, sts, dbr_ref[...], dbc_ref[...],
                         alr_ref[...], alc_ref[...], dsk_ref[...], gains)
        dyv = dy_ref[0].astype(f32)
        cot = ([dyv[:, GROUP_WIDTH * g:GROUP_WIDTH * (g + 1)] for g in range(SSM_GROUPS)],
               [dst[p] for p in range(SSM_PAIRS)])
        dxs, db, dc, ddt, ddt_t, dzs, dsts, ddbr, ddbc, dalr, dalc, ddsk, dgains = vjp(cot)
        for p in range(SSM_PAIRS):
            dst[p] = dsts[p]
        ddt_ref[0] = (ddt + ddt_t.T).astype(ddt_ref.dtype)
        dz_ref[0] = jnp.concatenate(dzs, axis=1).astype(dz_ref.dtype)
        _accum(ddbr_ref, ddbr, first)
        _accum(ddbc_ref, ddbc, first)
        _accum(dalr_ref, dalr, first)
        _accum(dalc_ref, dalc, first)
        _accum(ddsk_ref, ddsk, first)
        _accum(dgain_ref, jnp.concatenate(dgains, axis=1), first)

        dxc = jnp.concatenate(dxs + db + dc, axis=1)
        _accum(dcb_ref, jnp.sum(dxc, axis=0, keepdims=True), first)
        dxc_ext[0:ln, :] = dxc
        dxp = jnp.zeros((ln, wx), f32)
        for k in range(SSM_CONV):
            dxp = dxp + cw_ref[k:k + 1, :] * dxc_ext[pl.ds(SSM_CONV - 1 - k, ln), :]
        dxbc_ref[0] = dxp.astype(dxbc_ref.dtype)
        dxc_ext[ln:ln + HALO, :] = dxc[0:HALO, :]

        xin[0:HALO, :] = jnp.where(at_seq_start, 0.0, xbcb_ref[0])
        xin[HALO:HALO + ln, :] = xbc_ref[0]
        dcw_rows = [jnp.sum(dxc * xin[pl.ds(HALO - SSM_CONV + 1 + k, ln), :], axis=0, keepdims=True)
                    for k in range(SSM_CONV)]
        dcw_rows += [jnp.zeros((1, wx), f32)] * (HALO - SSM_CONV)
        _accum(dcw_ref, jnp.concatenate(dcw_rows, axis=0), first)

    params = [cw, cb, dbr, dbc, alr, alc, dsk, gain]
    pshape = lambda p: jax.ShapeDtypeStruct(p.shape, f32)
    return pl.pallas_call(
        body, name=name, grid=(nb, nt),
        in_specs=[tile(SSM_WIDTH), tile(wx), before, tile(wx), tile(LANES), tile(SSM_WIDTH), st_spec]
        + [_const2(p.shape) for p in params],
        out_specs=[tile(wx), tile(LANES), tile(SSM_WIDTH), _const2((HALO, wx))] + [_const2(p.shape) for p in params[1:]],
        out_shape=[jax.ShapeDtypeStruct((nb, ns, wx), MXU_DTYPE), jax.ShapeDtypeStruct((nb, ns, LANES), MXU_DTYPE),
                   jax.ShapeDtypeStruct((nb, ns, SSM_WIDTH), MXU_DTYPE), jax.ShapeDtypeStruct((HALO, wx), f32)]
        + [pshape(p) for p in params[1:]],
        scratch_shapes=[pltpu.VMEM((ln + HALO, wx), f32), pltpu.VMEM((SSM_PAIRS, SSM_STATE, LANES), f32),
                        pltpu.VMEM((ln + HALO, wx), f32)],
        compiler_params=_cparams("arbitrary", "arbitrary"),
    )(dy, xbc, xbc, xconv, dt_raw, z, stp, *params)


CONF_HALO = 32
CONF_OFF = CONF_HALO - CONF_KERNEL + 1


def _conf_specs(ts, c, nt):
    per = ts // CONF_HALO
    tile = pl.BlockSpec((1, ts, c), lambda b, j: (b, j, 0))
    before = pl.BlockSpec((1, CONF_HALO, c), lambda b, j: (b, jnp.maximum(j * per - 1, 0), 0))
    after = pl.BlockSpec((1, CONF_HALO, c), lambda b, j: (b, jnp.minimum((j + 1) * per, nt * per - 1), 0))
    return tile, before, after


SUBLANES = 8


def _shifted_copies(dst, x):
    rows = x.shape[0]
    dst[0] = x
    for b in range(1, SUBLANES):
        dst[b] = pltpu.roll(x, rows - b, 0)


def _window(copies, off, size):
    b = off % SUBLANES
    return copies[b, pl.ds(off - b, size), :]


def _glu(x):
    return x[:, :CONF_WIDTH] * jax.nn.sigmoid(x[:, CONF_WIDTH:])


def _layernorm_parts(c):
    xc = c - jnp.mean(c, axis=-1, keepdims=True)
    r = lax.rsqrt(jnp.mean(xc * xc, axis=-1, keepdims=True) + EPS)
    return xc * r, r


def conf_forward(glu, cw, cb, ln_g, ln_b, name):
    nb, ns, wg = glu.shape
    w = CONF_WIDTH
    ts = SEQ_TILE
    nt = ns // ts
    tile, before, _ = _conf_specs(ts, wg, nt)

    def body(x_ref, xb_ref, cw_ref, cb_ref, g_ref, b_ref, y_ref, u_rot):
        _shifted_copies(u_rot, jnp.concatenate(
            [jnp.where(pl.program_id(1) == 0, 0.0, _glu(xb_ref[0])), _glu(x_ref[0])], axis=0))
        conv = jnp.broadcast_to(cb_ref[...], (ts, w))
        for k in range(CONF_KERNEL):
            conv = conv + cw_ref[k:k + 1, :] * _window(u_rot, CONF_OFF + k, ts)
        xhat, _ = _layernorm_parts(conv)
        y_ref[0] = _silu(xhat * g_ref[...] + b_ref[...]).astype(y_ref.dtype)

    params = [cw, cb, ln_g, ln_b]
    return pl.pallas_call(
        body, name=name, grid=(nb, nt),
        in_specs=[tile, before] + [_const2(p.shape) for p in params],
        out_specs=pl.BlockSpec((1, ts, w), lambda b, j: (b, j, 0)),
        out_shape=jax.ShapeDtypeStruct((nb, ns, w), MXU_DTYPE),
        scratch_shapes=[pltpu.VMEM((SUBLANES, ts + CONF_HALO, w), f32)],
        compiler_params=_cparams("parallel", "parallel"),
    )(glu, glu, *params)


def conf_backward(dy, glu, cw, cb, ln_g, ln_b, name):
    nb, ns, wg = glu.shape
    w = CONF_WIDTH
    ts = SEQ_TILE
    nt = ns // ts
    te = ts + CONF_HALO
    tile, before, after = _conf_specs(ts, wg, nt)
    dtile, _, dafter = _conf_specs(ts, w, nt)

    def body(dy_ref, dya_ref, x_ref, xb_ref, xa_ref, cw_ref, cb_ref, g_ref, b_ref,
             dx_ref, dcw_ref, dcb_ref, dg_ref, db_ref, u_ext, dc_ext):
        j = pl.program_id(1)
        first = _first_step()
        x = x_ref[0]
        _shifted_copies(u_ext, jnp.concatenate(
            [jnp.where(j == 0, 0.0, _glu(xb_ref[0])), _glu(x), _glu(xa_ref[0])], axis=0))
        conv = jnp.broadcast_to(cb_ref[...], (te, w))
        for k in range(CONF_KERNEL):
            conv = conv + cw_ref[k:k + 1, :] * _window(u_ext, CONF_OFF + k, te)
        xhat, r = _layernorm_parts(conv)
        lnout = xhat * g_ref[...] + b_ref[...]
        sg = jax.nn.sigmoid(lnout)
        rows = lax.broadcasted_iota(jnp.int32, (te, w), 0)
        dyv = jnp.concatenate([dy_ref[0].astype(f32), dya_ref[0].astype(f32)], axis=0)
        dyv = jnp.where(jnp.logical_and(j == nt - 1, rows >= ts), 0.0, dyv)
        dln = dyv * sg * (1.0 + lnout * (1.0 - sg))
        in_tile = rows < ts
        _accum(dg_ref, jnp.sum(jnp.where(in_tile, dln * xhat, 0.0), axis=0, keepdims=True), first)
        _accum(db_ref, jnp.sum(jnp.where(in_tile, dln, 0.0), axis=0, keepdims=True), first)
        dxh = dln * g_ref[...]
        dconv = r * (dxh - jnp.mean(dxh, axis=-1, keepdims=True) - xhat * jnp.mean(dxh * xhat, axis=-1, keepdims=True))
        _shifted_copies(dc_ext, dconv)
        dct = dconv[0:ts, :]
        _accum(dcb_ref, jnp.sum(dct, axis=0, keepdims=True), first)
        du = jnp.zeros((ts, w), f32)
        dcw_rows = []
        for k in range(CONF_KERNEL):
            du = du + cw_ref[k:k + 1, :] * _window(dc_ext, CONF_KERNEL - 1 - k, ts)
            dcw_rows.append(jnp.sum(dct * _window(u_ext, CONF_OFF + k, ts), axis=0, keepdims=True))
        dcw_rows.append(jnp.zeros((1, w), f32))
        _accum(dcw_ref, jnp.concatenate(dcw_rows, axis=0), first)
        sb = jax.nn.sigmoid(x[:, w:])
        dx_ref[0] = jnp.concatenate([du * sb, du * x[:, :w] * sb * (1.0 - sb)], axis=1).astype(dx_ref.dtype)

    params = [cw, cb, ln_g, ln_b]
    return pl.pallas_call(
        body, name=name, grid=(nb, nt),
        in_specs=[dtile, dafter, tile, before, after] + [_const2(p.shape) for p in params],
        out_specs=[tile] + [_const2(p.shape) for p in params],
        out_shape=[jax.ShapeDtypeStruct((nb, ns, wg), MXU_DTYPE)] + [jax.ShapeDtypeStruct(p.shape, f32) for p in params],
        scratch_shapes=[pltpu.VMEM((SUBLANES, te + CONF_HALO, w), f32), pltpu.VMEM((SUBLANES, te, w), f32)],
        compiler_params=_cparams("arbitrary", "arbitrary"),
    )(dy, dy, glu, glu, glu, *params)


def _row(v):
    return v.reshape(1, -1).astype(f32)


def _pad_to(v, n, axis):
    pads = [(0, 0)] * v.ndim
    pads[axis] = (0, n - v.shape[axis])
    return jnp.pad(v, pads)


def _block_diag(w):
    nh, d, _ = w.shape
    eye = jnp.eye(nh, dtype=w.dtype)
    return (eye[:, None, :, None] * w[:, :, None, :]).reshape(nh * d, nh * d)


def _diag_blocks(m, nh):
    d = m.shape[0] // nh
    idx = jnp.arange(nh)
    return m.reshape(nh, d, nh, d)[idx, :, idx, :]


def _mix_even_fwd(h, gpre, w, wl, nb, ns, rider=None):
    t = nb * ns
    w_in = wl["w_in"]
    w_lx, w_lg, w_qkv = w_in[:, :LRU_WIDTH], w_in[:, LRU_WIDTH:2 * LRU_WIDTH], w_in[:, 2 * LRU_WIDTH:]
    xpre, gate, qkv = norm_matmul(h, gpre, [w_lx, w_lg, w_qkv], [f32, f32, f32], name="ev_in_proj")
    lru_p = [w["ev_lru_conv_w"][0], _row(w["ev_lru_conv_b"][0]),
             _block_diag(w["ev_lru_gate_a_w"][0]).astype(MXU_DTYPE), _row(w["ev_lru_gate_a_b"][0]),
             _block_diag(w["ev_lru_gate_x_w"][0]).astype(MXU_DTYPE), _row(w["ev_lru_gate_x_b"][0]),
             _row(w["ev_lru_lambda"][0])]
    xpre3, gate3, qkv3 = xpre.reshape(nb, ns, -1), gate.reshape(nb, ns, -1), qkv.reshape(nb, ns, -1)
    y_a, xc, hs = lru_forward(xpre3, gate3, *lru_p, name="ev_lru_fwd")
    o = sb_forward(qkv3, name="ev_sb_fwd", rider=rider)
    carried = None
    if rider is not None:
        o, carried = o
    ys = [y_a.reshape(t, -1), o.reshape(t, -1)]
    saved = dict(xpre=xpre3, gate=gate3, qkv=qkv3, xc=xc, hs=hs, o=o, lru_p=lru_p)
    return ys, saved, carried


def _mix_even_bwd(dys, saved, wtl, nb, ns, lru_rider=None, attention_rider=None):
    t = nb * ns
    dy_a, dy_b = [d.reshape(nb, ns, -1) for d in dys]
    outs = lru_backward(dy_a, saved["xpre"], saved["gate"], saved["xc"], saved["hs"], *saved["lru_p"],
                        name="ev_lru_bwd", rider=lru_rider)
    lru_carried = None
    if lru_rider is not None:
        outs, lru_carried = outs
    dxp, dgt, dcw, dcb, dga, dgab, dgx, dgxb, dlam = outs
    rider = attention_rider(lru_carried) if attention_rider is not None else None
    carried = None
    if rider is None:
        dq, dk, dv = sb_backward(saved["qkv"], saved["o"], dy_b, name="ev_sb_bwd")
    else:
        (dq, dk, dv), carried = sb_backward(saved["qkv"], saved["o"], dy_b, name="ev_sb_bwd", rider=rider)
    w_in_t = wtl["w_in"]
    pieces = [dxp, dgt, dq, dk, dv]
    gs = [d.reshape(t, -1) for d in pieces]
    wts = [w_in_t[LRU_WIDTH * i:LRU_WIDTH * (i + 1)] for i in range(5)]
    grads = {
        "ev_lru_conv_w": dcw[:LRU_CONV][None], "ev_lru_conv_b": dcb,
        "ev_lru_gate_a_w": _diag_blocks(dga, LRU_HEADS)[None], "ev_lru_gate_a_b": dgab,
        "ev_lru_gate_x_w": _diag_blocks(dgx, LRU_HEADS)[None], "ev_lru_gate_x_b": dgxb,
        "ev_lru_lambda": dlam,
    }
    return gs, wts, grads, carried


def _odd_params(w):
    ssd_p = [w["od_ssm_conv_w"][0], _row(w["od_ssm_conv_b"][0]),
             _pad_to(_row(w["od_ssm_dt_bias"][0]), LANES, 1), _pad_to(_row(w["od_ssm_dt_bias"][0]), LANES, 1).T,
             _pad_to(_row(w["od_ssm_a_log"][0]), LANES, 1), _pad_to(_row(w["od_ssm_a_log"][0]), LANES, 1).T,
             _pad_to(_row(w["od_ssm_d"][0]), LANES, 1), _row(w["od_ssm_norm"][0])]
    conf_p = [_pad_to(w["od_cm_conv_w"][0], CONF_HALO, 0), _row(w["od_cm_conv_b"][0]),
              _row(w["od_cm_ln_g"][0]), _row(w["od_cm_ln_b"][0])]
    return ssd_p, conf_p


ODD_SPLITS = (SSM_WIDTH, SSM_WIDTH + SSM_XBC, SSM_WIDTH + SSM_XBC + SSM_HEADS)


def _mix_odd_fwd(h, gpre, w, wl, nb, ns, rider=None):
    assert rider is None
    t = nb * ns
    w_in = wl["w_in"]
    s0, s1, s2 = ODD_SPLITS
    w_z, w_xbc, w_dt, w_glu = w_in[:, :s0], w_in[:, s0:s1], _pad_to(w_in[:, s1:s2], LANES, 1), w_in[:, s2:]
    zz, xbc, dtr, glu = norm_matmul(h, gpre, [w_z, w_xbc, w_dt, w_glu], [f32] * 4, name="od_in_proj")
    ssd_p, conf_p = _odd_params(w)
    zz3, xbc3, dtr3, glu3 = [a.reshape(nb, ns, -1) for a in (zz, xbc, dtr, glu)]
    y_c, xconv, stp = ssd_forward(xbc3, dtr3, zz3, *ssd_p, name="od_ssd_fwd")
    y_d = conf_forward(glu3, *conf_p, name="od_conf_fwd")
    ys = [y_c.reshape(t, -1), y_d.reshape(t, -1)]
    saved = dict(z=zz3, xbc=xbc3, dtr=dtr3, glu=glu3, xconv=xconv, stp=stp, ssd_p=ssd_p, conf_p=conf_p)
    return ys, saved, None


def _mix_odd_bwd(dys, saved, wtl, nb, ns, lru_rider=None, attention_rider=None):
    assert lru_rider is None and attention_rider is None
    t = nb * ns
    dy_c, dy_d = [d.reshape(nb, ns, -1) for d in dys]
    outs = ssd_backward(dy_c, saved["xbc"], saved["xconv"], saved["dtr"], saved["z"], saved["stp"], *saved["ssd_p"],
                        name="od_ssd_bwd")
    dxbc, ddt, dz, dcw, dcb, ddbr, ddbc, dalr, dalc, ddsk, dgain = outs
    dglu, ccw, ccb, clg, clb = conf_backward(dy_d, saved["glu"], *saved["conf_p"], name="od_conf_bwd")
    w_in_t = wtl["w_in"]
    s0, s1, s2 = ODD_SPLITS
    carried = None
    gs = [d.reshape(t, -1) for d in (dz, dxbc, ddt, dglu)]
    wts = [w_in_t[:s0], w_in_t[s0:s1], _pad_to(w_in_t[s1:s2], LANES, 0), w_in_t[s2:]]
    nh = SSM_HEADS
    grads = {
        "od_ssm_conv_w": dcw[:SSM_CONV][None], "od_ssm_conv_b": dcb,
        "od_ssm_dt_bias": ddbr[:, :nh] + ddbc[:nh, 0][None], "od_ssm_a_log": dalr[:, :nh] + dalc[:nh, 0][None],
        "od_ssm_d": ddsk[:, :nh], "od_ssm_norm": dgain,
        "od_cm_conv_w": ccw[:CONF_KERNEL][None], "od_cm_conv_b": ccb, "od_cm_ln_g": clg, "od_cm_ln_b": clb,
    }
    return gs, wts, grads, carried


LAYER_MATRICES = ("w_in", "w_out", "mlp_w1", "mlp_w2", "ple_w_proj", "ple_w_gate")
NORM_NAMES = ("norm_mix_pre", "norm_mix_post", "norm_mlp_pre", "norm_mlp_post", "norm_ple")


class NoOverlap:
    sums, from_chips = {}, {}

    def attention_fwd_rider(self):
        return None

    def weights_arrived(self, carried, wl, wtl):
        raise NotImplementedError

    def mlp_bwd_rider(self, layer1_grads):
        return None

    def after_mlp_bwd(self, carried):
        pass

    def lru_bwd_rider(self, layer0_grads):
        return None

    def attention_bwd_rider(self, carried):
        return None

    def after_attention_bwd(self, carried):
        pass


OUT_SPLIT = (LRU_WIDTH, SSM_WIDTH)


def local_step(x, p, target, w, wl, wtl, comm=NoOverlap()):
    nb, ns, d = x.shape
    t = nb * ns
    h = x.reshape(t, d)
    depth = p.shape[0]
    wl, wtl = list(wl), list(wtl)
    tapes = []
    for i in range(depth):
        even = i % 2 == 0
        tag = f"l{i}_"
        gpre = _row(w["norm_mix_pre"][i])
        rider = comm.attention_fwd_rider() if i == 0 else None
        ys, saved, carried = (_mix_even_fwd if even else _mix_odd_fwd)(h, gpre, w, wl[i], nb, ns, rider)
        if rider is not None:
            wl, wtl = comm.weights_arrived(carried, wl, wtl)
        w_out = wl[i]["w_out"]
        w_outs = [w_out[:OUT_SPLIT[i % 2]], w_out[OUT_SPLIT[i % 2]:]]
        h1, m = matmul_residual_norm(ys, w_outs, h, _row(w["norm_mix_post"][i]), name=tag + "out_proj")
        a1, = norm_matmul(h1, _row(w["norm_mlp_pre"][i]), [wl[i]["mlp_w1"]], [f32], name=tag + "mlp_up")
        h2, f = matmul_residual_norm([a1], [wl[i]["mlp_w2"]], h1, _row(w["norm_mlp_post"][i]), name=tag + "mlp_down",
                                     relu2=True)
        pi = p[i].reshape(t, -1)
        h3, gl, emb = ple_forward(h2, pi, wl[i]["ple_w_gate"], wl[i]["ple_w_proj"], _row(w["norm_ple"][i]),
                                  name=tag + "ple")
        tapes.append(dict(h=h, ys=ys, w_outs=w_outs, saved=saved, h1=h1, m=m, a1=a1, h2=h2, f=f, pi=pi, gl=gl,
                          emb=emb))
        h = h3

    loss_row, dh = loss_and_grad(h, target.reshape(t, d), name="loss")
    grads = {}
    norm_grads = {k: [None] * depth for k in NORM_NAMES}
    layer_grads = [None] * depth
    for i in reversed(range(depth)):
        even = i % 2 == 0
        tag = f"l{i}_"
        tp = tapes[i]
        lg = {}
        to_sibling = comm.mlp_bwd_rider(layer_grads[1]) if i == 0 else None
        dh2, dgl, demb, dg = ple_backward(dh, tp["h2"], tp["gl"], tp["emb"], _row(w["norm_ple"][i]),
                                          wtl[i]["ple_w_gate"], name=tag + "ple_bwd")
        norm_grads["norm_ple"][i] = dg
        lg["ple_w_gate"] = weight_grad(tp["h2"], dgl, name=tag + "dw_gate")
        lg["ple_w_proj"] = weight_grad(tp["pi"], demb, name=tag + "dw_proj")
        outs = bwd_through_norm_out(dh2, tp["f"], _row(w["norm_mlp_post"][i]), [wtl[i]["mlp_w2"]], [MXU_DTYPE],
                                    name=tag + "mlp_down_bwd", relu2_of=tp["a1"], rider=to_sibling)
        d_f, (da1,), dg = outs[:3]
        if to_sibling is not None:
            comm.after_mlp_bwd(outs[3])
        norm_grads["norm_mlp_post"][i] = dg
        lg["mlp_w2"] = weight_grad(tp["a1"], d_f, name=tag + "dw2", prologue="relu2")
        gpre = _row(w["norm_mlp_pre"][i])
        dh1, dg = bwd_through_norm_in(dh2, [da1], [wtl[i]["mlp_w1"]], tp["h1"], gpre, name=tag + "mlp_up_bwd")
        norm_grads["norm_mlp_pre"][i] = dg
        lg["mlp_w1"] = weight_grad(tp["h1"], da1, name=tag + "dw1", prologue="rms", gain=gpre)
        wt_out = wtl[i]["w_out"]
        split = tp["w_outs"][0].shape[0]
        dm, dys, dg = bwd_through_norm_out(dh1, tp["m"], _row(w["norm_mix_post"][i]),
                                           [wt_out[:, :split], wt_out[:, split:]], [f32, f32],
                                           name=tag + "out_proj_bwd")
        norm_grads["norm_mix_post"][i] = dg
        lg["w_out"] = jnp.concatenate([weight_grad(y, dm, name=tag + f"dw_out{k}") for k, y in enumerate(tp["ys"])],
                                      axis=0)
        lru_rider = comm.lru_bwd_rider(lg) if i == 0 else None
        gs, wts, mix_grads, carried = (_mix_even_bwd if even else _mix_odd_bwd)(
            dys, tp["saved"], wtl[i], nb, ns, lru_rider, comm.attention_bwd_rider if lru_rider is not None else None)
        if carried is not None:
            comm.after_attention_bwd(carried)
        grads.update(mix_grads)
        gpre = _row(w["norm_mix_pre"][i])
        dh, dg = bwd_through_norm_in(dh1, gs, wts, tp["h"], gpre, name=tag + "in_proj_bwd")
        norm_grads["norm_mix_pre"][i] = dg
        dw_in = [weight_grad(tp["h"], g, name=tag + f"dw_in{k}", prologue="rms", gain=gpre) for k, g in enumerate(gs)]
        if not even:
            dw_in[2] = dw_in[2][:, :SSM_HEADS]
        lg["w_in"] = jnp.concatenate(dw_in, axis=1)
        layer_grads[i] = lg
    for k, v in norm_grads.items():
        grads[k] = jnp.concatenate(v, axis=0)
    return loss_row[0, 0], dh.reshape(nb, ns, d), grads, layer_grads


MESH_ID = pl.DeviceIdType.MESH
ANY = pl.BlockSpec(memory_space=pl.ANY)


def _mesh_pos():
    return lax.axis_index("x"), lax.axis_index("y"), lax.axis_index("c")


def all_gather(shards, name):
    return _run_alone(gather_rider(shards), name)


class Rider:
    def __init__(self, inputs, out_shapes, scratch_shapes, start, finish, middle=None):
        self.inputs, self.out_shapes, self.scratch_shapes = list(inputs), list(out_shapes), list(scratch_shapes)
        self.start, self.finish, self.middle = start, finish, middle


def _run_alone(rider, name):
    ni, no = len(rider.inputs), len(rider.out_shapes)

    def body(*refs):
        args = (refs[:ni], refs[ni:ni + no], refs[ni + no:])
        rider.start(*args)
        if rider.middle is not None:
            rider.middle(*args)
        rider.finish(*args)

    return pl.pallas_call(
        body, name=name, out_shape=rider.out_shapes, in_specs=[ANY] * ni, out_specs=[ANY] * no,
        scratch_shapes=rider.scratch_shapes,
    )(*rider.inputs)


def _ride(rider, body, in_specs, out_specs, out_shape, scratch_shapes, grid):
    in_specs, out_specs, out_shape = list(in_specs), list(out_specs), list(out_shape)
    scratch_shapes = list(scratch_shapes)
    if rider is None:
        return body, in_specs, out_specs, out_shape, scratch_shapes
    n_in, n_out, n_scr = len(in_specs), len(out_specs), len(scratch_shapes)
    ri, ro = len(rider.inputs), len(rider.out_shapes)
    total = math.prod(grid)

    def carrying(*refs):
        ins, r_ins = refs[:n_in], refs[n_in:n_in + ri]
        o0 = n_in + ri
        outs, r_outs = refs[o0:o0 + n_out], refs[o0 + n_out:o0 + n_out + ro]
        s0 = o0 + n_out + ro
        scr, r_scr = refs[s0:s0 + n_scr], refs[s0 + n_scr:]
        step = pl.program_id(0)
        for ax in range(1, len(grid)):
            step = step * grid[ax] + pl.program_id(ax)
        args = (r_ins, r_outs, r_scr)
        pl.when(step == 0)(lambda: rider.start(*args))
        if rider.middle is not None:
            pl.when(step == total // 2)(lambda: rider.middle(*args))
        body(*ins, *outs, *scr)
        pl.when(step == total - 1)(lambda: rider.finish(*args))

    return (carrying, in_specs + [ANY] * ri, out_specs + [ANY] * ro, out_shape + rider.out_shapes,
            scratch_shapes + rider.scratch_shapes)


def gather_rider(shards):
    n = len(shards)

    def parts(x_refs, out_refs, scr):
        send_sems, recv_sems, local_sems = scr
        x, y, c = _mesh_pos()
        chips = [(1 - x, y), (x, 1 - y), (1 - x, 1 - y)]

        def slot(a, px, py, pc):
            return out_refs[a].at[4 * px + 2 * py + pc]

        def copy(a, k, block, to, src=None):
            return pltpu.make_async_remote_copy(
                src_ref=slot(a, *block) if src is None else src, dst_ref=slot(a, *block),
                send_sem=send_sems.at[7 * a + k], recv_sem=recv_sems.at[7 * a + k], device_id=to,
                device_id_type=MESH_ID)

        me, sibling = (x, y, c), (x, y, 1 - c)
        def mine():
            return [pltpu.make_async_copy(x_refs[a], slot(a, *me), local_sems.at[a]) for a in range(n)]

        def first():
            out = []
            for j, chip in enumerate(chips):
                out += [copy(a, 1 + j, me, (*chip, c), src=x_refs[a]) for a in range(n)]
            return out + [copy(a, 0, me, sibling, src=x_refs[a]) for a in range(n)]

        def passed(j):
            return [copy(a, 4 + j, (*chips[j], c), sibling) for a in range(n)]

        return me, sibling, chips, c, copy, mine, first, passed

    def start(x_refs, out_refs, scr):
        _, _, _, _, _, mine, first, _ = parts(x_refs, out_refs, scr)
        for cp in mine() + first():
            cp.start()

    def middle(x_refs, out_refs, scr):
        me, _, chips, c, copy, _, _, passed = parts(x_refs, out_refs, scr)
        for j, chip in enumerate(chips):
            for a, fwd in enumerate(passed(j)):
                copy(a, 1 + j, (*chip, c), me).wait_recv()
                fwd.start()

    def finish(x_refs, out_refs, scr):
        me, sibling, chips, c, copy, mine, first, passed = parts(x_refs, out_refs, scr)
        for a in range(n):
            copy(a, 0, sibling, me).wait_recv()
        for j, chip in enumerate(chips):
            for a in range(n):
                copy(a, 4 + j, (*chip, 1 - c), me).wait_recv()
        for cp in first() + [cp for j in range(len(chips)) for cp in passed(j)]:
            cp.wait_send()
        for cp in mine():
            cp.wait()

    return Rider(shards, [jax.ShapeDtypeStruct((N_DEV,) + s.shape, s.dtype) for s in shards],
                 [pltpu.SemaphoreType.DMA((7 * n,)), pltpu.SemaphoreType.DMA((7 * n,)), pltpu.SemaphoreType.DMA((n,))],
                 start, finish, middle)


def scatter_to_sibling(parts, name):
    return _run_alone(sibling_rider(parts), name)


def sibling_rider(parts):
    n = len(parts)

    def copies(g_refs, out_refs, scr):
        send_sems, recv_sems = scr
        x, y, c = _mesh_pos()
        return [pltpu.make_async_remote_copy(
            src_ref=g_refs[a].at[2 * chip + (1 - c)], dst_ref=out_refs[a].at[chip],
            send_sem=send_sems.at[4 * a + chip], recv_sem=recv_sems.at[4 * a + chip], device_id=(x, y, 1 - c),
            device_id_type=MESH_ID) for a in range(n) for chip in range(4)]

    def start(*refs):
        for cp in copies(*refs):
            cp.start()

    def finish(*refs):
        cps = copies(*refs)
        for cp in cps:
            cp.wait_recv()
        for cp in cps:
            cp.wait_send()

    return Rider(parts, [jax.ShapeDtypeStruct((4,) + p.shape[1:], p.dtype) for p in parts],
                 [pltpu.SemaphoreType.DMA((4 * n,)), pltpu.SemaphoreType.DMA((4 * n,))], start, finish)


def scatter_to_chips(partials, name):
    return _run_alone(chips_rider(partials), name)


def chips_rider(partials):
    n = len(partials)

    def copies(p_refs, out_refs, scr):
        send_sems, recv_sems = scr
        x, y, c = _mesh_pos()
        chips = [(1 - x, y), (x, 1 - y), (1 - x, 1 - y)]
        return [pltpu.make_async_remote_copy(
            src_ref=p_refs[a].at[2 * px + py], dst_ref=out_refs[a].at[j],
            send_sem=send_sems.at[3 * a + j], recv_sem=recv_sems.at[3 * a + j], device_id=(px, py, c),
            device_id_type=MESH_ID) for a in range(n) for j, (px, py) in enumerate(chips)]

    def start(*refs):
        for cp in copies(*refs):
            cp.start()

    def finish(*refs):
        cps = copies(*refs)
        for cp in cps:
            cp.wait_recv()
        for cp in cps:
            cp.wait_send()

    return Rider(partials, [jax.ShapeDtypeStruct((3,) + p.shape[1:], p.dtype) for p in partials],
                 [pltpu.SemaphoreType.DMA((3 * n,)), pltpu.SemaphoreType.DMA((3 * n,))], start, finish)


ICI_DTYPE = jnp.bfloat16
ELEMENTWISE_BLOCK_BYTES = 1 << 20


def _row_tile(rows, cols):
    cap = max(16, ELEMENTWISE_BLOCK_BYTES // (4 * cols))
    best = [t for t in range(16, min(rows, cap) + 1, 16) if rows % t == 0]
    return best[-1] if best else rows


def add_sibling_parts(parts, received, core, name):
    _, r, n = parts.shape
    tr = _row_tile(r, n)

    def body(c_ref, a_ref, b_ref, o_ref, ob_ref):
        s = a_ref[...] + b_ref[...]
        o_ref[...] = s
        ob_ref[...] = s.astype(ob_ref.dtype)

    blk = pl.BlockSpec((1, tr, n), lambda i, j, c_ref: (i, j, 0))
    return pl.pallas_call(
        body, name=name,
        grid_spec=pltpu.PrefetchScalarGridSpec(
            num_scalar_prefetch=1, grid=(4, r // tr),
            in_specs=[pl.BlockSpec((1, tr, n), lambda i, j, c_ref: (2 * i + c_ref[0], j, 0)), blk],
            out_specs=[blk, blk]),
        out_shape=[jax.ShapeDtypeStruct((4, r, n), f32), jax.ShapeDtypeStruct((4, r, n), ICI_DTYPE)],
        compiler_params=_cparams("parallel", "parallel"),
    )(core, parts, received)


def _adamw(w, g, m, v):
    m = ADAM_B1 * m + (1.0 - ADAM_B1) * g
    v = ADAM_B2 * v + (1.0 - ADAM_B2) * jnp.square(g)
    m_hat = m / (1.0 - ADAM_B1 ** ADAM_STEP)
    v_hat = v / (1.0 - ADAM_B2 ** ADAM_STEP)
    delta = -ADAM_LR * (m_hat / (jnp.sqrt(v_hat) + ADAM_EPS) + ADAM_WD * w)
    return delta, m, v


def adamw_sharded(partial, received, chip, w, m, v, name):
    _, r, n = partial.shape

    def body(k_ref, p_ref, r_ref, w_ref, m_ref, v_ref, g_out, d_out, m_out, v_out):
        g = p_ref[0] + r_ref[0].astype(f32)
        g = g + r_ref[1].astype(f32)
        g = g + r_ref[2].astype(f32)
        delta, mn, vn = _adamw(w_ref[...], g, m_ref[...], v_ref[...])
        g_out[...] = g
        d_out[...] = delta
        m_out[...] = mn
        v_out[...] = vn

    tr = _row_tile(r, n)
    flat = pl.BlockSpec((tr, n), lambda j, k_ref: (j, 0))
    return pl.pallas_call(
        body, name=name,
        grid_spec=pltpu.PrefetchScalarGridSpec(
            num_scalar_prefetch=1, grid=(r // tr,),
            in_specs=[pl.BlockSpec((1, tr, n), lambda j, k_ref: (k_ref[0], j, 0)),
                      pl.BlockSpec((3, tr, n), lambda j, k_ref: (0, j, 0)), flat, flat, flat],
            out_specs=[flat] * 4),
        out_shape=[jax.ShapeDtypeStruct((r, n), f32)] * 4,
        compiler_params=_cparams("parallel"),
    )(chip, partial, received, w, m, v)


def adamw_replicated(gathered, w, m, v, name):
    _, r, n = gathered.shape

    def body(g_ref, w_ref, m_ref, v_ref, g_out, d_out, m_out, v_out):
        g = g_ref[0]
        for k in range(1, N_DEV):
            g = g + g_ref[k]
        delta, mn, vn = _adamw(w_ref[...], g, m_ref[...], v_ref[...])
        g_out[...] = g
        d_out[...] = delta
        m_out[...] = mn
        v_out[...] = vn

    return pl.pallas_call(
        body, name=name,
        out_shape=[jax.ShapeDtypeStruct((r, n), f32)] * 4,
        compiler_params=pltpu.CompilerParams(vmem_limit_bytes=VMEM_LIMIT),
    )(gathered, w, m, v)


W_NAMES = ['ev_w_in', 'ev_lru_conv_w', 'ev_lru_conv_b', 'ev_lru_gate_a_w', 'ev_lru_gate_a_b', 'ev_lru_gate_x_w',
           'ev_lru_gate_x_b', 'ev_lru_lambda', 'ev_w_out', 'od_w_in', 'od_ssm_conv_w', 'od_ssm_conv_b',
           'od_ssm_dt_bias', 'od_ssm_a_log', 'od_ssm_d', 'od_ssm_norm', 'od_cm_conv_w', 'od_cm_conv_b', 'od_cm_ln_g',
           'od_cm_ln_b', 'od_w_out', 'norm_mix_pre', 'norm_mix_post', 'norm_mlp_pre', 'norm_mlp_post', 'norm_ple',
           'mlp_w1', 'mlp_w2', 'ple_w_proj', 'ple_w_gate']
BIG_SHARDED = {'ev_w_in': 2, 'ev_w_out': 1, 'od_w_in': 2, 'od_w_out': 1, 'mlp_w1': 2, 'mlp_w2': 1, 'ple_w_proj': 2,
               'ple_w_gate': 1}
SMALL_SHARDED = {'ev_lru_conv_w': 2, 'od_ssm_conv_w': 2, 'od_ssm_conv_b': 1, 'od_ssm_norm': 1, 'od_cm_conv_w': 2,
                 'od_cm_conv_b': 1, 'od_cm_ln_g': 1, 'od_cm_ln_b': 1}
SHARDED = {**BIG_SHARDED, **SMALL_SHARDED}
REPLICATED = [n for n in W_NAMES if n not in SHARDED]


def _round_up(n, k):
    return -(-n // k) * k


def _pack_rows(flat, rows_multiple):
    n = flat.shape[0]
    total = _round_up(n, LANES * rows_multiple)
    return jnp.pad(flat, (0, total - n)).reshape(-1, LANES)


def _unpack(flat, shapes):
    out, off = {}, 0
    for name, shape in shapes.items():
        size = math.prod(shape)
        out[name] = flat[off:off + size].reshape(shape)
        off += size
    return out


def _rows(a):
    return a.reshape(-1, a.shape[-1])


def _unshard(g8, shape, axis):
    g = jnp.moveaxis(g8.reshape((N_DEV,) + tuple(shape)), 0, axis)
    return g.reshape(tuple(shape[:axis]) + (N_DEV * shape[axis],) + tuple(shape[axis + 1:]))


def _to_shards(g, axis):
    shard = g.shape[axis] // N_DEV
    g = g.reshape(g.shape[:axis] + (N_DEV, shard) + g.shape[axis + 1:])
    return jnp.moveaxis(g, axis, 0)


def _pack_small(tree):
    return _pack_rows(jnp.concatenate([tree[k].astype(f32).reshape(-1) for k in SMALL_SHARDED]), 16)


def _layer_entry(i, key):
    mixer = "ev" if i % 2 == 0 else "od"
    return {"w_in": (mixer + "_w_in", i // 2, 1), "w_out": (mixer + "_w_out", i // 2, 0),
            "mlp_w1": ("mlp_w1", i, 1), "mlp_w2": ("mlp_w2", i, 0),
            "ple_w_proj": ("ple_w_proj", i, 1), "ple_w_gate": ("ple_w_gate", i, 0)}[key]


def _layer_shards(tree, i):
    out = {}
    for key in LAYER_MATRICES:
        name, idx, _ = _layer_entry(i, key)
        out[key] = tree[name][idx]
    return out


def _assemble_layer(i, gathered):
    wl = {key: _unshard(g8, g8.shape[1:], _layer_entry(i, key)[2]) for key, g8 in zip(LAYER_MATRICES, gathered)}
    return wl, {key: v.T for key, v in wl.items()}


def _gather_early(w):
    small = _pack_small(w)
    terms, rest = [], small
    for _ in range(3):
        term = rest.astype(MXU_DTYPE)
        terms.append(term)
        rest = rest - term.astype(f32)
    outs = all_gather([_layer_shards(w, 0)["w_in"].astype(MXU_DTYPE), jnp.concatenate(terms, axis=0)],
                      name="gather_weights")
    w_in = _unshard(outs[0], outs[0].shape[1:], _layer_entry(0, "w_in")[2])
    wl, wtl = {"w_in": w_in}, {"w_in": w_in.T}
    full = {k: w[k] for k in REPLICATED}
    t = outs[-1].astype(f32)
    nr = small.shape[0]
    vals = (t[:, :nr] + t[:, nr:2 * nr] + t[:, 2 * nr:]).reshape(N_DEV, -1)
    off = 0
    for k, axis in SMALL_SHARDED.items():
        size = math.prod(w[k].shape)
        full[k] = _unshard(vals[:, off:off + size], w[k].shape, axis)
        off += size
    return full, wl, wtl


class Overlap:
    LATE = [(0, key) for key in LAYER_MATRICES if key != "w_in"] + [(1, key) for key in LAYER_MATRICES]
    EARLY_GRADS = [(0, key) for key in LAYER_MATRICES if key != "w_in"]

    def __init__(self, w):
        self.w = w
        self.sums, self.from_chips, self.parts = {}, {}, {}

    def attention_fwd_rider(self):
        shards = [_layer_shards(self.w, 0), _layer_shards(self.w, 1)]
        return gather_rider([shards[i][key].astype(MXU_DTYPE) for i, key in self.LATE])

    def weights_arrived(self, carried, wl, wtl):
        wl = [dict(wl[0]), {}]
        wtl = [dict(wtl[0]), {}]
        for (i, key), g8 in zip(self.LATE, carried):
            wl[i][key] = _unshard(g8, g8.shape[1:], _layer_entry(i, key)[2])
            wtl[i][key] = wl[i][key].T
        return wl, wtl

    def _to_sibling(self, ids, layer_grads):
        for i, key in ids:
            self.parts[i, key] = _to_shards(layer_grads[key], _layer_entry(i, key)[2])
        return sibling_rider([self.parts[e] for e in ids])

    def _add(self, ids, carried):
        core = jnp.reshape(lax.axis_index("c"), (1,)).astype(jnp.int32)
        for (i, key), got in zip(ids, carried):
            self.sums[i, key] = add_sibling_parts(self.parts[i, key], got, core, name=f"add_sibling_l{i}_{key}")

    def mlp_bwd_rider(self, layer1_grads):
        return self._to_sibling([(1, key) for key in LAYER_MATRICES], layer1_grads)

    def after_mlp_bwd(self, carried):
        self._add([(1, key) for key in LAYER_MATRICES], carried)

    def lru_bwd_rider(self, layer0_grads):
        return self._to_sibling(self.EARLY_GRADS, layer0_grads)

    def attention_bwd_rider(self, carried):
        self._add(self.EARLY_GRADS, carried)
        self.travelling = list(self.sums)
        return chips_rider([self.sums[e][1] for e in self.travelling])

    def after_attention_bwd(self, carried):
        for e, got in zip(self.travelling, carried):
            self.from_chips[e] = got


def _pack_replicated(tree):
    return _pack_rows(jnp.concatenate([tree[k].astype(f32).reshape(-1) for k in REPLICATED]), 8)


def kernel(x, p, ev_w_in, ev_lru_conv_w, ev_lru_conv_b, ev_lru_gate_a_w, ev_lru_gate_a_b, ev_lru_gate_x_w, ev_lru_gate_x_b, ev_lru_lambda, ev_w_out, od_w_in, od_ssm_conv_w, od_ssm_conv_b, od_ssm_dt_bias, od_ssm_a_log, od_ssm_d, od_ssm_norm, od_cm_conv_w, od_cm_conv_b, od_cm_ln_g, od_cm_ln_b, od_w_out, norm_mix_pre, norm_mix_post, norm_mlp_pre, norm_mlp_post, norm_ple, mlp_w1, mlp_w2, ple_w_proj, ple_w_gate, loss_target, m_ev_w_in, m_ev_lru_conv_w, m_ev_lru_conv_b, m_ev_lru_gate_a_w, m_ev_lru_gate_a_b, m_ev_lru_gate_x_w, m_ev_lru_gate_x_b, m_ev_lru_lambda, m_ev_w_out, m_od_w_in, m_od_ssm_conv_w, m_od_ssm_conv_b, m_od_ssm_dt_bias, m_od_ssm_a_log, m_od_ssm_d, m_od_ssm_norm, m_od_cm_conv_w, m_od_cm_conv_b, m_od_cm_ln_g, m_od_cm_ln_b, m_od_w_out, m_norm_mix_pre, m_norm_mix_post, m_norm_mlp_pre, m_norm_mlp_post, m_norm_ple, m_mlp_w1, m_mlp_w2, m_ple_w_proj, m_ple_w_gate, v_ev_w_in, v_ev_lru_conv_w, v_ev_lru_conv_b, v_ev_lru_gate_a_w, v_ev_lru_gate_a_b, v_ev_lru_gate_x_w, v_ev_lru_gate_x_b, v_ev_lru_lambda, v_ev_w_out, v_od_w_in, v_od_ssm_conv_w, v_od_ssm_conv_b, v_od_ssm_dt_bias, v_od_ssm_a_log, v_od_ssm_d, v_od_ssm_norm, v_od_cm_conv_w, v_od_cm_conv_b, v_od_cm_ln_g, v_od_cm_ln_b, v_od_w_out, v_norm_mix_pre, v_norm_mix_post, v_norm_mlp_pre, v_norm_mlp_post, v_norm_ple, v_mlp_w1, v_mlp_w2, v_ple_w_proj, v_ple_w_gate):
    ws = [ev_w_in, ev_lru_conv_w, ev_lru_conv_b, ev_lru_gate_a_w, ev_lru_gate_a_b, ev_lru_gate_x_w, ev_lru_gate_x_b, ev_lru_lambda, ev_w_out, od_w_in, od_ssm_conv_w, od_ssm_conv_b, od_ssm_dt_bias, od_ssm_a_log, od_ssm_d, od_ssm_norm, od_cm_conv_w, od_cm_conv_b, od_cm_ln_g, od_cm_ln_b, od_w_out, norm_mix_pre, norm_mix_post, norm_mlp_pre, norm_mlp_post, norm_ple, mlp_w1, mlp_w2, ple_w_proj, ple_w_gate]
    ms = [m_ev_w_in, m_ev_lru_conv_w, m_ev_lru_conv_b, m_ev_lru_gate_a_w, m_ev_lru_gate_a_b, m_ev_lru_gate_x_w, m_ev_lru_gate_x_b, m_ev_lru_lambda, m_ev_w_out, m_od_w_in, m_od_ssm_conv_w, m_od_ssm_conv_b, m_od_ssm_dt_bias, m_od_ssm_a_log, m_od_ssm_d, m_od_ssm_norm, m_od_cm_conv_w, m_od_cm_conv_b, m_od_cm_ln_g, m_od_cm_ln_b, m_od_w_out, m_norm_mix_pre, m_norm_mix_post, m_norm_mlp_pre, m_norm_mlp_post, m_norm_ple, m_mlp_w1, m_mlp_w2, m_ple_w_proj, m_ple_w_gate]
    vs = [v_ev_w_in, v_ev_lru_conv_w, v_ev_lru_conv_b, v_ev_lru_gate_a_w, v_ev_lru_gate_a_b, v_ev_lru_gate_x_w, v_ev_lru_gate_x_b, v_ev_lru_lambda, v_ev_w_out, v_od_w_in, v_od_ssm_conv_w, v_od_ssm_conv_b, v_od_ssm_dt_bias, v_od_ssm_a_log, v_od_ssm_d, v_od_ssm_norm, v_od_cm_conv_w, v_od_cm_conv_b, v_od_cm_ln_g, v_od_cm_ln_b, v_od_w_out, v_norm_mix_pre, v_norm_mix_post, v_norm_mlp_pre, v_norm_mlp_post, v_norm_ple, v_mlp_w1, v_mlp_w2, v_ple_w_proj, v_ple_w_gate]
    w = dict(zip(W_NAMES, ws))
    m = dict(zip(W_NAMES, ms))
    v = dict(zip(W_NAMES, vs))
    full, wl0, wtl0 = _gather_early(w)
    comm = Overlap(w)
    loss_local, grad_x, grads, layer_grads = local_step(x, p, loss_target, full, [wl0, None], [wtl0, None], comm)
    loss = lax.psum(loss_local, ("x", "y", "c"))
    return (loss, grad_x, *_reduce_and_update(grads, layer_grads, w, m, v, comm))


def _reduce_and_update(grads, layer_grads, w, m, v, comm):
    mx, my, mc = _mesh_pos()

    entries = [(i, key) for i in range(len(layer_grads)) for key in LAYER_MATRICES]
    left = [e for e in entries if e not in comm.from_chips]
    parts = [_to_shards(layer_grads[i][key], _layer_entry(i, key)[2]) for i, key in left]
    small = jnp.concatenate([_to_shards(grads[k], axis).reshape(N_DEV, -1) for k, axis in SMALL_SHARDED.items()],
                            axis=1)
    small_rows = _pack_small(w).shape[0]
    small = jnp.pad(small, ((0, 0), (0, small_rows * LANES - small.shape[1]))).reshape(N_DEV, small_rows, LANES)
    parts.append(small)
    from_sibling = scatter_to_sibling(parts, name="scatter_sibling")
    core = jnp.reshape(mc, (1,)).astype(jnp.int32)
    sums = [add_sibling_parts(a, b, core, name=f"add_sibling_{i}") for i, (a, b) in enumerate(zip(parts, from_sibling))]
    from_chips = scatter_to_chips([s[1] for s in sums], name="scatter_chips")
    all_sums = {**comm.sums, **dict(zip(left, sums[:-1]))}
    all_from_chips = {**comm.from_chips, **dict(zip(left, from_chips[:-1]))}
    chip = jnp.reshape(2 * mx + my, (1,)).astype(jnp.int32)
    per_layer = []
    for i in range(len(layer_grads)):
        ws, ms, vs = _layer_shards(w, i), _layer_shards(m, i), _layer_shards(v, i)
        per_layer.append({key: adamw_sharded(all_sums[i, key][0], all_from_chips[i, key], chip, ws[key], ms[key],
                                             vs[key], name=f"adamw_l{i}_{key}") for key in LAYER_MATRICES})
    g_sh, d_sh, m_sh, v_sh = {}, {}, {}, {}
    for which, tree in enumerate((g_sh, d_sh, m_sh, v_sh)):
        for i in range(len(per_layer)):
            for key in LAYER_MATRICES:
                name, idx, _ = _layer_entry(i, key)
                tree.setdefault(name, {})[idx] = per_layer[i][key][which]
        for name in BIG_SHARDED:
            tree[name] = jnp.stack([tree[name][idx] for idx in sorted(tree[name])], axis=0)
    outs = adamw_sharded(sums[-1][0], from_chips[-1], chip, _pack_small(w), _pack_small(m), _pack_small(v),
                         name="adamw_small")
    small_shapes = {k: w[k].shape for k in SMALL_SHARDED}
    for tree, o in zip((g_sh, d_sh, m_sh, v_sh), outs):
        tree.update(_unpack(o.reshape(-1), small_shapes))

    rep_parts, = all_gather([_pack_replicated(grads)], name="gather_replicated_grads")
    outs = adamw_replicated(rep_parts, _pack_replicated(w), _pack_replicated(m), _pack_replicated(v),
                            name="adamw_replicated")
    rep_shapes = {k: w[k].shape for k in REPLICATED}
    g_rp, d_rp, m_rp, v_rp = [_unpack(o.reshape(-1), rep_shapes) for o in outs]

    pick = lambda sh, rp: [sh[k] if k in SHARDED else rp[k] for k in W_NAMES]
    return [*pick(g_sh, g_rp), *pick(d_sh, d_rp), *pick(m_sh, m_rp), *pick(v_sh, v_rp)]
```

```python
import functools
import math

import jax
import jax.numpy as jnp
from jax import lax
from jax.experimental import pallas as pl
from jax.experimental.pallas import tpu as pltpu

f32 = jnp.float32
bf16 = jnp.bfloat16
MXU_DTYPE = jnp.bfloat16

D_MODEL = 1024
EPS = 1e-6
LRU_WIDTH = 512
LRU_HEADS = 8
LRU_CONV = 4
LRU_C = 8.0
SB_WIDTH = 512
SB_HEAD_DIM = 64
SSM_WIDTH = 1024
SSM_HEADS = 16
SSM_HEAD_DIM = 64
SSM_GROUPS = 2
SSM_STATE = 128
SSM_CONV = 4
SSM_CHUNK = 128
SSM_XBC = SSM_WIDTH + 2 * SSM_GROUPS * SSM_STATE
CONF_WIDTH = 512
CONF_KERNEL = 31
MLP_HIDDEN = 4096
PLE_DIM = 256
LANES = 128
N_DEV = 8

ADAM_LR = 0.001
ADAM_B1 = 0.9
ADAM_B2 = 0.999
ADAM_EPS = 1e-08
ADAM_WD = 0.01
ADAM_STEP = 10

VMEM_LIMIT = 56 * 1024 * 1024


def _cparams(*sem):
    return pltpu.CompilerParams(dimension_semantics=sem, vmem_limit_bytes=VMEM_LIMIT)


def _mm(a, b):
    return jnp.dot(a.astype(MXU_DTYPE), b.astype(MXU_DTYPE), preferred_element_type=f32)


def _mm_nt(a, b):
    return lax.dot_general(a.astype(MXU_DTYPE), b.astype(MXU_DTYPE), (((1,), (1,)), ((), ())),
                           preferred_element_type=f32)


def _mm_tn(a, b):
    return lax.dot_general(a.astype(MXU_DTYPE), b.astype(MXU_DTYPE), (((0,), (0,)), ((), ())),
                           preferred_element_type=f32)


def _mm_exact(a, b):
    return jnp.dot(a, b, preferred_element_type=f32, precision=lax.Precision.HIGHEST)


@jax.custom_vjp
def dmm(a, b):
    return _mm(a, b)


def _dmm_fwd(a, b):
    return _mm(a, b), (a, b)


def _dmm_bwd(res, g):
    a, b = res
    return _mm_nt(g, b), _mm_tn(a, g)


dmm.defvjp(_dmm_fwd, _dmm_bwd)


@jax.custom_vjp
def dmm_nt(a, b):
    return _mm_nt(a, b)


def _dmm_nt_fwd(a, b):
    return _mm_nt(a, b), (a, b)


def _dmm_nt_bwd(res, g):
    a, b = res
    return _mm(g, b), _mm_tn(g, a)


dmm_nt.defvjp(_dmm_nt_fwd, _dmm_nt_bwd)


@jax.custom_vjp
def dmm_tn(a, b):
    return _mm_tn(a, b)


def _dmm_tn_fwd(a, b):
    return _mm_tn(a, b), (a, b)


def _dmm_tn_bwd(res, g):
    a, b = res
    return _mm_nt(b, g), _mm(a, g)


dmm_tn.defvjp(_dmm_tn_fwd, _dmm_tn_bwd)


def _rms(x, g):
    r = lax.rsqrt(jnp.mean(x * x, axis=-1, keepdims=True) + EPS)
    return x * r * g


def _rms_bwd(dy, x, g):
    r = lax.rsqrt(jnp.mean(x * x, axis=-1, keepdims=True) + EPS)
    dyg = dy * g
    dx = r * dyg - x * (r * r * r * jnp.mean(dyg * x, axis=-1, keepdims=True))
    return dx, dy * x * r


def _tok(tm, n):
    return pl.BlockSpec((tm, n), lambda i: (i, 0))


def _whole(shape):
    nd = len(shape)
    return pl.BlockSpec(tuple(shape), lambda i: (0,) * nd)


def _acc_rows(ref, val):
    s = jnp.sum(val, axis=0, keepdims=True)

    @pl.when(pl.program_id(0) == 0)
    def _():
        ref[...] = s

    @pl.when(pl.program_id(0) != 0)
    def _():
        ref[...] += s


TOKEN_TILE = 256


class Part:
    def __init__(self, whole, start, size, axis):
        self.whole, self.start, self.size, self.axis = whole, start, size, axis
        self.shape = tuple(size if a == axis else n for a, n in enumerate(whole.shape))


def _weights(ws):
    wholes, readers = [], []
    for w in ws:
        arr = w.whole if isinstance(w, Part) else w
        idx = next((i for i, a in enumerate(wholes) if a is arr), None)
        if idx is None:
            wholes.append(arr)
            idx = len(wholes) - 1
        if isinstance(w, Part):
            rows = pl.ds(w.start, w.size) if w.axis == 0 else slice(None)
            cols = pl.ds(w.start, w.size) if w.axis == 1 else slice(None)
            readers.append(lambda refs, idx=idx, rows=rows, cols=cols: refs[idx][rows, cols])
        else:
            readers.append(lambda refs, idx=idx: refs[idx][...])
    return wholes, readers


def norm_matmul(h, g, ws, out_dtypes, name):
    t, d = h.shape
    tm = TOKEN_TILE
    wholes, readers = _weights(ws)
    nw = len(wholes)

    def body(h_ref, g_ref, *refs):
        hn = _rms(h_ref[...], g_ref[...]).astype(MXU_DTYPE)
        for read, o_ref in zip(readers, refs[nw:]):
            o_ref[...] = jnp.dot(hn, read(refs[:nw]), preferred_element_type=f32).astype(o_ref.dtype)

    return pl.pallas_call(
        body, name=name, grid=(t // tm,),
        in_specs=[_tok(tm, d), _whole(g.shape)] + [_whole(w.shape) for w in wholes],
        out_specs=[_tok(tm, w.shape[1]) for w in ws],
        out_shape=[jax.ShapeDtypeStruct((t, w.shape[1]), dt) for w, dt in zip(ws, out_dtypes)],
        compiler_params=_cparams("parallel"),
    )(h, g, *wholes)


def matmul_residual_norm(xs, ws, h, g, name, relu2=False):
    t, d = h.shape
    tm = TOKEN_TILE
    nx = len(xs)
    wholes, readers = _weights(ws)
    nw = len(wholes)

    def body(*refs):
        x_refs, w_refs = refs[:nx], refs[nx:nx + nw]
        h_ref, g_ref, ho_ref, m_ref = refs[nx + nw:]
        m = None
        for x_ref, read in zip(x_refs, readers):
            x = x_ref[...]
            if relu2:
                x = jnp.square(jnp.maximum(x.astype(f32), 0.0))
            part = jnp.dot(x.astype(MXU_DTYPE), read(w_refs), preferred_element_type=f32)
            m = part if m is None else m + part
        m_ref[...] = m.astype(m_ref.dtype)
        ho_ref[...] = h_ref[...] + _rms(m, g_ref[...])

    return pl.pallas_call(
        body, name=name, grid=(t // tm,),
        in_specs=[_tok(tm, x.shape[1]) for x in xs] + [_whole(w.shape) for w in wholes]
        + [_tok(tm, d), _whole(g.shape)],
        out_specs=[_tok(tm, d), _tok(tm, d)],
        out_shape=[jax.ShapeDtypeStruct((t, d), f32), jax.ShapeDtypeStruct((t, d), MXU_DTYPE)],
        compiler_params=_cparams("parallel"),
    )(*xs, *wholes, h, g)


def ple_forward(h, p, w_gate, w_proj, g, name):
    t, d = h.shape
    tm = TOKEN_TILE

    def body(h_ref, p_ref, wg_ref, wp_ref, g_ref, ho_ref, gl_ref, emb_ref):
        hh = h_ref[...]
        gl = jnp.dot(hh.astype(MXU_DTYPE), wg_ref[...], preferred_element_type=f32)
        emb = jnp.dot(p_ref[...].astype(MXU_DTYPE), wp_ref[...], preferred_element_type=f32)
        gl_ref[...] = gl.astype(gl_ref.dtype)
        emb_ref[...] = emb.astype(emb_ref.dtype)
        ho_ref[...] = hh + _rms(jax.nn.sigmoid(gl) * emb, g_ref[...])

    return pl.pallas_call(
        body, name=name, grid=(t // tm,),
        in_specs=[_tok(tm, d), _tok(tm, p.shape[1]), _whole(w_gate.shape), _whole(w_proj.shape), _whole(g.shape)],
        out_specs=[_tok(tm, d)] * 3,
        out_shape=[jax.ShapeDtypeStruct((t, d), f32)] + [jax.ShapeDtypeStruct((t, d), MXU_DTYPE)] * 2,
        compiler_params=_cparams("parallel"),
    )(h, p, w_gate, w_proj, g)


def loss_and_grad(h, target, name):
    t, d = h.shape
    tm = TOKEN_TILE

    def body(h_ref, t_ref, l_ref, dh_ref):
        e = h_ref[...] - t_ref[...]
        dh_ref[...] = e * (1.0 / d)
        part = jnp.sum(jnp.sum(e * e, axis=1, keepdims=True), axis=0, keepdims=True) * (0.5 / d)
        _acc_rows(l_ref, jnp.broadcast_to(part, (1, LANES)))

    return pl.pallas_call(
        body, name=name, grid=(t // tm,),
        in_specs=[_tok(tm, d), _tok(tm, d)],
        out_specs=[_whole((1, LANES)), _tok(tm, d)],
        out_shape=[jax.ShapeDtypeStruct((1, LANES), f32), jax.ShapeDtypeStruct((t, d), f32)],
        compiler_params=_cparams("arbitrary"),
    )(h, target)


def bwd_through_norm_in(dh, gs, wts, h, g, name):
    t, d = h.shape
    tm = TOKEN_TILE
    ng = len(gs)
    wholes, readers = _weights(wts)
    nw = len(wholes)

    def body(*refs):
        dh_ref = refs[0]
        g_refs, w_refs = refs[1:1 + ng], refs[1 + ng:1 + ng + nw]
        h_ref, gain_ref, dho_ref, dg_ref = refs[1 + ng + nw:]
        dhn = None
        for g_ref, read in zip(g_refs, readers):
            part = jnp.dot(g_ref[...].astype(MXU_DTYPE), read(w_refs), preferred_element_type=f32)
            dhn = part if dhn is None else dhn + part
        dx, dgr = _rms_bwd(dhn, h_ref[...], gain_ref[...])
        dho_ref[...] = dh_ref[...] + dx
        _acc_rows(dg_ref, dgr)

    return pl.pallas_call(
        body, name=name, grid=(t // tm,),
        in_specs=[_tok(tm, d)] + [_tok(tm, x.shape[1]) for x in gs] + [_whole(w.shape) for w in wholes]
        + [_tok(tm, d), _whole(g.shape)],
        out_specs=[_tok(tm, d), _whole((1, d))],
        out_shape=[jax.ShapeDtypeStruct((t, d), f32), jax.ShapeDtypeStruct((1, d), f32)],
        compiler_params=_cparams("arbitrary"),
    )(dh, *gs, *wholes, h, g)


def bwd_through_norm_out(dh, n, g, wts, out_dtypes, name, relu2_of=None, rider=None):
    t, d = n.shape
    tm = TOKEN_TILE
    nw = len(wts)
    wholes, readers = _weights(wts)
    nwh = len(wholes)
    has_a = relu2_of is not None

    def body(*refs):
        dh_ref, n_ref, gain_ref = refs[:3]
        w_refs = refs[3:3 + nwh]
        rest = refs[3 + nwh:]
        if has_a:
            a_ref, rest = rest[0], rest[1:]
        dn_ref, dx_refs, dg_ref = rest[0], rest[1:1 + nw], rest[1 + nw]
        dn, dgr = _rms_bwd(dh_ref[...], n_ref[...].astype(f32), gain_ref[...])
        dnb = dn.astype(MXU_DTYPE)
        dn_ref[...] = dnb.astype(dn_ref.dtype)
        for read, dx_ref in zip(readers, dx_refs):
            dx = jnp.dot(dnb, read(w_refs), preferred_element_type=f32)
            if has_a:
                dx = dx * (2.0 * jnp.maximum(a_ref[...].astype(f32), 0.0))
            dx_ref[...] = dx.astype(dx_ref.dtype)
        _acc_rows(dg_ref, dgr)

    ins = [dh, n, g, *wholes] + ([relu2_of] if has_a else [])
    in_specs = [_tok(tm, d), _tok(tm, d), _whole(g.shape)] + [_whole(w.shape) for w in wholes]
    if has_a:
        in_specs.append(_tok(tm, relu2_of.shape[1]))
    grid = (t // tm,)
    body, in_specs, out_specs, out_shape, scratch = _ride(
        rider, body, in_specs, [_tok(tm, d)] + [_tok(tm, w.shape[1]) for w in wts] + [_whole((1, d))],
        [jax.ShapeDtypeStruct((t, d), MXU_DTYPE)]
        + [jax.ShapeDtypeStruct((t, w.shape[1]), dt) for w, dt in zip(wts, out_dtypes)]
        + [jax.ShapeDtypeStruct((1, d), f32)], [], grid)
    outs = pl.pallas_call(
        body, name=name, grid=grid, in_specs=in_specs, out_specs=out_specs, out_shape=out_shape,
        scratch_shapes=scratch, compiler_params=_cparams("arbitrary"),
    )(*ins, *(rider.inputs if rider else []))
    if rider:
        return outs[0], list(outs[1:1 + nw]), outs[1 + nw], list(outs[2 + nw:])
    return outs[0], list(outs[1:1 + nw]), outs[1 + nw]


def ple_backward(dh3, h2, gl, emb, g, w_gate_t, name):
    t, d = h2.shape
    tm = TOKEN_TILE

    def body(dh_ref, gl_ref, emb_ref, gain_ref, wt_ref, dho_ref, dgl_ref, demb_ref, dg_ref):
        gate = jax.nn.sigmoid(gl_ref[...].astype(f32))
        emb = emb_ref[...].astype(f32)
        dge, dgr = _rms_bwd(dh_ref[...], gate * emb, gain_ref[...])
        demb_ref[...] = (dge * gate).astype(demb_ref.dtype)
        dgl = (dge * emb * gate * (1.0 - gate)).astype(MXU_DTYPE)
        dgl_ref[...] = dgl.astype(dgl_ref.dtype)
        dho_ref[...] = dh_ref[...] + jnp.dot(dgl, wt_ref[...], preferred_element_type=f32)
        _acc_rows(dg_ref, dgr)

    return pl.pallas_call(
        body, name=name, grid=(t // tm,),
        in_specs=[_tok(tm, d), _tok(tm, d), _tok(tm, d), _whole(g.shape), _whole(w_gate_t.shape)],
        out_specs=[_tok(tm, d), _tok(tm, d), _tok(tm, d), _whole((1, d))],
        out_shape=[jax.ShapeDtypeStruct((t, d), f32), jax.ShapeDtypeStruct((t, d), MXU_DTYPE),
                   jax.ShapeDtypeStruct((t, d), MXU_DTYPE), jax.ShapeDtypeStruct((1, d), f32)],
        compiler_params=_cparams("arbitrary"),
    )(dh3, gl, emb, g, w_gate_t)


def _largest_tile(n, cap):
    if n <= cap:
        return n
    return max(c for c in range(LANES, cap + 1, LANES) if n % c == 0)


def weight_grad(x, gout, name, prologue="none", gain=None):
    t, k = x.shape
    n = gout.shape[1]
    tt = 1024
    tn = _largest_tile(n, 1024)
    tk = k if prologue == "rms" else _largest_tile(k, 1024)
    has_gain = prologue == "rms"

    def body(*refs):
        if has_gain:
            x_ref, gain_ref, g_ref, o_ref = refs
        else:
            x_ref, g_ref, o_ref = refs
        x = x_ref[...].astype(f32)
        if prologue == "relu2":
            x = jnp.square(jnp.maximum(x, 0.0))
        elif prologue == "rms":
            x = _rms(x, gain_ref[...])
        part = _mm_tn(x, g_ref[...])

        @pl.when(pl.program_id(2) == 0)
        def _():
            o_ref[...] = part

        @pl.when(pl.program_id(2) != 0)
        def _():
            o_ref[...] += part

    in_specs = [pl.BlockSpec((tt, tk), lambda i, j, s: (s, i))]
    ins = [x]
    if has_gain:
        in_specs.append(pl.BlockSpec(gain.shape, lambda i, j, s: (0, 0)))
        ins.append(gain)
    in_specs.append(pl.BlockSpec((tt, tn), lambda i, j, s: (s, j)))
    ins.append(gout)
    return pl.pallas_call(
        body, name=name, grid=(k // tk, n // tn, t // tt),
        in_specs=in_specs,
        out_specs=pl.BlockSpec((tk, tn), lambda i, j, s: (i, j)),
        out_shape=jax.ShapeDtypeStruct((k, n), f32),
        compiler_params=_cparams("parallel", "parallel", "arbitrary"),
    )(*ins)


SEQ_TILE = 256
HALO = 8


def _first_step():
    return jnp.logical_and(pl.program_id(0) == 0, pl.program_id(1) == 0)


def _accum(ref, val, first):
    @pl.when(first)
    def _():
        ref[...] = val

    @pl.when(jnp.logical_not(first))
    def _():
        ref[...] += val


def _softplus(x):
    return jnp.maximum(x, 0.0) + jnp.log1p(jnp.exp(-jnp.abs(x)))


def _neg_expm1(z):
    series = -z * (1.0 + z * (0.5 + z * (1.0 / 6.0 + z * (1.0 / 24.0 + z * (1.0 / 120.0)))))
    return jnp.where(z > -0.05, series, 1.0 - jnp.exp(z))


def _lru_gates(xc, ga, gab, gx, gxb, lam):
    r = jax.nn.sigmoid(dmm(xc, ga) + gab)
    i = jax.nn.sigmoid(dmm(xc, gx) + gxb)
    log_a = -LRU_C * r * _softplus(-lam)
    a = jnp.exp(log_a)
    u = jnp.sqrt(_neg_expm1(2.0 * log_a)) * (i * xc)
    return a, u


def _scan_down(a, u):
    n = a.shape[0]
    rows = lax.broadcasted_iota(jnp.int32, a.shape, 0)
    d = 1
    while d < n:
        keep = rows >= d
        a_s = jnp.where(keep, pltpu.roll(a, d, 0), 1.0)
        u_s = jnp.where(keep, pltpu.roll(u, d, 0), 0.0)
        u = a * u_s + u
        a = a * a_s
        d *= 2
    return a, u


def _scan_up(b, g):
    n = b.shape[0]
    rows = lax.broadcasted_iota(jnp.int32, b.shape, 0)
    d = 1
    while d < n:
        keep = rows < n - d
        b_s = jnp.where(keep, pltpu.roll(b, n - d, 0), 1.0)
        g_s = jnp.where(keep, pltpu.roll(g, n - d, 0), 0.0)
        g = g + b * g_s
        b = b * b_s
        d *= 2
    return g


def _seq_specs(ts, c, nt, reverse=False):
    per = ts // HALO

    def jj(j):
        return (nt - 1 - j) if reverse else j

    tile = pl.BlockSpec((1, ts, c), lambda b, j: (b, jj(j), 0))
    before = pl.BlockSpec((1, HALO, c), lambda b, j: (b, jnp.maximum(jj(j) * per - 1, 0), 0))
    after = pl.BlockSpec((1, HALO, c), lambda b, j: (b, jnp.minimum((jj(j) + 1) * per, nt * per - 1), 0))
    return tile, before, after


def _const2(shape):
    nd = len(shape)
    return pl.BlockSpec(tuple(shape), lambda b, j: (0,) * nd)


def lru_forward(xpre, gate, cw, cb, ga, gab, gx, gxb, lam, name):
    nb, ns, w = xpre.shape
    ts = SEQ_TILE
    nt = ns // ts
    tile, _, _ = _seq_specs(ts, w, nt)

    def body(xp_ref, gt_ref, cw_ref, cb_ref, ga_ref, gab_ref, gx_ref, gxb_ref, lam_ref,
             y_ref, xc_ref, hs_ref, xin, hcar):
        @pl.when(pl.program_id(1) == 0)
        def _():
            xin[0:HALO, :] = jnp.zeros((HALO, w), f32)
            hcar[...] = jnp.zeros_like(hcar)

        xin[HALO:HALO + ts, :] = xp_ref[0]
        xc = jnp.broadcast_to(cb_ref[...], (ts, w))
        for k in range(LRU_CONV):
            xc = xc + cw_ref[k:k + 1, :] * xin[pl.ds(HALO - LRU_CONV + 1 + k, ts), :]
        xin[0:HALO, :] = xin[ts:ts + HALO, :]
        a, u = _lru_gates(xc, ga_ref[...], gab_ref[...], gx_ref[...], gxb_ref[...], lam_ref[...])
        acum, h = _scan_down(a, u)
        h = h + acum * hcar[0:1, :]
        hcar[0:1, :] = h[ts - 1:ts, :]
        xc_ref[0] = xc
        hs_ref[0] = h
        y_ref[0] = (h * jax.nn.gelu(gt_ref[0])).astype(y_ref.dtype)

    params = [cw, cb, ga, gab, gx, gxb, lam]
    return pl.pallas_call(
        body, name=name, grid=(nb, nt),
        in_specs=[tile, tile] + [_const2(p.shape) for p in params],
        out_specs=[tile, tile, tile],
        out_shape=[jax.ShapeDtypeStruct((nb, ns, w), MXU_DTYPE), jax.ShapeDtypeStruct((nb, ns, w), f32),
                   jax.ShapeDtypeStruct((nb, ns, w), f32)],
        scratch_shapes=[pltpu.VMEM((ts + HALO, w), f32), pltpu.VMEM((HALO, w), f32)],
        compiler_params=_cparams("arbitrary", "arbitrary"),
    )(xpre, gate, *params)


def lru_backward(dy, xpre, gate, xc, hs, cw, cb, ga, gab, gx, gxb, lam, name, rider=None):
    nb, ns, w = xpre.shape
    ts = SEQ_TILE
    nt = ns // ts
    tile, before, _ = _seq_specs(ts, w, nt, reverse=True)

    def body(dy_ref, xp_ref, xpb_ref, gt_ref, xc_ref, hs_ref, hsb_ref,
             cw_ref, cb_ref, ga_ref, gab_ref, gx_ref, gxb_ref, lam_ref,
             dxp_ref, dgt_ref, dcw_ref, dcb_ref, dga_ref, dgab_ref, dgx_ref, dgxb_ref, dlam_ref,
             dxc_ext, gcar, xin):
        j = pl.program_id(1)
        first = _first_step()
        at_seq_start = j == nt - 1

        @pl.when(j == 0)
        def _():
            dxc_ext[ts:ts + HALO, :] = jnp.zeros((HALO, w), f32)
            gcar[...] = jnp.zeros_like(gcar)

        gt = gt_ref[0]
        h = hs_ref[0]
        dyv = dy_ref[0].astype(f32)
        gl, gelu_vjp = jax.vjp(jax.nn.gelu, gt)
        dgt_ref[0] = gelu_vjp(dyv * h)[0].astype(dgt_ref.dtype)
        dh = dyv * gl

        (a, _), gates_vjp = jax.vjp(_lru_gates, xc_ref[0], ga_ref[...], gab_ref[...], gx_ref[...], gxb_ref[...],
                                    lam_ref[...])
        rows = lax.broadcasted_iota(jnp.int32, (ts, w), 0)
        dh = dh + jnp.where(rows == ts - 1, gcar[0:1, :], 0.0)
        b = pltpu.roll(a, ts - 1, 0)
        g = _scan_up(b, dh)
        gcar[0:1, :] = a[0:1, :] * g[0:1, :]
        hprev_row = jnp.where(at_seq_start, 0.0, hsb_ref[0][HALO - 1:HALO, :])
        hprev = jnp.where(rows == 0, hprev_row, pltpu.roll(h, 1, 0))
        dxc, dga, dgab, dgx, dgxb, dlam = gates_vjp((g * hprev, g))

        _accum(dga_ref, dga, first)
        _accum(dgx_ref, dgx, first)
        _accum(dgab_ref, dgab, first)
        _accum(dgxb_ref, dgxb, first)
        _accum(dlam_ref, dlam, first)
        _accum(dcb_ref, jnp.sum(dxc, axis=0, keepdims=True), first)

        dxc_ext[0:ts, :] = dxc
        dxp = jnp.zeros((ts, w), f32)
        for k in range(LRU_CONV):
            dxp = dxp + cw_ref[k:k + 1, :] * dxc_ext[pl.ds(LRU_CONV - 1 - k, ts), :]
        dxp_ref[0] = dxp.astype(dxp_ref.dtype)
        dxc_ext[ts:ts + HALO, :] = dxc[0:HALO, :]

        xin[0:HALO, :] = jnp.where(at_seq_start, 0.0, xpb_ref[0])
        xin[HALO:HALO + ts, :] = xp_ref[0]
        dcw_rows = [jnp.sum(dxc * xin[pl.ds(HALO - LRU_CONV + 1 + k, ts), :], axis=0, keepdims=True)
                    for k in range(LRU_CONV)]
        dcw_rows += [jnp.zeros((1, w), f32)] * (HALO - LRU_CONV)
        _accum(dcw_ref, jnp.concatenate(dcw_rows, axis=0), first)

    params = [cw, cb, ga, gab, gx, gxb, lam]
    pshape = lambda p: jax.ShapeDtypeStruct(p.shape, f32)
    grid = (nb, nt)
    n_main = 3 + len(params) - 1
    body, in_specs, out_specs, out_shape, scratch = _ride(
        rider, body, [tile, tile, before, tile, tile, tile, before] + [_const2(p.shape) for p in params],
        [tile, tile, _const2((HALO, w))] + [_const2(p.shape) for p in params[1:]],
        [jax.ShapeDtypeStruct((nb, ns, w), MXU_DTYPE), jax.ShapeDtypeStruct((nb, ns, w), MXU_DTYPE),
         jax.ShapeDtypeStruct((HALO, w), f32)] + [pshape(p) for p in params[1:]],
        [pltpu.VMEM((ts + HALO, w), f32), pltpu.VMEM((HALO, w), f32), pltpu.VMEM((ts + HALO, w), f32)], grid)
    outs = pl.pallas_call(
        body, name=name, grid=grid, in_specs=in_specs, out_specs=out_specs, out_shape=out_shape,
        scratch_shapes=scratch, compiler_params=_cparams("arbitrary", "arbitrary"),
    )(dy, xpre, xpre, gate, xc, hs, hs, *params, *(rider.inputs if rider else []))
    return (list(outs[:n_main]), list(outs[n_main:])) if rider else outs


SB_TILE = 256


def _split_dot(x, m):
    hi = x.astype(MXU_DTYPE)
    lo = (x - hi.astype(f32)).astype(MXU_DTYPE)
    return jnp.dot(hi, m, preferred_element_type=f32) + jnp.dot(lo, m, preferred_element_type=f32)


def _suffix_matrices(n):
    r = lax.broadcasted_iota(jnp.int32, (n, n), 0)
    c = lax.broadcasted_iota(jnp.int32, (n, n), 1)
    return (r > c).astype(MXU_DTYPE), (r >= c).astype(MXU_DTYPE)


def _sb_logits(qh, kb, strict):
    z = _mm_nt(qh, kb)
    ls = jnp.minimum(z, 0.0) - jnp.log(1.0 + jnp.exp(-jnp.abs(z)))
    lk = ls - z
    if strict is not None:
        lk = jnp.where(strict, lk, 0.0)
    return ls, lk


def _head_masked(x, dtype):
    lane = lax.broadcasted_iota(jnp.int32, x.shape, 1)
    return (jnp.where(lane < SB_HEAD_DIM, x, 0.0).astype(dtype), jnp.where(lane >= SB_HEAD_DIM, x, 0.0).astype(dtype))


def _stack_heads(dst, x, tq):
    x0, x1 = _head_masked(x, dst.dtype)
    for blk in range(dst.shape[0]):
        dst[blk, 0:tq, :] = x0[blk * tq:(blk + 1) * tq]
        dst[blk, tq:2 * tq, :] = x1[blk * tq:(blk + 1) * tq]


def _strict_mask(tq):
    rr = lax.broadcasted_iota(jnp.int32, (2 * tq, tq), 0)
    cc = lax.broadcasted_iota(jnp.int32, (2 * tq, tq), 1)
    return cc < jnp.where(rr >= tq, rr - tq, rr)


def _sb_specs(ns):
    npair = SB_WIDTH // LANES
    q = pl.BlockSpec((1, ns, LANES), lambda b, p: (b, 0, p))
    k = pl.BlockSpec((1, ns, LANES), lambda b, p: (b, 0, npair + p))
    v = pl.BlockSpec((1, ns, LANES), lambda b, p: (b, 0, 2 * npair + p))
    return q, k, v, npair


def sb_forward(qkv, name, rider=None):
    nb, ns, _ = qkv.shape
    tq = SB_TILE
    nq = ns // tq
    qspec, kspec, vspec, npair = _sb_specs(ns)

    def body(q_ref, k_ref, v_ref, o_ref, qs, ks, vs, acc):
        scale = 1.0 / math.sqrt(SB_HEAD_DIM)
        _stack_heads(qs, q_ref[0] * scale, tq)
        ks[...] = k_ref[0].astype(MXU_DTYPE)
        _stack_heads(vs, v_ref[0], tq)
        mx, _ = _suffix_matrices(tq)
        strict = _strict_mask(tq)

        def step(q2, blks, r2, masked):
            kbs = [ks[pl.ds(pl.multiple_of(b * tq, tq), tq), :] for b in blks]
            lg = [_sb_logits(q2, kb, strict if masked else None) for kb in kbs]
            sums = [jnp.dot(lk.astype(MXU_DTYPE), mx, preferred_element_type=f32) for _, lk in lg]
            total = None
            for (ls, lk), s, b in zip(lg, sums, blks):
                a = r2 + s
                w = jnp.exp(ls + a)
                if masked:
                    w = jnp.where(strict, w, 0.0)
                wb = w.astype(MXU_DTYPE)
                part = (jnp.dot(wb[:tq], vs[b, 0:tq, :], preferred_element_type=f32)
                        + jnp.dot(wb[tq:], vs[b, tq:2 * tq, :], preferred_element_type=f32))
                total = part if total is None else total + part
                r2 = a[:, 0:1] + lk[:, 0:1]
            acc[...] += total
            return r2

        def q_block(qi, carry):
            acc[...] = jnp.zeros_like(acc)
            q2 = qs[qi]
            r2 = step(q2, [qi], jnp.zeros((2 * tq, 1), f32), True)
            r2 = lax.fori_loop(0, lax.shift_right_logical(qi, 1),
                               lambda i, r: step(q2, [qi - 1 - 2 * i, qi - 2 - 2 * i], r, False), r2)
            lax.cond(jnp.bitwise_and(qi, 1) == 1, lambda r: step(q2, [0], r, False), lambda r: r, r2)
            o_ref[0, pl.ds(pl.multiple_of(qi * tq, tq), tq), :] = acc[...]
            return carry

        lax.fori_loop(0, nq, q_block, 0)

    grid = (nb, npair)
    body, in_specs, out_specs, out_shape, scratch = _ride(
        rider, body, [qspec, kspec, vspec], [pl.BlockSpec((1, ns, LANES), lambda b, p: (b, 0, p))],
        [jax.ShapeDtypeStruct((nb, ns, SB_WIDTH), f32)],
        [pltpu.VMEM((nq, 2 * tq, LANES), MXU_DTYPE), pltpu.VMEM((ns, LANES), MXU_DTYPE),
         pltpu.VMEM((nq, 2 * tq, LANES), MXU_DTYPE), pltpu.VMEM((tq, LANES), f32)], grid)
    outs = pl.pallas_call(
        body, name=name, grid=grid, in_specs=in_specs, out_specs=out_specs, out_shape=out_shape,
        scratch_shapes=scratch,
        compiler_params=_cparams("arbitrary", "arbitrary") if rider else _cparams("parallel", "parallel"),
    )(qkv, qkv, qkv, *(rider.inputs if rider else []))
    return (outs[0], list(outs[1:])) if rider else outs[0]


def sb_backward(qkv, o, do, name, rider=None):
    nb, ns, _ = qkv.shape
    tq = SB_TILE
    nq = ns // tq
    qspec, kspec, vspec, npair = _sb_specs(ns)
    ospec = pl.BlockSpec((1, ns, LANES), lambda b, p: (b, 0, p))

    def body(q_ref, k_ref, v_ref, o_ref, do_ref, dq_ref, dk_ref, dv_ref, qs, ks, kcat, vs, dos, dqacc, dkacc, dvacc):
        scale = 1.0 / math.sqrt(SB_HEAD_DIM)
        _stack_heads(qs, q_ref[0] * scale, tq)
        ks[...] = k_ref[0].astype(MXU_DTYPE)
        _stack_heads(kcat, k_ref[0], tq)
        vs[...] = v_ref[0].astype(MXU_DTYPE)
        _stack_heads(dos, do_ref[0].astype(f32), tq)
        dkacc[...] = jnp.zeros_like(dkacc)
        dvacc[...] = jnp.zeros_like(dvacc)
        mx, mi = _suffix_matrices(tq)
        strict = _strict_mask(tq)

        def step(q2, do2, dtot2, blks, carry, masked):
            r2, g2 = carry
            k0s = [pl.multiple_of(b * tq, tq) for b in blks]
            lg = [_sb_logits(q2, ks[pl.ds(k0, tq), :], strict if masked else None) for k0 in k0s]
            dws = [_mm_nt(do2, vs[pl.ds(k0, tq), :]) for k0 in k0s]
            sums = [jnp.dot(lk.astype(MXU_DTYPE), mx, preferred_element_type=f32) for _, lk in lg]
            wbs, es = [], []
            for (ls, lk), s in zip(lg, sums):
                a = r2 + s
                w = jnp.exp(ls + a)
                if masked:
                    w = jnp.where(strict, w, 0.0)
                wbs.append(w.astype(MXU_DTYPE))
                r2 = a[:, 0:1] + lk[:, 0:1]
            es = [wb.astype(f32) * dw for wb, dw in zip(wbs, dws)]
            esums = [_split_dot(e, mi) for e in es]
            dq = None
            for (ls, _), e, esum, wb, b, k0 in zip(lg, es, esums, wbs, blks, k0s):
                esuf = g2 + esum
                beta = jnp.exp(ls)
                dz = e * (1.0 - beta) - beta * (dtot2 - esuf)
                if masked:
                    dz = jnp.where(strict, dz, 0.0)
                dzb = dz.astype(MXU_DTYPE)
                part = (jnp.dot(dzb[:tq], kcat[b, 0:tq, :], preferred_element_type=f32)
                        + jnp.dot(dzb[tq:], kcat[b, tq:2 * tq, :], preferred_element_type=f32))
                dq = part if dq is None else dq + part
                dkacc[pl.ds(k0, tq), :] += _mm_tn(dzb, q2)
                dvacc[pl.ds(k0, tq), :] += _mm_tn(wb, do2)
                g2 = esuf[:, 0:1]
            dqacc[...] += dq
            return r2, g2

        def q_block(qi, carry):
            dqacc[...] = jnp.zeros_like(dqacc)
            q2, do2 = qs[qi], dos[qi]
            ov = o_ref[0, pl.ds(pl.multiple_of(qi * tq, tq), tq), :]
            dtot2 = jnp.sum(do2.astype(f32) * jnp.concatenate([ov, ov], axis=0), axis=1, keepdims=True)
            zero = jnp.zeros((2 * tq, 1), f32)
            c = step(q2, do2, dtot2, [qi], (zero, zero), True)
            c = lax.fori_loop(0, lax.shift_right_logical(qi, 1),
                              lambda i, c: step(q2, do2, dtot2, [qi - 1 - 2 * i, qi - 2 - 2 * i], c, False), c)
            lax.cond(jnp.bitwise_and(qi, 1) == 1, lambda c: step(q2, do2, dtot2, [0], c, False), lambda c: c, c)
            dq_ref[0, pl.ds(pl.multiple_of(qi * tq, tq), tq), :] = (dqacc[...] * scale).astype(dq_ref.dtype)
            return carry

        lax.fori_loop(0, nq, q_block, 0)
        dk_ref[0] = dkacc[...].astype(dk_ref.dtype)
        dv_ref[0] = dvacc[...].astype(dv_ref.dtype)

    dshape = jax.ShapeDtypeStruct((nb, ns, SB_WIDTH), MXU_DTYPE)
    stacked = pltpu.VMEM((nq, 2 * tq, LANES), MXU_DTYPE)
    flat = pltpu.VMEM((ns, LANES), MXU_DTYPE)
    grid = (nb, npair)
    body, in_specs, out_specs, out_shape, scratch = _ride(
        rider, body, [qspec, kspec, vspec, ospec, ospec], [ospec, ospec, ospec], [dshape, dshape, dshape],
        [stacked, flat, stacked, flat, stacked,
         pltpu.VMEM((tq, LANES), f32), pltpu.VMEM((ns, LANES), f32), pltpu.VMEM((ns, LANES), f32)], grid)
    outs = pl.pallas_call(
        body, name=name, grid=grid, in_specs=in_specs, out_specs=out_specs, out_shape=out_shape,
        scratch_shapes=scratch,
        compiler_params=_cparams("arbitrary", "arbitrary") if rider else _cparams("parallel", "parallel"),
    )(qkv, qkv, qkv, o, do, *(rider.inputs if rider else []))
    return (list(outs[:3]), list(outs[3:])) if rider else list(outs)


SSM_PAIRS = SSM_HEADS // 2
PAIRS_PER_GROUP = SSM_PAIRS // SSM_GROUPS
GROUP_WIDTH = SSM_WIDTH // SSM_GROUPS


def _silu(x):
    return x * jax.nn.sigmoid(x)


def _ssd_chunk(xs_pre, b_pre, c_pre, dt_raw, dt_raw_t, z, st, dt_bias_r, dt_bias_c, a_log_r, a_log_c, d_skip,
               gains):
    n = dt_raw.shape[0]
    rows = lax.broadcasted_iota(jnp.int32, (n, n), 0)
    cols = lax.broadcasted_iota(jnp.int32, (n, n), 1)
    tril = cols <= rows
    tri_l = tril.astype(f32)
    tri_u = (rows <= cols).astype(f32)
    lane = lax.broadcasted_iota(jnp.int32, (n, LANES), 1)
    sub = lax.broadcasted_iota(jnp.int32, (LANES, n), 0)

    dt = _softplus(dt_raw + dt_bias_r)
    a_r = -jnp.exp(a_log_r)
    da = dt * a_r
    acs = _mm_exact(tri_l, da)
    dt_t = _softplus(dt_raw_t + dt_bias_c)
    acs_t = _mm_exact(dt_t * (-jnp.exp(a_log_c)), tri_u)

    bs = [_silu(b) for b in b_pre]
    cs = [_silu(c) for c in c_pre]
    cb = [dmm_nt(cs[g], bs[g]) for g in range(SSM_GROUPS)]

    end = jnp.sum(da, axis=0, keepdims=True)
    lane_row = lax.broadcasted_iota(jnp.int32, (1, LANES), 1)
    first_head = lane < SSM_HEAD_DIM
    first_head_row = lane_row < SSM_HEAD_DIM

    def head_col(v, h):
        return jnp.sum(jnp.where((lane if v.shape[0] == n else lane_row) == h, v, 0.0), axis=1, keepdims=True)

    ys, st_new = [], []
    for p in range(SSM_PAIRS):
        g = p // PAIRS_PER_GROUP
        h0, h1 = 2 * p, 2 * p + 1
        xs = _silu(xs_pre[p])
        acols = [head_col(acs, h0), head_col(acs, h1)]
        dt_p = jnp.where(first_head, head_col(dt, h0), head_col(dt, h1))
        acs_p = jnp.where(first_head, acols[0], acols[1])
        end_p = jnp.where(first_head_row, head_col(end, h0), head_col(end, h1))
        dsk_p = jnp.where(first_head_row, head_col(d_skip, h0), head_col(d_skip, h1))
        xdt = xs * dt_p
        y = jnp.exp(acs_p) * dmm(cs[g], st[p])
        for hh in range(2):
            row = jnp.sum(jnp.where(sub == 2 * p + hh, acs_t, 0.0), axis=0, keepdims=True)
            decay = jnp.where(tril, jnp.exp(jnp.where(tril, acols[hh] - row, 0.0)), 0.0)
            head = first_head if hh == 0 else jnp.logical_not(first_head)
            y = y + dmm(cb[g] * decay, jnp.where(head, xdt, 0.0))
        st_new.append(jnp.exp(end_p) * st[p] + dmm_tn(bs[g], xdt * jnp.exp(end_p - acs_p)))
        ys.append(y + dsk_p * xs)
    out = []
    for g in range(SSM_GROUPS):
        yg = jnp.concatenate(ys[g * PAIRS_PER_GROUP:(g + 1) * PAIRS_PER_GROUP], axis=1) * _silu(z[g])
        out.append(_rms(yg, gains[g]))
    return out, st_new


def _ssd_chunk_inputs(xconv, dtr, z, st_ref, gain):
    xs_pre = [xconv[:, LANES * p:LANES * (p + 1)] for p in range(SSM_PAIRS)]
    b0 = SSM_WIDTH
    c0 = SSM_WIDTH + SSM_GROUPS * SSM_STATE
    b_pre = [xconv[:, b0 + SSM_STATE * g:b0 + SSM_STATE * (g + 1)] for g in range(SSM_GROUPS)]
    c_pre = [xconv[:, c0 + SSM_STATE * g:c0 + SSM_STATE * (g + 1)] for g in range(SSM_GROUPS)]
    zs = [z[:, GROUP_WIDTH * g:GROUP_WIDTH * (g + 1)] for g in range(SSM_GROUPS)]
    sts = [st_ref[p] for p in range(SSM_PAIRS)]
    gains = [gain[:, GROUP_WIDTH * g:GROUP_WIDTH * (g + 1)] for g in range(SSM_GROUPS)]
    return xs_pre, b_pre, c_pre, dtr, dtr.T, zs, sts, gains


def ssd_forward(xbc, dt_raw, z, cw, cb, dbr, dbc, alr, alc, dsk, gain, name):
    nb, ns, wx = xbc.shape
    ln = SSM_CHUNK
    nt = ns // ln
    tile = lambda c: pl.BlockSpec((1, ln, c), lambda b, j: (b, j, 0))
    st_spec = pl.BlockSpec((1, 1, SSM_PAIRS, SSM_STATE, LANES), lambda b, j: (b, j, 0, 0, 0))

    def body(xbc_ref, dt_ref, z_ref, cw_ref, cb_ref, dbr_ref, dbc_ref, alr_ref, alc_ref, dsk_ref, gain_ref,
             y_ref, xconv_ref, stp_ref, xin, st):
        @pl.when(pl.program_id(1) == 0)
        def _():
            xin[0:HALO, :] = jnp.zeros((HALO, wx), f32)
            st[...] = jnp.zeros_like(st)

        xin[HALO:HALO + ln, :] = xbc_ref[0]
        xconv = jnp.broadcast_to(cb_ref[...], (ln, wx))
        for k in range(SSM_CONV):
            xconv = xconv + cw_ref[k:k + 1, :] * xin[pl.ds(HALO - SSM_CONV + 1 + k, ln), :]
        xin[0:HALO, :] = xin[ln:ln + HALO, :]
        xconv_ref[0] = xconv
        stp_ref[0, 0] = st[...]
        xs_pre, b_pre, c_pre, dtr, dtr_t, zs, sts, gains = _ssd_chunk_inputs(xconv, dt_ref[0], z_ref[0], st,
                                                                             gain_ref[...])
        out, st_new = _ssd_chunk(xs_pre, b_pre, c_pre, dtr, dtr_t, zs, sts, dbr_ref[...], dbc_ref[...],
                                 alr_ref[...], alc_ref[...], dsk_ref[...], gains)
        y_ref[0] = jnp.concatenate(out, axis=1).astype(y_ref.dtype)
        for p in range(SSM_PAIRS):
            st[p] = st_new[p]

    params = [cw, cb, dbr, dbc, alr, alc, dsk, gain]
    return pl.pallas_call(
        body, name=name, grid=(nb, nt),
        in_specs=[tile(wx), tile(LANES), tile(SSM_WIDTH)] + [_const2(p.shape) for p in params],
        out_specs=[tile(SSM_WIDTH), tile(wx), st_spec],
        out_shape=[jax.ShapeDtypeStruct((nb, ns, SSM_WIDTH), MXU_DTYPE), jax.ShapeDtypeStruct((nb, ns, wx), f32),
                   jax.ShapeDtypeStruct((nb, nt, SSM_PAIRS, SSM_STATE, LANES), f32)],
        scratch_shapes=[pltpu.VMEM((ln + HALO, wx), f32), pltpu.VMEM((SSM_PAIRS, SSM_STATE, LANES), f32)],
        compiler_params=_cparams("arbitrary", "arbitrary"),
    )(xbc, dt_raw, z, *params)


def ssd_backward(dy, xbc, xconv, dt_raw, z, stp, cw, cb, dbr, dbc, alr, alc, dsk, gain, name):
    nb, ns, wx = xbc.shape
    ln = SSM_CHUNK
    nt = ns // ln
    per = ln // HALO
    rj = lambda j: nt - 1 - j
    tile = lambda c: pl.BlockSpec((1, ln, c), lambda b, j: (b, rj(j), 0))
    before = pl.BlockSpec((1, HALO, wx), lambda b, j: (b, jnp.maximum(rj(j) * per - 1, 0), 0))
    st_spec = pl.BlockSpec((1, 1, SSM_PAIRS, SSM_STATE, LANES), lambda b, j: (b, rj(j), 0, 0, 0))

    def body(dy_ref, xbc_ref, xbcb_ref, xconv_ref, dt_ref, z_ref, stp_ref,
             cw_ref, cb_ref, dbr_ref, dbc_ref, alr_ref, alc_ref, dsk_ref, gain_ref,
             dxbc_ref, ddt_ref, dz_ref, dcw_ref, dcb_ref, ddbr_ref, ddbc_ref, dalr_ref, dalc_ref, ddsk_ref, dgain_ref,
             dxc_ext, dst, xin):
        j = pl.program_id(1)
        first = _first_step()
        at_seq_start = j == nt - 1

        @pl.when(j == 0)
        def _():
            dxc_ext[ln:ln + HALO, :] = jnp.zeros((HALO, wx), f32)
            dst[...] = jnp.zeros_like(dst)

        xs_pre, b_pre, c_pre, dtr, dtr_t, zs, sts, gains = _ssd_chunk_inputs(xconv_ref[0], dt_ref[0], z_ref[0],
                                                                             stp_ref.at[0, 0], gain_ref[...])
        _, vjp = jax.vjp(_ssd_chunk, xs_pre, b_pre, c_pre, dtr, dtr_t, zs, sts, dbr_ref[...], dbc_ref[...],
                         alr_ref[...], alc_ref[...], dsk_ref[...], gains)
        dyv = dy_ref[0].astype(f32)
        cot = ([dyv[:, GROUP_WIDTH * g:GROUP_WIDTH * (g + 1)] for g in range(SSM_GROUPS)],
               [dst[p] for p in range(SSM_PAIRS)])
        dxs, db, dc, ddt, ddt_t, dzs, dsts, ddbr, ddbc, dalr, dalc, ddsk, dgains = vjp(cot)
        for p in range(SSM_PAIRS):
            dst[p] = dsts[p]
        ddt_ref[0] = (ddt + ddt_t.T).astype(ddt_ref.dtype)
        dz_ref[0] = jnp.concatenate(dzs, axis=1).astype(dz_ref.dtype)
        _accum(ddbr_ref, ddbr, first)
        _accum(ddbc_ref, ddbc, first)
        _accum(dalr_ref, dalr, first)
        _accum(dalc_ref, dalc, first)
        _accum(ddsk_ref, ddsk, first)
        _accum(dgain_ref, jnp.concatenate(dgains, axis=1), first)

        dxc = jnp.concatenate(dxs + db + dc, axis=1)
        _accum(dcb_ref, jnp.sum(dxc, axis=0, keepdims=True), first)
        dxc_ext[0:ln, :] = dxc
        dxp = jnp.zeros((ln, wx), f32)
        for k in range(SSM_CONV):
            dxp = dxp + cw_ref[k:k + 1, :] * dxc_ext[pl.ds(SSM_CONV - 1 - k, ln), :]
        dxbc_ref[0] = dxp.astype(dxbc_ref.dtype)
        dxc_ext[ln:ln + HALO, :] = dxc[0:HALO, :]

        xin[0:HALO, :] = jnp.where(at_seq_start, 0.0, xbcb_ref[0])
        xin[HALO:HALO + ln, :] = xbc_ref[0]
        dcw_rows = [jnp.sum(dxc * xin[pl.ds(HALO - SSM_CONV + 1 + k, ln), :], axis=0, keepdims=True)
                    for k in range(SSM_CONV)]
        dcw_rows += [jnp.zeros((1, wx), f32)] * (HALO - SSM_CONV)
        _accum(dcw_ref, jnp.concatenate(dcw_rows, axis=0), first)

    params = [cw, cb, dbr, dbc, alr, alc, dsk, gain]
    pshape = lambda p: jax.ShapeDtypeStruct(p.shape, f32)
    return pl.pallas_call(
        body, name=name, grid=(nb, nt),
        in_specs=[tile(SSM_WIDTH), tile(wx), before, tile(wx), tile(LANES), tile(SSM_WIDTH), st_spec]
        + [_const2(p.shape) for p in params],
        out_specs=[tile(wx), tile(LANES), tile(SSM_WIDTH), _const2((HALO, wx))] + [_const2(p.shape) for p in params[1:]],
        out_shape=[jax.ShapeDtypeStruct((nb, ns, wx), MXU_DTYPE), jax.ShapeDtypeStruct((nb, ns, LANES), MXU_DTYPE),
                   jax.ShapeDtypeStruct((nb, ns, SSM_WIDTH), MXU_DTYPE), jax.ShapeDtypeStruct((HALO, wx), f32)]
        + [pshape(p) for p in params[1:]],
        scratch_shapes=[pltpu.VMEM((ln + HALO, wx), f32), pltpu.VMEM((SSM_PAIRS, SSM_STATE, LANES), f32),
                        pltpu.VMEM((ln + HALO, wx), f32)],
        compiler_params=_cparams("arbitrary", "arbitrary"),
    )(dy, xbc, xbc, xconv, dt_raw, z, stp, *params)


CONF_HALO = 32
CONF_OFF = CONF_HALO - CONF_KERNEL + 1


def _conf_specs(ts, c, nt):
    per = ts // CONF_HALO
    tile = pl.BlockSpec((1, ts, c), lambda b, j: (b, j, 0))
    before = pl.BlockSpec((1, CONF_HALO, c), lambda b, j: (b, jnp.maximum(j * per - 1, 0), 0))
    after = pl.BlockSpec((1, CONF_HALO, c), lambda b, j: (b, jnp.minimum((j + 1) * per, nt * per - 1), 0))
    return tile, before, after


SUBLANES = 8


def _shifted_copies(dst, x):
    rows = x.shape[0]
    dst[0] = x
    for b in range(1, SUBLANES):
        dst[b] = pltpu.roll(x, rows - b, 0)


def _window(copies, off, size):
    b = off % SUBLANES
    return copies[b, pl.ds(off - b, size), :]


def _glu(x):
    return x[:, :CONF_WIDTH] * jax.nn.sigmoid(x[:, CONF_WIDTH:])


def _layernorm_parts(c):
    xc = c - jnp.mean(c, axis=-1, keepdims=True)
    r = lax.rsqrt(jnp.mean(xc * xc, axis=-1, keepdims=True) + EPS)
    return xc * r, r


def conf_forward(glu, cw, cb, ln_g, ln_b, name):
    nb, ns, wg = glu.shape
    w = CONF_WIDTH
    ts = SEQ_TILE
    nt = ns // ts
    tile, before, _ = _conf_specs(ts, wg, nt)

    def body(x_ref, xb_ref, cw_ref, cb_ref, g_ref, b_ref, y_ref, u_rot):
        _shifted_copies(u_rot, jnp.concatenate(
            [jnp.where(pl.program_id(1) == 0, 0.0, _glu(xb_ref[0])), _glu(x_ref[0])], axis=0))
        conv = jnp.broadcast_to(cb_ref[...], (ts, w))
        for k in range(CONF_KERNEL):
            conv = conv + cw_ref[k:k + 1, :] * _window(u_rot, CONF_OFF + k, ts)
        xhat, _ = _layernorm_parts(conv)
        y_ref[0] = _silu(xhat * g_ref[...] + b_ref[...]).astype(y_ref.dtype)

    params = [cw, cb, ln_g, ln_b]
    return pl.pallas_call(
        body, name=name, grid=(nb, nt),
        in_specs=[tile, before] + [_const2(p.shape) for p in params],
        out_specs=pl.BlockSpec((1, ts, w), lambda b, j: (b, j, 0)),
        out_shape=jax.ShapeDtypeStruct((nb, ns, w), MXU_DTYPE),
        scratch_shapes=[pltpu.VMEM((SUBLANES, ts + CONF_HALO, w), f32)],
        compiler_params=_cparams("parallel", "parallel"),
    )(glu, glu, *params)


def conf_backward(dy, glu, cw, cb, ln_g, ln_b, name):
    nb, ns, wg = glu.shape
    w = CONF_WIDTH
    ts = SEQ_TILE
    nt = ns // ts
    te = ts + CONF_HALO
    tile, before, after = _conf_specs(ts, wg, nt)
    dtile, _, dafter = _conf_specs(ts, w, nt)

    def body(dy_ref, dya_ref, x_ref, xb_ref, xa_ref, cw_ref, cb_ref, g_ref, b_ref,
             dx_ref, dcw_ref, dcb_ref, dg_ref, db_ref, u_ext, dc_ext):
        j = pl.program_id(1)
        first = _first_step()
        x = x_ref[0]
        _shifted_copies(u_ext, jnp.concatenate(
            [jnp.where(j == 0, 0.0, _glu(xb_ref[0])), _glu(x), _glu(xa_ref[0])], axis=0))
        conv = jnp.broadcast_to(cb_ref[...], (te, w))
        for k in range(CONF_KERNEL):
            conv = conv + cw_ref[k:k + 1, :] * _window(u_ext, CONF_OFF + k, te)
        xhat, r = _layernorm_parts(conv)
        lnout = xhat * g_ref[...] + b_ref[...]
        sg = jax.nn.sigmoid(lnout)
        rows = lax.broadcasted_iota(jnp.int32, (te, w), 0)
        dyv = jnp.concatenate([dy_ref[0].astype(f32), dya_ref[0].astype(f32)], axis=0)
        dyv = jnp.where(jnp.logical_and(j == nt - 1, rows >= ts), 0.0, dyv)
        dln = dyv * sg * (1.0 + lnout * (1.0 - sg))
        in_tile = rows < ts
        _accum(dg_ref, jnp.sum(jnp.where(in_tile, dln * xhat, 0.0), axis=0, keepdims=True), first)
        _accum(db_ref, jnp.sum(jnp.where(in_tile, dln, 0.0), axis=0, keepdims=True), first)
        dxh = dln * g_ref[...]
        dconv = r * (dxh - jnp.mean(dxh, axis=-1, keepdims=True) - xhat * jnp.mean(dxh * xhat, axis=-1, keepdims=True))
        _shifted_copies(dc_ext, dconv)
        dct = dconv[0:ts, :]
        _accum(dcb_ref, jnp.sum(dct, axis=0, keepdims=True), first)
        du = jnp.zeros((ts, w), f32)
        dcw_rows = []
        for k in range(CONF_KERNEL):
            du = du + cw_ref[k:k + 1, :] * _window(dc_ext, CONF_KERNEL - 1 - k, ts)
            dcw_rows.append(jnp.sum(dct * _window(u_ext, CONF_OFF + k, ts), axis=0, keepdims=True))
        dcw_rows.append(jnp.zeros((1, w), f32))
        _accum(dcw_ref, jnp.concatenate(dcw_rows, axis=0), first)
        sb = jax.nn.sigmoid(x[:, w:])
        dx_ref[0] = jnp.concatenate([du * sb, du * x[:, :w] * sb * (1.0 - sb)], axis=1).astype(dx_ref.dtype)

    params = [cw, cb, ln_g, ln_b]
    return pl.pallas_call(
        body, name=name, grid=(nb, nt),
        in_specs=[dtile, dafter, tile, before, after] + [_const2(p.shape) for p in params],
        out_specs=[tile] + [_const2(p.shape) for p in params],
        out_shape=[jax.ShapeDtypeStruct((nb, ns, wg), MXU_DTYPE)] + [jax.ShapeDtypeStruct(p.shape, f32) for p in params],
        scratch_shapes=[pltpu.VMEM((SUBLANES, te + CONF_HALO, w), f32), pltpu.VMEM((SUBLANES, te, w), f32)],
        compiler_params=_cparams("arbitrary", "arbitrary"),
    )(dy, dy, glu, glu, glu, *params)


def _row(v):
    return v.reshape(1, -1).astype(f32)


def _pad_to(v, n, axis):
    pads = [(0, 0)] * v.ndim
    pads[axis] = (0, n - v.shape[axis])
    return jnp.pad(v, pads)


def _block_diag(w):
    nh, d, _ = w.shape
    eye = jnp.eye(nh, dtype=w.dtype)
    return (eye[:, None, :, None] * w[:, :, None, :]).reshape(nh * d, nh * d)


def _diag_blocks(m, nh):
    d = m.shape[0] // nh
    idx = jnp.arange(nh)
    return m.reshape(nh, d, nh, d)[idx, :, idx, :]


def _mix_even_fwd(h, gpre, w, wl, nb, ns, rider=None):
    t = nb * ns
    w_in = wl["w_in"]
    w_lx, w_lg = Part(w_in, 0, LRU_WIDTH, 1), Part(w_in, LRU_WIDTH, LRU_WIDTH, 1)
    w_qkv = Part(w_in, 2 * LRU_WIDTH, 3 * SB_WIDTH, 1)
    xpre, gate, qkv = norm_matmul(h, gpre, [w_lx, w_lg, w_qkv], [f32, f32, f32], name="ev_in_proj")
    lru_p = [w["ev_lru_conv_w"][0], _row(w["ev_lru_conv_b"][0]),
             _block_diag(w["ev_lru_gate_a_w"][0]).astype(MXU_DTYPE), _row(w["ev_lru_gate_a_b"][0]),
             _block_diag(w["ev_lru_gate_x_w"][0]).astype(MXU_DTYPE), _row(w["ev_lru_gate_x_b"][0]),
             _row(w["ev_lru_lambda"][0])]
    xpre3, gate3, qkv3 = xpre.reshape(nb, ns, -1), gate.reshape(nb, ns, -1), qkv.reshape(nb, ns, -1)
    y_a, xc, hs = lru_forward(xpre3, gate3, *lru_p, name="ev_lru_fwd")
    o = sb_forward(qkv3, name="ev_sb_fwd", rider=rider)
    carried = None
    if rider is not None:
        o, carried = o
    ys = [y_a.reshape(t, -1), o.reshape(t, -1)]
    saved = dict(xpre=xpre3, gate=gate3, qkv=qkv3, xc=xc, hs=hs, o=o, lru_p=lru_p)
    return ys, saved, carried


def _mix_even_bwd(dys, saved, wtl, nb, ns, lru_rider=None, attention_rider=None):
    t = nb * ns
    dy_a, dy_b = [d.reshape(nb, ns, -1) for d in dys]
    outs = lru_backward(dy_a, saved["xpre"], saved["gate"], saved["xc"], saved["hs"], *saved["lru_p"],
                        name="ev_lru_bwd", rider=lru_rider)
    lru_carried = None
    if lru_rider is not None:
        outs, lru_carried = outs
    dxp, dgt, dcw, dcb, dga, dgab, dgx, dgxb, dlam = outs
    rider = attention_rider(lru_carried) if attention_rider is not None else None
    carried = None
    if rider is None:
        dq, dk, dv = sb_backward(saved["qkv"], saved["o"], dy_b, name="ev_sb_bwd")
    else:
        (dq, dk, dv), carried = sb_backward(saved["qkv"], saved["o"], dy_b, name="ev_sb_bwd", rider=rider)
    w_in_t = wtl["w_in"]
    pieces = [dxp, dgt, dq, dk, dv]
    gs = [d.reshape(t, -1) for d in pieces]
    wts = [Part(w_in_t, LRU_WIDTH * i, LRU_WIDTH, 0) for i in range(5)]
    grads = {
        "ev_lru_conv_w": dcw[:LRU_CONV][None], "ev_lru_conv_b": dcb,
        "ev_lru_gate_a_w": _diag_blocks(dga, LRU_HEADS)[None], "ev_lru_gate_a_b": dgab,
        "ev_lru_gate_x_w": _diag_blocks(dgx, LRU_HEADS)[None], "ev_lru_gate_x_b": dgxb,
        "ev_lru_lambda": dlam,
    }
    return gs, wts, grads, carried


def _odd_params(w):
    ssd_p = [w["od_ssm_conv_w"][0], _row(w["od_ssm_conv_b"][0]),
             _pad_to(_row(w["od_ssm_dt_bias"][0]), LANES, 1), _pad_to(_row(w["od_ssm_dt_bias"][0]), LANES, 1).T,
             _pad_to(_row(w["od_ssm_a_log"][0]), LANES, 1), _pad_to(_row(w["od_ssm_a_log"][0]), LANES, 1).T,
             _pad_to(_row(w["od_ssm_d"][0]), LANES, 1), _row(w["od_ssm_norm"][0])]
    conf_p = [_pad_to(w["od_cm_conv_w"][0], CONF_HALO, 0), _row(w["od_cm_conv_b"][0]),
              _row(w["od_cm_ln_g"][0]), _row(w["od_cm_ln_b"][0])]
    return ssd_p, conf_p


ODD_SPLITS = (SSM_WIDTH, SSM_WIDTH + SSM_XBC, SSM_WIDTH + SSM_XBC + SSM_HEADS)


def _mix_odd_fwd(h, gpre, w, wl, nb, ns, rider=None):
    assert rider is None
    t = nb * ns
    w_in = wl["w_in"]
    s0, s1, s2 = ODD_SPLITS
    w_al = jnp.concatenate([w_in[:, :s1], w_in[:, s2:], _pad_to(w_in[:, s1:s2], LANES, 1)], axis=1)
    widths = (s0, s1 - s0, w_in.shape[1] - s2, LANES)
    starts = (0, s0, s1, s1 + widths[2])
    zz, xbc, glu, dtr = norm_matmul(h, gpre, [Part(w_al, a, n, 1) for a, n in zip(starts, widths)], [f32] * 4,
                                    name="od_in_proj")
    ssd_p, conf_p = _odd_params(w)
    zz3, xbc3, dtr3, glu3 = [a.reshape(nb, ns, -1) for a in (zz, xbc, dtr, glu)]
    y_c, xconv, stp = ssd_forward(xbc3, dtr3, zz3, *ssd_p, name="od_ssd_fwd")
    y_d = conf_forward(glu3, *conf_p, name="od_conf_fwd")
    ys = [y_c.reshape(t, -1), y_d.reshape(t, -1)]
    saved = dict(z=zz3, xbc=xbc3, dtr=dtr3, glu=glu3, xconv=xconv, stp=stp, ssd_p=ssd_p, conf_p=conf_p)
    return ys, saved, None


def _mix_odd_bwd(dys, saved, wtl, nb, ns, lru_rider=None, attention_rider=None):
    assert lru_rider is None and attention_rider is None
    t = nb * ns
    dy_c, dy_d = [d.reshape(nb, ns, -1) for d in dys]
    outs = ssd_backward(dy_c, saved["xbc"], saved["xconv"], saved["dtr"], saved["z"], saved["stp"], *saved["ssd_p"],
                        name="od_ssd_bwd")
    dxbc, ddt, dz, dcw, dcb, ddbr, ddbc, dalr, dalc, ddsk, dgain = outs
    dglu, ccw, ccb, clg, clb = conf_backward(dy_d, saved["glu"], *saved["conf_p"], name="od_conf_bwd")
    w_in_t = wtl["w_in"]
    s0, s1, s2 = ODD_SPLITS
    carried = None
    gs = [d.reshape(t, -1) for d in (dz, dxbc, dglu, ddt)]
    wt_al = jnp.concatenate([w_in_t[:s1], w_in_t[s2:], _pad_to(w_in_t[s1:s2], LANES, 0)], axis=0)
    widths = (s0, s1 - s0, w_in_t.shape[0] - s2, LANES)
    starts = (0, s0, s1, s1 + widths[2])
    wts = [Part(wt_al, a, n, 0) for a, n in zip(starts, widths)]
    nh = SSM_HEADS
    grads = {
        "od_ssm_conv_w": dcw[:SSM_CONV][None], "od_ssm_conv_b": dcb,
        "od_ssm_dt_bias": ddbr[:, :nh] + ddbc[:nh, 0][None], "od_ssm_a_log": dalr[:, :nh] + dalc[:nh, 0][None],
        "od_ssm_d": ddsk[:, :nh], "od_ssm_norm": dgain,
        "od_cm_conv_w": ccw[:CONF_KERNEL][None], "od_cm_conv_b": ccb, "od_cm_ln_g": clg, "od_cm_ln_b": clb,
    }
    return gs, wts, grads, carried


LAYER_MATRICES = ("w_in", "w_out", "mlp_w1", "mlp_w2", "ple_w_proj", "ple_w_gate")
NORM_NAMES = ("norm_mix_pre", "norm_mix_post", "norm_mlp_pre", "norm_mlp_post", "norm_ple")


class NoOverlap:
    sums, from_chips = {}, {}

    def attention_fwd_rider(self):
        return None

    def weights_arrived(self, carried, wl, wtl):
        raise NotImplementedError

    def mlp_bwd_rider(self, layer1_grads):
        return None

    def after_mlp_bwd(self, carried):
        pass

    def lru_bwd_rider(self, layer0_grads):
        return None

    def attention_bwd_rider(self, carried):
        return None

    def after_attention_bwd(self, carried):
        pass


OUT_SPLIT = (LRU_WIDTH, SSM_WIDTH)


def local_step(x, p, target, w, wl, wtl, comm=NoOverlap()):
    nb, ns, d = x.shape
    t = nb * ns
    h = x.reshape(t, d)
    depth = p.shape[0]
    wl, wtl = list(wl), list(wtl)
    tapes = []
    for i in range(depth):
        even = i % 2 == 0
        tag = f"l{i}_"
        gpre = _row(w["norm_mix_pre"][i])
        rider = comm.attention_fwd_rider() if i == 0 else None
        ys, saved, carried = (_mix_even_fwd if even else _mix_odd_fwd)(h, gpre, w, wl[i], nb, ns, rider)
        if rider is not None:
            wl, wtl = comm.weights_arrived(carried, wl, wtl)
        w_out = wl[i]["w_out"]
        split = OUT_SPLIT[i % 2]
        w_outs = [Part(w_out, 0, split, 0), Part(w_out, split, w_out.shape[0] - split, 0)]
        h1, m = matmul_residual_norm(ys, w_outs, h, _row(w["norm_mix_post"][i]), name=tag + "out_proj")
        a1, = norm_matmul(h1, _row(w["norm_mlp_pre"][i]), [wl[i]["mlp_w1"]], [MXU_DTYPE], name=tag + "mlp_up")
        h2, f = matmul_residual_norm([a1], [wl[i]["mlp_w2"]], h1, _row(w["norm_mlp_post"][i]), name=tag + "mlp_down",
                                     relu2=True)
        pi = p[i].reshape(t, -1)
        h3, gl, emb = ple_forward(h2, pi, wl[i]["ple_w_gate"], wl[i]["ple_w_proj"], _row(w["norm_ple"][i]),
                                  name=tag + "ple")
        tapes.append(dict(h=h, ys=ys, w_outs=w_outs, saved=saved, h1=h1, m=m, a1=a1, h2=h2, f=f, pi=pi, gl=gl,
                          emb=emb))
        h = h3

    loss_row, dh = loss_and_grad(h, target.reshape(t, d), name="loss")
    grads = {}
    norm_grads = {k: [None] * depth for k in NORM_NAMES}
    layer_grads = [None] * depth
    for i in reversed(range(depth)):
        even = i % 2 == 0
        tag = f"l{i}_"
        tp = tapes[i]
        lg = {}
        to_sibling = comm.mlp_bwd_rider(layer_grads[1]) if i == 0 else None
        dh2, dgl, demb, dg = ple_backward(dh, tp["h2"], tp["gl"], tp["emb"], _row(w["norm_ple"][i]),
                                          wtl[i]["ple_w_gate"], name=tag + "ple_bwd")
        norm_grads["norm_ple"][i] = dg
        lg["ple_w_gate"] = weight_grad(tp["h2"], dgl, name=tag + "dw_gate")
        lg["ple_w_proj"] = weight_grad(tp["pi"], demb, name=tag + "dw_proj")
        outs = bwd_through_norm_out(dh2, tp["f"], _row(w["norm_mlp_post"][i]), [wtl[i]["mlp_w2"]], [MXU_DTYPE],
                                    name=tag + "mlp_down_bwd", relu2_of=tp["a1"], rider=to_sibling)
        d_f, (da1,), dg = outs[:3]
        if to_sibling is not None:
            comm.after_mlp_bwd(outs[3])
        norm_grads["norm_mlp_post"][i] = dg
        lg["mlp_w2"] = weight_grad(tp["a1"], d_f, name=tag + "dw2", prologue="relu2")
        gpre = _row(w["norm_mlp_pre"][i])
        dh1, dg = bwd_through_norm_in(dh2, [da1], [wtl[i]["mlp_w1"]], tp["h1"], gpre, name=tag + "mlp_up_bwd")
        norm_grads["norm_mlp_pre"][i] = dg
        lg["mlp_w1"] = weight_grad(tp["h1"], da1, name=tag + "dw1", prologue="rms", gain=gpre)
        wt_out = wtl[i]["w_out"]
        split = tp["w_outs"][0].shape[0]
        dm, dys, dg = bwd_through_norm_out(dh1, tp["m"], _row(w["norm_mix_post"][i]),
                                           [Part(wt_out, 0, split, 1),
                                            Part(wt_out, split, wt_out.shape[1] - split, 1)], [f32, f32],
                                           name=tag + "out_proj_bwd")
        norm_grads["norm_mix_post"][i] = dg
        lg["w_out"] = jnp.concatenate([weight_grad(y, dm, name=tag + f"dw_out{k}") for k, y in enumerate(tp["ys"])],
                                      axis=0)
        lru_rider = comm.lru_bwd_rider(lg) if i == 0 else None
        gs, wts, mix_grads, carried = (_mix_even_bwd if even else _mix_odd_bwd)(
            dys, tp["saved"], wtl[i], nb, ns, lru_rider, comm.attention_bwd_rider if lru_rider is not None else None)
        if carried is not None:
            comm.after_attention_bwd(carried)
        grads.update(mix_grads)
        gpre = _row(w["norm_mix_pre"][i])
        dh, dg = bwd_through_norm_in(dh1, gs, wts, tp["h"], gpre, name=tag + "in_proj_bwd")
        norm_grads["norm_mix_pre"][i] = dg
        dw_in = [weight_grad(tp["h"], g, name=tag + f"dw_in{k}", prologue="rms", gain=gpre) for k, g in enumerate(gs)]
        if not even:
            dw_in = [dw_in[0], dw_in[1], dw_in[3][:, :SSM_HEADS], dw_in[2]]
        lg["w_in"] = jnp.concatenate(dw_in, axis=1)
        layer_grads[i] = lg
    for k, v in norm_grads.items():
        grads[k] = jnp.concatenate(v, axis=0)
    return loss_row[0, 0], dh.reshape(nb, ns, d), grads, layer_grads


MESH_ID = pl.DeviceIdType.MESH
ANY = pl.BlockSpec(memory_space=pl.ANY)


def _mesh_pos():
    return lax.axis_index("x"), lax.axis_index("y"), lax.axis_index("c")


def all_gather(shards, name):
    return _run_alone(gather_rider(shards), name)


class Rider:
    def __init__(self, inputs, out_shapes, scratch_shapes, start, finish, middle=None):
        self.inputs, self.out_shapes, self.scratch_shapes = list(inputs), list(out_shapes), list(scratch_shapes)
        self.start, self.finish, self.middle = start, finish, middle


def _run_alone(rider, name):
    ni, no = len(rider.inputs), len(rider.out_shapes)

    def body(*refs):
        args = (refs[:ni], refs[ni:ni + no], refs[ni + no:])
        rider.start(*args)
        if rider.middle is not None:
            rider.middle(*args)
        rider.finish(*args)

    return pl.pallas_call(
        body, name=name, out_shape=rider.out_shapes, in_specs=[ANY] * ni, out_specs=[ANY] * no,
        scratch_shapes=rider.scratch_shapes,
    )(*rider.inputs)


def _ride(rider, body, in_specs, out_specs, out_shape, scratch_shapes, grid):
    in_specs, out_specs, out_shape = list(in_specs), list(out_specs), list(out_shape)
    scratch_shapes = list(scratch_shapes)
    if rider is None:
        return body, in_specs, out_specs, out_shape, scratch_shapes
    n_in, n_out, n_scr = len(in_specs), len(out_specs), len(scratch_shapes)
    ri, ro = len(rider.inputs), len(rider.out_shapes)
    total = math.prod(grid)

    def carrying(*refs):
        ins, r_ins = refs[:n_in], refs[n_in:n_in + ri]
        o0 = n_in + ri
        outs, r_outs = refs[o0:o0 + n_out], refs[o0 + n_out:o0 + n_out + ro]
        s0 = o0 + n_out + ro
        scr, r_scr = refs[s0:s0 + n_scr], refs[s0 + n_scr:]
        step = pl.program_id(0)
        for ax in range(1, len(grid)):
            step = step * grid[ax] + pl.program_id(ax)
        args = (r_ins, r_outs, r_scr)
        pl.when(step == 0)(lambda: rider.start(*args))
        if rider.middle is not None:
            pl.when(step == total // 2)(lambda: rider.middle(*args))
        body(*ins, *outs, *scr)
        pl.when(step == total - 1)(lambda: rider.finish(*args))

    return (carrying, in_specs + [ANY] * ri, out_specs + [ANY] * ro, out_shape + rider.out_shapes,
            scratch_shapes + rider.scratch_shapes)


def gather_rider(shards):
    n = len(shards)

    def parts(x_refs, out_refs, scr):
        send_sems, recv_sems, local_sems = scr
        x, y, c = _mesh_pos()
        chips = [(1 - x, y), (x, 1 - y), (1 - x, 1 - y)]

        def slot(a, px, py, pc):
            return out_refs[a].at[4 * px + 2 * py + pc]

        def copy(a, k, block, to, src=None):
            return pltpu.make_async_remote_copy(
                src_ref=slot(a, *block) if src is None else src, dst_ref=slot(a, *block),
                send_sem=send_sems.at[7 * a + k], recv_sem=recv_sems.at[7 * a + k], device_id=to,
                device_id_type=MESH_ID)

        me, sibling = (x, y, c), (x, y, 1 - c)
        def mine():
            return [pltpu.make_async_copy(x_refs[a], slot(a, *me), local_sems.at[a]) for a in range(n)]

        def first():
            out = []
            for j, chip in enumerate(chips):
                out += [copy(a, 1 + j, me, (*chip, c), src=x_refs[a]) for a in range(n)]
            return out + [copy(a, 0, me, sibling, src=x_refs[a]) for a in range(n)]

        def passed(j):
            return [copy(a, 4 + j, (*chips[j], c), sibling) for a in range(n)]

        return me, sibling, chips, c, copy, mine, first, passed

    def start(x_refs, out_refs, scr):
        _, _, _, _, _, mine, first, _ = parts(x_refs, out_refs, scr)
        for cp in mine() + first():
            cp.start()

    def middle(x_refs, out_refs, scr):
        me, _, chips, c, copy, _, _, passed = parts(x_refs, out_refs, scr)
        for j, chip in enumerate(chips):
            for a, fwd in enumerate(passed(j)):
                copy(a, 1 + j, (*chip, c), me).wait_recv()
                fwd.start()

    def finish(x_refs, out_refs, scr):
        me, sibling, chips, c, copy, mine, first, passed = parts(x_refs, out_refs, scr)
        for a in range(n):
            copy(a, 0, sibling, me).wait_recv()
        for j, chip in enumerate(chips):
            for a in range(n):
                copy(a, 4 + j, (*chip, 1 - c), me).wait_recv()
        for cp in first() + [cp for j in range(len(chips)) for cp in passed(j)]:
            cp.wait_send()
        for cp in mine():
            cp.wait()

    return Rider(shards, [jax.ShapeDtypeStruct((N_DEV,) + s.shape, s.dtype) for s in shards],
                 [pltpu.SemaphoreType.DMA((7 * n,)), pltpu.SemaphoreType.DMA((7 * n,)), pltpu.SemaphoreType.DMA((n,))],
                 start, finish, middle)


def scatter_to_sibling(parts, name):
    return _run_alone(sibling_rider(parts), name)


def sibling_rider(parts):
    n = len(parts)

    def copies(g_refs, out_refs, scr):
        send_sems, recv_sems = scr
        x, y, c = _mesh_pos()
        return [pltpu.make_async_remote_copy(
            src_ref=g_refs[a].at[2 * chip + (1 - c)], dst_ref=out_refs[a].at[chip],
            send_sem=send_sems.at[4 * a + chip], recv_sem=recv_sems.at[4 * a + chip], device_id=(x, y, 1 - c),
            device_id_type=MESH_ID) for a in range(n) for chip in range(4)]

    def start(*refs):
        for cp in copies(*refs):
            cp.start()

    def finish(*refs):
        cps = copies(*refs)
        for cp in cps:
            cp.wait_recv()
        for cp in cps:
            cp.wait_send()

    return Rider(parts, [jax.ShapeDtypeStruct((4,) + p.shape[1:], p.dtype) for p in parts],
                 [pltpu.SemaphoreType.DMA((4 * n,)), pltpu.SemaphoreType.DMA((4 * n,))], start, finish)


def scatter_to_chips(partials, name):
    return _run_alone(chips_rider(partials), name)


def chips_rider(partials):
    n = len(partials)

    def copies(p_refs, out_refs, scr):
        send_sems, recv_sems = scr
        x, y, c = _mesh_pos()
        chips = [(1 - x, y), (x, 1 - y), (1 - x, 1 - y)]
        return [pltpu.make_async_remote_copy(
            src_ref=p_refs[a].at[2 * px + py], dst_ref=out_refs[a].at[j],
            send_sem=send_sems.at[3 * a + j], recv_sem=recv_sems.at[3 * a + j], device_id=(px, py, c),
            device_id_type=MESH_ID) for a in range(n) for j, (px, py) in enumerate(chips)]

    def start(*refs):
        for cp in copies(*refs):
            cp.start()

    def finish(*refs):
        cps = copies(*refs)
        for cp in cps:
            cp.wait_recv()
        for cp in cps:
            cp.wait_send()

    return Rider(partials, [jax.ShapeDtypeStruct((3,) + p.shape[1:], p.dtype) for p in partials],
                 [pltpu.SemaphoreType.DMA((3 * n,)), pltpu.SemaphoreType.DMA((3 * n,))], start, finish)


ICI_DTYPE = jnp.bfloat16
ELEMENTWISE_BLOCK_BYTES = 1 << 20


def _row_tile(rows, cols):
    cap = max(16, ELEMENTWISE_BLOCK_BYTES // (4 * cols))
    best = [t for t in range(16, min(rows, cap) + 1, 16) if rows % t == 0]
    return best[-1] if best else rows


def add_sibling_parts(parts, received, core, name):
    _, r, n = parts.shape
    tr = _row_tile(r, n)

    def body(c_ref, a_ref, b_ref, o_ref, ob_ref):
        s = a_ref[...] + b_ref[...]
        o_ref[...] = s
        ob_ref[...] = s.astype(ob_ref.dtype)

    blk = pl.BlockSpec((1, tr, n), lambda i, j, c_ref: (i, j, 0))
    return pl.pallas_call(
        body, name=name,
        grid_spec=pltpu.PrefetchScalarGridSpec(
            num_scalar_prefetch=1, grid=(4, r // tr),
            in_specs=[pl.BlockSpec((1, tr, n), lambda i, j, c_ref: (2 * i + c_ref[0], j, 0)), blk],
            out_specs=[blk, blk]),
        out_shape=[jax.ShapeDtypeStruct((4, r, n), f32), jax.ShapeDtypeStruct((4, r, n), ICI_DTYPE)],
        compiler_params=_cparams("parallel", "parallel"),
    )(core, parts, received)


def _adamw(w, g, m, v):
    m = ADAM_B1 * m + (1.0 - ADAM_B1) * g
    v = ADAM_B2 * v + (1.0 - ADAM_B2) * jnp.square(g)
    m_hat = m / (1.0 - ADAM_B1 ** ADAM_STEP)
    v_hat = v / (1.0 - ADAM_B2 ** ADAM_STEP)
    delta = -ADAM_LR * (m_hat / (jnp.sqrt(v_hat) + ADAM_EPS) + ADAM_WD * w)
    return delta, m, v


def adamw_sharded(partial, received, chip, w, m, v, name):
    _, r, n = partial.shape

    def body(k_ref, p_ref, r_ref, w_ref, m_ref, v_ref, g_out, d_out, m_out, v_out):
        g = p_ref[0] + r_ref[0].astype(f32)
        g = g + r_ref[1].astype(f32)
        g = g + r_ref[2].astype(f32)
        delta, mn, vn = _adamw(w_ref[...], g, m_ref[...], v_ref[...])
        g_out[...] = g
        d_out[...] = delta
        m_out[...] = mn
        v_out[...] = vn

    tr = _row_tile(r, n)
    flat = pl.BlockSpec((tr, n), lambda j, k_ref: (j, 0))
    return pl.pallas_call(
        body, name=name,
        grid_spec=pltpu.PrefetchScalarGridSpec(
            num_scalar_prefetch=1, grid=(r // tr,),
            in_specs=[pl.BlockSpec((1, tr, n), lambda j, k_ref: (k_ref[0], j, 0)),
                      pl.BlockSpec((3, tr, n), lambda j, k_ref: (0, j, 0)), flat, flat, flat],
            out_specs=[flat] * 4),
        out_shape=[jax.ShapeDtypeStruct((r, n), f32)] * 4,
        compiler_params=_cparams("parallel"),
    )(chip, partial, received, w, m, v)


def adamw_replicated(gathered, w, m, v, name):
    _, r, n = gathered.shape

    def body(g_ref, w_ref, m_ref, v_ref, g_out, d_out, m_out, v_out):
        g = g_ref[0]
        for k in range(1, N_DEV):
            g = g + g_ref[k]
        delta, mn, vn = _adamw(w_ref[...], g, m_ref[...], v_ref[...])
        g_out[...] = g
        d_out[...] = delta
        m_out[...] = mn
        v_out[...] = vn

    return pl.pallas_call(
        body, name=name,
        out_shape=[jax.ShapeDtypeStruct((r, n), f32)] * 4,
        compiler_params=pltpu.CompilerParams(vmem_limit_bytes=VMEM_LIMIT),
    )(gathered, w, m, v)


W_NAMES = ['ev_w_in', 'ev_lru_conv_w', 'ev_lru_conv_b', 'ev_lru_gate_a_w', 'ev_lru_gate_a_b', 'ev_lru_gate_x_w',
           'ev_lru_gate_x_b', 'ev_lru_lambda', 'ev_w_out', 'od_w_in', 'od_ssm_conv_w', 'od_ssm_conv_b',
           'od_ssm_dt_bias', 'od_ssm_a_log', 'od_ssm_d', 'od_ssm_norm', 'od_cm_conv_w', 'od_cm_conv_b', 'od_cm_ln_g',
           'od_cm_ln_b', 'od_w_out', 'norm_mix_pre', 'norm_mix_post', 'norm_mlp_pre', 'norm_mlp_post', 'norm_ple',
           'mlp_w1', 'mlp_w2', 'ple_w_proj', 'ple_w_gate']
BIG_SHARDED = {'ev_w_in': 2, 'ev_w_out': 1, 'od_w_in': 2, 'od_w_out': 1, 'mlp_w1': 2, 'mlp_w2': 1, 'ple_w_proj': 2,
               'ple_w_gate': 1}
SMALL_SHARDED = {'ev_lru_conv_w': 2, 'od_ssm_conv_w': 2, 'od_ssm_conv_b': 1, 'od_ssm_norm': 1, 'od_cm_conv_w': 2,
                 'od_cm_conv_b': 1, 'od_cm_ln_g': 1, 'od_cm_ln_b': 1}
SHARDED = {**BIG_SHARDED, **SMALL_SHARDED}
REPLICATED = [n for n in W_NAMES if n not in SHARDED]


def _round_up(n, k):
    return -(-n // k) * k


def _pack_rows(flat, rows_multiple):
    n = flat.shape[0]
    total = _round_up(n, LANES * rows_multiple)
    return jnp.pad(flat, (0, total - n)).reshape(-1, LANES)


def _unpack(flat, shapes):
    out, off = {}, 0
    for name, shape in shapes.items():
        size = math.prod(shape)
        out[name] = flat[off:off + size].reshape(shape)
        off += size
    return out


def _rows(a):
    return a.reshape(-1, a.shape[-1])


def _unshard(g8, shape, axis):
    g = jnp.moveaxis(g8.reshape((N_DEV,) + tuple(shape)), 0, axis)
    return g.reshape(tuple(shape[:axis]) + (N_DEV * shape[axis],) + tuple(shape[axis + 1:]))


def _to_shards(g, axis):
    shard = g.shape[axis] // N_DEV
    g = g.reshape(g.shape[:axis] + (N_DEV, shard) + g.shape[axis + 1:])
    return jnp.moveaxis(g, axis, 0)


def _pack_small(tree):
    return _pack_rows(jnp.concatenate([tree[k].astype(f32).reshape(-1) for k in SMALL_SHARDED]), 16)


def _layer_entry(i, key):
    mixer = "ev" if i % 2 == 0 else "od"
    return {"w_in": (mixer + "_w_in", i // 2, 1), "w_out": (mixer + "_w_out", i // 2, 0),
            "mlp_w1": ("mlp_w1", i, 1), "mlp_w2": ("mlp_w2", i, 0),
            "ple_w_proj": ("ple_w_proj", i, 1), "ple_w_gate": ("ple_w_gate", i, 0)}[key]


def _layer_shards(tree, i):
    out = {}
    for key in LAYER_MATRICES:
        name, idx, _ = _layer_entry(i, key)
        out[key] = tree[name][idx]
    return out


def _assemble_layer(i, gathered):
    wl = {key: _unshard(g8, g8.shape[1:], _layer_entry(i, key)[2]) for key, g8 in zip(LAYER_MATRICES, gathered)}
    return wl, {key: v.T for key, v in wl.items()}


def _gather_early(w):
    small = _pack_small(w)
    terms, rest = [], small
    for _ in range(3):
        term = rest.astype(MXU_DTYPE)
        terms.append(term)
        rest = rest - term.astype(f32)
    outs = all_gather([_layer_shards(w, 0)["w_in"].astype(MXU_DTYPE), jnp.concatenate(terms, axis=0)],
                      name="gather_weights")
    w_in = _unshard(outs[0], outs[0].shape[1:], _layer_entry(0, "w_in")[2])
    wl, wtl = {"w_in": w_in}, {"w_in": w_in.T}
    full = {k: w[k] for k in REPLICATED}
    t = outs[-1].astype(f32)
    nr = small.shape[0]
    vals = (t[:, :nr] + t[:, nr:2 * nr] + t[:, 2 * nr:]).reshape(N_DEV, -1)
    off = 0
    for k, axis in SMALL_SHARDED.items():
        size = math.prod(w[k].shape)
        full[k] = _unshard(vals[:, off:off + size], w[k].shape, axis)
        off += size
    return full, wl, wtl


class Overlap:
    LATE = [(0, key) for key in LAYER_MATRICES if key != "w_in"] + [(1, key) for key in LAYER_MATRICES]
    EARLY_GRADS = [(0, key) for key in LAYER_MATRICES if key != "w_in"]

    def __init__(self, w):
        self.w = w
        self.sums, self.from_chips, self.parts = {}, {}, {}

    def attention_fwd_rider(self):
        shards = [_layer_shards(self.w, 0), _layer_shards(self.w, 1)]
        return gather_rider([shards[i][key].astype(MXU_DTYPE) for i, key in self.LATE])

    def weights_arrived(self, carried, wl, wtl):
        wl = [dict(wl[0]), {}]
        wtl = [dict(wtl[0]), {}]
        for (i, key), g8 in zip(self.LATE, carried):
            wl[i][key] = _unshard(g8, g8.shape[1:], _layer_entry(i, key)[2])
            wtl[i][key] = wl[i][key].T
        return wl, wtl

    def _to_sibling(self, ids, layer_grads):
        for i, key in ids:
            self.parts[i, key] = _to_shards(layer_grads[key], _layer_entry(i, key)[2])
        return sibling_rider([self.parts[e] for e in ids])

    def _add(self, ids, carried):
        core = jnp.reshape(lax.axis_index("c"), (1,)).astype(jnp.int32)
        for (i, key), got in zip(ids, carried):
            self.sums[i, key] = add_sibling_parts(self.parts[i, key], got, core, name=f"add_sibling_l{i}_{key}")

    def mlp_bwd_rider(self, layer1_grads):
        return self._to_sibling([(1, key) for key in LAYER_MATRICES], layer1_grads)

    def after_mlp_bwd(self, carried):
        self._add([(1, key) for key in LAYER_MATRICES], carried)

    def lru_bwd_rider(self, layer0_grads):
        return self._to_sibling(self.EARLY_GRADS, layer0_grads)

    def attention_bwd_rider(self, carried):
        self._add(self.EARLY_GRADS, carried)
        self.travelling = list(self.sums)
        return chips_rider([self.sums[e][1] for e in self.travelling])

    def after_attention_bwd(self, carried):
        for e, got in zip(self.travelling, carried):
            self.from_chips[e] = got


def _pack_replicated(tree):
    return _pack_rows(jnp.concatenate([tree[k].astype(f32).reshape(-1) for k in REPLICATED]), 8)


def kernel(x, p, ev_w_in, ev_lru_conv_w, ev_lru_conv_b, ev_lru_gate_a_w, ev_lru_gate_a_b, ev_lru_gate_x_w, ev_lru_gate_x_b, ev_lru_lambda, ev_w_out, od_w_in, od_ssm_conv_w, od_ssm_conv_b, od_ssm_dt_bias, od_ssm_a_log, od_ssm_d, od_ssm_norm, od_cm_conv_w, od_cm_conv_b, od_cm_ln_g, od_cm_ln_b, od_w_out, norm_mix_pre, norm_mix_post, norm_mlp_pre, norm_mlp_post, norm_ple, mlp_w1, mlp_w2, ple_w_proj, ple_w_gate, loss_target, m_ev_w_in, m_ev_lru_conv_w, m_ev_lru_conv_b, m_ev_lru_gate_a_w, m_ev_lru_gate_a_b, m_ev_lru_gate_x_w, m_ev_lru_gate_x_b, m_ev_lru_lambda, m_ev_w_out, m_od_w_in, m_od_ssm_conv_w, m_od_ssm_conv_b, m_od_ssm_dt_bias, m_od_ssm_a_log, m_od_ssm_d, m_od_ssm_norm, m_od_cm_conv_w, m_od_cm_conv_b, m_od_cm_ln_g, m_od_cm_ln_b, m_od_w_out, m_norm_mix_pre, m_norm_mix_post, m_norm_mlp_pre, m_norm_mlp_post, m_norm_ple, m_mlp_w1, m_mlp_w2, m_ple_w_proj, m_ple_w_gate, v_ev_w_in, v_ev_lru_conv_w, v_ev_lru_conv_b, v_ev_lru_gate_a_w, v_ev_lru_gate_a_b, v_ev_lru_gate_x_w, v_ev_lru_gate_x_b, v_ev_lru_lambda, v_ev_w_out, v_od_w_in, v_od_ssm_conv_w, v_od_ssm_conv_b, v_od_ssm_dt_bias, v_od_ssm_a_log, v_od_ssm_d, v_od_ssm_norm, v_od_cm_conv_w, v_od_cm_conv_b, v_od_cm_ln_g, v_od_cm_ln_b, v_od_w_out, v_norm_mix_pre, v_norm_mix_post, v_norm_mlp_pre, v_norm_mlp_post, v_norm_ple, v_mlp_w1, v_mlp_w2, v_ple_w_proj, v_ple_w_gate):
    ws = [ev_w_in, ev_lru_conv_w, ev_lru_conv_b, ev_lru_gate_a_w, ev_lru_gate_a_b, ev_lru_gate_x_w, ev_lru_gate_x_b, ev_lru_lambda, ev_w_out, od_w_in, od_ssm_conv_w, od_ssm_conv_b, od_ssm_dt_bias, od_ssm_a_log, od_ssm_d, od_ssm_norm, od_cm_conv_w, od_cm_conv_b, od_cm_ln_g, od_cm_ln_b, od_w_out, norm_mix_pre, norm_mix_post, norm_mlp_pre, norm_mlp_post, norm_ple, mlp_w1, mlp_w2, ple_w_proj, ple_w_gate]
    ms = [m_ev_w_in, m_ev_lru_conv_w, m_ev_lru_conv_b, m_ev_lru_gate_a_w, m_ev_lru_gate_a_b, m_ev_lru_gate_x_w, m_ev_lru_gate_x_b, m_ev_lru_lambda, m_ev_w_out, m_od_w_in, m_od_ssm_conv_w, m_od_ssm_conv_b, m_od_ssm_dt_bias, m_od_ssm_a_log, m_od_ssm_d, m_od_ssm_norm, m_od_cm_conv_w, m_od_cm_conv_b, m_od_cm_ln_g, m_od_cm_ln_b, m_od_w_out, m_norm_mix_pre, m_norm_mix_post, m_norm_mlp_pre, m_norm_mlp_post, m_norm_ple, m_mlp_w1, m_mlp_w2, m_ple_w_proj, m_ple_w_gate]
    vs = [v_ev_w_in, v_ev_lru_conv_w, v_ev_lru_conv_b, v_ev_lru_gate_a_w, v_ev_lru_gate_a_b, v_ev_lru_gate_x_w, v_ev_lru_gate_x_b, v_ev_lru_lambda, v_ev_w_out, v_od_w_in, v_od_ssm_conv_w, v_od_ssm_conv_b, v_od_ssm_dt_bias, v_od_ssm_a_log, v_od_ssm_d, v_od_ssm_norm, v_od_cm_conv_w, v_od_cm_conv_b, v_od_cm_ln_g, v_od_cm_ln_b, v_od_w_out, v_norm_mix_pre, v_norm_mix_post, v_norm_mlp_pre, v_norm_mlp_post, v_norm_ple, v_mlp_w1, v_mlp_w2, v_ple_w_proj, v_ple_w_gate]
    w = dict(zip(W_NAMES, ws))
    m = dict(zip(W_NAMES, ms))
    v = dict(zip(W_NAMES, vs))
    full, wl0, wtl0 = _gather_early(w)
    comm = Overlap(w)
    loss_local, grad_x, grads, layer_grads = local_step(x, p, loss_target, full, [wl0, None], [wtl0, None], comm)
    loss = lax.psum(loss_local, ("x", "y", "c"))
    return (loss, grad_x, *_reduce_and_update(grads, layer_grads, w, m, v, comm))


def _reduce_and_update(grads, layer_grads, w, m, v, comm):
    mx, my, mc = _mesh_pos()

    entries = [(i, key) for i in range(len(layer_grads)) for key in LAYER_MATRICES]
    left = [e for e in entries if e not in comm.from_chips]
    parts = [_to_shards(layer_grads[i][key], _layer_entry(i, key)[2]) for i, key in left]
    small = jnp.concatenate([_to_shards(grads[k], axis).reshape(N_DEV, -1) for k, axis in SMALL_SHARDED.items()],
                            axis=1)
    small_rows = _pack_small(w).shape[0]
    small = jnp.pad(small, ((0, 0), (0, small_rows * LANES - small.shape[1]))).reshape(N_DEV, small_rows, LANES)
    parts.append(small)
    from_sibling = scatter_to_sibling(parts, name="scatter_sibling")
    core = jnp.reshape(mc, (1,)).astype(jnp.int32)
    sums = [add_sibling_parts(a, b, core, name=f"add_sibling_{i}") for i, (a, b) in enumerate(zip(parts, from_sibling))]
    from_chips = scatter_to_chips([s[1] for s in sums], name="scatter_chips")
    all_sums = {**comm.sums, **dict(zip(left, sums[:-1]))}
    all_from_chips = {**comm.from_chips, **dict(zip(left, from_chips[:-1]))}
    chip = jnp.reshape(2 * mx + my, (1,)).astype(jnp.int32)
    per_layer = []
    for i in range(len(layer_grads)):
        ws, ms, vs = _layer_shards(w, i), _layer_shards(m, i), _layer_shards(v, i)
        per_layer.append({key: adamw_sharded(all_sums[i, key][0], all_from_chips[i, key], chip, ws[key], ms[key],
                                             vs[key], name=f"adamw_l{i}_{key}") for key in LAYER_MATRICES})
    g_sh, d_sh, m_sh, v_sh = {}, {}, {}, {}
    for which, tree in enumerate((g_sh, d_sh, m_sh, v_sh)):
        for i in range(len(per_layer)):
            for key in LAYER_MATRICES:
                name, idx, _ = _layer_entry(i, key)
                tree.setdefault(name, {})[idx] = per_layer[i][key][which]
        for name in BIG_SHARDED:
            tree[name] = jnp.stack([tree[name][idx] for idx in sorted(tree[name])], axis=0)
    outs = adamw_sharded(sums[-1][0], from_chips[-1], chip, _pack_small(w), _pack_small(m), _pack_small(v),
                         name="adamw_small")
    small_shapes = {k: w[k].shape for k in SMALL_SHARDED}
    for tree, o in zip((g_sh, d_sh, m_sh, v_sh), outs):
        tree.update(_unpack(o.reshape(-1), small_shapes))

    rep_parts, = all_gather([_pack_replicated(grads)], name="gather_replicated_grads")
    outs = adamw_replicated(rep_parts, _pack_replicated(w), _pack_replicated(m), _pack_replicated(v),
                            name="adamw_replicated")
    rep_shapes = {k: w[k].shape for k in REPLICATED}
    g_rp, d_rp, m_rp, v_rp = [_unpack(o.reshape(-1), rep_shapes) for o in outs]

    pick = lambda sh, rp: [sh[k] if k in SHARDED else rp[k] for k in W_NAMES]
    return [*pick(g_sh, g_rp), *pick(d_sh, d_rp), *pick(m_sh, m_rp), *pick(v_sh, v_rp)]
```

```python
import functools
import math

import jax
import jax.numpy as jnp
from jax import lax
from jax.experimental import pallas as pl
from jax.experimental.pallas import tpu as pltpu

f32 = jnp.float32
bf16 = jnp.bfloat16
MXU_DTYPE = jnp.bfloat16

D_MODEL = 1024
EPS = 1e-6
LRU_WIDTH = 512
LRU_HEADS = 8
LRU_CONV = 4
LRU_C = 8.0
SB_WIDTH = 512
SB_HEAD_DIM = 64
SSM_WIDTH = 1024
SSM_HEADS = 16
SSM_HEAD_DIM = 64
SSM_GROUPS = 2
SSM_STATE = 128
SSM_CONV = 4
SSM_CHUNK = 128
SSM_XBC = SSM_WIDTH + 2 * SSM_GROUPS * SSM_STATE
CONF_WIDTH = 512
CONF_KERNEL = 31
MLP_HIDDEN = 4096
PLE_DIM = 256
LANES = 128
N_DEV = 8

ADAM_LR = 0.001
ADAM_B1 = 0.9
ADAM_B2 = 0.999
ADAM_EPS = 1e-08
ADAM_WD = 0.01
ADAM_STEP = 10

VMEM_LIMIT = 56 * 1024 * 1024


def _cparams(*sem):
    return pltpu.CompilerParams(dimension_semantics=sem, vmem_limit_bytes=VMEM_LIMIT)


def _mm(a, b):
    return jnp.dot(a.astype(MXU_DTYPE), b.astype(MXU_DTYPE), preferred_element_type=f32)


def _mm_nt(a, b):
    return lax.dot_general(a.astype(MXU_DTYPE), b.astype(MXU_DTYPE), (((1,), (1,)), ((), ())),
                           preferred_element_type=f32)


def _mm_tn(a, b):
    return lax.dot_general(a.astype(MXU_DTYPE), b.astype(MXU_DTYPE), (((0,), (0,)), ((), ())),
                           preferred_element_type=f32)


def _mm_exact(a, b):
    return jnp.dot(a, b, preferred_element_type=f32, precision=lax.Precision.HIGHEST)


@jax.custom_vjp
def dmm(a, b):
    return _mm(a, b)


def _dmm_fwd(a, b):
    return _mm(a, b), (a, b)


def _dmm_bwd(res, g):
    a, b = res
    return _mm_nt(g, b), _mm_tn(a, g)


dmm.defvjp(_dmm_fwd, _dmm_bwd)


@jax.custom_vjp
def dmm_nt(a, b):
    return _mm_nt(a, b)


def _dmm_nt_fwd(a, b):
    return _mm_nt(a, b), (a, b)


def _dmm_nt_bwd(res, g):
    a, b = res
    return _mm(g, b), _mm_tn(g, a)


dmm_nt.defvjp(_dmm_nt_fwd, _dmm_nt_bwd)


@jax.custom_vjp
def dmm_tn(a, b):
    return _mm_tn(a, b)


def _dmm_tn_fwd(a, b):
    return _mm_tn(a, b), (a, b)


def _dmm_tn_bwd(res, g):
    a, b = res
    return _mm_nt(b, g), _mm(a, g)


dmm_tn.defvjp(_dmm_tn_fwd, _dmm_tn_bwd)


def _rms(x, g):
    r = lax.rsqrt(jnp.mean(x * x, axis=-1, keepdims=True) + EPS)
    return x * r * g


def _rms_bwd(dy, x, g):
    r = lax.rsqrt(jnp.mean(x * x, axis=-1, keepdims=True) + EPS)
    dyg = dy * g
    dx = r * dyg - x * (r * r * r * jnp.mean(dyg * x, axis=-1, keepdims=True))
    return dx, dy * x * r


def _tok(tm, n):
    return pl.BlockSpec((tm, n), lambda i: (i, 0))


def _whole(shape):
    nd = len(shape)
    return pl.BlockSpec(tuple(shape), lambda i: (0,) * nd)


def _acc_rows(ref, val):
    s = jnp.sum(val, axis=0, keepdims=True)

    @pl.when(pl.program_id(0) == 0)
    def _():
        ref[...] = s

    @pl.when(pl.program_id(0) != 0)
    def _():
        ref[...] += s


TOKEN_TILE = 512


class Part:
    def __init__(self, whole, start, size, axis):
        self.whole, self.start, self.size, self.axis = whole, start, size, axis
        self.shape = tuple(size if a == axis else n for a, n in enumerate(whole.shape))


def _weights(ws):
    wholes, readers = [], []
    for w in ws:
        arr = w.whole if isinstance(w, Part) else w
        idx = next((i for i, a in enumerate(wholes) if a is arr), None)
        if idx is None:
            wholes.append(arr)
            idx = len(wholes) - 1
        if isinstance(w, Part):
            rows = pl.ds(w.start, w.size) if w.axis == 0 else slice(None)
            cols = pl.ds(w.start, w.size) if w.axis == 1 else slice(None)
            readers.append(lambda refs, idx=idx, rows=rows, cols=cols: refs[idx][rows, cols])
        else:
            readers.append(lambda refs, idx=idx: refs[idx][...])
    return wholes, readers


def norm_matmul(h, g, ws, out_dtypes, name):
    t, d = h.shape
    tm = TOKEN_TILE
    wholes, readers = _weights(ws)
    nw = len(wholes)

    def body(h_ref, g_ref, *refs):
        hn = _rms(h_ref[...], g_ref[...]).astype(MXU_DTYPE)
        for read, o_ref in zip(readers, refs[nw:]):
            o_ref[...] = jnp.dot(hn, read(refs[:nw]), preferred_element_type=f32).astype(o_ref.dtype)

    return pl.pallas_call(
        body, name=name, grid=(t // tm,),
        in_specs=[_tok(tm, d), _whole(g.shape)] + [_whole(w.shape) for w in wholes],
        out_specs=[_tok(tm, w.shape[1]) for w in ws],
        out_shape=[jax.ShapeDtypeStruct((t, w.shape[1]), dt) for w, dt in zip(ws, out_dtypes)],
        compiler_params=_cparams("parallel"),
    )(h, g, *wholes)


def matmul_residual_norm(xs, ws, h, g, name, relu2=False):
    t, d = h.shape
    tm = TOKEN_TILE
    nx = len(xs)
    wholes, readers = _weights(ws)
    nw = len(wholes)

    def body(*refs):
        x_refs, w_refs = refs[:nx], refs[nx:nx + nw]
        h_ref, g_ref, ho_ref, m_ref = refs[nx + nw:]
        m = None
        for x_ref, read in zip(x_refs, readers):
            x = x_ref[...]
            if relu2:
                x = jnp.square(jnp.maximum(x.astype(f32), 0.0))
            part = jnp.dot(x.astype(MXU_DTYPE), read(w_refs), preferred_element_type=f32)
            m = part if m is None else m + part
        m_ref[...] = m.astype(m_ref.dtype)
        ho_ref[...] = h_ref[...] + _rms(m, g_ref[...])

    return pl.pallas_call(
        body, name=name, grid=(t // tm,),
        in_specs=[_tok(tm, x.shape[1]) for x in xs] + [_whole(w.shape) for w in wholes]
        + [_tok(tm, d), _whole(g.shape)],
        out_specs=[_tok(tm, d), _tok(tm, d)],
        out_shape=[jax.ShapeDtypeStruct((t, d), f32), jax.ShapeDtypeStruct((t, d), MXU_DTYPE)],
        compiler_params=_cparams("parallel"),
    )(*xs, *wholes, h, g)


def ple_forward(h, p, w_gate, w_proj, g, name):
    t, d = h.shape
    tm = TOKEN_TILE

    def body(h_ref, p_ref, wg_ref, wp_ref, g_ref, ho_ref, gl_ref, emb_ref):
        hh = h_ref[...]
        gl = jnp.dot(hh.astype(MXU_DTYPE), wg_ref[...], preferred_element_type=f32)
        emb = jnp.dot(p_ref[...].astype(MXU_DTYPE), wp_ref[...], preferred_element_type=f32)
        gl_ref[...] = gl.astype(gl_ref.dtype)
        emb_ref[...] = emb.astype(emb_ref.dtype)
        ho_ref[...] = hh + _rms(jax.nn.sigmoid(gl) * emb, g_ref[...])

    return pl.pallas_call(
        body, name=name, grid=(t // tm,),
        in_specs=[_tok(tm, d), _tok(tm, p.shape[1]), _whole(w_gate.shape), _whole(w_proj.shape), _whole(g.shape)],
        out_specs=[_tok(tm, d)] * 3,
        out_shape=[jax.ShapeDtypeStruct((t, d), f32)] + [jax.ShapeDtypeStruct((t, d), MXU_DTYPE)] * 2,
        compiler_params=_cparams("parallel"),
    )(h, p, w_gate, w_proj, g)


def loss_and_grad(h, target, name):
    t, d = h.shape
    tm = TOKEN_TILE

    def body(h_ref, t_ref, l_ref, dh_ref):
        e = h_ref[...] - t_ref[...]
        dh_ref[...] = e * (1.0 / d)
        part = jnp.sum(jnp.sum(e * e, axis=1, keepdims=True), axis=0, keepdims=True) * (0.5 / d)
        _acc_rows(l_ref, jnp.broadcast_to(part, (1, LANES)))

    return pl.pallas_call(
        body, name=name, grid=(t // tm,),
        in_specs=[_tok(tm, d), _tok(tm, d)],
        out_specs=[_whole((1, LANES)), _tok(tm, d)],
        out_shape=[jax.ShapeDtypeStruct((1, LANES), f32), jax.ShapeDtypeStruct((t, d), f32)],
        compiler_params=_cparams("arbitrary"),
    )(h, target)


def bwd_through_norm_in(dh, gs, wts, h, g, name):
    t, d = h.shape
    tm = TOKEN_TILE
    ng = len(gs)
    wholes, readers = _weights(wts)
    nw = len(wholes)

    def body(*refs):
        dh_ref = refs[0]
        g_refs, w_refs = refs[1:1 + ng], refs[1 + ng:1 + ng + nw]
        h_ref, gain_ref, dho_ref, dg_ref = refs[1 + ng + nw:]
        dhn = None
        for g_ref, read in zip(g_refs, readers):
            part = jnp.dot(g_ref[...].astype(MXU_DTYPE), read(w_refs), preferred_element_type=f32)
            dhn = part if dhn is None else dhn + part
        dx, dgr = _rms_bwd(dhn, h_ref[...], gain_ref[...])
        dho_ref[...] = dh_ref[...] + dx
        _acc_rows(dg_ref, dgr)

    return pl.pallas_call(
        body, name=name, grid=(t // tm,),
        in_specs=[_tok(tm, d)] + [_tok(tm, x.shape[1]) for x in gs] + [_whole(w.shape) for w in wholes]
        + [_tok(tm, d), _whole(g.shape)],
        out_specs=[_tok(tm, d), _whole((1, d))],
        out_shape=[jax.ShapeDtypeStruct((t, d), f32), jax.ShapeDtypeStruct((1, d), f32)],
        compiler_params=_cparams("arbitrary"),
    )(dh, *gs, *wholes, h, g)


def bwd_through_norm_out(dh, n, g, wts, out_dtypes, name, relu2_of=None, rider=None):
    t, d = n.shape
    tm = TOKEN_TILE
    nw = len(wts)
    wholes, readers = _weights(wts)
    nwh = len(wholes)
    has_a = relu2_of is not None

    def body(*refs):
        dh_ref, n_ref, gain_ref = refs[:3]
        w_refs = refs[3:3 + nwh]
        rest = refs[3 + nwh:]
        if has_a:
            a_ref, rest = rest[0], rest[1:]
        dn_ref, dx_refs, dg_ref = rest[0], rest[1:1 + nw], rest[1 + nw]
        dn, dgr = _rms_bwd(dh_ref[...], n_ref[...].astype(f32), gain_ref[...])
        dnb = dn.astype(MXU_DTYPE)
        dn_ref[...] = dnb.astype(dn_ref.dtype)
        for read, dx_ref in zip(readers, dx_refs):
            dx = jnp.dot(dnb, read(w_refs), preferred_element_type=f32)
            if has_a:
                dx = dx * (2.0 * jnp.maximum(a_ref[...].astype(f32), 0.0))
            dx_ref[...] = dx.astype(dx_ref.dtype)
        _acc_rows(dg_ref, dgr)

    ins = [dh, n, g, *wholes] + ([relu2_of] if has_a else [])
    in_specs = [_tok(tm, d), _tok(tm, d), _whole(g.shape)] + [_whole(w.shape) for w in wholes]
    if has_a:
        in_specs.append(_tok(tm, relu2_of.shape[1]))
    grid = (t // tm,)
    body, in_specs, out_specs, out_shape, scratch = _ride(
        rider, body, in_specs, [_tok(tm, d)] + [_tok(tm, w.shape[1]) for w in wts] + [_whole((1, d))],
        [jax.ShapeDtypeStruct((t, d), MXU_DTYPE)]
        + [jax.ShapeDtypeStruct((t, w.shape[1]), dt) for w, dt in zip(wts, out_dtypes)]
        + [jax.ShapeDtypeStruct((1, d), f32)], [], grid)
    outs = pl.pallas_call(
        body, name=name, grid=grid, in_specs=in_specs, out_specs=out_specs, out_shape=out_shape,
        scratch_shapes=scratch, compiler_params=_cparams("arbitrary"),
    )(*ins, *(rider.inputs if rider else []))
    if rider:
        return outs[0], list(outs[1:1 + nw]), outs[1 + nw], list(outs[2 + nw:])
    return outs[0], list(outs[1:1 + nw]), outs[1 + nw]


def ple_backward(dh3, h2, gl, emb, g, w_gate_t, name):
    t, d = h2.shape
    tm = TOKEN_TILE

    def body(dh_ref, gl_ref, emb_ref, gain_ref, wt_ref, dho_ref, dgl_ref, demb_ref, dg_ref):
        gate = jax.nn.sigmoid(gl_ref[...].astype(f32))
        emb = emb_ref[...].astype(f32)
        dge, dgr = _rms_bwd(dh_ref[...], gate * emb, gain_ref[...])
        demb_ref[...] = (dge * gate).astype(demb_ref.dtype)
        dgl = (dge * emb * gate * (1.0 - gate)).astype(MXU_DTYPE)
        dgl_ref[...] = dgl.astype(dgl_ref.dtype)
        dho_ref[...] = dh_ref[...] + jnp.dot(dgl, wt_ref[...], preferred_element_type=f32)
        _acc_rows(dg_ref, dgr)

    return pl.pallas_call(
        body, name=name, grid=(t // tm,),
        in_specs=[_tok(tm, d), _tok(tm, d), _tok(tm, d), _whole(g.shape), _whole(w_gate_t.shape)],
        out_specs=[_tok(tm, d), _tok(tm, d), _tok(tm, d), _whole((1, d))],
        out_shape=[jax.ShapeDtypeStruct((t, d), f32), jax.ShapeDtypeStruct((t, d), MXU_DTYPE),
                   jax.ShapeDtypeStruct((t, d), MXU_DTYPE), jax.ShapeDtypeStruct((1, d), f32)],
        compiler_params=_cparams("arbitrary"),
    )(dh3, gl, emb, g, w_gate_t)


def _largest_tile(n, cap):
    if n <= cap:
        return n
    return max(c for c in range(LANES, cap + 1, LANES) if n % c == 0)


def weight_grad(x, gout, name, prologue="none", gain=None):
    t, k = x.shape
    n = gout.shape[1]
    tt = 1024
    tn = _largest_tile(n, 1024)
    tk = k if prologue == "rms" else _largest_tile(k, 1024)
    has_gain = prologue == "rms"

    def body(*refs):
        if has_gain:
            x_ref, gain_ref, g_ref, o_ref = refs
        else:
            x_ref, g_ref, o_ref = refs
        x = x_ref[...].astype(f32)
        if prologue == "relu2":
            x = jnp.square(jnp.maximum(x, 0.0))
        elif prologue == "rms":
            x = _rms(x, gain_ref[...])
        part = _mm_tn(x, g_ref[...])

        @pl.when(pl.program_id(2) == 0)
        def _():
            o_ref[...] = part

        @pl.when(pl.program_id(2) != 0)
        def _():
            o_ref[...] += part

    in_specs = [pl.BlockSpec((tt, tk), lambda i, j, s: (s, i))]
    ins = [x]
    if has_gain:
        in_specs.append(pl.BlockSpec(gain.shape, lambda i, j, s: (0, 0)))
        ins.append(gain)
    in_specs.append(pl.BlockSpec((tt, tn), lambda i, j, s: (s, j)))
    ins.append(gout)
    return pl.pallas_call(
        body, name=name, grid=(k // tk, n // tn, t // tt),
        in_specs=in_specs,
        out_specs=pl.BlockSpec((tk, tn), lambda i, j, s: (i, j)),
        out_shape=jax.ShapeDtypeStruct((k, n), f32),
        compiler_params=_cparams("parallel", "parallel", "arbitrary"),
    )(*ins)


def weight_grads_of_norm(x, gain, gouts, name):
    t, k = x.shape
    tt = 512
    ng = len(gouts)

    def body(x_ref, gain_ref, *refs):
        xn = _rms(x_ref[...], gain_ref[...]).astype(MXU_DTYPE)
        first = pl.program_id(0) == 0
        for g_ref, o_ref in zip(refs[:ng], refs[ng:]):
            _accum(o_ref, _mm_tn(xn, g_ref[...]), first)

    return pl.pallas_call(
        body, name=name, grid=(t // tt,),
        in_specs=[_tok(tt, k), _whole(gain.shape)] + [_tok(tt, g.shape[1]) for g in gouts],
        out_specs=[_whole((k, g.shape[1])) for g in gouts],
        out_shape=[jax.ShapeDtypeStruct((k, g.shape[1]), f32) for g in gouts],
        compiler_params=_cparams("arbitrary"),
    )(x, gain, *gouts)


SEQ_TILE = 256
HALO = 8


def _first_step():
    return jnp.logical_and(pl.program_id(0) == 0, pl.program_id(1) == 0)


def _accum(ref, val, first):
    @pl.when(first)
    def _():
        ref[...] = val

    @pl.when(jnp.logical_not(first))
    def _():
        ref[...] += val


def _softplus(x):
    return jnp.maximum(x, 0.0) + jnp.log1p(jnp.exp(-jnp.abs(x)))


def _neg_expm1(z):
    series = -z * (1.0 + z * (0.5 + z * (1.0 / 6.0 + z * (1.0 / 24.0 + z * (1.0 / 120.0)))))
    return jnp.where(z > -0.05, series, 1.0 - jnp.exp(z))


def _lru_gates(xc, ga, gab, gx, gxb, lam):
    r = jax.nn.sigmoid(dmm(xc, ga) + gab)
    i = jax.nn.sigmoid(dmm(xc, gx) + gxb)
    log_a = -LRU_C * r * _softplus(-lam)
    a = jnp.exp(log_a)
    u = jnp.sqrt(_neg_expm1(2.0 * log_a)) * (i * xc)
    return a, u


def _scan_down(a, u):
    n = a.shape[0]
    rows = lax.broadcasted_iota(jnp.int32, a.shape, 0)
    d = 1
    while d < n:
        keep = rows >= d
        a_s = jnp.where(keep, pltpu.roll(a, d, 0), 1.0)
        u_s = jnp.where(keep, pltpu.roll(u, d, 0), 0.0)
        u = a * u_s + u
        a = a * a_s
        d *= 2
    return a, u


def _scan_up(b, g):
    n = b.shape[0]
    rows = lax.broadcasted_iota(jnp.int32, b.shape, 0)
    d = 1
    while d < n:
        keep = rows < n - d
        b_s = jnp.where(keep, pltpu.roll(b, n - d, 0), 1.0)
        g_s = jnp.where(keep, pltpu.roll(g, n - d, 0), 0.0)
        g = g + b * g_s
        b = b * b_s
        d *= 2
    return g


def _seq_specs(ts, c, nt, reverse=False):
    per = ts // HALO

    def jj(j):
        return (nt - 1 - j) if reverse else j

    tile = pl.BlockSpec((1, ts, c), lambda b, j: (b, jj(j), 0))
    before = pl.BlockSpec((1, HALO, c), lambda b, j: (b, jnp.maximum(jj(j) * per - 1, 0), 0))
    after = pl.BlockSpec((1, HALO, c), lambda b, j: (b, jnp.minimum((jj(j) + 1) * per, nt * per - 1), 0))
    return tile, before, after


def _const2(shape):
    nd = len(shape)
    return pl.BlockSpec(tuple(shape), lambda b, j: (0,) * nd)


def lru_forward(xpre, gate, cw, cb, ga, gab, gx, gxb, lam, name):
    nb, ns, w = xpre.shape
    ts = SEQ_TILE
    nt = ns // ts
    tile, _, _ = _seq_specs(ts, w, nt)

    def body(xp_ref, gt_ref, cw_ref, cb_ref, ga_ref, gab_ref, gx_ref, gxb_ref, lam_ref,
             y_ref, xc_ref, hs_ref, xin, hcar):
        @pl.when(pl.program_id(1) == 0)
        def _():
            xin[0:HALO, :] = jnp.zeros((HALO, w), f32)
            hcar[...] = jnp.zeros_like(hcar)

        xin[HALO:HALO + ts, :] = xp_ref[0]
        xc = jnp.broadcast_to(cb_ref[...], (ts, w))
        for k in range(LRU_CONV):
            xc = xc + cw_ref[k:k + 1, :] * xin[pl.ds(HALO - LRU_CONV + 1 + k, ts), :]
        xin[0:HALO, :] = xin[ts:ts + HALO, :]
        a, u = _lru_gates(xc, ga_ref[...], gab_ref[...], gx_ref[...], gxb_ref[...], lam_ref[...])
        acum, h = _scan_down(a, u)
        h = h + acum * hcar[0:1, :]
        hcar[0:1, :] = h[ts - 1:ts, :]
        xc_ref[0] = xc
        hs_ref[0] = h
        y_ref[0] = (h * jax.nn.gelu(gt_ref[0])).astype(y_ref.dtype)

    params = [cw, cb, ga, gab, gx, gxb, lam]
    return pl.pallas_call(
        body, name=name, grid=(nb, nt),
        in_specs=[tile, tile] + [_const2(p.shape) for p in params],
        out_specs=[tile, tile, tile],
        out_shape=[jax.ShapeDtypeStruct((nb, ns, w), MXU_DTYPE), jax.ShapeDtypeStruct((nb, ns, w), f32),
                   jax.ShapeDtypeStruct((nb, ns, w), f32)],
        scratch_shapes=[pltpu.VMEM((ts + HALO, w), f32), pltpu.VMEM((HALO, w), f32)],
        compiler_params=_cparams("arbitrary", "arbitrary"),
    )(xpre, gate, *params)


def lru_backward(dy, xpre, gate, xc, hs, cw, cb, ga, gab, gx, gxb, lam, name, rider=None):
    nb, ns, w = xpre.shape
    ts = SEQ_TILE
    nt = ns // ts
    tile, before, _ = _seq_specs(ts, w, nt, reverse=True)

    def body(dy_ref, xp_ref, xpb_ref, gt_ref, xc_ref, hs_ref, hsb_ref,
             cw_ref, cb_ref, ga_ref, gab_ref, gx_ref, gxb_ref, lam_ref,
             dxp_ref, dgt_ref, dcw_ref, dcb_ref, dga_ref, dgab_ref, dgx_ref, dgxb_ref, dlam_ref,
             dxc_ext, gcar, xin):
        j = pl.program_id(1)
        first = _first_step()
        at_seq_start = j == nt - 1

        @pl.when(j == 0)
        def _():
            dxc_ext[ts:ts + HALO, :] = jnp.zeros((HALO, w), f32)
            gcar[...] = jnp.zeros_like(gcar)

        gt = gt_ref[0]
        h = hs_ref[0]
        dyv = dy_ref[0].astype(f32)
        gl, gelu_vjp = jax.vjp(jax.nn.gelu, gt)
        dgt_ref[0] = gelu_vjp(dyv * h)[0].astype(dgt_ref.dtype)
        dh = dyv * gl

        (a, _), gates_vjp = jax.vjp(_lru_gates, xc_ref[0], ga_ref[...], gab_ref[...], gx_ref[...], gxb_ref[...],
                                    lam_ref[...])
        rows = lax.broadcasted_iota(jnp.int32, (ts, w), 0)
        dh = dh + jnp.where(rows == ts - 1, gcar[0:1, :], 0.0)
        b = pltpu.roll(a, ts - 1, 0)
        g = _scan_up(b, dh)
        gcar[0:1, :] = a[0:1, :] * g[0:1, :]
        hprev_row = jnp.where(at_seq_start, 0.0, hsb_ref[0][HALO - 1:HALO, :])
        hprev = jnp.where(rows == 0, hprev_row, pltpu.roll(h, 1, 0))
        dxc, dga, dgab, dgx, dgxb, dlam = gates_vjp((g * hprev, g))

        _accum(dga_ref, dga, first)
        _accum(dgx_ref, dgx, first)
        _accum(dgab_ref, dgab, first)
        _accum(dgxb_ref, dgxb, first)
        _accum(dlam_ref, dlam, first)
        _accum(dcb_ref, jnp.sum(dxc, axis=0, keepdims=True), first)

        dxc_ext[0:ts, :] = dxc
        dxp = jnp.zeros((ts, w), f32)
        for k in range(LRU_CONV):
            dxp = dxp + cw_ref[k:k + 1, :] * dxc_ext[pl.ds(LRU_CONV - 1 - k, ts), :]
        dxp_ref[0] = dxp.astype(dxp_ref.dtype)
        dxc_ext[ts:ts + HALO, :] = dxc[0:HALO, :]

        xin[0:HALO, :] = jnp.where(at_seq_start, 0.0, xpb_ref[0])
        xin[HALO:HALO + ts, :] = xp_ref[0]
        dcw_rows = [jnp.sum(dxc * xin[pl.ds(HALO - LRU_CONV + 1 + k, ts), :], axis=0, keepdims=True)
                    for k in range(LRU_CONV)]
        dcw_rows += [jnp.zeros((1, w), f32)] * (HALO - LRU_CONV)
        _accum(dcw_ref, jnp.concatenate(dcw_rows, axis=0), first)

    params = [cw, cb, ga, gab, gx, gxb, lam]
    pshape = lambda p: jax.ShapeDtypeStruct(p.shape, f32)
    grid = (nb, nt)
    n_main = 3 + len(params) - 1
    body, in_specs, out_specs, out_shape, scratch = _ride(
        rider, body, [tile, tile, before, tile, tile, tile, before] + [_const2(p.shape) for p in params],
        [tile, tile, _const2((HALO, w))] + [_const2(p.shape) for p in params[1:]],
        [jax.ShapeDtypeStruct((nb, ns, w), MXU_DTYPE), jax.ShapeDtypeStruct((nb, ns, w), MXU_DTYPE),
         jax.ShapeDtypeStruct((HALO, w), f32)] + [pshape(p) for p in params[1:]],
        [pltpu.VMEM((ts + HALO, w), f32), pltpu.VMEM((HALO, w), f32), pltpu.VMEM((ts + HALO, w), f32)], grid)
    outs = pl.pallas_call(
        body, name=name, grid=grid, in_specs=in_specs, out_specs=out_specs, out_shape=out_shape,
        scratch_shapes=scratch, compiler_params=_cparams("arbitrary", "arbitrary"),
    )(dy, xpre, xpre, gate, xc, hs, hs, *params, *(rider.inputs if rider else []))
    return (list(outs[:n_main]), list(outs[n_main:])) if rider else outs


SB_TILE = 256


def _split_dot(x, m):
    hi = x.astype(MXU_DTYPE)
    lo = (x - hi.astype(f32)).astype(MXU_DTYPE)
    return jnp.dot(hi, m, preferred_element_type=f32) + jnp.dot(lo, m, preferred_element_type=f32)


def _suffix_matrices(n):
    r = lax.broadcasted_iota(jnp.int32, (n, n), 0)
    c = lax.broadcasted_iota(jnp.int32, (n, n), 1)
    return (r > c).astype(MXU_DTYPE), (r >= c).astype(MXU_DTYPE)


def _sb_logits(qh, kb, strict):
    z = _mm_nt(qh, kb)
    ls = jnp.minimum(z, 0.0) - jnp.log(1.0 + jnp.exp(-jnp.abs(z)))
    lk = ls - z
    if strict is not None:
        lk = jnp.where(strict, lk, 0.0)
    return ls, lk


def _head_masked(x, dtype):
    lane = lax.broadcasted_iota(jnp.int32, x.shape, 1)
    return (jnp.where(lane < SB_HEAD_DIM, x, 0.0).astype(dtype), jnp.where(lane >= SB_HEAD_DIM, x, 0.0).astype(dtype))


def _stack_heads(dst, x, tq):
    x0, x1 = _head_masked(x, dst.dtype)
    for blk in range(dst.shape[0]):
        dst[blk, 0:tq, :] = x0[blk * tq:(blk + 1) * tq]
        dst[blk, tq:2 * tq, :] = x1[blk * tq:(blk + 1) * tq]


def _strict_mask(tq):
    rr = lax.broadcasted_iota(jnp.int32, (2 * tq, tq), 0)
    cc = lax.broadcasted_iota(jnp.int32, (2 * tq, tq), 1)
    return cc < jnp.where(rr >= tq, rr - tq, rr)


def _sb_specs(ns):
    npair = SB_WIDTH // LANES
    q = pl.BlockSpec((1, ns, LANES), lambda b, p: (b, 0, p))
    k = pl.BlockSpec((1, ns, LANES), lambda b, p: (b, 0, npair + p))
    v = pl.BlockSpec((1, ns, LANES), lambda b, p: (b, 0, 2 * npair + p))
    return q, k, v, npair


def sb_forward(qkv, name, rider=None):
    nb, ns, _ = qkv.shape
    tq = SB_TILE
    nq = ns // tq
    qspec, kspec, vspec, npair = _sb_specs(ns)

    def body(q_ref, k_ref, v_ref, o_ref, qs, ks, vs, acc):
        scale = 1.0 / math.sqrt(SB_HEAD_DIM)
        _stack_heads(qs, q_ref[0] * scale, tq)
        ks[...] = k_ref[0].astype(MXU_DTYPE)
        _stack_heads(vs, v_ref[0], tq)
        mx, _ = _suffix_matrices(tq)
        strict = _strict_mask(tq)

        def step(q2, blks, r2, masked):
            kbs = [ks[pl.ds(pl.multiple_of(b * tq, tq), tq), :] for b in blks]
            lg = [_sb_logits(q2, kb, strict if masked else None) for kb in kbs]
            sums = [jnp.dot(lk.astype(MXU_DTYPE), mx, preferred_element_type=f32) for _, lk in lg]
            total = None
            for (ls, lk), s, b in zip(lg, sums, blks):
                a = r2 + s
                w = jnp.exp(ls + a)
                if masked:
                    w = jnp.where(strict, w, 0.0)
                wb = w.astype(MXU_DTYPE)
                part = (jnp.dot(wb[:tq], vs[b, 0:tq, :], preferred_element_type=f32)
                        + jnp.dot(wb[tq:], vs[b, tq:2 * tq, :], preferred_element_type=f32))
                total = part if total is None else total + part
                r2 = a[:, 0:1] + lk[:, 0:1]
            acc[...] += total
            return r2

        def q_block(qi, carry):
            acc[...] = jnp.zeros_like(acc)
            q2 = qs[qi]
            r2 = step(q2, [qi], jnp.zeros((2 * tq, 1), f32), True)
            r2 = lax.fori_loop(0, lax.shift_right_logical(qi, 1),
                               lambda i, r: step(q2, [qi - 1 - 2 * i, qi - 2 - 2 * i], r, False), r2)
            lax.cond(jnp.bitwise_and(qi, 1) == 1, lambda r: step(q2, [0], r, False), lambda r: r, r2)
            o_ref[0, pl.ds(pl.multiple_of(qi * tq, tq), tq), :] = acc[...]
            return carry

        lax.fori_loop(0, nq, q_block, 0)

    grid = (nb, npair)
    body, in_specs, out_specs, out_shape, scratch = _ride(
        rider, body, [qspec, kspec, vspec], [pl.BlockSpec((1, ns, LANES), lambda b, p: (b, 0, p))],
        [jax.ShapeDtypeStruct((nb, ns, SB_WIDTH), f32)],
        [pltpu.VMEM((nq, 2 * tq, LANES), MXU_DTYPE), pltpu.VMEM((ns, LANES), MXU_DTYPE),
         pltpu.VMEM((nq, 2 * tq, LANES), MXU_DTYPE), pltpu.VMEM((tq, LANES), f32)], grid)
    outs = pl.pallas_call(
        body, name=name, grid=grid, in_specs=in_specs, out_specs=out_specs, out_shape=out_shape,
        scratch_shapes=scratch,
        compiler_params=_cparams("arbitrary", "arbitrary") if rider else _cparams("parallel", "parallel"),
    )(qkv, qkv, qkv, *(rider.inputs if rider else []))
    return (outs[0], list(outs[1:])) if rider else outs[0]


def sb_backward(qkv, o, do, name, rider=None):
    nb, ns, _ = qkv.shape
    tq = SB_TILE
    nq = ns // tq
    qspec, kspec, vspec, npair = _sb_specs(ns)
    ospec = pl.BlockSpec((1, ns, LANES), lambda b, p: (b, 0, p))

    def body(q_ref, k_ref, v_ref, o_ref, do_ref, dq_ref, dk_ref, dv_ref, qs, ks, kcat, vs, dos, dqacc, dkacc, dvacc):
        scale = 1.0 / math.sqrt(SB_HEAD_DIM)
        _stack_heads(qs, q_ref[0] * scale, tq)
        ks[...] = k_ref[0].astype(MXU_DTYPE)
        _stack_heads(kcat, k_ref[0], tq)
        vs[...] = v_ref[0].astype(MXU_DTYPE)
        _stack_heads(dos, do_ref[0].astype(f32), tq)
        dkacc[...] = jnp.zeros_like(dkacc)
        dvacc[...] = jnp.zeros_like(dvacc)
        mx, mi = _suffix_matrices(tq)
        strict = _strict_mask(tq)

        def step(q2, do2, dtot2, blks, carry, masked):
            r2, g2 = carry
            k0s = [pl.multiple_of(b * tq, tq) for b in blks]
            lg = [_sb_logits(q2, ks[pl.ds(k0, tq), :], strict if masked else None) for k0 in k0s]
            dws = [_mm_nt(do2, vs[pl.ds(k0, tq), :]) for k0 in k0s]
            sums = [jnp.dot(lk.astype(MXU_DTYPE), mx, preferred_element_type=f32) for _, lk in lg]
            wbs, es = [], []
            for (ls, lk), s in zip(lg, sums):
                a = r2 + s
                w = jnp.exp(ls + a)
                if masked:
                    w = jnp.where(strict, w, 0.0)
                wbs.append(w.astype(MXU_DTYPE))
                r2 = a[:, 0:1] + lk[:, 0:1]
            es = [wb.astype(f32) * dw for wb, dw in zip(wbs, dws)]
            esums = [_split_dot(e, mi) for e in es]
            dq = None
            for (ls, _), e, esum, wb, b, k0 in zip(lg, es, esums, wbs, blks, k0s):
                esuf = g2 + esum
                beta = jnp.exp(ls)
                dz = e * (1.0 - beta) - beta * (dtot2 - esuf)
                if masked:
                    dz = jnp.where(strict, dz, 0.0)
                dzb = dz.astype(MXU_DTYPE)
                part = (jnp.dot(dzb[:tq], kcat[b, 0:tq, :], preferred_element_type=f32)
                        + jnp.dot(dzb[tq:], kcat[b, tq:2 * tq, :], preferred_element_type=f32))
                dq = part if dq is None else dq + part
                dkacc[pl.ds(k0, tq), :] += _mm_tn(dzb, q2)
                dvacc[pl.ds(k0, tq), :] += _mm_tn(wb, do2)
                g2 = esuf[:, 0:1]
            dqacc[...] += dq
            return r2, g2

        def q_block(qi, carry):
            dqacc[...] = jnp.zeros_like(dqacc)
            q2, do2 = qs[qi], dos[qi]
            ov = o_ref[0, pl.ds(pl.multiple_of(qi * tq, tq), tq), :]
            dtot2 = jnp.sum(do2.astype(f32) * jnp.concatenate([ov, ov], axis=0), axis=1, keepdims=True)
            zero = jnp.zeros((2 * tq, 1), f32)
            c = step(q2, do2, dtot2, [qi], (zero, zero), True)
            c = lax.fori_loop(0, lax.shift_right_logical(qi, 1),
                              lambda i, c: step(q2, do2, dtot2, [qi - 1 - 2 * i, qi - 2 - 2 * i], c, False), c)
            lax.cond(jnp.bitwise_and(qi, 1) == 1, lambda c: step(q2, do2, dtot2, [0], c, False), lambda c: c, c)
            dq_ref[0, pl.ds(pl.multiple_of(qi * tq, tq), tq), :] = (dqacc[...] * scale).astype(dq_ref.dtype)
            return carry

        lax.fori_loop(0, nq, q_block, 0)
        dk_ref[0] = dkacc[...].astype(dk_ref.dtype)
        dv_ref[0] = dvacc[...].astype(dv_ref.dtype)

    dshape = jax.ShapeDtypeStruct((nb, ns, SB_WIDTH), MXU_DTYPE)
    stacked = pltpu.VMEM((nq, 2 * tq, LANES), MXU_DTYPE)
    flat = pltpu.VMEM((ns, LANES), MXU_DTYPE)
    grid = (nb, npair)
    body, in_specs, out_specs, out_shape, scratch = _ride(
        rider, body, [qspec, kspec, vspec, ospec, ospec], [ospec, ospec, ospec], [dshape, dshape, dshape],
        [stacked, flat, stacked, flat, stacked,
         pltpu.VMEM((tq, LANES), f32), pltpu.VMEM((ns, LANES), f32), pltpu.VMEM((ns, LANES), f32)], grid)
    outs = pl.pallas_call(
        body, name=name, grid=grid, in_specs=in_specs, out_specs=out_specs, out_shape=out_shape,
        scratch_shapes=scratch,
        compiler_params=_cparams("arbitrary", "arbitrary") if rider else _cparams("parallel", "parallel"),
    )(qkv, qkv, qkv, o, do, *(rider.inputs if rider else []))
    return (list(outs[:3]), list(outs[3:])) if rider else list(outs)


SSM_PAIRS = SSM_HEADS // 2
PAIRS_PER_GROUP = SSM_PAIRS // SSM_GROUPS
GROUP_WIDTH = SSM_WIDTH // SSM_GROUPS


def _silu(x):
    return x * jax.nn.sigmoid(x)


def _ssd_chunk(xs_pre, b_pre, c_pre, dt_raw, dt_raw_t, z, st, dt_bias_r, dt_bias_c, a_log_r, a_log_c, d_skip,
               gains):
    n = dt_raw.shape[0]
    rows = lax.broadcasted_iota(jnp.int32, (n, n), 0)
    cols = lax.broadcasted_iota(jnp.int32, (n, n), 1)
    tril = cols <= rows
    tri_l = tril.astype(f32)
    tri_u = (rows <= cols).astype(f32)
    lane = lax.broadcasted_iota(jnp.int32, (n, LANES), 1)
    sub = lax.broadcasted_iota(jnp.int32, (LANES, n), 0)

    dt = _softplus(dt_raw + dt_bias_r)
    a_r = -jnp.exp(a_log_r)
    da = dt * a_r
    acs = _mm_exact(tri_l, da)
    dt_t = _softplus(dt_raw_t + dt_bias_c)
    acs_t = _mm_exact(dt_t * (-jnp.exp(a_log_c)), tri_u)

    bs = [_silu(b) for b in b_pre]
    cs = [_silu(c) for c in c_pre]
    cb = [dmm_nt(cs[g], bs[g]) for g in range(SSM_GROUPS)]

    end = jnp.sum(da, axis=0, keepdims=True)
    lane_row = lax.broadcasted_iota(jnp.int32, (1, LANES), 1)
    first_head = lane < SSM_HEAD_DIM
    first_head_row = lane_row < SSM_HEAD_DIM

    def head_col(v, h):
        return jnp.sum(jnp.where((lane if v.shape[0] == n else lane_row) == h, v, 0.0), axis=1, keepdims=True)

    ys, st_new = [], []
    for p in range(SSM_PAIRS):
        g = p // PAIRS_PER_GROUP
        h0, h1 = 2 * p, 2 * p + 1
        xs = _silu(xs_pre[p])
        acols = [head_col(acs, h0), head_col(acs, h1)]
        dt_p = jnp.where(first_head, head_col(dt, h0), head_col(dt, h1))
        acs_p = jnp.where(first_head, acols[0], acols[1])
        end_p = jnp.where(first_head_row, head_col(end, h0), head_col(end, h1))
        dsk_p = jnp.where(first_head_row, head_col(d_skip, h0), head_col(d_skip, h1))
        xdt = xs * dt_p
        y = jnp.exp(acs_p) * dmm(cs[g], st[p])
        for hh in range(2):
            row = jnp.sum(jnp.where(sub == 2 * p + hh, acs_t, 0.0), axis=0, keepdims=True)
            decay = jnp.where(tril, jnp.exp(jnp.where(tril, acols[hh] - row, 0.0)), 0.0)
            head = first_head if hh == 0 else jnp.logical_not(first_head)
            y = y + dmm(cb[g] * decay, jnp.where(head, xdt, 0.0))
        st_new.append(jnp.exp(end_p) * st[p] + dmm_tn(bs[g], xdt * jnp.exp(end_p - acs_p)))
        ys.append(y + dsk_p * xs)
    out = []
    for g in range(SSM_GROUPS):
        yg = jnp.concatenate(ys[g * PAIRS_PER_GROUP:(g + 1) * PAIRS_PER_GROUP], axis=1) * _silu(z[g])
        out.append(_rms(yg, gains[g]))
    return out, st_new


def _ssd_chunk_inputs(xconv, dtr, z, st_ref, gain):
    xs_pre = [xconv[:, LANES * p:LANES * (p + 1)] for p in range(SSM_PAIRS)]
    b0 = SSM_WIDTH
    c0 = SSM_WIDTH + SSM_GROUPS * SSM_STATE
    b_pre = [xconv[:, b0 + SSM_STATE * g:b0 + SSM_STATE * (g + 1)] for g in range(SSM_GROUPS)]
    c_pre = [xconv[:, c0 + SSM_STATE * g:c0 + SSM_STATE * (g + 1)] for g in range(SSM_GROUPS)]
    zs = [z[:, GROUP_WIDTH * g:GROUP_WIDTH * (g + 1)] for g in range(SSM_GROUPS)]
    sts = [st_ref[p] for p in range(SSM_PAIRS)]
    gains = [gain[:, GROUP_WIDTH * g:GROUP_WIDTH * (g + 1)] for g in range(SSM_GROUPS)]
    return xs_pre, b_pre, c_pre, dtr, dtr.T, zs, sts, gains


def ssd_forward(xbc, dt_raw, z, cw, cb, dbr, dbc, alr, alc, dsk, gain, name):
    nb, ns, wx = xbc.shape
    ln = SSM_CHUNK
    nt = ns // ln
    tile = lambda c: pl.BlockSpec((1, ln, c), lambda b, j: (b, j, 0))
    st_spec = pl.BlockSpec((1, 1, SSM_PAIRS, SSM_STATE, LANES), lambda b, j: (b, j, 0, 0, 0))

    def body(xbc_ref, dt_ref, z_ref, cw_ref, cb_ref, dbr_ref, dbc_ref, alr_ref, alc_ref, dsk_ref, gain_ref,
             y_ref, xconv_ref, stp_ref, xin, st):
        @pl.when(pl.program_id(1) == 0)
        def _():
            xin[0:HALO, :] = jnp.zeros((HALO, wx), f32)
            st[...] = jnp.zeros_like(st)

        xin[HALO:HALO + ln, :] = xbc_ref[0]
        xconv = jnp.broadcast_to(cb_ref[...], (ln, wx))
        for k in range(SSM_CONV):
            xconv = xconv + cw_ref[k:k + 1, :] * xin[pl.ds(HALO - SSM_CONV + 1 + k, ln), :]
        xin[0:HALO, :] = xin[ln:ln + HALO, :]
        xconv_ref[0] = xconv
        stp_ref[0, 0] = st[...]
        xs_pre, b_pre, c_pre, dtr, dtr_t, zs, sts, gains = _ssd_chunk_inputs(xconv, dt_ref[0], z_ref[0], st,
                                                                             gain_ref[...])
        out, st_new = _ssd_chunk(xs_pre, b_pre, c_pre, dtr, dtr_t, zs, sts, dbr_ref[...], dbc_ref[...],
                                 alr_ref[...], alc_ref[...], dsk_ref[...], gains)
        y_ref[0] = jnp.concatenate(out, axis=1).astype(y_ref.dtype)
        for p in range(SSM_PAIRS):
            st[p] = st_new[p]

    params = [cw, cb, dbr, dbc, alr, alc, dsk, gain]
    return pl.pallas_call(
        body, name=name, grid=(nb, nt),
        in_specs=[tile(wx), tile(LANES), tile(SSM_WIDTH)] + [_const2(p.shape) for p in params],
        out_specs=[tile(SSM_WIDTH), tile(wx), st_spec],
        out_shape=[jax.ShapeDtypeStruct((nb, ns, SSM_WIDTH), MXU_DTYPE), jax.ShapeDtypeStruct((nb, ns, wx), f32),
                   jax.ShapeDtypeStruct((nb, nt, SSM_PAIRS, SSM_STATE, LANES), f32)],
        scratch_shapes=[pltpu.VMEM((ln + HALO, wx), f32), pltpu.VMEM((SSM_PAIRS, SSM_STATE, LANES), f32)],
        compiler_params=_cparams("arbitrary", "arbitrary"),
    )(xbc, dt_raw, z, *params)


def ssd_backward(dy, xbc, xconv, dt_raw, z, stp, cw, cb, dbr, dbc, alr, alc, dsk, gain, name):
    nb, ns, wx = xbc.shape
    ln = SSM_CHUNK
    nt = ns // ln
    per = ln // HALO
    rj = lambda j: nt - 1 - j
    tile = lambda c: pl.BlockSpec((1, ln, c), lambda b, j: (b, rj(j), 0))
    before = pl.BlockSpec((1, HALO, wx), lambda b, j: (b, jnp.maximum(rj(j) * per - 1, 0), 0))
    st_spec = pl.BlockSpec((1, 1, SSM_PAIRS, SSM_STATE, LANES), lambda b, j: (b, rj(j), 0, 0, 0))

    def body(dy_ref, xbc_ref, xbcb_ref, xconv_ref, dt_ref, z_ref, stp_ref,
             cw_ref, cb_ref, dbr_ref, dbc_ref, alr_ref, alc_ref, dsk_ref, gain_ref,
             dxbc_ref, ddt_ref, dz_ref, dcw_ref, dcb_ref, ddbr_ref, ddbc_ref, dalr_ref, dalc_ref, ddsk_ref, dgain_ref,
             dxc_ext, dst, xin):
        j = pl.program_id(1)
        first = _first_step()
        at_seq_start = j == nt - 1

        @pl.when(j == 0)
        def _():
            dxc_ext[ln:ln + HALO, :] = jnp.zeros((HALO, wx), f32)
            dst[...] = jnp.zeros_like(dst)

        xs_pre, b_pre, c_pre, dtr, dtr_t, zs, sts, gains = _ssd_chunk_inputs(xconv_ref[0], dt_ref[0], z_ref[0],
                                                                             stp_ref.at[0, 0], gain_ref[...])
        _, vjp = jax.vjp(_ssd_chunk, xs_pre, b_pre, c_pre, dtr, dtr_t, zs, sts, dbr_ref[...], dbc_ref[...],
                         alr_ref[...], alc_ref[...], dsk_ref[...], gains)
        dyv = dy_ref[0].astype(f32)
        cot = ([dyv[:, GROUP_WIDTH * g:GROUP_WIDTH * (g + 1)] for g in range(SSM_GROUPS)],
               [dst[p] for p in range(SSM_PAIRS)])
        dxs, db, dc, ddt, ddt_t, dzs, dsts, ddbr, ddbc, dalr, dalc, ddsk, dgains = vjp(cot)
        for p in range(SSM_PAIRS):
            dst[p] = dsts[p]
        ddt_ref[0] = (ddt + ddt_t.T).astype(ddt_ref.dtype)
        dz_ref[0] = jnp.concatenate(dzs, axis=1).astype(dz_ref.dtype)
        _accum(ddbr_ref, ddbr, first)
        _accum(ddbc_ref, ddbc, first)
        _accum(dalr_ref, dalr, first)
        _accum(dalc_ref, dalc, first)
        _accum(ddsk_ref, ddsk, first)
        _accum(dgain_ref, jnp.concatenate(dgains, axis=1), first)

        dxc = jnp.concatenate(dxs + db + dc, axis=1)
        _accum(dcb_ref, jnp.sum(dxc, axis=0, keepdims=True), first)
        dxc_ext[0:ln, :] = dxc
        dxp = jnp.zeros((ln, wx), f32)
        for k in range(SSM_CONV):
            dxp = dxp + cw_ref[k:k + 1, :] * dxc_ext[pl.ds(SSM_CONV - 1 - k, ln), :]
        dxbc_ref[0] = dxp.astype(dxbc_ref.dtype)
        dxc_ext[ln:ln + HALO, :] = dxc[0:HALO, :]

        xin[0:HALO, :] = jnp.where(at_seq_start, 0.0, xbcb_ref[0])
        xin[HALO:HALO + ln, :] = xbc_ref[0]
        dcw_rows = [jnp.sum(dxc * xin[pl.ds(HALO - SSM_CONV + 1 + k, ln), :], axis=0, keepdims=True)
                    for k in range(SSM_CONV)]
        dcw_rows += [jnp.zeros((1, wx), f32)] * (HALO - SSM_CONV)
        _accum(dcw_ref, jnp.concatenate(dcw_rows, axis=0), first)

    params = [cw, cb, dbr, dbc, alr, alc, dsk, gain]
    pshape = lambda p: jax.ShapeDtypeStruct(p.shape, f32)
    return pl.pallas_call(
        body, name=name, grid=(nb, nt),
        in_specs=[tile(SSM_WIDTH), tile(wx), before, tile(wx), tile(LANES), tile(SSM_WIDTH), st_spec]
        + [_const2(p.shape) for p in params],
        out_specs=[tile(wx), tile(LANES), tile(SSM_WIDTH), _const2((HALO, wx))] + [_const2(p.shape) for p in params[1:]],
        out_shape=[jax.ShapeDtypeStruct((nb, ns, wx), MXU_DTYPE), jax.ShapeDtypeStruct((nb, ns, LANES), MXU_DTYPE),
                   jax.ShapeDtypeStruct((nb, ns, SSM_WIDTH), MXU_DTYPE), jax.ShapeDtypeStruct((HALO, wx), f32)]
        + [pshape(p) for p in params[1:]],
        scratch_shapes=[pltpu.VMEM((ln + HALO, wx), f32), pltpu.VMEM((SSM_PAIRS, SSM_STATE, LANES), f32),
                        pltpu.VMEM((ln + HALO, wx), f32)],
        compiler_params=_cparams("arbitrary", "arbitrary"),
    )(dy, xbc, xbc, xconv, dt_raw, z, stp, *params)


CONF_HALO = 32
CONF_OFF = CONF_HALO - CONF_KERNEL + 1


def _conf_specs(ts, c, nt):
    per = ts // CONF_HALO
    tile = pl.BlockSpec((1, ts, c), lambda b, j: (b, j, 0))
    before = pl.BlockSpec((1, CONF_HALO, c), lambda b, j: (b, jnp.maximum(j * per - 1, 0), 0))
    after = pl.BlockSpec((1, CONF_HALO, c), lambda b, j: (b, jnp.minimum((j + 1) * per, nt * per - 1), 0))
    return tile, before, after


SUBLANES = 8


def _shifted_copies(dst, x):
    rows = x.shape[0]
    dst[0] = x
    for b in range(1, SUBLANES):
        dst[b] = pltpu.roll(x, rows - b, 0)


def _window(copies, off, size):
    b = off % SUBLANES
    return copies[b, pl.ds(off - b, size), :]


def _glu(x):
    return x[:, :CONF_WIDTH] * jax.nn.sigmoid(x[:, CONF_WIDTH:])


def _layernorm_parts(c):
    xc = c - jnp.mean(c, axis=-1, keepdims=True)
    r = lax.rsqrt(jnp.mean(xc * xc, axis=-1, keepdims=True) + EPS)
    return xc * r, r


def conf_forward(glu, cw, cb, ln_g, ln_b, name):
    nb, ns, wg = glu.shape
    w = CONF_WIDTH
    ts = SEQ_TILE
    nt = ns // ts
    tile, before, _ = _conf_specs(ts, wg, nt)

    def body(x_ref, xb_ref, cw_ref, cb_ref, g_ref, b_ref, y_ref, u_rot):
        _shifted_copies(u_rot, jnp.concatenate(
            [jnp.where(pl.program_id(1) == 0, 0.0, _glu(xb_ref[0])), _glu(x_ref[0])], axis=0))
        conv = jnp.broadcast_to(cb_ref[...], (ts, w))
        for k in range(CONF_KERNEL):
            conv = conv + cw_ref[k:k + 1, :] * _window(u_rot, CONF_OFF + k, ts)
        xhat, _ = _layernorm_parts(conv)
        y_ref[0] = _silu(xhat * g_ref[...] + b_ref[...]).astype(y_ref.dtype)

    params = [cw, cb, ln_g, ln_b]
    return pl.pallas_call(
        body, name=name, grid=(nb, nt),
        in_specs=[tile, before] + [_const2(p.shape) for p in params],
        out_specs=pl.BlockSpec((1, ts, w), lambda b, j: (b, j, 0)),
        out_shape=jax.ShapeDtypeStruct((nb, ns, w), MXU_DTYPE),
        scratch_shapes=[pltpu.VMEM((SUBLANES, ts + CONF_HALO, w), f32)],
        compiler_params=_cparams("parallel", "parallel"),
    )(glu, glu, *params)


def conf_backward(dy, glu, cw, cb, ln_g, ln_b, name):
    nb, ns, wg = glu.shape
    w = CONF_WIDTH
    ts = SEQ_TILE
    nt = ns // ts
    te = ts + CONF_HALO
    tile, before, after = _conf_specs(ts, wg, nt)
    dtile, _, dafter = _conf_specs(ts, w, nt)

    def body(dy_ref, dya_ref, x_ref, xb_ref, xa_ref, cw_ref, cb_ref, g_ref, b_ref,
             dx_ref, dcw_ref, dcb_ref, dg_ref, db_ref, u_ext, dc_ext):
        j = pl.program_id(1)
        first = _first_step()
        x = x_ref[0]
        _shifted_copies(u_ext, jnp.concatenate(
            [jnp.where(j == 0, 0.0, _glu(xb_ref[0])), _glu(x), _glu(xa_ref[0])], axis=0))
        conv = jnp.broadcast_to(cb_ref[...], (te, w))
        for k in range(CONF_KERNEL):
            conv = conv + cw_ref[k:k + 1, :] * _window(u_ext, CONF_OFF + k, te)
        xhat, r = _layernorm_parts(conv)
        lnout = xhat * g_ref[...] + b_ref[...]
        sg = jax.nn.sigmoid(lnout)
        rows = lax.broadcasted_iota(jnp.int32, (te, w), 0)
        dyv = jnp.concatenate([dy_ref[0].astype(f32), dya_ref[0].astype(f32)], axis=0)
        dyv = jnp.where(jnp.logical_and(j == nt - 1, rows >= ts), 0.0, dyv)
        dln = dyv * sg * (1.0 + lnout * (1.0 - sg))
        in_tile = rows < ts
        _accum(dg_ref, jnp.sum(jnp.where(in_tile, dln * xhat, 0.0), axis=0, keepdims=True), first)
        _accum(db_ref, jnp.sum(jnp.where(in_tile, dln, 0.0), axis=0, keepdims=True), first)
        dxh = dln * g_ref[...]
        dconv = r * (dxh - jnp.mean(dxh, axis=-1, keepdims=True) - xhat * jnp.mean(dxh * xhat, axis=-1, keepdims=True))
        _shifted_copies(dc_ext, dconv)
        dct = dconv[0:ts, :]
        _accum(dcb_ref, jnp.sum(dct, axis=0, keepdims=True), first)
        du = jnp.zeros((ts, w), f32)
        dcw_rows = []
        for k in range(CONF_KERNEL):
            du = du + cw_ref[k:k + 1, :] * _window(dc_ext, CONF_KERNEL - 1 - k, ts)
            dcw_rows.append(jnp.sum(dct * _window(u_ext, CONF_OFF + k, ts), axis=0, keepdims=True))
        dcw_rows.append(jnp.zeros((1, w), f32))
        _accum(dcw_ref, jnp.concatenate(dcw_rows, axis=0), first)
        sb = jax.nn.sigmoid(x[:, w:])
        dx_ref[0] = jnp.concatenate([du * sb, du * x[:, :w] * sb * (1.0 - sb)], axis=1).astype(dx_ref.dtype)

    params = [cw, cb, ln_g, ln_b]
    return pl.pallas_call(
        body, name=name, grid=(nb, nt),
        in_specs=[dtile, dafter, tile, before, after] + [_const2(p.shape) for p in params],
        out_specs=[tile] + [_const2(p.shape) for p in params],
        out_shape=[jax.ShapeDtypeStruct((nb, ns, wg), MXU_DTYPE)] + [jax.ShapeDtypeStruct(p.shape, f32) for p in params],
        scratch_shapes=[pltpu.VMEM((SUBLANES, te + CONF_HALO, w), f32), pltpu.VMEM((SUBLANES, te, w), f32)],
        compiler_params=_cparams("arbitrary", "arbitrary"),
    )(dy, dy, glu, glu, glu, *params)


def _row(v):
    return v.reshape(1, -1).astype(f32)


def _pad_to(v, n, axis):
    pads = [(0, 0)] * v.ndim
    pads[axis] = (0, n - v.shape[axis])
    return jnp.pad(v, pads)


def _block_diag(w):
    nh, d, _ = w.shape
    eye = jnp.eye(nh, dtype=w.dtype)
    return (eye[:, None, :, None] * w[:, :, None, :]).reshape(nh * d, nh * d)


def _diag_blocks(m, nh):
    d = m.shape[0] // nh
    idx = jnp.arange(nh)
    return m.reshape(nh, d, nh, d)[idx, :, idx, :]


def _mix_even_fwd(h, gpre, w, wl, nb, ns, rider=None):
    t = nb * ns
    w_in = wl["w_in"]
    w_lx, w_lg = Part(w_in, 0, LRU_WIDTH, 1), Part(w_in, LRU_WIDTH, LRU_WIDTH, 1)
    w_qkv = Part(w_in, 2 * LRU_WIDTH, 3 * SB_WIDTH, 1)
    xpre, gate, qkv = norm_matmul(h, gpre, [w_lx, w_lg, w_qkv], [f32, f32, f32], name="ev_in_proj")
    lru_p = [w["ev_lru_conv_w"][0], _row(w["ev_lru_conv_b"][0]),
             _block_diag(w["ev_lru_gate_a_w"][0]).astype(MXU_DTYPE), _row(w["ev_lru_gate_a_b"][0]),
             _block_diag(w["ev_lru_gate_x_w"][0]).astype(MXU_DTYPE), _row(w["ev_lru_gate_x_b"][0]),
             _row(w["ev_lru_lambda"][0])]
    xpre3, gate3, qkv3 = xpre.reshape(nb, ns, -1), gate.reshape(nb, ns, -1), qkv.reshape(nb, ns, -1)
    y_a, xc, hs = lru_forward(xpre3, gate3, *lru_p, name="ev_lru_fwd")
    o = sb_forward(qkv3, name="ev_sb_fwd", rider=rider)
    carried = None
    if rider is not None:
        o, carried = o
    ys = [y_a.reshape(t, -1), o.reshape(t, -1)]
    saved = dict(xpre=xpre3, gate=gate3, qkv=qkv3, xc=xc, hs=hs, o=o, lru_p=lru_p)
    return ys, saved, carried


def _mix_even_bwd(dys, saved, wtl, nb, ns, lru_rider=None, attention_rider=None):
    t = nb * ns
    dy_a, dy_b = [d.reshape(nb, ns, -1) for d in dys]
    outs = lru_backward(dy_a, saved["xpre"], saved["gate"], saved["xc"], saved["hs"], *saved["lru_p"],
                        name="ev_lru_bwd", rider=lru_rider)
    lru_carried = None
    if lru_rider is not None:
        outs, lru_carried = outs
    dxp, dgt, dcw, dcb, dga, dgab, dgx, dgxb, dlam = outs
    rider = attention_rider(lru_carried) if attention_rider is not None else None
    carried = None
    if rider is None:
        dq, dk, dv = sb_backward(saved["qkv"], saved["o"], dy_b, name="ev_sb_bwd")
    else:
        (dq, dk, dv), carried = sb_backward(saved["qkv"], saved["o"], dy_b, name="ev_sb_bwd", rider=rider)
    w_in_t = wtl["w_in"]
    pieces = [dxp, dgt, dq, dk, dv]
    gs = [d.reshape(t, -1) for d in pieces]
    wts = [Part(w_in_t, LRU_WIDTH * i, LRU_WIDTH, 0) for i in range(5)]
    grads = {
        "ev_lru_conv_w": dcw[:LRU_CONV][None], "ev_lru_conv_b": dcb,
        "ev_lru_gate_a_w": _diag_blocks(dga, LRU_HEADS)[None], "ev_lru_gate_a_b": dgab,
        "ev_lru_gate_x_w": _diag_blocks(dgx, LRU_HEADS)[None], "ev_lru_gate_x_b": dgxb,
        "ev_lru_lambda": dlam,
    }
    return gs, wts, grads, carried


def _odd_params(w):
    ssd_p = [w["od_ssm_conv_w"][0], _row(w["od_ssm_conv_b"][0]),
             _pad_to(_row(w["od_ssm_dt_bias"][0]), LANES, 1), _pad_to(_row(w["od_ssm_dt_bias"][0]), LANES, 1).T,
             _pad_to(_row(w["od_ssm_a_log"][0]), LANES, 1), _pad_to(_row(w["od_ssm_a_log"][0]), LANES, 1).T,
             _pad_to(_row(w["od_ssm_d"][0]), LANES, 1), _row(w["od_ssm_norm"][0])]
    conf_p = [_pad_to(w["od_cm_conv_w"][0], CONF_HALO, 0), _row(w["od_cm_conv_b"][0]),
              _row(w["od_cm_ln_g"][0]), _row(w["od_cm_ln_b"][0])]
    return ssd_p, conf_p


ODD_SPLITS = (SSM_WIDTH, SSM_WIDTH + SSM_XBC, SSM_WIDTH + SSM_XBC + SSM_HEADS)


def _mix_odd_fwd(h, gpre, w, wl, nb, ns, rider=None):
    assert rider is None
    t = nb * ns
    w_in = wl["w_in"]
    s0, s1, s2 = ODD_SPLITS
    w_al = jnp.concatenate([w_in[:, :s1], w_in[:, s2:], _pad_to(w_in[:, s1:s2], LANES, 1)], axis=1)
    widths = (s0, s1 - s0, w_in.shape[1] - s2, LANES)
    starts = (0, s0, s1, s1 + widths[2])
    zz, xbc, glu, dtr = norm_matmul(h, gpre, [Part(w_al, a, n, 1) for a, n in zip(starts, widths)], [f32] * 4,
                                    name="od_in_proj")
    ssd_p, conf_p = _odd_params(w)
    zz3, xbc3, dtr3, glu3 = [a.reshape(nb, ns, -1) for a in (zz, xbc, dtr, glu)]
    y_c, xconv, stp = ssd_forward(xbc3, dtr3, zz3, *ssd_p, name="od_ssd_fwd")
    y_d = conf_forward(glu3, *conf_p, name="od_conf_fwd")
    ys = [y_c.reshape(t, -1), y_d.reshape(t, -1)]
    saved = dict(z=zz3, xbc=xbc3, dtr=dtr3, glu=glu3, xconv=xconv, stp=stp, ssd_p=ssd_p, conf_p=conf_p)
    return ys, saved, None


def _mix_odd_bwd(dys, saved, wtl, nb, ns, lru_rider=None, attention_rider=None):
    assert lru_rider is None and attention_rider is None
    t = nb * ns
    dy_c, dy_d = [d.reshape(nb, ns, -1) for d in dys]
    outs = ssd_backward(dy_c, saved["xbc"], saved["xconv"], saved["dtr"], saved["z"], saved["stp"], *saved["ssd_p"],
                        name="od_ssd_bwd")
    dxbc, ddt, dz, dcw, dcb, ddbr, ddbc, dalr, dalc, ddsk, dgain = outs
    dglu, ccw, ccb, clg, clb = conf_backward(dy_d, saved["glu"], *saved["conf_p"], name="od_conf_bwd")
    w_in_t = wtl["w_in"]
    s0, s1, s2 = ODD_SPLITS
    carried = None
    gs = [d.reshape(t, -1) for d in (dz, dxbc, dglu, ddt)]
    wt_al = jnp.concatenate([w_in_t[:s1], w_in_t[s2:], _pad_to(w_in_t[s1:s2], LANES, 0)], axis=0)
    widths = (s0, s1 - s0, w_in_t.shape[0] - s2, LANES)
    starts = (0, s0, s1, s1 + widths[2])
    wts = [Part(wt_al, a, n, 0) for a, n in zip(starts, widths)]
    nh = SSM_HEADS
    grads = {
        "od_ssm_conv_w": dcw[:SSM_CONV][None], "od_ssm_conv_b": dcb,
        "od_ssm_dt_bias": ddbr[:, :nh] + ddbc[:nh, 0][None], "od_ssm_a_log": dalr[:, :nh] + dalc[:nh, 0][None],
        "od_ssm_d": ddsk[:, :nh], "od_ssm_norm": dgain,
        "od_cm_conv_w": ccw[:CONF_KERNEL][None], "od_cm_conv_b": ccb, "od_cm_ln_g": clg, "od_cm_ln_b": clb,
    }
    return gs, wts, grads, carried


LAYER_MATRICES = ("w_in", "w_out", "mlp_w1", "mlp_w2", "ple_w_proj", "ple_w_gate")
NORM_NAMES = ("norm_mix_pre", "norm_mix_post", "norm_mlp_pre", "norm_mlp_post", "norm_ple")


class NoOverlap:
    sums, from_chips = {}, {}

    def attention_fwd_rider(self):
        return None

    def weights_arrived(self, carried, wl, wtl):
        raise NotImplementedError

    def mlp_bwd_rider(self, layer1_grads):
        return None

    def after_mlp_bwd(self, carried):
        pass

    def lru_bwd_rider(self, layer0_grads):
        return None

    def attention_bwd_rider(self, carried):
        return None

    def after_attention_bwd(self, carried):
        pass


OUT_SPLIT = (LRU_WIDTH, SSM_WIDTH)


def local_step(x, p, target, w, wl, wtl, comm=NoOverlap()):
    nb, ns, d = x.shape
    t = nb * ns
    h = x.reshape(t, d)
    depth = p.shape[0]
    wl, wtl = list(wl), list(wtl)
    tapes = []
    for i in range(depth):
        even = i % 2 == 0
        tag = f"l{i}_"
        gpre = _row(w["norm_mix_pre"][i])
        rider = comm.attention_fwd_rider() if i == 0 else None
        ys, saved, carried = (_mix_even_fwd if even else _mix_odd_fwd)(h, gpre, w, wl[i], nb, ns, rider)
        if rider is not None:
            wl, wtl = comm.weights_arrived(carried, wl, wtl)
        w_out = wl[i]["w_out"]
        split = OUT_SPLIT[i % 2]
        w_outs = [Part(w_out, 0, split, 0), Part(w_out, split, w_out.shape[0] - split, 0)]
        h1, m = matmul_residual_norm(ys, w_outs, h, _row(w["norm_mix_post"][i]), name=tag + "out_proj")
        a1, = norm_matmul(h1, _row(w["norm_mlp_pre"][i]), [wl[i]["mlp_w1"]], [MXU_DTYPE], name=tag + "mlp_up")
        h2, f = matmul_residual_norm([a1], [wl[i]["mlp_w2"]], h1, _row(w["norm_mlp_post"][i]), name=tag + "mlp_down",
                                     relu2=True)
        pi = p[i].reshape(t, -1)
        h3, gl, emb = ple_forward(h2, pi, wl[i]["ple_w_gate"], wl[i]["ple_w_proj"], _row(w["norm_ple"][i]),
                                  name=tag + "ple")
        tapes.append(dict(h=h, ys=ys, w_outs=w_outs, saved=saved, h1=h1, m=m, a1=a1, h2=h2, f=f, pi=pi, gl=gl,
                          emb=emb))
        h = h3

    loss_row, dh = loss_and_grad(h, target.reshape(t, d), name="loss")
    grads = {}
    norm_grads = {k: [None] * depth for k in NORM_NAMES}
    layer_grads = [None] * depth
    for i in reversed(range(depth)):
        even = i % 2 == 0
        tag = f"l{i}_"
        tp = tapes[i]
        lg = {}
        to_sibling = comm.mlp_bwd_rider(layer_grads[1]) if i == 0 else None
        dh2, dgl, demb, dg = ple_backward(dh, tp["h2"], tp["gl"], tp["emb"], _row(w["norm_ple"][i]),
                                          wtl[i]["ple_w_gate"], name=tag + "ple_bwd")
        norm_grads["norm_ple"][i] = dg
        lg["ple_w_gate"] = weight_grad(tp["h2"], dgl, name=tag + "dw_gate")
        lg["ple_w_proj"] = weight_grad(tp["pi"], demb, name=tag + "dw_proj")
        outs = bwd_through_norm_out(dh2, tp["f"], _row(w["norm_mlp_post"][i]), [wtl[i]["mlp_w2"]], [MXU_DTYPE],
                                    name=tag + "mlp_down_bwd", relu2_of=tp["a1"], rider=to_sibling)
        d_f, (da1,), dg = outs[:3]
        if to_sibling is not None:
            comm.after_mlp_bwd(outs[3])
        norm_grads["norm_mlp_post"][i] = dg
        lg["mlp_w2"] = weight_grad(tp["a1"], d_f, name=tag + "dw2", prologue="relu2")
        gpre = _row(w["norm_mlp_pre"][i])
        dh1, dg = bwd_through_norm_in(dh2, [da1], [wtl[i]["mlp_w1"]], tp["h1"], gpre, name=tag + "mlp_up_bwd")
        norm_grads["norm_mlp_pre"][i] = dg
        lg["mlp_w1"] = weight_grad(tp["h1"], da1, name=tag + "dw1", prologue="rms", gain=gpre)
        wt_out = wtl[i]["w_out"]
        split = tp["w_outs"][0].shape[0]
        dm, dys, dg = bwd_through_norm_out(dh1, tp["m"], _row(w["norm_mix_post"][i]),
                                           [Part(wt_out, 0, split, 1),
                                            Part(wt_out, split, wt_out.shape[1] - split, 1)],
                                           [f32, MXU_DTYPE if even else f32],
                                           name=tag + "out_proj_bwd")
        norm_grads["norm_mix_post"][i] = dg
        lg["w_out"] = jnp.concatenate([weight_grad(y, dm, name=tag + f"dw_out{k}") for k, y in enumerate(tp["ys"])],
                                      axis=0)
        lru_rider = comm.lru_bwd_rider(lg) if i == 0 else None
        gs, wts, mix_grads, carried = (_mix_even_bwd if even else _mix_odd_bwd)(
            dys, tp["saved"], wtl[i], nb, ns, lru_rider, comm.attention_bwd_rider if lru_rider is not None else None)
        if carried is not None:
            comm.after_attention_bwd(carried)
        grads.update(mix_grads)
        gpre = _row(w["norm_mix_pre"][i])
        dh, dg = bwd_through_norm_in(dh1, gs, wts, tp["h"], gpre, name=tag + "in_proj_bwd")
        norm_grads["norm_mix_pre"][i] = dg
        dw_in = weight_grads_of_norm(tp["h"], gpre, gs, name=tag + "dw_in")
        if not even:
            dw_in = [dw_in[0], dw_in[1], dw_in[3][:, :SSM_HEADS], dw_in[2]]
        lg["w_in"] = jnp.concatenate(dw_in, axis=1)
        layer_grads[i] = lg
    for k, v in norm_grads.items():
        grads[k] = jnp.concatenate(v, axis=0)
    return loss_row[0, 0], dh.reshape(nb, ns, d), grads, layer_grads


MESH_ID = pl.DeviceIdType.MESH
ANY = pl.BlockSpec(memory_space=pl.ANY)


def _mesh_pos():
    return lax.axis_index("x"), lax.axis_index("y"), lax.axis_index("c")


def all_gather(shards, name):
    return _run_alone(gather_rider(shards), name)


class Rider:
    def __init__(self, inputs, out_shapes, scratch_shapes, start, finish, middle=None):
        self.inputs, self.out_shapes, self.scratch_shapes = list(inputs), list(out_shapes), list(scratch_shapes)
        self.start, self.finish, self.middle = start, finish, middle


def _run_alone(rider, name):
    ni, no = len(rider.inputs), len(rider.out_shapes)

    def body(*refs):
        args = (refs[:ni], refs[ni:ni + no], refs[ni + no:])
        rider.start(*args)
        if rider.middle is not None:
            rider.middle(*args)
        rider.finish(*args)

    return pl.pallas_call(
        body, name=name, out_shape=rider.out_shapes, in_specs=[ANY] * ni, out_specs=[ANY] * no,
        scratch_shapes=rider.scratch_shapes,
    )(*rider.inputs)


def _ride(rider, body, in_specs, out_specs, out_shape, scratch_shapes, grid):
    in_specs, out_specs, out_shape = list(in_specs), list(out_specs), list(out_shape)
    scratch_shapes = list(scratch_shapes)
    if rider is None:
        return body, in_specs, out_specs, out_shape, scratch_shapes
    n_in, n_out, n_scr = len(in_specs), len(out_specs), len(scratch_shapes)
    ri, ro = len(rider.inputs), len(rider.out_shapes)
    total = math.prod(grid)

    def carrying(*refs):
        ins, r_ins = refs[:n_in], refs[n_in:n_in + ri]
        o0 = n_in + ri
        outs, r_outs = refs[o0:o0 + n_out], refs[o0 + n_out:o0 + n_out + ro]
        s0 = o0 + n_out + ro
        scr, r_scr = refs[s0:s0 + n_scr], refs[s0 + n_scr:]
        step = pl.program_id(0)
        for ax in range(1, len(grid)):
            step = step * grid[ax] + pl.program_id(ax)
        args = (r_ins, r_outs, r_scr)
        pl.when(step == 0)(lambda: rider.start(*args))
        if rider.middle is not None:
            pl.when(step == total // 2)(lambda: rider.middle(*args))
        body(*ins, *outs, *scr)
        pl.when(step == total - 1)(lambda: rider.finish(*args))

    return (carrying, in_specs + [ANY] * ri, out_specs + [ANY] * ro, out_shape + rider.out_shapes,
            scratch_shapes + rider.scratch_shapes)


def gather_rider(shards):
    n = len(shards)

    def parts(x_refs, out_refs, scr):
        send_sems, recv_sems, local_sems = scr
        x, y, c = _mesh_pos()
        chips = [(1 - x, y), (x, 1 - y), (1 - x, 1 - y)]

        def slot(a, px, py, pc):
            return out_refs[a].at[4 * px + 2 * py + pc]

        def copy(a, k, block, to, src=None):
            return pltpu.make_async_remote_copy(
                src_ref=slot(a, *block) if src is None else src, dst_ref=slot(a, *block),
                send_sem=send_sems.at[7 * a + k], recv_sem=recv_sems.at[7 * a + k], device_id=to,
                device_id_type=MESH_ID)

        me, sibling = (x, y, c), (x, y, 1 - c)
        def mine():
            return [pltpu.make_async_copy(x_refs[a], slot(a, *me), local_sems.at[a]) for a in range(n)]

        def first():
            out = []
            for j, chip in enumerate(chips):
                out += [copy(a, 1 + j, me, (*chip, c), src=x_refs[a]) for a in range(n)]
            return out + [copy(a, 0, me, sibling, src=x_refs[a]) for a in range(n)]

        def passed(j):
            return [copy(a, 4 + j, (*chips[j], c), sibling) for a in range(n)]

        return me, sibling, chips, c, copy, mine, first, passed

    def start(x_refs, out_refs, scr):
        _, _, _, _, _, mine, first, _ = parts(x_refs, out_refs, scr)
        for cp in mine() + first():
            cp.start()

    def middle(x_refs, out_refs, scr):
        me, _, chips, c, copy, _, _, passed = parts(x_refs, out_refs, scr)
        for j, chip in enumerate(chips):
            for a, fwd in enumerate(passed(j)):
                copy(a, 1 + j, (*chip, c), me).wait_recv()
                fwd.start()

    def finish(x_refs, out_refs, scr):
        me, sibling, chips, c, copy, mine, first, passed = parts(x_refs, out_refs, scr)
        for a in range(n):
            copy(a, 0, sibling, me).wait_recv()
        for j, chip in enumerate(chips):
            for a in range(n):
                copy(a, 4 + j, (*chip, 1 - c), me).wait_recv()
        for cp in first() + [cp for j in range(len(chips)) for cp in passed(j)]:
            cp.wait_send()
        for cp in mine():
            cp.wait()

    return Rider(shards, [jax.ShapeDtypeStruct((N_DEV,) + s.shape, s.dtype) for s in shards],
                 [pltpu.SemaphoreType.DMA((7 * n,)), pltpu.SemaphoreType.DMA((7 * n,)), pltpu.SemaphoreType.DMA((n,))],
                 start, finish, middle)


def scatter_to_sibling(parts, name):
    return _run_alone(sibling_rider(parts), name)


def sibling_rider(parts):
    n = len(parts)

    def copies(g_refs, out_refs, scr):
        send_sems, recv_sems = scr
        x, y, c = _mesh_pos()
        return [pltpu.make_async_remote_copy(
            src_ref=g_refs[a].at[2 * chip + (1 - c)], dst_ref=out_refs[a].at[chip],
            send_sem=send_sems.at[4 * a + chip], recv_sem=recv_sems.at[4 * a + chip], device_id=(x, y, 1 - c),
            device_id_type=MESH_ID) for a in range(n) for chip in range(4)]

    def start(*refs):
        for cp in copies(*refs):
            cp.start()

    def finish(*refs):
        cps = copies(*refs)
        for cp in cps:
            cp.wait_recv()
        for cp in cps:
            cp.wait_send()

    return Rider(parts, [jax.ShapeDtypeStruct((4,) + p.shape[1:], p.dtype) for p in parts],
                 [pltpu.SemaphoreType.DMA((4 * n,)), pltpu.SemaphoreType.DMA((4 * n,))], start, finish)


def scatter_to_chips(partials, name):
    return _run_alone(chips_rider(partials), name)


def chips_rider(partials):
    n = len(partials)

    def copies(p_refs, out_refs, scr):
        send_sems, recv_sems = scr
        x, y, c = _mesh_pos()
        chips = [(1 - x, y), (x, 1 - y), (1 - x, 1 - y)]
        return [pltpu.make_async_remote_copy(
            src_ref=p_refs[a].at[2 * px + py], dst_ref=out_refs[a].at[j],
            send_sem=send_sems.at[3 * a + j], recv_sem=recv_sems.at[3 * a + j], device_id=(px, py, c),
            device_id_type=MESH_ID) for a in range(n) for j, (px, py) in enumerate(chips)]

    def start(*refs):
        for cp in copies(*refs):
            cp.start()

    def finish(*refs):
        cps = copies(*refs)
        for cp in cps:
            cp.wait_recv()
        for cp in cps:
            cp.wait_send()

    return Rider(partials, [jax.ShapeDtypeStruct((3,) + p.shape[1:], p.dtype) for p in partials],
                 [pltpu.SemaphoreType.DMA((3 * n,)), pltpu.SemaphoreType.DMA((3 * n,))], start, finish)


ICI_DTYPE = jnp.bfloat16
ELEMENTWISE_BLOCK_BYTES = 1 << 20


def _row_tile(rows, cols):
    cap = max(16, ELEMENTWISE_BLOCK_BYTES // (4 * cols))
    best = [t for t in range(16, min(rows, cap) + 1, 16) if rows % t == 0]
    return best[-1] if best else rows


def add_sibling_parts(parts, received, core, name):
    _, r, n = parts.shape
    tr = _row_tile(r, n)

    def body(c_ref, a_ref, b_ref, o_ref, ob_ref):
        s = a_ref[...] + b_ref[...]
        o_ref[...] = s
        ob_ref[...] = s.astype(ob_ref.dtype)

    blk = pl.BlockSpec((1, tr, n), lambda i, j, c_ref: (i, j, 0))
    return pl.pallas_call(
        body, name=name,
        grid_spec=pltpu.PrefetchScalarGridSpec(
            num_scalar_prefetch=1, grid=(4, r // tr),
            in_specs=[pl.BlockSpec((1, tr, n), lambda i, j, c_ref: (2 * i + c_ref[0], j, 0)), blk],
            out_specs=[blk, blk]),
        out_shape=[jax.ShapeDtypeStruct((4, r, n), f32), jax.ShapeDtypeStruct((4, r, n), ICI_DTYPE)],
        compiler_params=_cparams("parallel", "parallel"),
    )(core, parts, received)


def _adamw(w, g, m, v):
    m = ADAM_B1 * m + (1.0 - ADAM_B1) * g
    v = ADAM_B2 * v + (1.0 - ADAM_B2) * jnp.square(g)
    m_hat = m / (1.0 - ADAM_B1 ** ADAM_STEP)
    v_hat = v / (1.0 - ADAM_B2 ** ADAM_STEP)
    delta = -ADAM_LR * (m_hat / (jnp.sqrt(v_hat) + ADAM_EPS) + ADAM_WD * w)
    return delta, m, v


def adamw_sharded(partial, received, chip, w, m, v, name):
    _, r, n = partial.shape

    def body(k_ref, p_ref, r_ref, w_ref, m_ref, v_ref, g_out, d_out, m_out, v_out):
        g = p_ref[0] + r_ref[0].astype(f32)
        g = g + r_ref[1].astype(f32)
        g = g + r_ref[2].astype(f32)
        delta, mn, vn = _adamw(w_ref[...], g, m_ref[...], v_ref[...])
        g_out[...] = g
        d_out[...] = delta
        m_out[...] = mn
        v_out[...] = vn

    tr = _row_tile(r, n)
    flat = pl.BlockSpec((tr, n), lambda j, k_ref: (j, 0))
    return pl.pallas_call(
        body, name=name,
        grid_spec=pltpu.PrefetchScalarGridSpec(
            num_scalar_prefetch=1, grid=(r // tr,),
            in_specs=[pl.BlockSpec((1, tr, n), lambda j, k_ref: (k_ref[0], j, 0)),
                      pl.BlockSpec((3, tr, n), lambda j, k_ref: (0, j, 0)), flat, flat, flat],
            out_specs=[flat] * 4),
        out_shape=[jax.ShapeDtypeStruct((r, n), f32)] * 4,
        compiler_params=_cparams("parallel"),
    )(chip, partial, received, w, m, v)


def adamw_replicated(gathered, w, m, v, name):
    _, r, n = gathered.shape

    def body(g_ref, w_ref, m_ref, v_ref, g_out, d_out, m_out, v_out):
        g = g_ref[0]
        for k in range(1, N_DEV):
            g = g + g_ref[k]
        delta, mn, vn = _adamw(w_ref[...], g, m_ref[...], v_ref[...])
        g_out[...] = g
        d_out[...] = delta
        m_out[...] = mn
        v_out[...] = vn

    return pl.pallas_call(
        body, name=name,
        out_shape=[jax.ShapeDtypeStruct((r, n), f32)] * 4,
        compiler_params=pltpu.CompilerParams(vmem_limit_bytes=VMEM_LIMIT),
    )(gathered, w, m, v)


W_NAMES = ['ev_w_in', 'ev_lru_conv_w', 'ev_lru_conv_b', 'ev_lru_gate_a_w', 'ev_lru_gate_a_b', 'ev_lru_gate_x_w',
           'ev_lru_gate_x_b', 'ev_lru_lambda', 'ev_w_out', 'od_w_in', 'od_ssm_conv_w', 'od_ssm_conv_b',
           'od_ssm_dt_bias', 'od_ssm_a_log', 'od_ssm_d', 'od_ssm_norm', 'od_cm_conv_w', 'od_cm_conv_b', 'od_cm_ln_g',
           'od_cm_ln_b', 'od_w_out', 'norm_mix_pre', 'norm_mix_post', 'norm_mlp_pre', 'norm_mlp_post', 'norm_ple',
           'mlp_w1', 'mlp_w2', 'ple_w_proj', 'ple_w_gate']
BIG_SHARDED = {'ev_w_in': 2, 'ev_w_out': 1, 'od_w_in': 2, 'od_w_out': 1, 'mlp_w1': 2, 'mlp_w2': 1, 'ple_w_proj': 2,
               'ple_w_gate': 1}
SMALL_SHARDED = {'ev_lru_conv_w': 2, 'od_ssm_conv_w': 2, 'od_ssm_conv_b': 1, 'od_ssm_norm': 1, 'od_cm_conv_w': 2,
                 'od_cm_conv_b': 1, 'od_cm_ln_g': 1, 'od_cm_ln_b': 1}
SHARDED = {**BIG_SHARDED, **SMALL_SHARDED}
REPLICATED = [n for n in W_NAMES if n not in SHARDED]


def _round_up(n, k):
    return -(-n // k) * k


def _pack_rows(flat, rows_multiple):
    n = flat.shape[0]
    total = _round_up(n, LANES * rows_multiple)
    return jnp.pad(flat, (0, total - n)).reshape(-1, LANES)


def _unpack(flat, shapes):
    out, off = {}, 0
    for name, shape in shapes.items():
        size = math.prod(shape)
        out[name] = flat[off:off + size].reshape(shape)
        off += size
    return out


def _rows(a):
    return a.reshape(-1, a.shape[-1])


def _unshard(g8, shape, axis):
    g = jnp.moveaxis(g8.reshape((N_DEV,) + tuple(shape)), 0, axis)
    return g.reshape(tuple(shape[:axis]) + (N_DEV * shape[axis],) + tuple(shape[axis + 1:]))


def _to_shards(g, axis):
    shard = g.shape[axis] // N_DEV
    g = g.reshape(g.shape[:axis] + (N_DEV, shard) + g.shape[axis + 1:])
    return jnp.moveaxis(g, axis, 0)


def _pack_small(tree):
    return _pack_rows(jnp.concatenate([tree[k].astype(f32).reshape(-1) for k in SMALL_SHARDED]), 16)


def _layer_entry(i, key):
    mixer = "ev" if i % 2 == 0 else "od"
    return {"w_in": (mixer + "_w_in", i // 2, 1), "w_out": (mixer + "_w_out", i // 2, 0),
            "mlp_w1": ("mlp_w1", i, 1), "mlp_w2": ("mlp_w2", i, 0),
            "ple_w_proj": ("ple_w_proj", i, 1), "ple_w_gate": ("ple_w_gate", i, 0)}[key]


def _layer_shards(tree, i):
    out = {}
    for key in LAYER_MATRICES:
        name, idx, _ = _layer_entry(i, key)
        out[key] = tree[name][idx]
    return out


def _assemble_layer(i, gathered):
    wl = {key: _unshard(g8, g8.shape[1:], _layer_entry(i, key)[2]) for key, g8 in zip(LAYER_MATRICES, gathered)}
    return wl, {key: v.T for key, v in wl.items()}


def _gather_early(w):
    small = _pack_small(w)
    terms, rest = [], small
    for _ in range(3):
        term = rest.astype(MXU_DTYPE)
        terms.append(term)
        rest = rest - term.astype(f32)
    outs = all_gather([_layer_shards(w, 0)["w_in"].astype(MXU_DTYPE), jnp.concatenate(terms, axis=0)],
                      name="gather_weights")
    w_in = _unshard(outs[0], outs[0].shape[1:], _layer_entry(0, "w_in")[2])
    wl, wtl = {"w_in": w_in}, {"w_in": w_in.T}
    full = {k: w[k] for k in REPLICATED}
    t = outs[-1].astype(f32)
    nr = small.shape[0]
    vals = (t[:, :nr] + t[:, nr:2 * nr] + t[:, 2 * nr:]).reshape(N_DEV, -1)
    off = 0
    for k, axis in SMALL_SHARDED.items():
        size = math.prod(w[k].shape)
        full[k] = _unshard(vals[:, off:off + size], w[k].shape, axis)
        off += size
    return full, wl, wtl


class Overlap:
    LATE = [(0, key) for key in LAYER_MATRICES if key != "w_in"] + [(1, key) for key in LAYER_MATRICES]
    EARLY_GRADS = [(0, key) for key in LAYER_MATRICES if key != "w_in"]

    def __init__(self, w):
        self.w = w
        self.sums, self.from_chips, self.parts = {}, {}, {}

    def attention_fwd_rider(self):
        shards = [_layer_shards(self.w, 0), _layer_shards(self.w, 1)]
        return gather_rider([shards[i][key].astype(MXU_DTYPE) for i, key in self.LATE])

    def weights_arrived(self, carried, wl, wtl):
        wl = [dict(wl[0]), {}]
        wtl = [dict(wtl[0]), {}]
        for (i, key), g8 in zip(self.LATE, carried):
            wl[i][key] = _unshard(g8, g8.shape[1:], _layer_entry(i, key)[2])
            wtl[i][key] = wl[i][key].T
        return wl, wtl

    def _to_sibling(self, ids, layer_grads):
        for i, key in ids:
            self.parts[i, key] = _to_shards(layer_grads[key], _layer_entry(i, key)[2])
        return sibling_rider([self.parts[e] for e in ids])

    def _add(self, ids, carried):
        core = jnp.reshape(lax.axis_index("c"), (1,)).astype(jnp.int32)
        for (i, key), got in zip(ids, carried):
            self.sums[i, key] = add_sibling_parts(self.parts[i, key], got, core, name=f"add_sibling_l{i}_{key}")

    def mlp_bwd_rider(self, layer1_grads):
        return self._to_sibling([(1, key) for key in LAYER_MATRICES], layer1_grads)

    def after_mlp_bwd(self, carried):
        self._add([(1, key) for key in LAYER_MATRICES], carried)

    def lru_bwd_rider(self, layer0_grads):
        return self._to_sibling(self.EARLY_GRADS, layer0_grads)

    def attention_bwd_rider(self, carried):
        self._add(self.EARLY_GRADS, carried)
        self.travelling = list(self.sums)
        return chips_rider([self.sums[e][1] for e in self.travelling])

    def after_attention_bwd(self, carried):
        for e, got in zip(self.travelling, carried):
            self.from_chips[e] = got


def _pack_replicated(tree):
    return _pack_rows(jnp.concatenate([tree[k].astype(f32).reshape(-1) for k in REPLICATED]), 8)


def kernel(x, p, ev_w_in, ev_lru_conv_w, ev_lru_conv_b, ev_lru_gate_a_w, ev_lru_gate_a_b, ev_lru_gate_x_w, ev_lru_gate_x_b, ev_lru_lambda, ev_w_out, od_w_in, od_ssm_conv_w, od_ssm_conv_b, od_ssm_dt_bias, od_ssm_a_log, od_ssm_d, od_ssm_norm, od_cm_conv_w, od_cm_conv_b, od_cm_ln_g, od_cm_ln_b, od_w_out, norm_mix_pre, norm_mix_post, norm_mlp_pre, norm_mlp_post, norm_ple, mlp_w1, mlp_w2, ple_w_proj, ple_w_gate, loss_target, m_ev_w_in, m_ev_lru_conv_w, m_ev_lru_conv_b, m_ev_lru_gate_a_w, m_ev_lru_gate_a_b, m_ev_lru_gate_x_w, m_ev_lru_gate_x_b, m_ev_lru_lambda, m_ev_w_out, m_od_w_in, m_od_ssm_conv_w, m_od_ssm_conv_b, m_od_ssm_dt_bias, m_od_ssm_a_log, m_od_ssm_d, m_od_ssm_norm, m_od_cm_conv_w, m_od_cm_conv_b, m_od_cm_ln_g, m_od_cm_ln_b, m_od_w_out, m_norm_mix_pre, m_norm_mix_post, m_norm_mlp_pre, m_norm_mlp_post, m_norm_ple, m_mlp_w1, m_mlp_w2, m_ple_w_proj, m_ple_w_gate, v_ev_w_in, v_ev_lru_conv_w, v_ev_lru_conv_b, v_ev_lru_gate_a_w, v_ev_lru_gate_a_b, v_ev_lru_gate_x_w, v_ev_lru_gate_x_b, v_ev_lru_lambda, v_ev_w_out, v_od_w_in, v_od_ssm_conv_w, v_od_ssm_conv_b, v_od_ssm_dt_bias, v_od_ssm_a_log, v_od_ssm_d, v_od_ssm_norm, v_od_cm_conv_w, v_od_cm_conv_b, v_od_cm_ln_g, v_od_cm_ln_b, v_od_w_out, v_norm_mix_pre, v_norm_mix_post, v_norm_mlp_pre, v_norm_mlp_post, v_norm_ple, v_mlp_w1, v_mlp_w2, v_ple_w_proj, v_ple_w_gate):
    ws = [ev_w_in, ev_lru_conv_w, ev_lru_conv_b, ev_lru_gate_a_w, ev_lru_gate_a_b, ev_lru_gate_x_w, ev_lru_gate_x_b, ev_lru_lambda, ev_w_out, od_w_in, od_ssm_conv_w, od_ssm_conv_b, od_ssm_dt_bias, od_ssm_a_log, od_ssm_d, od_ssm_norm, od_cm_conv_w, od_cm_conv_b, od_cm_ln_g, od_cm_ln_b, od_w_out, norm_mix_pre, norm_mix_post, norm_mlp_pre, norm_mlp_post, norm_ple, mlp_w1, mlp_w2, ple_w_proj, ple_w_gate]
    ms = [m_ev_w_in, m_ev_lru_conv_w, m_ev_lru_conv_b, m_ev_lru_gate_a_w, m_ev_lru_gate_a_b, m_ev_lru_gate_x_w, m_ev_lru_gate_x_b, m_ev_lru_lambda, m_ev_w_out, m_od_w_in, m_od_ssm_conv_w, m_od_ssm_conv_b, m_od_ssm_dt_bias, m_od_ssm_a_log, m_od_ssm_d, m_od_ssm_norm, m_od_cm_conv_w, m_od_cm_conv_b, m_od_cm_ln_g, m_od_cm_ln_b, m_od_w_out, m_norm_mix_pre, m_norm_mix_post, m_norm_mlp_pre, m_norm_mlp_post, m_norm_ple, m_mlp_w1, m_mlp_w2, m_ple_w_proj, m_ple_w_gate]
    vs = [v_ev_w_in, v_ev_lru_conv_w, v_ev_lru_conv_b, v_ev_lru_gate_a_w, v_ev_lru_gate_a_b, v_ev_lru_gate_x_w, v_ev_lru_gate_x_b, v_ev_lru_lambda, v_ev_w_out, v_od_w_in, v_od_ssm_conv_w, v_od_ssm_conv_b, v_od_ssm_dt_bias, v_od_ssm_a_log, v_od_ssm_d, v_od_ssm_norm, v_od_cm_conv_w, v_od_cm_conv_b, v_od_cm_ln_g, v_od_cm_ln_b, v_od_w_out, v_norm_mix_pre, v_norm_mix_post, v_norm_mlp_pre, v_norm_mlp_post, v_norm_ple, v_mlp_w1, v_mlp_w2, v_ple_w_proj, v_ple_w_gate]
    w = dict(zip(W_NAMES, ws))
    m = dict(zip(W_NAMES, ms))
    v = dict(zip(W_NAMES, vs))
    full, wl0, wtl0 = _gather_early(w)
    comm = Overlap(w)
    loss_local, grad_x, grads, layer_grads = local_step(x, p, loss_target, full, [wl0, None], [wtl0, None], comm)
    loss = lax.psum(loss_local, ("x", "y", "c"))
    return (loss, grad_x, *_reduce_and_update(grads, layer_grads, w, m, v, comm))


def _reduce_and_update(grads, layer_grads, w, m, v, comm):
    mx, my, mc = _mesh_pos()

    entries = [(i, key) for i in range(len(layer_grads)) for key in LAYER_MATRICES]
    left = [e for e in entries if e not in comm.from_chips]
    parts = [_to_shards(layer_grads[i][key], _layer_entry(i, key)[2]) for i, key in left]
    small = jnp.concatenate([_to_shards(grads[k], axis).reshape(N_DEV, -1) for k, axis in SMALL_SHARDED.items()],
                            axis=1)
    small_rows = _pack_small(w).shape[0]
    small = jnp.pad(small, ((0, 0), (0, small_rows * LANES - small.shape[1]))).reshape(N_DEV, small_rows, LANES)
    parts.append(small)
    from_sibling = scatter_to_sibling(parts, name="scatter_sibling")
    core = jnp.reshape(mc, (1,)).astype(jnp.int32)
    sums = [add_sibling_parts(a, b, core, name=f"add_sibling_{i}") for i, (a, b) in enumerate(zip(parts, from_sibling))]
    from_chips = scatter_to_chips([s[1] for s in sums], name="scatter_chips")
    all_sums = {**comm.sums, **dict(zip(left, sums[:-1]))}
    all_from_chips = {**comm.from_chips, **dict(zip(left, from_chips[:-1]))}
    chip = jnp.reshape(2 * mx + my, (1,)).astype(jnp.int32)
    per_layer = []
    for i in range(len(layer_grads)):
        ws, ms, vs = _layer_shards(w, i), _layer_shards(m, i), _layer_shards(v, i)
        per_layer.append({key: adamw_sharded(all_sums[i, key][0], all_from_chips[i, key], chip, ws[key], ms[key],
                                             vs[key], name=f"adamw_l{i}_{key}") for key in LAYER_MATRICES})
    g_sh, d_sh, m_sh, v_sh = {}, {}, {}, {}
    for which, tree in enumerate((g_sh, d_sh, m_sh, v_sh)):
        for i in range(len(per_layer)):
            for key in LAYER_MATRICES:
                name, idx, _ = _layer_entry(i, key)
                tree.setdefault(name, {})[idx] = per_layer[i][key][which]
        for name in BIG_SHARDED:
            tree[name] = jnp.stack([tree[name][idx] for idx in sorted(tree[name])], axis=0)
    outs = adamw_sharded(sums[-1][0], from_chips[-1], chip, _pack_small(w), _pack_small(m), _pack_small(v),
                         name="adamw_small")
    small_shapes = {k: w[k].shape for k in SMALL_SHARDED}
    for tree, o in zip((g_sh, d_sh, m_sh, v_sh), outs):
        tree.update(_unpack(o.reshape(-1), small_shapes))

    rep_parts, = all_gather([_pack_replicated(grads)], name="gather_replicated_grads")
    outs = adamw_replicated(rep_parts, _pack_replicated(w), _pack_replicated(m), _pack_replicated(v),
                            name="adamw_replicated")
    rep_shapes = {k: w[k].shape for k in REPLICATED}
    g_rp, d_rp, m_rp, v_rp = [_unpack(o.reshape(-1), rep_shapes) for o in outs]

    pick = lambda sh, rp: [sh[k] if k in SHARDED else rp[k] for k in W_NAMES]
    return [*pick(g_sh, g_rp), *pick(d_sh, d_rp), *pick(m_sh, m_rp), *pick(v_sh, v_rp)]
```

```python
import math

import jax
import jax.numpy as jnp
from jax import lax
from jax.experimental import pallas as pl
from jax.experimental.pallas import tpu as pltpu

f32 = jnp.float32
bf16 = jnp.bfloat16
MXU_DTYPE = jnp.bfloat16

D_MODEL = 1024
EPS = 1e-6
LRU_WIDTH = 512
LRU_HEADS = 8
LRU_CONV = 4
LRU_C = 8.0
SB_WIDTH = 512
SB_HEAD_DIM = 64
SSM_WIDTH = 1024
SSM_HEADS = 16
SSM_HEAD_DIM = 64
SSM_GROUPS = 2
SSM_STATE = 128
SSM_CONV = 4
SSM_CHUNK = 128
SSM_XBC = SSM_WIDTH + 2 * SSM_GROUPS * SSM_STATE
CONF_WIDTH = 512
CONF_KERNEL = 31
MLP_HIDDEN = 4096
PLE_DIM = 256
LANES = 128
N_DEV = 8

ADAM_LR = 0.001
ADAM_B1 = 0.9
ADAM_B2 = 0.999
ADAM_EPS = 1e-08
ADAM_WD = 0.01
ADAM_STEP = 10

VMEM_LIMIT = 56 * 1024 * 1024


def _cparams(*sem):
    return pltpu.CompilerParams(dimension_semantics=sem, vmem_limit_bytes=VMEM_LIMIT)


def _mm(a, b):
    return jnp.dot(a.astype(MXU_DTYPE), b.astype(MXU_DTYPE), preferred_element_type=f32)


def _mm_nt(a, b):
    return lax.dot_general(a.astype(MXU_DTYPE), b.astype(MXU_DTYPE), (((1,), (1,)), ((), ())),
                           preferred_element_type=f32)


def _mm_tn(a, b):
    return lax.dot_general(a.astype(MXU_DTYPE), b.astype(MXU_DTYPE), (((0,), (0,)), ((), ())),
                           preferred_element_type=f32)


def _mm_exact(a, b):
    return jnp.dot(a, b, preferred_element_type=f32, precision=lax.Precision.HIGHEST)


@jax.custom_vjp
def dmm(a, b):
    return _mm(a, b)


def _dmm_fwd(a, b):
    return _mm(a, b), (a, b)


def _dmm_bwd(res, g):
    a, b = res
    return _mm_nt(g, b), _mm_tn(a, g)


dmm.defvjp(_dmm_fwd, _dmm_bwd)


@jax.custom_vjp
def dmm_nt(a, b):
    return _mm_nt(a, b)


def _dmm_nt_fwd(a, b):
    return _mm_nt(a, b), (a, b)


def _dmm_nt_bwd(res, g):
    a, b = res
    return _mm(g, b), _mm_tn(g, a)


dmm_nt.defvjp(_dmm_nt_fwd, _dmm_nt_bwd)


@jax.custom_vjp
def dmm_tn(a, b):
    return _mm_tn(a, b)


def _dmm_tn_fwd(a, b):
    return _mm_tn(a, b), (a, b)


def _dmm_tn_bwd(res, g):
    a, b = res
    return _mm_nt(b, g), _mm(a, g)


dmm_tn.defvjp(_dmm_tn_fwd, _dmm_tn_bwd)


def _rms(x, g):
    r = lax.rsqrt(jnp.mean(x * x, axis=-1, keepdims=True) + EPS)
    return x * r * g


def _rms_bwd(dy, x, g):
    r = lax.rsqrt(jnp.mean(x * x, axis=-1, keepdims=True) + EPS)
    dyg = dy * g
    dx = r * dyg - x * (r * r * r * jnp.mean(dyg * x, axis=-1, keepdims=True))
    return dx, dy * x * r


def _tok(tm, n):
    return pl.BlockSpec((tm, n), lambda i: (i, 0))


def _whole(shape):
    nd = len(shape)
    return pl.BlockSpec(tuple(shape), lambda i: (0,) * nd)


def _acc_rows(ref, val):
    s = jnp.sum(val, axis=0, keepdims=True)

    @pl.when(pl.program_id(0) == 0)
    def _():
        ref[...] = s

    @pl.when(pl.program_id(0) != 0)
    def _():
        ref[...] += s


TOKEN_TILE = 512
WEIGHT_GRAD_TOKENS = 1024


class Part:
    def __init__(self, whole, start, size, axis):
        self.whole, self.start, self.size, self.axis = whole, start, size, axis
        self.shape = tuple(size if a == axis else n for a, n in enumerate(whole.shape))


def _weights(ws):
    wholes, readers = [], []
    for w in ws:
        arr = w.whole if isinstance(w, Part) else w
        idx = next((i for i, a in enumerate(wholes) if a is arr), None)
        if idx is None:
            wholes.append(arr)
            idx = len(wholes) - 1
        if isinstance(w, Part):
            rows = pl.ds(w.start, w.size) if w.axis == 0 else slice(None)
            cols = pl.ds(w.start, w.size) if w.axis == 1 else slice(None)
            readers.append(lambda refs, idx=idx, rows=rows, cols=cols: refs[idx][rows, cols])
        else:
            readers.append(lambda refs, idx=idx: refs[idx][...])
    return wholes, readers


def norm_matmul(h, g, ws, out_dtypes, name):
    t, d = h.shape
    tm = TOKEN_TILE
    wholes, readers = _weights(ws)
    nw = len(wholes)

    def body(h_ref, g_ref, *refs):
        hn = _rms(h_ref[...], g_ref[...]).astype(MXU_DTYPE)
        for read, o_ref in zip(readers, refs[nw:]):
            o_ref[...] = jnp.dot(hn, read(refs[:nw]), preferred_element_type=f32).astype(o_ref.dtype)

    return pl.pallas_call(
        body, name=name, grid=(t // tm,),
        in_specs=[_tok(tm, d), _whole(g.shape)] + [_whole(w.shape) for w in wholes],
        out_specs=[_tok(tm, w.shape[1]) for w in ws],
        out_shape=[jax.ShapeDtypeStruct((t, w.shape[1]), dt) for w, dt in zip(ws, out_dtypes)],
        compiler_params=_cparams("parallel"),
    )(h, g, *wholes)


def matmul_residual_norm(xs, ws, h, g, name, relu2=False):
    t, d = h.shape
    tm = TOKEN_TILE
    nx = len(xs)
    wholes, readers = _weights(ws)
    nw = len(wholes)

    def body(*refs):
        x_refs, w_refs = refs[:nx], refs[nx:nx + nw]
        h_ref, g_ref, ho_ref, m_ref = refs[nx + nw:]
        m = None
        for x_ref, read in zip(x_refs, readers):
            x = x_ref[...]
            if relu2:
                x = jnp.square(jnp.maximum(x.astype(f32), 0.0))
            part = jnp.dot(x.astype(MXU_DTYPE), read(w_refs), preferred_element_type=f32)
            m = part if m is None else m + part
        m_ref[...] = m.astype(m_ref.dtype)
        ho_ref[...] = h_ref[...] + _rms(m, g_ref[...])

    return pl.pallas_call(
        body, name=name, grid=(t // tm,),
        in_specs=[_tok(tm, x.shape[1]) for x in xs] + [_whole(w.shape) for w in wholes]
        + [_tok(tm, d), _whole(g.shape)],
        out_specs=[_tok(tm, d), _tok(tm, d)],
        out_shape=[jax.ShapeDtypeStruct((t, d), f32), jax.ShapeDtypeStruct((t, d), MXU_DTYPE)],
        compiler_params=_cparams("parallel"),
    )(*xs, *wholes, h, g)


def ple_forward(h, p, w_gate, w_proj, g, name):
    t, d = h.shape
    tm = TOKEN_TILE

    def body(h_ref, p_ref, wg_ref, wp_ref, g_ref, ho_ref, gl_ref, emb_ref):
        hh = h_ref[...]
        gl = jnp.dot(hh.astype(MXU_DTYPE), wg_ref[...], preferred_element_type=f32)
        emb = jnp.dot(p_ref[...].astype(MXU_DTYPE), wp_ref[...], preferred_element_type=f32)
        gl_ref[...] = gl.astype(gl_ref.dtype)
        emb_ref[...] = emb.astype(emb_ref.dtype)
        ho_ref[...] = hh + _rms(jax.nn.sigmoid(gl) * emb, g_ref[...])

    return pl.pallas_call(
        body, name=name, grid=(t // tm,),
        in_specs=[_tok(tm, d), _tok(tm, p.shape[1]), _whole(w_gate.shape), _whole(w_proj.shape), _whole(g.shape)],
        out_specs=[_tok(tm, d)] * 3,
        out_shape=[jax.ShapeDtypeStruct((t, d), f32)] + [jax.ShapeDtypeStruct((t, d), MXU_DTYPE)] * 2,
        compiler_params=_cparams("parallel"),
    )(h, p, w_gate, w_proj, g)


def loss_and_grad(h, target, name):
    t, d = h.shape
    tm = TOKEN_TILE

    def body(h_ref, t_ref, l_ref, dh_ref):
        e = h_ref[...] - t_ref[...]
        dh_ref[...] = e * (1.0 / d)
        part = jnp.sum(jnp.sum(e * e, axis=1, keepdims=True), axis=0, keepdims=True) * (0.5 / d)
        _acc_rows(l_ref, jnp.broadcast_to(part, (1, LANES)))

    return pl.pallas_call(
        body, name=name, grid=(t // tm,),
        in_specs=[_tok(tm, d), _tok(tm, d)],
        out_specs=[_whole((1, LANES)), _tok(tm, d)],
        out_shape=[jax.ShapeDtypeStruct((1, LANES), f32), jax.ShapeDtypeStruct((t, d), f32)],
        compiler_params=_cparams("arbitrary"),
    )(h, target)


def bwd_through_norm_in(dh, gs, wts, h, g, name):
    t, d = h.shape
    tm = TOKEN_TILE
    ng = len(gs)
    wholes, readers = _weights(wts)
    nw = len(wholes)

    def body(*refs):
        dh_ref = refs[0]
        g_refs, w_refs = refs[1:1 + ng], refs[1 + ng:1 + ng + nw]
        h_ref, gain_ref, dho_ref, dg_ref = refs[1 + ng + nw:]
        dhn = None
        for g_ref, read in zip(g_refs, readers):
            part = jnp.dot(g_ref[...].astype(MXU_DTYPE), read(w_refs), preferred_element_type=f32)
            dhn = part if dhn is None else dhn + part
        dx, dgr = _rms_bwd(dhn, h_ref[...], gain_ref[...])
        dho_ref[...] = dh_ref[...] + dx
        _acc_rows(dg_ref, dgr)

    return pl.pallas_call(
        body, name=name, grid=(t // tm,),
        in_specs=[_tok(tm, d)] + [_tok(tm, x.shape[1]) for x in gs] + [_whole(w.shape) for w in wholes]
        + [_tok(tm, d), _whole(g.shape)],
        out_specs=[_tok(tm, d), _whole((1, d))],
        out_shape=[jax.ShapeDtypeStruct((t, d), f32), jax.ShapeDtypeStruct((1, d), f32)],
        compiler_params=_cparams("arbitrary"),
    )(dh, *gs, *wholes, h, g)


def bwd_through_norm_out(dh, n, g, wts, out_dtypes, name, relu2_of=None, rider=None):
    t, d = n.shape
    tm = TOKEN_TILE
    nw = len(wts)
    wholes, readers = _weights(wts)
    nwh = len(wholes)
    has_a = relu2_of is not None

    def body(*refs):
        dh_ref, n_ref, gain_ref = refs[:3]
        w_refs = refs[3:3 + nwh]
        rest = refs[3 + nwh:]
        if has_a:
            a_ref, rest = rest[0], rest[1:]
        dn_ref, dx_refs, dg_ref = rest[0], rest[1:1 + nw], rest[1 + nw]
        dn, dgr = _rms_bwd(dh_ref[...], n_ref[...].astype(f32), gain_ref[...])
        dnb = dn.astype(MXU_DTYPE)
        dn_ref[...] = dnb.astype(dn_ref.dtype)
        for read, dx_ref in zip(readers, dx_refs):
            dx = jnp.dot(dnb, read(w_refs), preferred_element_type=f32)
            if has_a:
                dx = dx * (2.0 * jnp.maximum(a_ref[...].astype(f32), 0.0))
            dx_ref[...] = dx.astype(dx_ref.dtype)
        _acc_rows(dg_ref, dgr)

    ins = [dh, n, g, *wholes] + ([relu2_of] if has_a else [])
    in_specs = [_tok(tm, d), _tok(tm, d), _whole(g.shape)] + [_whole(w.shape) for w in wholes]
    if has_a:
        in_specs.append(_tok(tm, relu2_of.shape[1]))
    grid = (t // tm,)
    body, in_specs, out_specs, out_shape, scratch = _ride(
        rider, body, in_specs, [_tok(tm, d)] + [_tok(tm, w.shape[1]) for w in wts] + [_whole((1, d))],
        [jax.ShapeDtypeStruct((t, d), MXU_DTYPE)]
        + [jax.ShapeDtypeStruct((t, w.shape[1]), dt) for w, dt in zip(wts, out_dtypes)]
        + [jax.ShapeDtypeStruct((1, d), f32)], [], grid)
    outs = pl.pallas_call(
        body, name=name, grid=grid, in_specs=in_specs, out_specs=out_specs, out_shape=out_shape,
        scratch_shapes=scratch, compiler_params=_cparams("arbitrary"),
    )(*ins, *(rider.inputs if rider else []))
    if rider:
        return outs[0], list(outs[1:1 + nw]), outs[1 + nw], list(outs[2 + nw:])
    return outs[0], list(outs[1:1 + nw]), outs[1 + nw]


def ple_backward(dh3, h2, gl, emb, g, w_gate_t, name):
    t, d = h2.shape
    tm = TOKEN_TILE

    def body(dh_ref, gl_ref, emb_ref, gain_ref, wt_ref, dho_ref, dgl_ref, demb_ref, dg_ref):
        gate = jax.nn.sigmoid(gl_ref[...].astype(f32))
        emb = emb_ref[...].astype(f32)
        dge, dgr = _rms_bwd(dh_ref[...], gate * emb, gain_ref[...])
        demb_ref[...] = (dge * gate).astype(demb_ref.dtype)
        dgl = (dge * emb * gate * (1.0 - gate)).astype(MXU_DTYPE)
        dgl_ref[...] = dgl.astype(dgl_ref.dtype)
        dho_ref[...] = dh_ref[...] + jnp.dot(dgl, wt_ref[...], preferred_element_type=f32)
        _acc_rows(dg_ref, dgr)

    return pl.pallas_call(
        body, name=name, grid=(t // tm,),
        in_specs=[_tok(tm, d), _tok(tm, d), _tok(tm, d), _whole(g.shape), _whole(w_gate_t.shape)],
        out_specs=[_tok(tm, d), _tok(tm, d), _tok(tm, d), _whole((1, d))],
        out_shape=[jax.ShapeDtypeStruct((t, d), f32), jax.ShapeDtypeStruct((t, d), MXU_DTYPE),
                   jax.ShapeDtypeStruct((t, d), MXU_DTYPE), jax.ShapeDtypeStruct((1, d), f32)],
        compiler_params=_cparams("arbitrary"),
    )(dh3, gl, emb, g, w_gate_t)


def _largest_tile(n, cap):
    if n <= cap:
        return n
    return max(c for c in range(LANES, cap + 1, LANES) if n % c == 0)


def weight_grad(x, gout, name, prologue="none", gain=None):
    t, k = x.shape
    n = gout.shape[1]
    tt = WEIGHT_GRAD_TOKENS
    tn = _largest_tile(n, 1024)
    tk = k if prologue == "rms" else _largest_tile(k, 1024)
    has_gain = prologue == "rms"

    def body(*refs):
        if has_gain:
            x_ref, gain_ref, g_ref, o_ref = refs
        else:
            x_ref, g_ref, o_ref = refs
        x = x_ref[...].astype(f32)
        if prologue == "relu2":
            x = jnp.square(jnp.maximum(x, 0.0))
        elif prologue == "rms":
            x = _rms(x, gain_ref[...])
        part = _mm_tn(x, g_ref[...])

        @pl.when(pl.program_id(2) == 0)
        def _():
            o_ref[...] = part

        @pl.when(pl.program_id(2) != 0)
        def _():
            o_ref[...] += part

    in_specs = [pl.BlockSpec((tt, tk), lambda i, j, s: (s, i))]
    ins = [x]
    if has_gain:
        in_specs.append(pl.BlockSpec(gain.shape, lambda i, j, s: (0, 0)))
        ins.append(gain)
    in_specs.append(pl.BlockSpec((tt, tn), lambda i, j, s: (s, j)))
    ins.append(gout)
    return pl.pallas_call(
        body, name=name, grid=(k // tk, n // tn, t // tt),
        in_specs=in_specs,
        out_specs=pl.BlockSpec((tk, tn), lambda i, j, s: (i, j)),
        out_shape=jax.ShapeDtypeStruct((k, n), f32),
        compiler_params=_cparams("parallel", "parallel", "arbitrary"),
    )(*ins)


def weight_grads_of_norm(x, gain, gouts, name):
    t, k = x.shape
    tt = TOKEN_TILE
    ng = len(gouts)

    def body(x_ref, gain_ref, *refs):
        xn = _rms(x_ref[...], gain_ref[...]).astype(MXU_DTYPE)
        first = pl.program_id(0) == 0
        for g_ref, o_ref in zip(refs[:ng], refs[ng:]):
            _accum(o_ref, _mm_tn(xn, g_ref[...]), first)

    return pl.pallas_call(
        body, name=name, grid=(t // tt,),
        in_specs=[_tok(tt, k), _whole(gain.shape)] + [_tok(tt, g.shape[1]) for g in gouts],
        out_specs=[_whole((k, g.shape[1])) for g in gouts],
        out_shape=[jax.ShapeDtypeStruct((k, g.shape[1]), f32) for g in gouts],
        compiler_params=_cparams("arbitrary"),
    )(x, gain, *gouts)


SEQ_TILE = 256
HALO = 8


def _first_step():
    return jnp.logical_and(pl.program_id(0) == 0, pl.program_id(1) == 0)


def _accum(ref, val, first):
    @pl.when(first)
    def _():
        ref[...] = val

    @pl.when(jnp.logical_not(first))
    def _():
        ref[...] += val


def _softplus(x):
    return jnp.maximum(x, 0.0) + jnp.log1p(jnp.exp(-jnp.abs(x)))


def _neg_expm1(z):
    series = -z * (1.0 + z * (0.5 + z * (1.0 / 6.0 + z * (1.0 / 24.0 + z * (1.0 / 120.0)))))
    return jnp.where(z > -0.05, series, 1.0 - jnp.exp(z))


def _lru_gates(xc, ga, gab, gx, gxb, lam):
    r = jax.nn.sigmoid(dmm(xc, ga) + gab)
    i = jax.nn.sigmoid(dmm(xc, gx) + gxb)
    log_a = -LRU_C * r * _softplus(-lam)
    a = jnp.exp(log_a)
    u = jnp.sqrt(_neg_expm1(2.0 * log_a)) * (i * xc)
    return a, u


def _scan_down(a, u):
    n = a.shape[0]
    rows = lax.broadcasted_iota(jnp.int32, a.shape, 0)
    d = 1
    while d < n:
        keep = rows >= d
        a_s = jnp.where(keep, pltpu.roll(a, d, 0), 1.0)
        u_s = jnp.where(keep, pltpu.roll(u, d, 0), 0.0)
        u = a * u_s + u
        a = a * a_s
        d *= 2
    return a, u


def _scan_up(b, g):
    n = b.shape[0]
    rows = lax.broadcasted_iota(jnp.int32, b.shape, 0)
    d = 1
    while d < n:
        keep = rows < n - d
        b_s = jnp.where(keep, pltpu.roll(b, n - d, 0), 1.0)
        g_s = jnp.where(keep, pltpu.roll(g, n - d, 0), 0.0)
        g = g + b * g_s
        b = b * b_s
        d *= 2
    return g


def _seq_specs(ts, c, nt, reverse=False):
    per = ts // HALO

    def jj(j):
        return (nt - 1 - j) if reverse else j

    tile = pl.BlockSpec((1, ts, c), lambda b, j: (b, jj(j), 0))
    before = pl.BlockSpec((1, HALO, c), lambda b, j: (b, jnp.maximum(jj(j) * per - 1, 0), 0))
    after = pl.BlockSpec((1, HALO, c), lambda b, j: (b, jnp.minimum((jj(j) + 1) * per, nt * per - 1), 0))
    return tile, before, after


def _const2(shape):
    nd = len(shape)
    return pl.BlockSpec(tuple(shape), lambda b, j: (0,) * nd)


def lru_forward(xpre, gate, cw, cb, ga, gab, gx, gxb, lam, name):
    nb, ns, w = xpre.shape
    ts = SEQ_TILE
    nt = ns // ts
    tile, _, _ = _seq_specs(ts, w, nt)

    def body(xp_ref, gt_ref, cw_ref, cb_ref, ga_ref, gab_ref, gx_ref, gxb_ref, lam_ref,
             y_ref, xc_ref, hs_ref, xin, hcar):
        @pl.when(pl.program_id(1) == 0)
        def _():
            xin[0:HALO, :] = jnp.zeros((HALO, w), f32)
            hcar[...] = jnp.zeros_like(hcar)

        xin[HALO:HALO + ts, :] = xp_ref[0]
        xc = jnp.broadcast_to(cb_ref[...], (ts, w))
        for k in range(LRU_CONV):
            xc = xc + cw_ref[k:k + 1, :] * xin[pl.ds(HALO - LRU_CONV + 1 + k, ts), :]
        xin[0:HALO, :] = xin[ts:ts + HALO, :]
        a, u = _lru_gates(xc, ga_ref[...], gab_ref[...], gx_ref[...], gxb_ref[...], lam_ref[...])
        acum, h = _scan_down(a, u)
        h = h + acum * hcar[0:1, :]
        hcar[0:1, :] = h[ts - 1:ts, :]
        xc_ref[0] = xc
        hs_ref[0] = h
        y_ref[0] = (h * jax.nn.gelu(gt_ref[0])).astype(y_ref.dtype)

    params = [cw, cb, ga, gab, gx, gxb, lam]
    return pl.pallas_call(
        body, name=name, grid=(nb, nt),
        in_specs=[tile, tile] + [_const2(p.shape) for p in params],
        out_specs=[tile, tile, tile],
        out_shape=[jax.ShapeDtypeStruct((nb, ns, w), MXU_DTYPE), jax.ShapeDtypeStruct((nb, ns, w), f32),
                   jax.ShapeDtypeStruct((nb, ns, w), f32)],
        scratch_shapes=[pltpu.VMEM((ts + HALO, w), f32), pltpu.VMEM((HALO, w), f32)],
        compiler_params=_cparams("arbitrary", "arbitrary"),
    )(xpre, gate, *params)


def lru_backward(dy, xpre, gate, xc, hs, cw, cb, ga, gab, gx, gxb, lam, name, rider=None):
    nb, ns, w = xpre.shape
    ts = SEQ_TILE
    nt = ns // ts
    tile, before, _ = _seq_specs(ts, w, nt, reverse=True)

    def body(dy_ref, xp_ref, xpb_ref, gt_ref, xc_ref, hs_ref, hsb_ref,
             cw_ref, cb_ref, ga_ref, gab_ref, gx_ref, gxb_ref, lam_ref,
             dxp_ref, dgt_ref, dcw_ref, dcb_ref, dga_ref, dgab_ref, dgx_ref, dgxb_ref, dlam_ref,
             dxc_ext, gcar, xin):
        j = pl.program_id(1)
        first = _first_step()
        at_seq_start = j == nt - 1

        @pl.when(j == 0)
        def _():
            dxc_ext[ts:ts + HALO, :] = jnp.zeros((HALO, w), f32)
            gcar[...] = jnp.zeros_like(gcar)

        gt = gt_ref[0]
        h = hs_ref[0]
        dyv = dy_ref[0].astype(f32)
        gl, gelu_vjp = jax.vjp(jax.nn.gelu, gt)
        dgt_ref[0] = gelu_vjp(dyv * h)[0].astype(dgt_ref.dtype)
        dh = dyv * gl

        (a, _), gates_vjp = jax.vjp(_lru_gates, xc_ref[0], ga_ref[...], gab_ref[...], gx_ref[...], gxb_ref[...],
                                    lam_ref[...])
        rows = lax.broadcasted_iota(jnp.int32, (ts, w), 0)
        dh = dh + jnp.where(rows == ts - 1, gcar[0:1, :], 0.0)
        b = pltpu.roll(a, ts - 1, 0)
        g = _scan_up(b, dh)
        gcar[0:1, :] = a[0:1, :] * g[0:1, :]
        hprev_row = jnp.where(at_seq_start, 0.0, hsb_ref[0][HALO - 1:HALO, :])
        hprev = jnp.where(rows == 0, hprev_row, pltpu.roll(h, 1, 0))
        dxc, dga, dgab, dgx, dgxb, dlam = gates_vjp((g * hprev, g))

        _accum(dga_ref, dga, first)
        _accum(dgx_ref, dgx, first)
        _accum(dgab_ref, dgab, first)
        _accum(dgxb_ref, dgxb, first)
        _accum(dlam_ref, dlam, first)
        _accum(dcb_ref, jnp.sum(dxc, axis=0, keepdims=True), first)

        dxc_ext[0:ts, :] = dxc
        dxp = jnp.zeros((ts, w), f32)
        for k in range(LRU_CONV):
            dxp = dxp + cw_ref[k:k + 1, :] * dxc_ext[pl.ds(LRU_CONV - 1 - k, ts), :]
        dxp_ref[0] = dxp.astype(dxp_ref.dtype)
        dxc_ext[ts:ts + HALO, :] = dxc[0:HALO, :]

        xin[0:HALO, :] = jnp.where(at_seq_start, 0.0, xpb_ref[0])
        xin[HALO:HALO + ts, :] = xp_ref[0]
        dcw_rows = [jnp.sum(dxc * xin[pl.ds(HALO - LRU_CONV + 1 + k, ts), :], axis=0, keepdims=True)
                    for k in range(LRU_CONV)]
        dcw_rows += [jnp.zeros((1, w), f32)] * (HALO - LRU_CONV)
        _accum(dcw_ref, jnp.concatenate(dcw_rows, axis=0), first)

    params = [cw, cb, ga, gab, gx, gxb, lam]
    pshape = lambda p: jax.ShapeDtypeStruct(p.shape, f32)
    grid = (nb, nt)
    n_main = 3 + len(params) - 1
    body, in_specs, out_specs, out_shape, scratch = _ride(
        rider, body, [tile, tile, before, tile, tile, tile, before] + [_const2(p.shape) for p in params],
        [tile, tile, _const2((HALO, w))] + [_const2(p.shape) for p in params[1:]],
        [jax.ShapeDtypeStruct((nb, ns, w), MXU_DTYPE), jax.ShapeDtypeStruct((nb, ns, w), MXU_DTYPE),
         jax.ShapeDtypeStruct((HALO, w), f32)] + [pshape(p) for p in params[1:]],
        [pltpu.VMEM((ts + HALO, w), f32), pltpu.VMEM((HALO, w), f32), pltpu.VMEM((ts + HALO, w), f32)], grid)
    outs = pl.pallas_call(
        body, name=name, grid=grid, in_specs=in_specs, out_specs=out_specs, out_shape=out_shape,
        scratch_shapes=scratch, compiler_params=_cparams("arbitrary", "arbitrary"),
    )(dy, xpre, xpre, gate, xc, hs, hs, *params, *(rider.inputs if rider else []))
    return (list(outs[:n_main]), list(outs[n_main:])) if rider else outs


SB_TILE = 256


def _split_dot(x, m):
    hi = x.astype(MXU_DTYPE)
    lo = (x - hi.astype(f32)).astype(MXU_DTYPE)
    return jnp.dot(hi, m, preferred_element_type=f32) + jnp.dot(lo, m, preferred_element_type=f32)


def _suffix_matrices(n):
    r = lax.broadcasted_iota(jnp.int32, (n, n), 0)
    c = lax.broadcasted_iota(jnp.int32, (n, n), 1)
    return (r > c).astype(MXU_DTYPE), (r >= c).astype(MXU_DTYPE)


def _sb_logits(qh, kb, strict):
    z = _mm_nt(qh, kb)
    ls = jnp.minimum(z, 0.0) - jnp.log(1.0 + jnp.exp(-jnp.abs(z)))
    lk = ls - z
    if strict is not None:
        lk = jnp.where(strict, lk, 0.0)
    return ls, lk


def _head_masked(x, dtype):
    lane = lax.broadcasted_iota(jnp.int32, x.shape, 1)
    return (jnp.where(lane < SB_HEAD_DIM, x, 0.0).astype(dtype), jnp.where(lane >= SB_HEAD_DIM, x, 0.0).astype(dtype))


def _stack_heads(dst, x, tq):
    x0, x1 = _head_masked(x, dst.dtype)
    for blk in range(dst.shape[0]):
        dst[blk, 0:tq, :] = x0[blk * tq:(blk + 1) * tq]
        dst[blk, tq:2 * tq, :] = x1[blk * tq:(blk + 1) * tq]


def _strict_mask(tq):
    rr = lax.broadcasted_iota(jnp.int32, (2 * tq, tq), 0)
    cc = lax.broadcasted_iota(jnp.int32, (2 * tq, tq), 1)
    return cc < jnp.where(rr >= tq, rr - tq, rr)


def _sb_specs(ns):
    npair = SB_WIDTH // LANES
    q = pl.BlockSpec((1, ns, LANES), lambda b, p: (b, 0, p))
    k = pl.BlockSpec((1, ns, LANES), lambda b, p: (b, 0, npair + p))
    v = pl.BlockSpec((1, ns, LANES), lambda b, p: (b, 0, 2 * npair + p))
    return q, k, v, npair


def sb_forward(qkv, name, rider=None):
    nb, ns, _ = qkv.shape
    tq = SB_TILE
    nq = ns // tq
    qspec, kspec, vspec, npair = _sb_specs(ns)

    def body(q_ref, k_ref, v_ref, o_ref, qs, ks, vs, acc):
        scale = 1.0 / math.sqrt(SB_HEAD_DIM)
        _stack_heads(qs, q_ref[0] * scale, tq)
        ks[...] = k_ref[0].astype(MXU_DTYPE)
        _stack_heads(vs, v_ref[0], tq)
        mx, _ = _suffix_matrices(tq)
        strict = _strict_mask(tq)

        def step(q2, blks, r2, masked):
            kbs = [ks[pl.ds(pl.multiple_of(b * tq, tq), tq), :] for b in blks]
            lg = [_sb_logits(q2, kb, strict if masked else None) for kb in kbs]
            sums = [jnp.dot(lk.astype(MXU_DTYPE), mx, preferred_element_type=f32) for _, lk in lg]
            total = None
            for (ls, lk), s, b in zip(lg, sums, blks):
                a = r2 + s
                w = jnp.exp(ls + a)
                if masked:
                    w = jnp.where(strict, w, 0.0)
                wb = w.astype(MXU_DTYPE)
                part = (jnp.dot(wb[:tq], vs[b, 0:tq, :], preferred_element_type=f32)
                        + jnp.dot(wb[tq:], vs[b, tq:2 * tq, :], preferred_element_type=f32))
                total = part if total is None else total + part
                r2 = a[:, 0:1] + lk[:, 0:1]
            acc[...] += total
            return r2

        def q_block(qi, carry):
            acc[...] = jnp.zeros_like(acc)
            q2 = qs[qi]
            r2 = step(q2, [qi], jnp.zeros((2 * tq, 1), f32), True)
            r2 = lax.fori_loop(0, lax.shift_right_logical(qi, 1),
                               lambda i, r: step(q2, [qi - 1 - 2 * i, qi - 2 - 2 * i], r, False), r2)
            lax.cond(jnp.bitwise_and(qi, 1) == 1, lambda r: step(q2, [0], r, False), lambda r: r, r2)
            o_ref[0, pl.ds(pl.multiple_of(qi * tq, tq), tq), :] = acc[...]
            return carry

        lax.fori_loop(0, nq, q_block, 0)

    grid = (nb, npair)
    body, in_specs, out_specs, out_shape, scratch = _ride(
        rider, body, [qspec, kspec, vspec], [pl.BlockSpec((1, ns, LANES), lambda b, p: (b, 0, p))],
        [jax.ShapeDtypeStruct((nb, ns, SB_WIDTH), f32)],
        [pltpu.VMEM((nq, 2 * tq, LANES), MXU_DTYPE), pltpu.VMEM((ns, LANES), MXU_DTYPE),
         pltpu.VMEM((nq, 2 * tq, LANES), MXU_DTYPE), pltpu.VMEM((tq, LANES), f32)], grid)
    outs = pl.pallas_call(
        body, name=name, grid=grid, in_specs=in_specs, out_specs=out_specs, out_shape=out_shape,
        scratch_shapes=scratch,
        compiler_params=_cparams("arbitrary", "arbitrary") if rider else _cparams("parallel", "parallel"),
    )(qkv, qkv, qkv, *(rider.inputs if rider else []))
    return (outs[0], list(outs[1:])) if rider else outs[0]


def sb_backward(qkv, o, do, name, rider=None):
    nb, ns, _ = qkv.shape
    tq = SB_TILE
    nq = ns // tq
    qspec, kspec, vspec, npair = _sb_specs(ns)
    ospec = pl.BlockSpec((1, ns, LANES), lambda b, p: (b, 0, p))

    def body(q_ref, k_ref, v_ref, o_ref, do_ref, dq_ref, dk_ref, dv_ref, qs, ks, kcat, vs, dos, dqacc, dkacc, dvacc):
        scale = 1.0 / math.sqrt(SB_HEAD_DIM)
        _stack_heads(qs, q_ref[0] * scale, tq)
        ks[...] = k_ref[0].astype(MXU_DTYPE)
        _stack_heads(kcat, k_ref[0], tq)
        vs[...] = v_ref[0].astype(MXU_DTYPE)
        _stack_heads(dos, do_ref[0].astype(f32), tq)
        dkacc[...] = jnp.zeros_like(dkacc)
        dvacc[...] = jnp.zeros_like(dvacc)
        mx, mi = _suffix_matrices(tq)
        strict = _strict_mask(tq)

        def step(q2, do2, q2t, do2t, dtot2, blks, carry, masked):
            r2, g2 = carry
            k0s = [pl.multiple_of(b * tq, tq) for b in blks]
            lg = [_sb_logits(q2, ks[pl.ds(k0, tq), :], strict if masked else None) for k0 in k0s]
            dws = [_mm_nt(do2, vs[pl.ds(k0, tq), :]) for k0 in k0s]
            sums = [jnp.dot(lk.astype(MXU_DTYPE), mx, preferred_element_type=f32) for _, lk in lg]
            wbs, es = [], []
            for (ls, lk), s in zip(lg, sums):
                a = r2 + s
                w = jnp.exp(ls + a)
                if masked:
                    w = jnp.where(strict, w, 0.0)
                wbs.append(w.astype(MXU_DTYPE))
                r2 = a[:, 0:1] + lk[:, 0:1]
            es = [wb.astype(f32) * dw for wb, dw in zip(wbs, dws)]
            esums = [_split_dot(e, mi) for e in es]
            dq = None
            for (ls, _), e, esum, wb, b, k0 in zip(lg, es, esums, wbs, blks, k0s):
                esuf = g2 + esum
                beta = jnp.exp(ls)
                dz = e * (1.0 - beta) - beta * (dtot2 - esuf)
                if masked:
                    dz = jnp.where(strict, dz, 0.0)
                dzb = dz.astype(MXU_DTYPE)
                part = (jnp.dot(dzb[:tq], kcat[b, 0:tq, :], preferred_element_type=f32)
                        + jnp.dot(dzb[tq:], kcat[b, tq:2 * tq, :], preferred_element_type=f32))
                dq = part if dq is None else dq + part
                dkacc[:, pl.ds(k0, tq)] += jnp.dot(q2t, dzb, preferred_element_type=f32)
                dvacc[:, pl.ds(k0, tq)] += jnp.dot(do2t, wb, preferred_element_type=f32)
                g2 = esuf[:, 0:1]
            dqacc[...] += dq
            return r2, g2

        def q_block(qi, carry):
            dqacc[...] = jnp.zeros_like(dqacc)
            q2, do2 = qs[qi], dos[qi]
            q2t, do2t = q2.T, do2.T
            ov = o_ref[0, pl.ds(pl.multiple_of(qi * tq, tq), tq), :]
            dtot2 = jnp.sum(do2.astype(f32) * jnp.concatenate([ov, ov], axis=0), axis=1, keepdims=True)
            zero = jnp.zeros((2 * tq, 1), f32)
            args = (q2, do2, q2t, do2t, dtot2)
            c = step(*args, [qi], (zero, zero), True)
            c = lax.fori_loop(0, lax.shift_right_logical(qi, 1),
                              lambda i, c: step(*args, [qi - 1 - 2 * i, qi - 2 - 2 * i], c, False), c)
            lax.cond(jnp.bitwise_and(qi, 1) == 1, lambda c: step(*args, [0], c, False), lambda c: c, c)
            dq_ref[0, pl.ds(pl.multiple_of(qi * tq, tq), tq), :] = (dqacc[...] * scale).astype(dq_ref.dtype)
            return carry

        lax.fori_loop(0, nq, q_block, 0)
        dk_ref[0] = dkacc[...].T.astype(dk_ref.dtype)
        dv_ref[0] = dvacc[...].T.astype(dv_ref.dtype)

    dshape = jax.ShapeDtypeStruct((nb, ns, SB_WIDTH), MXU_DTYPE)
    stacked = pltpu.VMEM((nq, 2 * tq, LANES), MXU_DTYPE)
    flat = pltpu.VMEM((ns, LANES), MXU_DTYPE)
    grid = (nb, npair)
    body, in_specs, out_specs, out_shape, scratch = _ride(
        rider, body, [qspec, kspec, vspec, ospec, ospec], [ospec, ospec, ospec], [dshape, dshape, dshape],
        [stacked, flat, stacked, flat, stacked,
         pltpu.VMEM((tq, LANES), f32), pltpu.VMEM((LANES, ns), f32), pltpu.VMEM((LANES, ns), f32)], grid)
    outs = pl.pallas_call(
        body, name=name, grid=grid, in_specs=in_specs, out_specs=out_specs, out_shape=out_shape,
        scratch_shapes=scratch,
        compiler_params=_cparams("arbitrary", "arbitrary") if rider else _cparams("parallel", "parallel"),
    )(qkv, qkv, qkv, o, do, *(rider.inputs if rider else []))
    return (list(outs[:3]), list(outs[3:])) if rider else list(outs)


SSM_PAIRS = SSM_HEADS // 2
PAIRS_PER_GROUP = SSM_PAIRS // SSM_GROUPS
GROUP_WIDTH = SSM_WIDTH // SSM_GROUPS


def _silu(x):
    return x * jax.nn.sigmoid(x)


def _ssd_chunk(xs_pre, b_pre, c_pre, dt_raw, dt_raw_t, z, st, dt_bias_r, dt_bias_c, a_log_r, a_log_c, d_skip,
               gains):
    n = dt_raw.shape[0]
    rows = lax.broadcasted_iota(jnp.int32, (n, n), 0)
    cols = lax.broadcasted_iota(jnp.int32, (n, n), 1)
    tril = cols <= rows
    tri_l = tril.astype(f32)
    tri_u = (rows <= cols).astype(f32)
    lane = lax.broadcasted_iota(jnp.int32, (n, LANES), 1)
    sub = lax.broadcasted_iota(jnp.int32, (LANES, n), 0)

    dt = _softplus(dt_raw + dt_bias_r)
    a_r = -jnp.exp(a_log_r)
    da = dt * a_r
    acs = _mm_exact(tri_l, da)
    dt_t = _softplus(dt_raw_t + dt_bias_c)
    acs_t = _mm_exact(dt_t * (-jnp.exp(a_log_c)), tri_u)

    bs = [_silu(b) for b in b_pre]
    cs = [_silu(c) for c in c_pre]
    cb = [dmm_nt(cs[g], bs[g]) for g in range(SSM_GROUPS)]

    end = jnp.sum(da, axis=0, keepdims=True)
    lane_row = lax.broadcasted_iota(jnp.int32, (1, LANES), 1)
    first_head = lane < SSM_HEAD_DIM
    first_head_row = lane_row < SSM_HEAD_DIM

    def head_col(v, h):
        return jnp.sum(jnp.where((lane if v.shape[0] == n else lane_row) == h, v, 0.0), axis=1, keepdims=True)

    ys, st_new = [], []
    for p in range(SSM_PAIRS):
        g = p // PAIRS_PER_GROUP
        h0, h1 = 2 * p, 2 * p + 1
        xs = _silu(xs_pre[p])
        acols = [head_col(acs, h0), head_col(acs, h1)]
        dt_p = jnp.where(first_head, head_col(dt, h0), head_col(dt, h1))
        acs_p = jnp.where(first_head, acols[0], acols[1])
        end_p = jnp.where(first_head_row, head_col(end, h0), head_col(end, h1))
        dsk_p = jnp.where(first_head_row, head_col(d_skip, h0), head_col(d_skip, h1))
        xdt = xs * dt_p
        y = jnp.exp(acs_p) * dmm(cs[g], st[p])
        for hh in range(2):
            row = jnp.sum(jnp.where(sub == 2 * p + hh, acs_t, 0.0), axis=0, keepdims=True)
            decay = jnp.where(tril, jnp.exp(jnp.where(tril, acols[hh] - row, 0.0)), 0.0)
            head = first_head if hh == 0 else jnp.logical_not(first_head)
            y = y + dmm(cb[g] * decay, jnp.where(head, xdt, 0.0))
        st_new.append(jnp.exp(end_p) * st[p] + dmm_tn(bs[g], xdt * jnp.exp(end_p - acs_p)))
        ys.append(y + dsk_p * xs)
    out = []
    for g in range(SSM_GROUPS):
        yg = jnp.concatenate(ys[g * PAIRS_PER_GROUP:(g + 1) * PAIRS_PER_GROUP], axis=1) * _silu(z[g])
        out.append(_rms(yg, gains[g]))
    return out, st_new


def _ssd_chunk_inputs(xconv, dtr, z, st_ref, gain):
    xs_pre = [xconv[:, LANES * p:LANES * (p + 1)] for p in range(SSM_PAIRS)]
    b0 = SSM_WIDTH
    c0 = SSM_WIDTH + SSM_GROUPS * SSM_STATE
    b_pre = [xconv[:, b0 + SSM_STATE * g:b0 + SSM_STATE * (g + 1)] for g in range(SSM_GROUPS)]
    c_pre = [xconv[:, c0 + SSM_STATE * g:c0 + SSM_STATE * (g + 1)] for g in range(SSM_GROUPS)]
    zs = [z[:, GROUP_WIDTH * g:GROUP_WIDTH * (g + 1)] for g in range(SSM_GROUPS)]
    sts = [st_ref[p] for p in range(SSM_PAIRS)]
    gains = [gain[:, GROUP_WIDTH * g:GROUP_WIDTH * (g + 1)] for g in range(SSM_GROUPS)]
    return xs_pre, b_pre, c_pre, dtr, dtr.T, zs, sts, gains


def ssd_forward(xbc, dt_raw, z, cw, cb, dbr, dbc, alr, alc, dsk, gain, name):
    nb, ns, wx = xbc.shape
    ln = SSM_CHUNK
    nt = ns // ln
    tile = lambda c: pl.BlockSpec((1, ln, c), lambda b, j: (b, j, 0))
    st_spec = pl.BlockSpec((1, 1, SSM_PAIRS, SSM_STATE, LANES), lambda b, j: (b, j, 0, 0, 0))

    def body(xbc_ref, dt_ref, z_ref, cw_ref, cb_ref, dbr_ref, dbc_ref, alr_ref, alc_ref, dsk_ref, gain_ref,
             y_ref, xconv_ref, stp_ref, xin, st):
        @pl.when(pl.program_id(1) == 0)
        def _():
            xin[0:HALO, :] = jnp.zeros((HALO, wx), f32)
            st[...] = jnp.zeros_like(st)

        xin[HALO:HALO + ln, :] = xbc_ref[0]
        xconv = jnp.broadcast_to(cb_ref[...], (ln, wx))
        for k in range(SSM_CONV):
            xconv = xconv + cw_ref[k:k + 1, :] * xin[pl.ds(HALO - SSM_CONV + 1 + k, ln), :]
        xin[0:HALO, :] = xin[ln:ln + HALO, :]
        xconv_ref[0] = xconv
        stp_ref[0, 0] = st[...]
        xs_pre, b_pre, c_pre, dtr, dtr_t, zs, sts, gains = _ssd_chunk_inputs(xconv, dt_ref[0], z_ref[0], st,
                                                                             gain_ref[...])
        out, st_new = _ssd_chunk(xs_pre, b_pre, c_pre, dtr, dtr_t, zs, sts, dbr_ref[...], dbc_ref[...],
                                 alr_ref[...], alc_ref[...], dsk_ref[...], gains)
        y_ref[0] = jnp.concatenate(out, axis=1).astype(y_ref.dtype)
        for p in range(SSM_PAIRS):
            st[p] = st_new[p]

    params = [cw, cb, dbr, dbc, alr, alc, dsk, gain]
    return pl.pallas_call(
        body, name=name, grid=(nb, nt),
        in_specs=[tile(wx), tile(LANES), tile(SSM_WIDTH)] + [_const2(p.shape) for p in params],
        out_specs=[tile(SSM_WIDTH), tile(wx), st_spec],
        out_shape=[jax.ShapeDtypeStruct((nb, ns, SSM_WIDTH), MXU_DTYPE), jax.ShapeDtypeStruct((nb, ns, wx), f32),
                   jax.ShapeDtypeStruct((nb, nt, SSM_PAIRS, SSM_STATE, LANES), f32)],
        scratch_shapes=[pltpu.VMEM((ln + HALO, wx), f32), pltpu.VMEM((SSM_PAIRS, SSM_STATE, LANES), f32)],
        compiler_params=_cparams("arbitrary", "arbitrary"),
    )(xbc, dt_raw, z, *params)


def ssd_backward(dy, xbc, xconv, dt_raw, z, stp, cw, cb, dbr, dbc, alr, alc, dsk, gain, name):
    nb, ns, wx = xbc.shape
    ln = SSM_CHUNK
    nt = ns // ln
    per = ln // HALO
    rj = lambda j: nt - 1 - j
    tile = lambda c: pl.BlockSpec((1, ln, c), lambda b, j: (b, rj(j), 0))
    before = pl.BlockSpec((1, HALO, wx), lambda b, j: (b, jnp.maximum(rj(j) * per - 1, 0), 0))
    st_spec = pl.BlockSpec((1, 1, SSM_PAIRS, SSM_STATE, LANES), lambda b, j: (b, rj(j), 0, 0, 0))

    def body(dy_ref, xbc_ref, xbcb_ref, xconv_ref, dt_ref, z_ref, stp_ref,
             cw_ref, cb_ref, dbr_ref, dbc_ref, alr_ref, alc_ref, dsk_ref, gain_ref,
             dxbc_ref, ddt_ref, dz_ref, dcw_ref, dcb_ref, ddbr_ref, ddbc_ref, dalr_ref, dalc_ref, ddsk_ref, dgain_ref,
             dxc_ext, dst, xin):
        j = pl.program_id(1)
        first = _first_step()
        at_seq_start = j == nt - 1

        @pl.when(j == 0)
        def _():
            dxc_ext[ln:ln + HALO, :] = jnp.zeros((HALO, wx), f32)
            dst[...] = jnp.zeros_like(dst)

        xs_pre, b_pre, c_pre, dtr, dtr_t, zs, sts, gains = _ssd_chunk_inputs(xconv_ref[0], dt_ref[0], z_ref[0],
                                                                             stp_ref.at[0, 0], gain_ref[...])
        _, vjp = jax.vjp(_ssd_chunk, xs_pre, b_pre, c_pre, dtr, dtr_t, zs, sts, dbr_ref[...], dbc_ref[...],
                         alr_ref[...], alc_ref[...], dsk_ref[...], gains)
        dyv = dy_ref[0].astype(f32)
        cot = ([dyv[:, GROUP_WIDTH * g:GROUP_WIDTH * (g + 1)] for g in range(SSM_GROUPS)],
               [dst[p] for p in range(SSM_PAIRS)])
        dxs, db, dc, ddt, ddt_t, dzs, dsts, ddbr, ddbc, dalr, dalc, ddsk, dgains = vjp(cot)
        for p in range(SSM_PAIRS):
            dst[p] = dsts[p]
        ddt_ref[0] = (ddt + ddt_t.T).astype(ddt_ref.dtype)
        dz_ref[0] = jnp.concatenate(dzs, axis=1).astype(dz_ref.dtype)
        _accum(ddbr_ref, ddbr, first)
        _accum(ddbc_ref, ddbc, first)
        _accum(dalr_ref, dalr, first)
        _accum(dalc_ref, dalc, first)
        _accum(ddsk_ref, ddsk, first)
        _accum(dgain_ref, jnp.concatenate(dgains, axis=1), first)

        dxc = jnp.concatenate(dxs + db + dc, axis=1)
        _accum(dcb_ref, jnp.sum(dxc, axis=0, keepdims=True), first)
        dxc_ext[0:ln, :] = dxc
        dxp = jnp.zeros((ln, wx), f32)
        for k in range(SSM_CONV):
            dxp = dxp + cw_ref[k:k + 1, :] * dxc_ext[pl.ds(SSM_CONV - 1 - k, ln), :]
        dxbc_ref[0] = dxp.astype(dxbc_ref.dtype)
        dxc_ext[ln:ln + HALO, :] = dxc[0:HALO, :]

        xin[0:HALO, :] = jnp.where(at_seq_start, 0.0, xbcb_ref[0])
        xin[HALO:HALO + ln, :] = xbc_ref[0]
        dcw_rows = [jnp.sum(dxc * xin[pl.ds(HALO - SSM_CONV + 1 + k, ln), :], axis=0, keepdims=True)
                    for k in range(SSM_CONV)]
        dcw_rows += [jnp.zeros((1, wx), f32)] * (HALO - SSM_CONV)
        _accum(dcw_ref, jnp.concatenate(dcw_rows, axis=0), first)

    params = [cw, cb, dbr, dbc, alr, alc, dsk, gain]
    pshape = lambda p: jax.ShapeDtypeStruct(p.shape, f32)
    return pl.pallas_call(
        body, name=name, grid=(nb, nt),
        in_specs=[tile(SSM_WIDTH), tile(wx), before, tile(wx), tile(LANES), tile(SSM_WIDTH), st_spec]
        + [_const2(p.shape) for p in params],
        out_specs=[tile(wx), tile(LANES), tile(SSM_WIDTH), _const2((HALO, wx))] + [_const2(p.shape) for p in params[1:]],
        out_shape=[jax.ShapeDtypeStruct((nb, ns, wx), MXU_DTYPE), jax.ShapeDtypeStruct((nb, ns, LANES), MXU_DTYPE),
                   jax.ShapeDtypeStruct((nb, ns, SSM_WIDTH), MXU_DTYPE), jax.ShapeDtypeStruct((HALO, wx), f32)]
        + [pshape(p) for p in params[1:]],
        scratch_shapes=[pltpu.VMEM((ln + HALO, wx), f32), pltpu.VMEM((SSM_PAIRS, SSM_STATE, LANES), f32),
                        pltpu.VMEM((ln + HALO, wx), f32)],
        compiler_params=_cparams("arbitrary", "arbitrary"),
    )(dy, xbc, xbc, xconv, dt_raw, z, stp, *params)


CONF_HALO = 32
CONF_OFF = CONF_HALO - CONF_KERNEL + 1


def _conf_specs(ts, c, nt):
    per = ts // CONF_HALO
    tile = pl.BlockSpec((1, ts, c), lambda b, j: (b, j, 0))
    before = pl.BlockSpec((1, CONF_HALO, c), lambda b, j: (b, jnp.maximum(j * per - 1, 0), 0))
    after = pl.BlockSpec((1, CONF_HALO, c), lambda b, j: (b, jnp.minimum((j + 1) * per, nt * per - 1), 0))
    return tile, before, after


SUBLANES = 8


def _shifted_copies(dst, x):
    rows = x.shape[0]
    dst[0] = x
    for b in range(1, SUBLANES):
        dst[b] = pltpu.roll(x, rows - b, 0)


def _window(copies, off, size):
    b = off % SUBLANES
    return copies[b, pl.ds(off - b, size), :]


def _glu(x):
    return x[:, :CONF_WIDTH] * jax.nn.sigmoid(x[:, CONF_WIDTH:])


def _layernorm_parts(c):
    xc = c - jnp.mean(c, axis=-1, keepdims=True)
    r = lax.rsqrt(jnp.mean(xc * xc, axis=-1, keepdims=True) + EPS)
    return xc * r, r


def conf_forward(glu, cw, cb, ln_g, ln_b, name):
    nb, ns, wg = glu.shape
    w = CONF_WIDTH
    ts = SEQ_TILE
    nt = ns // ts
    tile, before, _ = _conf_specs(ts, wg, nt)

    def body(x_ref, xb_ref, cw_ref, cb_ref, g_ref, b_ref, y_ref, u_rot):
        _shifted_copies(u_rot, jnp.concatenate(
            [jnp.where(pl.program_id(1) == 0, 0.0, _glu(xb_ref[0])), _glu(x_ref[0])], axis=0))
        conv = jnp.broadcast_to(cb_ref[...], (ts, w))
        for k in range(CONF_KERNEL):
            conv = conv + cw_ref[k:k + 1, :] * _window(u_rot, CONF_OFF + k, ts)
        xhat, _ = _layernorm_parts(conv)
        y_ref[0] = _silu(xhat * g_ref[...] + b_ref[...]).astype(y_ref.dtype)

    params = [cw, cb, ln_g, ln_b]
    return pl.pallas_call(
        body, name=name, grid=(nb, nt),
        in_specs=[tile, before] + [_const2(p.shape) for p in params],
        out_specs=pl.BlockSpec((1, ts, w), lambda b, j: (b, j, 0)),
        out_shape=jax.ShapeDtypeStruct((nb, ns, w), MXU_DTYPE),
        scratch_shapes=[pltpu.VMEM((SUBLANES, ts + CONF_HALO, w), f32)],
        compiler_params=_cparams("parallel", "parallel"),
    )(glu, glu, *params)


def conf_backward(dy, glu, cw, cb, ln_g, ln_b, name):
    nb, ns, wg = glu.shape
    w = CONF_WIDTH
    ts = SEQ_TILE
    nt = ns // ts
    te = ts + CONF_HALO
    tile, before, after = _conf_specs(ts, wg, nt)
    dtile, _, dafter = _conf_specs(ts, w, nt)

    def body(dy_ref, dya_ref, x_ref, xb_ref, xa_ref, cw_ref, cb_ref, g_ref, b_ref,
             dx_ref, dcw_ref, dcb_ref, dg_ref, db_ref, u_ext, dc_ext):
        j = pl.program_id(1)
        first = _first_step()
        x = x_ref[0]
        _shifted_copies(u_ext, jnp.concatenate(
            [jnp.where(j == 0, 0.0, _glu(xb_ref[0])), _glu(x), _glu(xa_ref[0])], axis=0))
        conv = jnp.broadcast_to(cb_ref[...], (te, w))
        for k in range(CONF_KERNEL):
            conv = conv + cw_ref[k:k + 1, :] * _window(u_ext, CONF_OFF + k, te)
        xhat, r = _layernorm_parts(conv)
        lnout = xhat * g_ref[...] + b_ref[...]
        sg = jax.nn.sigmoid(lnout)
        rows = lax.broadcasted_iota(jnp.int32, (te, w), 0)
        dyv = jnp.concatenate([dy_ref[0].astype(f32), dya_ref[0].astype(f32)], axis=0)
        dyv = jnp.where(jnp.logical_and(j == nt - 1, rows >= ts), 0.0, dyv)
        dln = dyv * sg * (1.0 + lnout * (1.0 - sg))
        in_tile = rows < ts
        _accum(dg_ref, jnp.sum(jnp.where(in_tile, dln * xhat, 0.0), axis=0, keepdims=True), first)
        _accum(db_ref, jnp.sum(jnp.where(in_tile, dln, 0.0), axis=0, keepdims=True), first)
        dxh = dln * g_ref[...]
        dconv = r * (dxh - jnp.mean(dxh, axis=-1, keepdims=True) - xhat * jnp.mean(dxh * xhat, axis=-1, keepdims=True))
        _shifted_copies(dc_ext, dconv)
        dct = dconv[0:ts, :]
        _accum(dcb_ref, jnp.sum(dct, axis=0, keepdims=True), first)
        du = jnp.zeros((ts, w), f32)
        dcw_rows = []
        for k in range(CONF_KERNEL):
            du = du + cw_ref[k:k + 1, :] * _window(dc_ext, CONF_KERNEL - 1 - k, ts)
            dcw_rows.append(jnp.sum(dct * _window(u_ext, CONF_OFF + k, ts), axis=0, keepdims=True))
        dcw_rows.append(jnp.zeros((1, w), f32))
        _accum(dcw_ref, jnp.concatenate(dcw_rows, axis=0), first)
        sb = jax.nn.sigmoid(x[:, w:])
        dx_ref[0] = jnp.concatenate([du * sb, du * x[:, :w] * sb * (1.0 - sb)], axis=1).astype(dx_ref.dtype)

    params = [cw, cb, ln_g, ln_b]
    return pl.pallas_call(
        body, name=name, grid=(nb, nt),
        in_specs=[dtile, dafter, tile, before, after] + [_const2(p.shape) for p in params],
        out_specs=[tile] + [_const2(p.shape) for p in params],
        out_shape=[jax.ShapeDtypeStruct((nb, ns, wg), MXU_DTYPE)] + [jax.ShapeDtypeStruct(p.shape, f32) for p in params],
        scratch_shapes=[pltpu.VMEM((SUBLANES, te + CONF_HALO, w), f32), pltpu.VMEM((SUBLANES, te, w), f32)],
        compiler_params=_cparams("arbitrary", "arbitrary"),
    )(dy, dy, glu, glu, glu, *params)


def _row(v):
    return v.reshape(1, -1).astype(f32)


def _pad_to(v, n, axis):
    pads = [(0, 0)] * v.ndim
    pads[axis] = (0, n - v.shape[axis])
    return jnp.pad(v, pads)


def _block_diag(w):
    nh, d, _ = w.shape
    eye = jnp.eye(nh, dtype=w.dtype)
    return (eye[:, None, :, None] * w[:, :, None, :]).reshape(nh * d, nh * d)


def _diag_blocks(m, nh):
    d = m.shape[0] // nh
    idx = jnp.arange(nh)
    return m.reshape(nh, d, nh, d)[idx, :, idx, :]


def _mix_even_fwd(h, gpre, w, wl, nb, ns, rider=None):
    t = nb * ns
    w_in = wl["w_in"]
    w_lx, w_lg = Part(w_in, 0, LRU_WIDTH, 1), Part(w_in, LRU_WIDTH, LRU_WIDTH, 1)
    w_qkv = Part(w_in, 2 * LRU_WIDTH, 3 * SB_WIDTH, 1)
    xpre, gate, qkv = norm_matmul(h, gpre, [w_lx, w_lg, w_qkv], [f32, f32, f32], name="ev_in_proj")
    lru_p = [w["ev_lru_conv_w"][0], _row(w["ev_lru_conv_b"][0]),
             _block_diag(w["ev_lru_gate_a_w"][0]).astype(MXU_DTYPE), _row(w["ev_lru_gate_a_b"][0]),
             _block_diag(w["ev_lru_gate_x_w"][0]).astype(MXU_DTYPE), _row(w["ev_lru_gate_x_b"][0]),
             _row(w["ev_lru_lambda"][0])]
    xpre3, gate3, qkv3 = xpre.reshape(nb, ns, -1), gate.reshape(nb, ns, -1), qkv.reshape(nb, ns, -1)
    y_a, xc, hs = lru_forward(xpre3, gate3, *lru_p, name="ev_lru_fwd")
    o = sb_forward(qkv3, name="ev_sb_fwd", rider=rider)
    carried = None
    if rider is not None:
        o, carried = o
    ys = [y_a.reshape(t, -1), o.reshape(t, -1)]
    saved = dict(xpre=xpre3, gate=gate3, qkv=qkv3, xc=xc, hs=hs, o=o, lru_p=lru_p)
    return ys, saved, carried


def _mix_even_bwd(dys, saved, wtl, nb, ns, lru_rider=None, attention_rider=None):
    t = nb * ns
    dy_a, dy_b = [d.reshape(nb, ns, -1) for d in dys]
    outs = lru_backward(dy_a, saved["xpre"], saved["gate"], saved["xc"], saved["hs"], *saved["lru_p"],
                        name="ev_lru_bwd", rider=lru_rider)
    lru_carried = None
    if lru_rider is not None:
        outs, lru_carried = outs
    dxp, dgt, dcw, dcb, dga, dgab, dgx, dgxb, dlam = outs
    rider = attention_rider(lru_carried) if attention_rider is not None else None
    carried = None
    if rider is None:
        dq, dk, dv = sb_backward(saved["qkv"], saved["o"], dy_b, name="ev_sb_bwd")
    else:
        (dq, dk, dv), carried = sb_backward(saved["qkv"], saved["o"], dy_b, name="ev_sb_bwd", rider=rider)
    w_in_t = wtl["w_in"]
    pieces = [dxp, dgt, dq, dk, dv]
    gs = [d.reshape(t, -1) for d in pieces]
    wts = [Part(w_in_t, LRU_WIDTH * i, LRU_WIDTH, 0) for i in range(5)]
    grads = {
        "ev_lru_conv_w": dcw[:LRU_CONV][None], "ev_lru_conv_b": dcb,
        "ev_lru_gate_a_w": _diag_blocks(dga, LRU_HEADS)[None], "ev_lru_gate_a_b": dgab,
        "ev_lru_gate_x_w": _diag_blocks(dgx, LRU_HEADS)[None], "ev_lru_gate_x_b": dgxb,
        "ev_lru_lambda": dlam,
    }
    return gs, wts, grads, carried


def _odd_params(w):
    ssd_p = [w["od_ssm_conv_w"][0], _row(w["od_ssm_conv_b"][0]),
             _pad_to(_row(w["od_ssm_dt_bias"][0]), LANES, 1), _pad_to(_row(w["od_ssm_dt_bias"][0]), LANES, 1).T,
             _pad_to(_row(w["od_ssm_a_log"][0]), LANES, 1), _pad_to(_row(w["od_ssm_a_log"][0]), LANES, 1).T,
             _pad_to(_row(w["od_ssm_d"][0]), LANES, 1), _row(w["od_ssm_norm"][0])]
    conf_p = [_pad_to(w["od_cm_conv_w"][0], CONF_HALO, 0), _row(w["od_cm_conv_b"][0]),
              _row(w["od_cm_ln_g"][0]), _row(w["od_cm_ln_b"][0])]
    return ssd_p, conf_p


ODD_SPLITS = (SSM_WIDTH, SSM_WIDTH + SSM_XBC, SSM_WIDTH + SSM_XBC + SSM_HEADS)


def _mix_odd_fwd(h, gpre, w, wl, nb, ns, rider=None):
    assert rider is None
    t = nb * ns
    w_in = wl["w_in"]
    s0, s1, s2 = ODD_SPLITS
    w_al = jnp.concatenate([w_in[:, :s1], w_in[:, s2:], _pad_to(w_in[:, s1:s2], LANES, 1)], axis=1)
    widths = (s0, s1 - s0, w_in.shape[1] - s2, LANES)
    starts = (0, s0, s1, s1 + widths[2])
    zz, xbc, glu, dtr = norm_matmul(h, gpre, [Part(w_al, a, n, 1) for a, n in zip(starts, widths)], [f32] * 4,
                                    name="od_in_proj")
    ssd_p, conf_p = _odd_params(w)
    zz3, xbc3, dtr3, glu3 = [a.reshape(nb, ns, -1) for a in (zz, xbc, dtr, glu)]
    y_c, xconv, stp = ssd_forward(xbc3, dtr3, zz3, *ssd_p, name="od_ssd_fwd")
    y_d = conf_forward(glu3, *conf_p, name="od_conf_fwd")
    ys = [y_c.reshape(t, -1), y_d.reshape(t, -1)]
    saved = dict(z=zz3, xbc=xbc3, dtr=dtr3, glu=glu3, xconv=xconv, stp=stp, ssd_p=ssd_p, conf_p=conf_p)
    return ys, saved, None


def _mix_odd_bwd(dys, saved, wtl, nb, ns, lru_rider=None, attention_rider=None):
    assert lru_rider is None and attention_rider is None
    t = nb * ns
    dy_c, dy_d = [d.reshape(nb, ns, -1) for d in dys]
    outs = ssd_backward(dy_c, saved["xbc"], saved["xconv"], saved["dtr"], saved["z"], saved["stp"], *saved["ssd_p"],
                        name="od_ssd_bwd")
    dxbc, ddt, dz, dcw, dcb, ddbr, ddbc, dalr, dalc, ddsk, dgain = outs
    dglu, ccw, ccb, clg, clb = conf_backward(dy_d, saved["glu"], *saved["conf_p"], name="od_conf_bwd")
    w_in_t = wtl["w_in"]
    s0, s1, s2 = ODD_SPLITS
    carried = None
    gs = [d.reshape(t, -1) for d in (dz, dxbc, dglu, ddt)]
    wt_al = jnp.concatenate([w_in_t[:s1], w_in_t[s2:], _pad_to(w_in_t[s1:s2], LANES, 0)], axis=0)
    widths = (s0, s1 - s0, w_in_t.shape[0] - s2, LANES)
    starts = (0, s0, s1, s1 + widths[2])
    wts = [Part(wt_al, a, n, 0) for a, n in zip(starts, widths)]
    nh = SSM_HEADS
    grads = {
        "od_ssm_conv_w": dcw[:SSM_CONV][None], "od_ssm_conv_b": dcb,
        "od_ssm_dt_bias": ddbr[:, :nh] + ddbc[:nh, 0][None], "od_ssm_a_log": dalr[:, :nh] + dalc[:nh, 0][None],
        "od_ssm_d": ddsk[:, :nh], "od_ssm_norm": dgain,
        "od_cm_conv_w": ccw[:CONF_KERNEL][None], "od_cm_conv_b": ccb, "od_cm_ln_g": clg, "od_cm_ln_b": clb,
    }
    return gs, wts, grads, carried


LAYER_MATRICES = ("w_in", "w_out", "mlp_w1", "mlp_w2", "ple_w_proj", "ple_w_gate")
NORM_NAMES = ("norm_mix_pre", "norm_mix_post", "norm_mlp_pre", "norm_mlp_post", "norm_ple")


class NoOverlap:
    sums, from_chips = {}, {}

    def attention_fwd_rider(self):
        return None

    def weights_arrived(self, carried, wl, wtl):
        raise NotImplementedError

    def mlp_bwd_rider(self, layer1_grads):
        return None

    def after_mlp_bwd(self, carried):
        pass

    def lru_bwd_rider(self, layer0_grads):
        return None

    def attention_bwd_rider(self, carried):
        return None

    def after_attention_bwd(self, carried):
        pass


OUT_SPLIT = (LRU_WIDTH, SSM_WIDTH)


def local_step(x, p, target, w, wl, wtl, comm=NoOverlap()):
    nb, ns, d = x.shape
    t = nb * ns
    h = x.reshape(t, d)
    depth = p.shape[0]
    wl, wtl = list(wl), list(wtl)
    tapes = []
    for i in range(depth):
        even = i % 2 == 0
        tag = f"l{i}_"
        gpre = _row(w["norm_mix_pre"][i])
        rider = comm.attention_fwd_rider() if i == 0 else None
        ys, saved, carried = (_mix_even_fwd if even else _mix_odd_fwd)(h, gpre, w, wl[i], nb, ns, rider)
        if rider is not None:
            wl, wtl = comm.weights_arrived(carried, wl, wtl)
        w_out = wl[i]["w_out"]
        split = OUT_SPLIT[i % 2]
        w_outs = [Part(w_out, 0, split, 0), Part(w_out, split, w_out.shape[0] - split, 0)]
        h1, m = matmul_residual_norm(ys, w_outs, h, _row(w["norm_mix_post"][i]), name=tag + "out_proj")
        a1, = norm_matmul(h1, _row(w["norm_mlp_pre"][i]), [wl[i]["mlp_w1"]], [MXU_DTYPE], name=tag + "mlp_up")
        h2, f = matmul_residual_norm([a1], [wl[i]["mlp_w2"]], h1, _row(w["norm_mlp_post"][i]), name=tag + "mlp_down",
                                     relu2=True)
        pi = p[i].reshape(t, -1)
        h3, gl, emb = ple_forward(h2, pi, wl[i]["ple_w_gate"], wl[i]["ple_w_proj"], _row(w["norm_ple"][i]),
                                  name=tag + "ple")
        tapes.append(dict(h=h, ys=ys, w_outs=w_outs, saved=saved, h1=h1, m=m, a1=a1, h2=h2, f=f, pi=pi, gl=gl,
                          emb=emb))
        h = h3

    loss_row, dh = loss_and_grad(h, target.reshape(t, d), name="loss")
    grads = {}
    norm_grads = {k: [None] * depth for k in NORM_NAMES}
    layer_grads = [None] * depth
    for i in reversed(range(depth)):
        even = i % 2 == 0
        tag = f"l{i}_"
        tp = tapes[i]
        lg = {}
        to_sibling = comm.mlp_bwd_rider(layer_grads[1]) if i == 0 else None
        dh2, dgl, demb, dg = ple_backward(dh, tp["h2"], tp["gl"], tp["emb"], _row(w["norm_ple"][i]),
                                          wtl[i]["ple_w_gate"], name=tag + "ple_bwd")
        norm_grads["norm_ple"][i] = dg
        lg["ple_w_gate"] = weight_grad(tp["h2"], dgl, name=tag + "dw_gate")
        lg["ple_w_proj"] = weight_grad(tp["pi"], demb, name=tag + "dw_proj")
        outs = bwd_through_norm_out(dh2, tp["f"], _row(w["norm_mlp_post"][i]), [wtl[i]["mlp_w2"]], [MXU_DTYPE],
                                    name=tag + "mlp_down_bwd", relu2_of=tp["a1"], rider=to_sibling)
        d_f, (da1,), dg = outs[:3]
        if to_sibling is not None:
            comm.after_mlp_bwd(outs[3])
        norm_grads["norm_mlp_post"][i] = dg
        lg["mlp_w2"] = weight_grad(tp["a1"], d_f, name=tag + "dw2", prologue="relu2")
        gpre = _row(w["norm_mlp_pre"][i])
        dh1, dg = bwd_through_norm_in(dh2, [da1], [wtl[i]["mlp_w1"]], tp["h1"], gpre, name=tag + "mlp_up_bwd")
        norm_grads["norm_mlp_pre"][i] = dg
        lg["mlp_w1"] = weight_grad(tp["h1"], da1, name=tag + "dw1", prologue="rms", gain=gpre)
        wt_out = wtl[i]["w_out"]
        split = tp["w_outs"][0].shape[0]
        dm, dys, dg = bwd_through_norm_out(dh1, tp["m"], _row(w["norm_mix_post"][i]),
                                           [Part(wt_out, 0, split, 1),
                                            Part(wt_out, split, wt_out.shape[1] - split, 1)],
                                           [f32, MXU_DTYPE if even else f32],
                                           name=tag + "out_proj_bwd")
        norm_grads["norm_mix_post"][i] = dg
        lg["w_out"] = jnp.concatenate([weight_grad(y, dm, name=tag + f"dw_out{k}") for k, y in enumerate(tp["ys"])],
                                      axis=0)
        lru_rider = comm.lru_bwd_rider(lg) if i == 0 else None
        gs, wts, mix_grads, carried = (_mix_even_bwd if even else _mix_odd_bwd)(
            dys, tp["saved"], wtl[i], nb, ns, lru_rider, comm.attention_bwd_rider if lru_rider is not None else None)
        if carried is not None:
            comm.after_attention_bwd(carried)
        grads.update(mix_grads)
        gpre = _row(w["norm_mix_pre"][i])
        dh, dg = bwd_through_norm_in(dh1, gs, wts, tp["h"], gpre, name=tag + "in_proj_bwd")
        norm_grads["norm_mix_pre"][i] = dg
        dw_in = weight_grads_of_norm(tp["h"], gpre, gs, name=tag + "dw_in")
        if not even:
            dw_in = [dw_in[0], dw_in[1], dw_in[3][:, :SSM_HEADS], dw_in[2]]
        lg["w_in"] = jnp.concatenate(dw_in, axis=1)
        layer_grads[i] = lg
    for k, v in norm_grads.items():
        grads[k] = jnp.concatenate(v, axis=0)
    return loss_row[0, 0], dh.reshape(nb, ns, d), grads, layer_grads


MESH_ID = pl.DeviceIdType.MESH
ANY = pl.BlockSpec(memory_space=pl.ANY)


def _mesh_pos():
    return lax.axis_index("x"), lax.axis_index("y"), lax.axis_index("c")


def all_gather(shards, name):
    return _run_alone(gather_rider(shards), name)


class Rider:
    def __init__(self, inputs, out_shapes, scratch_shapes, start, finish, middle=None):
        self.inputs, self.out_shapes, self.scratch_shapes = list(inputs), list(out_shapes), list(scratch_shapes)
        self.start, self.finish, self.middle = start, finish, middle


def _run_alone(rider, name):
    ni, no = len(rider.inputs), len(rider.out_shapes)

    def body(*refs):
        args = (refs[:ni], refs[ni:ni + no], refs[ni + no:])
        rider.start(*args)
        if rider.middle is not None:
            rider.middle(*args)
        rider.finish(*args)

    return pl.pallas_call(
        body, name=name, out_shape=rider.out_shapes, in_specs=[ANY] * ni, out_specs=[ANY] * no,
        scratch_shapes=rider.scratch_shapes,
    )(*rider.inputs)


def _ride(rider, body, in_specs, out_specs, out_shape, scratch_shapes, grid):
    in_specs, out_specs, out_shape = list(in_specs), list(out_specs), list(out_shape)
    scratch_shapes = list(scratch_shapes)
    if rider is None:
        return body, in_specs, out_specs, out_shape, scratch_shapes
    n_in, n_out, n_scr = len(in_specs), len(out_specs), len(scratch_shapes)
    ri, ro = len(rider.inputs), len(rider.out_shapes)
    total = math.prod(grid)

    def carrying(*refs):
        ins, r_ins = refs[:n_in], refs[n_in:n_in + ri]
        o0 = n_in + ri
        outs, r_outs = refs[o0:o0 + n_out], refs[o0 + n_out:o0 + n_out + ro]
        s0 = o0 + n_out + ro
        scr, r_scr = refs[s0:s0 + n_scr], refs[s0 + n_scr:]
        step = pl.program_id(0)
        for ax in range(1, len(grid)):
            step = step * grid[ax] + pl.program_id(ax)
        args = (r_ins, r_outs, r_scr)
        pl.when(step == 0)(lambda: rider.start(*args))
        if rider.middle is not None:
            pl.when(step == total // 2)(lambda: rider.middle(*args))
        body(*ins, *outs, *scr)
        pl.when(step == total - 1)(lambda: rider.finish(*args))

    return (carrying, in_specs + [ANY] * ri, out_specs + [ANY] * ro, out_shape + rider.out_shapes,
            scratch_shapes + rider.scratch_shapes)


def gather_rider(shards):
    n = len(shards)

    def parts(x_refs, out_refs, scr):
        send_sems, recv_sems, local_sems = scr
        x, y, c = _mesh_pos()
        chips = [(1 - x, y), (x, 1 - y), (1 - x, 1 - y)]

        def slot(a, px, py, pc):
            return out_refs[a].at[4 * px + 2 * py + pc]

        def copy(a, k, block, to, src=None):
            return pltpu.make_async_remote_copy(
                src_ref=slot(a, *block) if src is None else src, dst_ref=slot(a, *block),
                send_sem=send_sems.at[7 * a + k], recv_sem=recv_sems.at[7 * a + k], device_id=to,
                device_id_type=MESH_ID)

        me, sibling = (x, y, c), (x, y, 1 - c)
        def mine():
            return [pltpu.make_async_copy(x_refs[a], slot(a, *me), local_sems.at[a]) for a in range(n)]

        def first():
            out = []
            for j, chip in enumerate(chips):
                out += [copy(a, 1 + j, me, (*chip, c), src=x_refs[a]) for a in range(n)]
            return out + [copy(a, 0, me, sibling, src=x_refs[a]) for a in range(n)]

        def passed(j):
            return [copy(a, 4 + j, (*chips[j], c), sibling) for a in range(n)]

        return me, sibling, chips, c, copy, mine, first, passed

    def start(x_refs, out_refs, scr):
        _, _, _, _, _, mine, first, _ = parts(x_refs, out_refs, scr)
        for cp in mine() + first():
            cp.start()

    def middle(x_refs, out_refs, scr):
        me, _, chips, c, copy, _, _, passed = parts(x_refs, out_refs, scr)
        for j, chip in enumerate(chips):
            for a, fwd in enumerate(passed(j)):
                copy(a, 1 + j, (*chip, c), me).wait_recv()
                fwd.start()

    def finish(x_refs, out_refs, scr):
        me, sibling, chips, c, copy, mine, first, passed = parts(x_refs, out_refs, scr)
        for a in range(n):
            copy(a, 0, sibling, me).wait_recv()
        for j, chip in enumerate(chips):
            for a in range(n):
                copy(a, 4 + j, (*chip, 1 - c), me).wait_recv()
        for cp in first() + [cp for j in range(len(chips)) for cp in passed(j)]:
            cp.wait_send()
        for cp in mine():
            cp.wait()

    return Rider(shards, [jax.ShapeDtypeStruct((N_DEV,) + s.shape, s.dtype) for s in shards],
                 [pltpu.SemaphoreType.DMA((7 * n,)), pltpu.SemaphoreType.DMA((7 * n,)), pltpu.SemaphoreType.DMA((n,))],
                 start, finish, middle)


def scatter_to_sibling(parts, name):
    return _run_alone(sibling_rider(parts), name)


def sibling_rider(parts):
    n = len(parts)

    def copies(g_refs, out_refs, scr):
        send_sems, recv_sems = scr
        x, y, c = _mesh_pos()
        return [pltpu.make_async_remote_copy(
            src_ref=g_refs[a].at[2 * chip + (1 - c)], dst_ref=out_refs[a].at[chip],
            send_sem=send_sems.at[4 * a + chip], recv_sem=recv_sems.at[4 * a + chip], device_id=(x, y, 1 - c),
            device_id_type=MESH_ID) for a in range(n) for chip in range(4)]

    def start(*refs):
        for cp in copies(*refs):
            cp.start()

    def finish(*refs):
        cps = copies(*refs)
        for cp in cps:
            cp.wait_recv()
        for cp in cps:
            cp.wait_send()

    return Rider(parts, [jax.ShapeDtypeStruct((4,) + p.shape[1:], p.dtype) for p in parts],
                 [pltpu.SemaphoreType.DMA((4 * n,)), pltpu.SemaphoreType.DMA((4 * n,))], start, finish)


def scatter_to_chips(partials, name):
    return _run_alone(chips_rider(partials), name)


def chips_rider(partials):
    n = len(partials)

    def copies(p_refs, out_refs, scr):
        send_sems, recv_sems = scr
        x, y, c = _mesh_pos()
        chips = [(1 - x, y), (x, 1 - y), (1 - x, 1 - y)]
        return [pltpu.make_async_remote_copy(
            src_ref=p_refs[a].at[2 * px + py], dst_ref=out_refs[a].at[j],
            send_sem=send_sems.at[3 * a + j], recv_sem=recv_sems.at[3 * a + j], device_id=(px, py, c),
            device_id_type=MESH_ID) for a in range(n) for j, (px, py) in enumerate(chips)]

    def start(*refs):
        for cp in copies(*refs):
            cp.start()

    def finish(*refs):
        cps = copies(*refs)
        for cp in cps:
            cp.wait_recv()
        for cp in cps:
            cp.wait_send()

    return Rider(partials, [jax.ShapeDtypeStruct((3,) + p.shape[1:], p.dtype) for p in partials],
                 [pltpu.SemaphoreType.DMA((3 * n,)), pltpu.SemaphoreType.DMA((3 * n,))], start, finish)


ICI_DTYPE = jnp.bfloat16
ELEMENTWISE_BLOCK_BYTES = 1 << 20


def _row_tile(rows, cols):
    cap = max(16, ELEMENTWISE_BLOCK_BYTES // (4 * cols))
    best = [t for t in range(16, min(rows, cap) + 1, 16) if rows % t == 0]
    return best[-1] if best else rows


def add_sibling_parts(parts, received, core, name):
    _, r, n = parts.shape
    tr = _row_tile(r, n)

    def body(c_ref, a_ref, b_ref, o_ref, ob_ref):
        s = a_ref[...] + b_ref[...]
        o_ref[...] = s
        ob_ref[...] = s.astype(ob_ref.dtype)

    blk = pl.BlockSpec((1, tr, n), lambda i, j, c_ref: (i, j, 0))
    return pl.pallas_call(
        body, name=name,
        grid_spec=pltpu.PrefetchScalarGridSpec(
            num_scalar_prefetch=1, grid=(4, r // tr),
            in_specs=[pl.BlockSpec((1, tr, n), lambda i, j, c_ref: (2 * i + c_ref[0], j, 0)), blk],
            out_specs=[blk, blk]),
        out_shape=[jax.ShapeDtypeStruct((4, r, n), f32), jax.ShapeDtypeStruct((4, r, n), ICI_DTYPE)],
        compiler_params=_cparams("parallel", "parallel"),
    )(core, parts, received)


def _adamw(w, g, m, v):
    m = ADAM_B1 * m + (1.0 - ADAM_B1) * g
    v = ADAM_B2 * v + (1.0 - ADAM_B2) * jnp.square(g)
    m_hat = m / (1.0 - ADAM_B1 ** ADAM_STEP)
    v_hat = v / (1.0 - ADAM_B2 ** ADAM_STEP)
    delta = -ADAM_LR * (m_hat / (jnp.sqrt(v_hat) + ADAM_EPS) + ADAM_WD * w)
    return delta, m, v


def adamw_sharded(partial, received, chip, w, m, v, name):
    _, r, n = partial.shape

    def body(k_ref, p_ref, r_ref, w_ref, m_ref, v_ref, g_out, d_out, m_out, v_out):
        g = p_ref[0] + r_ref[0].astype(f32)
        g = g + r_ref[1].astype(f32)
        g = g + r_ref[2].astype(f32)
        delta, mn, vn = _adamw(w_ref[...], g, m_ref[...], v_ref[...])
        g_out[...] = g
        d_out[...] = delta
        m_out[...] = mn
        v_out[...] = vn

    tr = _row_tile(r, n)
    flat = pl.BlockSpec((tr, n), lambda j, k_ref: (j, 0))
    return pl.pallas_call(
        body, name=name,
        grid_spec=pltpu.PrefetchScalarGridSpec(
            num_scalar_prefetch=1, grid=(r // tr,),
            in_specs=[pl.BlockSpec((1, tr, n), lambda j, k_ref: (k_ref[0], j, 0)),
                      pl.BlockSpec((3, tr, n), lambda j, k_ref: (0, j, 0)), flat, flat, flat],
            out_specs=[flat] * 4),
        out_shape=[jax.ShapeDtypeStruct((r, n), f32)] * 4,
        compiler_params=_cparams("parallel"),
    )(chip, partial, received, w, m, v)


def adamw_replicated(gathered, w, m, v, name):
    _, r, n = gathered.shape

    def body(g_ref, w_ref, m_ref, v_ref, g_out, d_out, m_out, v_out):
        g = g_ref[0]
        for k in range(1, N_DEV):
            g = g + g_ref[k]
        delta, mn, vn = _adamw(w_ref[...], g, m_ref[...], v_ref[...])
        g_out[...] = g
        d_out[...] = delta
        m_out[...] = mn
        v_out[...] = vn

    return pl.pallas_call(
        body, name=name,
        out_shape=[jax.ShapeDtypeStruct((r, n), f32)] * 4,
        compiler_params=pltpu.CompilerParams(vmem_limit_bytes=VMEM_LIMIT),
    )(gathered, w, m, v)


W_NAMES = ['ev_w_in', 'ev_lru_conv_w', 'ev_lru_conv_b', 'ev_lru_gate_a_w', 'ev_lru_gate_a_b', 'ev_lru_gate_x_w',
           'ev_lru_gate_x_b', 'ev_lru_lambda', 'ev_w_out', 'od_w_in', 'od_ssm_conv_w', 'od_ssm_conv_b',
           'od_ssm_dt_bias', 'od_ssm_a_log', 'od_ssm_d', 'od_ssm_norm', 'od_cm_conv_w', 'od_cm_conv_b', 'od_cm_ln_g',
           'od_cm_ln_b', 'od_w_out', 'norm_mix_pre', 'norm_mix_post', 'norm_mlp_pre', 'norm_mlp_post', 'norm_ple',
           'mlp_w1', 'mlp_w2', 'ple_w_proj', 'ple_w_gate']
BIG_SHARDED = {'ev_w_in': 2, 'ev_w_out': 1, 'od_w_in': 2, 'od_w_out': 1, 'mlp_w1': 2, 'mlp_w2': 1, 'ple_w_proj': 2,
               'ple_w_gate': 1}
SMALL_SHARDED = {'ev_lru_conv_w': 2, 'od_ssm_conv_w': 2, 'od_ssm_conv_b': 1, 'od_ssm_norm': 1, 'od_cm_conv_w': 2,
                 'od_cm_conv_b': 1, 'od_cm_ln_g': 1, 'od_cm_ln_b': 1}
SHARDED = {**BIG_SHARDED, **SMALL_SHARDED}
REPLICATED = [n for n in W_NAMES if n not in SHARDED]


def _round_up(n, k):
    return -(-n // k) * k


def _pack_rows(flat, rows_multiple):
    n = flat.shape[0]
    total = _round_up(n, LANES * rows_multiple)
    return jnp.pad(flat, (0, total - n)).reshape(-1, LANES)


def _unpack(flat, shapes):
    out, off = {}, 0
    for name, shape in shapes.items():
        size = math.prod(shape)
        out[name] = flat[off:off + size].reshape(shape)
        off += size
    return out


def _rows(a):
    return a.reshape(-1, a.shape[-1])


def _unshard(g8, shape, axis):
    g = jnp.moveaxis(g8.reshape((N_DEV,) + tuple(shape)), 0, axis)
    return g.reshape(tuple(shape[:axis]) + (N_DEV * shape[axis],) + tuple(shape[axis + 1:]))


def _to_shards(g, axis):
    shard = g.shape[axis] // N_DEV
    g = g.reshape(g.shape[:axis] + (N_DEV, shard) + g.shape[axis + 1:])
    return jnp.moveaxis(g, axis, 0)


def _pack_small(tree):
    return _pack_rows(jnp.concatenate([tree[k].astype(f32).reshape(-1) for k in SMALL_SHARDED]), 16)


def _layer_entry(i, key):
    mixer = "ev" if i % 2 == 0 else "od"
    return {"w_in": (mixer + "_w_in", i // 2, 1), "w_out": (mixer + "_w_out", i // 2, 0),
            "mlp_w1": ("mlp_w1", i, 1), "mlp_w2": ("mlp_w2", i, 0),
            "ple_w_proj": ("ple_w_proj", i, 1), "ple_w_gate": ("ple_w_gate", i, 0)}[key]


def _layer_shards(tree, i):
    out = {}
    for key in LAYER_MATRICES:
        name, idx, _ = _layer_entry(i, key)
        out[key] = tree[name][idx]
    return out


def _assemble_layer(i, gathered):
    wl = {key: _unshard(g8, g8.shape[1:], _layer_entry(i, key)[2]) for key, g8 in zip(LAYER_MATRICES, gathered)}
    return wl, {key: v.T for key, v in wl.items()}


def _gather_early(w):
    small = _pack_small(w)
    terms, rest = [], small
    for _ in range(3):
        term = rest.astype(MXU_DTYPE)
        terms.append(term)
        rest = rest - term.astype(f32)
    outs = all_gather([_layer_shards(w, 0)["w_in"].astype(MXU_DTYPE), jnp.concatenate(terms, axis=0)],
                      name="gather_weights")
    w_in = _unshard(outs[0], outs[0].shape[1:], _layer_entry(0, "w_in")[2])
    wl, wtl = {"w_in": w_in}, {"w_in": w_in.T}
    full = {k: w[k] for k in REPLICATED}
    t = outs[-1].astype(f32)
    nr = small.shape[0]
    vals = (t[:, :nr] + t[:, nr:2 * nr] + t[:, 2 * nr:]).reshape(N_DEV, -1)
    off = 0
    for k, axis in SMALL_SHARDED.items():
        size = math.prod(w[k].shape)
        full[k] = _unshard(vals[:, off:off + size], w[k].shape, axis)
        off += size
    return full, wl, wtl


class Overlap:
    LATE = [(0, key) for key in LAYER_MATRICES if key != "w_in"] + [(1, key) for key in LAYER_MATRICES]
    EARLY_GRADS = [(0, key) for key in LAYER_MATRICES if key != "w_in"]

    def __init__(self, w):
        self.w = w
        self.sums, self.from_chips, self.parts = {}, {}, {}

    def attention_fwd_rider(self):
        shards = [_layer_shards(self.w, 0), _layer_shards(self.w, 1)]
        return gather_rider([shards[i][key].astype(MXU_DTYPE) for i, key in self.LATE])

    def weights_arrived(self, carried, wl, wtl):
        wl = [dict(wl[0]), {}]
        wtl = [dict(wtl[0]), {}]
        for (i, key), g8 in zip(self.LATE, carried):
            wl[i][key] = _unshard(g8, g8.shape[1:], _layer_entry(i, key)[2])
            wtl[i][key] = wl[i][key].T
        return wl, wtl

    def _to_sibling(self, ids, layer_grads):
        for i, key in ids:
            self.parts[i, key] = _to_shards(layer_grads[key], _layer_entry(i, key)[2])
        return sibling_rider([self.parts[e] for e in ids])

    def _add(self, ids, carried):
        core = jnp.reshape(lax.axis_index("c"), (1,)).astype(jnp.int32)
        for (i, key), got in zip(ids, carried):
            self.sums[i, key] = add_sibling_parts(self.parts[i, key], got, core, name=f"add_sibling_l{i}_{key}")

    def mlp_bwd_rider(self, layer1_grads):
        return self._to_sibling([(1, key) for key in LAYER_MATRICES], layer1_grads)

    def after_mlp_bwd(self, carried):
        self._add([(1, key) for key in LAYER_MATRICES], carried)

    def lru_bwd_rider(self, layer0_grads):
        return self._to_sibling(self.EARLY_GRADS, layer0_grads)

    def attention_bwd_rider(self, carried):
        self._add(self.EARLY_GRADS, carried)
        self.travelling = list(self.sums)
        return chips_rider([self.sums[e][1] for e in self.travelling])

    def after_attention_bwd(self, carried):
        for e, got in zip(self.travelling, carried):
            self.from_chips[e] = got


def _pack_replicated(tree):
    return _pack_rows(jnp.concatenate([tree[k].astype(f32).reshape(-1) for k in REPLICATED]), 8)


def kernel(x, p, ev_w_in, ev_lru_conv_w, ev_lru_conv_b, ev_lru_gate_a_w, ev_lru_gate_a_b, ev_lru_gate_x_w, ev_lru_gate_x_b, ev_lru_lambda, ev_w_out, od_w_in, od_ssm_conv_w, od_ssm_conv_b, od_ssm_dt_bias, od_ssm_a_log, od_ssm_d, od_ssm_norm, od_cm_conv_w, od_cm_conv_b, od_cm_ln_g, od_cm_ln_b, od_w_out, norm_mix_pre, norm_mix_post, norm_mlp_pre, norm_mlp_post, norm_ple, mlp_w1, mlp_w2, ple_w_proj, ple_w_gate, loss_target, m_ev_w_in, m_ev_lru_conv_w, m_ev_lru_conv_b, m_ev_lru_gate_a_w, m_ev_lru_gate_a_b, m_ev_lru_gate_x_w, m_ev_lru_gate_x_b, m_ev_lru_lambda, m_ev_w_out, m_od_w_in, m_od_ssm_conv_w, m_od_ssm_conv_b, m_od_ssm_dt_bias, m_od_ssm_a_log, m_od_ssm_d, m_od_ssm_norm, m_od_cm_conv_w, m_od_cm_conv_b, m_od_cm_ln_g, m_od_cm_ln_b, m_od_w_out, m_norm_mix_pre, m_norm_mix_post, m_norm_mlp_pre, m_norm_mlp_post, m_norm_ple, m_mlp_w1, m_mlp_w2, m_ple_w_proj, m_ple_w_gate, v_ev_w_in, v_ev_lru_conv_w, v_ev_lru_conv_b, v_ev_lru_gate_a_w, v_ev_lru_gate_a_b, v_ev_lru_gate_x_w, v_ev_lru_gate_x_b, v_ev_lru_lambda, v_ev_w_out, v_od_w_in, v_od_ssm_conv_w, v_od_ssm_conv_b, v_od_ssm_dt_bias, v_od_ssm_a_log, v_od_ssm_d, v_od_ssm_norm, v_od_cm_conv_w, v_od_cm_conv_b, v_od_cm_ln_g, v_od_cm_ln_b, v_od_w_out, v_norm_mix_pre, v_norm_mix_post, v_norm_mlp_pre, v_norm_mlp_post, v_norm_ple, v_mlp_w1, v_mlp_w2, v_ple_w_proj, v_ple_w_gate):
    ws = [ev_w_in, ev_lru_conv_w, ev_lru_conv_b, ev_lru_gate_a_w, ev_lru_gate_a_b, ev_lru_gate_x_w, ev_lru_gate_x_b, ev_lru_lambda, ev_w_out, od_w_in, od_ssm_conv_w, od_ssm_conv_b, od_ssm_dt_bias, od_ssm_a_log, od_ssm_d, od_ssm_norm, od_cm_conv_w, od_cm_conv_b, od_cm_ln_g, od_cm_ln_b, od_w_out, norm_mix_pre, norm_mix_post, norm_mlp_pre, norm_mlp_post, norm_ple, mlp_w1, mlp_w2, ple_w_proj, ple_w_gate]
    ms = [m_ev_w_in, m_ev_lru_conv_w, m_ev_lru_conv_b, m_ev_lru_gate_a_w, m_ev_lru_gate_a_b, m_ev_lru_gate_x_w, m_ev_lru_gate_x_b, m_ev_lru_lambda, m_ev_w_out, m_od_w_in, m_od_ssm_conv_w, m_od_ssm_conv_b, m_od_ssm_dt_bias, m_od_ssm_a_log, m_od_ssm_d, m_od_ssm_norm, m_od_cm_conv_w, m_od_cm_conv_b, m_od_cm_ln_g, m_od_cm_ln_b, m_od_w_out, m_norm_mix_pre, m_norm_mix_post, m_norm_mlp_pre, m_norm_mlp_post, m_norm_ple, m_mlp_w1, m_mlp_w2, m_ple_w_proj, m_ple_w_gate]
    vs = [v_ev_w_in, v_ev_lru_conv_w, v_ev_lru_conv_b, v_ev_lru_gate_a_w, v_ev_lru_gate_a_b, v_ev_lru_gate_x_w, v_ev_lru_gate_x_b, v_ev_lru_lambda, v_ev_w_out, v_od_w_in, v_od_ssm_conv_w, v_od_ssm_conv_b, v_od_ssm_dt_bias, v_od_ssm_a_log, v_od_ssm_d, v_od_ssm_norm, v_od_cm_conv_w, v_od_cm_conv_b, v_od_cm_ln_g, v_od_cm_ln_b, v_od_w_out, v_norm_mix_pre, v_norm_mix_post, v_norm_mlp_pre, v_norm_mlp_post, v_norm_ple, v_mlp_w1, v_mlp_w2, v_ple_w_proj, v_ple_w_gate]
    w = dict(zip(W_NAMES, ws))
    m = dict(zip(W_NAMES, ms))
    v = dict(zip(W_NAMES, vs))
    full, wl0, wtl0 = _gather_early(w)
    comm = Overlap(w)
    loss_local, grad_x, grads, layer_grads = local_step(x, p, loss_target, full, [wl0, None], [wtl0, None], comm)
    loss = lax.psum(loss_local, ("x", "y", "c"))
    return (loss, grad_x, *_reduce_and_update(grads, layer_grads, w, m, v, comm))


def _reduce_and_update(grads, layer_grads, w, m, v, comm):
    mx, my, mc = _mesh_pos()

    entries = [(i, key) for i in range(len(layer_grads)) for key in LAYER_MATRICES]
    left = [e for e in entries if e not in comm.from_chips]
    parts = [_to_shards(layer_grads[i][key], _layer_entry(i, key)[2]) for i, key in left]
    small = jnp.concatenate([_to_shards(grads[k], axis).reshape(N_DEV, -1) for k, axis in SMALL_SHARDED.items()],
                            axis=1)
    small_rows = _pack_small(w).shape[0]
    small = jnp.pad(small, ((0, 0), (0, small_rows * LANES - small.shape[1]))).reshape(N_DEV, small_rows, LANES)
    parts.append(small)
    from_sibling = scatter_to_sibling(parts, name="scatter_sibling")
    core = jnp.reshape(mc, (1,)).astype(jnp.int32)
    sums = [add_sibling_parts(a, b, core, name=f"add_sibling_{i}") for i, (a, b) in enumerate(zip(parts, from_sibling))]
    from_chips = scatter_to_chips([s[1] for s in sums], name="scatter_chips")
    all_sums = {**comm.sums, **dict(zip(left, sums[:-1]))}
    all_from_chips = {**comm.from_chips, **dict(zip(left, from_chips[:-1]))}
    chip = jnp.reshape(2 * mx + my, (1,)).astype(jnp.int32)
    per_layer = []
    for i in range(len(layer_grads)):
        ws, ms, vs = _layer_shards(w, i), _layer_shards(m, i), _layer_shards(v, i)
        per_layer.append({key: adamw_sharded(all_sums[i, key][0], all_from_chips[i, key], chip, ws[key], ms[key],
                                             vs[key], name=f"adamw_l{i}_{key}") for key in LAYER_MATRICES})
    g_sh, d_sh, m_sh, v_sh = {}, {}, {}, {}
    for which, tree in enumerate((g_sh, d_sh, m_sh, v_sh)):
        for i in range(len(per_layer)):
            for key in LAYER_MATRICES:
                name, idx, _ = _layer_entry(i, key)
                tree.setdefault(name, {})[idx] = per_layer[i][key][which]
        for name in BIG_SHARDED:
            tree[name] = jnp.stack([tree[name][idx] for idx in sorted(tree[name])], axis=0)
    outs = adamw_sharded(sums[-1][0], from_chips[-1], chip, _pack_small(w), _pack_small(m), _pack_small(v),
                         name="adamw_small")
    small_shapes = {k: w[k].shape for k in SMALL_SHARDED}
    for tree, o in zip((g_sh, d_sh, m_sh, v_sh), outs):
        tree.update(_unpack(o.reshape(-1), small_shapes))

    rep_parts, = all_gather([_pack_replicated(grads)], name="gather_replicated_grads")
    outs = adamw_replicated(rep_parts, _pack_replicated(w), _pack_replicated(m), _pack_replicated(v),
                            name="adamw_replicated")
    rep_shapes = {k: w[k].shape for k in REPLICATED}
    g_rp, d_rp, m_rp, v_rp = [_unpack(o.reshape(-1), rep_shapes) for o in outs]

    pick = lambda sh, rp: [sh[k] if k in SHARDED else rp[k] for k in W_NAMES]
    return [*pick(g_sh, g_rp), *pick(d_sh, d_rp), *pick(m_sh, m_rp), *pick(v_sh, v_rp)]
```

```python
import math

import jax
import jax.numpy as jnp
from jax import lax
from jax.experimental import pallas as pl
from jax.experimental.pallas import tpu as pltpu

f32 = jnp.float32
bf16 = jnp.bfloat16
MXU_DTYPE = jnp.bfloat16

D_MODEL = 1024
EPS = 1e-6
LRU_WIDTH = 512
LRU_HEADS = 8
LRU_CONV = 4
LRU_C = 8.0
SB_WIDTH = 512
SB_HEAD_DIM = 64
SSM_WIDTH = 1024
SSM_HEADS = 16
SSM_HEAD_DIM = 64
SSM_GROUPS = 2
SSM_STATE = 128
SSM_CONV = 4
SSM_CHUNK = 128
SSM_XBC = SSM_WIDTH + 2 * SSM_GROUPS * SSM_STATE
CONF_WIDTH = 512
CONF_KERNEL = 31
MLP_HIDDEN = 4096
PLE_DIM = 256
LANES = 128
N_DEV = 8

ADAM_LR = 0.001
ADAM_B1 = 0.9
ADAM_B2 = 0.999
ADAM_EPS = 1e-08
ADAM_WD = 0.01
ADAM_STEP = 10

VMEM_LIMIT = 56 * 1024 * 1024


def _cparams(*sem):
    return pltpu.CompilerParams(dimension_semantics=sem, vmem_limit_bytes=VMEM_LIMIT)


def _mm(a, b):
    return jnp.dot(a.astype(MXU_DTYPE), b.astype(MXU_DTYPE), preferred_element_type=f32)


def _mm_nt(a, b):
    return lax.dot_general(a.astype(MXU_DTYPE), b.astype(MXU_DTYPE), (((1,), (1,)), ((), ())),
                           preferred_element_type=f32)


def _mm_tn(a, b):
    return lax.dot_general(a.astype(MXU_DTYPE), b.astype(MXU_DTYPE), (((0,), (0,)), ((), ())),
                           preferred_element_type=f32)


def _mm_exact(a, b):
    return jnp.dot(a, b, preferred_element_type=f32, precision=lax.Precision.HIGHEST)


@jax.custom_vjp
def dmm(a, b):
    return _mm(a, b)


def _dmm_fwd(a, b):
    return _mm(a, b), (a, b)


def _dmm_bwd(res, g):
    a, b = res
    return _mm_nt(g, b), _mm_tn(a, g)


dmm.defvjp(_dmm_fwd, _dmm_bwd)


@jax.custom_vjp
def dmm_nt(a, b):
    return _mm_nt(a, b)


def _dmm_nt_fwd(a, b):
    return _mm_nt(a, b), (a, b)


def _dmm_nt_bwd(res, g):
    a, b = res
    return _mm(g, b), _mm_tn(g, a)


dmm_nt.defvjp(_dmm_nt_fwd, _dmm_nt_bwd)


@jax.custom_vjp
def dmm_tn(a, b):
    return _mm_tn(a, b)


def _dmm_tn_fwd(a, b):
    return _mm_tn(a, b), (a, b)


def _dmm_tn_bwd(res, g):
    a, b = res
    return _mm_nt(b, g), _mm(a, g)


dmm_tn.defvjp(_dmm_tn_fwd, _dmm_tn_bwd)


def _rms(x, g):
    r = lax.rsqrt(jnp.mean(x * x, axis=-1, keepdims=True) + EPS)
    return x * r * g


def _rms_bwd(dy, x, g):
    r = lax.rsqrt(jnp.mean(x * x, axis=-1, keepdims=True) + EPS)
    dyg = dy * g
    dx = r * dyg - x * (r * r * r * jnp.mean(dyg * x, axis=-1, keepdims=True))
    return dx, dy * x * r


def _tok(tm, n):
    return pl.BlockSpec((tm, n), lambda i: (i, 0))


def _whole(shape):
    nd = len(shape)
    return pl.BlockSpec(tuple(shape), lambda i: (0,) * nd)


def _acc_rows(ref, val):
    s = jnp.sum(val, axis=0, keepdims=True)

    @pl.when(pl.program_id(0) == 0)
    def _():
        ref[...] = s

    @pl.when(pl.program_id(0) != 0)
    def _():
        ref[...] += s


TOKEN_TILE = 512
WEIGHT_GRAD_TOKENS = 1024


class Part:
    def __init__(self, whole, start, size, axis):
        self.whole, self.start, self.size, self.axis = whole, start, size, axis
        self.shape = tuple(size if a == axis else n for a, n in enumerate(whole.shape))


def _weights(ws):
    wholes, readers = [], []
    for w in ws:
        arr = w.whole if isinstance(w, Part) else w
        idx = next((i for i, a in enumerate(wholes) if a is arr), None)
        if idx is None:
            wholes.append(arr)
            idx = len(wholes) - 1
        if isinstance(w, Part):
            rows = pl.ds(w.start, w.size) if w.axis == 0 else slice(None)
            cols = pl.ds(w.start, w.size) if w.axis == 1 else slice(None)
            readers.append(lambda refs, idx=idx, rows=rows, cols=cols: refs[idx][rows, cols])
        else:
            readers.append(lambda refs, idx=idx: refs[idx][...])
    return wholes, readers


def norm_matmul(h, g, ws, out_dtypes, name):
    t, d = h.shape
    tm = TOKEN_TILE
    wholes, readers = _weights(ws)
    nw = len(wholes)

    def body(h_ref, g_ref, *refs):
        hn = _rms(h_ref[...], g_ref[...]).astype(MXU_DTYPE)
        for read, o_ref in zip(readers, refs[nw:]):
            o_ref[...] = jnp.dot(hn, read(refs[:nw]), preferred_element_type=f32).astype(o_ref.dtype)

    return pl.pallas_call(
        body, name=name, grid=(t // tm,),
        in_specs=[_tok(tm, d), _whole(g.shape)] + [_whole(w.shape) for w in wholes],
        out_specs=[_tok(tm, w.shape[1]) for w in ws],
        out_shape=[jax.ShapeDtypeStruct((t, w.shape[1]), dt) for w, dt in zip(ws, out_dtypes)],
        compiler_params=_cparams("parallel"),
    )(h, g, *wholes)


def matmul_residual_norm(xs, ws, h, g, name, relu2=False):
    t, d = h.shape
    tm = TOKEN_TILE
    nx = len(xs)
    wholes, readers = _weights(ws)
    nw = len(wholes)

    def body(*refs):
        x_refs, w_refs = refs[:nx], refs[nx:nx + nw]
        h_ref, g_ref, ho_ref, m_ref = refs[nx + nw:]
        m = None
        for x_ref, read in zip(x_refs, readers):
            x = x_ref[...]
            if relu2:
                x = jnp.square(jnp.maximum(x.astype(f32), 0.0))
            part = jnp.dot(x.astype(MXU_DTYPE), read(w_refs), preferred_element_type=f32)
            m = part if m is None else m + part
        m_ref[...] = m.astype(m_ref.dtype)
        ho_ref[...] = h_ref[...] + _rms(m, g_ref[...])

    return pl.pallas_call(
        body, name=name, grid=(t // tm,),
        in_specs=[_tok(tm, x.shape[1]) for x in xs] + [_whole(w.shape) for w in wholes]
        + [_tok(tm, d), _whole(g.shape)],
        out_specs=[_tok(tm, d), _tok(tm, d)],
        out_shape=[jax.ShapeDtypeStruct((t, d), f32), jax.ShapeDtypeStruct((t, d), MXU_DTYPE)],
        compiler_params=_cparams("parallel"),
    )(*xs, *wholes, h, g)


def ple_forward(h, p, w_gate, w_proj, g, name):
    t, d = h.shape
    tm = TOKEN_TILE

    def body(h_ref, p_ref, wg_ref, wp_ref, g_ref, ho_ref, gl_ref, emb_ref):
        hh = h_ref[...]
        gl = jnp.dot(hh.astype(MXU_DTYPE), wg_ref[...], preferred_element_type=f32)
        emb = jnp.dot(p_ref[...].astype(MXU_DTYPE), wp_ref[...], preferred_element_type=f32)
        gl_ref[...] = gl.astype(gl_ref.dtype)
        emb_ref[...] = emb.astype(emb_ref.dtype)
        ho_ref[...] = hh + _rms(jax.nn.sigmoid(gl) * emb, g_ref[...])

    return pl.pallas_call(
        body, name=name, grid=(t // tm,),
        in_specs=[_tok(tm, d), _tok(tm, p.shape[1]), _whole(w_gate.shape), _whole(w_proj.shape), _whole(g.shape)],
        out_specs=[_tok(tm, d)] * 3,
        out_shape=[jax.ShapeDtypeStruct((t, d), f32)] + [jax.ShapeDtypeStruct((t, d), MXU_DTYPE)] * 2,
        compiler_params=_cparams("parallel"),
    )(h, p, w_gate, w_proj, g)


def loss_and_grad(h, target, name):
    t, d = h.shape
    tm = TOKEN_TILE

    def body(h_ref, t_ref, l_ref, dh_ref):
        e = h_ref[...] - t_ref[...]
        dh_ref[...] = e * (1.0 / d)
        part = jnp.sum(jnp.sum(e * e, axis=1, keepdims=True), axis=0, keepdims=True) * (0.5 / d)
        _acc_rows(l_ref, jnp.broadcast_to(part, (1, LANES)))

    return pl.pallas_call(
        body, name=name, grid=(t // tm,),
        in_specs=[_tok(tm, d), _tok(tm, d)],
        out_specs=[_whole((1, LANES)), _tok(tm, d)],
        out_shape=[jax.ShapeDtypeStruct((1, LANES), f32), jax.ShapeDtypeStruct((t, d), f32)],
        compiler_params=_cparams("arbitrary"),
    )(h, target)


def bwd_through_norm_in(dh, gs, wts, h, g, name):
    t, d = h.shape
    tm = TOKEN_TILE
    ng = len(gs)
    wholes, readers = _weights(wts)
    nw = len(wholes)

    def body(*refs):
        dh_ref = refs[0]
        g_refs, w_refs = refs[1:1 + ng], refs[1 + ng:1 + ng + nw]
        h_ref, gain_ref, dho_ref, dg_ref = refs[1 + ng + nw:]
        dhn = None
        for g_ref, read in zip(g_refs, readers):
            part = jnp.dot(g_ref[...].astype(MXU_DTYPE), read(w_refs), preferred_element_type=f32)
            dhn = part if dhn is None else dhn + part
        dx, dgr = _rms_bwd(dhn, h_ref[...], gain_ref[...])
        dho_ref[...] = dh_ref[...] + dx
        _acc_rows(dg_ref, dgr)

    return pl.pallas_call(
        body, name=name, grid=(t // tm,),
        in_specs=[_tok(tm, d)] + [_tok(tm, x.shape[1]) for x in gs] + [_whole(w.shape) for w in wholes]
        + [_tok(tm, d), _whole(g.shape)],
        out_specs=[_tok(tm, d), _whole((1, d))],
        out_shape=[jax.ShapeDtypeStruct((t, d), f32), jax.ShapeDtypeStruct((1, d), f32)],
        compiler_params=_cparams("arbitrary"),
    )(dh, *gs, *wholes, h, g)


def bwd_through_norm_out(dh, n, g, wts, out_dtypes, name, relu2_of=None, rider=None):
    t, d = n.shape
    tm = TOKEN_TILE
    nw = len(wts)
    wholes, readers = _weights(wts)
    nwh = len(wholes)
    has_a = relu2_of is not None

    def body(*refs):
        dh_ref, n_ref, gain_ref = refs[:3]
        w_refs = refs[3:3 + nwh]
        rest = refs[3 + nwh:]
        if has_a:
            a_ref, rest = rest[0], rest[1:]
        dn_ref, dx_refs, dg_ref = rest[0], rest[1:1 + nw], rest[1 + nw]
        dn, dgr = _rms_bwd(dh_ref[...], n_ref[...].astype(f32), gain_ref[...])
        dnb = dn.astype(MXU_DTYPE)
        dn_ref[...] = dnb.astype(dn_ref.dtype)
        for read, dx_ref in zip(readers, dx_refs):
            dx = jnp.dot(dnb, read(w_refs), preferred_element_type=f32)
            if has_a:
                dx = dx * (2.0 * jnp.maximum(a_ref[...].astype(f32), 0.0))
            dx_ref[...] = dx.astype(dx_ref.dtype)
        _acc_rows(dg_ref, dgr)

    ins = [dh, n, g, *wholes] + ([relu2_of] if has_a else [])
    in_specs = [_tok(tm, d), _tok(tm, d), _whole(g.shape)] + [_whole(w.shape) for w in wholes]
    if has_a:
        in_specs.append(_tok(tm, relu2_of.shape[1]))
    grid = (t // tm,)
    body, in_specs, out_specs, out_shape, scratch = _ride(
        rider, body, in_specs, [_tok(tm, d)] + [_tok(tm, w.shape[1]) for w in wts] + [_whole((1, d))],
        [jax.ShapeDtypeStruct((t, d), MXU_DTYPE)]
        + [jax.ShapeDtypeStruct((t, w.shape[1]), dt) for w, dt in zip(wts, out_dtypes)]
        + [jax.ShapeDtypeStruct((1, d), f32)], [], grid)
    outs = pl.pallas_call(
        body, name=name, grid=grid, in_specs=in_specs, out_specs=out_specs, out_shape=out_shape,
        scratch_shapes=scratch, compiler_params=_cparams("arbitrary"),
    )(*ins, *(rider.inputs if rider else []))
    if rider:
        return outs[0], list(outs[1:1 + nw]), outs[1 + nw], list(outs[2 + nw:])
    return outs[0], list(outs[1:1 + nw]), outs[1 + nw]


def ple_backward(dh3, h2, gl, emb, g, w_gate_t, name):
    t, d = h2.shape
    tm = TOKEN_TILE

    def body(dh_ref, gl_ref, emb_ref, gain_ref, wt_ref, dho_ref, dgl_ref, demb_ref, dg_ref):
        gate = jax.nn.sigmoid(gl_ref[...].astype(f32))
        emb = emb_ref[...].astype(f32)
        dge, dgr = _rms_bwd(dh_ref[...], gate * emb, gain_ref[...])
        demb_ref[...] = (dge * gate).astype(demb_ref.dtype)
        dgl = (dge * emb * gate * (1.0 - gate)).astype(MXU_DTYPE)
        dgl_ref[...] = dgl.astype(dgl_ref.dtype)
        dho_ref[...] = dh_ref[...] + jnp.dot(dgl, wt_ref[...], preferred_element_type=f32)
        _acc_rows(dg_ref, dgr)

    return pl.pallas_call(
        body, name=name, grid=(t // tm,),
        in_specs=[_tok(tm, d), _tok(tm, d), _tok(tm, d), _whole(g.shape), _whole(w_gate_t.shape)],
        out_specs=[_tok(tm, d), _tok(tm, d), _tok(tm, d), _whole((1, d))],
        out_shape=[jax.ShapeDtypeStruct((t, d), f32), jax.ShapeDtypeStruct((t, d), MXU_DTYPE),
                   jax.ShapeDtypeStruct((t, d), MXU_DTYPE), jax.ShapeDtypeStruct((1, d), f32)],
        compiler_params=_cparams("arbitrary"),
    )(dh3, gl, emb, g, w_gate_t)


def _largest_tile(n, cap):
    if n <= cap:
        return n
    return max(c for c in range(LANES, cap + 1, LANES) if n % c == 0)


def weight_grad(x, gout, name, prologue="none", gain=None):
    t, k = x.shape
    n = gout.shape[1]
    tt = WEIGHT_GRAD_TOKENS
    tn = _largest_tile(n, 1024)
    tk = k if prologue == "rms" else _largest_tile(k, 1024)
    has_gain = prologue == "rms"

    def body(*refs):
        if has_gain:
            x_ref, gain_ref, g_ref, o_ref = refs
        else:
            x_ref, g_ref, o_ref = refs
        x = x_ref[...].astype(f32)
        if prologue == "relu2":
            x = jnp.square(jnp.maximum(x, 0.0))
        elif prologue == "rms":
            x = _rms(x, gain_ref[...])
        part = _mm_tn(x, g_ref[...])

        @pl.when(pl.program_id(2) == 0)
        def _():
            o_ref[...] = part

        @pl.when(pl.program_id(2) != 0)
        def _():
            o_ref[...] += part

    in_specs = [pl.BlockSpec((tt, tk), lambda i, j, s: (s, i))]
    ins = [x]
    if has_gain:
        in_specs.append(pl.BlockSpec(gain.shape, lambda i, j, s: (0, 0)))
        ins.append(gain)
    in_specs.append(pl.BlockSpec((tt, tn), lambda i, j, s: (s, j)))
    ins.append(gout)
    return pl.pallas_call(
        body, name=name, grid=(k // tk, n // tn, t // tt),
        in_specs=in_specs,
        out_specs=pl.BlockSpec((tk, tn), lambda i, j, s: (i, j)),
        out_shape=jax.ShapeDtypeStruct((k, n), f32),
        compiler_params=_cparams("parallel", "parallel", "arbitrary"),
    )(*ins)


def weight_grads_of_norm(x, gain, gouts, name):
    t, k = x.shape
    tt = TOKEN_TILE
    ng = len(gouts)

    def body(x_ref, gain_ref, *refs):
        xn = _rms(x_ref[...], gain_ref[...]).astype(MXU_DTYPE)
        first = pl.program_id(0) == 0
        for g_ref, o_ref in zip(refs[:ng], refs[ng:]):
            _accum(o_ref, _mm_tn(xn, g_ref[...]), first)

    return pl.pallas_call(
        body, name=name, grid=(t // tt,),
        in_specs=[_tok(tt, k), _whole(gain.shape)] + [_tok(tt, g.shape[1]) for g in gouts],
        out_specs=[_whole((k, g.shape[1])) for g in gouts],
        out_shape=[jax.ShapeDtypeStruct((k, g.shape[1]), f32) for g in gouts],
        compiler_params=_cparams("arbitrary"),
    )(x, gain, *gouts)


SEQ_TILE = 256
HALO = 8


def _first_step():
    return jnp.logical_and(pl.program_id(0) == 0, pl.program_id(1) == 0)


def _accum(ref, val, first):
    @pl.when(first)
    def _():
        ref[...] = val

    @pl.when(jnp.logical_not(first))
    def _():
        ref[...] += val


def _softplus(x):
    return jnp.maximum(x, 0.0) + jnp.log1p(jnp.exp(-jnp.abs(x)))


def _neg_expm1(z):
    series = -z * (1.0 + z * (0.5 + z * (1.0 / 6.0 + z * (1.0 / 24.0 + z * (1.0 / 120.0)))))
    return jnp.where(z > -0.05, series, 1.0 - jnp.exp(z))


def _lru_gates(xc, ga, gab, gx, gxb, lam):
    r = jax.nn.sigmoid(dmm(xc, ga) + gab)
    i = jax.nn.sigmoid(dmm(xc, gx) + gxb)
    log_a = -LRU_C * r * _softplus(-lam)
    a = jnp.exp(log_a)
    u = jnp.sqrt(_neg_expm1(2.0 * log_a)) * (i * xc)
    return a, u


def _scan_down(a, u):
    n = a.shape[0]
    rows = lax.broadcasted_iota(jnp.int32, a.shape, 0)
    d = 1
    while d < n:
        keep = rows >= d
        a_s = jnp.where(keep, pltpu.roll(a, d, 0), 1.0)
        u_s = jnp.where(keep, pltpu.roll(u, d, 0), 0.0)
        u = a * u_s + u
        a = a * a_s
        d *= 2
    return a, u


def _scan_up(b, g):
    n = b.shape[0]
    rows = lax.broadcasted_iota(jnp.int32, b.shape, 0)
    d = 1
    while d < n:
        keep = rows < n - d
        b_s = jnp.where(keep, pltpu.roll(b, n - d, 0), 1.0)
        g_s = jnp.where(keep, pltpu.roll(g, n - d, 0), 0.0)
        g = g + b * g_s
        b = b * b_s
        d *= 2
    return g


def _seq_specs(ts, c, nt, reverse=False):
    per = ts // HALO

    def jj(j):
        return (nt - 1 - j) if reverse else j

    tile = pl.BlockSpec((1, ts, c), lambda b, j: (b, jj(j), 0))
    before = pl.BlockSpec((1, HALO, c), lambda b, j: (b, jnp.maximum(jj(j) * per - 1, 0), 0))
    after = pl.BlockSpec((1, HALO, c), lambda b, j: (b, jnp.minimum((jj(j) + 1) * per, nt * per - 1), 0))
    return tile, before, after


def _const2(shape):
    nd = len(shape)
    return pl.BlockSpec(tuple(shape), lambda b, j: (0,) * nd)


def lru_forward(xpre, gate, cw, cb, ga, gab, gx, gxb, lam, name):
    nb, ns, w = xpre.shape
    ts = SEQ_TILE
    nt = ns // ts
    tile, _, _ = _seq_specs(ts, w, nt)

    def body(xp_ref, gt_ref, cw_ref, cb_ref, ga_ref, gab_ref, gx_ref, gxb_ref, lam_ref,
             y_ref, xc_ref, hs_ref, xin, hcar):
        @pl.when(pl.program_id(1) == 0)
        def _():
            xin[0:HALO, :] = jnp.zeros((HALO, w), f32)
            hcar[...] = jnp.zeros_like(hcar)

        xin[HALO:HALO + ts, :] = xp_ref[0]
        xc = jnp.broadcast_to(cb_ref[...], (ts, w))
        for k in range(LRU_CONV):
            xc = xc + cw_ref[k:k + 1, :] * xin[pl.ds(HALO - LRU_CONV + 1 + k, ts), :]
        xin[0:HALO, :] = xin[ts:ts + HALO, :]
        a, u = _lru_gates(xc, ga_ref[...], gab_ref[...], gx_ref[...], gxb_ref[...], lam_ref[...])
        acum, h = _scan_down(a, u)
        h = h + acum * hcar[0:1, :]
        hcar[0:1, :] = h[ts - 1:ts, :]
        xc_ref[0] = xc
        hs_ref[0] = h
        y_ref[0] = (h * jax.nn.gelu(gt_ref[0])).astype(y_ref.dtype)

    params = [cw, cb, ga, gab, gx, gxb, lam]
    return pl.pallas_call(
        body, name=name, grid=(nb, nt),
        in_specs=[tile, tile] + [_const2(p.shape) for p in params],
        out_specs=[tile, tile, tile],
        out_shape=[jax.ShapeDtypeStruct((nb, ns, w), MXU_DTYPE), jax.ShapeDtypeStruct((nb, ns, w), f32),
                   jax.ShapeDtypeStruct((nb, ns, w), f32)],
        scratch_shapes=[pltpu.VMEM((ts + HALO, w), f32), pltpu.VMEM((HALO, w), f32)],
        compiler_params=_cparams("arbitrary", "arbitrary"),
    )(xpre, gate, *params)


def lru_backward(dy, xpre, gate, xc, hs, cw, cb, ga, gab, gx, gxb, lam, name, rider=None):
    nb, ns, w = xpre.shape
    ts = SEQ_TILE
    nt = ns // ts
    tile, before, _ = _seq_specs(ts, w, nt, reverse=True)

    def body(dy_ref, xp_ref, xpb_ref, gt_ref, xc_ref, hs_ref, hsb_ref,
             cw_ref, cb_ref, ga_ref, gab_ref, gx_ref, gxb_ref, lam_ref,
             dxp_ref, dgt_ref, dcw_ref, dcb_ref, dga_ref, dgab_ref, dgx_ref, dgxb_ref, dlam_ref,
             dxc_ext, gcar, xin):
        j = pl.program_id(1)
        first = _first_step()
        at_seq_start = j == nt - 1

        @pl.when(j == 0)
        def _():
            dxc_ext[ts:ts + HALO, :] = jnp.zeros((HALO, w), f32)
            gcar[...] = jnp.zeros_like(gcar)

        gt = gt_ref[0]
        h = hs_ref[0]
        dyv = dy_ref[0].astype(f32)
        gl, gelu_vjp = jax.vjp(jax.nn.gelu, gt)
        dgt_ref[0] = gelu_vjp(dyv * h)[0].astype(dgt_ref.dtype)
        dh = dyv * gl

        (a, _), gates_vjp = jax.vjp(_lru_gates, xc_ref[0], ga_ref[...], gab_ref[...], gx_ref[...], gxb_ref[...],
                                    lam_ref[...])
        rows = lax.broadcasted_iota(jnp.int32, (ts, w), 0)
        dh = dh + jnp.where(rows == ts - 1, gcar[0:1, :], 0.0)
        b = pltpu.roll(a, ts - 1, 0)
        g = _scan_up(b, dh)
        gcar[0:1, :] = a[0:1, :] * g[0:1, :]
        hprev_row = jnp.where(at_seq_start, 0.0, hsb_ref[0][HALO - 1:HALO, :])
        hprev = jnp.where(rows == 0, hprev_row, pltpu.roll(h, 1, 0))
        dxc, dga, dgab, dgx, dgxb, dlam = gates_vjp((g * hprev, g))

        _accum(dga_ref, dga, first)
        _accum(dgx_ref, dgx, first)
        _accum(dgab_ref, dgab, first)
        _accum(dgxb_ref, dgxb, first)
        _accum(dlam_ref, dlam, first)
        _accum(dcb_ref, jnp.sum(dxc, axis=0, keepdims=True), first)

        dxc_ext[0:ts, :] = dxc
        dxp = jnp.zeros((ts, w), f32)
        for k in range(LRU_CONV):
            dxp = dxp + cw_ref[k:k + 1, :] * dxc_ext[pl.ds(LRU_CONV - 1 - k, ts), :]
        dxp_ref[0] = dxp.astype(dxp_ref.dtype)
        dxc_ext[ts:ts + HALO, :] = dxc[0:HALO, :]

        xin[0:HALO, :] = jnp.where(at_seq_start, 0.0, xpb_ref[0])
        xin[HALO:HALO + ts, :] = xp_ref[0]
        dcw_rows = [jnp.sum(dxc * xin[pl.ds(HALO - LRU_CONV + 1 + k, ts), :], axis=0, keepdims=True)
                    for k in range(LRU_CONV)]
        dcw_rows += [jnp.zeros((1, w), f32)] * (HALO - LRU_CONV)
        _accum(dcw_ref, jnp.concatenate(dcw_rows, axis=0), first)

    params = [cw, cb, ga, gab, gx, gxb, lam]
    pshape = lambda p: jax.ShapeDtypeStruct(p.shape, f32)
    grid = (nb, nt)
    n_main = 3 + len(params) - 1
    body, in_specs, out_specs, out_shape, scratch = _ride(
        rider, body, [tile, tile, before, tile, tile, tile, before] + [_const2(p.shape) for p in params],
        [tile, tile, _const2((HALO, w))] + [_const2(p.shape) for p in params[1:]],
        [jax.ShapeDtypeStruct((nb, ns, w), MXU_DTYPE), jax.ShapeDtypeStruct((nb, ns, w), MXU_DTYPE),
         jax.ShapeDtypeStruct((HALO, w), f32)] + [pshape(p) for p in params[1:]],
        [pltpu.VMEM((ts + HALO, w), f32), pltpu.VMEM((HALO, w), f32), pltpu.VMEM((ts + HALO, w), f32)], grid)
    outs = pl.pallas_call(
        body, name=name, grid=grid, in_specs=in_specs, out_specs=out_specs, out_shape=out_shape,
        scratch_shapes=scratch, compiler_params=_cparams("arbitrary", "arbitrary"),
    )(dy, xpre, xpre, gate, xc, hs, hs, *params, *(rider.inputs if rider else []))
    return (list(outs[:n_main]), list(outs[n_main:])) if rider else outs


SB_TILE = 256


def _split_dot(x, m):
    hi = x.astype(MXU_DTYPE)
    lo = (x - hi.astype(f32)).astype(MXU_DTYPE)
    return jnp.dot(hi, m, preferred_element_type=f32) + jnp.dot(lo, m, preferred_element_type=f32)


def _suffix_matrices(n):
    r = lax.broadcasted_iota(jnp.int32, (n, n), 0)
    c = lax.broadcasted_iota(jnp.int32, (n, n), 1)
    return (r > c).astype(MXU_DTYPE), (r >= c).astype(MXU_DTYPE)


LOG2E = 1.4426950408889634


def _sb_logits(qh, kb, strict):
    z = _mm_nt(qh, kb)
    ls = jnp.minimum(z, 0.0) - jnp.log2(1.0 + jnp.exp2(-jnp.abs(z)))
    lk = ls - z
    if strict is not None:
        lk = jnp.where(strict, lk, 0.0)
    return ls, lk


def _head_masked(x, dtype):
    lane = lax.broadcasted_iota(jnp.int32, x.shape, 1)
    return (jnp.where(lane < SB_HEAD_DIM, x, 0.0).astype(dtype), jnp.where(lane >= SB_HEAD_DIM, x, 0.0).astype(dtype))


def _stack_heads(dst, x, tq):
    x0, x1 = _head_masked(x, dst.dtype)
    for blk in range(dst.shape[0]):
        dst[blk, 0:tq, :] = x0[blk * tq:(blk + 1) * tq]
        dst[blk, tq:2 * tq, :] = x1[blk * tq:(blk + 1) * tq]


def _strict_mask(tq):
    rr = lax.broadcasted_iota(jnp.int32, (2 * tq, tq), 0)
    cc = lax.broadcasted_iota(jnp.int32, (2 * tq, tq), 1)
    return cc < jnp.where(rr >= tq, rr - tq, rr)


def _sb_specs(ns):
    npair = SB_WIDTH // LANES
    q = pl.BlockSpec((1, ns, LANES), lambda b, p: (b, 0, p))
    k = pl.BlockSpec((1, ns, LANES), lambda b, p: (b, 0, npair + p))
    v = pl.BlockSpec((1, ns, LANES), lambda b, p: (b, 0, 2 * npair + p))
    return q, k, v, npair


def sb_forward(qkv, name, rider=None):
    nb, ns, _ = qkv.shape
    tq = SB_TILE
    nq = ns // tq
    qspec, kspec, vspec, npair = _sb_specs(ns)

    def body(q_ref, k_ref, v_ref, o_ref, qs, ks, vs, acc):
        scale = 1.0 / math.sqrt(SB_HEAD_DIM)
        _stack_heads(qs, q_ref[0] * (scale * LOG2E), tq)
        ks[...] = k_ref[0].astype(MXU_DTYPE)
        _stack_heads(vs, v_ref[0], tq)
        mx, _ = _suffix_matrices(tq)
        strict = _strict_mask(tq)

        def step(q2, blks, r2, masked):
            kbs = [ks[pl.ds(pl.multiple_of(b * tq, tq), tq), :] for b in blks]
            lg = [_sb_logits(q2, kb, strict if masked else None) for kb in kbs]
            sums = [jnp.dot(lk.astype(MXU_DTYPE), mx, preferred_element_type=f32) for _, lk in lg]
            total = None
            for (ls, lk), s, b in zip(lg, sums, blks):
                a = r2 + s
                w = jnp.exp2(ls + a)
                if masked:
                    w = jnp.where(strict, w, 0.0)
                wb = w.astype(MXU_DTYPE)
                part = (jnp.dot(wb[:tq], vs[b, 0:tq, :], preferred_element_type=f32)
                        + jnp.dot(wb[tq:], vs[b, tq:2 * tq, :], preferred_element_type=f32))
                total = part if total is None else total + part
                r2 = a[:, 0:1] + lk[:, 0:1]
            acc[...] += total
            return r2

        def q_block(qi, carry):
            acc[...] = jnp.zeros_like(acc)
            q2 = qs[qi]
            r2 = step(q2, [qi], jnp.zeros((2 * tq, 1), f32), True)
            r2 = lax.fori_loop(0, lax.shift_right_logical(qi, 1),
                               lambda i, r: step(q2, [qi - 1 - 2 * i, qi - 2 - 2 * i], r, False), r2)
            lax.cond(jnp.bitwise_and(qi, 1) == 1, lambda r: step(q2, [0], r, False), lambda r: r, r2)
            o_ref[0, pl.ds(pl.multiple_of(qi * tq, tq), tq), :] = acc[...]
            return carry

        lax.fori_loop(0, nq, q_block, 0)

    grid = (nb, npair)
    body, in_specs, out_specs, out_shape, scratch = _ride(
        rider, body, [qspec, kspec, vspec], [pl.BlockSpec((1, ns, LANES), lambda b, p: (b, 0, p))],
        [jax.ShapeDtypeStruct((nb, ns, SB_WIDTH), f32)],
        [pltpu.VMEM((nq, 2 * tq, LANES), MXU_DTYPE), pltpu.VMEM((ns, LANES), MXU_DTYPE),
         pltpu.VMEM((nq, 2 * tq, LANES), MXU_DTYPE), pltpu.VMEM((tq, LANES), f32)], grid)
    outs = pl.pallas_call(
        body, name=name, grid=grid, in_specs=in_specs, out_specs=out_specs, out_shape=out_shape,
        scratch_shapes=scratch,
        compiler_params=_cparams("arbitrary", "arbitrary") if rider else _cparams("parallel", "parallel"),
    )(qkv, qkv, qkv, *(rider.inputs if rider else []))
    return (outs[0], list(outs[1:])) if rider else outs[0]


def sb_backward(qkv, o, do, name, rider=None):
    nb, ns, _ = qkv.shape
    tq = SB_TILE
    nq = ns // tq
    qspec, kspec, vspec, npair = _sb_specs(ns)
    ospec = pl.BlockSpec((1, ns, LANES), lambda b, p: (b, 0, p))

    def body(q_ref, k_ref, v_ref, o_ref, do_ref, dq_ref, dk_ref, dv_ref, qs, ks, kcat, vs, dos, dqacc, dkacc, dvacc):
        scale = 1.0 / math.sqrt(SB_HEAD_DIM)
        _stack_heads(qs, q_ref[0] * (scale * LOG2E), tq)
        ks[...] = k_ref[0].astype(MXU_DTYPE)
        _stack_heads(kcat, k_ref[0], tq)
        vs[...] = v_ref[0].astype(MXU_DTYPE)
        _stack_heads(dos, do_ref[0].astype(f32), tq)
        dkacc[...] = jnp.zeros_like(dkacc)
        dvacc[...] = jnp.zeros_like(dvacc)
        mx, mi = _suffix_matrices(tq)
        strict = _strict_mask(tq)

        def step(q2, do2, q2t, do2t, dtot2, blks, carry, masked):
            r2, g2 = carry
            k0s = [pl.multiple_of(b * tq, tq) for b in blks]
            lg = [_sb_logits(q2, ks[pl.ds(k0, tq), :], strict if masked else None) for k0 in k0s]
            dws = [_mm_nt(do2, vs[pl.ds(k0, tq), :]) for k0 in k0s]
            sums = [jnp.dot(lk.astype(MXU_DTYPE), mx, preferred_element_type=f32) for _, lk in lg]
            wbs, es = [], []
            for (ls, lk), s in zip(lg, sums):
                a = r2 + s
                w = jnp.exp2(ls + a)
                if masked:
                    w = jnp.where(strict, w, 0.0)
                wbs.append(w.astype(MXU_DTYPE))
                r2 = a[:, 0:1] + lk[:, 0:1]
            es = [wb.astype(f32) * dw for wb, dw in zip(wbs, dws)]
            esums = [_split_dot(e, mi) for e in es]
            dq = None
            for (ls, _), e, esum, wb, b, k0 in zip(lg, es, esums, wbs, blks, k0s):
                esuf = g2 + esum
                beta = jnp.exp2(ls)
                dz = e - beta * (e + (dtot2 - esuf))
                if masked:
                    dz = jnp.where(strict, dz, 0.0)
                dzb = dz.astype(MXU_DTYPE)
                part = (jnp.dot(dzb[:tq], kcat[b, 0:tq, :], preferred_element_type=f32)
                        + jnp.dot(dzb[tq:], kcat[b, tq:2 * tq, :], preferred_element_type=f32))
                dq = part if dq is None else dq + part
                dkacc[:, pl.ds(k0, tq)] += jnp.dot(q2t, dzb, preferred_element_type=f32)
                dvacc[:, pl.ds(k0, tq)] += jnp.dot(do2t, wb, preferred_element_type=f32)
                g2 = esuf[:, 0:1]
            dqacc[...] += dq
            return r2, g2

        def q_block(qi, carry):
            dqacc[...] = jnp.zeros_like(dqacc)
            q2, do2 = qs[qi], dos[qi]
            q2t, do2t = q2.T, do2.T
            ov = o_ref[0, pl.ds(pl.multiple_of(qi * tq, tq), tq), :]
            dtot2 = jnp.sum(do2.astype(f32) * jnp.concatenate([ov, ov], axis=0), axis=1, keepdims=True)
            zero = jnp.zeros((2 * tq, 1), f32)
            args = (q2, do2, q2t, do2t, dtot2)
            c = step(*args, [qi], (zero, zero), True)
            c = lax.fori_loop(0, lax.shift_right_logical(qi, 1),
                              lambda i, c: step(*args, [qi - 1 - 2 * i, qi - 2 - 2 * i], c, False), c)
            lax.cond(jnp.bitwise_and(qi, 1) == 1, lambda c: step(*args, [0], c, False), lambda c: c, c)
            dq_ref[0, pl.ds(pl.multiple_of(qi * tq, tq), tq), :] = (dqacc[...] * scale).astype(dq_ref.dtype)
            return carry

        lax.fori_loop(0, nq, q_block, 0)
        dk_ref[0] = (dkacc[...].T * (1.0 / LOG2E)).astype(dk_ref.dtype)
        dv_ref[0] = dvacc[...].T.astype(dv_ref.dtype)

    dshape = jax.ShapeDtypeStruct((nb, ns, SB_WIDTH), MXU_DTYPE)
    stacked = pltpu.VMEM((nq, 2 * tq, LANES), MXU_DTYPE)
    flat = pltpu.VMEM((ns, LANES), MXU_DTYPE)
    grid = (nb, npair)
    body, in_specs, out_specs, out_shape, scratch = _ride(
        rider, body, [qspec, kspec, vspec, ospec, ospec], [ospec, ospec, ospec], [dshape, dshape, dshape],
        [stacked, flat, stacked, flat, stacked,
         pltpu.VMEM((tq, LANES), f32), pltpu.VMEM((LANES, ns), f32), pltpu.VMEM((LANES, ns), f32)], grid)
    outs = pl.pallas_call(
        body, name=name, grid=grid, in_specs=in_specs, out_specs=out_specs, out_shape=out_shape,
        scratch_shapes=scratch,
        compiler_params=_cparams("arbitrary", "arbitrary") if rider else _cparams("parallel", "parallel"),
    )(qkv, qkv, qkv, o, do, *(rider.inputs if rider else []))
    return (list(outs[:3]), list(outs[3:])) if rider else list(outs)


SSM_PAIRS = SSM_HEADS // 2
PAIRS_PER_GROUP = SSM_PAIRS // SSM_GROUPS
GROUP_WIDTH = SSM_WIDTH // SSM_GROUPS


def _silu(x):
    return x * jax.nn.sigmoid(x)


def _ssd_chunk(xs_pre, b_pre, c_pre, dt_raw, dt_raw_t, z, st, dt_bias_r, dt_bias_c, a_log_r, a_log_c, d_skip,
               gains):
    n = dt_raw.shape[0]
    rows = lax.broadcasted_iota(jnp.int32, (n, n), 0)
    cols = lax.broadcasted_iota(jnp.int32, (n, n), 1)
    tril = cols <= rows
    tri_l = tril.astype(f32)
    tri_u = (rows <= cols).astype(f32)
    lane = lax.broadcasted_iota(jnp.int32, (n, LANES), 1)
    sub = lax.broadcasted_iota(jnp.int32, (LANES, n), 0)

    dt = _softplus(dt_raw + dt_bias_r)
    a_r = -jnp.exp(a_log_r)
    da = dt * a_r
    acs = _mm_exact(tri_l, da)
    dt_t = _softplus(dt_raw_t + dt_bias_c)
    acs_t = _mm_exact(dt_t * (-jnp.exp(a_log_c)), tri_u)

    bs = [_silu(b) for b in b_pre]
    cs = [_silu(c) for c in c_pre]
    cb = [dmm_nt(cs[g], bs[g]) for g in range(SSM_GROUPS)]

    end = jnp.sum(da, axis=0, keepdims=True)
    lane_row = lax.broadcasted_iota(jnp.int32, (1, LANES), 1)
    first_head = lane < SSM_HEAD_DIM
    first_head_row = lane_row < SSM_HEAD_DIM

    def head_col(v, h):
        return jnp.sum(jnp.where((lane if v.shape[0] == n else lane_row) == h, v, 0.0), axis=1, keepdims=True)

    ys, st_new = [], []
    for p in range(SSM_PAIRS):
        g = p // PAIRS_PER_GROUP
        h0, h1 = 2 * p, 2 * p + 1
        xs = _silu(xs_pre[p])
        acols = [head_col(acs, h0), head_col(acs, h1)]
        dt_p = jnp.where(first_head, head_col(dt, h0), head_col(dt, h1))
        acs_p = jnp.where(first_head, acols[0], acols[1])
        end_p = jnp.where(first_head_row, head_col(end, h0), head_col(end, h1))
        dsk_p = jnp.where(first_head_row, head_col(d_skip, h0), head_col(d_skip, h1))
        xdt = xs * dt_p
        y = jnp.exp(acs_p) * dmm(cs[g], st[p])
        for hh in range(2):
            row = jnp.sum(jnp.where(sub == 2 * p + hh, acs_t, 0.0), axis=0, keepdims=True)
            decay = jnp.where(tril, jnp.exp(jnp.where(tril, acols[hh] - row, 0.0)), 0.0)
            head = first_head if hh == 0 else jnp.logical_not(first_head)
            y = y + dmm(cb[g] * decay, jnp.where(head, xdt, 0.0))
        st_new.append(jnp.exp(end_p) * st[p] + dmm_tn(bs[g], xdt * jnp.exp(end_p - acs_p)))
        ys.append(y + dsk_p * xs)
    out = []
    for g in range(SSM_GROUPS):
        yg = jnp.concatenate(ys[g * PAIRS_PER_GROUP:(g + 1) * PAIRS_PER_GROUP], axis=1) * _silu(z[g])
        out.append(_rms(yg, gains[g]))
    return out, st_new


def _ssd_chunk_inputs(xconv, dtr, z, st_ref, gain):
    xs_pre = [xconv[:, LANES * p:LANES * (p + 1)] for p in range(SSM_PAIRS)]
    b0 = SSM_WIDTH
    c0 = SSM_WIDTH + SSM_GROUPS * SSM_STATE
    b_pre = [xconv[:, b0 + SSM_STATE * g:b0 + SSM_STATE * (g + 1)] for g in range(SSM_GROUPS)]
    c_pre = [xconv[:, c0 + SSM_STATE * g:c0 + SSM_STATE * (g + 1)] for g in range(SSM_GROUPS)]
    zs = [z[:, GROUP_WIDTH * g:GROUP_WIDTH * (g + 1)] for g in range(SSM_GROUPS)]
    sts = [st_ref[p] for p in range(SSM_PAIRS)]
    gains = [gain[:, GROUP_WIDTH * g:GROUP_WIDTH * (g + 1)] for g in range(SSM_GROUPS)]
    return xs_pre, b_pre, c_pre, dtr, dtr.T, zs, sts, gains


def ssd_forward(xbc, dt_raw, z, cw, cb, dbr, dbc, alr, alc, dsk, gain, name):
    nb, ns, wx = xbc.shape
    ln = SSM_CHUNK
    nt = ns // ln
    tile = lambda c: pl.BlockSpec((1, ln, c), lambda b, j: (b, j, 0))
    st_spec = pl.BlockSpec((1, 1, SSM_PAIRS, SSM_STATE, LANES), lambda b, j: (b, j, 0, 0, 0))

    def body(xbc_ref, dt_ref, z_ref, cw_ref, cb_ref, dbr_ref, dbc_ref, alr_ref, alc_ref, dsk_ref, gain_ref,
             y_ref, xconv_ref, stp_ref, xin, st):
        @pl.when(pl.program_id(1) == 0)
        def _():
            xin[0:HALO, :] = jnp.zeros((HALO, wx), f32)
            st[...] = jnp.zeros_like(st)

        xin[HALO:HALO + ln, :] = xbc_ref[0]
        xconv = jnp.broadcast_to(cb_ref[...], (ln, wx))
        for k in range(SSM_CONV):
            xconv = xconv + cw_ref[k:k + 1, :] * xin[pl.ds(HALO - SSM_CONV + 1 + k, ln), :]
        xin[0:HALO, :] = xin[ln:ln + HALO, :]
        xconv_ref[0] = xconv
        stp_ref[0, 0] = st[...]
        xs_pre, b_pre, c_pre, dtr, dtr_t, zs, sts, gains = _ssd_chunk_inputs(xconv, dt_ref[0], z_ref[0], st,
                                                                             gain_ref[...])
        out, st_new = _ssd_chunk(xs_pre, b_pre, c_pre, dtr, dtr_t, zs, sts, dbr_ref[...], dbc_ref[...],
                                 alr_ref[...], alc_ref[...], dsk_ref[...], gains)
        y_ref[0] = jnp.concatenate(out, axis=1).astype(y_ref.dtype)
        for p in range(SSM_PAIRS):
            st[p] = st_new[p]

    params = [cw, cb, dbr, dbc, alr, alc, dsk, gain]
    return pl.pallas_call(
        body, name=name, grid=(nb, nt),
        in_specs=[tile(wx), tile(LANES), tile(SSM_WIDTH)] + [_const2(p.shape) for p in params],
        out_specs=[tile(SSM_WIDTH), tile(wx), st_spec],
        out_shape=[jax.ShapeDtypeStruct((nb, ns, SSM_WIDTH), MXU_DTYPE), jax.ShapeDtypeStruct((nb, ns, wx), f32),
                   jax.ShapeDtypeStruct((nb, nt, SSM_PAIRS, SSM_STATE, LANES), f32)],
        scratch_shapes=[pltpu.VMEM((ln + HALO, wx), f32), pltpu.VMEM((SSM_PAIRS, SSM_STATE, LANES), f32)],
        compiler_params=_cparams("arbitrary", "arbitrary"),
    )(xbc, dt_raw, z, *params)


def ssd_backward(dy, xbc, xconv, dt_raw, z, stp, cw, cb, dbr, dbc, alr, alc, dsk, gain, name):
    nb, ns, wx = xbc.shape
    ln = SSM_CHUNK
    nt = ns // ln
    per = ln // HALO
    rj = lambda j: nt - 1 - j
    tile = lambda c: pl.BlockSpec((1, ln, c), lambda b, j: (b, rj(j), 0))
    before = pl.BlockSpec((1, HALO, wx), lambda b, j: (b, jnp.maximum(rj(j) * per - 1, 0), 0))
    st_spec = pl.BlockSpec((1, 1, SSM_PAIRS, SSM_STATE, LANES), lambda b, j: (b, rj(j), 0, 0, 0))

    def body(dy_ref, xbc_ref, xbcb_ref, xconv_ref, dt_ref, z_ref, stp_ref,
             cw_ref, cb_ref, dbr_ref, dbc_ref, alr_ref, alc_ref, dsk_ref, gain_ref,
             dxbc_ref, ddt_ref, dz_ref, dcw_ref, dcb_ref, ddbr_ref, ddbc_ref, dalr_ref, dalc_ref, ddsk_ref, dgain_ref,
             dxc_ext, dst, xin):
        j = pl.program_id(1)
        first = _first_step()
        at_seq_start = j == nt - 1

        @pl.when(j == 0)
        def _():
            dxc_ext[ln:ln + HALO, :] = jnp.zeros((HALO, wx), f32)
            dst[...] = jnp.zeros_like(dst)

        xs_pre, b_pre, c_pre, dtr, dtr_t, zs, sts, gains = _ssd_chunk_inputs(xconv_ref[0], dt_ref[0], z_ref[0],
                                                                             stp_ref.at[0, 0], gain_ref[...])
        _, vjp = jax.vjp(_ssd_chunk, xs_pre, b_pre, c_pre, dtr, dtr_t, zs, sts, dbr_ref[...], dbc_ref[...],
                         alr_ref[...], alc_ref[...], dsk_ref[...], gains)
        dyv = dy_ref[0].astype(f32)
        cot = ([dyv[:, GROUP_WIDTH * g:GROUP_WIDTH * (g + 1)] for g in range(SSM_GROUPS)],
               [dst[p] for p in range(SSM_PAIRS)])
        dxs, db, dc, ddt, ddt_t, dzs, dsts, ddbr, ddbc, dalr, dalc, ddsk, dgains = vjp(cot)
        for p in range(SSM_PAIRS):
            dst[p] = dsts[p]
        ddt_ref[0] = (ddt + ddt_t.T).astype(ddt_ref.dtype)
        dz_ref[0] = jnp.concatenate(dzs, axis=1).astype(dz_ref.dtype)
        _accum(ddbr_ref, ddbr, first)
        _accum(ddbc_ref, ddbc, first)
        _accum(dalr_ref, dalr, first)
        _accum(dalc_ref, dalc, first)
        _accum(ddsk_ref, ddsk, first)
        _accum(dgain_ref, jnp.concatenate(dgains, axis=1), first)

        dxc = jnp.concatenate(dxs + db + dc, axis=1)
        _accum(dcb_ref, jnp.sum(dxc, axis=0, keepdims=True), first)
        dxc_ext[0:ln, :] = dxc
        dxp = jnp.zeros((ln, wx), f32)
        for k in range(SSM_CONV):
            dxp = dxp + cw_ref[k:k + 1, :] * dxc_ext[pl.ds(SSM_CONV - 1 - k, ln), :]
        dxbc_ref[0] = dxp.astype(dxbc_ref.dtype)
        dxc_ext[ln:ln + HALO, :] = dxc[0:HALO, :]

        xin[0:HALO, :] = jnp.where(at_seq_start, 0.0, xbcb_ref[0])
        xin[HALO:HALO + ln, :] = xbc_ref[0]
        dcw_rows = [jnp.sum(dxc * xin[pl.ds(HALO - SSM_CONV + 1 + k, ln), :], axis=0, keepdims=True)
                    for k in range(SSM_CONV)]
        dcw_rows += [jnp.zeros((1, wx), f32)] * (HALO - SSM_CONV)
        _accum(dcw_ref, jnp.concatenate(dcw_rows, axis=0), first)

    params = [cw, cb, dbr, dbc, alr, alc, dsk, gain]
    pshape = lambda p: jax.ShapeDtypeStruct(p.shape, f32)
    return pl.pallas_call(
        body, name=name, grid=(nb, nt),
        in_specs=[tile(SSM_WIDTH), tile(wx), before, tile(wx), tile(LANES), tile(SSM_WIDTH), st_spec]
        + [_const2(p.shape) for p in params],
        out_specs=[tile(wx), tile(LANES), tile(SSM_WIDTH), _const2((HALO, wx))] + [_const2(p.shape) for p in params[1:]],
        out_shape=[jax.ShapeDtypeStruct((nb, ns, wx), MXU_DTYPE), jax.ShapeDtypeStruct((nb, ns, LANES), MXU_DTYPE),
                   jax.ShapeDtypeStruct((nb, ns, SSM_WIDTH), MXU_DTYPE), jax.ShapeDtypeStruct((HALO, wx), f32)]
        + [pshape(p) for p in params[1:]],
        scratch_shapes=[pltpu.VMEM((ln + HALO, wx), f32), pltpu.VMEM((SSM_PAIRS, SSM_STATE, LANES), f32),
                        pltpu.VMEM((ln + HALO, wx), f32)],
        compiler_params=_cparams("arbitrary", "arbitrary"),
    )(dy, xbc, xbc, xconv, dt_raw, z, stp, *params)


CONF_HALO = 32
CONF_OFF = CONF_HALO - CONF_KERNEL + 1


def _conf_specs(ts, c, nt):
    per = ts // CONF_HALO
    tile = pl.BlockSpec((1, ts, c), lambda b, j: (b, j, 0))
    before = pl.BlockSpec((1, CONF_HALO, c), lambda b, j: (b, jnp.maximum(j * per - 1, 0), 0))
    after = pl.BlockSpec((1, CONF_HALO, c), lambda b, j: (b, jnp.minimum((j + 1) * per, nt * per - 1), 0))
    return tile, before, after


SUBLANES = 8


def _shifted_copies(dst, x):
    rows = x.shape[0]
    dst[0] = x
    for b in range(1, SUBLANES):
        dst[b] = pltpu.roll(x, rows - b, 0)


def _window(copies, off, size):
    b = off % SUBLANES
    return copies[b, pl.ds(off - b, size), :]


def _glu(x):
    return x[:, :CONF_WIDTH] * jax.nn.sigmoid(x[:, CONF_WIDTH:])


def _layernorm_parts(c):
    xc = c - jnp.mean(c, axis=-1, keepdims=True)
    r = lax.rsqrt(jnp.mean(xc * xc, axis=-1, keepdims=True) + EPS)
    return xc * r, r


def conf_forward(glu, cw, cb, ln_g, ln_b, name):
    nb, ns, wg = glu.shape
    w = CONF_WIDTH
    ts = SEQ_TILE
    nt = ns // ts
    tile, before, _ = _conf_specs(ts, wg, nt)

    def body(x_ref, xb_ref, cw_ref, cb_ref, g_ref, b_ref, y_ref, u_rot):
        _shifted_copies(u_rot, jnp.concatenate(
            [jnp.where(pl.program_id(1) == 0, 0.0, _glu(xb_ref[0])), _glu(x_ref[0])], axis=0))
        conv = jnp.broadcast_to(cb_ref[...], (ts, w))
        for k in range(CONF_KERNEL):
            conv = conv + cw_ref[k:k + 1, :] * _window(u_rot, CONF_OFF + k, ts)
        xhat, _ = _layernorm_parts(conv)
        y_ref[0] = _silu(xhat * g_ref[...] + b_ref[...]).astype(y_ref.dtype)

    params = [cw, cb, ln_g, ln_b]
    return pl.pallas_call(
        body, name=name, grid=(nb, nt),
        in_specs=[tile, before] + [_const2(p.shape) for p in params],
        out_specs=pl.BlockSpec((1, ts, w), lambda b, j: (b, j, 0)),
        out_shape=jax.ShapeDtypeStruct((nb, ns, w), MXU_DTYPE),
        scratch_shapes=[pltpu.VMEM((SUBLANES, ts + CONF_HALO, w), f32)],
        compiler_params=_cparams("parallel", "parallel"),
    )(glu, glu, *params)


def conf_backward(dy, glu, cw, cb, ln_g, ln_b, name):
    nb, ns, wg = glu.shape
    w = CONF_WIDTH
    ts = SEQ_TILE
    nt = ns // ts
    te = ts + CONF_HALO
    tile, before, after = _conf_specs(ts, wg, nt)
    dtile, _, dafter = _conf_specs(ts, w, nt)

    def body(dy_ref, dya_ref, x_ref, xb_ref, xa_ref, cw_ref, cb_ref, g_ref, b_ref,
             dx_ref, dcw_ref, dcb_ref, dg_ref, db_ref, u_ext, dc_ext):
        j = pl.program_id(1)
        first = _first_step()
        x = x_ref[0]
        _shifted_copies(u_ext, jnp.concatenate(
            [jnp.where(j == 0, 0.0, _glu(xb_ref[0])), _glu(x), _glu(xa_ref[0])], axis=0))
        conv = jnp.broadcast_to(cb_ref[...], (te, w))
        for k in range(CONF_KERNEL):
            conv = conv + cw_ref[k:k + 1, :] * _window(u_ext, CONF_OFF + k, te)
        xhat, r = _layernorm_parts(conv)
        lnout = xhat * g_ref[...] + b_ref[...]
        sg = jax.nn.sigmoid(lnout)
        rows = lax.broadcasted_iota(jnp.int32, (te, w), 0)
        dyv = jnp.concatenate([dy_ref[0].astype(f32), dya_ref[0].astype(f32)], axis=0)
        dyv = jnp.where(jnp.logical_and(j == nt - 1, rows >= ts), 0.0, dyv)
        dln = dyv * sg * (1.0 + lnout * (1.0 - sg))
        in_tile = rows < ts
        _accum(dg_ref, jnp.sum(jnp.where(in_tile, dln * xhat, 0.0), axis=0, keepdims=True), first)
        _accum(db_ref, jnp.sum(jnp.where(in_tile, dln, 0.0), axis=0, keepdims=True), first)
        dxh = dln * g_ref[...]
        dconv = r * (dxh - jnp.mean(dxh, axis=-1, keepdims=True) - xhat * jnp.mean(dxh * xhat, axis=-1, keepdims=True))
        _shifted_copies(dc_ext, dconv)
        dct = dconv[0:ts, :]
        _accum(dcb_ref, jnp.sum(dct, axis=0, keepdims=True), first)
        du = jnp.zeros((ts, w), f32)
        dcw_rows = []
        for k in range(CONF_KERNEL):
            du = du + cw_ref[k:k + 1, :] * _window(dc_ext, CONF_KERNEL - 1 - k, ts)
            dcw_rows.append(jnp.sum(dct * _window(u_ext, CONF_OFF + k, ts), axis=0, keepdims=True))
        dcw_rows.append(jnp.zeros((1, w), f32))
        _accum(dcw_ref, jnp.concatenate(dcw_rows, axis=0), first)
        sb = jax.nn.sigmoid(x[:, w:])
        dx_ref[0] = jnp.concatenate([du * sb, du * x[:, :w] * sb * (1.0 - sb)], axis=1).astype(dx_ref.dtype)

    params = [cw, cb, ln_g, ln_b]
    return pl.pallas_call(
        body, name=name, grid=(nb, nt),
        in_specs=[dtile, dafter, tile, before, after] + [_const2(p.shape) for p in params],
        out_specs=[tile] + [_const2(p.shape) for p in params],
        out_shape=[jax.ShapeDtypeStruct((nb, ns, wg), MXU_DTYPE)] + [jax.ShapeDtypeStruct(p.shape, f32) for p in params],
        scratch_shapes=[pltpu.VMEM((SUBLANES, te + CONF_HALO, w), f32), pltpu.VMEM((SUBLANES, te, w), f32)],
        compiler_params=_cparams("arbitrary", "arbitrary"),
    )(dy, dy, glu, glu, glu, *params)


def _row(v):
    return v.reshape(1, -1).astype(f32)


def _pad_to(v, n, axis):
    pads = [(0, 0)] * v.ndim
    pads[axis] = (0, n - v.shape[axis])
    return jnp.pad(v, pads)


def _block_diag(w):
    nh, d, _ = w.shape
    eye = jnp.eye(nh, dtype=w.dtype)
    return (eye[:, None, :, None] * w[:, :, None, :]).reshape(nh * d, nh * d)


def _diag_blocks(m, nh):
    d = m.shape[0] // nh
    idx = jnp.arange(nh)
    return m.reshape(nh, d, nh, d)[idx, :, idx, :]


def _mix_even_fwd(h, gpre, w, wl, nb, ns, rider=None):
    t = nb * ns
    w_in = wl["w_in"]
    w_lx, w_lg = Part(w_in, 0, LRU_WIDTH, 1), Part(w_in, LRU_WIDTH, LRU_WIDTH, 1)
    w_qkv = Part(w_in, 2 * LRU_WIDTH, 3 * SB_WIDTH, 1)
    xpre, gate, qkv = norm_matmul(h, gpre, [w_lx, w_lg, w_qkv], [f32, f32, f32], name="ev_in_proj")
    lru_p = [w["ev_lru_conv_w"][0], _row(w["ev_lru_conv_b"][0]),
             _block_diag(w["ev_lru_gate_a_w"][0]).astype(MXU_DTYPE), _row(w["ev_lru_gate_a_b"][0]),
             _block_diag(w["ev_lru_gate_x_w"][0]).astype(MXU_DTYPE), _row(w["ev_lru_gate_x_b"][0]),
             _row(w["ev_lru_lambda"][0])]
    xpre3, gate3, qkv3 = xpre.reshape(nb, ns, -1), gate.reshape(nb, ns, -1), qkv.reshape(nb, ns, -1)
    y_a, xc, hs = lru_forward(xpre3, gate3, *lru_p, name="ev_lru_fwd")
    o = sb_forward(qkv3, name="ev_sb_fwd", rider=rider)
    carried = None
    if rider is not None:
        o, carried = o
    ys = [y_a.reshape(t, -1), o.reshape(t, -1)]
    saved = dict(xpre=xpre3, gate=gate3, qkv=qkv3, xc=xc, hs=hs, o=o, lru_p=lru_p)
    return ys, saved, carried


def _mix_even_bwd(dys, saved, wtl, nb, ns, lru_rider=None, attention_rider=None):
    t = nb * ns
    dy_a, dy_b = [d.reshape(nb, ns, -1) for d in dys]
    outs = lru_backward(dy_a, saved["xpre"], saved["gate"], saved["xc"], saved["hs"], *saved["lru_p"],
                        name="ev_lru_bwd", rider=lru_rider)
    lru_carried = None
    if lru_rider is not None:
        outs, lru_carried = outs
    dxp, dgt, dcw, dcb, dga, dgab, dgx, dgxb, dlam = outs
    rider = attention_rider(lru_carried) if attention_rider is not None else None
    carried = None
    if rider is None:
        dq, dk, dv = sb_backward(saved["qkv"], saved["o"], dy_b, name="ev_sb_bwd")
    else:
        (dq, dk, dv), carried = sb_backward(saved["qkv"], saved["o"], dy_b, name="ev_sb_bwd", rider=rider)
    w_in_t = wtl["w_in"]
    pieces = [dxp, dgt, dq, dk, dv]
    gs = [d.reshape(t, -1) for d in pieces]
    wts = [Part(w_in_t, LRU_WIDTH * i, LRU_WIDTH, 0) for i in range(5)]
    grads = {
        "ev_lru_conv_w": dcw[:LRU_CONV][None], "ev_lru_conv_b": dcb,
        "ev_lru_gate_a_w": _diag_blocks(dga, LRU_HEADS)[None], "ev_lru_gate_a_b": dgab,
        "ev_lru_gate_x_w": _diag_blocks(dgx, LRU_HEADS)[None], "ev_lru_gate_x_b": dgxb,
        "ev_lru_lambda": dlam,
    }
    return gs, wts, grads, carried


def _odd_params(w):
    ssd_p = [w["od_ssm_conv_w"][0], _row(w["od_ssm_conv_b"][0]),
             _pad_to(_row(w["od_ssm_dt_bias"][0]), LANES, 1), _pad_to(_row(w["od_ssm_dt_bias"][0]), LANES, 1).T,
             _pad_to(_row(w["od_ssm_a_log"][0]), LANES, 1), _pad_to(_row(w["od_ssm_a_log"][0]), LANES, 1).T,
             _pad_to(_row(w["od_ssm_d"][0]), LANES, 1), _row(w["od_ssm_norm"][0])]
    conf_p = [_pad_to(w["od_cm_conv_w"][0], CONF_HALO, 0), _row(w["od_cm_conv_b"][0]),
              _row(w["od_cm_ln_g"][0]), _row(w["od_cm_ln_b"][0])]
    return ssd_p, conf_p


ODD_SPLITS = (SSM_WIDTH, SSM_WIDTH + SSM_XBC, SSM_WIDTH + SSM_XBC + SSM_HEADS)


def _mix_odd_fwd(h, gpre, w, wl, nb, ns, rider=None):
    assert rider is None
    t = nb * ns
    w_in = wl["w_in"]
    s0, s1, s2 = ODD_SPLITS
    w_al = jnp.concatenate([w_in[:, :s1], w_in[:, s2:], _pad_to(w_in[:, s1:s2], LANES, 1)], axis=1)
    widths = (s0, s1 - s0, w_in.shape[1] - s2, LANES)
    starts = (0, s0, s1, s1 + widths[2])
    zz, xbc, glu, dtr = norm_matmul(h, gpre, [Part(w_al, a, n, 1) for a, n in zip(starts, widths)], [f32] * 4,
                                    name="od_in_proj")
    ssd_p, conf_p = _odd_params(w)
    zz3, xbc3, dtr3, glu3 = [a.reshape(nb, ns, -1) for a in (zz, xbc, dtr, glu)]
    y_c, xconv, stp = ssd_forward(xbc3, dtr3, zz3, *ssd_p, name="od_ssd_fwd")
    y_d = conf_forward(glu3, *conf_p, name="od_conf_fwd")
    ys = [y_c.reshape(t, -1), y_d.reshape(t, -1)]
    saved = dict(z=zz3, xbc=xbc3, dtr=dtr3, glu=glu3, xconv=xconv, stp=stp, ssd_p=ssd_p, conf_p=conf_p)
    return ys, saved, None


def _mix_odd_bwd(dys, saved, wtl, nb, ns, lru_rider=None, attention_rider=None):
    assert lru_rider is None and attention_rider is None
    t = nb * ns
    dy_c, dy_d = [d.reshape(nb, ns, -1) for d in dys]
    outs = ssd_backward(dy_c, saved["xbc"], saved["xconv"], saved["dtr"], saved["z"], saved["stp"], *saved["ssd_p"],
                        name="od_ssd_bwd")
    dxbc, ddt, dz, dcw, dcb, ddbr, ddbc, dalr, dalc, ddsk, dgain = outs
    dglu, ccw, ccb, clg, clb = conf_backward(dy_d, saved["glu"], *saved["conf_p"], name="od_conf_bwd")
    w_in_t = wtl["w_in"]
    s0, s1, s2 = ODD_SPLITS
    carried = None
    gs = [d.reshape(t, -1) for d in (dz, dxbc, dglu, ddt)]
    wt_al = jnp.concatenate([w_in_t[:s1], w_in_t[s2:], _pad_to(w_in_t[s1:s2], LANES, 0)], axis=0)
    widths = (s0, s1 - s0, w_in_t.shape[0] - s2, LANES)
    starts = (0, s0, s1, s1 + widths[2])
    wts = [Part(wt_al, a, n, 0) for a, n in zip(starts, widths)]
    nh = SSM_HEADS
    grads = {
        "od_ssm_conv_w": dcw[:SSM_CONV][None], "od_ssm_conv_b": dcb,
        "od_ssm_dt_bias": ddbr[:, :nh] + ddbc[:nh, 0][None], "od_ssm_a_log": dalr[:, :nh] + dalc[:nh, 0][None],
        "od_ssm_d": ddsk[:, :nh], "od_ssm_norm": dgain,
        "od_cm_conv_w": ccw[:CONF_KERNEL][None], "od_cm_conv_b": ccb, "od_cm_ln_g": clg, "od_cm_ln_b": clb,
    }
    return gs, wts, grads, carried


LAYER_MATRICES = ("w_in", "w_out", "mlp_w1", "mlp_w2", "ple_w_proj", "ple_w_gate")
NORM_NAMES = ("norm_mix_pre", "norm_mix_post", "norm_mlp_pre", "norm_mlp_post", "norm_ple")


class NoOverlap:
    sums, from_chips = {}, {}

    def attention_fwd_rider(self):
        return None

    def weights_arrived(self, carried, wl, wtl):
        raise NotImplementedError

    def mlp_bwd_rider(self, layer1_grads):
        return None

    def after_mlp_bwd(self, carried):
        pass

    def lru_bwd_rider(self, layer0_grads):
        return None

    def attention_bwd_rider(self, carried):
        return None

    def after_attention_bwd(self, carried):
        pass


OUT_SPLIT = (LRU_WIDTH, SSM_WIDTH)


def local_step(x, p, target, w, wl, wtl, comm=NoOverlap()):
    nb, ns, d = x.shape
    t = nb * ns
    h = x.reshape(t, d)
    depth = p.shape[0]
    wl, wtl = list(wl), list(wtl)
    tapes = []
    for i in range(depth):
        even = i % 2 == 0
        tag = f"l{i}_"
        gpre = _row(w["norm_mix_pre"][i])
        rider = comm.attention_fwd_rider() if i == 0 else None
        ys, saved, carried = (_mix_even_fwd if even else _mix_odd_fwd)(h, gpre, w, wl[i], nb, ns, rider)
        if rider is not None:
            wl, wtl = comm.weights_arrived(carried, wl, wtl)
        w_out = wl[i]["w_out"]
        split = OUT_SPLIT[i % 2]
        w_outs = [Part(w_out, 0, split, 0), Part(w_out, split, w_out.shape[0] - split, 0)]
        h1, m = matmul_residual_norm(ys, w_outs, h, _row(w["norm_mix_post"][i]), name=tag + "out_proj")
        a1, = norm_matmul(h1, _row(w["norm_mlp_pre"][i]), [wl[i]["mlp_w1"]], [MXU_DTYPE], name=tag + "mlp_up")
        h2, f = matmul_residual_norm([a1], [wl[i]["mlp_w2"]], h1, _row(w["norm_mlp_post"][i]), name=tag + "mlp_down",
                                     relu2=True)
        pi = p[i].reshape(t, -1)
        h3, gl, emb = ple_forward(h2, pi, wl[i]["ple_w_gate"], wl[i]["ple_w_proj"], _row(w["norm_ple"][i]),
                                  name=tag + "ple")
        tapes.append(dict(h=h, ys=ys, w_outs=w_outs, saved=saved, h1=h1, m=m, a1=a1, h2=h2, f=f, pi=pi, gl=gl,
                          emb=emb))
        h = h3

    loss_row, dh = loss_and_grad(h, target.reshape(t, d), name="loss")
    grads = {}
    norm_grads = {k: [None] * depth for k in NORM_NAMES}
    layer_grads = [None] * depth
    for i in reversed(range(depth)):
        even = i % 2 == 0
        tag = f"l{i}_"
        tp = tapes[i]
        lg = {}
        to_sibling = comm.mlp_bwd_rider(layer_grads[1]) if i == 0 else None
        dh2, dgl, demb, dg = ple_backward(dh, tp["h2"], tp["gl"], tp["emb"], _row(w["norm_ple"][i]),
                                          wtl[i]["ple_w_gate"], name=tag + "ple_bwd")
        norm_grads["norm_ple"][i] = dg
        lg["ple_w_gate"] = weight_grad(tp["h2"], dgl, name=tag + "dw_gate")
        lg["ple_w_proj"] = weight_grad(tp["pi"], demb, name=tag + "dw_proj")
        outs = bwd_through_norm_out(dh2, tp["f"], _row(w["norm_mlp_post"][i]), [wtl[i]["mlp_w2"]], [MXU_DTYPE],
                                    name=tag + "mlp_down_bwd", relu2_of=tp["a1"], rider=to_sibling)
        d_f, (da1,), dg = outs[:3]
        if to_sibling is not None:
            comm.after_mlp_bwd(outs[3])
        norm_grads["norm_mlp_post"][i] = dg
        lg["mlp_w2"] = weight_grad(tp["a1"], d_f, name=tag + "dw2", prologue="relu2")
        gpre = _row(w["norm_mlp_pre"][i])
        dh1, dg = bwd_through_norm_in(dh2, [da1], [wtl[i]["mlp_w1"]], tp["h1"], gpre, name=tag + "mlp_up_bwd")
        norm_grads["norm_mlp_pre"][i] = dg
        lg["mlp_w1"] = weight_grad(tp["h1"], da1, name=tag + "dw1", prologue="rms", gain=gpre)
        wt_out = wtl[i]["w_out"]
        split = tp["w_outs"][0].shape[0]
        dm, dys, dg = bwd_through_norm_out(dh1, tp["m"], _row(w["norm_mix_post"][i]),
                                           [Part(wt_out, 0, split, 1),
                                            Part(wt_out, split, wt_out.shape[1] - split, 1)],
                                           [f32, MXU_DTYPE if even else f32],
                                           name=tag + "out_proj_bwd")
        norm_grads["norm_mix_post"][i] = dg
        lg["w_out"] = jnp.concatenate([weight_grad(y, dm, name=tag + f"dw_out{k}") for k, y in enumerate(tp["ys"])],
                                      axis=0)
        lru_rider = comm.lru_bwd_rider(lg) if i == 0 else None
        gs, wts, mix_grads, carried = (_mix_even_bwd if even else _mix_odd_bwd)(
            dys, tp["saved"], wtl[i], nb, ns, lru_rider, comm.attention_bwd_rider if lru_rider is not None else None)
        if carried is not None:
            comm.after_attention_bwd(carried)
        grads.update(mix_grads)
        gpre = _row(w["norm_mix_pre"][i])
        dh, dg = bwd_through_norm_in(dh1, gs, wts, tp["h"], gpre, name=tag + "in_proj_bwd")
        norm_grads["norm_mix_pre"][i] = dg
        dw_in = weight_grads_of_norm(tp["h"], gpre, gs, name=tag + "dw_in")
        if not even:
            dw_in = [dw_in[0], dw_in[1], dw_in[3][:, :SSM_HEADS], dw_in[2]]
        lg["w_in"] = jnp.concatenate(dw_in, axis=1)
        layer_grads[i] = lg
    for k, v in norm_grads.items():
        grads[k] = jnp.concatenate(v, axis=0)
    return loss_row[0, 0], dh.reshape(nb, ns, d), grads, layer_grads


MESH_ID = pl.DeviceIdType.MESH
ANY = pl.BlockSpec(memory_space=pl.ANY)


def _mesh_pos():
    return lax.axis_index("x"), lax.axis_index("y"), lax.axis_index("c")


def all_gather(shards, name):
    return _run_alone(gather_rider(shards), name)


class Rider:
    def __init__(self, inputs, out_shapes, scratch_shapes, start, finish, middle=None):
        self.inputs, self.out_shapes, self.scratch_shapes = list(inputs), list(out_shapes), list(scratch_shapes)
        self.start, self.finish, self.middle = start, finish, middle


def _run_alone(rider, name):
    ni, no = len(rider.inputs), len(rider.out_shapes)

    def body(*refs):
        args = (refs[:ni], refs[ni:ni + no], refs[ni + no:])
        rider.start(*args)
        if rider.middle is not None:
            rider.middle(*args)
        rider.finish(*args)

    return pl.pallas_call(
        body, name=name, out_shape=rider.out_shapes, in_specs=[ANY] * ni, out_specs=[ANY] * no,
        scratch_shapes=rider.scratch_shapes,
    )(*rider.inputs)


def _ride(rider, body, in_specs, out_specs, out_shape, scratch_shapes, grid):
    in_specs, out_specs, out_shape = list(in_specs), list(out_specs), list(out_shape)
    scratch_shapes = list(scratch_shapes)
    if rider is None:
        return body, in_specs, out_specs, out_shape, scratch_shapes
    n_in, n_out, n_scr = len(in_specs), len(out_specs), len(scratch_shapes)
    ri, ro = len(rider.inputs), len(rider.out_shapes)
    total = math.prod(grid)

    def carrying(*refs):
        ins, r_ins = refs[:n_in], refs[n_in:n_in + ri]
        o0 = n_in + ri
        outs, r_outs = refs[o0:o0 + n_out], refs[o0 + n_out:o0 + n_out + ro]
        s0 = o0 + n_out + ro
        scr, r_scr = refs[s0:s0 + n_scr], refs[s0 + n_scr:]
        step = pl.program_id(0)
        for ax in range(1, len(grid)):
            step = step * grid[ax] + pl.program_id(ax)
        args = (r_ins, r_outs, r_scr)
        pl.when(step == 0)(lambda: rider.start(*args))
        if rider.middle is not None:
            pl.when(step == total // 2)(lambda: rider.middle(*args))
        body(*ins, *outs, *scr)
        pl.when(step == total - 1)(lambda: rider.finish(*args))

    return (carrying, in_specs + [ANY] * ri, out_specs + [ANY] * ro, out_shape + rider.out_shapes,
            scratch_shapes + rider.scratch_shapes)


def gather_rider(shards):
    n = len(shards)

    def parts(x_refs, out_refs, scr):
        send_sems, recv_sems, local_sems = scr
        x, y, c = _mesh_pos()
        chips = [(1 - x, y), (x, 1 - y), (1 - x, 1 - y)]

        def slot(a, px, py, pc):
            return out_refs[a].at[4 * px + 2 * py + pc]

        def copy(a, k, block, to, src=None):
            return pltpu.make_async_remote_copy(
                src_ref=slot(a, *block) if src is None else src, dst_ref=slot(a, *block),
                send_sem=send_sems.at[7 * a + k], recv_sem=recv_sems.at[7 * a + k], device_id=to,
                device_id_type=MESH_ID)

        me, sibling = (x, y, c), (x, y, 1 - c)
        def mine():
            return [pltpu.make_async_copy(x_refs[a], slot(a, *me), local_sems.at[a]) for a in range(n)]

        def first():
            out = []
            for j, chip in enumerate(chips):
                out += [copy(a, 1 + j, me, (*chip, c), src=x_refs[a]) for a in range(n)]
            return out + [copy(a, 0, me, sibling, src=x_refs[a]) for a in range(n)]

        def passed(j):
            return [copy(a, 4 + j, (*chips[j], c), sibling) for a in range(n)]

        return me, sibling, chips, c, copy, mine, first, passed

    def start(x_refs, out_refs, scr):
        _, _, _, _, _, mine, first, _ = parts(x_refs, out_refs, scr)
        for cp in mine() + first():
            cp.start()

    def middle(x_refs, out_refs, scr):
        me, _, chips, c, copy, _, _, passed = parts(x_refs, out_refs, scr)
        for j, chip in enumerate(chips):
            for a, fwd in enumerate(passed(j)):
                copy(a, 1 + j, (*chip, c), me).wait_recv()
                fwd.start()

    def finish(x_refs, out_refs, scr):
        me, sibling, chips, c, copy, mine, first, passed = parts(x_refs, out_refs, scr)
        for a in range(n):
            copy(a, 0, sibling, me).wait_recv()
        for j, chip in enumerate(chips):
            for a in range(n):
                copy(a, 4 + j, (*chip, 1 - c), me).wait_recv()
        for cp in first() + [cp for j in range(len(chips)) for cp in passed(j)]:
            cp.wait_send()
        for cp in mine():
            cp.wait()

    return Rider(shards, [jax.ShapeDtypeStruct((N_DEV,) + s.shape, s.dtype) for s in shards],
                 [pltpu.SemaphoreType.DMA((7 * n,)), pltpu.SemaphoreType.DMA((7 * n,)), pltpu.SemaphoreType.DMA((n,))],
                 start, finish, middle)


def scatter_to_sibling(parts, name):
    return _run_alone(sibling_rider(parts), name)


def sibling_rider(parts):
    n = len(parts)

    def copies(g_refs, out_refs, scr):
        send_sems, recv_sems = scr
        x, y, c = _mesh_pos()
        return [pltpu.make_async_remote_copy(
            src_ref=g_refs[a].at[2 * chip + (1 - c)], dst_ref=out_refs[a].at[chip],
            send_sem=send_sems.at[4 * a + chip], recv_sem=recv_sems.at[4 * a + chip], device_id=(x, y, 1 - c),
            device_id_type=MESH_ID) for a in range(n) for chip in range(4)]

    def start(*refs):
        for cp in copies(*refs):
            cp.start()

    def finish(*refs):
        cps = copies(*refs)
        for cp in cps:
            cp.wait_recv()
        for cp in cps:
            cp.wait_send()

    return Rider(parts, [jax.ShapeDtypeStruct((4,) + p.shape[1:], p.dtype) for p in parts],
                 [pltpu.SemaphoreType.DMA((4 * n,)), pltpu.SemaphoreType.DMA((4 * n,))], start, finish)


def scatter_to_chips(partials, name):
    return _run_alone(chips_rider(partials), name)


def chips_rider(partials):
    n = len(partials)

    def copies(p_refs, out_refs, scr):
        send_sems, recv_sems = scr
        x, y, c = _mesh_pos()
        chips = [(1 - x, y), (x, 1 - y), (1 - x, 1 - y)]
        return [pltpu.make_async_remote_copy(
            src_ref=p_refs[a].at[2 * px + py], dst_ref=out_refs[a].at[j],
            send_sem=send_sems.at[3 * a + j], recv_sem=recv_sems.at[3 * a + j], device_id=(px, py, c),
            device_id_type=MESH_ID) for a in range(n) for j, (px, py) in enumerate(chips)]

    def start(*refs):
        for cp in copies(*refs):
            cp.start()

    def finish(*refs):
        cps = copies(*refs)
        for cp in cps:
            cp.wait_recv()
        for cp in cps:
            cp.wait_send()

    return Rider(partials, [jax.ShapeDtypeStruct((3,) + p.shape[1:], p.dtype) for p in partials],
                 [pltpu.SemaphoreType.DMA((3 * n,)), pltpu.SemaphoreType.DMA((3 * n,))], start, finish)


ICI_DTYPE = jnp.bfloat16
ELEMENTWISE_BLOCK_BYTES = 1 << 20


def _row_tile(rows, cols):
    cap = max(16, ELEMENTWISE_BLOCK_BYTES // (4 * cols))
    best = [t for t in range(16, min(rows, cap) + 1, 16) if rows % t == 0]
    return best[-1] if best else rows


def add_sibling_parts(parts, received, core, name):
    _, r, n = parts.shape
    tr = _row_tile(r, n)

    def body(c_ref, a_ref, b_ref, o_ref, ob_ref):
        s = a_ref[...] + b_ref[...]
        o_ref[...] = s
        ob_ref[...] = s.astype(ob_ref.dtype)

    blk = pl.BlockSpec((1, tr, n), lambda i, j, c_ref: (i, j, 0))
    return pl.pallas_call(
        body, name=name,
        grid_spec=pltpu.PrefetchScalarGridSpec(
            num_scalar_prefetch=1, grid=(4, r // tr),
            in_specs=[pl.BlockSpec((1, tr, n), lambda i, j, c_ref: (2 * i + c_ref[0], j, 0)), blk],
            out_specs=[blk, blk]),
        out_shape=[jax.ShapeDtypeStruct((4, r, n), f32), jax.ShapeDtypeStruct((4, r, n), ICI_DTYPE)],
        compiler_params=_cparams("parallel", "parallel"),
    )(core, parts, received)


def _adamw(w, g, m, v):
    m = ADAM_B1 * m + (1.0 - ADAM_B1) * g
    v = ADAM_B2 * v + (1.0 - ADAM_B2) * jnp.square(g)
    m_hat = m / (1.0 - ADAM_B1 ** ADAM_STEP)
    v_hat = v / (1.0 - ADAM_B2 ** ADAM_STEP)
    delta = -ADAM_LR * (m_hat / (jnp.sqrt(v_hat) + ADAM_EPS) + ADAM_WD * w)
    return delta, m, v


def adamw_sharded(partial, received, chip, w, m, v, name):
    _, r, n = partial.shape

    def body(k_ref, p_ref, r_ref, w_ref, m_ref, v_ref, g_out, d_out, m_out, v_out):
        g = p_ref[0] + r_ref[0].astype(f32)
        g = g + r_ref[1].astype(f32)
        g = g + r_ref[2].astype(f32)
        delta, mn, vn = _adamw(w_ref[...], g, m_ref[...], v_ref[...])
        g_out[...] = g
        d_out[...] = delta
        m_out[...] = mn
        v_out[...] = vn

    tr = _row_tile(r, n)
    flat = pl.BlockSpec((tr, n), lambda j, k_ref: (j, 0))
    return pl.pallas_call(
        body, name=name,
        grid_spec=pltpu.PrefetchScalarGridSpec(
            num_scalar_prefetch=1, grid=(r // tr,),
            in_specs=[pl.BlockSpec((1, tr, n), lambda j, k_ref: (k_ref[0], j, 0)),
                      pl.BlockSpec((3, tr, n), lambda j, k_ref: (0, j, 0)), flat, flat, flat],
            out_specs=[flat] * 4),
        out_shape=[jax.ShapeDtypeStruct((r, n), f32)] * 4,
        compiler_params=_cparams("parallel"),
    )(chip, partial, received, w, m, v)


def adamw_replicated(gathered, w, m, v, name):
    _, r, n = gathered.shape

    def body(g_ref, w_ref, m_ref, v_ref, g_out, d_out, m_out, v_out):
        g = g_ref[0]
        for k in range(1, N_DEV):
            g = g + g_ref[k]
        delta, mn, vn = _adamw(w_ref[...], g, m_ref[...], v_ref[...])
        g_out[...] = g
        d_out[...] = delta
        m_out[...] = mn
        v_out[...] = vn

    return pl.pallas_call(
        body, name=name,
        out_shape=[jax.ShapeDtypeStruct((r, n), f32)] * 4,
        compiler_params=pltpu.CompilerParams(vmem_limit_bytes=VMEM_LIMIT),
    )(gathered, w, m, v)


W_NAMES = ['ev_w_in', 'ev_lru_conv_w', 'ev_lru_conv_b', 'ev_lru_gate_a_w', 'ev_lru_gate_a_b', 'ev_lru_gate_x_w',
           'ev_lru_gate_x_b', 'ev_lru_lambda', 'ev_w_out', 'od_w_in', 'od_ssm_conv_w', 'od_ssm_conv_b',
           'od_ssm_dt_bias', 'od_ssm_a_log', 'od_ssm_d', 'od_ssm_norm', 'od_cm_conv_w', 'od_cm_conv_b', 'od_cm_ln_g',
           'od_cm_ln_b', 'od_w_out', 'norm_mix_pre', 'norm_mix_post', 'norm_mlp_pre', 'norm_mlp_post', 'norm_ple',
           'mlp_w1', 'mlp_w2', 'ple_w_proj', 'ple_w_gate']
BIG_SHARDED = {'ev_w_in': 2, 'ev_w_out': 1, 'od_w_in': 2, 'od_w_out': 1, 'mlp_w1': 2, 'mlp_w2': 1, 'ple_w_proj': 2,
               'ple_w_gate': 1}
SMALL_SHARDED = {'ev_lru_conv_w': 2, 'od_ssm_conv_w': 2, 'od_ssm_conv_b': 1, 'od_ssm_norm': 1, 'od_cm_conv_w': 2,
                 'od_cm_conv_b': 1, 'od_cm_ln_g': 1, 'od_cm_ln_b': 1}
SHARDED = {**BIG_SHARDED, **SMALL_SHARDED}
REPLICATED = [n for n in W_NAMES if n not in SHARDED]


def _round_up(n, k):
    return -(-n // k) * k


def _pack_rows(flat, rows_multiple):
    n = flat.shape[0]
    total = _round_up(n, LANES * rows_multiple)
    return jnp.pad(flat, (0, total - n)).reshape(-1, LANES)


def _unpack(flat, shapes):
    out, off = {}, 0
    for name, shape in shapes.items():
        size = math.prod(shape)
        out[name] = flat[off:off + size].reshape(shape)
        off += size
    return out


def _rows(a):
    return a.reshape(-1, a.shape[-1])


def _unshard(g8, shape, axis):
    g = jnp.moveaxis(g8.reshape((N_DEV,) + tuple(shape)), 0, axis)
    return g.reshape(tuple(shape[:axis]) + (N_DEV * shape[axis],) + tuple(shape[axis + 1:]))


def _to_shards(g, axis):
    shard = g.shape[axis] // N_DEV
    g = g.reshape(g.shape[:axis] + (N_DEV, shard) + g.shape[axis + 1:])
    return jnp.moveaxis(g, axis, 0)


def _pack_small(tree):
    return _pack_rows(jnp.concatenate([tree[k].astype(f32).reshape(-1) for k in SMALL_SHARDED]), 16)


def _layer_entry(i, key):
    mixer = "ev" if i % 2 == 0 else "od"
    return {"w_in": (mixer + "_w_in", i // 2, 1), "w_out": (mixer + "_w_out", i // 2, 0),
            "mlp_w1": ("mlp_w1", i, 1), "mlp_w2": ("mlp_w2", i, 0),
            "ple_w_proj": ("ple_w_proj", i, 1), "ple_w_gate": ("ple_w_gate", i, 0)}[key]


def _layer_shards(tree, i):
    out = {}
    for key in LAYER_MATRICES:
        name, idx, _ = _layer_entry(i, key)
        out[key] = tree[name][idx]
    return out


def _assemble_layer(i, gathered):
    wl = {key: _unshard(g8, g8.shape[1:], _layer_entry(i, key)[2]) for key, g8 in zip(LAYER_MATRICES, gathered)}
    return wl, {key: v.T for key, v in wl.items()}


def _gather_early(w):
    small = _pack_small(w)
    terms, rest = [], small
    for _ in range(3):
        term = rest.astype(MXU_DTYPE)
        terms.append(term)
        rest = rest - term.astype(f32)
    outs = all_gather([_layer_shards(w, 0)["w_in"].astype(MXU_DTYPE), jnp.concatenate(terms, axis=0)],
                      name="gather_weights")
    w_in = _unshard(outs[0], outs[0].shape[1:], _layer_entry(0, "w_in")[2])
    wl, wtl = {"w_in": w_in}, {"w_in": w_in.T}
    full = {k: w[k] for k in REPLICATED}
    t = outs[-1].astype(f32)
    nr = small.shape[0]
    vals = (t[:, :nr] + t[:, nr:2 * nr] + t[:, 2 * nr:]).reshape(N_DEV, -1)
    off = 0
    for k, axis in SMALL_SHARDED.items():
        size = math.prod(w[k].shape)
        full[k] = _unshard(vals[:, off:off + size], w[k].shape, axis)
        off += size
    return full, wl, wtl


class Overlap:
    LATE = [(0, key) for key in LAYER_MATRICES if key != "w_in"] + [(1, key) for key in LAYER_MATRICES]
    EARLY_GRADS = [(0, key) for key in LAYER_MATRICES if key != "w_in"]

    def __init__(self, w):
        self.w = w
        self.sums, self.from_chips, self.parts = {}, {}, {}

    def attention_fwd_rider(self):
        shards = [_layer_shards(self.w, 0), _layer_shards(self.w, 1)]
        return gather_rider([shards[i][key].astype(MXU_DTYPE) for i, key in self.LATE])

    def weights_arrived(self, carried, wl, wtl):
        wl = [dict(wl[0]), {}]
        wtl = [dict(wtl[0]), {}]
        for (i, key), g8 in zip(self.LATE, carried):
            wl[i][key] = _unshard(g8, g8.shape[1:], _layer_entry(i, key)[2])
            wtl[i][key] = wl[i][key].T
        return wl, wtl

    def _to_sibling(self, ids, layer_grads):
        for i, key in ids:
            self.parts[i, key] = _to_shards(layer_grads[key], _layer_entry(i, key)[2])
        return sibling_rider([self.parts[e] for e in ids])

    def _add(self, ids, carried):
        core = jnp.reshape(lax.axis_index("c"), (1,)).astype(jnp.int32)
        for (i, key), got in zip(ids, carried):
            self.sums[i, key] = add_sibling_parts(self.parts[i, key], got, core, name=f"add_sibling_l{i}_{key}")

    def mlp_bwd_rider(self, layer1_grads):
        return self._to_sibling([(1, key) for key in LAYER_MATRICES], layer1_grads)

    def after_mlp_bwd(self, carried):
        self._add([(1, key) for key in LAYER_MATRICES], carried)

    def lru_bwd_rider(self, layer0_grads):
        return self._to_sibling(self.EARLY_GRADS, layer0_grads)

    def attention_bwd_rider(self, carried):
        self._add(self.EARLY_GRADS, carried)
        self.travelling = list(self.sums)
        return chips_rider([self.sums[e][1] for e in self.travelling])

    def after_attention_bwd(self, carried):
        for e, got in zip(self.travelling, carried):
            self.from_chips[e] = got


def _pack_replicated(tree):
    return _pack_rows(jnp.concatenate([tree[k].astype(f32).reshape(-1) for k in REPLICATED]), 8)


def kernel(x, p, ev_w_in, ev_lru_conv_w, ev_lru_conv_b, ev_lru_gate_a_w, ev_lru_gate_a_b, ev_lru_gate_x_w, ev_lru_gate_x_b, ev_lru_lambda, ev_w_out, od_w_in, od_ssm_conv_w, od_ssm_conv_b, od_ssm_dt_bias, od_ssm_a_log, od_ssm_d, od_ssm_norm, od_cm_conv_w, od_cm_conv_b, od_cm_ln_g, od_cm_ln_b, od_w_out, norm_mix_pre, norm_mix_post, norm_mlp_pre, norm_mlp_post, norm_ple, mlp_w1, mlp_w2, ple_w_proj, ple_w_gate, loss_target, m_ev_w_in, m_ev_lru_conv_w, m_ev_lru_conv_b, m_ev_lru_gate_a_w, m_ev_lru_gate_a_b, m_ev_lru_gate_x_w, m_ev_lru_gate_x_b, m_ev_lru_lambda, m_ev_w_out, m_od_w_in, m_od_ssm_conv_w, m_od_ssm_conv_b, m_od_ssm_dt_bias, m_od_ssm_a_log, m_od_ssm_d, m_od_ssm_norm, m_od_cm_conv_w, m_od_cm_conv_b, m_od_cm_ln_g, m_od_cm_ln_b, m_od_w_out, m_norm_mix_pre, m_norm_mix_post, m_norm_mlp_pre, m_norm_mlp_post, m_norm_ple, m_mlp_w1, m_mlp_w2, m_ple_w_proj, m_ple_w_gate, v_ev_w_in, v_ev_lru_conv_w, v_ev_lru_conv_b, v_ev_lru_gate_a_w, v_ev_lru_gate_a_b, v_ev_lru_gate_x_w, v_ev_lru_gate_x_b, v_ev_lru_lambda, v_ev_w_out, v_od_w_in, v_od_ssm_conv_w, v_od_ssm_conv_b, v_od_ssm_dt_bias, v_od_ssm_a_log, v_od_ssm_d, v_od_ssm_norm, v_od_cm_conv_w, v_od_cm_conv_b, v_od_cm_ln_g, v_od_cm_ln_b, v_od_w_out, v_norm_mix_pre, v_norm_mix_post, v_norm_mlp_pre, v_norm_mlp_post, v_norm_ple, v_mlp_w1, v_mlp_w2, v_ple_w_proj, v_ple_w_gate):
    ws = [ev_w_in, ev_lru_conv_w, ev_lru_conv_b, ev_lru_gate_a_w, ev_lru_gate_a_b, ev_lru_gate_x_w, ev_lru_gate_x_b, ev_lru_lambda, ev_w_out, od_w_in, od_ssm_conv_w, od_ssm_conv_b, od_ssm_dt_bias, od_ssm_a_log, od_ssm_d, od_ssm_norm, od_cm_conv_w, od_cm_conv_b, od_cm_ln_g, od_cm_ln_b, od_w_out, norm_mix_pre, norm_mix_post, norm_mlp_pre, norm_mlp_post, norm_ple, mlp_w1, mlp_w2, ple_w_proj, ple_w_gate]
    ms = [m_ev_w_in, m_ev_lru_conv_w, m_ev_lru_conv_b, m_ev_lru_gate_a_w, m_ev_lru_gate_a_b, m_ev_lru_gate_x_w, m_ev_lru_gate_x_b, m_ev_lru_lambda, m_ev_w_out, m_od_w_in, m_od_ssm_conv_w, m_od_ssm_conv_b, m_od_ssm_dt_bias, m_od_ssm_a_log, m_od_ssm_d, m_od_ssm_norm, m_od_cm_conv_w, m_od_cm_conv_b, m_od_cm_ln_g, m_od_cm_ln_b, m_od_w_out, m_norm_mix_pre, m_norm_mix_post, m_norm_mlp_pre, m_norm_mlp_post, m_norm_ple, m_mlp_w1, m_mlp_w2, m_ple_w_proj, m_ple_w_gate]
    vs = [v_ev_w_in, v_ev_lru_conv_w, v_ev_lru_conv_b, v_ev_lru_gate_a_w, v_ev_lru_gate_a_b, v_ev_lru_gate_x_w, v_ev_lru_gate_x_b, v_ev_lru_lambda, v_ev_w_out, v_od_w_in, v_od_ssm_conv_w, v_od_ssm_conv_b, v_od_ssm_dt_bias, v_od_ssm_a_log, v_od_ssm_d, v_od_ssm_norm, v_od_cm_conv_w, v_od_cm_conv_b, v_od_cm_ln_g, v_od_cm_ln_b, v_od_w_out, v_norm_mix_pre, v_norm_mix_post, v_norm_mlp_pre, v_norm_mlp_post, v_norm_ple, v_mlp_w1, v_mlp_w2, v_ple_w_proj, v_ple_w_gate]
    w = dict(zip(W_NAMES, ws))
    m = dict(zip(W_NAMES, ms))
    v = dict(zip(W_NAMES, vs))
    full, wl0, wtl0 = _gather_early(w)
    comm = Overlap(w)
    loss_local, grad_x, grads, layer_grads = local_step(x, p, loss_target, full, [wl0, None], [wtl0, None], comm)
    loss = lax.psum(loss_local, ("x", "y", "c"))
    return (loss, grad_x, *_reduce_and_update(grads, layer_grads, w, m, v, comm))


def _reduce_and_update(grads, layer_grads, w, m, v, comm):
    mx, my, mc = _mesh_pos()

    entries = [(i, key) for i in range(len(layer_grads)) for key in LAYER_MATRICES]
    left = [e for e in entries if e not in comm.from_chips]
    parts = [_to_shards(layer_grads[i][key], _layer_entry(i, key)[2]) for i, key in left]
    small = jnp.concatenate([_to_shards(grads[k], axis).reshape(N_DEV, -1) for k, axis in SMALL_SHARDED.items()],
                            axis=1)
    small_rows = _pack_small(w).shape[0]
    small = jnp.pad(small, ((0, 0), (0, small_rows * LANES - small.shape[1]))).reshape(N_DEV, small_rows, LANES)
    parts.append(small)
    from_sibling = scatter_to_sibling(parts, name="scatter_sibling")
    core = jnp.reshape(mc, (1,)).astype(jnp.int32)
    sums = [add_sibling_parts(a, b, core, name=f"add_sibling_{i}") for i, (a, b) in enumerate(zip(parts, from_sibling))]
    from_chips = scatter_to_chips([s[1] for s in sums], name="scatter_chips")
    all_sums = {**comm.sums, **dict(zip(left, sums[:-1]))}
    all_from_chips = {**comm.from_chips, **dict(zip(left, from_chips[:-1]))}
    chip = jnp.reshape(2 * mx + my, (1,)).astype(jnp.int32)
    per_layer = []
    for i in range(len(layer_grads)):
        ws, ms, vs = _layer_shards(w, i), _layer_shards(m, i), _layer_shards(v, i)
        per_layer.append({key: adamw_sharded(all_sums[i, key][0], all_from_chips[i, key], chip, ws[key], ms[key],
                                             vs[key], name=f"adamw_l{i}_{key}") for key in LAYER_MATRICES})
    g_sh, d_sh, m_sh, v_sh = {}, {}, {}, {}
    for which, tree in enumerate((g_sh, d_sh, m_sh, v_sh)):
        for i in range(len(per_layer)):
            for key in LAYER_MATRICES:
                name, idx, _ = _layer_entry(i, key)
                tree.setdefault(name, {})[idx] = per_layer[i][key][which]
        for name in BIG_SHARDED:
            tree[name] = jnp.stack([tree[name][idx] for idx in sorted(tree[name])], axis=0)
    outs = adamw_sharded(sums[-1][0], from_chips[-1], chip, _pack_small(w), _pack_small(m), _pack_small(v),
                         name="adamw_small")
    small_shapes = {k: w[k].shape for k in SMALL_SHARDED}
    for tree, o in zip((g_sh, d_sh, m_sh, v_sh), outs):
        tree.update(_unpack(o.reshape(-1), small_shapes))

    rep_parts, = all_gather([_pack_replicated(grads)], name="gather_replicated_grads")
    outs = adamw_replicated(rep_parts, _pack_replicated(w), _pack_replicated(m), _pack_replicated(v),
                            name="adamw_replicated")
    rep_shapes = {k: w[k].shape for k in REPLICATED}
    g_rp, d_rp, m_rp, v_rp = [_unpack(o.reshape(-1), rep_shapes) for o in outs]

    pick = lambda sh, rp: [sh[k] if k in SHARDED else rp[k] for k in W_NAMES]
    return [*pick(g_sh, g_rp), *pick(d_sh, d_rp), *pick(m_sh, m_rp), *pick(v_sh, v_rp)]
```

```python
import math

import jax
import jax.numpy as jnp
from jax import lax
from jax.experimental import pallas as pl
from jax.experimental.pallas import tpu as pltpu

f32 = jnp.float32
MXU_DTYPE = jnp.bfloat16

EPS = 1e-6
LRU_WIDTH = 512
LRU_HEADS = 8
LRU_CONV = 4
LRU_C = 8.0
SB_WIDTH = 512
SB_HEAD_DIM = 64
SSM_WIDTH = 1024
SSM_HEADS = 16
SSM_HEAD_DIM = 64
SSM_GROUPS = 2
SSM_STATE = 128
SSM_CONV = 4
SSM_CHUNK = 128
SSM_XBC = SSM_WIDTH + 2 * SSM_GROUPS * SSM_STATE
CONF_WIDTH = 512
CONF_KERNEL = 31
LANES = 128
N_DEV = 8

ADAM_LR = 0.001
ADAM_B1 = 0.9
ADAM_B2 = 0.999
ADAM_EPS = 1e-08
ADAM_WD = 0.01
ADAM_STEP = 10

VMEM_LIMIT = 56 * 1024 * 1024


def _cparams(*sem):
    return pltpu.CompilerParams(dimension_semantics=sem, vmem_limit_bytes=VMEM_LIMIT)


def _mm(a, b):
    return jnp.dot(a.astype(MXU_DTYPE), b.astype(MXU_DTYPE), preferred_element_type=f32)


def _mm_nt(a, b):
    return lax.dot_general(a.astype(MXU_DTYPE), b.astype(MXU_DTYPE), (((1,), (1,)), ((), ())),
                           preferred_element_type=f32)


def _mm_tn(a, b):
    return lax.dot_general(a.astype(MXU_DTYPE), b.astype(MXU_DTYPE), (((0,), (0,)), ((), ())),
                           preferred_element_type=f32)


def _mm_exact(a, b):
    return jnp.dot(a, b, preferred_element_type=f32, precision=lax.Precision.HIGHEST)


@jax.custom_vjp
def dmm(a, b):
    return _mm(a, b)


def _dmm_fwd(a, b):
    return _mm(a, b), (a, b)


def _dmm_bwd(res, g):
    a, b = res
    return _mm_nt(g, b), _mm_tn(a, g)


dmm.defvjp(_dmm_fwd, _dmm_bwd)


@jax.custom_vjp
def dmm_nt(a, b):
    return _mm_nt(a, b)


def _dmm_nt_fwd(a, b):
    return _mm_nt(a, b), (a, b)


def _dmm_nt_bwd(res, g):
    a, b = res
    return _mm(g, b), _mm_tn(g, a)


dmm_nt.defvjp(_dmm_nt_fwd, _dmm_nt_bwd)


@jax.custom_vjp
def dmm_tn(a, b):
    return _mm_tn(a, b)


def _dmm_tn_fwd(a, b):
    return _mm_tn(a, b), (a, b)


def _dmm_tn_bwd(res, g):
    a, b = res
    return _mm_nt(b, g), _mm(a, g)


dmm_tn.defvjp(_dmm_tn_fwd, _dmm_tn_bwd)


def _rms(x, g):
    r = lax.rsqrt(jnp.mean(x * x, axis=-1, keepdims=True) + EPS)
    return x * r * g


def _rms_bwd(dy, x, g):
    r = lax.rsqrt(jnp.mean(x * x, axis=-1, keepdims=True) + EPS)
    dyg = dy * g
    dx = r * dyg - x * (r * r * r * jnp.mean(dyg * x, axis=-1, keepdims=True))
    return dx, dy * x * r


def _tok(tm, n):
    return pl.BlockSpec((tm, n), lambda i: (i, 0))


def _whole(shape):
    nd = len(shape)
    return pl.BlockSpec(tuple(shape), lambda i: (0,) * nd)


def _acc_rows(ref, val):
    s = jnp.sum(val, axis=0, keepdims=True)

    @pl.when(pl.program_id(0) == 0)
    def _():
        ref[...] = s

    @pl.when(pl.program_id(0) != 0)
    def _():
        ref[...] += s


TOKEN_TILE = 512
WEIGHT_GRAD_TOKENS = 1024


class Part:
    def __init__(self, whole, start, size, axis):
        self.whole, self.start, self.size, self.axis = whole, start, size, axis
        self.shape = tuple(size if a == axis else n for a, n in enumerate(whole.shape))


def _weights(ws):
    wholes, readers = [], []
    for w in ws:
        arr = w.whole if isinstance(w, Part) else w
        idx = next((i for i, a in enumerate(wholes) if a is arr), None)
        if idx is None:
            wholes.append(arr)
            idx = len(wholes) - 1
        if isinstance(w, Part):
            rows = pl.ds(w.start, w.size) if w.axis == 0 else slice(None)
            cols = pl.ds(w.start, w.size) if w.axis == 1 else slice(None)
            readers.append(lambda refs, idx=idx, rows=rows, cols=cols: refs[idx][rows, cols])
        else:
            readers.append(lambda refs, idx=idx: refs[idx][...])
    return wholes, readers


def norm_matmul(h, g, ws, out_dtypes, name):
    t, d = h.shape
    tm = TOKEN_TILE
    wholes, readers = _weights(ws)
    nw = len(wholes)

    def body(h_ref, g_ref, *refs):
        hn = _rms(h_ref[...], g_ref[...]).astype(MXU_DTYPE)
        for read, o_ref in zip(readers, refs[nw:]):
            o_ref[...] = jnp.dot(hn, read(refs[:nw]), preferred_element_type=f32).astype(o_ref.dtype)

    return pl.pallas_call(
        body, name=name, grid=(t // tm,),
        in_specs=[_tok(tm, d), _whole(g.shape)] + [_whole(w.shape) for w in wholes],
        out_specs=[_tok(tm, w.shape[1]) for w in ws],
        out_shape=[jax.ShapeDtypeStruct((t, w.shape[1]), dt) for w, dt in zip(ws, out_dtypes)],
        compiler_params=_cparams("parallel"),
    )(h, g, *wholes)


def matmul_residual_norm(xs, ws, h, g, name, relu2=False):
    t, d = h.shape
    tm = TOKEN_TILE
    nx = len(xs)
    wholes, readers = _weights(ws)
    nw = len(wholes)

    def body(*refs):
        x_refs, w_refs = refs[:nx], refs[nx:nx + nw]
        h_ref, g_ref, ho_ref, m_ref = refs[nx + nw:]
        m = None
        for x_ref, read in zip(x_refs, readers):
            x = x_ref[...]
            if relu2:
                x = jnp.square(jnp.maximum(x.astype(f32), 0.0))
            part = jnp.dot(x.astype(MXU_DTYPE), read(w_refs), preferred_element_type=f32)
            m = part if m is None else m + part
        m_ref[...] = m.astype(m_ref.dtype)
        ho_ref[...] = h_ref[...] + _rms(m, g_ref[...])

    return pl.pallas_call(
        body, name=name, grid=(t // tm,),
        in_specs=[_tok(tm, x.shape[1]) for x in xs] + [_whole(w.shape) for w in wholes]
        + [_tok(tm, d), _whole(g.shape)],
        out_specs=[_tok(tm, d), _tok(tm, d)],
        out_shape=[jax.ShapeDtypeStruct((t, d), f32), jax.ShapeDtypeStruct((t, d), MXU_DTYPE)],
        compiler_params=_cparams("parallel"),
    )(*xs, *wholes, h, g)


def ple_forward(h, p, w_gate, w_proj, g, name, target=None):
    t, d = h.shape
    tm = TOKEN_TILE
    last = target is not None

    def body(h_ref, p_ref, wg_ref, wp_ref, g_ref, *refs):
        hh = h_ref[...]
        gl = jnp.dot(hh.astype(MXU_DTYPE), wg_ref[...], preferred_element_type=f32)
        emb = jnp.dot(p_ref[...].astype(MXU_DTYPE), wp_ref[...], preferred_element_type=f32)
        y = hh + _rms(jax.nn.sigmoid(gl) * emb, g_ref[...])
        if last:
            t_ref, l_ref, dy_ref, gl_ref, emb_ref = refs
            e = y - t_ref[...]
            dy_ref[...] = e * (1.0 / d)
            part = jnp.sum(jnp.sum(e * e, axis=1, keepdims=True), axis=0, keepdims=True) * (0.5 / d)
            _acc_rows(l_ref, jnp.broadcast_to(part, (1, LANES)))
        else:
            y_ref, gl_ref, emb_ref = refs
            y_ref[...] = y
        gl_ref[...] = gl.astype(gl_ref.dtype)
        emb_ref[...] = emb.astype(emb_ref.dtype)

    saved = [jax.ShapeDtypeStruct((t, d), MXU_DTYPE)] * 2
    in_specs = [_tok(tm, d), _tok(tm, p.shape[1]), _whole(w_gate.shape), _whole(w_proj.shape), _whole(g.shape)]
    if last:
        return pl.pallas_call(
            body, name=name, grid=(t // tm,),
            in_specs=in_specs + [_tok(tm, d)],
            out_specs=[_whole((1, LANES))] + [_tok(tm, d)] * 3,
            out_shape=[jax.ShapeDtypeStruct((1, LANES), f32), jax.ShapeDtypeStruct((t, d), f32)] + saved,
            compiler_params=_cparams("arbitrary"),
        )(h, p, w_gate, w_proj, g, target)
    return pl.pallas_call(
        body, name=name, grid=(t // tm,),
        in_specs=in_specs,
        out_specs=[_tok(tm, d)] * 3,
        out_shape=[jax.ShapeDtypeStruct((t, d), f32)] + saved,
        compiler_params=_cparams("parallel"),
    )(h, p, w_gate, w_proj, g)


def bwd_through_norm_in(dh, gs, wts, h, g, name):
    t, d = h.shape
    tm = TOKEN_TILE
    ng = len(gs)
    wholes, readers = _weights(wts)
    nw = len(wholes)

    def body(*refs):
        dh_ref = refs[0]
        g_refs, w_refs = refs[1:1 + ng], refs[1 + ng:1 + ng + nw]
        h_ref, gain_ref, dho_ref, dg_ref = refs[1 + ng + nw:]
        dhn = None
        for g_ref, read in zip(g_refs, readers):
            part = jnp.dot(g_ref[...].astype(MXU_DTYPE), read(w_refs), preferred_element_type=f32)
            dhn = part if dhn is None else dhn + part
        dx, dgr = _rms_bwd(dhn, h_ref[...], gain_ref[...])
        dho_ref[...] = dh_ref[...] + dx
        _acc_rows(dg_ref, dgr)

    return pl.pallas_call(
        body, name=name, grid=(t // tm,),
        in_specs=[_tok(tm, d)] + [_tok(tm, x.shape[1]) for x in gs] + [_whole(w.shape) for w in wholes]
        + [_tok(tm, d), _whole(g.shape)],
        out_specs=[_tok(tm, d), _whole((1, d))],
        out_shape=[jax.ShapeDtypeStruct((t, d), f32), jax.ShapeDtypeStruct((1, d), f32)],
        compiler_params=_cparams("arbitrary"),
    )(dh, *gs, *wholes, h, g)


def bwd_through_norm_out(dh, n, g, wts, out_dtypes, name, relu2_of=None, rider=None):
    t, d = n.shape
    tm = TOKEN_TILE
    nw = len(wts)
    wholes, readers = _weights(wts)
    nwh = len(wholes)
    has_a = relu2_of is not None

    def body(*refs):
        dh_ref, n_ref, gain_ref = refs[:3]
        w_refs = refs[3:3 + nwh]
        rest = refs[3 + nwh:]
        if has_a:
            a_ref, rest = rest[0], rest[1:]
        dn_ref, dx_refs, dg_ref = rest[0], rest[1:1 + nw], rest[1 + nw]
        dn, dgr = _rms_bwd(dh_ref[...], n_ref[...].astype(f32), gain_ref[...])
        dnb = dn.astype(MXU_DTYPE)
        dn_ref[...] = dnb.astype(dn_ref.dtype)
        for read, dx_ref in zip(readers, dx_refs):
            dx = jnp.dot(dnb, read(w_refs), preferred_element_type=f32)
            if has_a:
                dx = dx * (2.0 * jnp.maximum(a_ref[...].astype(f32), 0.0))
            dx_ref[...] = dx.astype(dx_ref.dtype)
        _acc_rows(dg_ref, dgr)

    ins = [dh, n, g, *wholes] + ([relu2_of] if has_a else [])
    in_specs = [_tok(tm, d), _tok(tm, d), _whole(g.shape)] + [_whole(w.shape) for w in wholes]
    if has_a:
        in_specs.append(_tok(tm, relu2_of.shape[1]))
    grid = (t // tm,)
    body, in_specs, out_specs, out_shape, scratch = _ride(
        rider, body, in_specs, [_tok(tm, d)] + [_tok(tm, w.shape[1]) for w in wts] + [_whole((1, d))],
        [jax.ShapeDtypeStruct((t, d), MXU_DTYPE)]
        + [jax.ShapeDtypeStruct((t, w.shape[1]), dt) for w, dt in zip(wts, out_dtypes)]
        + [jax.ShapeDtypeStruct((1, d), f32)], [], grid)
    outs = pl.pallas_call(
        body, name=name, grid=grid, in_specs=in_specs, out_specs=out_specs, out_shape=out_shape,
        scratch_shapes=scratch, compiler_params=_cparams("arbitrary"),
    )(*ins, *(rider.inputs if rider else []))
    if rider:
        return outs[0], list(outs[1:1 + nw]), outs[1 + nw], list(outs[2 + nw:])
    return outs[0], list(outs[1:1 + nw]), outs[1 + nw]


def ple_backward(dh3, h2, gl, emb, g, w_gate_t, name):
    t, d = h2.shape
    tm = TOKEN_TILE

    def body(dh_ref, gl_ref, emb_ref, gain_ref, wt_ref, dho_ref, dgl_ref, demb_ref, dg_ref):
        gate = jax.nn.sigmoid(gl_ref[...].astype(f32))
        emb = emb_ref[...].astype(f32)
        dge, dgr = _rms_bwd(dh_ref[...], gate * emb, gain_ref[...])
        demb_ref[...] = (dge * gate).astype(demb_ref.dtype)
        dgl = (dge * emb * gate * (1.0 - gate)).astype(MXU_DTYPE)
        dgl_ref[...] = dgl.astype(dgl_ref.dtype)
        dho_ref[...] = dh_ref[...] + jnp.dot(dgl, wt_ref[...], preferred_element_type=f32)
        _acc_rows(dg_ref, dgr)

    return pl.pallas_call(
        body, name=name, grid=(t // tm,),
        in_specs=[_tok(tm, d), _tok(tm, d), _tok(tm, d), _whole(g.shape), _whole(w_gate_t.shape)],
        out_specs=[_tok(tm, d), _tok(tm, d), _tok(tm, d), _whole((1, d))],
        out_shape=[jax.ShapeDtypeStruct((t, d), f32), jax.ShapeDtypeStruct((t, d), MXU_DTYPE),
                   jax.ShapeDtypeStruct((t, d), MXU_DTYPE), jax.ShapeDtypeStruct((1, d), f32)],
        compiler_params=_cparams("arbitrary"),
    )(dh3, gl, emb, g, w_gate_t)


def _largest_tile(n, cap):
    if n <= cap:
        return n
    return max(c for c in range(LANES, cap + 1, LANES) if n % c == 0)


def weight_grad(x, gout, name, prologue="none", gain=None):
    t, k = x.shape
    n = gout.shape[1]
    tt = WEIGHT_GRAD_TOKENS
    tn = _largest_tile(n, 1024)
    tk = k if prologue == "rms" else _largest_tile(k, 1024)
    has_gain = prologue == "rms"

    def body(*refs):
        if has_gain:
            x_ref, gain_ref, g_ref, o_ref = refs
        else:
            x_ref, g_ref, o_ref = refs
        x = x_ref[...].astype(f32)
        if prologue == "relu2":
            x = jnp.square(jnp.maximum(x, 0.0))
        elif prologue == "rms":
            x = _rms(x, gain_ref[...])
        part = _mm_tn(x, g_ref[...])

        @pl.when(pl.program_id(2) == 0)
        def _():
            o_ref[...] = part

        @pl.when(pl.program_id(2) != 0)
        def _():
            o_ref[...] += part

    in_specs = [pl.BlockSpec((tt, tk), lambda i, j, s: (s, i))]
    ins = [x]
    if has_gain:
        in_specs.append(pl.BlockSpec(gain.shape, lambda i, j, s: (0, 0)))
        ins.append(gain)
    in_specs.append(pl.BlockSpec((tt, tn), lambda i, j, s: (s, j)))
    ins.append(gout)
    return pl.pallas_call(
        body, name=name, grid=(k // tk, n // tn, t // tt),
        in_specs=in_specs,
        out_specs=pl.BlockSpec((tk, tn), lambda i, j, s: (i, j)),
        out_shape=jax.ShapeDtypeStruct((k, n), f32),
        compiler_params=_cparams("parallel", "parallel", "arbitrary"),
    )(*ins)


def weight_grads_of_norm(x, gain, gouts, name):
    t, k = x.shape
    tt = TOKEN_TILE
    ng = len(gouts)

    def body(x_ref, gain_ref, *refs):
        xn = _rms(x_ref[...], gain_ref[...]).astype(MXU_DTYPE)
        first = pl.program_id(0) == 0
        for g_ref, o_ref in zip(refs[:ng], refs[ng:]):
            _accum(o_ref, _mm_tn(xn, g_ref[...]), first)

    return pl.pallas_call(
        body, name=name, grid=(t // tt,),
        in_specs=[_tok(tt, k), _whole(gain.shape)] + [_tok(tt, g.shape[1]) for g in gouts],
        out_specs=[_whole((k, g.shape[1])) for g in gouts],
        out_shape=[jax.ShapeDtypeStruct((k, g.shape[1]), f32) for g in gouts],
        compiler_params=_cparams("arbitrary"),
    )(x, gain, *gouts)


SEQ_TILE = 256
HALO = 8


def _first_step():
    return jnp.logical_and(pl.program_id(0) == 0, pl.program_id(1) == 0)


def _accum(ref, val, first):
    @pl.when(first)
    def _():
        ref[...] = val

    @pl.when(jnp.logical_not(first))
    def _():
        ref[...] += val


def _softplus(x):
    return jnp.maximum(x, 0.0) + jnp.log1p(jnp.exp(-jnp.abs(x)))


def _neg_expm1(z):
    series = -z * (1.0 + z * (0.5 + z * (1.0 / 6.0 + z * (1.0 / 24.0 + z * (1.0 / 120.0)))))
    return jnp.where(z > -0.05, series, 1.0 - jnp.exp(z))


def _lru_gates(xc, ga, gab, gx, gxb, lam):
    r = jax.nn.sigmoid(dmm(xc, ga) + gab)
    i = jax.nn.sigmoid(dmm(xc, gx) + gxb)
    log_a = -LRU_C * r * _softplus(-lam)
    a = jnp.exp(log_a)
    u = jnp.sqrt(_neg_expm1(2.0 * log_a)) * (i * xc)
    return a, u


def _scan_down(a, u):
    n = a.shape[0]
    rows = lax.broadcasted_iota(jnp.int32, a.shape, 0)
    d = 1
    while d < n:
        keep = rows >= d
        a_s = jnp.where(keep, pltpu.roll(a, d, 0), 1.0)
        u_s = jnp.where(keep, pltpu.roll(u, d, 0), 0.0)
        u = a * u_s + u
        a = a * a_s
        d *= 2
    return a, u


def _scan_up(b, g):
    n = b.shape[0]
    rows = lax.broadcasted_iota(jnp.int32, b.shape, 0)
    d = 1
    while d < n:
        keep = rows < n - d
        b_s = jnp.where(keep, pltpu.roll(b, n - d, 0), 1.0)
        g_s = jnp.where(keep, pltpu.roll(g, n - d, 0), 0.0)
        g = g + b * g_s
        b = b * b_s
        d *= 2
    return g


def _seq_specs(ts, c, nt, reverse=False):
    per = ts // HALO

    def jj(j):
        return (nt - 1 - j) if reverse else j

    tile = pl.BlockSpec((1, ts, c), lambda b, j: (b, jj(j), 0))
    before = pl.BlockSpec((1, HALO, c), lambda b, j: (b, jnp.maximum(jj(j) * per - 1, 0), 0))
    after = pl.BlockSpec((1, HALO, c), lambda b, j: (b, jnp.minimum((jj(j) + 1) * per, nt * per - 1), 0))
    return tile, before, after


def _const2(shape):
    nd = len(shape)
    return pl.BlockSpec(tuple(shape), lambda b, j: (0,) * nd)


def lru_forward(xpre, gate, cw, cb, ga, gab, gx, gxb, lam, name):
    nb, ns, w = xpre.shape
    ts = SEQ_TILE
    nt = ns // ts
    tile, _, _ = _seq_specs(ts, w, nt)

    def body(xp_ref, gt_ref, cw_ref, cb_ref, ga_ref, gab_ref, gx_ref, gxb_ref, lam_ref,
             y_ref, xc_ref, hs_ref, xin, hcar):
        @pl.when(pl.program_id(1) == 0)
        def _():
            xin[0:HALO, :] = jnp.zeros((HALO, w), f32)
            hcar[...] = jnp.zeros_like(hcar)

        xin[HALO:HALO + ts, :] = xp_ref[0]
        xc = jnp.broadcast_to(cb_ref[...], (ts, w))
        for k in range(LRU_CONV):
            xc = xc + cw_ref[k:k + 1, :] * xin[pl.ds(HALO - LRU_CONV + 1 + k, ts), :]
        xin[0:HALO, :] = xin[ts:ts + HALO, :]
        a, u = _lru_gates(xc, ga_ref[...], gab_ref[...], gx_ref[...], gxb_ref[...], lam_ref[...])
        acum, h = _scan_down(a, u)
        h = h + acum * hcar[0:1, :]
        hcar[0:1, :] = h[ts - 1:ts, :]
        xc_ref[0] = xc
        hs_ref[0] = h
        y_ref[0] = (h * jax.nn.gelu(gt_ref[0])).astype(y_ref.dtype)

    params = [cw, cb, ga, gab, gx, gxb, lam]
    return pl.pallas_call(
        body, name=name, grid=(nb, nt),
        in_specs=[tile, tile] + [_const2(p.shape) for p in params],
        out_specs=[tile, tile, tile],
        out_shape=[jax.ShapeDtypeStruct((nb, ns, w), MXU_DTYPE), jax.ShapeDtypeStruct((nb, ns, w), f32),
                   jax.ShapeDtypeStruct((nb, ns, w), f32)],
        scratch_shapes=[pltpu.VMEM((ts + HALO, w), f32), pltpu.VMEM((HALO, w), f32)],
        compiler_params=_cparams("arbitrary", "arbitrary"),
    )(xpre, gate, *params)


def lru_backward(dy, xpre, gate, xc, hs, cw, cb, ga, gab, gx, gxb, lam, name, rider=None):
    nb, ns, w = xpre.shape
    ts = SEQ_TILE
    nt = ns // ts
    tile, before, _ = _seq_specs(ts, w, nt, reverse=True)

    def body(dy_ref, xp_ref, xpb_ref, gt_ref, xc_ref, hs_ref, hsb_ref,
             cw_ref, cb_ref, ga_ref, gab_ref, gx_ref, gxb_ref, lam_ref,
             dxp_ref, dgt_ref, dcw_ref, dcb_ref, dga_ref, dgab_ref, dgx_ref, dgxb_ref, dlam_ref,
             dxc_ext, gcar, xin):
        j = pl.program_id(1)
        first = _first_step()
        at_seq_start = j == nt - 1

        @pl.when(j == 0)
        def _():
            dxc_ext[ts:ts + HALO, :] = jnp.zeros((HALO, w), f32)
            gcar[...] = jnp.zeros_like(gcar)

        gt = gt_ref[0]
        h = hs_ref[0]
        dyv = dy_ref[0].astype(f32)
        gl, gelu_vjp = jax.vjp(jax.nn.gelu, gt)
        dgt_ref[0] = gelu_vjp(dyv * h)[0].astype(dgt_ref.dtype)
        dh = dyv * gl

        (a, _), gates_vjp = jax.vjp(_lru_gates, xc_ref[0], ga_ref[...], gab_ref[...], gx_ref[...], gxb_ref[...],
                                    lam_ref[...])
        rows = lax.broadcasted_iota(jnp.int32, (ts, w), 0)
        dh = dh + jnp.where(rows == ts - 1, gcar[0:1, :], 0.0)
        b = pltpu.roll(a, ts - 1, 0)
        g = _scan_up(b, dh)
        gcar[0:1, :] = a[0:1, :] * g[0:1, :]
        hprev_row = jnp.where(at_seq_start, 0.0, hsb_ref[0][HALO - 1:HALO, :])
        hprev = jnp.where(rows == 0, hprev_row, pltpu.roll(h, 1, 0))
        dxc, dga, dgab, dgx, dgxb, dlam = gates_vjp((g * hprev, g))

        _accum(dga_ref, dga, first)
        _accum(dgx_ref, dgx, first)
        _accum(dgab_ref, dgab, first)
        _accum(dgxb_ref, dgxb, first)
        _accum(dlam_ref, dlam, first)
        _accum(dcb_ref, jnp.sum(dxc, axis=0, keepdims=True), first)

        dxc_ext[0:ts, :] = dxc
        dxp = jnp.zeros((ts, w), f32)
        for k in range(LRU_CONV):
            dxp = dxp + cw_ref[k:k + 1, :] * dxc_ext[pl.ds(LRU_CONV - 1 - k, ts), :]
        dxp_ref[0] = dxp.astype(dxp_ref.dtype)
        dxc_ext[ts:ts + HALO, :] = dxc[0:HALO, :]

        xin[0:HALO, :] = jnp.where(at_seq_start, 0.0, xpb_ref[0])
        xin[HALO:HALO + ts, :] = xp_ref[0]
        dcw_rows = [jnp.sum(dxc * xin[pl.ds(HALO - LRU_CONV + 1 + k, ts), :], axis=0, keepdims=True)
                    for k in range(LRU_CONV)]
        dcw_rows += [jnp.zeros((1, w), f32)] * (HALO - LRU_CONV)
        _accum(dcw_ref, jnp.concatenate(dcw_rows, axis=0), first)

    params = [cw, cb, ga, gab, gx, gxb, lam]
    pshape = lambda p: jax.ShapeDtypeStruct(p.shape, f32)
    grid = (nb, nt)
    n_main = 3 + len(params) - 1
    body, in_specs, out_specs, out_shape, scratch = _ride(
        rider, body, [tile, tile, before, tile, tile, tile, before] + [_const2(p.shape) for p in params],
        [tile, tile, _const2((HALO, w))] + [_const2(p.shape) for p in params[1:]],
        [jax.ShapeDtypeStruct((nb, ns, w), MXU_DTYPE), jax.ShapeDtypeStruct((nb, ns, w), MXU_DTYPE),
         jax.ShapeDtypeStruct((HALO, w), f32)] + [pshape(p) for p in params[1:]],
        [pltpu.VMEM((ts + HALO, w), f32), pltpu.VMEM((HALO, w), f32), pltpu.VMEM((ts + HALO, w), f32)], grid)
    outs = pl.pallas_call(
        body, name=name, grid=grid, in_specs=in_specs, out_specs=out_specs, out_shape=out_shape,
        scratch_shapes=scratch, compiler_params=_cparams("arbitrary", "arbitrary"),
    )(dy, xpre, xpre, gate, xc, hs, hs, *params, *(rider.inputs if rider else []))
    return (list(outs[:n_main]), list(outs[n_main:])) if rider else outs


SB_TILE = 256


def _split_dot(x, m):
    hi = x.astype(MXU_DTYPE)
    lo = (x - hi.astype(f32)).astype(MXU_DTYPE)
    return jnp.dot(hi, m, preferred_element_type=f32) + jnp.dot(lo, m, preferred_element_type=f32)


def _suffix_matrices(n):
    r = lax.broadcasted_iota(jnp.int32, (n, n), 0)
    c = lax.broadcasted_iota(jnp.int32, (n, n), 1)
    return (r > c).astype(MXU_DTYPE), (r >= c).astype(MXU_DTYPE)


LOG2E = 1.4426950408889634


def _sb_logits(qh, kb, strict):
    z = _mm_nt(qh, kb)
    ls = jnp.minimum(z, 0.0) - jnp.log2(1.0 + jnp.exp2(-jnp.abs(z)))
    lk = ls - z
    if strict is not None:
        lk = jnp.where(strict, lk, 0.0)
    return ls, lk


def _head_masked(x, dtype):
    lane = lax.broadcasted_iota(jnp.int32, x.shape, 1)
    return (jnp.where(lane < SB_HEAD_DIM, x, 0.0).astype(dtype), jnp.where(lane >= SB_HEAD_DIM, x, 0.0).astype(dtype))


def _stack_heads(dst, x, tq):
    x0, x1 = _head_masked(x, dst.dtype)
    for blk in range(dst.shape[0]):
        dst[blk, 0:tq, :] = x0[blk * tq:(blk + 1) * tq]
        dst[blk, tq:2 * tq, :] = x1[blk * tq:(blk + 1) * tq]


def _strict_mask(tq):
    rr = lax.broadcasted_iota(jnp.int32, (2 * tq, tq), 0)
    cc = lax.broadcasted_iota(jnp.int32, (2 * tq, tq), 1)
    return cc < jnp.where(rr >= tq, rr - tq, rr)


def _sb_specs(ns):
    npair = SB_WIDTH // LANES
    q = pl.BlockSpec((1, ns, LANES), lambda b, p: (b, 0, p))
    k = pl.BlockSpec((1, ns, LANES), lambda b, p: (b, 0, npair + p))
    v = pl.BlockSpec((1, ns, LANES), lambda b, p: (b, 0, 2 * npair + p))
    return q, k, v, npair


def sb_forward(qkv, name, rider=None):
    nb, ns, _ = qkv.shape
    tq = SB_TILE
    nq = ns // tq
    qspec, kspec, vspec, npair = _sb_specs(ns)

    def body(q_ref, k_ref, v_ref, o_ref, qs, ks, vs, acc):
        scale = 1.0 / math.sqrt(SB_HEAD_DIM)
        _stack_heads(qs, q_ref[0] * (scale * LOG2E), tq)
        ks[...] = k_ref[0].astype(MXU_DTYPE)
        _stack_heads(vs, v_ref[0], tq)
        mx, _ = _suffix_matrices(tq)
        strict = _strict_mask(tq)

        def step(q2, blks, r2, masked):
            kbs = [ks[pl.ds(pl.multiple_of(b * tq, tq), tq), :] for b in blks]
            lg = [_sb_logits(q2, kb, strict if masked else None) for kb in kbs]
            sums = [jnp.dot(lk.astype(MXU_DTYPE), mx, preferred_element_type=f32) for _, lk in lg]
            total = None
            for (ls, lk), s, b in zip(lg, sums, blks):
                a = r2 + s
                w = jnp.exp2(ls + a)
                if masked:
                    w = jnp.where(strict, w, 0.0)
                wb = w.astype(MXU_DTYPE)
                part = (jnp.dot(wb[:tq], vs[b, 0:tq, :], preferred_element_type=f32)
                        + jnp.dot(wb[tq:], vs[b, tq:2 * tq, :], preferred_element_type=f32))
                total = part if total is None else total + part
                r2 = a[:, 0:1] + lk[:, 0:1]
            acc[...] += total
            return r2

        def q_block(qi, carry):
            acc[...] = jnp.zeros_like(acc)
            q2 = qs[qi]
            r2 = step(q2, [qi], jnp.zeros((2 * tq, 1), f32), True)
            r2 = lax.fori_loop(0, lax.shift_right_logical(qi, 1),
                               lambda i, r: step(q2, [qi - 1 - 2 * i, qi - 2 - 2 * i], r, False), r2)
            lax.cond(jnp.bitwise_and(qi, 1) == 1, lambda r: step(q2, [0], r, False), lambda r: r, r2)
            o_ref[0, pl.ds(pl.multiple_of(qi * tq, tq), tq), :] = acc[...]
            return carry

        lax.fori_loop(0, nq, q_block, 0)

    grid = (nb, npair)
    body, in_specs, out_specs, out_shape, scratch = _ride(
        rider, body, [qspec, kspec, vspec], [pl.BlockSpec((1, ns, LANES), lambda b, p: (b, 0, p))],
        [jax.ShapeDtypeStruct((nb, ns, SB_WIDTH), f32)],
        [pltpu.VMEM((nq, 2 * tq, LANES), MXU_DTYPE), pltpu.VMEM((ns, LANES), MXU_DTYPE),
         pltpu.VMEM((nq, 2 * tq, LANES), MXU_DTYPE), pltpu.VMEM((tq, LANES), f32)], grid)
    outs = pl.pallas_call(
        body, name=name, grid=grid, in_specs=in_specs, out_specs=out_specs, out_shape=out_shape,
        scratch_shapes=scratch,
        compiler_params=_cparams("arbitrary", "arbitrary") if rider else _cparams("parallel", "parallel"),
    )(qkv, qkv, qkv, *(rider.inputs if rider else []))
    return (outs[0], list(outs[1:])) if rider else outs[0]


def sb_backward(qkv, o, do, name, rider=None):
    nb, ns, _ = qkv.shape
    tq = SB_TILE
    nq = ns // tq
    qspec, kspec, vspec, npair = _sb_specs(ns)
    ospec = pl.BlockSpec((1, ns, LANES), lambda b, p: (b, 0, p))

    def body(q_ref, k_ref, v_ref, o_ref, do_ref, dq_ref, dk_ref, dv_ref, qs, ks, kcat, vs, dos, dqacc, dkacc, dvacc):
        scale = 1.0 / math.sqrt(SB_HEAD_DIM)
        _stack_heads(qs, q_ref[0] * (scale * LOG2E), tq)
        ks[...] = k_ref[0].astype(MXU_DTYPE)
        _stack_heads(kcat, k_ref[0], tq)
        vs[...] = v_ref[0].astype(MXU_DTYPE)
        _stack_heads(dos, do_ref[0].astype(f32), tq)
        dkacc[...] = jnp.zeros_like(dkacc)
        dvacc[...] = jnp.zeros_like(dvacc)
        mx, mi = _suffix_matrices(tq)
        strict = _strict_mask(tq)

        def step(q2, do2, q2t, do2t, dtot2, blks, carry, masked):
            r2, g2 = carry
            k0s = [pl.multiple_of(b * tq, tq) for b in blks]
            lg = [_sb_logits(q2, ks[pl.ds(k0, tq), :], strict if masked else None) for k0 in k0s]
            dws = [_mm_nt(do2, vs[pl.ds(k0, tq), :]) for k0 in k0s]
            sums = [jnp.dot(lk.astype(MXU_DTYPE), mx, preferred_element_type=f32) for _, lk in lg]
            wbs, es = [], []
            for (ls, lk), s in zip(lg, sums):
                a = r2 + s
                w = jnp.exp2(ls + a)
                if masked:
                    w = jnp.where(strict, w, 0.0)
                wbs.append(w.astype(MXU_DTYPE))
                r2 = a[:, 0:1] + lk[:, 0:1]
            es = [wb.astype(f32) * dw for wb, dw in zip(wbs, dws)]
            esums = [_split_dot(e, mi) for e in es]
            dq = None
            for (ls, _), e, esum, wb, b, k0 in zip(lg, es, esums, wbs, blks, k0s):
                esuf = g2 + esum
                beta = jnp.exp2(ls)
                dz = e - beta * (e + (dtot2 - esuf))
                if masked:
                    dz = jnp.where(strict, dz, 0.0)
                dzb = dz.astype(MXU_DTYPE)
                part = (jnp.dot(dzb[:tq], kcat[b, 0:tq, :], preferred_element_type=f32)
                        + jnp.dot(dzb[tq:], kcat[b, tq:2 * tq, :], preferred_element_type=f32))
                dq = part if dq is None else dq + part
                dkacc[:, pl.ds(k0, tq)] += jnp.dot(q2t, dzb, preferred_element_type=f32)
                dvacc[:, pl.ds(k0, tq)] += jnp.dot(do2t, wb, preferred_element_type=f32)
                g2 = esuf[:, 0:1]
            dqacc[...] += dq
            return r2, g2

        def q_block(qi, carry):
            dqacc[...] = jnp.zeros_like(dqacc)
            q2, do2 = qs[qi], dos[qi]
            q2t, do2t = q2.T, do2.T
            ov = o_ref[0, pl.ds(pl.multiple_of(qi * tq, tq), tq), :]
            dtot2 = jnp.sum(do2.astype(f32) * jnp.concatenate([ov, ov], axis=0), axis=1, keepdims=True)
            zero = jnp.zeros((2 * tq, 1), f32)
            args = (q2, do2, q2t, do2t, dtot2)
            c = step(*args, [qi], (zero, zero), True)
            c = lax.fori_loop(0, lax.shift_right_logical(qi, 1),
                              lambda i, c: step(*args, [qi - 1 - 2 * i, qi - 2 - 2 * i], c, False), c)
            lax.cond(jnp.bitwise_and(qi, 1) == 1, lambda c: step(*args, [0], c, False), lambda c: c, c)
            dq_ref[0, pl.ds(pl.multiple_of(qi * tq, tq), tq), :] = (dqacc[...] * scale).astype(dq_ref.dtype)
            return carry

        lax.fori_loop(0, nq, q_block, 0)
        dk_ref[0] = (dkacc[...].T * (1.0 / LOG2E)).astype(dk_ref.dtype)
        dv_ref[0] = dvacc[...].T.astype(dv_ref.dtype)

    dshape = jax.ShapeDtypeStruct((nb, ns, SB_WIDTH), MXU_DTYPE)
    stacked = pltpu.VMEM((nq, 2 * tq, LANES), MXU_DTYPE)
    flat = pltpu.VMEM((ns, LANES), MXU_DTYPE)
    grid = (nb, npair)
    body, in_specs, out_specs, out_shape, scratch = _ride(
        rider, body, [qspec, kspec, vspec, ospec, ospec], [ospec, ospec, ospec], [dshape, dshape, dshape],
        [stacked, flat, stacked, flat, stacked,
         pltpu.VMEM((tq, LANES), f32), pltpu.VMEM((LANES, ns), f32), pltpu.VMEM((LANES, ns), f32)], grid)
    outs = pl.pallas_call(
        body, name=name, grid=grid, in_specs=in_specs, out_specs=out_specs, out_shape=out_shape,
        scratch_shapes=scratch,
        compiler_params=_cparams("arbitrary", "arbitrary") if rider else _cparams("parallel", "parallel"),
    )(qkv, qkv, qkv, o, do, *(rider.inputs if rider else []))
    return (list(outs[:3]), list(outs[3:])) if rider else list(outs)


SSM_PAIRS = SSM_HEADS // 2
PAIRS_PER_GROUP = SSM_PAIRS // SSM_GROUPS
GROUP_WIDTH = SSM_WIDTH // SSM_GROUPS


def _silu(x):
    return x * jax.nn.sigmoid(x)


def _ssd_chunk(xs_pre, b_pre, c_pre, dt_raw, dt_raw_t, z, st, dt_bias_r, dt_bias_c, a_log_r, a_log_c, d_skip,
               gains):
    n = dt_raw.shape[0]
    rows = lax.broadcasted_iota(jnp.int32, (n, n), 0)
    cols = lax.broadcasted_iota(jnp.int32, (n, n), 1)
    tril = cols <= rows
    tri_l = tril.astype(f32)
    tri_u = (rows <= cols).astype(f32)
    lane = lax.broadcasted_iota(jnp.int32, (n, LANES), 1)
    sub = lax.broadcasted_iota(jnp.int32, (LANES, n), 0)

    dt = _softplus(dt_raw + dt_bias_r)
    a_r = -jnp.exp(a_log_r)
    da = dt * a_r
    acs = _mm_exact(tri_l, da)
    dt_t = _softplus(dt_raw_t + dt_bias_c)
    acs_t = _mm_exact(dt_t * (-jnp.exp(a_log_c)), tri_u)

    bs = [_silu(b) for b in b_pre]
    cs = [_silu(c) for c in c_pre]
    cb = [dmm_nt(cs[g], bs[g]) for g in range(SSM_GROUPS)]

    end = jnp.sum(da, axis=0, keepdims=True)
    lane_row = lax.broadcasted_iota(jnp.int32, (1, LANES), 1)
    first_head = lane < SSM_HEAD_DIM
    first_head_row = lane_row < SSM_HEAD_DIM

    def head_col(v, h):
        return jnp.sum(jnp.where((lane if v.shape[0] == n else lane_row) == h, v, 0.0), axis=1, keepdims=True)

    ys, st_new = [], []
    for p in range(SSM_PAIRS):
        g = p // PAIRS_PER_GROUP
        h0, h1 = 2 * p, 2 * p + 1
        xs = _silu(xs_pre[p])
        acols = [head_col(acs, h0), head_col(acs, h1)]
        dt_p = jnp.where(first_head, head_col(dt, h0), head_col(dt, h1))
        acs_p = jnp.where(first_head, acols[0], acols[1])
        end_p = jnp.where(first_head_row, head_col(end, h0), head_col(end, h1))
        dsk_p = jnp.where(first_head_row, head_col(d_skip, h0), head_col(d_skip, h1))
        xdt = xs * dt_p
        y = jnp.exp(acs_p) * dmm(cs[g], st[p])
        for hh in range(2):
            row = jnp.sum(jnp.where(sub == 2 * p + hh, acs_t, 0.0), axis=0, keepdims=True)
            decay = jnp.where(tril, jnp.exp(jnp.where(tril, acols[hh] - row, 0.0)), 0.0)
            head = first_head if hh == 0 else jnp.logical_not(first_head)
            y = y + dmm(cb[g] * decay, jnp.where(head, xdt, 0.0))
        st_new.append(jnp.exp(end_p) * st[p] + dmm_tn(bs[g], xdt * jnp.exp(end_p - acs_p)))
        ys.append(y + dsk_p * xs)
    out = []
    for g in range(SSM_GROUPS):
        yg = jnp.concatenate(ys[g * PAIRS_PER_GROUP:(g + 1) * PAIRS_PER_GROUP], axis=1) * _silu(z[g])
        out.append(_rms(yg, gains[g]))
    return out, st_new


def _ssd_chunk_inputs(xconv, dtr, z, st_ref, gain):
    xs_pre = [xconv[:, LANES * p:LANES * (p + 1)] for p in range(SSM_PAIRS)]
    b0 = SSM_WIDTH
    c0 = SSM_WIDTH + SSM_GROUPS * SSM_STATE
    b_pre = [xconv[:, b0 + SSM_STATE * g:b0 + SSM_STATE * (g + 1)] for g in range(SSM_GROUPS)]
    c_pre = [xconv[:, c0 + SSM_STATE * g:c0 + SSM_STATE * (g + 1)] for g in range(SSM_GROUPS)]
    zs = [z[:, GROUP_WIDTH * g:GROUP_WIDTH * (g + 1)] for g in range(SSM_GROUPS)]
    sts = [st_ref[p] for p in range(SSM_PAIRS)]
    gains = [gain[:, GROUP_WIDTH * g:GROUP_WIDTH * (g + 1)] for g in range(SSM_GROUPS)]
    return xs_pre, b_pre, c_pre, dtr, dtr.T, zs, sts, gains


def ssd_forward(xbc, dt_raw, z, cw, cb, dbr, dbc, alr, alc, dsk, gain, name):
    nb, ns, wx = xbc.shape
    ln = SSM_CHUNK
    nt = ns // ln
    tile = lambda c: pl.BlockSpec((1, ln, c), lambda b, j: (b, j, 0))
    st_spec = pl.BlockSpec((1, 1, SSM_PAIRS, SSM_STATE, LANES), lambda b, j: (b, j, 0, 0, 0))

    def body(xbc_ref, dt_ref, z_ref, cw_ref, cb_ref, dbr_ref, dbc_ref, alr_ref, alc_ref, dsk_ref, gain_ref,
             y_ref, xconv_ref, stp_ref, xin, st):
        @pl.when(pl.program_id(1) == 0)
        def _():
            xin[0:HALO, :] = jnp.zeros((HALO, wx), f32)
            st[...] = jnp.zeros_like(st)

        xin[HALO:HALO + ln, :] = xbc_ref[0]
        xconv = jnp.broadcast_to(cb_ref[...], (ln, wx))
        for k in range(SSM_CONV):
            xconv = xconv + cw_ref[k:k + 1, :] * xin[pl.ds(HALO - SSM_CONV + 1 + k, ln), :]
        xin[0:HALO, :] = xin[ln:ln + HALO, :]
        xconv_ref[0] = xconv
        stp_ref[0, 0] = st[...]
        xs_pre, b_pre, c_pre, dtr, dtr_t, zs, sts, gains = _ssd_chunk_inputs(xconv, dt_ref[0], z_ref[0], st,
                                                                             gain_ref[...])
        out, st_new = _ssd_chunk(xs_pre, b_pre, c_pre, dtr, dtr_t, zs, sts, dbr_ref[...], dbc_ref[...],
                                 alr_ref[...], alc_ref[...], dsk_ref[...], gains)
        y_ref[0] = jnp.concatenate(out, axis=1).astype(y_ref.dtype)
        for p in range(SSM_PAIRS):
            st[p] = st_new[p]

    params = [cw, cb, dbr, dbc, alr, alc, dsk, gain]
    return pl.pallas_call(
        body, name=name, grid=(nb, nt),
        in_specs=[tile(wx), tile(LANES), tile(SSM_WIDTH)] + [_const2(p.shape) for p in params],
        out_specs=[tile(SSM_WIDTH), tile(wx), st_spec],
        out_shape=[jax.ShapeDtypeStruct((nb, ns, SSM_WIDTH), MXU_DTYPE), jax.ShapeDtypeStruct((nb, ns, wx), f32),
                   jax.ShapeDtypeStruct((nb, nt, SSM_PAIRS, SSM_STATE, LANES), f32)],
        scratch_shapes=[pltpu.VMEM((ln + HALO, wx), f32), pltpu.VMEM((SSM_PAIRS, SSM_STATE, LANES), f32)],
        compiler_params=_cparams("arbitrary", "arbitrary"),
    )(xbc, dt_raw, z, *params)


def ssd_backward(dy, xbc, xconv, dt_raw, z, stp, cw, cb, dbr, dbc, alr, alc, dsk, gain, name):
    nb, ns, wx = xbc.shape
    ln = SSM_CHUNK
    nt = ns // ln
    per = ln // HALO
    rj = lambda j: nt - 1 - j
    tile = lambda c: pl.BlockSpec((1, ln, c), lambda b, j: (b, rj(j), 0))
    before = pl.BlockSpec((1, HALO, wx), lambda b, j: (b, jnp.maximum(rj(j) * per - 1, 0), 0))
    st_spec = pl.BlockSpec((1, 1, SSM_PAIRS, SSM_STATE, LANES), lambda b, j: (b, rj(j), 0, 0, 0))

    def body(dy_ref, xbc_ref, xbcb_ref, xconv_ref, dt_ref, z_ref, stp_ref,
             cw_ref, cb_ref, dbr_ref, dbc_ref, alr_ref, alc_ref, dsk_ref, gain_ref,
             dxbc_ref, ddt_ref, dz_ref, dcw_ref, dcb_ref, ddbr_ref, ddbc_ref, dalr_ref, dalc_ref, ddsk_ref, dgain_ref,
             dxc_ext, dst, xin):
        j = pl.program_id(1)
        first = _first_step()
        at_seq_start = j == nt - 1

        @pl.when(j == 0)
        def _():
            dxc_ext[ln:ln + HALO, :] = jnp.zeros((HALO, wx), f32)
            dst[...] = jnp.zeros_like(dst)

        xs_pre, b_pre, c_pre, dtr, dtr_t, zs, sts, gains = _ssd_chunk_inputs(xconv_ref[0], dt_ref[0], z_ref[0],
                                                                             stp_ref.at[0, 0], gain_ref[...])
        _, vjp = jax.vjp(_ssd_chunk, xs_pre, b_pre, c_pre, dtr, dtr_t, zs, sts, dbr_ref[...], dbc_ref[...],
                         alr_ref[...], alc_ref[...], dsk_ref[...], gains)
        dyv = dy_ref[0].astype(f32)
        cot = ([dyv[:, GROUP_WIDTH * g:GROUP_WIDTH * (g + 1)] for g in range(SSM_GROUPS)],
               [dst[p] for p in range(SSM_PAIRS)])
        dxs, db, dc, ddt, ddt_t, dzs, dsts, ddbr, ddbc, dalr, dalc, ddsk, dgains = vjp(cot)
        for p in range(SSM_PAIRS):
            dst[p] = dsts[p]
        ddt_ref[0] = (ddt + ddt_t.T).astype(ddt_ref.dtype)
        dz_ref[0] = jnp.concatenate(dzs, axis=1).astype(dz_ref.dtype)
        _accum(ddbr_ref, ddbr, first)
        _accum(ddbc_ref, ddbc, first)
        _accum(dalr_ref, dalr, first)
        _accum(dalc_ref, dalc, first)
        _accum(ddsk_ref, ddsk, first)
        _accum(dgain_ref, jnp.concatenate(dgains, axis=1), first)

        dxc = jnp.concatenate(dxs + db + dc, axis=1)
        _accum(dcb_ref, jnp.sum(dxc, axis=0, keepdims=True), first)
        dxc_ext[0:ln, :] = dxc
        dxp = jnp.zeros((ln, wx), f32)
        for k in range(SSM_CONV):
            dxp = dxp + cw_ref[k:k + 1, :] * dxc_ext[pl.ds(SSM_CONV - 1 - k, ln), :]
        dxbc_ref[0] = dxp.astype(dxbc_ref.dtype)
        dxc_ext[ln:ln + HALO, :] = dxc[0:HALO, :]

        xin[0:HALO, :] = jnp.where(at_seq_start, 0.0, xbcb_ref[0])
        xin[HALO:HALO + ln, :] = xbc_ref[0]
        dcw_rows = [jnp.sum(dxc * xin[pl.ds(HALO - SSM_CONV + 1 + k, ln), :], axis=0, keepdims=True)
                    for k in range(SSM_CONV)]
        dcw_rows += [jnp.zeros((1, wx), f32)] * (HALO - SSM_CONV)
        _accum(dcw_ref, jnp.concatenate(dcw_rows, axis=0), first)

    params = [cw, cb, dbr, dbc, alr, alc, dsk, gain]
    pshape = lambda p: jax.ShapeDtypeStruct(p.shape, f32)
    return pl.pallas_call(
        body, name=name, grid=(nb, nt),
        in_specs=[tile(SSM_WIDTH), tile(wx), before, tile(wx), tile(LANES), tile(SSM_WIDTH), st_spec]
        + [_const2(p.shape) for p in params],
        out_specs=[tile(wx), tile(LANES), tile(SSM_WIDTH), _const2((HALO, wx))] + [_const2(p.shape) for p in params[1:]],
        out_shape=[jax.ShapeDtypeStruct((nb, ns, wx), MXU_DTYPE), jax.ShapeDtypeStruct((nb, ns, LANES), MXU_DTYPE),
                   jax.ShapeDtypeStruct((nb, ns, SSM_WIDTH), MXU_DTYPE), jax.ShapeDtypeStruct((HALO, wx), f32)]
        + [pshape(p) for p in params[1:]],
        scratch_shapes=[pltpu.VMEM((ln + HALO, wx), f32), pltpu.VMEM((SSM_PAIRS, SSM_STATE, LANES), f32),
                        pltpu.VMEM((ln + HALO, wx), f32)],
        compiler_params=_cparams("arbitrary", "arbitrary"),
    )(dy, xbc, xbc, xconv, dt_raw, z, stp, *params)


CONF_HALO = 32
CONF_OFF = CONF_HALO - CONF_KERNEL + 1


def _conf_specs(ts, c, nt):
    per = ts // CONF_HALO
    tile = pl.BlockSpec((1, ts, c), lambda b, j: (b, j, 0))
    before = pl.BlockSpec((1, CONF_HALO, c), lambda b, j: (b, jnp.maximum(j * per - 1, 0), 0))
    after = pl.BlockSpec((1, CONF_HALO, c), lambda b, j: (b, jnp.minimum((j + 1) * per, nt * per - 1), 0))
    return tile, before, after


SUBLANES = 8


def _shifted_copies(dst, x):
    rows = x.shape[0]
    dst[0] = x
    for b in range(1, SUBLANES):
        dst[b] = pltpu.roll(x, rows - b, 0)


def _window(copies, off, size):
    b = off % SUBLANES
    return copies[b, pl.ds(off - b, size), :]


def _glu(x):
    return x[:, :CONF_WIDTH] * jax.nn.sigmoid(x[:, CONF_WIDTH:])


def _layernorm_parts(c):
    xc = c - jnp.mean(c, axis=-1, keepdims=True)
    r = lax.rsqrt(jnp.mean(xc * xc, axis=-1, keepdims=True) + EPS)
    return xc * r, r


def conf_forward(glu, cw, cb, ln_g, ln_b, name):
    nb, ns, wg = glu.shape
    w = CONF_WIDTH
    ts = SEQ_TILE
    nt = ns // ts
    tile, before, _ = _conf_specs(ts, wg, nt)

    def body(x_ref, xb_ref, cw_ref, cb_ref, g_ref, b_ref, y_ref, u_rot):
        _shifted_copies(u_rot, jnp.concatenate(
            [jnp.where(pl.program_id(1) == 0, 0.0, _glu(xb_ref[0])), _glu(x_ref[0])], axis=0))
        conv = jnp.broadcast_to(cb_ref[...], (ts, w))
        for k in range(CONF_KERNEL):
            conv = conv + cw_ref[k:k + 1, :] * _window(u_rot, CONF_OFF + k, ts)
        xhat, _ = _layernorm_parts(conv)
        y_ref[0] = _silu(xhat * g_ref[...] + b_ref[...]).astype(y_ref.dtype)

    params = [cw, cb, ln_g, ln_b]
    return pl.pallas_call(
        body, name=name, grid=(nb, nt),
        in_specs=[tile, before] + [_const2(p.shape) for p in params],
        out_specs=pl.BlockSpec((1, ts, w), lambda b, j: (b, j, 0)),
        out_shape=jax.ShapeDtypeStruct((nb, ns, w), MXU_DTYPE),
        scratch_shapes=[pltpu.VMEM((SUBLANES, ts + CONF_HALO, w), f32)],
        compiler_params=_cparams("parallel", "parallel"),
    )(glu, glu, *params)


def conf_backward(dy, glu, cw, cb, ln_g, ln_b, name):
    nb, ns, wg = glu.shape
    w = CONF_WIDTH
    ts = SEQ_TILE
    nt = ns // ts
    te = ts + CONF_HALO
    tile, before, after = _conf_specs(ts, wg, nt)
    dtile, _, dafter = _conf_specs(ts, w, nt)

    def body(dy_ref, dya_ref, x_ref, xb_ref, xa_ref, cw_ref, cb_ref, g_ref, b_ref,
             dx_ref, dcw_ref, dcb_ref, dg_ref, db_ref, u_ext, dc_ext):
        j = pl.program_id(1)
        first = _first_step()
        x = x_ref[0]
        _shifted_copies(u_ext, jnp.concatenate(
            [jnp.where(j == 0, 0.0, _glu(xb_ref[0])), _glu(x), _glu(xa_ref[0])], axis=0))
        conv = jnp.broadcast_to(cb_ref[...], (te, w))
        for k in range(CONF_KERNEL):
            conv = conv + cw_ref[k:k + 1, :] * _window(u_ext, CONF_OFF + k, te)
        xhat, r = _layernorm_parts(conv)
        lnout = xhat * g_ref[...] + b_ref[...]
        sg = jax.nn.sigmoid(lnout)
        rows = lax.broadcasted_iota(jnp.int32, (te, w), 0)
        dyv = jnp.concatenate([dy_ref[0].astype(f32), dya_ref[0].astype(f32)], axis=0)
        dyv = jnp.where(jnp.logical_and(j == nt - 1, rows >= ts), 0.0, dyv)
        dln = dyv * sg * (1.0 + lnout * (1.0 - sg))
        in_tile = rows < ts
        _accum(dg_ref, jnp.sum(jnp.where(in_tile, dln * xhat, 0.0), axis=0, keepdims=True), first)
        _accum(db_ref, jnp.sum(jnp.where(in_tile, dln, 0.0), axis=0, keepdims=True), first)
        dxh = dln * g_ref[...]
        dconv = r * (dxh - jnp.mean(dxh, axis=-1, keepdims=True) - xhat * jnp.mean(dxh * xhat, axis=-1, keepdims=True))
        _shifted_copies(dc_ext, dconv)
        dct = dconv[0:ts, :]
        _accum(dcb_ref, jnp.sum(dct, axis=0, keepdims=True), first)
        du = jnp.zeros((ts, w), f32)
        dcw_rows = []
        for k in range(CONF_KERNEL):
            du = du + cw_ref[k:k + 1, :] * _window(dc_ext, CONF_KERNEL - 1 - k, ts)
            dcw_rows.append(jnp.sum(dct * _window(u_ext, CONF_OFF + k, ts), axis=0, keepdims=True))
        dcw_rows.append(jnp.zeros((1, w), f32))
        _accum(dcw_ref, jnp.concatenate(dcw_rows, axis=0), first)
        sb = jax.nn.sigmoid(x[:, w:])
        dx_ref[0] = jnp.concatenate([du * sb, du * x[:, :w] * sb * (1.0 - sb)], axis=1).astype(dx_ref.dtype)

    params = [cw, cb, ln_g, ln_b]
    return pl.pallas_call(
        body, name=name, grid=(nb, nt),
        in_specs=[dtile, dafter, tile, before, after] + [_const2(p.shape) for p in params],
        out_specs=[tile] + [_const2(p.shape) for p in params],
        out_shape=[jax.ShapeDtypeStruct((nb, ns, wg), MXU_DTYPE)] + [jax.ShapeDtypeStruct(p.shape, f32) for p in params],
        scratch_shapes=[pltpu.VMEM((SUBLANES, te + CONF_HALO, w), f32), pltpu.VMEM((SUBLANES, te, w), f32)],
        compiler_params=_cparams("arbitrary", "arbitrary"),
    )(dy, dy, glu, glu, glu, *params)


def _row(v):
    return v.reshape(1, -1).astype(f32)


def _pad_to(v, n, axis):
    pads = [(0, 0)] * v.ndim
    pads[axis] = (0, n - v.shape[axis])
    return jnp.pad(v, pads)


def _block_diag(w):
    nh, d, _ = w.shape
    eye = jnp.eye(nh, dtype=w.dtype)
    return (eye[:, None, :, None] * w[:, :, None, :]).reshape(nh * d, nh * d)


def _diag_blocks(m, nh):
    d = m.shape[0] // nh
    idx = jnp.arange(nh)
    return m.reshape(nh, d, nh, d)[idx, :, idx, :]


def _mix_even_fwd(h, gpre, w, wl, nb, ns, rider=None):
    t = nb * ns
    w_in = wl["w_in"]
    w_lx, w_lg = Part(w_in, 0, LRU_WIDTH, 1), Part(w_in, LRU_WIDTH, LRU_WIDTH, 1)
    w_qkv = Part(w_in, 2 * LRU_WIDTH, 3 * SB_WIDTH, 1)
    xpre, gate, qkv = norm_matmul(h, gpre, [w_lx, w_lg, w_qkv], [f32, f32, f32], name="ev_in_proj")
    lru_p = [w["ev_lru_conv_w"][0], _row(w["ev_lru_conv_b"][0]),
             _block_diag(w["ev_lru_gate_a_w"][0]).astype(MXU_DTYPE), _row(w["ev_lru_gate_a_b"][0]),
             _block_diag(w["ev_lru_gate_x_w"][0]).astype(MXU_DTYPE), _row(w["ev_lru_gate_x_b"][0]),
             _row(w["ev_lru_lambda"][0])]
    xpre3, gate3, qkv3 = xpre.reshape(nb, ns, -1), gate.reshape(nb, ns, -1), qkv.reshape(nb, ns, -1)
    y_a, xc, hs = lru_forward(xpre3, gate3, *lru_p, name="ev_lru_fwd")
    o = sb_forward(qkv3, name="ev_sb_fwd", rider=rider)
    carried = None
    if rider is not None:
        o, carried = o
    ys = [y_a.reshape(t, -1), o.reshape(t, -1)]
    saved = dict(xpre=xpre3, gate=gate3, qkv=qkv3, xc=xc, hs=hs, o=o, lru_p=lru_p)
    return ys, saved, carried


def _mix_even_bwd(dys, saved, wtl, nb, ns, lru_rider=None, attention_rider=None):
    t = nb * ns
    dy_a, dy_b = [d.reshape(nb, ns, -1) for d in dys]
    outs = lru_backward(dy_a, saved["xpre"], saved["gate"], saved["xc"], saved["hs"], *saved["lru_p"],
                        name="ev_lru_bwd", rider=lru_rider)
    lru_carried = None
    if lru_rider is not None:
        outs, lru_carried = outs
    dxp, dgt, dcw, dcb, dga, dgab, dgx, dgxb, dlam = outs
    rider = attention_rider(lru_carried) if attention_rider is not None else None
    carried = None
    if rider is None:
        dq, dk, dv = sb_backward(saved["qkv"], saved["o"], dy_b, name="ev_sb_bwd")
    else:
        (dq, dk, dv), carried = sb_backward(saved["qkv"], saved["o"], dy_b, name="ev_sb_bwd", rider=rider)
    w_in_t = wtl["w_in"]
    pieces = [dxp, dgt, dq, dk, dv]
    gs = [d.reshape(t, -1) for d in pieces]
    wts = [Part(w_in_t, LRU_WIDTH * i, LRU_WIDTH, 0) for i in range(5)]
    grads = {
        "ev_lru_conv_w": dcw[:LRU_CONV][None], "ev_lru_conv_b": dcb,
        "ev_lru_gate_a_w": _diag_blocks(dga, LRU_HEADS)[None], "ev_lru_gate_a_b": dgab,
        "ev_lru_gate_x_w": _diag_blocks(dgx, LRU_HEADS)[None], "ev_lru_gate_x_b": dgxb,
        "ev_lru_lambda": dlam,
    }
    return gs, wts, grads, carried


def _odd_params(w):
    ssd_p = [w["od_ssm_conv_w"][0], _row(w["od_ssm_conv_b"][0]),
             _pad_to(_row(w["od_ssm_dt_bias"][0]), LANES, 1), _pad_to(_row(w["od_ssm_dt_bias"][0]), LANES, 1).T,
             _pad_to(_row(w["od_ssm_a_log"][0]), LANES, 1), _pad_to(_row(w["od_ssm_a_log"][0]), LANES, 1).T,
             _pad_to(_row(w["od_ssm_d"][0]), LANES, 1), _row(w["od_ssm_norm"][0])]
    conf_p = [_pad_to(w["od_cm_conv_w"][0], CONF_HALO, 0), _row(w["od_cm_conv_b"][0]),
              _row(w["od_cm_ln_g"][0]), _row(w["od_cm_ln_b"][0])]
    return ssd_p, conf_p


ODD_SPLITS = (SSM_WIDTH, SSM_WIDTH + SSM_XBC, SSM_WIDTH + SSM_XBC + SSM_HEADS)


def _mix_odd_fwd(h, gpre, w, wl, nb, ns, rider=None):
    assert rider is None
    t = nb * ns
    w_in = wl["w_in"]
    s0, s1, s2 = ODD_SPLITS
    w_al = jnp.concatenate([w_in[:, :s1], w_in[:, s2:], _pad_to(w_in[:, s1:s2], LANES, 1)], axis=1)
    widths = (s0, s1 - s0, w_in.shape[1] - s2, LANES)
    starts = (0, s0, s1, s1 + widths[2])
    zz, xbc, glu, dtr = norm_matmul(h, gpre, [Part(w_al, a, n, 1) for a, n in zip(starts, widths)], [f32] * 4,
                                    name="od_in_proj")
    ssd_p, conf_p = _odd_params(w)
    zz3, xbc3, dtr3, glu3 = [a.reshape(nb, ns, -1) for a in (zz, xbc, dtr, glu)]
    y_c, xconv, stp = ssd_forward(xbc3, dtr3, zz3, *ssd_p, name="od_ssd_fwd")
    y_d = conf_forward(glu3, *conf_p, name="od_conf_fwd")
    ys = [y_c.reshape(t, -1), y_d.reshape(t, -1)]
    saved = dict(z=zz3, xbc=xbc3, dtr=dtr3, glu=glu3, xconv=xconv, stp=stp, ssd_p=ssd_p, conf_p=conf_p)
    return ys, saved, None


def _mix_odd_bwd(dys, saved, wtl, nb, ns, lru_rider=None, attention_rider=None):
    assert lru_rider is None and attention_rider is None
    t = nb * ns
    dy_c, dy_d = [d.reshape(nb, ns, -1) for d in dys]
    outs = ssd_backward(dy_c, saved["xbc"], saved["xconv"], saved["dtr"], saved["z"], saved["stp"], *saved["ssd_p"],
                        name="od_ssd_bwd")
    dxbc, ddt, dz, dcw, dcb, ddbr, ddbc, dalr, dalc, ddsk, dgain = outs
    dglu, ccw, ccb, clg, clb = conf_backward(dy_d, saved["glu"], *saved["conf_p"], name="od_conf_bwd")
    w_in_t = wtl["w_in"]
    s0, s1, s2 = ODD_SPLITS
    carried = None
    gs = [d.reshape(t, -1) for d in (dz, dxbc, dglu, ddt)]
    wt_al = jnp.concatenate([w_in_t[:s1], w_in_t[s2:], _pad_to(w_in_t[s1:s2], LANES, 0)], axis=0)
    widths = (s0, s1 - s0, w_in_t.shape[0] - s2, LANES)
    starts = (0, s0, s1, s1 + widths[2])
    wts = [Part(wt_al, a, n, 0) for a, n in zip(starts, widths)]
    nh = SSM_HEADS
    grads = {
        "od_ssm_conv_w": dcw[:SSM_CONV][None], "od_ssm_conv_b": dcb,
        "od_ssm_dt_bias": ddbr[:, :nh] + ddbc[:nh, 0][None], "od_ssm_a_log": dalr[:, :nh] + dalc[:nh, 0][None],
        "od_ssm_d": ddsk[:, :nh], "od_ssm_norm": dgain,
        "od_cm_conv_w": ccw[:CONF_KERNEL][None], "od_cm_conv_b": ccb, "od_cm_ln_g": clg, "od_cm_ln_b": clb,
    }
    return gs, wts, grads, carried


LAYER_MATRICES = ("w_in", "w_out", "mlp_w1", "mlp_w2", "ple_w_proj", "ple_w_gate")
NORM_NAMES = ("norm_mix_pre", "norm_mix_post", "norm_mlp_pre", "norm_mlp_post", "norm_ple")


class NoOverlap:
    sums, from_chips = {}, {}

    def attention_fwd_rider(self):
        return None

    def weights_arrived(self, carried, wl, wtl):
        raise NotImplementedError

    def mlp_bwd_rider(self, layer1_grads):
        return None

    def after_mlp_bwd(self, carried):
        pass

    def lru_bwd_rider(self, layer0_grads):
        return None

    def attention_bwd_rider(self, carried):
        return None

    def after_attention_bwd(self, carried):
        pass


OUT_SPLIT = (LRU_WIDTH, SSM_WIDTH)


def local_step(x, p, target, w, wl, wtl, comm=NoOverlap()):
    nb, ns, d = x.shape
    t = nb * ns
    h = x.reshape(t, d)
    depth = p.shape[0]
    wl, wtl = list(wl), list(wtl)
    tapes = []
    for i in range(depth):
        even = i % 2 == 0
        tag = f"l{i}_"
        gpre = _row(w["norm_mix_pre"][i])
        rider = comm.attention_fwd_rider() if i == 0 else None
        ys, saved, carried = (_mix_even_fwd if even else _mix_odd_fwd)(h, gpre, w, wl[i], nb, ns, rider)
        if rider is not None:
            wl, wtl = comm.weights_arrived(carried, wl, wtl)
        w_out = wl[i]["w_out"]
        split = OUT_SPLIT[i % 2]
        w_outs = [Part(w_out, 0, split, 0), Part(w_out, split, w_out.shape[0] - split, 0)]
        h1, m = matmul_residual_norm(ys, w_outs, h, _row(w["norm_mix_post"][i]), name=tag + "out_proj")
        a1, = norm_matmul(h1, _row(w["norm_mlp_pre"][i]), [wl[i]["mlp_w1"]], [MXU_DTYPE], name=tag + "mlp_up")
        h2, f = matmul_residual_norm([a1], [wl[i]["mlp_w2"]], h1, _row(w["norm_mlp_post"][i]), name=tag + "mlp_down",
                                     relu2=True)
        pi = p[i].reshape(t, -1)
        ple_args = (h2, pi, wl[i]["ple_w_gate"], wl[i]["ple_w_proj"], _row(w["norm_ple"][i]))
        if i < depth - 1:
            h3, gl, emb = ple_forward(*ple_args, name=tag + "ple")
        else:
            loss_row, dh, gl, emb = ple_forward(*ple_args, name=tag + "ple_loss", target=target.reshape(t, d))
            h3 = None
        tapes.append(dict(h=h, ys=ys, w_outs=w_outs, saved=saved, h1=h1, m=m, a1=a1, h2=h2, f=f, pi=pi, gl=gl,
                          emb=emb))
        h = h3

    grads = {}
    norm_grads = {k: [None] * depth for k in NORM_NAMES}
    layer_grads = [None] * depth
    for i in reversed(range(depth)):
        even = i % 2 == 0
        tag = f"l{i}_"
        tp = tapes[i]
        lg = {}
        to_sibling = comm.mlp_bwd_rider(layer_grads[1]) if i == 0 else None
        dh2, dgl, demb, dg = ple_backward(dh, tp["h2"], tp["gl"], tp["emb"], _row(w["norm_ple"][i]),
                                          wtl[i]["ple_w_gate"], name=tag + "ple_bwd")
        norm_grads["norm_ple"][i] = dg
        lg["ple_w_gate"] = weight_grad(tp["h2"], dgl, name=tag + "dw_gate")
        lg["ple_w_proj"] = weight_grad(tp["pi"], demb, name=tag + "dw_proj")
        outs = bwd_through_norm_out(dh2, tp["f"], _row(w["norm_mlp_post"][i]), [wtl[i]["mlp_w2"]], [MXU_DTYPE],
                                    name=tag + "mlp_down_bwd", relu2_of=tp["a1"], rider=to_sibling)
        d_f, (da1,), dg = outs[:3]
        if to_sibling is not None:
            comm.after_mlp_bwd(outs[3])
        norm_grads["norm_mlp_post"][i] = dg
        lg["mlp_w2"] = weight_grad(tp["a1"], d_f, name=tag + "dw2", prologue="relu2")
        gpre = _row(w["norm_mlp_pre"][i])
        dh1, dg = bwd_through_norm_in(dh2, [da1], [wtl[i]["mlp_w1"]], tp["h1"], gpre, name=tag + "mlp_up_bwd")
        norm_grads["norm_mlp_pre"][i] = dg
        lg["mlp_w1"] = weight_grad(tp["h1"], da1, name=tag + "dw1", prologue="rms", gain=gpre)
        wt_out = wtl[i]["w_out"]
        split = tp["w_outs"][0].shape[0]
        dm, dys, dg = bwd_through_norm_out(dh1, tp["m"], _row(w["norm_mix_post"][i]),
                                           [Part(wt_out, 0, split, 1),
                                            Part(wt_out, split, wt_out.shape[1] - split, 1)],
                                           [f32, MXU_DTYPE if even else f32],
                                           name=tag + "out_proj_bwd")
        norm_grads["norm_mix_post"][i] = dg
        lg["w_out"] = jnp.concatenate([weight_grad(y, dm, name=tag + f"dw_out{k}") for k, y in enumerate(tp["ys"])],
                                      axis=0)
        lru_rider = comm.lru_bwd_rider(lg) if i == 0 else None
        gs, wts, mix_grads, carried = (_mix_even_bwd if even else _mix_odd_bwd)(
            dys, tp["saved"], wtl[i], nb, ns, lru_rider, comm.attention_bwd_rider if lru_rider is not None else None)
        if carried is not None:
            comm.after_attention_bwd(carried)
        grads.update(mix_grads)
        gpre = _row(w["norm_mix_pre"][i])
        dh, dg = bwd_through_norm_in(dh1, gs, wts, tp["h"], gpre, name=tag + "in_proj_bwd")
        norm_grads["norm_mix_pre"][i] = dg
        dw_in = weight_grads_of_norm(tp["h"], gpre, gs, name=tag + "dw_in")
        if not even:
            dw_in = [dw_in[0], dw_in[1], dw_in[3][:, :SSM_HEADS], dw_in[2]]
        lg["w_in"] = jnp.concatenate(dw_in, axis=1)
        layer_grads[i] = lg
    for k, v in norm_grads.items():
        grads[k] = jnp.concatenate(v, axis=0)
    return loss_row[0, 0], dh.reshape(nb, ns, d), grads, layer_grads


MESH_ID = pl.DeviceIdType.MESH
ANY = pl.BlockSpec(memory_space=pl.ANY)


def _mesh_pos():
    return lax.axis_index("x"), lax.axis_index("y"), lax.axis_index("c")


def all_gather(shards, name):
    return _run_alone(gather_rider(shards), name)


class Rider:
    def __init__(self, inputs, out_shapes, scratch_shapes, start, finish, middle=None):
        self.inputs, self.out_shapes, self.scratch_shapes = list(inputs), list(out_shapes), list(scratch_shapes)
        self.start, self.finish, self.middle = start, finish, middle


def _run_alone(rider, name):
    ni, no = len(rider.inputs), len(rider.out_shapes)

    def body(*refs):
        args = (refs[:ni], refs[ni:ni + no], refs[ni + no:])
        rider.start(*args)
        if rider.middle is not None:
            rider.middle(*args)
        rider.finish(*args)

    return pl.pallas_call(
        body, name=name, out_shape=rider.out_shapes, in_specs=[ANY] * ni, out_specs=[ANY] * no,
        scratch_shapes=rider.scratch_shapes,
    )(*rider.inputs)


def _ride(rider, body, in_specs, out_specs, out_shape, scratch_shapes, grid):
    in_specs, out_specs, out_shape = list(in_specs), list(out_specs), list(out_shape)
    scratch_shapes = list(scratch_shapes)
    if rider is None:
        return body, in_specs, out_specs, out_shape, scratch_shapes
    n_in, n_out, n_scr = len(in_specs), len(out_specs), len(scratch_shapes)
    ri, ro = len(rider.inputs), len(rider.out_shapes)
    total = math.prod(grid)

    def carrying(*refs):
        ins, r_ins = refs[:n_in], refs[n_in:n_in + ri]
        o0 = n_in + ri
        outs, r_outs = refs[o0:o0 + n_out], refs[o0 + n_out:o0 + n_out + ro]
        s0 = o0 + n_out + ro
        scr, r_scr = refs[s0:s0 + n_scr], refs[s0 + n_scr:]
        step = pl.program_id(0)
        for ax in range(1, len(grid)):
            step = step * grid[ax] + pl.program_id(ax)
        args = (r_ins, r_outs, r_scr)
        pl.when(step == 0)(lambda: rider.start(*args))
        if rider.middle is not None:
            pl.when(step == total // 2)(lambda: rider.middle(*args))
        body(*ins, *outs, *scr)
        pl.when(step == total - 1)(lambda: rider.finish(*args))

    return (carrying, in_specs + [ANY] * ri, out_specs + [ANY] * ro, out_shape + rider.out_shapes,
            scratch_shapes + rider.scratch_shapes)


def gather_rider(shards):
    n = len(shards)

    def parts(x_refs, out_refs, scr):
        send_sems, recv_sems, local_sems = scr
        x, y, c = _mesh_pos()
        chips = [(1 - x, y), (x, 1 - y), (1 - x, 1 - y)]

        def slot(a, px, py, pc):
            return out_refs[a].at[4 * px + 2 * py + pc]

        def copy(a, k, block, to, src=None):
            return pltpu.make_async_remote_copy(
                src_ref=slot(a, *block) if src is None else src, dst_ref=slot(a, *block),
                send_sem=send_sems.at[7 * a + k], recv_sem=recv_sems.at[7 * a + k], device_id=to,
                device_id_type=MESH_ID)

        me, sibling = (x, y, c), (x, y, 1 - c)
        def mine():
            return [pltpu.make_async_copy(x_refs[a], slot(a, *me), local_sems.at[a]) for a in range(n)]

        def first():
            out = []
            for j, chip in enumerate(chips):
                out += [copy(a, 1 + j, me, (*chip, c), src=x_refs[a]) for a in range(n)]
            return out + [copy(a, 0, me, sibling, src=x_refs[a]) for a in range(n)]

        def passed(j):
            return [copy(a, 4 + j, (*chips[j], c), sibling) for a in range(n)]

        return me, sibling, chips, c, copy, mine, first, passed

    def start(x_refs, out_refs, scr):
        _, _, _, _, _, mine, first, _ = parts(x_refs, out_refs, scr)
        for cp in mine() + first():
            cp.start()

    def middle(x_refs, out_refs, scr):
        me, _, chips, c, copy, _, _, passed = parts(x_refs, out_refs, scr)
        for j, chip in enumerate(chips):
            for a, fwd in enumerate(passed(j)):
                copy(a, 1 + j, (*chip, c), me).wait_recv()
                fwd.start()

    def finish(x_refs, out_refs, scr):
        me, sibling, chips, c, copy, mine, first, passed = parts(x_refs, out_refs, scr)
        for a in range(n):
            copy(a, 0, sibling, me).wait_recv()
        for j, chip in enumerate(chips):
            for a in range(n):
                copy(a, 4 + j, (*chip, 1 - c), me).wait_recv()
        for cp in first() + [cp for j in range(len(chips)) for cp in passed(j)]:
            cp.wait_send()
        for cp in mine():
            cp.wait()

    return Rider(shards, [jax.ShapeDtypeStruct((N_DEV,) + s.shape, s.dtype) for s in shards],
                 [pltpu.SemaphoreType.DMA((7 * n,)), pltpu.SemaphoreType.DMA((7 * n,)), pltpu.SemaphoreType.DMA((n,))],
                 start, finish, middle)


def scatter_to_sibling(parts, name):
    return _run_alone(sibling_rider(parts), name)


def sibling_rider(parts):
    n = len(parts)

    def copies(g_refs, out_refs, scr):
        send_sems, recv_sems = scr
        x, y, c = _mesh_pos()
        return [pltpu.make_async_remote_copy(
            src_ref=g_refs[a].at[2 * chip + (1 - c)], dst_ref=out_refs[a].at[chip],
            send_sem=send_sems.at[4 * a + chip], recv_sem=recv_sems.at[4 * a + chip], device_id=(x, y, 1 - c),
            device_id_type=MESH_ID) for a in range(n) for chip in range(4)]

    def start(*refs):
        for cp in copies(*refs):
            cp.start()

    def finish(*refs):
        cps = copies(*refs)
        for cp in cps:
            cp.wait_recv()
        for cp in cps:
            cp.wait_send()

    return Rider(parts, [jax.ShapeDtypeStruct((4,) + p.shape[1:], p.dtype) for p in parts],
                 [pltpu.SemaphoreType.DMA((4 * n,)), pltpu.SemaphoreType.DMA((4 * n,))], start, finish)


def scatter_to_chips(partials, name):
    return _run_alone(chips_rider(partials), name)


def chips_rider(partials):
    n = len(partials)

    def copies(p_refs, out_refs, scr):
        send_sems, recv_sems = scr
        x, y, c = _mesh_pos()
        chips = [(1 - x, y), (x, 1 - y), (1 - x, 1 - y)]
        return [pltpu.make_async_remote_copy(
            src_ref=p_refs[a].at[2 * px + py], dst_ref=out_refs[a].at[j],
            send_sem=send_sems.at[3 * a + j], recv_sem=recv_sems.at[3 * a + j], device_id=(px, py, c),
            device_id_type=MESH_ID) for a in range(n) for j, (px, py) in enumerate(chips)]

    def start(*refs):
        for cp in copies(*refs):
            cp.start()

    def finish(*refs):
        cps = copies(*refs)
        for cp in cps:
            cp.wait_recv()
        for cp in cps:
            cp.wait_send()

    return Rider(partials, [jax.ShapeDtypeStruct((3,) + p.shape[1:], p.dtype) for p in partials],
                 [pltpu.SemaphoreType.DMA((3 * n,)), pltpu.SemaphoreType.DMA((3 * n,))], start, finish)


ICI_DTYPE = jnp.bfloat16
ELEMENTWISE_BLOCK_BYTES = 1 << 20


def _row_tile(rows, cols):
    cap = max(16, ELEMENTWISE_BLOCK_BYTES // (4 * cols))
    best = [t for t in range(16, min(rows, cap) + 1, 16) if rows % t == 0]
    return best[-1] if best else rows


def add_sibling_parts(parts, received, core, name):
    _, r, n = parts.shape
    tr = _row_tile(r, n)

    def body(c_ref, a_ref, b_ref, o_ref, ob_ref):
        s = a_ref[...] + b_ref[...]
        o_ref[...] = s
        ob_ref[...] = s.astype(ob_ref.dtype)

    blk = pl.BlockSpec((1, tr, n), lambda i, j, c_ref: (i, j, 0))
    return pl.pallas_call(
        body, name=name,
        grid_spec=pltpu.PrefetchScalarGridSpec(
            num_scalar_prefetch=1, grid=(4, r // tr),
            in_specs=[pl.BlockSpec((1, tr, n), lambda i, j, c_ref: (2 * i + c_ref[0], j, 0)), blk],
            out_specs=[blk, blk]),
        out_shape=[jax.ShapeDtypeStruct((4, r, n), f32), jax.ShapeDtypeStruct((4, r, n), ICI_DTYPE)],
        compiler_params=_cparams("parallel", "parallel"),
    )(core, parts, received)


def _adamw(w, g, m, v):
    m = ADAM_B1 * m + (1.0 - ADAM_B1) * g
    v = ADAM_B2 * v + (1.0 - ADAM_B2) * jnp.square(g)
    m_hat = m / (1.0 - ADAM_B1 ** ADAM_STEP)
    v_hat = v / (1.0 - ADAM_B2 ** ADAM_STEP)
    delta = -ADAM_LR * (m_hat / (jnp.sqrt(v_hat) + ADAM_EPS) + ADAM_WD * w)
    return delta, m, v


def adamw_sharded(partial, received, chip, w, m, v, name):
    _, r, n = partial.shape

    def body(k_ref, p_ref, r_ref, w_ref, m_ref, v_ref, g_out, d_out, m_out, v_out):
        g = p_ref[0] + r_ref[0].astype(f32)
        g = g + r_ref[1].astype(f32)
        g = g + r_ref[2].astype(f32)
        delta, mn, vn = _adamw(w_ref[...], g, m_ref[...], v_ref[...])
        g_out[...] = g
        d_out[...] = delta
        m_out[...] = mn
        v_out[...] = vn

    tr = _row_tile(r, n)
    flat = pl.BlockSpec((tr, n), lambda j, k_ref: (j, 0))
    return pl.pallas_call(
        body, name=name,
        grid_spec=pltpu.PrefetchScalarGridSpec(
            num_scalar_prefetch=1, grid=(r // tr,),
            in_specs=[pl.BlockSpec((1, tr, n), lambda j, k_ref: (k_ref[0], j, 0)),
                      pl.BlockSpec((3, tr, n), lambda j, k_ref: (0, j, 0)), flat, flat, flat],
            out_specs=[flat] * 4),
        out_shape=[jax.ShapeDtypeStruct((r, n), f32)] * 4,
        compiler_params=_cparams("parallel"),
    )(chip, partial, received, w, m, v)


def adamw_replicated(gathered, w, m, v, name):
    _, r, n = gathered.shape

    def body(g_ref, w_ref, m_ref, v_ref, g_out, d_out, m_out, v_out):
        g = g_ref[0]
        for k in range(1, N_DEV):
            g = g + g_ref[k]
        delta, mn, vn = _adamw(w_ref[...], g, m_ref[...], v_ref[...])
        g_out[...] = g
        d_out[...] = delta
        m_out[...] = mn
        v_out[...] = vn

    return pl.pallas_call(
        body, name=name,
        out_shape=[jax.ShapeDtypeStruct((r, n), f32)] * 4,
        compiler_params=pltpu.CompilerParams(vmem_limit_bytes=VMEM_LIMIT),
    )(gathered, w, m, v)


W_NAMES = ['ev_w_in', 'ev_lru_conv_w', 'ev_lru_conv_b', 'ev_lru_gate_a_w', 'ev_lru_gate_a_b', 'ev_lru_gate_x_w',
           'ev_lru_gate_x_b', 'ev_lru_lambda', 'ev_w_out', 'od_w_in', 'od_ssm_conv_w', 'od_ssm_conv_b',
           'od_ssm_dt_bias', 'od_ssm_a_log', 'od_ssm_d', 'od_ssm_norm', 'od_cm_conv_w', 'od_cm_conv_b', 'od_cm_ln_g',
           'od_cm_ln_b', 'od_w_out', 'norm_mix_pre', 'norm_mix_post', 'norm_mlp_pre', 'norm_mlp_post', 'norm_ple',
           'mlp_w1', 'mlp_w2', 'ple_w_proj', 'ple_w_gate']
BIG_SHARDED = {'ev_w_in': 2, 'ev_w_out': 1, 'od_w_in': 2, 'od_w_out': 1, 'mlp_w1': 2, 'mlp_w2': 1, 'ple_w_proj': 2,
               'ple_w_gate': 1}
SMALL_SHARDED = {'ev_lru_conv_w': 2, 'od_ssm_conv_w': 2, 'od_ssm_conv_b': 1, 'od_ssm_norm': 1, 'od_cm_conv_w': 2,
                 'od_cm_conv_b': 1, 'od_cm_ln_g': 1, 'od_cm_ln_b': 1}
SHARDED = {**BIG_SHARDED, **SMALL_SHARDED}
REPLICATED = [n for n in W_NAMES if n not in SHARDED]


def _round_up(n, k):
    return -(-n // k) * k


def _pack_rows(flat, rows_multiple):
    n = flat.shape[0]
    total = _round_up(n, LANES * rows_multiple)
    return jnp.pad(flat, (0, total - n)).reshape(-1, LANES)


def _unpack(flat, shapes):
    out, off = {}, 0
    for name, shape in shapes.items():
        size = math.prod(shape)
        out[name] = flat[off:off + size].reshape(shape)
        off += size
    return out


def _unshard(g8, shape, axis):
    g = jnp.moveaxis(g8.reshape((N_DEV,) + tuple(shape)), 0, axis)
    return g.reshape(tuple(shape[:axis]) + (N_DEV * shape[axis],) + tuple(shape[axis + 1:]))


def _to_shards(g, axis):
    shard = g.shape[axis] // N_DEV
    g = g.reshape(g.shape[:axis] + (N_DEV, shard) + g.shape[axis + 1:])
    return jnp.moveaxis(g, axis, 0)


def _pack_small(tree):
    return _pack_rows(jnp.concatenate([tree[k].astype(f32).reshape(-1) for k in SMALL_SHARDED]), 16)


def _layer_entry(i, key):
    mixer = "ev" if i % 2 == 0 else "od"
    return {"w_in": (mixer + "_w_in", i // 2, 1), "w_out": (mixer + "_w_out", i // 2, 0),
            "mlp_w1": ("mlp_w1", i, 1), "mlp_w2": ("mlp_w2", i, 0),
            "ple_w_proj": ("ple_w_proj", i, 1), "ple_w_gate": ("ple_w_gate", i, 0)}[key]


def _layer_shards(tree, i):
    out = {}
    for key in LAYER_MATRICES:
        name, idx, _ = _layer_entry(i, key)
        out[key] = tree[name][idx]
    return out


def _gather_early(w):
    small = _pack_small(w)
    terms, rest = [], small
    for _ in range(3):
        term = rest.astype(MXU_DTYPE)
        terms.append(term)
        rest = rest - term.astype(f32)
    outs = all_gather([_layer_shards(w, 0)["w_in"].astype(MXU_DTYPE), jnp.concatenate(terms, axis=0)],
                      name="gather_weights")
    w_in = _unshard(outs[0], outs[0].shape[1:], _layer_entry(0, "w_in")[2])
    wl, wtl = {"w_in": w_in}, {"w_in": w_in.T}
    full = {k: w[k] for k in REPLICATED}
    t = outs[-1].astype(f32)
    nr = small.shape[0]
    vals = (t[:, :nr] + t[:, nr:2 * nr] + t[:, 2 * nr:]).reshape(N_DEV, -1)
    off = 0
    for k, axis in SMALL_SHARDED.items():
        size = math.prod(w[k].shape)
        full[k] = _unshard(vals[:, off:off + size], w[k].shape, axis)
        off += size
    return full, wl, wtl


class Overlap:
    LATE = [(0, key) for key in LAYER_MATRICES if key != "w_in"] + [(1, key) for key in LAYER_MATRICES]
    EARLY_GRADS = [(0, key) for key in LAYER_MATRICES if key != "w_in"]

    def __init__(self, w):
        self.w = w
        self.sums, self.from_chips, self.parts = {}, {}, {}

    def attention_fwd_rider(self):
        shards = [_layer_shards(self.w, 0), _layer_shards(self.w, 1)]
        return gather_rider([shards[i][key].astype(MXU_DTYPE) for i, key in self.LATE])

    def weights_arrived(self, carried, wl, wtl):
        wl = [dict(wl[0]), {}]
        wtl = [dict(wtl[0]), {}]
        for (i, key), g8 in zip(self.LATE, carried):
            wl[i][key] = _unshard(g8, g8.shape[1:], _layer_entry(i, key)[2])
            wtl[i][key] = wl[i][key].T
        return wl, wtl

    def _to_sibling(self, ids, layer_grads):
        for i, key in ids:
            self.parts[i, key] = _to_shards(layer_grads[key], _layer_entry(i, key)[2])
        return sibling_rider([self.parts[e] for e in ids])

    def _add(self, ids, carried):
        core = jnp.reshape(lax.axis_index("c"), (1,)).astype(jnp.int32)
        for (i, key), got in zip(ids, carried):
            self.sums[i, key] = add_sibling_parts(self.parts[i, key], got, core, name=f"add_sibling_l{i}_{key}")

    def mlp_bwd_rider(self, layer1_grads):
        return self._to_sibling([(1, key) for key in LAYER_MATRICES], layer1_grads)

    def after_mlp_bwd(self, carried):
        self._add([(1, key) for key in LAYER_MATRICES], carried)

    def lru_bwd_rider(self, layer0_grads):
        return self._to_sibling(self.EARLY_GRADS, layer0_grads)

    def attention_bwd_rider(self, carried):
        self._add(self.EARLY_GRADS, carried)
        self.travelling = list(self.sums)
        return chips_rider([self.sums[e][1] for e in self.travelling])

    def after_attention_bwd(self, carried):
        for e, got in zip(self.travelling, carried):
            self.from_chips[e] = got


def _pack_replicated(tree):
    return _pack_rows(jnp.concatenate([tree[k].astype(f32).reshape(-1) for k in REPLICATED]), 8)


def kernel(x, p, ev_w_in, ev_lru_conv_w, ev_lru_conv_b, ev_lru_gate_a_w, ev_lru_gate_a_b, ev_lru_gate_x_w, ev_lru_gate_x_b, ev_lru_lambda, ev_w_out, od_w_in, od_ssm_conv_w, od_ssm_conv_b, od_ssm_dt_bias, od_ssm_a_log, od_ssm_d, od_ssm_norm, od_cm_conv_w, od_cm_conv_b, od_cm_ln_g, od_cm_ln_b, od_w_out, norm_mix_pre, norm_mix_post, norm_mlp_pre, norm_mlp_post, norm_ple, mlp_w1, mlp_w2, ple_w_proj, ple_w_gate, loss_target, m_ev_w_in, m_ev_lru_conv_w, m_ev_lru_conv_b, m_ev_lru_gate_a_w, m_ev_lru_gate_a_b, m_ev_lru_gate_x_w, m_ev_lru_gate_x_b, m_ev_lru_lambda, m_ev_w_out, m_od_w_in, m_od_ssm_conv_w, m_od_ssm_conv_b, m_od_ssm_dt_bias, m_od_ssm_a_log, m_od_ssm_d, m_od_ssm_norm, m_od_cm_conv_w, m_od_cm_conv_b, m_od_cm_ln_g, m_od_cm_ln_b, m_od_w_out, m_norm_mix_pre, m_norm_mix_post, m_norm_mlp_pre, m_norm_mlp_post, m_norm_ple, m_mlp_w1, m_mlp_w2, m_ple_w_proj, m_ple_w_gate, v_ev_w_in, v_ev_lru_conv_w, v_ev_lru_conv_b, v_ev_lru_gate_a_w, v_ev_lru_gate_a_b, v_ev_lru_gate_x_w, v_ev_lru_gate_x_b, v_ev_lru_lambda, v_ev_w_out, v_od_w_in, v_od_ssm_conv_w, v_od_ssm_conv_b, v_od_ssm_dt_bias, v_od_ssm_a_log, v_od_ssm_d, v_od_ssm_norm, v_od_cm_conv_w, v_od_cm_conv_b, v_od_cm_ln_g, v_od_cm_ln_b, v_od_w_out, v_norm_mix_pre, v_norm_mix_post, v_norm_mlp_pre, v_norm_mlp_post, v_norm_ple, v_mlp_w1, v_mlp_w2, v_ple_w_proj, v_ple_w_gate):
    ws = [ev_w_in, ev_lru_conv_w, ev_lru_conv_b, ev_lru_gate_a_w, ev_lru_gate_a_b, ev_lru_gate_x_w, ev_lru_gate_x_b, ev_lru_lambda, ev_w_out, od_w_in, od_ssm_conv_w, od_ssm_conv_b, od_ssm_dt_bias, od_ssm_a_log, od_ssm_d, od_ssm_norm, od_cm_conv_w, od_cm_conv_b, od_cm_ln_g, od_cm_ln_b, od_w_out, norm_mix_pre, norm_mix_post, norm_mlp_pre, norm_mlp_post, norm_ple, mlp_w1, mlp_w2, ple_w_proj, ple_w_gate]
    ms = [m_ev_w_in, m_ev_lru_conv_w, m_ev_lru_conv_b, m_ev_lru_gate_a_w, m_ev_lru_gate_a_b, m_ev_lru_gate_x_w, m_ev_lru_gate_x_b, m_ev_lru_lambda, m_ev_w_out, m_od_w_in, m_od_ssm_conv_w, m_od_ssm_conv_b, m_od_ssm_dt_bias, m_od_ssm_a_log, m_od_ssm_d, m_od_ssm_norm, m_od_cm_conv_w, m_od_cm_conv_b, m_od_cm_ln_g, m_od_cm_ln_b, m_od_w_out, m_norm_mix_pre, m_norm_mix_post, m_norm_mlp_pre, m_norm_mlp_post, m_norm_ple, m_mlp_w1, m_mlp_w2, m_ple_w_proj, m_ple_w_gate]
    vs = [v_ev_w_in, v_ev_lru_conv_w, v_ev_lru_conv_b, v_ev_lru_gate_a_w, v_ev_lru_gate_a_b, v_ev_lru_gate_x_w, v_ev_lru_gate_x_b, v_ev_lru_lambda, v_ev_w_out, v_od_w_in, v_od_ssm_conv_w, v_od_ssm_conv_b, v_od_ssm_dt_bias, v_od_ssm_a_log, v_od_ssm_d, v_od_ssm_norm, v_od_cm_conv_w, v_od_cm_conv_b, v_od_cm_ln_g, v_od_cm_ln_b, v_od_w_out, v_norm_mix_pre, v_norm_mix_post, v_norm_mlp_pre, v_norm_mlp_post, v_norm_ple, v_mlp_w1, v_mlp_w2, v_ple_w_proj, v_ple_w_gate]
    w = dict(zip(W_NAMES, ws))
    m = dict(zip(W_NAMES, ms))
    v = dict(zip(W_NAMES, vs))
    full, wl0, wtl0 = _gather_early(w)
    comm = Overlap(w)
    loss_local, grad_x, grads, layer_grads = local_step(x, p, loss_target, full, [wl0, None], [wtl0, None], comm)
    loss = lax.psum(loss_local, ("x", "y", "c"))
    return (loss, grad_x, *_reduce_and_update(grads, layer_grads, w, m, v, comm))


def _reduce_and_update(grads, layer_grads, w, m, v, comm):
    mx, my, mc = _mesh_pos()

    entries = [(i, key) for i in range(len(layer_grads)) for key in LAYER_MATRICES]
    left = [e for e in entries if e not in comm.from_chips]
    parts = [_to_shards(layer_grads[i][key], _layer_entry(i, key)[2]) for i, key in left]
    small = jnp.concatenate([_to_shards(grads[k], axis).reshape(N_DEV, -1) for k, axis in SMALL_SHARDED.items()],
                            axis=1)
    small_rows = _pack_small(w).shape[0]
    small = jnp.pad(small, ((0, 0), (0, small_rows * LANES - small.shape[1]))).reshape(N_DEV, small_rows, LANES)
    parts.append(small)
    from_sibling = scatter_to_sibling(parts, name="scatter_sibling")
    core = jnp.reshape(mc, (1,)).astype(jnp.int32)
    sums = [add_sibling_parts(a, b, core, name=f"add_sibling_{i}") for i, (a, b) in enumerate(zip(parts, from_sibling))]
    from_chips = scatter_to_chips([s[1] for s in sums], name="scatter_chips")
    all_sums = {**comm.sums, **dict(zip(left, sums[:-1]))}
    all_from_chips = {**comm.from_chips, **dict(zip(left, from_chips[:-1]))}
    chip = jnp.reshape(2 * mx + my, (1,)).astype(jnp.int32)
    per_layer = []
    for i in range(len(layer_grads)):
        ws, ms, vs = _layer_shards(w, i), _layer_shards(m, i), _layer_shards(v, i)
        per_layer.append({key: adamw_sharded(all_sums[i, key][0], all_from_chips[i, key], chip, ws[key], ms[key],
                                             vs[key], name=f"adamw_l{i}_{key}") for key in LAYER_MATRICES})
    g_sh, d_sh, m_sh, v_sh = {}, {}, {}, {}
    for which, tree in enumerate((g_sh, d_sh, m_sh, v_sh)):
        for i in range(len(per_layer)):
            for key in LAYER_MATRICES:
                name, idx, _ = _layer_entry(i, key)
                tree.setdefault(name, {})[idx] = per_layer[i][key][which]
        for name in BIG_SHARDED:
            tree[name] = jnp.stack([tree[name][idx] for idx in sorted(tree[name])], axis=0)
    outs = adamw_sharded(sums[-1][0], from_chips[-1], chip, _pack_small(w), _pack_small(m), _pack_small(v),
                         name="adamw_small")
    small_shapes = {k: w[k].shape for k in SMALL_SHARDED}
    for tree, o in zip((g_sh, d_sh, m_sh, v_sh), outs):
        tree.update(_unpack(o.reshape(-1), small_shapes))

    rep_parts, = all_gather([_pack_replicated(grads)], name="gather_replicated_grads")
    outs = adamw_replicated(rep_parts, _pack_replicated(w), _pack_replicated(m), _pack_replicated(v),
                            name="adamw_replicated")
    rep_shapes = {k: w[k].shape for k in REPLICATED}
    g_rp, d_rp, m_rp, v_rp = [_unpack(o.reshape(-1), rep_shapes) for o in outs]

    pick = lambda sh, rp: [sh[k] if k in SHARDED else rp[k] for k in W_NAMES]
    return [*pick(g_sh, g_rp), *pick(d_sh, d_rp), *pick(m_sh, m_rp), *pick(v_sh, v_rp)]
```

```python
import math

import jax
import jax.numpy as jnp
from jax import lax
from jax.experimental import pallas as pl
from jax.experimental.pallas import tpu as pltpu

f32 = jnp.float32
MXU_DTYPE = jnp.bfloat16

EPS = 1e-6
LRU_WIDTH = 512
LRU_HEADS = 8
LRU_CONV = 4
LRU_C = 8.0
SB_WIDTH = 512
SB_HEAD_DIM = 64
SSM_WIDTH = 1024
SSM_HEADS = 16
SSM_HEAD_DIM = 64
SSM_GROUPS = 2
SSM_STATE = 128
SSM_CONV = 4
SSM_CHUNK = 128
SSM_XBC = SSM_WIDTH + 2 * SSM_GROUPS * SSM_STATE
CONF_WIDTH = 512
CONF_KERNEL = 31
LANES = 128
N_DEV = 8

ADAM_LR = 0.001
ADAM_B1 = 0.9
ADAM_B2 = 0.999
ADAM_EPS = 1e-08
ADAM_WD = 0.01
ADAM_STEP = 10

VMEM_LIMIT = 56 * 1024 * 1024


def _cparams(*sem):
    return pltpu.CompilerParams(dimension_semantics=sem, vmem_limit_bytes=VMEM_LIMIT)


def _mm(a, b):
    return jnp.dot(a.astype(MXU_DTYPE), b.astype(MXU_DTYPE), preferred_element_type=f32)


def _mm_nt(a, b):
    return lax.dot_general(a.astype(MXU_DTYPE), b.astype(MXU_DTYPE), (((1,), (1,)), ((), ())),
                           preferred_element_type=f32)


def _mm_tn(a, b):
    return lax.dot_general(a.astype(MXU_DTYPE), b.astype(MXU_DTYPE), (((0,), (0,)), ((), ())),
                           preferred_element_type=f32)


def _mm_exact(a, b):
    return jnp.dot(a, b, preferred_element_type=f32, precision=lax.Precision.HIGHEST)


@jax.custom_vjp
def dmm(a, b):
    return _mm(a, b)


def _dmm_fwd(a, b):
    return _mm(a, b), (a, b)


def _dmm_bwd(res, g):
    a, b = res
    return _mm_nt(g, b), _mm_tn(a, g)


dmm.defvjp(_dmm_fwd, _dmm_bwd)


@jax.custom_vjp
def dmm_nt(a, b):
    return _mm_nt(a, b)


def _dmm_nt_fwd(a, b):
    return _mm_nt(a, b), (a, b)


def _dmm_nt_bwd(res, g):
    a, b = res
    return _mm(g, b), _mm_tn(g, a)


dmm_nt.defvjp(_dmm_nt_fwd, _dmm_nt_bwd)


@jax.custom_vjp
def dmm_tn(a, b):
    return _mm_tn(a, b)


def _dmm_tn_fwd(a, b):
    return _mm_tn(a, b), (a, b)


def _dmm_tn_bwd(res, g):
    a, b = res
    return _mm_nt(b, g), _mm(a, g)


dmm_tn.defvjp(_dmm_tn_fwd, _dmm_tn_bwd)


def _rms(x, g):
    r = lax.rsqrt(jnp.mean(x * x, axis=-1, keepdims=True) + EPS)
    return x * r * g


def _rms_bwd(dy, x, g):
    r = lax.rsqrt(jnp.mean(x * x, axis=-1, keepdims=True) + EPS)
    dyg = dy * g
    dx = r * dyg - x * (r * r * r * jnp.mean(dyg * x, axis=-1, keepdims=True))
    return dx, dy * x * r


def _tok(tm, n):
    return pl.BlockSpec((tm, n), lambda i: (i, 0))


def _whole(shape):
    nd = len(shape)
    return pl.BlockSpec(tuple(shape), lambda i: (0,) * nd)


def _acc_rows(ref, val):
    s = jnp.sum(val, axis=0, keepdims=True)

    @pl.when(pl.program_id(0) == 0)
    def _():
        ref[...] = s

    @pl.when(pl.program_id(0) != 0)
    def _():
        ref[...] += s


TOKEN_TILE = 512
WEIGHT_GRAD_TOKENS = 1024


class Part:
    def __init__(self, whole, start, size, axis):
        self.whole, self.start, self.size, self.axis = whole, start, size, axis
        self.shape = tuple(size if a == axis else n for a, n in enumerate(whole.shape))


def _weights(ws):
    wholes, readers = [], []
    for w in ws:
        arr = w.whole if isinstance(w, Part) else w
        idx = next((i for i, a in enumerate(wholes) if a is arr), None)
        if idx is None:
            wholes.append(arr)
            idx = len(wholes) - 1
        if isinstance(w, Part):
            rows = pl.ds(w.start, w.size) if w.axis == 0 else slice(None)
            cols = pl.ds(w.start, w.size) if w.axis == 1 else slice(None)
            readers.append(lambda refs, idx=idx, rows=rows, cols=cols: refs[idx][rows, cols])
        else:
            readers.append(lambda refs, idx=idx: refs[idx][...])
    return wholes, readers


def norm_matmul(h, g, ws, out_dtypes, name):
    t, d = h.shape
    tm = TOKEN_TILE
    wholes, readers = _weights(ws)
    nw = len(wholes)

    def body(h_ref, g_ref, *refs):
        hn = _rms(h_ref[...], g_ref[...]).astype(MXU_DTYPE)
        for read, o_ref in zip(readers, refs[nw:]):
            o_ref[...] = jnp.dot(hn, read(refs[:nw]), preferred_element_type=f32).astype(o_ref.dtype)

    return pl.pallas_call(
        body, name=name, grid=(t // tm,),
        in_specs=[_tok(tm, d), _whole(g.shape)] + [_whole(w.shape) for w in wholes],
        out_specs=[_tok(tm, w.shape[1]) for w in ws],
        out_shape=[jax.ShapeDtypeStruct((t, w.shape[1]), dt) for w, dt in zip(ws, out_dtypes)],
        compiler_params=_cparams("parallel"),
    )(h, g, *wholes)


def matmul_residual_norm(xs, ws, h, g, name, relu2=False):
    t, d = h.shape
    tm = TOKEN_TILE
    nx = len(xs)
    wholes, readers = _weights(ws)
    nw = len(wholes)

    def body(*refs):
        x_refs, w_refs = refs[:nx], refs[nx:nx + nw]
        h_ref, g_ref, ho_ref, m_ref = refs[nx + nw:]
        m = None
        for x_ref, read in zip(x_refs, readers):
            x = x_ref[...]
            if relu2:
                x = jnp.square(jnp.maximum(x.astype(f32), 0.0))
            part = jnp.dot(x.astype(MXU_DTYPE), read(w_refs), preferred_element_type=f32)
            m = part if m is None else m + part
        m_ref[...] = m.astype(m_ref.dtype)
        ho_ref[...] = h_ref[...] + _rms(m, g_ref[...])

    return pl.pallas_call(
        body, name=name, grid=(t // tm,),
        in_specs=[_tok(tm, x.shape[1]) for x in xs] + [_whole(w.shape) for w in wholes]
        + [_tok(tm, d), _whole(g.shape)],
        out_specs=[_tok(tm, d), _tok(tm, d)],
        out_shape=[jax.ShapeDtypeStruct((t, d), f32), jax.ShapeDtypeStruct((t, d), MXU_DTYPE)],
        compiler_params=_cparams("parallel"),
    )(*xs, *wholes, h, g)


def ple_forward(h, p, w_gate, w_proj, g, name, target=None):
    t, d = h.shape
    tm = TOKEN_TILE
    last = target is not None

    def body(h_ref, p_ref, wg_ref, wp_ref, g_ref, *refs):
        hh = h_ref[...]
        gl = jnp.dot(hh.astype(MXU_DTYPE), wg_ref[...], preferred_element_type=f32)
        emb = jnp.dot(p_ref[...].astype(MXU_DTYPE), wp_ref[...], preferred_element_type=f32)
        y = hh + _rms(jax.nn.sigmoid(gl) * emb, g_ref[...])
        if last:
            t_ref, l_ref, dy_ref, gl_ref, emb_ref = refs
            e = y - t_ref[...]
            dy_ref[...] = e * (1.0 / d)
            part = jnp.sum(jnp.sum(e * e, axis=1, keepdims=True), axis=0, keepdims=True) * (0.5 / d)
            _acc_rows(l_ref, jnp.broadcast_to(part, (1, LANES)))
        else:
            y_ref, gl_ref, emb_ref = refs
            y_ref[...] = y
        gl_ref[...] = gl.astype(gl_ref.dtype)
        emb_ref[...] = emb.astype(emb_ref.dtype)

    saved = [jax.ShapeDtypeStruct((t, d), MXU_DTYPE)] * 2
    in_specs = [_tok(tm, d), _tok(tm, p.shape[1]), _whole(w_gate.shape), _whole(w_proj.shape), _whole(g.shape)]
    if last:
        return pl.pallas_call(
            body, name=name, grid=(t // tm,),
            in_specs=in_specs + [_tok(tm, d)],
            out_specs=[_whole((1, LANES))] + [_tok(tm, d)] * 3,
            out_shape=[jax.ShapeDtypeStruct((1, LANES), f32), jax.ShapeDtypeStruct((t, d), f32)] + saved,
            compiler_params=_cparams("arbitrary"),
        )(h, p, w_gate, w_proj, g, target)
    return pl.pallas_call(
        body, name=name, grid=(t // tm,),
        in_specs=in_specs,
        out_specs=[_tok(tm, d)] * 3,
        out_shape=[jax.ShapeDtypeStruct((t, d), f32)] + saved,
        compiler_params=_cparams("parallel"),
    )(h, p, w_gate, w_proj, g)


def bwd_through_norm_in(dh, gs, wts, h, g, name):
    t, d = h.shape
    tm = TOKEN_TILE
    ng = len(gs)
    wholes, readers = _weights(wts)
    nw = len(wholes)

    def body(*refs):
        dh_ref = refs[0]
        g_refs, w_refs = refs[1:1 + ng], refs[1 + ng:1 + ng + nw]
        h_ref, gain_ref, dho_ref, dg_ref = refs[1 + ng + nw:]
        dhn = None
        for g_ref, read in zip(g_refs, readers):
            part = jnp.dot(g_ref[...].astype(MXU_DTYPE), read(w_refs), preferred_element_type=f32)
            dhn = part if dhn is None else dhn + part
        dx, dgr = _rms_bwd(dhn, h_ref[...], gain_ref[...])
        dho_ref[...] = dh_ref[...] + dx
        _acc_rows(dg_ref, dgr)

    return pl.pallas_call(
        body, name=name, grid=(t // tm,),
        in_specs=[_tok(tm, d)] + [_tok(tm, x.shape[1]) for x in gs] + [_whole(w.shape) for w in wholes]
        + [_tok(tm, d), _whole(g.shape)],
        out_specs=[_tok(tm, d), _whole((1, d))],
        out_shape=[jax.ShapeDtypeStruct((t, d), f32), jax.ShapeDtypeStruct((1, d), f32)],
        compiler_params=_cparams("arbitrary"),
    )(dh, *gs, *wholes, h, g)


def bwd_through_norm_out(dh, n, g, wts, out_dtypes, name, relu2_of=None, rider=None):
    t, d = n.shape
    tm = TOKEN_TILE
    nw = len(wts)
    wholes, readers = _weights(wts)
    nwh = len(wholes)
    has_a = relu2_of is not None

    def body(*refs):
        dh_ref, n_ref, gain_ref = refs[:3]
        w_refs = refs[3:3 + nwh]
        rest = refs[3 + nwh:]
        if has_a:
            a_ref, rest = rest[0], rest[1:]
        dn_ref, dx_refs, dg_ref = rest[0], rest[1:1 + nw], rest[1 + nw]
        dn, dgr = _rms_bwd(dh_ref[...], n_ref[...].astype(f32), gain_ref[...])
        dnb = dn.astype(MXU_DTYPE)
        dn_ref[...] = dnb.astype(dn_ref.dtype)
        for read, dx_ref in zip(readers, dx_refs):
            dx = jnp.dot(dnb, read(w_refs), preferred_element_type=f32)
            if has_a:
                dx = dx * (2.0 * jnp.maximum(a_ref[...].astype(f32), 0.0))
            dx_ref[...] = dx.astype(dx_ref.dtype)
        _acc_rows(dg_ref, dgr)

    ins = [dh, n, g, *wholes] + ([relu2_of] if has_a else [])
    in_specs = [_tok(tm, d), _tok(tm, d), _whole(g.shape)] + [_whole(w.shape) for w in wholes]
    if has_a:
        in_specs.append(_tok(tm, relu2_of.shape[1]))
    grid = (t // tm,)
    body, in_specs, out_specs, out_shape, scratch = _ride(
        rider, body, in_specs, [_tok(tm, d)] + [_tok(tm, w.shape[1]) for w in wts] + [_whole((1, d))],
        [jax.ShapeDtypeStruct((t, d), MXU_DTYPE)]
        + [jax.ShapeDtypeStruct((t, w.shape[1]), dt) for w, dt in zip(wts, out_dtypes)]
        + [jax.ShapeDtypeStruct((1, d), f32)], [], grid)
    outs = pl.pallas_call(
        body, name=name, grid=grid, in_specs=in_specs, out_specs=out_specs, out_shape=out_shape,
        scratch_shapes=scratch, compiler_params=_cparams("arbitrary"),
    )(*ins, *(rider.inputs if rider else []))
    if rider:
        return outs[0], list(outs[1:1 + nw]), outs[1 + nw], list(outs[2 + nw:])
    return outs[0], list(outs[1:1 + nw]), outs[1 + nw]


def ple_backward(dh3, h2, gl, emb, g, w_gate_t, name):
    t, d = h2.shape
    tm = TOKEN_TILE

    def body(dh_ref, gl_ref, emb_ref, gain_ref, wt_ref, dho_ref, dgl_ref, demb_ref, dg_ref):
        gate = jax.nn.sigmoid(gl_ref[...].astype(f32))
        emb = emb_ref[...].astype(f32)
        dge, dgr = _rms_bwd(dh_ref[...], gate * emb, gain_ref[...])
        demb_ref[...] = (dge * gate).astype(demb_ref.dtype)
        dgl = (dge * emb * gate * (1.0 - gate)).astype(MXU_DTYPE)
        dgl_ref[...] = dgl.astype(dgl_ref.dtype)
        dho_ref[...] = dh_ref[...] + jnp.dot(dgl, wt_ref[...], preferred_element_type=f32)
        _acc_rows(dg_ref, dgr)

    return pl.pallas_call(
        body, name=name, grid=(t // tm,),
        in_specs=[_tok(tm, d), _tok(tm, d), _tok(tm, d), _whole(g.shape), _whole(w_gate_t.shape)],
        out_specs=[_tok(tm, d), _tok(tm, d), _tok(tm, d), _whole((1, d))],
        out_shape=[jax.ShapeDtypeStruct((t, d), f32), jax.ShapeDtypeStruct((t, d), MXU_DTYPE),
                   jax.ShapeDtypeStruct((t, d), MXU_DTYPE), jax.ShapeDtypeStruct((1, d), f32)],
        compiler_params=_cparams("arbitrary"),
    )(dh3, gl, emb, g, w_gate_t)


def _largest_tile(n, cap):
    if n <= cap:
        return n
    return max(c for c in range(LANES, cap + 1, LANES) if n % c == 0)


def weight_grad(x, gout, name, prologue="none", gain=None):
    t, k = x.shape
    n = gout.shape[1]
    tt = WEIGHT_GRAD_TOKENS
    tn = _largest_tile(n, 1024)
    tk = k if prologue == "rms" else _largest_tile(k, 1024)
    has_gain = prologue == "rms"

    def body(*refs):
        if has_gain:
            x_ref, gain_ref, g_ref, o_ref = refs
        else:
            x_ref, g_ref, o_ref = refs
        x = x_ref[...].astype(f32)
        if prologue == "relu2":
            x = jnp.square(jnp.maximum(x, 0.0))
        elif prologue == "rms":
            x = _rms(x, gain_ref[...])
        part = _mm_tn(x, g_ref[...])

        @pl.when(pl.program_id(2) == 0)
        def _():
            o_ref[...] = part

        @pl.when(pl.program_id(2) != 0)
        def _():
            o_ref[...] += part

    in_specs = [pl.BlockSpec((tt, tk), lambda i, j, s: (s, i))]
    ins = [x]
    if has_gain:
        in_specs.append(pl.BlockSpec(gain.shape, lambda i, j, s: (0, 0)))
        ins.append(gain)
    in_specs.append(pl.BlockSpec((tt, tn), lambda i, j, s: (s, j)))
    ins.append(gout)
    return pl.pallas_call(
        body, name=name, grid=(k // tk, n // tn, t // tt),
        in_specs=in_specs,
        out_specs=pl.BlockSpec((tk, tn), lambda i, j, s: (i, j)),
        out_shape=jax.ShapeDtypeStruct((k, n), f32),
        compiler_params=_cparams("parallel", "parallel", "arbitrary"),
    )(*ins)


def weight_grads_of_norm(x, gain, gouts, name):
    t, k = x.shape
    tt = TOKEN_TILE
    ng = len(gouts)

    def body(x_ref, gain_ref, *refs):
        xn = _rms(x_ref[...], gain_ref[...]).astype(MXU_DTYPE)
        first = pl.program_id(0) == 0
        for g_ref, o_ref in zip(refs[:ng], refs[ng:]):
            _accum(o_ref, _mm_tn(xn, g_ref[...]), first)

    return pl.pallas_call(
        body, name=name, grid=(t // tt,),
        in_specs=[_tok(tt, k), _whole(gain.shape)] + [_tok(tt, g.shape[1]) for g in gouts],
        out_specs=[_whole((k, g.shape[1])) for g in gouts],
        out_shape=[jax.ShapeDtypeStruct((k, g.shape[1]), f32) for g in gouts],
        compiler_params=_cparams("arbitrary"),
    )(x, gain, *gouts)


SEQ_TILE = 256
HALO = 8


def _first_step():
    return jnp.logical_and(pl.program_id(0) == 0, pl.program_id(1) == 0)


def _accum(ref, val, first):
    @pl.when(first)
    def _():
        ref[...] = val

    @pl.when(jnp.logical_not(first))
    def _():
        ref[...] += val


def _softplus(x):
    return jnp.maximum(x, 0.0) + jnp.log1p(jnp.exp(-jnp.abs(x)))


def _neg_expm1(z):
    series = -z * (1.0 + z * (0.5 + z * (1.0 / 6.0 + z * (1.0 / 24.0 + z * (1.0 / 120.0)))))
    return jnp.where(z > -0.05, series, 1.0 - jnp.exp(z))


def _lru_gates(xc, ga, gab, gx, gxb, lam):
    r = jax.nn.sigmoid(dmm(xc, ga) + gab)
    i = jax.nn.sigmoid(dmm(xc, gx) + gxb)
    log_a = -LRU_C * r * _softplus(-lam)
    a = jnp.exp(log_a)
    u = jnp.sqrt(_neg_expm1(2.0 * log_a)) * (i * xc)
    return a, u


def _scan_down(a, u):
    n = a.shape[0]
    rows = lax.broadcasted_iota(jnp.int32, a.shape, 0)
    d = 1
    while d < n:
        keep = rows >= d
        a_s = jnp.where(keep, pltpu.roll(a, d, 0), 1.0)
        u_s = jnp.where(keep, pltpu.roll(u, d, 0), 0.0)
        u = a * u_s + u
        a = a * a_s
        d *= 2
    return a, u


def _scan_up(b, g):
    n = b.shape[0]
    rows = lax.broadcasted_iota(jnp.int32, b.shape, 0)
    d = 1
    while d < n:
        keep = rows < n - d
        b_s = jnp.where(keep, pltpu.roll(b, n - d, 0), 1.0)
        g_s = jnp.where(keep, pltpu.roll(g, n - d, 0), 0.0)
        g = g + b * g_s
        b = b * b_s
        d *= 2
    return g


def _seq_specs(ts, c, nt, reverse=False):
    per = ts // HALO

    def jj(j):
        return (nt - 1 - j) if reverse else j

    tile = pl.BlockSpec((1, ts, c), lambda b, j: (b, jj(j), 0))
    before = pl.BlockSpec((1, HALO, c), lambda b, j: (b, jnp.maximum(jj(j) * per - 1, 0), 0))
    after = pl.BlockSpec((1, HALO, c), lambda b, j: (b, jnp.minimum((jj(j) + 1) * per, nt * per - 1), 0))
    return tile, before, after


def _const2(shape):
    nd = len(shape)
    return pl.BlockSpec(tuple(shape), lambda b, j: (0,) * nd)


def lru_forward(xpre, gate, cw, cb, ga, gab, gx, gxb, lam, name):
    nb, ns, w = xpre.shape
    ts = SEQ_TILE
    nt = ns // ts
    tile, _, _ = _seq_specs(ts, w, nt)

    def body(xp_ref, gt_ref, cw_ref, cb_ref, ga_ref, gab_ref, gx_ref, gxb_ref, lam_ref,
             y_ref, xc_ref, hs_ref, xin, hcar):
        @pl.when(pl.program_id(1) == 0)
        def _():
            xin[0:HALO, :] = jnp.zeros((HALO, w), f32)
            hcar[...] = jnp.zeros_like(hcar)

        xin[HALO:HALO + ts, :] = xp_ref[0]
        xc = jnp.broadcast_to(cb_ref[...], (ts, w))
        for k in range(LRU_CONV):
            xc = xc + cw_ref[k:k + 1, :] * xin[pl.ds(HALO - LRU_CONV + 1 + k, ts), :]
        xin[0:HALO, :] = xin[ts:ts + HALO, :]
        a, u = _lru_gates(xc, ga_ref[...], gab_ref[...], gx_ref[...], gxb_ref[...], lam_ref[...])
        acum, h = _scan_down(a, u)
        h = h + acum * hcar[0:1, :]
        hcar[0:1, :] = h[ts - 1:ts, :]
        xc_ref[0] = xc
        hs_ref[0] = h
        y_ref[0] = (h * jax.nn.gelu(gt_ref[0])).astype(y_ref.dtype)

    params = [cw, cb, ga, gab, gx, gxb, lam]
    return pl.pallas_call(
        body, name=name, grid=(nb, nt),
        in_specs=[tile, tile] + [_const2(p.shape) for p in params],
        out_specs=[tile, tile, tile],
        out_shape=[jax.ShapeDtypeStruct((nb, ns, w), MXU_DTYPE), jax.ShapeDtypeStruct((nb, ns, w), f32),
                   jax.ShapeDtypeStruct((nb, ns, w), f32)],
        scratch_shapes=[pltpu.VMEM((ts + HALO, w), f32), pltpu.VMEM((HALO, w), f32)],
        compiler_params=_cparams("arbitrary", "arbitrary"),
    )(xpre, gate, *params)


def lru_backward(dy, xpre, gate, xc, hs, cw, cb, ga, gab, gx, gxb, lam, name, rider=None):
    nb, ns, w = xpre.shape
    ts = SEQ_TILE
    nt = ns // ts
    tile, before, _ = _seq_specs(ts, w, nt, reverse=True)

    def body(dy_ref, xp_ref, xpb_ref, gt_ref, xc_ref, hs_ref, hsb_ref,
             cw_ref, cb_ref, ga_ref, gab_ref, gx_ref, gxb_ref, lam_ref,
             dxp_ref, dgt_ref, dcw_ref, dcb_ref, dga_ref, dgab_ref, dgx_ref, dgxb_ref, dlam_ref,
             dxc_ext, gcar, xin):
        j = pl.program_id(1)
        first = _first_step()
        at_seq_start = j == nt - 1

        @pl.when(j == 0)
        def _():
            dxc_ext[ts:ts + HALO, :] = jnp.zeros((HALO, w), f32)
            gcar[...] = jnp.zeros_like(gcar)

        gt = gt_ref[0]
        h = hs_ref[0]
        dyv = dy_ref[0].astype(f32)
        gl, gelu_vjp = jax.vjp(jax.nn.gelu, gt)
        dgt_ref[0] = gelu_vjp(dyv * h)[0].astype(dgt_ref.dtype)
        dh = dyv * gl

        (a, _), gates_vjp = jax.vjp(_lru_gates, xc_ref[0], ga_ref[...], gab_ref[...], gx_ref[...], gxb_ref[...],
                                    lam_ref[...])
        rows = lax.broadcasted_iota(jnp.int32, (ts, w), 0)
        dh = dh + jnp.where(rows == ts - 1, gcar[0:1, :], 0.0)
        b = pltpu.roll(a, ts - 1, 0)
        g = _scan_up(b, dh)
        gcar[0:1, :] = a[0:1, :] * g[0:1, :]
        hprev_row = jnp.where(at_seq_start, 0.0, hsb_ref[0][HALO - 1:HALO, :])
        hprev = jnp.where(rows == 0, hprev_row, pltpu.roll(h, 1, 0))
        dxc, dga, dgab, dgx, dgxb, dlam = gates_vjp((g * hprev, g))

        _accum(dga_ref, dga, first)
        _accum(dgx_ref, dgx, first)
        _accum(dgab_ref, dgab, first)
        _accum(dgxb_ref, dgxb, first)
        _accum(dlam_ref, dlam, first)
        _accum(dcb_ref, jnp.sum(dxc, axis=0, keepdims=True), first)

        dxc_ext[0:ts, :] = dxc
        dxp = jnp.zeros((ts, w), f32)
        for k in range(LRU_CONV):
            dxp = dxp + cw_ref[k:k + 1, :] * dxc_ext[pl.ds(LRU_CONV - 1 - k, ts), :]
        dxp_ref[0] = dxp.astype(dxp_ref.dtype)
        dxc_ext[ts:ts + HALO, :] = dxc[0:HALO, :]

        xin[0:HALO, :] = jnp.where(at_seq_start, 0.0, xpb_ref[0])
        xin[HALO:HALO + ts, :] = xp_ref[0]
        dcw_rows = [jnp.sum(dxc * xin[pl.ds(HALO - LRU_CONV + 1 + k, ts), :], axis=0, keepdims=True)
                    for k in range(LRU_CONV)]
        dcw_rows += [jnp.zeros((1, w), f32)] * (HALO - LRU_CONV)
        _accum(dcw_ref, jnp.concatenate(dcw_rows, axis=0), first)

    params = [cw, cb, ga, gab, gx, gxb, lam]
    pshape = lambda p: jax.ShapeDtypeStruct(p.shape, f32)
    grid = (nb, nt)
    n_main = 3 + len(params) - 1
    body, in_specs, out_specs, out_shape, scratch = _ride(
        rider, body, [tile, tile, before, tile, tile, tile, before] + [_const2(p.shape) for p in params],
        [tile, tile, _const2((HALO, w))] + [_const2(p.shape) for p in params[1:]],
        [jax.ShapeDtypeStruct((nb, ns, w), MXU_DTYPE), jax.ShapeDtypeStruct((nb, ns, w), MXU_DTYPE),
         jax.ShapeDtypeStruct((HALO, w), f32)] + [pshape(p) for p in params[1:]],
        [pltpu.VMEM((ts + HALO, w), f32), pltpu.VMEM((HALO, w), f32), pltpu.VMEM((ts + HALO, w), f32)], grid)
    outs = pl.pallas_call(
        body, name=name, grid=grid, in_specs=in_specs, out_specs=out_specs, out_shape=out_shape,
        scratch_shapes=scratch, compiler_params=_cparams("arbitrary", "arbitrary"),
    )(dy, xpre, xpre, gate, xc, hs, hs, *params, *(rider.inputs if rider else []))
    return (list(outs[:n_main]), list(outs[n_main:])) if rider else outs


SB_TILE = 256


def _split_dot(x, m):
    hi = x.astype(MXU_DTYPE)
    lo = (x - hi.astype(f32)).astype(MXU_DTYPE)
    return jnp.dot(hi, m, preferred_element_type=f32) + jnp.dot(lo, m, preferred_element_type=f32)


def _suffix_matrices(n):
    r = lax.broadcasted_iota(jnp.int32, (n, n), 0)
    c = lax.broadcasted_iota(jnp.int32, (n, n), 1)
    return (r > c).astype(MXU_DTYPE), (r >= c).astype(MXU_DTYPE)


LOG2E = 1.4426950408889634


def _sb_logits(qh, kb, strict):
    z = _mm_nt(qh, kb)
    ls = jnp.minimum(z, 0.0) - jnp.log2(1.0 + jnp.exp2(-jnp.abs(z)))
    lk = ls - z
    if strict is not None:
        lk = jnp.where(strict, lk, 0.0)
    return ls, lk


def _head_masked(x, dtype):
    lane = lax.broadcasted_iota(jnp.int32, x.shape, 1)
    return (jnp.where(lane < SB_HEAD_DIM, x, 0.0).astype(dtype), jnp.where(lane >= SB_HEAD_DIM, x, 0.0).astype(dtype))


def _stack_heads(dst, x, tq):
    x0, x1 = _head_masked(x, dst.dtype)
    for blk in range(dst.shape[0]):
        dst[blk, 0:tq, :] = x0[blk * tq:(blk + 1) * tq]
        dst[blk, tq:2 * tq, :] = x1[blk * tq:(blk + 1) * tq]


def _strict_mask(tq):
    rr = lax.broadcasted_iota(jnp.int32, (2 * tq, tq), 0)
    cc = lax.broadcasted_iota(jnp.int32, (2 * tq, tq), 1)
    return cc < jnp.where(rr >= tq, rr - tq, rr)


def _sb_specs(ns):
    npair = SB_WIDTH // LANES
    q = pl.BlockSpec((1, ns, LANES), lambda b, p: (b, 0, p))
    k = pl.BlockSpec((1, ns, LANES), lambda b, p: (b, 0, npair + p))
    v = pl.BlockSpec((1, ns, LANES), lambda b, p: (b, 0, 2 * npair + p))
    return q, k, v, npair


def sb_forward(qkv, name, rider=None):
    nb, ns, _ = qkv.shape
    tq = SB_TILE
    nq = ns // tq
    qspec, kspec, vspec, npair = _sb_specs(ns)

    def body(q_ref, k_ref, v_ref, o_ref, qs, ks, vs, acc):
        scale = 1.0 / math.sqrt(SB_HEAD_DIM)
        _stack_heads(qs, q_ref[0] * (scale * LOG2E), tq)
        ks[...] = k_ref[0].astype(MXU_DTYPE)
        _stack_heads(vs, v_ref[0], tq)
        mx, _ = _suffix_matrices(tq)
        strict = _strict_mask(tq)

        def step(q2, blks, r2, masked):
            kbs = [ks[pl.ds(pl.multiple_of(b * tq, tq), tq), :] for b in blks]
            lg = [_sb_logits(q2, kb, strict if masked else None) for kb in kbs]
            sums = [jnp.dot(lk.astype(MXU_DTYPE), mx, preferred_element_type=f32) for _, lk in lg]
            total = None
            for (ls, lk), s, b in zip(lg, sums, blks):
                a = r2 + s
                w = jnp.exp2(ls + a)
                if masked:
                    w = jnp.where(strict, w, 0.0)
                wb = w.astype(MXU_DTYPE)
                part = (jnp.dot(wb[:tq], vs[b, 0:tq, :], preferred_element_type=f32)
                        + jnp.dot(wb[tq:], vs[b, tq:2 * tq, :], preferred_element_type=f32))
                total = part if total is None else total + part
                r2 = a[:, 0:1] + lk[:, 0:1]
            acc[...] += total
            return r2

        def q_block(qi, carry):
            acc[...] = jnp.zeros_like(acc)
            q2 = qs[qi]
            r2 = step(q2, [qi], jnp.zeros((2 * tq, 1), f32), True)
            r2 = lax.fori_loop(0, lax.shift_right_logical(qi, 2),
                               lambda i, r: step(q2, [qi - 1 - 4 * i - u for u in range(4)], r, False), r2)
            r2 = lax.cond(jnp.bitwise_and(qi, 2) == 2,
                          lambda r: step(q2, [jnp.bitwise_and(qi, 3) - 1, jnp.bitwise_and(qi, 3) - 2], r, False),
                          lambda r: r, r2)
            lax.cond(jnp.bitwise_and(qi, 1) == 1, lambda r: step(q2, [0], r, False), lambda r: r, r2)
            o_ref[0, pl.ds(pl.multiple_of(qi * tq, tq), tq), :] = acc[...]
            return carry

        lax.fori_loop(0, nq, q_block, 0)

    grid = (nb, npair)
    body, in_specs, out_specs, out_shape, scratch = _ride(
        rider, body, [qspec, kspec, vspec], [pl.BlockSpec((1, ns, LANES), lambda b, p: (b, 0, p))],
        [jax.ShapeDtypeStruct((nb, ns, SB_WIDTH), f32)],
        [pltpu.VMEM((nq, 2 * tq, LANES), MXU_DTYPE), pltpu.VMEM((ns, LANES), MXU_DTYPE),
         pltpu.VMEM((nq, 2 * tq, LANES), MXU_DTYPE), pltpu.VMEM((tq, LANES), f32)], grid)
    outs = pl.pallas_call(
        body, name=name, grid=grid, in_specs=in_specs, out_specs=out_specs, out_shape=out_shape,
        scratch_shapes=scratch,
        compiler_params=_cparams("arbitrary", "arbitrary") if rider else _cparams("parallel", "parallel"),
    )(qkv, qkv, qkv, *(rider.inputs if rider else []))
    return (outs[0], list(outs[1:])) if rider else outs[0]


def sb_backward(qkv, o, do, name, rider=None):
    nb, ns, _ = qkv.shape
    tq = SB_TILE
    nq = ns // tq
    qspec, kspec, vspec, npair = _sb_specs(ns)
    ospec = pl.BlockSpec((1, ns, LANES), lambda b, p: (b, 0, p))

    def body(q_ref, k_ref, v_ref, o_ref, do_ref, dq_ref, dk_ref, dv_ref, qs, ks, kcat, vs, dos, dqacc, dkacc, dvacc):
        scale = 1.0 / math.sqrt(SB_HEAD_DIM)
        _stack_heads(qs, q_ref[0] * (scale * LOG2E), tq)
        ks[...] = k_ref[0].astype(MXU_DTYPE)
        _stack_heads(kcat, k_ref[0], tq)
        vs[...] = v_ref[0].astype(MXU_DTYPE)
        _stack_heads(dos, do_ref[0].astype(f32), tq)
        dkacc[...] = jnp.zeros_like(dkacc)
        dvacc[...] = jnp.zeros_like(dvacc)
        mx, mi = _suffix_matrices(tq)
        strict = _strict_mask(tq)

        def step(q2, do2, q2t, do2t, dtot2, blks, carry, masked):
            r2, g2 = carry
            k0s = [pl.multiple_of(b * tq, tq) for b in blks]
            lg = [_sb_logits(q2, ks[pl.ds(k0, tq), :], strict if masked else None) for k0 in k0s]
            dws = [_mm_nt(do2, vs[pl.ds(k0, tq), :]) for k0 in k0s]
            sums = [jnp.dot(lk.astype(MXU_DTYPE), mx, preferred_element_type=f32) for _, lk in lg]
            wbs, es = [], []
            for (ls, lk), s in zip(lg, sums):
                a = r2 + s
                w = jnp.exp2(ls + a)
                if masked:
                    w = jnp.where(strict, w, 0.0)
                wbs.append(w.astype(MXU_DTYPE))
                r2 = a[:, 0:1] + lk[:, 0:1]
            es = [wb.astype(f32) * dw for wb, dw in zip(wbs, dws)]
            esums = [_split_dot(e, mi) for e in es]
            dq = None
            for (ls, _), e, esum, wb, b, k0 in zip(lg, es, esums, wbs, blks, k0s):
                esuf = g2 + esum
                beta = jnp.exp2(ls)
                dz = e - beta * (e + (dtot2 - esuf))
                if masked:
                    dz = jnp.where(strict, dz, 0.0)
                dzb = dz.astype(MXU_DTYPE)
                part = (jnp.dot(dzb[:tq], kcat[b, 0:tq, :], preferred_element_type=f32)
                        + jnp.dot(dzb[tq:], kcat[b, tq:2 * tq, :], preferred_element_type=f32))
                dq = part if dq is None else dq + part
                dkacc[:, pl.ds(k0, tq)] += jnp.dot(q2t, dzb, preferred_element_type=f32)
                dvacc[:, pl.ds(k0, tq)] += jnp.dot(do2t, wb, preferred_element_type=f32)
                g2 = esuf[:, 0:1]
            dqacc[...] += dq
            return r2, g2

        def q_block(qi, carry):
            dqacc[...] = jnp.zeros_like(dqacc)
            q2, do2 = qs[qi], dos[qi]
            q2t, do2t = q2.T, do2.T
            ov = o_ref[0, pl.ds(pl.multiple_of(qi * tq, tq), tq), :]
            dtot2 = jnp.sum(do2.astype(f32) * jnp.concatenate([ov, ov], axis=0), axis=1, keepdims=True)
            zero = jnp.zeros((2 * tq, 1), f32)
            args = (q2, do2, q2t, do2t, dtot2)
            c = step(*args, [qi], (zero, zero), True)
            c = lax.fori_loop(0, lax.shift_right_logical(qi, 2),
                              lambda i, c: step(*args, [qi - 1 - 4 * i - u for u in range(4)], c, False), c)
            c = lax.cond(jnp.bitwise_and(qi, 2) == 2,
                         lambda c: step(*args, [jnp.bitwise_and(qi, 3) - 1, jnp.bitwise_and(qi, 3) - 2], c, False),
                         lambda c: c, c)
            lax.cond(jnp.bitwise_and(qi, 1) == 1, lambda c: step(*args, [0], c, False), lambda c: c, c)
            dq_ref[0, pl.ds(pl.multiple_of(qi * tq, tq), tq), :] = (dqacc[...] * scale).astype(dq_ref.dtype)
            return carry

        lax.fori_loop(0, nq, q_block, 0)
        dk_ref[0] = (dkacc[...].T * (1.0 / LOG2E)).astype(dk_ref.dtype)
        dv_ref[0] = dvacc[...].T.astype(dv_ref.dtype)

    dshape = jax.ShapeDtypeStruct((nb, ns, SB_WIDTH), MXU_DTYPE)
    stacked = pltpu.VMEM((nq, 2 * tq, LANES), MXU_DTYPE)
    flat = pltpu.VMEM((ns, LANES), MXU_DTYPE)
    grid = (nb, npair)
    body, in_specs, out_specs, out_shape, scratch = _ride(
        rider, body, [qspec, kspec, vspec, ospec, ospec], [ospec, ospec, ospec], [dshape, dshape, dshape],
        [stacked, flat, stacked, flat, stacked,
         pltpu.VMEM((tq, LANES), f32), pltpu.VMEM((LANES, ns), f32), pltpu.VMEM((LANES, ns), f32)], grid)
    outs = pl.pallas_call(
        body, name=name, grid=grid, in_specs=in_specs, out_specs=out_specs, out_shape=out_shape,
        scratch_shapes=scratch,
        compiler_params=_cparams("arbitrary", "arbitrary") if rider else _cparams("parallel", "parallel"),
    )(qkv, qkv, qkv, o, do, *(rider.inputs if rider else []))
    return (list(outs[:3]), list(outs[3:])) if rider else list(outs)


SSM_PAIRS = SSM_HEADS // 2
PAIRS_PER_GROUP = SSM_PAIRS // SSM_GROUPS
GROUP_WIDTH = SSM_WIDTH // SSM_GROUPS


def _silu(x):
    return x * jax.nn.sigmoid(x)


def _ssd_chunk(xs_pre, b_pre, c_pre, dt_raw, dt_raw_t, z, st, dt_bias_r, dt_bias_c, a_log_r, a_log_c, d_skip,
               gains):
    n = dt_raw.shape[0]
    rows = lax.broadcasted_iota(jnp.int32, (n, n), 0)
    cols = lax.broadcasted_iota(jnp.int32, (n, n), 1)
    tril = cols <= rows
    tri_l = tril.astype(f32)
    tri_u = (rows <= cols).astype(f32)
    lane = lax.broadcasted_iota(jnp.int32, (n, LANES), 1)
    sub = lax.broadcasted_iota(jnp.int32, (LANES, n), 0)

    dt = _softplus(dt_raw + dt_bias_r)
    a_r = -jnp.exp(a_log_r)
    da = dt * a_r
    acs = _mm_exact(tri_l, da)
    dt_t = _softplus(dt_raw_t + dt_bias_c)
    acs_t = _mm_exact(dt_t * (-jnp.exp(a_log_c)), tri_u)

    bs = [_silu(b) for b in b_pre]
    cs = [_silu(c) for c in c_pre]
    cb = [dmm_nt(cs[g], bs[g]) for g in range(SSM_GROUPS)]

    end = jnp.sum(da, axis=0, keepdims=True)
    lane_row = lax.broadcasted_iota(jnp.int32, (1, LANES), 1)
    first_head = lane < SSM_HEAD_DIM
    first_head_row = lane_row < SSM_HEAD_DIM

    def head_col(v, h):
        return jnp.sum(jnp.where((lane if v.shape[0] == n else lane_row) == h, v, 0.0), axis=1, keepdims=True)

    ys, st_new = [], []
    for p in range(SSM_PAIRS):
        g = p // PAIRS_PER_GROUP
        h0, h1 = 2 * p, 2 * p + 1
        xs = _silu(xs_pre[p])
        acols = [head_col(acs, h0), head_col(acs, h1)]
        dt_p = jnp.where(first_head, head_col(dt, h0), head_col(dt, h1))
        acs_p = jnp.where(first_head, acols[0], acols[1])
        end_p = jnp.where(first_head_row, head_col(end, h0), head_col(end, h1))
        dsk_p = jnp.where(first_head_row, head_col(d_skip, h0), head_col(d_skip, h1))
        xdt = xs * dt_p
        y = jnp.exp(acs_p) * dmm(cs[g], st[p])
        for hh in range(2):
            row = jnp.sum(jnp.where(sub == 2 * p + hh, acs_t, 0.0), axis=0, keepdims=True)
            decay = jnp.where(tril, jnp.exp(jnp.where(tril, acols[hh] - row, 0.0)), 0.0)
            head = first_head if hh == 0 else jnp.logical_not(first_head)
            y = y + dmm(cb[g] * decay, jnp.where(head, xdt, 0.0))
        st_new.append(jnp.exp(end_p) * st[p] + dmm_tn(bs[g], xdt * jnp.exp(end_p - acs_p)))
        ys.append(y + dsk_p * xs)
    out = []
    for g in range(SSM_GROUPS):
        yg = jnp.concatenate(ys[g * PAIRS_PER_GROUP:(g + 1) * PAIRS_PER_GROUP], axis=1) * _silu(z[g])
        out.append(_rms(yg, gains[g]))
    return out, st_new


def _ssd_chunk_inputs(xconv, dtr, z, st_ref, gain):
    xs_pre = [xconv[:, LANES * p:LANES * (p + 1)] for p in range(SSM_PAIRS)]
    b0 = SSM_WIDTH
    c0 = SSM_WIDTH + SSM_GROUPS * SSM_STATE
    b_pre = [xconv[:, b0 + SSM_STATE * g:b0 + SSM_STATE * (g + 1)] for g in range(SSM_GROUPS)]
    c_pre = [xconv[:, c0 + SSM_STATE * g:c0 + SSM_STATE * (g + 1)] for g in range(SSM_GROUPS)]
    zs = [z[:, GROUP_WIDTH * g:GROUP_WIDTH * (g + 1)] for g in range(SSM_GROUPS)]
    sts = [st_ref[p] for p in range(SSM_PAIRS)]
    gains = [gain[:, GROUP_WIDTH * g:GROUP_WIDTH * (g + 1)] for g in range(SSM_GROUPS)]
    return xs_pre, b_pre, c_pre, dtr, dtr.T, zs, sts, gains


def ssd_forward(xbc, dt_raw, z, cw, cb, dbr, dbc, alr, alc, dsk, gain, name):
    nb, ns, wx = xbc.shape
    ln = SSM_CHUNK
    nt = ns // ln
    tile = lambda c: pl.BlockSpec((1, ln, c), lambda b, j: (b, j, 0))
    st_spec = pl.BlockSpec((1, 1, SSM_PAIRS, SSM_STATE, LANES), lambda b, j: (b, j, 0, 0, 0))

    def body(xbc_ref, dt_ref, z_ref, cw_ref, cb_ref, dbr_ref, dbc_ref, alr_ref, alc_ref, dsk_ref, gain_ref,
             y_ref, xconv_ref, stp_ref, xin, st):
        @pl.when(pl.program_id(1) == 0)
        def _():
            xin[0:HALO, :] = jnp.zeros((HALO, wx), f32)
            st[...] = jnp.zeros_like(st)

        xin[HALO:HALO + ln, :] = xbc_ref[0]
        xconv = jnp.broadcast_to(cb_ref[...], (ln, wx))
        for k in range(SSM_CONV):
            xconv = xconv + cw_ref[k:k + 1, :] * xin[pl.ds(HALO - SSM_CONV + 1 + k, ln), :]
        xin[0:HALO, :] = xin[ln:ln + HALO, :]
        xconv_ref[0] = xconv
        stp_ref[0, 0] = st[...]
        xs_pre, b_pre, c_pre, dtr, dtr_t, zs, sts, gains = _ssd_chunk_inputs(xconv, dt_ref[0], z_ref[0], st,
                                                                             gain_ref[...])
        out, st_new = _ssd_chunk(xs_pre, b_pre, c_pre, dtr, dtr_t, zs, sts, dbr_ref[...], dbc_ref[...],
                                 alr_ref[...], alc_ref[...], dsk_ref[...], gains)
        y_ref[0] = jnp.concatenate(out, axis=1).astype(y_ref.dtype)
        for p in range(SSM_PAIRS):
            st[p] = st_new[p]

    params = [cw, cb, dbr, dbc, alr, alc, dsk, gain]
    return pl.pallas_call(
        body, name=name, grid=(nb, nt),
        in_specs=[tile(wx), tile(LANES), tile(SSM_WIDTH)] + [_const2(p.shape) for p in params],
        out_specs=[tile(SSM_WIDTH), tile(wx), st_spec],
        out_shape=[jax.ShapeDtypeStruct((nb, ns, SSM_WIDTH), MXU_DTYPE), jax.ShapeDtypeStruct((nb, ns, wx), f32),
                   jax.ShapeDtypeStruct((nb, nt, SSM_PAIRS, SSM_STATE, LANES), f32)],
        scratch_shapes=[pltpu.VMEM((ln + HALO, wx), f32), pltpu.VMEM((SSM_PAIRS, SSM_STATE, LANES), f32)],
        compiler_params=_cparams("arbitrary", "arbitrary"),
    )(xbc, dt_raw, z, *params)


def ssd_backward(dy, xbc, xconv, dt_raw, z, stp, cw, cb, dbr, dbc, alr, alc, dsk, gain, name):
    nb, ns, wx = xbc.shape
    ln = SSM_CHUNK
    nt = ns // ln
    per = ln // HALO
    rj = lambda j: nt - 1 - j
    tile = lambda c: pl.BlockSpec((1, ln, c), lambda b, j: (b, rj(j), 0))
    before = pl.BlockSpec((1, HALO, wx), lambda b, j: (b, jnp.maximum(rj(j) * per - 1, 0), 0))
    st_spec = pl.BlockSpec((1, 1, SSM_PAIRS, SSM_STATE, LANES), lambda b, j: (b, rj(j), 0, 0, 0))

    def body(dy_ref, xbc_ref, xbcb_ref, xconv_ref, dt_ref, z_ref, stp_ref,
             cw_ref, cb_ref, dbr_ref, dbc_ref, alr_ref, alc_ref, dsk_ref, gain_ref,
             dxbc_ref, ddt_ref, dz_ref, dcw_ref, dcb_ref, ddbr_ref, ddbc_ref, dalr_ref, dalc_ref, ddsk_ref, dgain_ref,
             dxc_ext, dst, xin):
        j = pl.program_id(1)
        first = _first_step()
        at_seq_start = j == nt - 1

        @pl.when(j == 0)
        def _():
            dxc_ext[ln:ln + HALO, :] = jnp.zeros((HALO, wx), f32)
            dst[...] = jnp.zeros_like(dst)

        xs_pre, b_pre, c_pre, dtr, dtr_t, zs, sts, gains = _ssd_chunk_inputs(xconv_ref[0], dt_ref[0], z_ref[0],
                                                                             stp_ref.at[0, 0], gain_ref[...])
        _, vjp = jax.vjp(_ssd_chunk, xs_pre, b_pre, c_pre, dtr, dtr_t, zs, sts, dbr_ref[...], dbc_ref[...],
                         alr_ref[...], alc_ref[...], dsk_ref[...], gains)
        dyv = dy_ref[0].astype(f32)
        cot = ([dyv[:, GROUP_WIDTH * g:GROUP_WIDTH * (g + 1)] for g in range(SSM_GROUPS)],
               [dst[p] for p in range(SSM_PAIRS)])
        dxs, db, dc, ddt, ddt_t, dzs, dsts, ddbr, ddbc, dalr, dalc, ddsk, dgains = vjp(cot)
        for p in range(SSM_PAIRS):
            dst[p] = dsts[p]
        ddt_ref[0] = (ddt + ddt_t.T).astype(ddt_ref.dtype)
        dz_ref[0] = jnp.concatenate(dzs, axis=1).astype(dz_ref.dtype)
        _accum(ddbr_ref, ddbr, first)
        _accum(ddbc_ref, ddbc, first)
        _accum(dalr_ref, dalr, first)
        _accum(dalc_ref, dalc, first)
        _accum(ddsk_ref, ddsk, first)
        _accum(dgain_ref, jnp.concatenate(dgains, axis=1), first)

        dxc = jnp.concatenate(dxs + db + dc, axis=1)
        _accum(dcb_ref, jnp.sum(dxc, axis=0, keepdims=True), first)
        dxc_ext[0:ln, :] = dxc
        dxp = jnp.zeros((ln, wx), f32)
        for k in range(SSM_CONV):
            dxp = dxp + cw_ref[k:k + 1, :] * dxc_ext[pl.ds(SSM_CONV - 1 - k, ln), :]
        dxbc_ref[0] = dxp.astype(dxbc_ref.dtype)
        dxc_ext[ln:ln + HALO, :] = dxc[0:HALO, :]

        xin[0:HALO, :] = jnp.where(at_seq_start, 0.0, xbcb_ref[0])
        xin[HALO:HALO + ln, :] = xbc_ref[0]
        dcw_rows = [jnp.sum(dxc * xin[pl.ds(HALO - SSM_CONV + 1 + k, ln), :], axis=0, keepdims=True)
                    for k in range(SSM_CONV)]
        dcw_rows += [jnp.zeros((1, wx), f32)] * (HALO - SSM_CONV)
        _accum(dcw_ref, jnp.concatenate(dcw_rows, axis=0), first)

    params = [cw, cb, dbr, dbc, alr, alc, dsk, gain]
    pshape = lambda p: jax.ShapeDtypeStruct(p.shape, f32)
    return pl.pallas_call(
        body, name=name, grid=(nb, nt),
        in_specs=[tile(SSM_WIDTH), tile(wx), before, tile(wx), tile(LANES), tile(SSM_WIDTH), st_spec]
        + [_const2(p.shape) for p in params],
        out_specs=[tile(wx), tile(LANES), tile(SSM_WIDTH), _const2((HALO, wx))] + [_const2(p.shape) for p in params[1:]],
        out_shape=[jax.ShapeDtypeStruct((nb, ns, wx), MXU_DTYPE), jax.ShapeDtypeStruct((nb, ns, LANES), MXU_DTYPE),
                   jax.ShapeDtypeStruct((nb, ns, SSM_WIDTH), MXU_DTYPE), jax.ShapeDtypeStruct((HALO, wx), f32)]
        + [pshape(p) for p in params[1:]],
        scratch_shapes=[pltpu.VMEM((ln + HALO, wx), f32), pltpu.VMEM((SSM_PAIRS, SSM_STATE, LANES), f32),
                        pltpu.VMEM((ln + HALO, wx), f32)],
        compiler_params=_cparams("arbitrary", "arbitrary"),
    )(dy, xbc, xbc, xconv, dt_raw, z, stp, *params)


CONF_HALO = 32
CONF_OFF = CONF_HALO - CONF_KERNEL + 1


def _conf_specs(ts, c, nt):
    per = ts // CONF_HALO
    tile = pl.BlockSpec((1, ts, c), lambda b, j: (b, j, 0))
    before = pl.BlockSpec((1, CONF_HALO, c), lambda b, j: (b, jnp.maximum(j * per - 1, 0), 0))
    after = pl.BlockSpec((1, CONF_HALO, c), lambda b, j: (b, jnp.minimum((j + 1) * per, nt * per - 1), 0))
    return tile, before, after


SUBLANES = 8


def _shifted_copies(dst, x):
    rows = x.shape[0]
    dst[0] = x
    for b in range(1, SUBLANES):
        dst[b] = pltpu.roll(x, rows - b, 0)


def _window(copies, off, size):
    b = off % SUBLANES
    return copies[b, pl.ds(off - b, size), :]


def _glu(x):
    return x[:, :CONF_WIDTH] * jax.nn.sigmoid(x[:, CONF_WIDTH:])


def _layernorm_parts(c):
    xc = c - jnp.mean(c, axis=-1, keepdims=True)
    r = lax.rsqrt(jnp.mean(xc * xc, axis=-1, keepdims=True) + EPS)
    return xc * r, r


def conf_forward(glu, cw, cb, ln_g, ln_b, name):
    nb, ns, wg = glu.shape
    w = CONF_WIDTH
    ts = SEQ_TILE
    nt = ns // ts
    tile, before, _ = _conf_specs(ts, wg, nt)

    def body(x_ref, xb_ref, cw_ref, cb_ref, g_ref, b_ref, y_ref, u_rot):
        _shifted_copies(u_rot, jnp.concatenate(
            [jnp.where(pl.program_id(1) == 0, 0.0, _glu(xb_ref[0])), _glu(x_ref[0])], axis=0))
        conv = jnp.broadcast_to(cb_ref[...], (ts, w))
        for k in range(CONF_KERNEL):
            conv = conv + cw_ref[k:k + 1, :] * _window(u_rot, CONF_OFF + k, ts)
        xhat, _ = _layernorm_parts(conv)
        y_ref[0] = _silu(xhat * g_ref[...] + b_ref[...]).astype(y_ref.dtype)

    params = [cw, cb, ln_g, ln_b]
    return pl.pallas_call(
        body, name=name, grid=(nb, nt),
        in_specs=[tile, before] + [_const2(p.shape) for p in params],
        out_specs=pl.BlockSpec((1, ts, w), lambda b, j: (b, j, 0)),
        out_shape=jax.ShapeDtypeStruct((nb, ns, w), MXU_DTYPE),
        scratch_shapes=[pltpu.VMEM((SUBLANES, ts + CONF_HALO, w), f32)],
        compiler_params=_cparams("parallel", "parallel"),
    )(glu, glu, *params)


def conf_backward(dy, glu, cw, cb, ln_g, ln_b, name):
    nb, ns, wg = glu.shape
    w = CONF_WIDTH
    ts = SEQ_TILE
    nt = ns // ts
    te = ts + CONF_HALO
    tile, before, after = _conf_specs(ts, wg, nt)
    dtile, _, dafter = _conf_specs(ts, w, nt)

    def body(dy_ref, dya_ref, x_ref, xb_ref, xa_ref, cw_ref, cb_ref, g_ref, b_ref,
             dx_ref, dcw_ref, dcb_ref, dg_ref, db_ref, u_ext, dc_ext):
        j = pl.program_id(1)
        first = _first_step()
        x = x_ref[0]
        _shifted_copies(u_ext, jnp.concatenate(
            [jnp.where(j == 0, 0.0, _glu(xb_ref[0])), _glu(x), _glu(xa_ref[0])], axis=0))
        conv = jnp.broadcast_to(cb_ref[...], (te, w))
        for k in range(CONF_KERNEL):
            conv = conv + cw_ref[k:k + 1, :] * _window(u_ext, CONF_OFF + k, te)
        xhat, r = _layernorm_parts(conv)
        lnout = xhat * g_ref[...] + b_ref[...]
        sg = jax.nn.sigmoid(lnout)
        rows = lax.broadcasted_iota(jnp.int32, (te, w), 0)
        dyv = jnp.concatenate([dy_ref[0].astype(f32), dya_ref[0].astype(f32)], axis=0)
        dyv = jnp.where(jnp.logical_and(j == nt - 1, rows >= ts), 0.0, dyv)
        dln = dyv * sg * (1.0 + lnout * (1.0 - sg))
        in_tile = rows < ts
        _accum(dg_ref, jnp.sum(jnp.where(in_tile, dln * xhat, 0.0), axis=0, keepdims=True), first)
        _accum(db_ref, jnp.sum(jnp.where(in_tile, dln, 0.0), axis=0, keepdims=True), first)
        dxh = dln * g_ref[...]
        dconv = r * (dxh - jnp.mean(dxh, axis=-1, keepdims=True) - xhat * jnp.mean(dxh * xhat, axis=-1, keepdims=True))
        _shifted_copies(dc_ext, dconv)
        dct = dconv[0:ts, :]
        _accum(dcb_ref, jnp.sum(dct, axis=0, keepdims=True), first)
        du = jnp.zeros((ts, w), f32)
        dcw_rows = []
        for k in range(CONF_KERNEL):
            du = du + cw_ref[k:k + 1, :] * _window(dc_ext, CONF_KERNEL - 1 - k, ts)
            dcw_rows.append(jnp.sum(dct * _window(u_ext, CONF_OFF + k, ts), axis=0, keepdims=True))
        dcw_rows.append(jnp.zeros((1, w), f32))
        _accum(dcw_ref, jnp.concatenate(dcw_rows, axis=0), first)
        sb = jax.nn.sigmoid(x[:, w:])
        dx_ref[0] = jnp.concatenate([du * sb, du * x[:, :w] * sb * (1.0 - sb)], axis=1).astype(dx_ref.dtype)

    params = [cw, cb, ln_g, ln_b]
    return pl.pallas_call(
        body, name=name, grid=(nb, nt),
        in_specs=[dtile, dafter, tile, before, after] + [_const2(p.shape) for p in params],
        out_specs=[tile] + [_const2(p.shape) for p in params],
        out_shape=[jax.ShapeDtypeStruct((nb, ns, wg), MXU_DTYPE)] + [jax.ShapeDtypeStruct(p.shape, f32) for p in params],
        scratch_shapes=[pltpu.VMEM((SUBLANES, te + CONF_HALO, w), f32), pltpu.VMEM((SUBLANES, te, w), f32)],
        compiler_params=_cparams("arbitrary", "arbitrary"),
    )(dy, dy, glu, glu, glu, *params)


def _row(v):
    return v.reshape(1, -1).astype(f32)


def _pad_to(v, n, axis):
    pads = [(0, 0)] * v.ndim
    pads[axis] = (0, n - v.shape[axis])
    return jnp.pad(v, pads)


def _block_diag(w):
    nh, d, _ = w.shape
    eye = jnp.eye(nh, dtype=w.dtype)
    return (eye[:, None, :, None] * w[:, :, None, :]).reshape(nh * d, nh * d)


def _diag_blocks(m, nh):
    d = m.shape[0] // nh
    idx = jnp.arange(nh)
    return m.reshape(nh, d, nh, d)[idx, :, idx, :]


def _mix_even_fwd(h, gpre, w, wl, nb, ns, rider=None):
    t = nb * ns
    w_in = wl["w_in"]
    w_lx, w_lg = Part(w_in, 0, LRU_WIDTH, 1), Part(w_in, LRU_WIDTH, LRU_WIDTH, 1)
    w_qkv = Part(w_in, 2 * LRU_WIDTH, 3 * SB_WIDTH, 1)
    xpre, gate, qkv = norm_matmul(h, gpre, [w_lx, w_lg, w_qkv], [f32, f32, f32], name="ev_in_proj")
    lru_p = [w["ev_lru_conv_w"][0], _row(w["ev_lru_conv_b"][0]),
             _block_diag(w["ev_lru_gate_a_w"][0]).astype(MXU_DTYPE), _row(w["ev_lru_gate_a_b"][0]),
             _block_diag(w["ev_lru_gate_x_w"][0]).astype(MXU_DTYPE), _row(w["ev_lru_gate_x_b"][0]),
             _row(w["ev_lru_lambda"][0])]
    xpre3, gate3, qkv3 = xpre.reshape(nb, ns, -1), gate.reshape(nb, ns, -1), qkv.reshape(nb, ns, -1)
    y_a, xc, hs = lru_forward(xpre3, gate3, *lru_p, name="ev_lru_fwd")
    o = sb_forward(qkv3, name="ev_sb_fwd", rider=rider)
    carried = None
    if rider is not None:
        o, carried = o
    ys = [y_a.reshape(t, -1), o.reshape(t, -1)]
    saved = dict(xpre=xpre3, gate=gate3, qkv=qkv3, xc=xc, hs=hs, o=o, lru_p=lru_p)
    return ys, saved, carried


def _mix_even_bwd(dys, saved, wtl, nb, ns, lru_rider=None, attention_rider=None):
    t = nb * ns
    dy_a, dy_b = [d.reshape(nb, ns, -1) for d in dys]
    outs = lru_backward(dy_a, saved["xpre"], saved["gate"], saved["xc"], saved["hs"], *saved["lru_p"],
                        name="ev_lru_bwd", rider=lru_rider)
    lru_carried = None
    if lru_rider is not None:
        outs, lru_carried = outs
    dxp, dgt, dcw, dcb, dga, dgab, dgx, dgxb, dlam = outs
    rider = attention_rider(lru_carried) if attention_rider is not None else None
    carried = None
    if rider is None:
        dq, dk, dv = sb_backward(saved["qkv"], saved["o"], dy_b, name="ev_sb_bwd")
    else:
        (dq, dk, dv), carried = sb_backward(saved["qkv"], saved["o"], dy_b, name="ev_sb_bwd", rider=rider)
    w_in_t = wtl["w_in"]
    pieces = [dxp, dgt, dq, dk, dv]
    gs = [d.reshape(t, -1) for d in pieces]
    wts = [Part(w_in_t, LRU_WIDTH * i, LRU_WIDTH, 0) for i in range(5)]
    grads = {
        "ev_lru_conv_w": dcw[:LRU_CONV][None], "ev_lru_conv_b": dcb,
        "ev_lru_gate_a_w": _diag_blocks(dga, LRU_HEADS)[None], "ev_lru_gate_a_b": dgab,
        "ev_lru_gate_x_w": _diag_blocks(dgx, LRU_HEADS)[None], "ev_lru_gate_x_b": dgxb,
        "ev_lru_lambda": dlam,
    }
    return gs, wts, grads, carried


def _odd_params(w):
    ssd_p = [w["od_ssm_conv_w"][0], _row(w["od_ssm_conv_b"][0]),
             _pad_to(_row(w["od_ssm_dt_bias"][0]), LANES, 1), _pad_to(_row(w["od_ssm_dt_bias"][0]), LANES, 1).T,
             _pad_to(_row(w["od_ssm_a_log"][0]), LANES, 1), _pad_to(_row(w["od_ssm_a_log"][0]), LANES, 1).T,
             _pad_to(_row(w["od_ssm_d"][0]), LANES, 1), _row(w["od_ssm_norm"][0])]
    conf_p = [_pad_to(w["od_cm_conv_w"][0], CONF_HALO, 0), _row(w["od_cm_conv_b"][0]),
              _row(w["od_cm_ln_g"][0]), _row(w["od_cm_ln_b"][0])]
    return ssd_p, conf_p


ODD_SPLITS = (SSM_WIDTH, SSM_WIDTH + SSM_XBC, SSM_WIDTH + SSM_XBC + SSM_HEADS)


def _mix_odd_fwd(h, gpre, w, wl, nb, ns, rider=None):
    assert rider is None
    t = nb * ns
    w_in = wl["w_in"]
    s0, s1, s2 = ODD_SPLITS
    w_al = jnp.concatenate([w_in[:, :s1], w_in[:, s2:], _pad_to(w_in[:, s1:s2], LANES, 1)], axis=1)
    widths = (s0, s1 - s0, w_in.shape[1] - s2, LANES)
    starts = (0, s0, s1, s1 + widths[2])
    zz, xbc, glu, dtr = norm_matmul(h, gpre, [Part(w_al, a, n, 1) for a, n in zip(starts, widths)], [f32] * 4,
                                    name="od_in_proj")
    ssd_p, conf_p = _odd_params(w)
    zz3, xbc3, dtr3, glu3 = [a.reshape(nb, ns, -1) for a in (zz, xbc, dtr, glu)]
    y_c, xconv, stp = ssd_forward(xbc3, dtr3, zz3, *ssd_p, name="od_ssd_fwd")
    y_d = conf_forward(glu3, *conf_p, name="od_conf_fwd")
    ys = [y_c.reshape(t, -1), y_d.reshape(t, -1)]
    saved = dict(z=zz3, xbc=xbc3, dtr=dtr3, glu=glu3, xconv=xconv, stp=stp, ssd_p=ssd_p, conf_p=conf_p)
    return ys, saved, None


def _mix_odd_bwd(dys, saved, wtl, nb, ns, lru_rider=None, attention_rider=None):
    assert lru_rider is None and attention_rider is None
    t = nb * ns
    dy_c, dy_d = [d.reshape(nb, ns, -1) for d in dys]
    outs = ssd_backward(dy_c, saved["xbc"], saved["xconv"], saved["dtr"], saved["z"], saved["stp"], *saved["ssd_p"],
                        name="od_ssd_bwd")
    dxbc, ddt, dz, dcw, dcb, ddbr, ddbc, dalr, dalc, ddsk, dgain = outs
    dglu, ccw, ccb, clg, clb = conf_backward(dy_d, saved["glu"], *saved["conf_p"], name="od_conf_bwd")
    w_in_t = wtl["w_in"]
    s0, s1, s2 = ODD_SPLITS
    carried = None
    gs = [d.reshape(t, -1) for d in (dz, dxbc, dglu, ddt)]
    wt_al = jnp.concatenate([w_in_t[:s1], w_in_t[s2:], _pad_to(w_in_t[s1:s2], LANES, 0)], axis=0)
    widths = (s0, s1 - s0, w_in_t.shape[0] - s2, LANES)
    starts = (0, s0, s1, s1 + widths[2])
    wts = [Part(wt_al, a, n, 0) for a, n in zip(starts, widths)]
    nh = SSM_HEADS
    grads = {
        "od_ssm_conv_w": dcw[:SSM_CONV][None], "od_ssm_conv_b": dcb,
        "od_ssm_dt_bias": ddbr[:, :nh] + ddbc[:nh, 0][None], "od_ssm_a_log": dalr[:, :nh] + dalc[:nh, 0][None],
        "od_ssm_d": ddsk[:, :nh], "od_ssm_norm": dgain,
        "od_cm_conv_w": ccw[:CONF_KERNEL][None], "od_cm_conv_b": ccb, "od_cm_ln_g": clg, "od_cm_ln_b": clb,
    }
    return gs, wts, grads, carried


LAYER_MATRICES = ("w_in", "w_out", "mlp_w1", "mlp_w2", "ple_w_proj", "ple_w_gate")
NORM_NAMES = ("norm_mix_pre", "norm_mix_post", "norm_mlp_pre", "norm_mlp_post", "norm_ple")


class NoOverlap:
    sums, from_chips = {}, {}

    def attention_fwd_rider(self):
        return None

    def weights_arrived(self, carried, wl, wtl):
        raise NotImplementedError

    def mlp_bwd_rider(self, layer1_grads):
        return None

    def after_mlp_bwd(self, carried):
        pass

    def lru_bwd_rider(self, layer0_grads):
        return None

    def attention_bwd_rider(self, carried):
        return None

    def after_attention_bwd(self, carried):
        pass


OUT_SPLIT = (LRU_WIDTH, SSM_WIDTH)


def local_step(x, p, target, w, wl, wtl, comm=NoOverlap()):
    nb, ns, d = x.shape
    t = nb * ns
    h = x.reshape(t, d)
    depth = p.shape[0]
    wl, wtl = list(wl), list(wtl)
    tapes = []
    for i in range(depth):
        even = i % 2 == 0
        tag = f"l{i}_"
        gpre = _row(w["norm_mix_pre"][i])
        rider = comm.attention_fwd_rider() if i == 0 else None
        ys, saved, carried = (_mix_even_fwd if even else _mix_odd_fwd)(h, gpre, w, wl[i], nb, ns, rider)
        if rider is not None:
            wl, wtl = comm.weights_arrived(carried, wl, wtl)
        w_out = wl[i]["w_out"]
        split = OUT_SPLIT[i % 2]
        w_outs = [Part(w_out, 0, split, 0), Part(w_out, split, w_out.shape[0] - split, 0)]
        h1, m = matmul_residual_norm(ys, w_outs, h, _row(w["norm_mix_post"][i]), name=tag + "out_proj")
        a1, = norm_matmul(h1, _row(w["norm_mlp_pre"][i]), [wl[i]["mlp_w1"]], [MXU_DTYPE], name=tag + "mlp_up")
        h2, f = matmul_residual_norm([a1], [wl[i]["mlp_w2"]], h1, _row(w["norm_mlp_post"][i]), name=tag + "mlp_down",
                                     relu2=True)
        pi = p[i].reshape(t, -1)
        ple_args = (h2, pi, wl[i]["ple_w_gate"], wl[i]["ple_w_proj"], _row(w["norm_ple"][i]))
        if i < depth - 1:
            h3, gl, emb = ple_forward(*ple_args, name=tag + "ple")
        else:
            loss_row, dh, gl, emb = ple_forward(*ple_args, name=tag + "ple_loss", target=target.reshape(t, d))
            h3 = None
        tapes.append(dict(h=h, ys=ys, w_outs=w_outs, saved=saved, h1=h1, m=m, a1=a1, h2=h2, f=f, pi=pi, gl=gl,
                          emb=emb))
        h = h3

    grads = {}
    norm_grads = {k: [None] * depth for k in NORM_NAMES}
    layer_grads = [None] * depth
    for i in reversed(range(depth)):
        even = i % 2 == 0
        tag = f"l{i}_"
        tp = tapes[i]
        lg = {}
        to_sibling = comm.mlp_bwd_rider(layer_grads[1]) if i == 0 else None
        dh2, dgl, demb, dg = ple_backward(dh, tp["h2"], tp["gl"], tp["emb"], _row(w["norm_ple"][i]),
                                          wtl[i]["ple_w_gate"], name=tag + "ple_bwd")
        norm_grads["norm_ple"][i] = dg
        lg["ple_w_gate"] = weight_grad(tp["h2"], dgl, name=tag + "dw_gate")
        lg["ple_w_proj"] = weight_grad(tp["pi"], demb, name=tag + "dw_proj")
        outs = bwd_through_norm_out(dh2, tp["f"], _row(w["norm_mlp_post"][i]), [wtl[i]["mlp_w2"]], [MXU_DTYPE],
                                    name=tag + "mlp_down_bwd", relu2_of=tp["a1"], rider=to_sibling)
        d_f, (da1,), dg = outs[:3]
        if to_sibling is not None:
            comm.after_mlp_bwd(outs[3])
        norm_grads["norm_mlp_post"][i] = dg
        lg["mlp_w2"] = weight_grad(tp["a1"], d_f, name=tag + "dw2", prologue="relu2")
        gpre = _row(w["norm_mlp_pre"][i])
        dh1, dg = bwd_through_norm_in(dh2, [da1], [wtl[i]["mlp_w1"]], tp["h1"], gpre, name=tag + "mlp_up_bwd")
        norm_grads["norm_mlp_pre"][i] = dg
        lg["mlp_w1"] = weight_grad(tp["h1"], da1, name=tag + "dw1", prologue="rms", gain=gpre)
        wt_out = wtl[i]["w_out"]
        split = tp["w_outs"][0].shape[0]
        dm, dys, dg = bwd_through_norm_out(dh1, tp["m"], _row(w["norm_mix_post"][i]),
                                           [Part(wt_out, 0, split, 1),
                                            Part(wt_out, split, wt_out.shape[1] - split, 1)],
                                           [f32, MXU_DTYPE if even else f32],
                                           name=tag + "out_proj_bwd")
        norm_grads["norm_mix_post"][i] = dg
        lg["w_out"] = jnp.concatenate([weight_grad(y, dm, name=tag + f"dw_out{k}") for k, y in enumerate(tp["ys"])],
                                      axis=0)
        lru_rider = comm.lru_bwd_rider(lg) if i == 0 else None
        gs, wts, mix_grads, carried = (_mix_even_bwd if even else _mix_odd_bwd)(
            dys, tp["saved"], wtl[i], nb, ns, lru_rider, comm.attention_bwd_rider if lru_rider is not None else None)
        if carried is not None:
            comm.after_attention_bwd(carried)
        grads.update(mix_grads)
        gpre = _row(w["norm_mix_pre"][i])
        dh, dg = bwd_through_norm_in(dh1, gs, wts, tp["h"], gpre, name=tag + "in_proj_bwd")
        norm_grads["norm_mix_pre"][i] = dg
        dw_in = weight_grads_of_norm(tp["h"], gpre, gs, name=tag + "dw_in")
        if not even:
            dw_in = [dw_in[0], dw_in[1], dw_in[3][:, :SSM_HEADS], dw_in[2]]
        lg["w_in"] = jnp.concatenate(dw_in, axis=1)
        layer_grads[i] = lg
    for k, v in norm_grads.items():
        grads[k] = jnp.concatenate(v, axis=0)
    return loss_row[0, 0], dh.reshape(nb, ns, d), grads, layer_grads


MESH_ID = pl.DeviceIdType.MESH
ANY = pl.BlockSpec(memory_space=pl.ANY)


def _mesh_pos():
    return lax.axis_index("x"), lax.axis_index("y"), lax.axis_index("c")


def all_gather(shards, name):
    return _run_alone(gather_rider(shards), name)


class Rider:
    def __init__(self, inputs, out_shapes, scratch_shapes, start, finish, middle=None):
        self.inputs, self.out_shapes, self.scratch_shapes = list(inputs), list(out_shapes), list(scratch_shapes)
        self.start, self.finish, self.middle = start, finish, middle


def _run_alone(rider, name):
    ni, no = len(rider.inputs), len(rider.out_shapes)

    def body(*refs):
        args = (refs[:ni], refs[ni:ni + no], refs[ni + no:])
        rider.start(*args)
        if rider.middle is not None:
            rider.middle(*args)
        rider.finish(*args)

    return pl.pallas_call(
        body, name=name, out_shape=rider.out_shapes, in_specs=[ANY] * ni, out_specs=[ANY] * no,
        scratch_shapes=rider.scratch_shapes,
    )(*rider.inputs)


def _ride(rider, body, in_specs, out_specs, out_shape, scratch_shapes, grid):
    in_specs, out_specs, out_shape = list(in_specs), list(out_specs), list(out_shape)
    scratch_shapes = list(scratch_shapes)
    if rider is None:
        return body, in_specs, out_specs, out_shape, scratch_shapes
    n_in, n_out, n_scr = len(in_specs), len(out_specs), len(scratch_shapes)
    ri, ro = len(rider.inputs), len(rider.out_shapes)
    total = math.prod(grid)

    def carrying(*refs):
        ins, r_ins = refs[:n_in], refs[n_in:n_in + ri]
        o0 = n_in + ri
        outs, r_outs = refs[o0:o0 + n_out], refs[o0 + n_out:o0 + n_out + ro]
        s0 = o0 + n_out + ro
        scr, r_scr = refs[s0:s0 + n_scr], refs[s0 + n_scr:]
        step = pl.program_id(0)
        for ax in range(1, len(grid)):
            step = step * grid[ax] + pl.program_id(ax)
        args = (r_ins, r_outs, r_scr)
        pl.when(step == 0)(lambda: rider.start(*args))
        if rider.middle is not None:
            pl.when(step == total // 2)(lambda: rider.middle(*args))
        body(*ins, *outs, *scr)
        pl.when(step == total - 1)(lambda: rider.finish(*args))

    return (carrying, in_specs + [ANY] * ri, out_specs + [ANY] * ro, out_shape + rider.out_shapes,
            scratch_shapes + rider.scratch_shapes)


def gather_rider(shards):
    n = len(shards)

    def parts(x_refs, out_refs, scr):
        send_sems, recv_sems, local_sems = scr
        x, y, c = _mesh_pos()
        chips = [(1 - x, y), (x, 1 - y), (1 - x, 1 - y)]

        def slot(a, px, py, pc):
            return out_refs[a].at[4 * px + 2 * py + pc]

        def copy(a, k, block, to, src=None):
            return pltpu.make_async_remote_copy(
                src_ref=slot(a, *block) if src is None else src, dst_ref=slot(a, *block),
                send_sem=send_sems.at[7 * a + k], recv_sem=recv_sems.at[7 * a + k], device_id=to,
                device_id_type=MESH_ID)

        me, sibling = (x, y, c), (x, y, 1 - c)
        def mine():
            return [pltpu.make_async_copy(x_refs[a], slot(a, *me), local_sems.at[a]) for a in range(n)]

        def first():
            out = []
            for j, chip in enumerate(chips):
                out += [copy(a, 1 + j, me, (*chip, c), src=x_refs[a]) for a in range(n)]
            return out + [copy(a, 0, me, sibling, src=x_refs[a]) for a in range(n)]

        def passed(j):
            return [copy(a, 4 + j, (*chips[j], c), sibling) for a in range(n)]

        return me, sibling, chips, c, copy, mine, first, passed

    def start(x_refs, out_refs, scr):
        _, _, _, _, _, mine, first, _ = parts(x_refs, out_refs, scr)
        for cp in mine() + first():
            cp.start()

    def middle(x_refs, out_refs, scr):
        me, _, chips, c, copy, _, _, passed = parts(x_refs, out_refs, scr)
        for j, chip in enumerate(chips):
            for a, fwd in enumerate(passed(j)):
                copy(a, 1 + j, (*chip, c), me).wait_recv()
                fwd.start()

    def finish(x_refs, out_refs, scr):
        me, sibling, chips, c, copy, mine, first, passed = parts(x_refs, out_refs, scr)
        for a in range(n):
            copy(a, 0, sibling, me).wait_recv()
        for j, chip in enumerate(chips):
            for a in range(n):
                copy(a, 4 + j, (*chip, 1 - c), me).wait_recv()
        for cp in first() + [cp for j in range(len(chips)) for cp in passed(j)]:
            cp.wait_send()
        for cp in mine():
            cp.wait()

    return Rider(shards, [jax.ShapeDtypeStruct((N_DEV,) + s.shape, s.dtype) for s in shards],
                 [pltpu.SemaphoreType.DMA((7 * n,)), pltpu.SemaphoreType.DMA((7 * n,)), pltpu.SemaphoreType.DMA((n,))],
                 start, finish, middle)


def scatter_to_sibling(parts, name):
    return _run_alone(sibling_rider(parts), name)


def sibling_rider(parts):
    n = len(parts)

    def copies(g_refs, out_refs, scr):
        send_sems, recv_sems = scr
        x, y, c = _mesh_pos()
        return [pltpu.make_async_remote_copy(
            src_ref=g_refs[a].at[2 * chip + (1 - c)], dst_ref=out_refs[a].at[chip],
            send_sem=send_sems.at[4 * a + chip], recv_sem=recv_sems.at[4 * a + chip], device_id=(x, y, 1 - c),
            device_id_type=MESH_ID) for a in range(n) for chip in range(4)]

    def start(*refs):
        for cp in copies(*refs):
            cp.start()

    def finish(*refs):
        cps = copies(*refs)
        for cp in cps:
            cp.wait_recv()
        for cp in cps:
            cp.wait_send()

    return Rider(parts, [jax.ShapeDtypeStruct((4,) + p.shape[1:], p.dtype) for p in parts],
                 [pltpu.SemaphoreType.DMA((4 * n,)), pltpu.SemaphoreType.DMA((4 * n,))], start, finish)


def scatter_to_chips(partials, name):
    return _run_alone(chips_rider(partials), name)


def chips_rider(partials):
    n = len(partials)

    def copies(p_refs, out_refs, scr):
        send_sems, recv_sems = scr
        x, y, c = _mesh_pos()
        chips = [(1 - x, y), (x, 1 - y), (1 - x, 1 - y)]
        return [pltpu.make_async_remote_copy(
            src_ref=p_refs[a].at[2 * px + py], dst_ref=out_refs[a].at[j],
            send_sem=send_sems.at[3 * a + j], recv_sem=recv_sems.at[3 * a + j], device_id=(px, py, c),
            device_id_type=MESH_ID) for a in range(n) for j, (px, py) in enumerate(chips)]

    def start(*refs):
        for cp in copies(*refs):
            cp.start()

    def finish(*refs):
        cps = copies(*refs)
        for cp in cps:
            cp.wait_recv()
        for cp in cps:
            cp.wait_send()

    return Rider(partials, [jax.ShapeDtypeStruct((3,) + p.shape[1:], p.dtype) for p in partials],
                 [pltpu.SemaphoreType.DMA((3 * n,)), pltpu.SemaphoreType.DMA((3 * n,))], start, finish)


ICI_DTYPE = jnp.bfloat16
ELEMENTWISE_BLOCK_BYTES = 1 << 20


def _row_tile(rows, cols):
    cap = max(16, ELEMENTWISE_BLOCK_BYTES // (4 * cols))
    best = [t for t in range(16, min(rows, cap) + 1, 16) if rows % t == 0]
    return best[-1] if best else rows


def add_sibling_parts(parts, received, core, name):
    _, r, n = parts.shape
    tr = _row_tile(r, n)

    def body(c_ref, a_ref, b_ref, o_ref, ob_ref):
        s = a_ref[...] + b_ref[...]
        o_ref[...] = s
        ob_ref[...] = s.astype(ob_ref.dtype)

    blk = pl.BlockSpec((1, tr, n), lambda i, j, c_ref: (i, j, 0))
    return pl.pallas_call(
        body, name=name,
        grid_spec=pltpu.PrefetchScalarGridSpec(
            num_scalar_prefetch=1, grid=(4, r // tr),
            in_specs=[pl.BlockSpec((1, tr, n), lambda i, j, c_ref: (2 * i + c_ref[0], j, 0)), blk],
            out_specs=[blk, blk]),
        out_shape=[jax.ShapeDtypeStruct((4, r, n), f32), jax.ShapeDtypeStruct((4, r, n), ICI_DTYPE)],
        compiler_params=_cparams("parallel", "parallel"),
    )(core, parts, received)


def _adamw(w, g, m, v):
    m = ADAM_B1 * m + (1.0 - ADAM_B1) * g
    v = ADAM_B2 * v + (1.0 - ADAM_B2) * jnp.square(g)
    m_hat = m / (1.0 - ADAM_B1 ** ADAM_STEP)
    v_hat = v / (1.0 - ADAM_B2 ** ADAM_STEP)
    delta = -ADAM_LR * (m_hat / (jnp.sqrt(v_hat) + ADAM_EPS) + ADAM_WD * w)
    return delta, m, v


def adamw_sharded(partial, received, chip, w, m, v, name):
    _, r, n = partial.shape

    def body(k_ref, p_ref, r_ref, w_ref, m_ref, v_ref, g_out, d_out, m_out, v_out):
        g = p_ref[0] + r_ref[0].astype(f32)
        g = g + r_ref[1].astype(f32)
        g = g + r_ref[2].astype(f32)
        delta, mn, vn = _adamw(w_ref[...], g, m_ref[...], v_ref[...])
        g_out[...] = g
        d_out[...] = delta
        m_out[...] = mn
        v_out[...] = vn

    tr = _row_tile(r, n)
    flat = pl.BlockSpec((tr, n), lambda j, k_ref: (j, 0))
    return pl.pallas_call(
        body, name=name,
        grid_spec=pltpu.PrefetchScalarGridSpec(
            num_scalar_prefetch=1, grid=(r // tr,),
            in_specs=[pl.BlockSpec((1, tr, n), lambda j, k_ref: (k_ref[0], j, 0)),
                      pl.BlockSpec((3, tr, n), lambda j, k_ref: (0, j, 0)), flat, flat, flat],
            out_specs=[flat] * 4),
        out_shape=[jax.ShapeDtypeStruct((r, n), f32)] * 4,
        compiler_params=_cparams("parallel"),
    )(chip, partial, received, w, m, v)


def adamw_replicated(gathered, w, m, v, name):
    _, r, n = gathered.shape

    def body(g_ref, w_ref, m_ref, v_ref, g_out, d_out, m_out, v_out):
        g = g_ref[0]
        for k in range(1, N_DEV):
            g = g + g_ref[k]
        delta, mn, vn = _adamw(w_ref[...], g, m_ref[...], v_ref[...])
        g_out[...] = g
        d_out[...] = delta
        m_out[...] = mn
        v_out[...] = vn

    return pl.pallas_call(
        body, name=name,
        out_shape=[jax.ShapeDtypeStruct((r, n), f32)] * 4,
        compiler_params=pltpu.CompilerParams(vmem_limit_bytes=VMEM_LIMIT),
    )(gathered, w, m, v)


W_NAMES = ['ev_w_in', 'ev_lru_conv_w', 'ev_lru_conv_b', 'ev_lru_gate_a_w', 'ev_lru_gate_a_b', 'ev_lru_gate_x_w',
           'ev_lru_gate_x_b', 'ev_lru_lambda', 'ev_w_out', 'od_w_in', 'od_ssm_conv_w', 'od_ssm_conv_b',
           'od_ssm_dt_bias', 'od_ssm_a_log', 'od_ssm_d', 'od_ssm_norm', 'od_cm_conv_w', 'od_cm_conv_b', 'od_cm_ln_g',
           'od_cm_ln_b', 'od_w_out', 'norm_mix_pre', 'norm_mix_post', 'norm_mlp_pre', 'norm_mlp_post', 'norm_ple',
           'mlp_w1', 'mlp_w2', 'ple_w_proj', 'ple_w_gate']
BIG_SHARDED = {'ev_w_in': 2, 'ev_w_out': 1, 'od_w_in': 2, 'od_w_out': 1, 'mlp_w1': 2, 'mlp_w2': 1, 'ple_w_proj': 2,
               'ple_w_gate': 1}
SMALL_SHARDED = {'ev_lru_conv_w': 2, 'od_ssm_conv_w': 2, 'od_ssm_conv_b': 1, 'od_ssm_norm': 1, 'od_cm_conv_w': 2,
                 'od_cm_conv_b': 1, 'od_cm_ln_g': 1, 'od_cm_ln_b': 1}
SHARDED = {**BIG_SHARDED, **SMALL_SHARDED}
REPLICATED = [n for n in W_NAMES if n not in SHARDED]


def _round_up(n, k):
    return -(-n // k) * k


def _pack_rows(flat, rows_multiple):
    n = flat.shape[0]
    total = _round_up(n, LANES * rows_multiple)
    return jnp.pad(flat, (0, total - n)).reshape(-1, LANES)


def _unpack(flat, shapes):
    out, off = {}, 0
    for name, shape in shapes.items():
        size = math.prod(shape)
        out[name] = flat[off:off + size].reshape(shape)
        off += size
    return out


def _unshard(g8, shape, axis):
    g = jnp.moveaxis(g8.reshape((N_DEV,) + tuple(shape)), 0, axis)
    return g.reshape(tuple(shape[:axis]) + (N_DEV * shape[axis],) + tuple(shape[axis + 1:]))


def _to_shards(g, axis):
    shard = g.shape[axis] // N_DEV
    g = g.reshape(g.shape[:axis] + (N_DEV, shard) + g.shape[axis + 1:])
    return jnp.moveaxis(g, axis, 0)


def _pack_small(tree):
    return _pack_rows(jnp.concatenate([tree[k].astype(f32).reshape(-1) for k in SMALL_SHARDED]), 16)


def _layer_entry(i, key):
    mixer = "ev" if i % 2 == 0 else "od"
    return {"w_in": (mixer + "_w_in", i // 2, 1), "w_out": (mixer + "_w_out", i // 2, 0),
            "mlp_w1": ("mlp_w1", i, 1), "mlp_w2": ("mlp_w2", i, 0),
            "ple_w_proj": ("ple_w_proj", i, 1), "ple_w_gate": ("ple_w_gate", i, 0)}[key]


def _layer_shards(tree, i):
    out = {}
    for key in LAYER_MATRICES:
        name, idx, _ = _layer_entry(i, key)
        out[key] = tree[name][idx]
    return out


def _gather_early(w):
    small = _pack_small(w)
    terms, rest = [], small
    for _ in range(3):
        term = rest.astype(MXU_DTYPE)
        terms.append(term)
        rest = rest - term.astype(f32)
    outs = all_gather([_layer_shards(w, 0)["w_in"].astype(MXU_DTYPE), jnp.concatenate(terms, axis=0)],
                      name="gather_weights")
    w_in = _unshard(outs[0], outs[0].shape[1:], _layer_entry(0, "w_in")[2])
    wl, wtl = {"w_in": w_in}, {"w_in": w_in.T}
    full = {k: w[k] for k in REPLICATED}
    t = outs[-1].astype(f32)
    nr = small.shape[0]
    vals = (t[:, :nr] + t[:, nr:2 * nr] + t[:, 2 * nr:]).reshape(N_DEV, -1)
    off = 0
    for k, axis in SMALL_SHARDED.items():
        size = math.prod(w[k].shape)
        full[k] = _unshard(vals[:, off:off + size], w[k].shape, axis)
        off += size
    return full, wl, wtl


class Overlap:
    LATE = [(0, key) for key in LAYER_MATRICES if key != "w_in"] + [(1, key) for key in LAYER_MATRICES]
    EARLY_GRADS = [(0, key) for key in LAYER_MATRICES if key != "w_in"]

    def __init__(self, w):
        self.w = w
        self.sums, self.from_chips, self.parts = {}, {}, {}

    def attention_fwd_rider(self):
        shards = [_layer_shards(self.w, 0), _layer_shards(self.w, 1)]
        return gather_rider([shards[i][key].astype(MXU_DTYPE) for i, key in self.LATE])

    def weights_arrived(self, carried, wl, wtl):
        wl = [dict(wl[0]), {}]
        wtl = [dict(wtl[0]), {}]
        for (i, key), g8 in zip(self.LATE, carried):
            wl[i][key] = _unshard(g8, g8.shape[1:], _layer_entry(i, key)[2])
            wtl[i][key] = wl[i][key].T
        return wl, wtl

    def _to_sibling(self, ids, layer_grads):
        for i, key in ids:
            self.parts[i, key] = _to_shards(layer_grads[key], _layer_entry(i, key)[2])
        return sibling_rider([self.parts[e] for e in ids])

    def _add(self, ids, carried):
        core = jnp.reshape(lax.axis_index("c"), (1,)).astype(jnp.int32)
        for (i, key), got in zip(ids, carried):
            self.sums[i, key] = add_sibling_parts(self.parts[i, key], got, core, name=f"add_sibling_l{i}_{key}")

    def mlp_bwd_rider(self, layer1_grads):
        return self._to_sibling([(1, key) for key in LAYER_MATRICES], layer1_grads)

    def after_mlp_bwd(self, carried):
        self._add([(1, key) for key in LAYER_MATRICES], carried)

    def lru_bwd_rider(self, layer0_grads):
        return self._to_sibling(self.EARLY_GRADS, layer0_grads)

    def attention_bwd_rider(self, carried):
        self._add(self.EARLY_GRADS, carried)
        self.travelling = list(self.sums)
        return chips_rider([self.sums[e][1] for e in self.travelling])

    def after_attention_bwd(self, carried):
        for e, got in zip(self.travelling, carried):
            self.from_chips[e] = got


def _pack_replicated(tree):
    return _pack_rows(jnp.concatenate([tree[k].astype(f32).reshape(-1) for k in REPLICATED]), 8)


def kernel(x, p, ev_w_in, ev_lru_conv_w, ev_lru_conv_b, ev_lru_gate_a_w, ev_lru_gate_a_b, ev_lru_gate_x_w, ev_lru_gate_x_b, ev_lru_lambda, ev_w_out, od_w_in, od_ssm_conv_w, od_ssm_conv_b, od_ssm_dt_bias, od_ssm_a_log, od_ssm_d, od_ssm_norm, od_cm_conv_w, od_cm_conv_b, od_cm_ln_g, od_cm_ln_b, od_w_out, norm_mix_pre, norm_mix_post, norm_mlp_pre, norm_mlp_post, norm_ple, mlp_w1, mlp_w2, ple_w_proj, ple_w_gate, loss_target, m_ev_w_in, m_ev_lru_conv_w, m_ev_lru_conv_b, m_ev_lru_gate_a_w, m_ev_lru_gate_a_b, m_ev_lru_gate_x_w, m_ev_lru_gate_x_b, m_ev_lru_lambda, m_ev_w_out, m_od_w_in, m_od_ssm_conv_w, m_od_ssm_conv_b, m_od_ssm_dt_bias, m_od_ssm_a_log, m_od_ssm_d, m_od_ssm_norm, m_od_cm_conv_w, m_od_cm_conv_b, m_od_cm_ln_g, m_od_cm_ln_b, m_od_w_out, m_norm_mix_pre, m_norm_mix_post, m_norm_mlp_pre, m_norm_mlp_post, m_norm_ple, m_mlp_w1, m_mlp_w2, m_ple_w_proj, m_ple_w_gate, v_ev_w_in, v_ev_lru_conv_w, v_ev_lru_conv_b, v_ev_lru_gate_a_w, v_ev_lru_gate_a_b, v_ev_lru_gate_x_w, v_ev_lru_gate_x_b, v_ev_lru_lambda, v_ev_w_out, v_od_w_in, v_od_ssm_conv_w, v_od_ssm_conv_b, v_od_ssm_dt_bias, v_od_ssm_a_log, v_od_ssm_d, v_od_ssm_norm, v_od_cm_conv_w, v_od_cm_conv_b, v_od_cm_ln_g, v_od_cm_ln_b, v_od_w_out, v_norm_mix_pre, v_norm_mix_post, v_norm_mlp_pre, v_norm_mlp_post, v_norm_ple, v_mlp_w1, v_mlp_w2, v_ple_w_proj, v_ple_w_gate):
    ws = [ev_w_in, ev_lru_conv_w, ev_lru_conv_b, ev_lru_gate_a_w, ev_lru_gate_a_b, ev_lru_gate_x_w, ev_lru_gate_x_b, ev_lru_lambda, ev_w_out, od_w_in, od_ssm_conv_w, od_ssm_conv_b, od_ssm_dt_bias, od_ssm_a_log, od_ssm_d, od_ssm_norm, od_cm_conv_w, od_cm_conv_b, od_cm_ln_g, od_cm_ln_b, od_w_out, norm_mix_pre, norm_mix_post, norm_mlp_pre, norm_mlp_post, norm_ple, mlp_w1, mlp_w2, ple_w_proj, ple_w_gate]
    ms = [m_ev_w_in, m_ev_lru_conv_w, m_ev_lru_conv_b, m_ev_lru_gate_a_w, m_ev_lru_gate_a_b, m_ev_lru_gate_x_w, m_ev_lru_gate_x_b, m_ev_lru_lambda, m_ev_w_out, m_od_w_in, m_od_ssm_conv_w, m_od_ssm_conv_b, m_od_ssm_dt_bias, m_od_ssm_a_log, m_od_ssm_d, m_od_ssm_norm, m_od_cm_conv_w, m_od_cm_conv_b, m_od_cm_ln_g, m_od_cm_ln_b, m_od_w_out, m_norm_mix_pre, m_norm_mix_post, m_norm_mlp_pre, m_norm_mlp_post, m_norm_ple, m_mlp_w1, m_mlp_w2, m_ple_w_proj, m_ple_w_gate]
    vs = [v_ev_w_in, v_ev_lru_conv_w, v_ev_lru_conv_b, v_ev_lru_gate_a_w, v_ev_lru_gate_a_b, v_ev_lru_gate_x_w, v_ev_lru_gate_x_b, v_ev_lru_lambda, v_ev_w_out, v_od_w_in, v_od_ssm_conv_w, v_od_ssm_conv_b, v_od_ssm_dt_bias, v_od_ssm_a_log, v_od_ssm_d, v_od_ssm_norm, v_od_cm_conv_w, v_od_cm_conv_b, v_od_cm_ln_g, v_od_cm_ln_b, v_od_w_out, v_norm_mix_pre, v_norm_mix_post, v_norm_mlp_pre, v_norm_mlp_post, v_norm_ple, v_mlp_w1, v_mlp_w2, v_ple_w_proj, v_ple_w_gate]
    w = dict(zip(W_NAMES, ws))
    m = dict(zip(W_NAMES, ms))
    v = dict(zip(W_NAMES, vs))
    full, wl0, wtl0 = _gather_early(w)
    comm = Overlap(w)
    loss_local, grad_x, grads, layer_grads = local_step(x, p, loss_target, full, [wl0, None], [wtl0, None], comm)
    loss = lax.psum(loss_local, ("x", "y", "c"))
    return (loss, grad_x, *_reduce_and_update(grads, layer_grads, w, m, v, comm))


def _reduce_and_update(grads, layer_grads, w, m, v, comm):
    mx, my, mc = _mesh_pos()

    entries = [(i, key) for i in range(len(layer_grads)) for key in LAYER_MATRICES]
    left = [e for e in entries if e not in comm.from_chips]
    parts = [_to_shards(layer_grads[i][key], _layer_entry(i, key)[2]) for i, key in left]
    small = jnp.concatenate([_to_shards(grads[k], axis).reshape(N_DEV, -1) for k, axis in SMALL_SHARDED.items()],
                            axis=1)
    small_rows = _pack_small(w).shape[0]
    small = jnp.pad(small, ((0, 0), (0, small_rows * LANES - small.shape[1]))).reshape(N_DEV, small_rows, LANES)
    parts.append(small)
    from_sibling = scatter_to_sibling(parts, name="scatter_sibling")
    core = jnp.reshape(mc, (1,)).astype(jnp.int32)
    sums = [add_sibling_parts(a, b, core, name=f"add_sibling_{i}") for i, (a, b) in enumerate(zip(parts, from_sibling))]
    from_chips = scatter_to_chips([s[1] for s in sums], name="scatter_chips")
    all_sums = {**comm.sums, **dict(zip(left, sums[:-1]))}
    all_from_chips = {**comm.from_chips, **dict(zip(left, from_chips[:-1]))}
    chip = jnp.reshape(2 * mx + my, (1,)).astype(jnp.int32)
    per_layer = []
    for i in range(len(layer_grads)):
        ws, ms, vs = _layer_shards(w, i), _layer_shards(m, i), _layer_shards(v, i)
        per_layer.append({key: adamw_sharded(all_sums[i, key][0], all_from_chips[i, key], chip, ws[key], ms[key],
                                             vs[key], name=f"adamw_l{i}_{key}") for key in LAYER_MATRICES})
    g_sh, d_sh, m_sh, v_sh = {}, {}, {}, {}
    for which, tree in enumerate((g_sh, d_sh, m_sh, v_sh)):
        for i in range(len(per_layer)):
            for key in LAYER_MATRICES:
                name, idx, _ = _layer_entry(i, key)
                tree.setdefault(name, {})[idx] = per_layer[i][key][which]
        for name in BIG_SHARDED:
            tree[name] = jnp.stack([tree[name][idx] for idx in sorted(tree[name])], axis=0)
    outs = adamw_sharded(sums[-1][0], from_chips[-1], chip, _pack_small(w), _pack_small(m), _pack_small(v),
                         name="adamw_small")
    small_shapes = {k: w[k].shape for k in SMALL_SHARDED}
    for tree, o in zip((g_sh, d_sh, m_sh, v_sh), outs):
        tree.update(_unpack(o.reshape(-1), small_shapes))

    rep_parts, = all_gather([_pack_replicated(grads)], name="gather_replicated_grads")
    outs = adamw_replicated(rep_parts, _pack_replicated(w), _pack_replicated(m), _pack_replicated(v),
                            name="adamw_replicated")
    rep_shapes = {k: w[k].shape for k in REPLICATED}
    g_rp, d_rp, m_rp, v_rp = [_unpack(o.reshape(-1), rep_shapes) for o in outs]

    pick = lambda sh, rp: [sh[k] if k in SHARDED else rp[k] for k in W_NAMES]
    return [*pick(g_sh, g_rp), *pick(d_sh, d_rp), *pick(m_sh, m_rp), *pick(v_sh, v_rp)]
```

```python
import math

import jax
import jax.numpy as jnp
from jax import lax
from jax.experimental import pallas as pl
from jax.experimental.pallas import tpu as pltpu

f32 = jnp.float32
MXU_DTYPE = jnp.bfloat16

EPS = 1e-6
LRU_WIDTH = 512
LRU_HEADS = 8
LRU_CONV = 4
LRU_C = 8.0
SB_WIDTH = 512
SB_HEAD_DIM = 64
SSM_WIDTH = 1024
SSM_HEADS = 16
SSM_HEAD_DIM = 64
SSM_GROUPS = 2
SSM_STATE = 128
SSM_CONV = 4
SSM_CHUNK = 128
SSM_XBC = SSM_WIDTH + 2 * SSM_GROUPS * SSM_STATE
CONF_WIDTH = 512
CONF_KERNEL = 31
LANES = 128
N_DEV = 8

ADAM_LR = 0.001
ADAM_B1 = 0.9
ADAM_B2 = 0.999
ADAM_EPS = 1e-08
ADAM_WD = 0.01
ADAM_STEP = 10

VMEM_LIMIT = 56 * 1024 * 1024


def _cparams(*sem):
    return pltpu.CompilerParams(dimension_semantics=sem, vmem_limit_bytes=VMEM_LIMIT)


def _mm(a, b):
    return jnp.dot(a.astype(MXU_DTYPE), b.astype(MXU_DTYPE), preferred_element_type=f32)


def _mm_nt(a, b):
    return lax.dot_general(a.astype(MXU_DTYPE), b.astype(MXU_DTYPE), (((1,), (1,)), ((), ())),
                           preferred_element_type=f32)


def _mm_tn(a, b):
    return lax.dot_general(a.astype(MXU_DTYPE), b.astype(MXU_DTYPE), (((0,), (0,)), ((), ())),
                           preferred_element_type=f32)


def _mm_exact(a, b):
    return jnp.dot(a, b, preferred_element_type=f32, precision=lax.Precision.HIGHEST)


@jax.custom_vjp
def dmm(a, b):
    return _mm(a, b)


def _dmm_fwd(a, b):
    return _mm(a, b), (a, b)


def _dmm_bwd(res, g):
    a, b = res
    return _mm_nt(g, b), _mm_tn(a, g)


dmm.defvjp(_dmm_fwd, _dmm_bwd)


@jax.custom_vjp
def dmm_nt(a, b):
    return _mm_nt(a, b)


def _dmm_nt_fwd(a, b):
    return _mm_nt(a, b), (a, b)


def _dmm_nt_bwd(res, g):
    a, b = res
    return _mm(g, b), _mm_tn(g, a)


dmm_nt.defvjp(_dmm_nt_fwd, _dmm_nt_bwd)


@jax.custom_vjp
def dmm_tn(a, b):
    return _mm_tn(a, b)


def _dmm_tn_fwd(a, b):
    return _mm_tn(a, b), (a, b)


def _dmm_tn_bwd(res, g):
    a, b = res
    return _mm_nt(b, g), _mm(a, g)


dmm_tn.defvjp(_dmm_tn_fwd, _dmm_tn_bwd)


def _rms(x, g):
    r = lax.rsqrt(jnp.mean(x * x, axis=-1, keepdims=True) + EPS)
    return x * r * g


def _rms_bwd(dy, x, g):
    r = lax.rsqrt(jnp.mean(x * x, axis=-1, keepdims=True) + EPS)
    dyg = dy * g
    dx = r * dyg - x * (r * r * r * jnp.mean(dyg * x, axis=-1, keepdims=True))
    return dx, dy * x * r


def _tok(tm, n):
    return pl.BlockSpec((tm, n), lambda i: (i, 0))


def _whole(shape):
    nd = len(shape)
    return pl.BlockSpec(tuple(shape), lambda i: (0,) * nd)


def _acc_rows(ref, val):
    s = jnp.sum(val, axis=0, keepdims=True)

    @pl.when(pl.program_id(0) == 0)
    def _():
        ref[...] = s

    @pl.when(pl.program_id(0) != 0)
    def _():
        ref[...] += s


TOKEN_TILE = 512
WEIGHT_GRAD_TOKENS = 1024


class Part:
    def __init__(self, whole, start, size, axis):
        self.whole, self.start, self.size, self.axis = whole, start, size, axis
        self.shape = tuple(size if a == axis else n for a, n in enumerate(whole.shape))


def _weights(ws):
    wholes, readers = [], []
    for w in ws:
        arr = w.whole if isinstance(w, Part) else w
        idx = next((i for i, a in enumerate(wholes) if a is arr), None)
        if idx is None:
            wholes.append(arr)
            idx = len(wholes) - 1
        if isinstance(w, Part):
            rows = pl.ds(w.start, w.size) if w.axis == 0 else slice(None)
            cols = pl.ds(w.start, w.size) if w.axis == 1 else slice(None)
            readers.append(lambda refs, idx=idx, rows=rows, cols=cols: refs[idx][rows, cols])
        else:
            readers.append(lambda refs, idx=idx: refs[idx][...])
    return wholes, readers


def norm_matmul(h, g, ws, out_dtypes, name):
    t, d = h.shape
    tm = TOKEN_TILE
    wholes, readers = _weights(ws)
    nw = len(wholes)

    def body(h_ref, g_ref, *refs):
        hn = _rms(h_ref[...], g_ref[...]).astype(MXU_DTYPE)
        for read, o_ref in zip(readers, refs[nw:]):
            o_ref[...] = jnp.dot(hn, read(refs[:nw]), preferred_element_type=f32).astype(o_ref.dtype)

    return pl.pallas_call(
        body, name=name, grid=(t // tm,),
        in_specs=[_tok(tm, d), _whole(g.shape)] + [_whole(w.shape) for w in wholes],
        out_specs=[_tok(tm, w.shape[1]) for w in ws],
        out_shape=[jax.ShapeDtypeStruct((t, w.shape[1]), dt) for w, dt in zip(ws, out_dtypes)],
        compiler_params=_cparams("parallel"),
    )(h, g, *wholes)


def matmul_residual_norm(xs, ws, h, g, name, relu2=False):
    t, d = h.shape
    tm = TOKEN_TILE
    nx = len(xs)
    wholes, readers = _weights(ws)
    nw = len(wholes)

    def body(*refs):
        x_refs, w_refs = refs[:nx], refs[nx:nx + nw]
        h_ref, g_ref, ho_ref, m_ref = refs[nx + nw:]
        m = None
        for x_ref, read in zip(x_refs, readers):
            x = x_ref[...]
            if relu2:
                x = jnp.square(jnp.maximum(x.astype(f32), 0.0))
            part = jnp.dot(x.astype(MXU_DTYPE), read(w_refs), preferred_element_type=f32)
            m = part if m is None else m + part
        m_ref[...] = m.astype(m_ref.dtype)
        ho_ref[...] = h_ref[...] + _rms(m, g_ref[...])

    return pl.pallas_call(
        body, name=name, grid=(t // tm,),
        in_specs=[_tok(tm, x.shape[1]) for x in xs] + [_whole(w.shape) for w in wholes]
        + [_tok(tm, d), _whole(g.shape)],
        out_specs=[_tok(tm, d), _tok(tm, d)],
        out_shape=[jax.ShapeDtypeStruct((t, d), f32), jax.ShapeDtypeStruct((t, d), MXU_DTYPE)],
        compiler_params=_cparams("parallel"),
    )(*xs, *wholes, h, g)


def ple_forward(h, p, w_gate, w_proj, g, name, target=None):
    t, d = h.shape
    tm = TOKEN_TILE
    last = target is not None

    def body(h_ref, p_ref, wg_ref, wp_ref, g_ref, *refs):
        hh = h_ref[...]
        gl = jnp.dot(hh.astype(MXU_DTYPE), wg_ref[...], preferred_element_type=f32)
        emb = jnp.dot(p_ref[...].astype(MXU_DTYPE), wp_ref[...], preferred_element_type=f32)
        y = hh + _rms(jax.nn.sigmoid(gl) * emb, g_ref[...])
        if last:
            t_ref, l_ref, dy_ref, gl_ref, emb_ref = refs
            e = y - t_ref[...]
            dy_ref[...] = e * (1.0 / d)
            part = jnp.sum(jnp.sum(e * e, axis=1, keepdims=True), axis=0, keepdims=True) * (0.5 / d)
            _acc_rows(l_ref, jnp.broadcast_to(part, (1, LANES)))
        else:
            y_ref, gl_ref, emb_ref = refs
            y_ref[...] = y
        gl_ref[...] = gl.astype(gl_ref.dtype)
        emb_ref[...] = emb.astype(emb_ref.dtype)

    saved = [jax.ShapeDtypeStruct((t, d), MXU_DTYPE)] * 2
    in_specs = [_tok(tm, d), _tok(tm, p.shape[1]), _whole(w_gate.shape), _whole(w_proj.shape), _whole(g.shape)]
    if last:
        return pl.pallas_call(
            body, name=name, grid=(t // tm,),
            in_specs=in_specs + [_tok(tm, d)],
            out_specs=[_whole((1, LANES))] + [_tok(tm, d)] * 3,
            out_shape=[jax.ShapeDtypeStruct((1, LANES), f32), jax.ShapeDtypeStruct((t, d), f32)] + saved,
            compiler_params=_cparams("arbitrary"),
        )(h, p, w_gate, w_proj, g, target)
    return pl.pallas_call(
        body, name=name, grid=(t // tm,),
        in_specs=in_specs,
        out_specs=[_tok(tm, d)] * 3,
        out_shape=[jax.ShapeDtypeStruct((t, d), f32)] + saved,
        compiler_params=_cparams("parallel"),
    )(h, p, w_gate, w_proj, g)


def bwd_through_norm_in(dh, gs, wts, h, g, name):
    t, d = h.shape
    tm = TOKEN_TILE
    ng = len(gs)
    wholes, readers = _weights(wts)
    nw = len(wholes)

    def body(*refs):
        dh_ref = refs[0]
        g_refs, w_refs = refs[1:1 + ng], refs[1 + ng:1 + ng + nw]
        h_ref, gain_ref, dho_ref, dg_ref = refs[1 + ng + nw:]
        dhn = None
        for g_ref, read in zip(g_refs, readers):
            part = jnp.dot(g_ref[...].astype(MXU_DTYPE), read(w_refs), preferred_element_type=f32)
            dhn = part if dhn is None else dhn + part
        dx, dgr = _rms_bwd(dhn, h_ref[...], gain_ref[...])
        dho_ref[...] = dh_ref[...] + dx
        _acc_rows(dg_ref, dgr)

    return pl.pallas_call(
        body, name=name, grid=(t // tm,),
        in_specs=[_tok(tm, d)] + [_tok(tm, x.shape[1]) for x in gs] + [_whole(w.shape) for w in wholes]
        + [_tok(tm, d), _whole(g.shape)],
        out_specs=[_tok(tm, d), _whole((1, d))],
        out_shape=[jax.ShapeDtypeStruct((t, d), f32), jax.ShapeDtypeStruct((1, d), f32)],
        compiler_params=_cparams("arbitrary"),
    )(dh, *gs, *wholes, h, g)


def bwd_through_norm_out(dh, n, g, wts, out_dtypes, name, relu2_of=None, rider=None):
    t, d = n.shape
    tm = TOKEN_TILE
    nw = len(wts)
    wholes, readers = _weights(wts)
    nwh = len(wholes)
    has_a = relu2_of is not None

    def body(*refs):
        dh_ref, n_ref, gain_ref = refs[:3]
        w_refs = refs[3:3 + nwh]
        rest = refs[3 + nwh:]
        if has_a:
            a_ref, rest = rest[0], rest[1:]
        dn_ref, dx_refs, dg_ref = rest[0], rest[1:1 + nw], rest[1 + nw]
        dn, dgr = _rms_bwd(dh_ref[...], n_ref[...].astype(f32), gain_ref[...])
        dnb = dn.astype(MXU_DTYPE)
        dn_ref[...] = dnb.astype(dn_ref.dtype)
        for read, dx_ref in zip(readers, dx_refs):
            dx = jnp.dot(dnb, read(w_refs), preferred_element_type=f32)
            if has_a:
                dx = dx * (2.0 * jnp.maximum(a_ref[...].astype(f32), 0.0))
            dx_ref[...] = dx.astype(dx_ref.dtype)
        _acc_rows(dg_ref, dgr)

    ins = [dh, n, g, *wholes] + ([relu2_of] if has_a else [])
    in_specs = [_tok(tm, d), _tok(tm, d), _whole(g.shape)] + [_whole(w.shape) for w in wholes]
    if has_a:
        in_specs.append(_tok(tm, relu2_of.shape[1]))
    grid = (t // tm,)
    body, in_specs, out_specs, out_shape, scratch = _ride(
        rider, body, in_specs, [_tok(tm, d)] + [_tok(tm, w.shape[1]) for w in wts] + [_whole((1, d))],
        [jax.ShapeDtypeStruct((t, d), MXU_DTYPE)]
        + [jax.ShapeDtypeStruct((t, w.shape[1]), dt) for w, dt in zip(wts, out_dtypes)]
        + [jax.ShapeDtypeStruct((1, d), f32)], [], grid)
    outs = pl.pallas_call(
        body, name=name, grid=grid, in_specs=in_specs, out_specs=out_specs, out_shape=out_shape,
        scratch_shapes=scratch, compiler_params=_cparams("arbitrary"),
    )(*ins, *(rider.inputs if rider else []))
    if rider:
        return outs[0], list(outs[1:1 + nw]), outs[1 + nw], list(outs[2 + nw:])
    return outs[0], list(outs[1:1 + nw]), outs[1 + nw]


def ple_backward(dh3, h2, gl, emb, g, w_gate_t, name):
    t, d = h2.shape
    tm = TOKEN_TILE

    def body(dh_ref, gl_ref, emb_ref, gain_ref, wt_ref, dho_ref, dgl_ref, demb_ref, dg_ref):
        gate = jax.nn.sigmoid(gl_ref[...].astype(f32))
        emb = emb_ref[...].astype(f32)
        dge, dgr = _rms_bwd(dh_ref[...], gate * emb, gain_ref[...])
        demb_ref[...] = (dge * gate).astype(demb_ref.dtype)
        dgl = (dge * emb * gate * (1.0 - gate)).astype(MXU_DTYPE)
        dgl_ref[...] = dgl.astype(dgl_ref.dtype)
        dho_ref[...] = dh_ref[...] + jnp.dot(dgl, wt_ref[...], preferred_element_type=f32)
        _acc_rows(dg_ref, dgr)

    return pl.pallas_call(
        body, name=name, grid=(t // tm,),
        in_specs=[_tok(tm, d), _tok(tm, d), _tok(tm, d), _whole(g.shape), _whole(w_gate_t.shape)],
        out_specs=[_tok(tm, d), _tok(tm, d), _tok(tm, d), _whole((1, d))],
        out_shape=[jax.ShapeDtypeStruct((t, d), f32), jax.ShapeDtypeStruct((t, d), MXU_DTYPE),
                   jax.ShapeDtypeStruct((t, d), MXU_DTYPE), jax.ShapeDtypeStruct((1, d), f32)],
        compiler_params=_cparams("arbitrary"),
    )(dh3, gl, emb, g, w_gate_t)


def _largest_tile(n, cap):
    if n <= cap:
        return n
    return max(c for c in range(LANES, cap + 1, LANES) if n % c == 0)


def weight_grad(x, gout, name, prologue="none", gain=None):
    t, k = x.shape
    n = gout.shape[1]
    tt = WEIGHT_GRAD_TOKENS
    tn = _largest_tile(n, 1024)
    tk = k if prologue == "rms" else _largest_tile(k, 1024)
    has_gain = prologue == "rms"

    def body(*refs):
        if has_gain:
            x_ref, gain_ref, g_ref, o_ref = refs
        else:
            x_ref, g_ref, o_ref = refs
        x = x_ref[...].astype(f32)
        if prologue == "relu2":
            x = jnp.square(jnp.maximum(x, 0.0))
        elif prologue == "rms":
            x = _rms(x, gain_ref[...])
        part = _mm_tn(x, g_ref[...])

        @pl.when(pl.program_id(2) == 0)
        def _():
            o_ref[...] = part

        @pl.when(pl.program_id(2) != 0)
        def _():
            o_ref[...] += part

    in_specs = [pl.BlockSpec((tt, tk), lambda i, j, s: (s, i))]
    ins = [x]
    if has_gain:
        in_specs.append(pl.BlockSpec(gain.shape, lambda i, j, s: (0, 0)))
        ins.append(gain)
    in_specs.append(pl.BlockSpec((tt, tn), lambda i, j, s: (s, j)))
    ins.append(gout)
    return pl.pallas_call(
        body, name=name, grid=(k // tk, n // tn, t // tt),
        in_specs=in_specs,
        out_specs=pl.BlockSpec((tk, tn), lambda i, j, s: (i, j)),
        out_shape=jax.ShapeDtypeStruct((k, n), f32),
        compiler_params=_cparams("parallel", "parallel", "arbitrary"),
    )(*ins)


def weight_grads_of_norm(x, gain, gouts, name):
    t, k = x.shape
    tt = TOKEN_TILE
    ng = len(gouts)

    def body(x_ref, gain_ref, *refs):
        xn = _rms(x_ref[...], gain_ref[...]).astype(MXU_DTYPE)
        first = pl.program_id(0) == 0
        for g_ref, o_ref in zip(refs[:ng], refs[ng:]):
            _accum(o_ref, _mm_tn(xn, g_ref[...]), first)

    return pl.pallas_call(
        body, name=name, grid=(t // tt,),
        in_specs=[_tok(tt, k), _whole(gain.shape)] + [_tok(tt, g.shape[1]) for g in gouts],
        out_specs=[_whole((k, g.shape[1])) for g in gouts],
        out_shape=[jax.ShapeDtypeStruct((k, g.shape[1]), f32) for g in gouts],
        compiler_params=_cparams("arbitrary"),
    )(x, gain, *gouts)


SEQ_TILE = 256
HALO = 8


def _first_step():
    return jnp.logical_and(pl.program_id(0) == 0, pl.program_id(1) == 0)


def _accum(ref, val, first):
    @pl.when(first)
    def _():
        ref[...] = val

    @pl.when(jnp.logical_not(first))
    def _():
        ref[...] += val


def _softplus(x):
    return jnp.maximum(x, 0.0) + jnp.log1p(jnp.exp(-jnp.abs(x)))


def _neg_expm1(z):
    series = -z * (1.0 + z * (0.5 + z * (1.0 / 6.0 + z * (1.0 / 24.0 + z * (1.0 / 120.0)))))
    return jnp.where(z > -0.05, series, 1.0 - jnp.exp(z))


def _lru_gates(xc, ga, gab, gx, gxb, lam):
    r = jax.nn.sigmoid(dmm(xc, ga) + gab)
    i = jax.nn.sigmoid(dmm(xc, gx) + gxb)
    log_a = -LRU_C * r * _softplus(-lam)
    a = jnp.exp(log_a)
    u = jnp.sqrt(_neg_expm1(2.0 * log_a)) * (i * xc)
    return a, u


def _scan_down(a, u):
    n = a.shape[0]
    rows = lax.broadcasted_iota(jnp.int32, a.shape, 0)
    d = 1
    while d < n:
        keep = rows >= d
        a_s = jnp.where(keep, pltpu.roll(a, d, 0), 1.0)
        u_s = jnp.where(keep, pltpu.roll(u, d, 0), 0.0)
        u = a * u_s + u
        a = a * a_s
        d *= 2
    return a, u


def _scan_up(b, g):
    n = b.shape[0]
    rows = lax.broadcasted_iota(jnp.int32, b.shape, 0)
    d = 1
    while d < n:
        keep = rows < n - d
        b_s = jnp.where(keep, pltpu.roll(b, n - d, 0), 1.0)
        g_s = jnp.where(keep, pltpu.roll(g, n - d, 0), 0.0)
        g = g + b * g_s
        b = b * b_s
        d *= 2
    return g


def _seq_specs(ts, c, nt, reverse=False):
    per = ts // HALO

    def jj(j):
        return (nt - 1 - j) if reverse else j

    tile = pl.BlockSpec((1, ts, c), lambda b, j: (b, jj(j), 0))
    before = pl.BlockSpec((1, HALO, c), lambda b, j: (b, jnp.maximum(jj(j) * per - 1, 0), 0))
    after = pl.BlockSpec((1, HALO, c), lambda b, j: (b, jnp.minimum((jj(j) + 1) * per, nt * per - 1), 0))
    return tile, before, after


def _const2(shape):
    nd = len(shape)
    return pl.BlockSpec(tuple(shape), lambda b, j: (0,) * nd)


def lru_forward(xpre, gate, cw, cb, ga, gab, gx, gxb, lam, name):
    nb, ns, w = xpre.shape
    ts = SEQ_TILE
    nt = ns // ts
    tile, _, _ = _seq_specs(ts, w, nt)

    def body(xp_ref, gt_ref, cw_ref, cb_ref, ga_ref, gab_ref, gx_ref, gxb_ref, lam_ref,
             y_ref, xc_ref, hs_ref, xin, hcar):
        @pl.when(pl.program_id(1) == 0)
        def _():
            xin[0:HALO, :] = jnp.zeros((HALO, w), f32)
            hcar[...] = jnp.zeros_like(hcar)

        xin[HALO:HALO + ts, :] = xp_ref[0]
        xc = jnp.broadcast_to(cb_ref[...], (ts, w))
        for k in range(LRU_CONV):
            xc = xc + cw_ref[k:k + 1, :] * xin[pl.ds(HALO - LRU_CONV + 1 + k, ts), :]
        xin[0:HALO, :] = xin[ts:ts + HALO, :]
        a, u = _lru_gates(xc, ga_ref[...], gab_ref[...], gx_ref[...], gxb_ref[...], lam_ref[...])
        acum, h = _scan_down(a, u)
        h = h + acum * hcar[0:1, :]
        hcar[0:1, :] = h[ts - 1:ts, :]
        xc_ref[0] = xc
        hs_ref[0] = h
        y_ref[0] = (h * jax.nn.gelu(gt_ref[0])).astype(y_ref.dtype)

    params = [cw, cb, ga, gab, gx, gxb, lam]
    return pl.pallas_call(
        body, name=name, grid=(nb, nt),
        in_specs=[tile, tile] + [_const2(p.shape) for p in params],
        out_specs=[tile, tile, tile],
        out_shape=[jax.ShapeDtypeStruct((nb, ns, w), MXU_DTYPE), jax.ShapeDtypeStruct((nb, ns, w), f32),
                   jax.ShapeDtypeStruct((nb, ns, w), f32)],
        scratch_shapes=[pltpu.VMEM((ts + HALO, w), f32), pltpu.VMEM((HALO, w), f32)],
        compiler_params=_cparams("arbitrary", "arbitrary"),
    )(xpre, gate, *params)


def lru_backward(dy, xpre, gate, xc, hs, cw, cb, ga, gab, gx, gxb, lam, name, rider=None):
    nb, ns, w = xpre.shape
    ts = SEQ_TILE
    nt = ns // ts
    tile, before, _ = _seq_specs(ts, w, nt, reverse=True)

    def body(dy_ref, xp_ref, xpb_ref, gt_ref, xc_ref, hs_ref, hsb_ref,
             cw_ref, cb_ref, ga_ref, gab_ref, gx_ref, gxb_ref, lam_ref,
             dxp_ref, dgt_ref, dcw_ref, dcb_ref, dga_ref, dgab_ref, dgx_ref, dgxb_ref, dlam_ref,
             dxc_ext, gcar, xin):
        j = pl.program_id(1)
        first = _first_step()
        at_seq_start = j == nt - 1

        @pl.when(j == 0)
        def _():
            dxc_ext[ts:ts + HALO, :] = jnp.zeros((HALO, w), f32)
            gcar[...] = jnp.zeros_like(gcar)

        gt = gt_ref[0]
        h = hs_ref[0]
        dyv = dy_ref[0].astype(f32)
        gl, gelu_vjp = jax.vjp(jax.nn.gelu, gt)
        dgt_ref[0] = gelu_vjp(dyv * h)[0].astype(dgt_ref.dtype)
        dh = dyv * gl

        (a, _), gates_vjp = jax.vjp(_lru_gates, xc_ref[0], ga_ref[...], gab_ref[...], gx_ref[...], gxb_ref[...],
                                    lam_ref[...])
        rows = lax.broadcasted_iota(jnp.int32, (ts, w), 0)
        dh = dh + jnp.where(rows == ts - 1, gcar[0:1, :], 0.0)
        b = pltpu.roll(a, ts - 1, 0)
        g = _scan_up(b, dh)
        gcar[0:1, :] = a[0:1, :] * g[0:1, :]
        hprev_row = jnp.where(at_seq_start, 0.0, hsb_ref[0][HALO - 1:HALO, :])
        hprev = jnp.where(rows == 0, hprev_row, pltpu.roll(h, 1, 0))
        dxc, dga, dgab, dgx, dgxb, dlam = gates_vjp((g * hprev, g))

        _accum(dga_ref, dga, first)
        _accum(dgx_ref, dgx, first)
        _accum(dgab_ref, dgab, first)
        _accum(dgxb_ref, dgxb, first)
        _accum(dlam_ref, dlam, first)
        _accum(dcb_ref, jnp.sum(dxc, axis=0, keepdims=True), first)

        dxc_ext[0:ts, :] = dxc
        dxp = jnp.zeros((ts, w), f32)
        for k in range(LRU_CONV):
            dxp = dxp + cw_ref[k:k + 1, :] * dxc_ext[pl.ds(LRU_CONV - 1 - k, ts), :]
        dxp_ref[0] = dxp.astype(dxp_ref.dtype)
        dxc_ext[ts:ts + HALO, :] = dxc[0:HALO, :]

        xin[0:HALO, :] = jnp.where(at_seq_start, 0.0, xpb_ref[0])
        xin[HALO:HALO + ts, :] = xp_ref[0]
        dcw_rows = [jnp.sum(dxc * xin[pl.ds(HALO - LRU_CONV + 1 + k, ts), :], axis=0, keepdims=True)
                    for k in range(LRU_CONV)]
        dcw_rows += [jnp.zeros((1, w), f32)] * (HALO - LRU_CONV)
        _accum(dcw_ref, jnp.concatenate(dcw_rows, axis=0), first)

    params = [cw, cb, ga, gab, gx, gxb, lam]
    pshape = lambda p: jax.ShapeDtypeStruct(p.shape, f32)
    grid = (nb, nt)
    n_main = 3 + len(params) - 1
    body, in_specs, out_specs, out_shape, scratch = _ride(
        rider, body, [tile, tile, before, tile, tile, tile, before] + [_const2(p.shape) for p in params],
        [tile, tile, _const2((HALO, w))] + [_const2(p.shape) for p in params[1:]],
        [jax.ShapeDtypeStruct((nb, ns, w), MXU_DTYPE), jax.ShapeDtypeStruct((nb, ns, w), MXU_DTYPE),
         jax.ShapeDtypeStruct((HALO, w), f32)] + [pshape(p) for p in params[1:]],
        [pltpu.VMEM((ts + HALO, w), f32), pltpu.VMEM((HALO, w), f32), pltpu.VMEM((ts + HALO, w), f32)], grid)
    outs = pl.pallas_call(
        body, name=name, grid=grid, in_specs=in_specs, out_specs=out_specs, out_shape=out_shape,
        scratch_shapes=scratch, compiler_params=_cparams("arbitrary", "arbitrary"),
    )(dy, xpre, xpre, gate, xc, hs, hs, *params, *(rider.inputs if rider else []))
    return (list(outs[:n_main]), list(outs[n_main:])) if rider else outs


SB_TILE = 256


def _split_dot(x, m):
    hi = x.astype(MXU_DTYPE)
    lo = (x - hi.astype(f32)).astype(MXU_DTYPE)
    return jnp.dot(hi, m, preferred_element_type=f32) + jnp.dot(lo, m, preferred_element_type=f32)


def _suffix_matrices(n):
    r = lax.broadcasted_iota(jnp.int32, (n, n), 0)
    c = lax.broadcasted_iota(jnp.int32, (n, n), 1)
    return (r > c).astype(MXU_DTYPE), (r >= c).astype(MXU_DTYPE)


LOG2E = 1.4426950408889634


def _sb_logits(qh, kb, strict):
    z = _mm_nt(qh, kb)
    ls = jnp.minimum(z, 0.0) - jnp.log2(1.0 + jnp.exp2(-jnp.abs(z)))
    lk = ls - z
    if strict is not None:
        lk = jnp.where(strict, lk, 0.0)
    return ls, lk


def _head_masked(x, dtype):
    lane = lax.broadcasted_iota(jnp.int32, x.shape, 1)
    return (jnp.where(lane < SB_HEAD_DIM, x, 0.0).astype(dtype), jnp.where(lane >= SB_HEAD_DIM, x, 0.0).astype(dtype))


def _stack_heads(dst, x, tq):
    x0, x1 = _head_masked(x, dst.dtype)
    for blk in range(dst.shape[0]):
        dst[blk, 0:tq, :] = x0[blk * tq:(blk + 1) * tq]
        dst[blk, tq:2 * tq, :] = x1[blk * tq:(blk + 1) * tq]


def _strict_mask(tq):
    rr = lax.broadcasted_iota(jnp.int32, (2 * tq, tq), 0)
    cc = lax.broadcasted_iota(jnp.int32, (2 * tq, tq), 1)
    return cc < jnp.where(rr >= tq, rr - tq, rr)


def _sb_specs(ns):
    npair = SB_WIDTH // LANES
    q = pl.BlockSpec((1, ns, LANES), lambda b, p: (b, 0, p))
    k = pl.BlockSpec((1, ns, LANES), lambda b, p: (b, 0, npair + p))
    v = pl.BlockSpec((1, ns, LANES), lambda b, p: (b, 0, 2 * npair + p))
    return q, k, v, npair


def sb_forward(qkv, name, rider=None):
    nb, ns, _ = qkv.shape
    tq = SB_TILE
    nq = ns // tq
    qspec, kspec, vspec, npair = _sb_specs(ns)

    def body(q_ref, k_ref, v_ref, o_ref, qs, ks, vs, acc):
        scale = 1.0 / math.sqrt(SB_HEAD_DIM)
        _stack_heads(qs, q_ref[0] * (scale * LOG2E), tq)
        ks[...] = k_ref[0].astype(MXU_DTYPE)
        _stack_heads(vs, v_ref[0], tq)
        mx, _ = _suffix_matrices(tq)
        strict = _strict_mask(tq)

        def step(q2, blks, r2, masked):
            kbs = [ks[pl.ds(pl.multiple_of(b * tq, tq), tq), :] for b in blks]
            lg = [_sb_logits(q2, kb, strict if masked else None) for kb in kbs]
            sums = [jnp.dot(lk.astype(MXU_DTYPE), mx, preferred_element_type=f32) for _, lk in lg]
            total = None
            for (ls, lk), s, b in zip(lg, sums, blks):
                a = r2 + s
                w = jnp.exp2(ls + a)
                if masked:
                    w = jnp.where(strict, w, 0.0)
                wb = w.astype(MXU_DTYPE)
                part = (jnp.dot(wb[:tq], vs[b, 0:tq, :], preferred_element_type=f32)
                        + jnp.dot(wb[tq:], vs[b, tq:2 * tq, :], preferred_element_type=f32))
                total = part if total is None else total + part
                r2 = a[:, 0:1] + lk[:, 0:1]
            acc[...] += total
            return r2

        def q_block(qi, carry):
            acc[...] = jnp.zeros_like(acc)
            q2 = qs[qi]
            r2 = step(q2, [qi], jnp.zeros((2 * tq, 1), f32), True)
            r2 = lax.fori_loop(0, lax.shift_right_logical(qi, 2),
                               lambda i, r: step(q2, [qi - 1 - 4 * i - u for u in range(4)], r, False), r2)
            r2 = lax.cond(jnp.bitwise_and(qi, 2) == 2,
                          lambda r: step(q2, [jnp.bitwise_and(qi, 3) - 1, jnp.bitwise_and(qi, 3) - 2], r, False),
                          lambda r: r, r2)
            lax.cond(jnp.bitwise_and(qi, 1) == 1, lambda r: step(q2, [0], r, False), lambda r: r, r2)
            o_ref[0, pl.ds(pl.multiple_of(qi * tq, tq), tq), :] = acc[...]
            return carry

        lax.fori_loop(0, nq, q_block, 0)

    grid = (nb, npair)
    body, in_specs, out_specs, out_shape, scratch = _ride(
        rider, body, [qspec, kspec, vspec], [pl.BlockSpec((1, ns, LANES), lambda b, p: (b, 0, p))],
        [jax.ShapeDtypeStruct((nb, ns, SB_WIDTH), f32)],
        [pltpu.VMEM((nq, 2 * tq, LANES), MXU_DTYPE), pltpu.VMEM((ns, LANES), MXU_DTYPE),
         pltpu.VMEM((nq, 2 * tq, LANES), MXU_DTYPE), pltpu.VMEM((tq, LANES), f32)], grid)
    outs = pl.pallas_call(
        body, name=name, grid=grid, in_specs=in_specs, out_specs=out_specs, out_shape=out_shape,
        scratch_shapes=scratch,
        compiler_params=_cparams("arbitrary", "arbitrary") if rider else _cparams("parallel", "parallel"),
    )(qkv, qkv, qkv, *(rider.inputs if rider else []))
    return (outs[0], list(outs[1:])) if rider else outs[0]


def sb_backward(qkv, o, do, name, rider=None):
    nb, ns, _ = qkv.shape
    tq = SB_TILE
    nq = ns // tq
    qspec, kspec, vspec, npair = _sb_specs(ns)
    ospec = pl.BlockSpec((1, ns, LANES), lambda b, p: (b, 0, p))

    def body(q_ref, k_ref, v_ref, o_ref, do_ref, dq_ref, dk_ref, dv_ref, qs, ks, kcat, vs, dos, dqacc, dkacc, dvacc):
        scale = 1.0 / math.sqrt(SB_HEAD_DIM)
        _stack_heads(qs, q_ref[0] * (scale * LOG2E), tq)
        ks[...] = k_ref[0].astype(MXU_DTYPE)
        _stack_heads(kcat, k_ref[0], tq)
        vs[...] = v_ref[0].astype(MXU_DTYPE)
        _stack_heads(dos, do_ref[0].astype(f32), tq)
        dkacc[...] = jnp.zeros_like(dkacc)
        dvacc[...] = jnp.zeros_like(dvacc)
        mx, mi = _suffix_matrices(tq)
        strict = _strict_mask(tq)

        def step(q2, do2, q2t, do2t, dtot2, blks, carry, masked):
            r2, g2 = carry
            k0s = [pl.multiple_of(b * tq, tq) for b in blks]
            lg = [_sb_logits(q2, ks[pl.ds(k0, tq), :], strict if masked else None) for k0 in k0s]
            dws = [_mm_nt(do2, vs[pl.ds(k0, tq), :]) for k0 in k0s]
            sums = [jnp.dot(lk.astype(MXU_DTYPE), mx, preferred_element_type=f32) for _, lk in lg]
            wbs, es = [], []
            for (ls, lk), s in zip(lg, sums):
                a = r2 + s
                w = jnp.exp2(ls + a)
                if masked:
                    w = jnp.where(strict, w, 0.0)
                wbs.append(w.astype(MXU_DTYPE))
                r2 = a[:, 0:1] + lk[:, 0:1]
            es = [wb.astype(f32) * dw for wb, dw in zip(wbs, dws)]
            esums = [_split_dot(e, mi) for e in es]
            dq = None
            for (ls, _), e, esum, wb, b, k0 in zip(lg, es, esums, wbs, blks, k0s):
                esuf = g2 + esum
                beta = jnp.exp2(ls)
                dz = e - beta * (e + (dtot2 - esuf))
                if masked:
                    dz = jnp.where(strict, dz, 0.0)
                dzb = dz.astype(MXU_DTYPE)
                part = (jnp.dot(dzb[:tq], kcat[b, 0:tq, :], preferred_element_type=f32)
                        + jnp.dot(dzb[tq:], kcat[b, tq:2 * tq, :], preferred_element_type=f32))
                dq = part if dq is None else dq + part
                dkacc[:, pl.ds(k0, tq)] += jnp.dot(q2t, dzb, preferred_element_type=f32)
                dvacc[:, pl.ds(k0, tq)] += jnp.dot(do2t, wb, preferred_element_type=f32)
                g2 = esuf[:, 0:1]
            dqacc[...] += dq
            return r2, g2

        def q_block(qi, carry):
            dqacc[...] = jnp.zeros_like(dqacc)
            q2, do2 = qs[qi], dos[qi]
            q2t, do2t = q2.T, do2.T
            ov = o_ref[0, pl.ds(pl.multiple_of(qi * tq, tq), tq), :]
            dtot2 = jnp.sum(do2.astype(f32) * jnp.concatenate([ov, ov], axis=0), axis=1, keepdims=True)
            zero = jnp.zeros((2 * tq, 1), f32)
            args = (q2, do2, q2t, do2t, dtot2)
            c = step(*args, [qi], (zero, zero), True)
            c = lax.fori_loop(0, lax.shift_right_logical(qi, 2),
                              lambda i, c: step(*args, [qi - 1 - 4 * i - u for u in range(4)], c, False), c)
            c = lax.cond(jnp.bitwise_and(qi, 2) == 2,
                         lambda c: step(*args, [jnp.bitwise_and(qi, 3) - 1, jnp.bitwise_and(qi, 3) - 2], c, False),
                         lambda c: c, c)
            lax.cond(jnp.bitwise_and(qi, 1) == 1, lambda c: step(*args, [0], c, False), lambda c: c, c)
            dq_ref[0, pl.ds(pl.multiple_of(qi * tq, tq), tq), :] = (dqacc[...] * scale).astype(dq_ref.dtype)
            return carry

        lax.fori_loop(0, nq, q_block, 0)
        dk_ref[0] = (dkacc[...].T * (1.0 / LOG2E)).astype(dk_ref.dtype)
        dv_ref[0] = dvacc[...].T.astype(dv_ref.dtype)

    dshape = jax.ShapeDtypeStruct((nb, ns, SB_WIDTH), MXU_DTYPE)
    stacked = pltpu.VMEM((nq, 2 * tq, LANES), MXU_DTYPE)
    flat = pltpu.VMEM((ns, LANES), MXU_DTYPE)
    grid = (nb, npair)
    body, in_specs, out_specs, out_shape, scratch = _ride(
        rider, body, [qspec, kspec, vspec, ospec, ospec], [ospec, ospec, ospec], [dshape, dshape, dshape],
        [stacked, flat, stacked, flat, stacked,
         pltpu.VMEM((tq, LANES), f32), pltpu.VMEM((LANES, ns), f32), pltpu.VMEM((LANES, ns), f32)], grid)
    outs = pl.pallas_call(
        body, name=name, grid=grid, in_specs=in_specs, out_specs=out_specs, out_shape=out_shape,
        scratch_shapes=scratch,
        compiler_params=_cparams("arbitrary", "arbitrary") if rider else _cparams("parallel", "parallel"),
    )(qkv, qkv, qkv, o, do, *(rider.inputs if rider else []))
    return (list(outs[:3]), list(outs[3:])) if rider else list(outs)


SSM_PAIRS = SSM_HEADS // 2
PAIRS_PER_GROUP = SSM_PAIRS // SSM_GROUPS
GROUP_WIDTH = SSM_WIDTH // SSM_GROUPS


def _silu(x):
    return x * jax.nn.sigmoid(x)


def _ssd_chunk(xs_pre, b_pre, c_pre, dt_raw, dt_raw_t, z, st, dt_bias_r, dt_bias_c, a_log_r, a_log_c, d_skip,
               gains):
    n = dt_raw.shape[0]
    rows = lax.broadcasted_iota(jnp.int32, (n, n), 0)
    cols = lax.broadcasted_iota(jnp.int32, (n, n), 1)
    tril = cols <= rows
    tri_l = tril.astype(f32)
    tri_u = (rows <= cols).astype(f32)
    lane = lax.broadcasted_iota(jnp.int32, (n, LANES), 1)
    sub = lax.broadcasted_iota(jnp.int32, (LANES, n), 0)

    dt = _softplus(dt_raw + dt_bias_r)
    a_r = -jnp.exp(a_log_r)
    da = dt * a_r
    acs = _mm_exact(tri_l, da)
    dt_t = _softplus(dt_raw_t + dt_bias_c)
    acs_t = _mm_exact(dt_t * (-jnp.exp(a_log_c)), tri_u)

    bs = [_silu(b) for b in b_pre]
    cs = [_silu(c) for c in c_pre]
    cb = [dmm_nt(cs[g], bs[g]) for g in range(SSM_GROUPS)]

    end = jnp.sum(da, axis=0, keepdims=True)
    lane_row = lax.broadcasted_iota(jnp.int32, (1, LANES), 1)
    first_head = lane < SSM_HEAD_DIM
    first_head_row = lane_row < SSM_HEAD_DIM

    def head_col(v, h):
        return jnp.sum(jnp.where((lane if v.shape[0] == n else lane_row) == h, v, 0.0), axis=1, keepdims=True)

    ys, st_new = [], []
    for p in range(SSM_PAIRS):
        g = p // PAIRS_PER_GROUP
        h0, h1 = 2 * p, 2 * p + 1
        xs = _silu(xs_pre[p])
        acols = [head_col(acs, h0), head_col(acs, h1)]
        dt_p = jnp.where(first_head, head_col(dt, h0), head_col(dt, h1))
        acs_p = jnp.where(first_head, acols[0], acols[1])
        end_p = jnp.where(first_head_row, head_col(end, h0), head_col(end, h1))
        dsk_p = jnp.where(first_head_row, head_col(d_skip, h0), head_col(d_skip, h1))
        xdt = xs * dt_p
        y = jnp.exp(acs_p) * dmm(cs[g], st[p])
        for hh in range(2):
            row = jnp.sum(jnp.where(sub == 2 * p + hh, acs_t, 0.0), axis=0, keepdims=True)
            decay = jnp.where(tril, jnp.exp(jnp.where(tril, acols[hh] - row, 0.0)), 0.0)
            head = first_head if hh == 0 else jnp.logical_not(first_head)
            y = y + dmm(cb[g] * decay, jnp.where(head, xdt, 0.0))
        st_new.append(jnp.exp(end_p) * st[p] + dmm_tn(bs[g], xdt * jnp.exp(end_p - acs_p)))
        ys.append(y + dsk_p * xs)
    out = []
    for g in range(SSM_GROUPS):
        yg = jnp.concatenate(ys[g * PAIRS_PER_GROUP:(g + 1) * PAIRS_PER_GROUP], axis=1) * _silu(z[g])
        out.append(_rms(yg, gains[g]))
    return out, st_new


def _ssd_chunk_inputs(xconv, dtr, z, st_ref, gain):
    xs_pre = [xconv[:, LANES * p:LANES * (p + 1)] for p in range(SSM_PAIRS)]
    b0 = SSM_WIDTH
    c0 = SSM_WIDTH + SSM_GROUPS * SSM_STATE
    b_pre = [xconv[:, b0 + SSM_STATE * g:b0 + SSM_STATE * (g + 1)] for g in range(SSM_GROUPS)]
    c_pre = [xconv[:, c0 + SSM_STATE * g:c0 + SSM_STATE * (g + 1)] for g in range(SSM_GROUPS)]
    zs = [z[:, GROUP_WIDTH * g:GROUP_WIDTH * (g + 1)] for g in range(SSM_GROUPS)]
    sts = [st_ref[p] for p in range(SSM_PAIRS)]
    gains = [gain[:, GROUP_WIDTH * g:GROUP_WIDTH * (g + 1)] for g in range(SSM_GROUPS)]
    return xs_pre, b_pre, c_pre, dtr, dtr.T, zs, sts, gains


def ssd_forward(xbc, dt_raw, z, cw, cb, dbr, dbc, alr, alc, dsk, gain, name):
    nb, ns, wx = xbc.shape
    ln = SSM_CHUNK
    nt = ns // ln
    tile = lambda c: pl.BlockSpec((1, ln, c), lambda b, j: (b, j, 0))
    st_spec = pl.BlockSpec((1, 1, SSM_PAIRS, SSM_STATE, LANES), lambda b, j: (b, j, 0, 0, 0))

    def body(xbc_ref, dt_ref, z_ref, cw_ref, cb_ref, dbr_ref, dbc_ref, alr_ref, alc_ref, dsk_ref, gain_ref,
             y_ref, xconv_ref, stp_ref, xin, st):
        @pl.when(pl.program_id(1) == 0)
        def _():
            xin[0:HALO, :] = jnp.zeros((HALO, wx), f32)
            st[...] = jnp.zeros_like(st)

        xin[HALO:HALO + ln, :] = xbc_ref[0]
        xconv = jnp.broadcast_to(cb_ref[...], (ln, wx))
        for k in range(SSM_CONV):
            xconv = xconv + cw_ref[k:k + 1, :] * xin[pl.ds(HALO - SSM_CONV + 1 + k, ln), :]
        xin[0:HALO, :] = xin[ln:ln + HALO, :]
        xconv_ref[0] = xconv
        stp_ref[0, 0] = st[...]
        xs_pre, b_pre, c_pre, dtr, dtr_t, zs, sts, gains = _ssd_chunk_inputs(xconv, dt_ref[0], z_ref[0], st,
                                                                             gain_ref[...])
        out, st_new = _ssd_chunk(xs_pre, b_pre, c_pre, dtr, dtr_t, zs, sts, dbr_ref[...], dbc_ref[...],
                                 alr_ref[...], alc_ref[...], dsk_ref[...], gains)
        y_ref[0] = jnp.concatenate(out, axis=1).astype(y_ref.dtype)
        for p in range(SSM_PAIRS):
            st[p] = st_new[p]

    params = [cw, cb, dbr, dbc, alr, alc, dsk, gain]
    return pl.pallas_call(
        body, name=name, grid=(nb, nt),
        in_specs=[tile(wx), tile(LANES), tile(SSM_WIDTH)] + [_const2(p.shape) for p in params],
        out_specs=[tile(SSM_WIDTH), tile(wx), st_spec],
        out_shape=[jax.ShapeDtypeStruct((nb, ns, SSM_WIDTH), MXU_DTYPE), jax.ShapeDtypeStruct((nb, ns, wx), f32),
                   jax.ShapeDtypeStruct((nb, nt, SSM_PAIRS, SSM_STATE, LANES), f32)],
        scratch_shapes=[pltpu.VMEM((ln + HALO, wx), f32), pltpu.VMEM((SSM_PAIRS, SSM_STATE, LANES), f32)],
        compiler_params=_cparams("arbitrary", "arbitrary"),
    )(xbc, dt_raw, z, *params)


def ssd_backward(dy, xbc, xconv, dt_raw, z, stp, cw, cb, dbr, dbc, alr, alc, dsk, gain, name):
    nb, ns, wx = xbc.shape
    ln = SSM_CHUNK
    nt = ns // ln
    per = ln // HALO
    rj = lambda j: nt - 1 - j
    tile = lambda c: pl.BlockSpec((1, ln, c), lambda b, j: (b, rj(j), 0))
    before = pl.BlockSpec((1, HALO, wx), lambda b, j: (b, jnp.maximum(rj(j) * per - 1, 0), 0))
    st_spec = pl.BlockSpec((1, 1, SSM_PAIRS, SSM_STATE, LANES), lambda b, j: (b, rj(j), 0, 0, 0))

    def body(dy_ref, xbc_ref, xbcb_ref, xconv_ref, dt_ref, z_ref, stp_ref,
             cw_ref, cb_ref, dbr_ref, dbc_ref, alr_ref, alc_ref, dsk_ref, gain_ref,
             dxbc_ref, ddt_ref, dz_ref, dcw_ref, dcb_ref, ddbr_ref, ddbc_ref, dalr_ref, dalc_ref, ddsk_ref, dgain_ref,
             dxc_ext, dst, xin):
        j = pl.program_id(1)
        first = _first_step()
        at_seq_start = j == nt - 1

        @pl.when(j == 0)
        def _():
            dxc_ext[ln:ln + HALO, :] = jnp.zeros((HALO, wx), f32)
            dst[...] = jnp.zeros_like(dst)

        xs_pre, b_pre, c_pre, dtr, dtr_t, zs, sts, gains = _ssd_chunk_inputs(xconv_ref[0], dt_ref[0], z_ref[0],
                                                                             stp_ref.at[0, 0], gain_ref[...])
        _, vjp = jax.vjp(_ssd_chunk, xs_pre, b_pre, c_pre, dtr, dtr_t, zs, sts, dbr_ref[...], dbc_ref[...],
                         alr_ref[...], alc_ref[...], dsk_ref[...], gains)
        dyv = dy_ref[0].astype(f32)
        cot = ([dyv[:, GROUP_WIDTH * g:GROUP_WIDTH * (g + 1)] for g in range(SSM_GROUPS)],
               [dst[p] for p in range(SSM_PAIRS)])
        dxs, db, dc, ddt, ddt_t, dzs, dsts, ddbr, ddbc, dalr, dalc, ddsk, dgains = vjp(cot)
        for p in range(SSM_PAIRS):
            dst[p] = dsts[p]
        ddt_ref[0] = (ddt + ddt_t.T).astype(ddt_ref.dtype)
        dz_ref[0] = jnp.concatenate(dzs, axis=1).astype(dz_ref.dtype)
        _accum(ddbr_ref, ddbr, first)
        _accum(ddbc_ref, ddbc, first)
        _accum(dalr_ref, dalr, first)
        _accum(dalc_ref, dalc, first)
        _accum(ddsk_ref, ddsk, first)
        _accum(dgain_ref, jnp.concatenate(dgains, axis=1), first)

        dxc = jnp.concatenate(dxs + db + dc, axis=1)
        _accum(dcb_ref, jnp.sum(dxc, axis=0, keepdims=True), first)
        dxc_ext[0:ln, :] = dxc
        dxp = jnp.zeros((ln, wx), f32)
        for k in range(SSM_CONV):
            dxp = dxp + cw_ref[k:k + 1, :] * dxc_ext[pl.ds(SSM_CONV - 1 - k, ln), :]
        dxbc_ref[0] = dxp.astype(dxbc_ref.dtype)
        dxc_ext[ln:ln + HALO, :] = dxc[0:HALO, :]

        xin[0:HALO, :] = jnp.where(at_seq_start, 0.0, xbcb_ref[0])
        xin[HALO:HALO + ln, :] = xbc_ref[0]
        dcw_rows = [jnp.sum(dxc * xin[pl.ds(HALO - SSM_CONV + 1 + k, ln), :], axis=0, keepdims=True)
                    for k in range(SSM_CONV)]
        dcw_rows += [jnp.zeros((1, wx), f32)] * (HALO - SSM_CONV)
        _accum(dcw_ref, jnp.concatenate(dcw_rows, axis=0), first)

    params = [cw, cb, dbr, dbc, alr, alc, dsk, gain]
    pshape = lambda p: jax.ShapeDtypeStruct(p.shape, f32)
    return pl.pallas_call(
        body, name=name, grid=(nb, nt),
        in_specs=[tile(SSM_WIDTH), tile(wx), before, tile(wx), tile(LANES), tile(SSM_WIDTH), st_spec]
        + [_const2(p.shape) for p in params],
        out_specs=[tile(wx), tile(LANES), tile(SSM_WIDTH), _const2((HALO, wx))] + [_const2(p.shape) for p in params[1:]],
        out_shape=[jax.ShapeDtypeStruct((nb, ns, wx), MXU_DTYPE), jax.ShapeDtypeStruct((nb, ns, LANES), MXU_DTYPE),
                   jax.ShapeDtypeStruct((nb, ns, SSM_WIDTH), MXU_DTYPE), jax.ShapeDtypeStruct((HALO, wx), f32)]
        + [pshape(p) for p in params[1:]],
        scratch_shapes=[pltpu.VMEM((ln + HALO, wx), f32), pltpu.VMEM((SSM_PAIRS, SSM_STATE, LANES), f32),
                        pltpu.VMEM((ln + HALO, wx), f32)],
        compiler_params=_cparams("arbitrary", "arbitrary"),
    )(dy, xbc, xbc, xconv, dt_raw, z, stp, *params)


CONF_HALO = 32
CONF_OFF = CONF_HALO - CONF_KERNEL + 1


def _conf_specs(ts, c, nt):
    per = ts // CONF_HALO
    tile = pl.BlockSpec((1, ts, c), lambda b, j: (b, j, 0))
    before = pl.BlockSpec((1, CONF_HALO, c), lambda b, j: (b, jnp.maximum(j * per - 1, 0), 0))
    after = pl.BlockSpec((1, CONF_HALO, c), lambda b, j: (b, jnp.minimum((j + 1) * per, nt * per - 1), 0))
    return tile, before, after


SUBLANES = 8


def _shifted_copies(dst, x):
    rows = x.shape[0]
    dst[0] = x
    for b in range(1, SUBLANES):
        dst[b] = pltpu.roll(x, rows - b, 0)


def _window(copies, off, size):
    b = off % SUBLANES
    return copies[b, pl.ds(off - b, size), :]


def _glu(x):
    return x[:, :CONF_WIDTH] * jax.nn.sigmoid(x[:, CONF_WIDTH:])


def _layernorm_parts(c):
    xc = c - jnp.mean(c, axis=-1, keepdims=True)
    r = lax.rsqrt(jnp.mean(xc * xc, axis=-1, keepdims=True) + EPS)
    return xc * r, r


def conf_forward(glu, cw, cb, ln_g, ln_b, name):
    nb, ns, wg = glu.shape
    w = CONF_WIDTH
    ts = SEQ_TILE
    nt = ns // ts
    tile, before, _ = _conf_specs(ts, wg, nt)

    def body(x_ref, xb_ref, cw_ref, cb_ref, g_ref, b_ref, y_ref, u_rot):
        _shifted_copies(u_rot, jnp.concatenate(
            [jnp.where(pl.program_id(1) == 0, 0.0, _glu(xb_ref[0])), _glu(x_ref[0])], axis=0))
        conv = jnp.broadcast_to(cb_ref[...], (ts, w))
        for k in range(CONF_KERNEL):
            conv = conv + cw_ref[k:k + 1, :] * _window(u_rot, CONF_OFF + k, ts)
        xhat, _ = _layernorm_parts(conv)
        y_ref[0] = _silu(xhat * g_ref[...] + b_ref[...]).astype(y_ref.dtype)

    params = [cw, cb, ln_g, ln_b]
    return pl.pallas_call(
        body, name=name, grid=(nb, nt),
        in_specs=[tile, before] + [_const2(p.shape) for p in params],
        out_specs=pl.BlockSpec((1, ts, w), lambda b, j: (b, j, 0)),
        out_shape=jax.ShapeDtypeStruct((nb, ns, w), MXU_DTYPE),
        scratch_shapes=[pltpu.VMEM((SUBLANES, ts + CONF_HALO, w), f32)],
        compiler_params=_cparams("parallel", "parallel"),
    )(glu, glu, *params)


def conf_backward(dy, glu, cw, cb, ln_g, ln_b, name):
    nb, ns, wg = glu.shape
    w = CONF_WIDTH
    ts = SEQ_TILE
    nt = ns // ts
    te = ts + CONF_HALO
    tile, before, after = _conf_specs(ts, wg, nt)
    dtile, _, dafter = _conf_specs(ts, w, nt)

    def body(dy_ref, dya_ref, x_ref, xb_ref, xa_ref, cw_ref, cb_ref, g_ref, b_ref,
             dx_ref, dcw_ref, dcb_ref, dg_ref, db_ref, u_ext, dc_ext):
        j = pl.program_id(1)
        first = _first_step()
        x = x_ref[0]
        _shifted_copies(u_ext, jnp.concatenate(
            [jnp.where(j == 0, 0.0, _glu(xb_ref[0])), _glu(x), _glu(xa_ref[0])], axis=0))
        conv = jnp.broadcast_to(cb_ref[...], (te, w))
        for k in range(CONF_KERNEL):
            conv = conv + cw_ref[k:k + 1, :] * _window(u_ext, CONF_OFF + k, te)
        xhat, r = _layernorm_parts(conv)
        lnout = xhat * g_ref[...] + b_ref[...]
        sg = jax.nn.sigmoid(lnout)
        rows = lax.broadcasted_iota(jnp.int32, (te, w), 0)
        dyv = jnp.concatenate([dy_ref[0].astype(f32), dya_ref[0].astype(f32)], axis=0)
        dyv = jnp.where(jnp.logical_and(j == nt - 1, rows >= ts), 0.0, dyv)
        dln = dyv * sg * (1.0 + lnout * (1.0 - sg))
        in_tile = rows < ts
        _accum(dg_ref, jnp.sum(jnp.where(in_tile, dln * xhat, 0.0), axis=0, keepdims=True), first)
        _accum(db_ref, jnp.sum(jnp.where(in_tile, dln, 0.0), axis=0, keepdims=True), first)
        dxh = dln * g_ref[...]
        dconv = r * (dxh - jnp.mean(dxh, axis=-1, keepdims=True) - xhat * jnp.mean(dxh * xhat, axis=-1, keepdims=True))
        _shifted_copies(dc_ext, dconv)
        dct = dconv[0:ts, :]
        _accum(dcb_ref, jnp.sum(dct, axis=0, keepdims=True), first)
        du = jnp.zeros((ts, w), f32)
        dcw_rows = []
        for k in range(CONF_KERNEL):
            du = du + cw_ref[k:k + 1, :] * _window(dc_ext, CONF_KERNEL - 1 - k, ts)
            dcw_rows.append(jnp.sum(dct * _window(u_ext, CONF_OFF + k, ts), axis=0, keepdims=True))
        dcw_rows.append(jnp.zeros((1, w), f32))
        _accum(dcw_ref, jnp.concatenate(dcw_rows, axis=0), first)
        sb = jax.nn.sigmoid(x[:, w:])
        dx_ref[0] = jnp.concatenate([du * sb, du * x[:, :w] * sb * (1.0 - sb)], axis=1).astype(dx_ref.dtype)

    params = [cw, cb, ln_g, ln_b]
    return pl.pallas_call(
        body, name=name, grid=(nb, nt),
        in_specs=[dtile, dafter, tile, before, after] + [_const2(p.shape) for p in params],
        out_specs=[tile] + [_const2(p.shape) for p in params],
        out_shape=[jax.ShapeDtypeStruct((nb, ns, wg), MXU_DTYPE)] + [jax.ShapeDtypeStruct(p.shape, f32) for p in params],
        scratch_shapes=[pltpu.VMEM((SUBLANES, te + CONF_HALO, w), f32), pltpu.VMEM((SUBLANES, te, w), f32)],
        compiler_params=_cparams("arbitrary", "arbitrary"),
    )(dy, dy, glu, glu, glu, *params)


def _row(v):
    return v.reshape(1, -1).astype(f32)


def _pad_to(v, n, axis):
    pads = [(0, 0)] * v.ndim
    pads[axis] = (0, n - v.shape[axis])
    return jnp.pad(v, pads)


def _block_diag(w):
    nh, d, _ = w.shape
    eye = jnp.eye(nh, dtype=w.dtype)
    return (eye[:, None, :, None] * w[:, :, None, :]).reshape(nh * d, nh * d)


def _diag_blocks(m, nh):
    d = m.shape[0] // nh
    idx = jnp.arange(nh)
    return m.reshape(nh, d, nh, d)[idx, :, idx, :]


def _mix_even_fwd(h, gpre, w, wl, nb, ns, rider=None):
    t = nb * ns
    w_in = wl["w_in"]
    w_lx, w_lg = Part(w_in, 0, LRU_WIDTH, 1), Part(w_in, LRU_WIDTH, LRU_WIDTH, 1)
    w_qkv = Part(w_in, 2 * LRU_WIDTH, 3 * SB_WIDTH, 1)
    xpre, gate, qkv = norm_matmul(h, gpre, [w_lx, w_lg, w_qkv], [f32, f32, f32], name="ev_in_proj")
    lru_p = [w["ev_lru_conv_w"][0], _row(w["ev_lru_conv_b"][0]),
             _block_diag(w["ev_lru_gate_a_w"][0]).astype(MXU_DTYPE), _row(w["ev_lru_gate_a_b"][0]),
             _block_diag(w["ev_lru_gate_x_w"][0]).astype(MXU_DTYPE), _row(w["ev_lru_gate_x_b"][0]),
             _row(w["ev_lru_lambda"][0])]
    xpre3, gate3, qkv3 = xpre.reshape(nb, ns, -1), gate.reshape(nb, ns, -1), qkv.reshape(nb, ns, -1)
    y_a, xc, hs = lru_forward(xpre3, gate3, *lru_p, name="ev_lru_fwd")
    o = sb_forward(qkv3, name="ev_sb_fwd", rider=rider)
    carried = None
    if rider is not None:
        o, carried = o
    ys = [y_a.reshape(t, -1), o.reshape(t, -1)]
    saved = dict(xpre=xpre3, gate=gate3, qkv=qkv3, xc=xc, hs=hs, o=o, lru_p=lru_p)
    return ys, saved, carried


def _mix_even_bwd(dys, saved, wtl, nb, ns, lru_rider=None, attention_rider=None):
    t = nb * ns
    dy_a, dy_b = [d.reshape(nb, ns, -1) for d in dys]
    outs = lru_backward(dy_a, saved["xpre"], saved["gate"], saved["xc"], saved["hs"], *saved["lru_p"],
                        name="ev_lru_bwd", rider=lru_rider)
    lru_carried = None
    if lru_rider is not None:
        outs, lru_carried = outs
    dxp, dgt, dcw, dcb, dga, dgab, dgx, dgxb, dlam = outs
    rider = attention_rider(lru_carried) if attention_rider is not None else None
    carried = None
    if rider is None:
        dq, dk, dv = sb_backward(saved["qkv"], saved["o"], dy_b, name="ev_sb_bwd")
    else:
        (dq, dk, dv), carried = sb_backward(saved["qkv"], saved["o"], dy_b, name="ev_sb_bwd", rider=rider)
    w_in_t = wtl["w_in"]
    pieces = [dxp, dgt, dq, dk, dv]
    gs = [d.reshape(t, -1) for d in pieces]
    wts = [Part(w_in_t, LRU_WIDTH * i, LRU_WIDTH, 0) for i in range(5)]
    grads = {
        "ev_lru_conv_w": dcw[:LRU_CONV][None], "ev_lru_conv_b": dcb,
        "ev_lru_gate_a_w": _diag_blocks(dga, LRU_HEADS)[None], "ev_lru_gate_a_b": dgab,
        "ev_lru_gate_x_w": _diag_blocks(dgx, LRU_HEADS)[None], "ev_lru_gate_x_b": dgxb,
        "ev_lru_lambda": dlam,
    }
    return gs, wts, grads, carried


def _odd_params(w):
    ssd_p = [w["od_ssm_conv_w"][0], _row(w["od_ssm_conv_b"][0]),
             _pad_to(_row(w["od_ssm_dt_bias"][0]), LANES, 1), _pad_to(_row(w["od_ssm_dt_bias"][0]), LANES, 1).T,
             _pad_to(_row(w["od_ssm_a_log"][0]), LANES, 1), _pad_to(_row(w["od_ssm_a_log"][0]), LANES, 1).T,
             _pad_to(_row(w["od_ssm_d"][0]), LANES, 1), _row(w["od_ssm_norm"][0])]
    conf_p = [_pad_to(w["od_cm_conv_w"][0], CONF_HALO, 0), _row(w["od_cm_conv_b"][0]),
              _row(w["od_cm_ln_g"][0]), _row(w["od_cm_ln_b"][0])]
    return ssd_p, conf_p


ODD_SPLITS = (SSM_WIDTH, SSM_WIDTH + SSM_XBC, SSM_WIDTH + SSM_XBC + SSM_HEADS)


def _mix_odd_fwd(h, gpre, w, wl, nb, ns, rider=None):
    assert rider is None
    t = nb * ns
    w_in = wl["w_in"]
    s0, s1, s2 = ODD_SPLITS
    w_al = jnp.concatenate([w_in[:, :s1], w_in[:, s2:], _pad_to(w_in[:, s1:s2], LANES, 1)], axis=1)
    widths = (s0, s1 - s0, w_in.shape[1] - s2, LANES)
    starts = (0, s0, s1, s1 + widths[2])
    zz, xbc, glu, dtr = norm_matmul(h, gpre, [Part(w_al, a, n, 1) for a, n in zip(starts, widths)], [f32] * 4,
                                    name="od_in_proj")
    ssd_p, conf_p = _odd_params(w)
    zz3, xbc3, dtr3, glu3 = [a.reshape(nb, ns, -1) for a in (zz, xbc, dtr, glu)]
    y_c, xconv, stp = ssd_forward(xbc3, dtr3, zz3, *ssd_p, name="od_ssd_fwd")
    y_d = conf_forward(glu3, *conf_p, name="od_conf_fwd")
    ys = [y_c.reshape(t, -1), y_d.reshape(t, -1)]
    saved = dict(z=zz3, xbc=xbc3, dtr=dtr3, glu=glu3, xconv=xconv, stp=stp, ssd_p=ssd_p, conf_p=conf_p)
    return ys, saved, None


def _mix_odd_bwd(dys, saved, wtl, nb, ns, lru_rider=None, attention_rider=None):
    assert lru_rider is None and attention_rider is None
    t = nb * ns
    dy_c, dy_d = [d.reshape(nb, ns, -1) for d in dys]
    outs = ssd_backward(dy_c, saved["xbc"], saved["xconv"], saved["dtr"], saved["z"], saved["stp"], *saved["ssd_p"],
                        name="od_ssd_bwd")
    dxbc, ddt, dz, dcw, dcb, ddbr, ddbc, dalr, dalc, ddsk, dgain = outs
    dglu, ccw, ccb, clg, clb = conf_backward(dy_d, saved["glu"], *saved["conf_p"], name="od_conf_bwd")
    w_in_t = wtl["w_in"]
    s0, s1, s2 = ODD_SPLITS
    carried = None
    gs = [d.reshape(t, -1) for d in (dz, dxbc, dglu, ddt)]
    wt_al = jnp.concatenate([w_in_t[:s1], w_in_t[s2:], _pad_to(w_in_t[s1:s2], LANES, 0)], axis=0)
    widths = (s0, s1 - s0, w_in_t.shape[0] - s2, LANES)
    starts = (0, s0, s1, s1 + widths[2])
    wts = [Part(wt_al, a, n, 0) for a, n in zip(starts, widths)]
    nh = SSM_HEADS
    grads = {
        "od_ssm_conv_w": dcw[:SSM_CONV][None], "od_ssm_conv_b": dcb,
        "od_ssm_dt_bias": ddbr[:, :nh] + ddbc[:nh, 0][None], "od_ssm_a_log": dalr[:, :nh] + dalc[:nh, 0][None],
        "od_ssm_d": ddsk[:, :nh], "od_ssm_norm": dgain,
        "od_cm_conv_w": ccw[:CONF_KERNEL][None], "od_cm_conv_b": ccb, "od_cm_ln_g": clg, "od_cm_ln_b": clb,
    }
    return gs, wts, grads, carried


LAYER_MATRICES = ("w_in", "w_out", "mlp_w1", "mlp_w2", "ple_w_proj", "ple_w_gate")
NORM_NAMES = ("norm_mix_pre", "norm_mix_post", "norm_mlp_pre", "norm_mlp_post", "norm_ple")


class NoOverlap:
    sums, from_chips = {}, {}

    def attention_fwd_rider(self):
        return None

    def weights_arrived(self, carried, wl, wtl):
        raise NotImplementedError

    def mlp_bwd_rider(self, layer1_grads):
        return None

    def after_mlp_bwd(self, carried):
        pass

    def lru_bwd_rider(self, layer0_grads):
        return None

    def attention_bwd_rider(self, carried):
        return None

    def after_attention_bwd(self, carried):
        pass


OUT_SPLIT = (LRU_WIDTH, SSM_WIDTH)


def local_step(x, p, target, w, wl, wtl, comm=NoOverlap()):
    nb, ns, d = x.shape
    t = nb * ns
    h = x.reshape(t, d)
    depth = p.shape[0]
    wl, wtl = list(wl), list(wtl)
    tapes = []
    for i in range(depth):
        even = i % 2 == 0
        tag = f"l{i}_"
        gpre = _row(w["norm_mix_pre"][i])
        rider = comm.attention_fwd_rider() if i == 0 else None
        ys, saved, carried = (_mix_even_fwd if even else _mix_odd_fwd)(h, gpre, w, wl[i], nb, ns, rider)
        if rider is not None:
            wl, wtl = comm.weights_arrived(carried, wl, wtl)
        w_out = wl[i]["w_out"]
        split = OUT_SPLIT[i % 2]
        w_outs = [Part(w_out, 0, split, 0), Part(w_out, split, w_out.shape[0] - split, 0)]
        h1, m = matmul_residual_norm(ys, w_outs, h, _row(w["norm_mix_post"][i]), name=tag + "out_proj")
        a1, = norm_matmul(h1, _row(w["norm_mlp_pre"][i]), [wl[i]["mlp_w1"]], [MXU_DTYPE], name=tag + "mlp_up")
        h2, f = matmul_residual_norm([a1], [wl[i]["mlp_w2"]], h1, _row(w["norm_mlp_post"][i]), name=tag + "mlp_down",
                                     relu2=True)
        pi = p[i].reshape(t, -1)
        ple_args = (h2, pi, wl[i]["ple_w_gate"], wl[i]["ple_w_proj"], _row(w["norm_ple"][i]))
        if i < depth - 1:
            h3, gl, emb = ple_forward(*ple_args, name=tag + "ple")
        else:
            loss_row, dh, gl, emb = ple_forward(*ple_args, name=tag + "ple_loss", target=target.reshape(t, d))
            h3 = None
        tapes.append(dict(h=h, ys=ys, w_outs=w_outs, saved=saved, h1=h1, m=m, a1=a1, h2=h2, f=f, pi=pi, gl=gl,
                          emb=emb))
        h = h3

    grads = {}
    norm_grads = {k: [None] * depth for k in NORM_NAMES}
    layer_grads = [None] * depth
    for i in reversed(range(depth)):
        even = i % 2 == 0
        tag = f"l{i}_"
        tp = tapes[i]
        lg = {}
        to_sibling = comm.mlp_bwd_rider(layer_grads[1]) if i == 0 else None
        dh2, dgl, demb, dg = ple_backward(dh, tp["h2"], tp["gl"], tp["emb"], _row(w["norm_ple"][i]),
                                          wtl[i]["ple_w_gate"], name=tag + "ple_bwd")
        norm_grads["norm_ple"][i] = dg
        lg["ple_w_gate"] = weight_grad(tp["h2"], dgl, name=tag + "dw_gate")
        lg["ple_w_proj"] = weight_grad(tp["pi"], demb, name=tag + "dw_proj")
        outs = bwd_through_norm_out(dh2, tp["f"], _row(w["norm_mlp_post"][i]), [wtl[i]["mlp_w2"]], [MXU_DTYPE],
                                    name=tag + "mlp_down_bwd", relu2_of=tp["a1"], rider=to_sibling)
        d_f, (da1,), dg = outs[:3]
        if to_sibling is not None:
            comm.after_mlp_bwd(outs[3])
        norm_grads["norm_mlp_post"][i] = dg
        lg["mlp_w2"] = weight_grad(tp["a1"], d_f, name=tag + "dw2", prologue="relu2")
        gpre = _row(w["norm_mlp_pre"][i])
        dh1, dg = bwd_through_norm_in(dh2, [da1], [wtl[i]["mlp_w1"]], tp["h1"], gpre, name=tag + "mlp_up_bwd")
        norm_grads["norm_mlp_pre"][i] = dg
        lg["mlp_w1"] = weight_grad(tp["h1"], da1, name=tag + "dw1", prologue="rms", gain=gpre)
        wt_out = wtl[i]["w_out"]
        split = tp["w_outs"][0].shape[0]
        dm, dys, dg = bwd_through_norm_out(dh1, tp["m"], _row(w["norm_mix_post"][i]),
                                           [Part(wt_out, 0, split, 1),
                                            Part(wt_out, split, wt_out.shape[1] - split, 1)],
                                           [f32, MXU_DTYPE if even else f32],
                                           name=tag + "out_proj_bwd")
        norm_grads["norm_mix_post"][i] = dg
        lg["w_out"] = jnp.concatenate([weight_grad(y, dm, name=tag + f"dw_out{k}") for k, y in enumerate(tp["ys"])],
                                      axis=0)
        lru_rider = comm.lru_bwd_rider(lg) if i == 0 else None
        gs, wts, mix_grads, carried = (_mix_even_bwd if even else _mix_odd_bwd)(
            dys, tp["saved"], wtl[i], nb, ns, lru_rider, comm.attention_bwd_rider if lru_rider is not None else None)
        if carried is not None:
            comm.after_attention_bwd(carried)
        grads.update(mix_grads)
        gpre = _row(w["norm_mix_pre"][i])
        dh, dg = bwd_through_norm_in(dh1, gs, wts, tp["h"], gpre, name=tag + "in_proj_bwd")
        norm_grads["norm_mix_pre"][i] = dg
        dw_in = weight_grads_of_norm(tp["h"], gpre, gs, name=tag + "dw_in")
        if not even:
            dw_in = [dw_in[0], dw_in[1], dw_in[3][:, :SSM_HEADS], dw_in[2]]
        lg["w_in"] = jnp.concatenate(dw_in, axis=1)
        layer_grads[i] = lg
    for k, v in norm_grads.items():
        grads[k] = jnp.concatenate(v, axis=0)
    return loss_row[0, 0], dh.reshape(nb, ns, d), grads, layer_grads


MESH_ID = pl.DeviceIdType.MESH
ANY = pl.BlockSpec(memory_space=pl.ANY)


def _mesh_pos():
    return lax.axis_index("x"), lax.axis_index("y"), lax.axis_index("c")


def all_gather(shards, name):
    return _run_alone(gather_rider(shards), name)


class Rider:
    def __init__(self, inputs, out_shapes, scratch_shapes, start, finish, middle=None):
        self.inputs, self.out_shapes, self.scratch_shapes = list(inputs), list(out_shapes), list(scratch_shapes)
        self.start, self.finish, self.middle = start, finish, middle


def _run_alone(rider, name):
    ni, no = len(rider.inputs), len(rider.out_shapes)

    def body(*refs):
        args = (refs[:ni], refs[ni:ni + no], refs[ni + no:])
        rider.start(*args)
        if rider.middle is not None:
            rider.middle(*args)
        rider.finish(*args)

    return pl.pallas_call(
        body, name=name, out_shape=rider.out_shapes, in_specs=[ANY] * ni, out_specs=[ANY] * no,
        scratch_shapes=rider.scratch_shapes,
    )(*rider.inputs)


def _ride(rider, body, in_specs, out_specs, out_shape, scratch_shapes, grid):
    in_specs, out_specs, out_shape = list(in_specs), list(out_specs), list(out_shape)
    scratch_shapes = list(scratch_shapes)
    if rider is None:
        return body, in_specs, out_specs, out_shape, scratch_shapes
    n_in, n_out, n_scr = len(in_specs), len(out_specs), len(scratch_shapes)
    ri, ro = len(rider.inputs), len(rider.out_shapes)
    total = math.prod(grid)

    def carrying(*refs):
        ins, r_ins = refs[:n_in], refs[n_in:n_in + ri]
        o0 = n_in + ri
        outs, r_outs = refs[o0:o0 + n_out], refs[o0 + n_out:o0 + n_out + ro]
        s0 = o0 + n_out + ro
        scr, r_scr = refs[s0:s0 + n_scr], refs[s0 + n_scr:]
        step = pl.program_id(0)
        for ax in range(1, len(grid)):
            step = step * grid[ax] + pl.program_id(ax)
        args = (r_ins, r_outs, r_scr)
        pl.when(step == 0)(lambda: rider.start(*args))
        if rider.middle is not None:
            pl.when(step == total // 2)(lambda: rider.middle(*args))
        body(*ins, *outs, *scr)
        pl.when(step == total - 1)(lambda: rider.finish(*args))

    return (carrying, in_specs + [ANY] * ri, out_specs + [ANY] * ro, out_shape + rider.out_shapes,
            scratch_shapes + rider.scratch_shapes)


def gather_rider(shards):
    n = len(shards)

    def parts(x_refs, out_refs, scr):
        send_sems, recv_sems, local_sems = scr
        x, y, c = _mesh_pos()
        chips = [(1 - x, y), (x, 1 - y), (1 - x, 1 - y)]

        def slot(a, px, py, pc):
            return out_refs[a].at[4 * px + 2 * py + pc]

        def copy(a, k, block, to, src=None):
            return pltpu.make_async_remote_copy(
                src_ref=slot(a, *block) if src is None else src, dst_ref=slot(a, *block),
                send_sem=send_sems.at[7 * a + k], recv_sem=recv_sems.at[7 * a + k], device_id=to,
                device_id_type=MESH_ID)

        me, sibling = (x, y, c), (x, y, 1 - c)
        def mine():
            return [pltpu.make_async_copy(x_refs[a], slot(a, *me), local_sems.at[a]) for a in range(n)]

        def first():
            out = []
            for j, chip in enumerate(chips):
                out += [copy(a, 1 + j, me, (*chip, c), src=x_refs[a]) for a in range(n)]
            return out + [copy(a, 0, me, sibling, src=x_refs[a]) for a in range(n)]

        def passed(j):
            return [copy(a, 4 + j, (*chips[j], c), sibling) for a in range(n)]

        return me, sibling, chips, c, copy, mine, first, passed

    def start(x_refs, out_refs, scr):
        _, _, _, _, _, mine, first, _ = parts(x_refs, out_refs, scr)
        for cp in mine() + first():
            cp.start()

    def middle(x_refs, out_refs, scr):
        me, _, chips, c, copy, _, _, passed = parts(x_refs, out_refs, scr)
        for j, chip in enumerate(chips):
            for a, fwd in enumerate(passed(j)):
                copy(a, 1 + j, (*chip, c), me).wait_recv()
                fwd.start()

    def finish(x_refs, out_refs, scr):
        me, sibling, chips, c, copy, mine, first, passed = parts(x_refs, out_refs, scr)
        for a in range(n):
            copy(a, 0, sibling, me).wait_recv()
        for j, chip in enumerate(chips):
            for a in range(n):
                copy(a, 4 + j, (*chip, 1 - c), me).wait_recv()
        for cp in first() + [cp for j in range(len(chips)) for cp in passed(j)]:
            cp.wait_send()
        for cp in mine():
            cp.wait()

    return Rider(shards, [jax.ShapeDtypeStruct((N_DEV,) + s.shape, s.dtype) for s in shards],
                 [pltpu.SemaphoreType.DMA((7 * n,)), pltpu.SemaphoreType.DMA((7 * n,)), pltpu.SemaphoreType.DMA((n,))],
                 start, finish, middle)


def scatter_to_sibling(parts, name):
    return _run_alone(sibling_rider(parts), name)


def sibling_rider(parts):
    n = len(parts)

    def copies(g_refs, out_refs, scr):
        send_sems, recv_sems = scr
        x, y, c = _mesh_pos()
        return [pltpu.make_async_remote_copy(
            src_ref=g_refs[a].at[2 * chip + (1 - c)], dst_ref=out_refs[a].at[chip],
            send_sem=send_sems.at[4 * a + chip], recv_sem=recv_sems.at[4 * a + chip], device_id=(x, y, 1 - c),
            device_id_type=MESH_ID) for a in range(n) for chip in range(4)]

    def start(*refs):
        for cp in copies(*refs):
            cp.start()

    def finish(*refs):
        cps = copies(*refs)
        for cp in cps:
            cp.wait_recv()
        for cp in cps:
            cp.wait_send()

    return Rider(parts, [jax.ShapeDtypeStruct((4,) + p.shape[1:], p.dtype) for p in parts],
                 [pltpu.SemaphoreType.DMA((4 * n,)), pltpu.SemaphoreType.DMA((4 * n,))], start, finish)


def scatter_to_chips(partials, name):
    return _run_alone(chips_rider(partials), name)


def chips_rider(partials):
    n = len(partials)

    def copies(p_refs, out_refs, scr):
        send_sems, recv_sems = scr
        x, y, c = _mesh_pos()
        chips = [(1 - x, y), (x, 1 - y), (1 - x, 1 - y)]
        return [pltpu.make_async_remote_copy(
            src_ref=p_refs[a].at[2 * px + py], dst_ref=out_refs[a].at[j],
            send_sem=send_sems.at[3 * a + j], recv_sem=recv_sems.at[3 * a + j], device_id=(px, py, c),
            device_id_type=MESH_ID) for a in range(n) for j, (px, py) in enumerate(chips)]

    def start(*refs):
        for cp in copies(*refs):
            cp.start()

    def finish(*refs):
        cps = copies(*refs)
        for cp in cps:
            cp.wait_recv()
        for cp in cps:
            cp.wait_send()

    return Rider(partials, [jax.ShapeDtypeStruct((3,) + p.shape[1:], p.dtype) for p in partials],
                 [pltpu.SemaphoreType.DMA((3 * n,)), pltpu.SemaphoreType.DMA((3 * n,))], start, finish)


ICI_DTYPE = jnp.bfloat16
ELEMENTWISE_BLOCK_BYTES = 1 << 20


def _row_tile(rows, cols):
    cap = max(16, ELEMENTWISE_BLOCK_BYTES // (4 * cols))
    best = [t for t in range(16, min(rows, cap) + 1, 16) if rows % t == 0]
    return best[-1] if best else rows


def add_sibling_parts(parts, received, core, name):
    _, r, n = parts.shape
    tr = _row_tile(r, n)

    def body(c_ref, a_ref, b_ref, ob_ref):
        ob_ref[...] = (a_ref[...] + b_ref[...]).astype(ob_ref.dtype)

    blk = pl.BlockSpec((1, tr, n), lambda i, j, c_ref: (i, j, 0))
    return pl.pallas_call(
        body, name=name,
        grid_spec=pltpu.PrefetchScalarGridSpec(
            num_scalar_prefetch=1, grid=(4, r // tr),
            in_specs=[pl.BlockSpec((1, tr, n), lambda i, j, c_ref: (2 * i + c_ref[0], j, 0)), blk],
            out_specs=blk),
        out_shape=jax.ShapeDtypeStruct((4, r, n), ICI_DTYPE),
        compiler_params=_cparams("parallel", "parallel"),
    )(core, parts, received)


def _adamw(w, g, m, v):
    m = ADAM_B1 * m + (1.0 - ADAM_B1) * g
    v = ADAM_B2 * v + (1.0 - ADAM_B2) * jnp.square(g)
    m_hat = m / (1.0 - ADAM_B1 ** ADAM_STEP)
    v_hat = v / (1.0 - ADAM_B2 ** ADAM_STEP)
    delta = -ADAM_LR * (m_hat / (jnp.sqrt(v_hat) + ADAM_EPS) + ADAM_WD * w)
    return delta, m, v


def adamw_sharded(parts, from_sibling, received, place, w, m, v, name):
    _, r, n = parts.shape

    def body(k_ref, a_ref, b_ref, r_ref, w_ref, m_ref, v_ref, g_out, d_out, m_out, v_out):
        g = (a_ref[0] + b_ref[0]) + r_ref[0].astype(f32)
        g = g + r_ref[1].astype(f32)
        g = g + r_ref[2].astype(f32)
        delta, mn, vn = _adamw(w_ref[...], g, m_ref[...], v_ref[...])
        g_out[...] = g
        d_out[...] = delta
        m_out[...] = mn
        v_out[...] = vn

    tr = _row_tile(r, n)
    flat = pl.BlockSpec((tr, n), lambda j, k_ref: (j, 0))
    return pl.pallas_call(
        body, name=name,
        grid_spec=pltpu.PrefetchScalarGridSpec(
            num_scalar_prefetch=1, grid=(r // tr,),
            in_specs=[pl.BlockSpec((1, tr, n), lambda j, k_ref: (k_ref[0], j, 0)),
                      pl.BlockSpec((1, tr, n), lambda j, k_ref: (k_ref[1], j, 0)),
                      pl.BlockSpec((3, tr, n), lambda j, k_ref: (0, j, 0)), flat, flat, flat],
            out_specs=[flat] * 4),
        out_shape=[jax.ShapeDtypeStruct((r, n), f32)] * 4,
        compiler_params=_cparams("parallel"),
    )(place, parts, from_sibling, received, w, m, v)


def adamw_replicated(gathered, w, m, v, name):
    _, r, n = gathered.shape

    def body(g_ref, w_ref, m_ref, v_ref, g_out, d_out, m_out, v_out):
        g = g_ref[0]
        for k in range(1, N_DEV):
            g = g + g_ref[k]
        delta, mn, vn = _adamw(w_ref[...], g, m_ref[...], v_ref[...])
        g_out[...] = g
        d_out[...] = delta
        m_out[...] = mn
        v_out[...] = vn

    return pl.pallas_call(
        body, name=name,
        out_shape=[jax.ShapeDtypeStruct((r, n), f32)] * 4,
        compiler_params=pltpu.CompilerParams(vmem_limit_bytes=VMEM_LIMIT),
    )(gathered, w, m, v)


W_NAMES = ['ev_w_in', 'ev_lru_conv_w', 'ev_lru_conv_b', 'ev_lru_gate_a_w', 'ev_lru_gate_a_b', 'ev_lru_gate_x_w',
           'ev_lru_gate_x_b', 'ev_lru_lambda', 'ev_w_out', 'od_w_in', 'od_ssm_conv_w', 'od_ssm_conv_b',
           'od_ssm_dt_bias', 'od_ssm_a_log', 'od_ssm_d', 'od_ssm_norm', 'od_cm_conv_w', 'od_cm_conv_b', 'od_cm_ln_g',
           'od_cm_ln_b', 'od_w_out', 'norm_mix_pre', 'norm_mix_post', 'norm_mlp_pre', 'norm_mlp_post', 'norm_ple',
           'mlp_w1', 'mlp_w2', 'ple_w_proj', 'ple_w_gate']
BIG_SHARDED = {'ev_w_in': 2, 'ev_w_out': 1, 'od_w_in': 2, 'od_w_out': 1, 'mlp_w1': 2, 'mlp_w2': 1, 'ple_w_proj': 2,
               'ple_w_gate': 1}
SMALL_SHARDED = {'ev_lru_conv_w': 2, 'od_ssm_conv_w': 2, 'od_ssm_conv_b': 1, 'od_ssm_norm': 1, 'od_cm_conv_w': 2,
                 'od_cm_conv_b': 1, 'od_cm_ln_g': 1, 'od_cm_ln_b': 1}
SHARDED = {**BIG_SHARDED, **SMALL_SHARDED}
REPLICATED = [n for n in W_NAMES if n not in SHARDED]


def _round_up(n, k):
    return -(-n // k) * k


def _pack_rows(flat, rows_multiple):
    n = flat.shape[0]
    total = _round_up(n, LANES * rows_multiple)
    return jnp.pad(flat, (0, total - n)).reshape(-1, LANES)


def _unpack(flat, shapes):
    out, off = {}, 0
    for name, shape in shapes.items():
        size = math.prod(shape)
        out[name] = flat[off:off + size].reshape(shape)
        off += size
    return out


def _unshard(g8, shape, axis):
    g = jnp.moveaxis(g8.reshape((N_DEV,) + tuple(shape)), 0, axis)
    return g.reshape(tuple(shape[:axis]) + (N_DEV * shape[axis],) + tuple(shape[axis + 1:]))


def _to_shards(g, axis):
    shard = g.shape[axis] // N_DEV
    g = g.reshape(g.shape[:axis] + (N_DEV, shard) + g.shape[axis + 1:])
    return jnp.moveaxis(g, axis, 0)


def _pack_small(tree):
    return _pack_rows(jnp.concatenate([tree[k].astype(f32).reshape(-1) for k in SMALL_SHARDED]), 16)


def _layer_entry(i, key):
    mixer = "ev" if i % 2 == 0 else "od"
    return {"w_in": (mixer + "_w_in", i // 2, 1), "w_out": (mixer + "_w_out", i // 2, 0),
            "mlp_w1": ("mlp_w1", i, 1), "mlp_w2": ("mlp_w2", i, 0),
            "ple_w_proj": ("ple_w_proj", i, 1), "ple_w_gate": ("ple_w_gate", i, 0)}[key]


def _layer_shards(tree, i):
    out = {}
    for key in LAYER_MATRICES:
        name, idx, _ = _layer_entry(i, key)
        out[key] = tree[name][idx]
    return out


def _gather_early(w):
    small = _pack_small(w)
    terms, rest = [], small
    for _ in range(3):
        term = rest.astype(MXU_DTYPE)
        terms.append(term)
        rest = rest - term.astype(f32)
    outs = all_gather([_layer_shards(w, 0)["w_in"].astype(MXU_DTYPE), jnp.concatenate(terms, axis=0)],
                      name="gather_weights")
    w_in = _unshard(outs[0], outs[0].shape[1:], _layer_entry(0, "w_in")[2])
    wl, wtl = {"w_in": w_in}, {"w_in": w_in.T}
    full = {k: w[k] for k in REPLICATED}
    t = outs[-1].astype(f32)
    nr = small.shape[0]
    vals = (t[:, :nr] + t[:, nr:2 * nr] + t[:, 2 * nr:]).reshape(N_DEV, -1)
    off = 0
    for k, axis in SMALL_SHARDED.items():
        size = math.prod(w[k].shape)
        full[k] = _unshard(vals[:, off:off + size], w[k].shape, axis)
        off += size
    return full, wl, wtl


class Overlap:
    LATE = [(0, key) for key in LAYER_MATRICES if key != "w_in"] + [(1, key) for key in LAYER_MATRICES]
    EARLY_GRADS = [(0, key) for key in LAYER_MATRICES if key != "w_in"]

    def __init__(self, w):
        self.w = w
        self.sums, self.from_chips, self.parts = {}, {}, {}

    def attention_fwd_rider(self):
        shards = [_layer_shards(self.w, 0), _layer_shards(self.w, 1)]
        return gather_rider([shards[i][key].astype(MXU_DTYPE) for i, key in self.LATE])

    def weights_arrived(self, carried, wl, wtl):
        wl = [dict(wl[0]), {}]
        wtl = [dict(wtl[0]), {}]
        for (i, key), g8 in zip(self.LATE, carried):
            wl[i][key] = _unshard(g8, g8.shape[1:], _layer_entry(i, key)[2])
            wtl[i][key] = wl[i][key].T
        return wl, wtl

    def _to_sibling(self, ids, layer_grads):
        for i, key in ids:
            self.parts[i, key] = _to_shards(layer_grads[key], _layer_entry(i, key)[2])
        return sibling_rider([self.parts[e] for e in ids])

    def _add(self, ids, carried):
        core = jnp.reshape(lax.axis_index("c"), (1,)).astype(jnp.int32)
        for (i, key), got in zip(ids, carried):
            self.sums[i, key] = (self.parts[i, key], got,
                                 add_sibling_parts(self.parts[i, key], got, core, name=f"add_sibling_l{i}_{key}"))

    def mlp_bwd_rider(self, layer1_grads):
        return self._to_sibling([(1, key) for key in LAYER_MATRICES], layer1_grads)

    def after_mlp_bwd(self, carried):
        self._add([(1, key) for key in LAYER_MATRICES], carried)

    def lru_bwd_rider(self, layer0_grads):
        return self._to_sibling(self.EARLY_GRADS, layer0_grads)

    def attention_bwd_rider(self, carried):
        self._add(self.EARLY_GRADS, carried)
        self.travelling = list(self.sums)
        return chips_rider([self.sums[e][2] for e in self.travelling])

    def after_attention_bwd(self, carried):
        for e, got in zip(self.travelling, carried):
            self.from_chips[e] = got


def _pack_replicated(tree):
    return _pack_rows(jnp.concatenate([tree[k].astype(f32).reshape(-1) for k in REPLICATED]), 8)


def kernel(x, p, ev_w_in, ev_lru_conv_w, ev_lru_conv_b, ev_lru_gate_a_w, ev_lru_gate_a_b, ev_lru_gate_x_w, ev_lru_gate_x_b, ev_lru_lambda, ev_w_out, od_w_in, od_ssm_conv_w, od_ssm_conv_b, od_ssm_dt_bias, od_ssm_a_log, od_ssm_d, od_ssm_norm, od_cm_conv_w, od_cm_conv_b, od_cm_ln_g, od_cm_ln_b, od_w_out, norm_mix_pre, norm_mix_post, norm_mlp_pre, norm_mlp_post, norm_ple, mlp_w1, mlp_w2, ple_w_proj, ple_w_gate, loss_target, m_ev_w_in, m_ev_lru_conv_w, m_ev_lru_conv_b, m_ev_lru_gate_a_w, m_ev_lru_gate_a_b, m_ev_lru_gate_x_w, m_ev_lru_gate_x_b, m_ev_lru_lambda, m_ev_w_out, m_od_w_in, m_od_ssm_conv_w, m_od_ssm_conv_b, m_od_ssm_dt_bias, m_od_ssm_a_log, m_od_ssm_d, m_od_ssm_norm, m_od_cm_conv_w, m_od_cm_conv_b, m_od_cm_ln_g, m_od_cm_ln_b, m_od_w_out, m_norm_mix_pre, m_norm_mix_post, m_norm_mlp_pre, m_norm_mlp_post, m_norm_ple, m_mlp_w1, m_mlp_w2, m_ple_w_proj, m_ple_w_gate, v_ev_w_in, v_ev_lru_conv_w, v_ev_lru_conv_b, v_ev_lru_gate_a_w, v_ev_lru_gate_a_b, v_ev_lru_gate_x_w, v_ev_lru_gate_x_b, v_ev_lru_lambda, v_ev_w_out, v_od_w_in, v_od_ssm_conv_w, v_od_ssm_conv_b, v_od_ssm_dt_bias, v_od_ssm_a_log, v_od_ssm_d, v_od_ssm_norm, v_od_cm_conv_w, v_od_cm_conv_b, v_od_cm_ln_g, v_od_cm_ln_b, v_od_w_out, v_norm_mix_pre, v_norm_mix_post, v_norm_mlp_pre, v_norm_mlp_post, v_norm_ple, v_mlp_w1, v_mlp_w2, v_ple_w_proj, v_ple_w_gate):
    ws = [ev_w_in, ev_lru_conv_w, ev_lru_conv_b, ev_lru_gate_a_w, ev_lru_gate_a_b, ev_lru_gate_x_w, ev_lru_gate_x_b, ev_lru_lambda, ev_w_out, od_w_in, od_ssm_conv_w, od_ssm_conv_b, od_ssm_dt_bias, od_ssm_a_log, od_ssm_d, od_ssm_norm, od_cm_conv_w, od_cm_conv_b, od_cm_ln_g, od_cm_ln_b, od_w_out, norm_mix_pre, norm_mix_post, norm_mlp_pre, norm_mlp_post, norm_ple, mlp_w1, mlp_w2, ple_w_proj, ple_w_gate]
    ms = [m_ev_w_in, m_ev_lru_conv_w, m_ev_lru_conv_b, m_ev_lru_gate_a_w, m_ev_lru_gate_a_b, m_ev_lru_gate_x_w, m_ev_lru_gate_x_b, m_ev_lru_lambda, m_ev_w_out, m_od_w_in, m_od_ssm_conv_w, m_od_ssm_conv_b, m_od_ssm_dt_bias, m_od_ssm_a_log, m_od_ssm_d, m_od_ssm_norm, m_od_cm_conv_w, m_od_cm_conv_b, m_od_cm_ln_g, m_od_cm_ln_b, m_od_w_out, m_norm_mix_pre, m_norm_mix_post, m_norm_mlp_pre, m_norm_mlp_post, m_norm_ple, m_mlp_w1, m_mlp_w2, m_ple_w_proj, m_ple_w_gate]
    vs = [v_ev_w_in, v_ev_lru_conv_w, v_ev_lru_conv_b, v_ev_lru_gate_a_w, v_ev_lru_gate_a_b, v_ev_lru_gate_x_w, v_ev_lru_gate_x_b, v_ev_lru_lambda, v_ev_w_out, v_od_w_in, v_od_ssm_conv_w, v_od_ssm_conv_b, v_od_ssm_dt_bias, v_od_ssm_a_log, v_od_ssm_d, v_od_ssm_norm, v_od_cm_conv_w, v_od_cm_conv_b, v_od_cm_ln_g, v_od_cm_ln_b, v_od_w_out, v_norm_mix_pre, v_norm_mix_post, v_norm_mlp_pre, v_norm_mlp_post, v_norm_ple, v_mlp_w1, v_mlp_w2, v_ple_w_proj, v_ple_w_gate]
    w = dict(zip(W_NAMES, ws))
    m = dict(zip(W_NAMES, ms))
    v = dict(zip(W_NAMES, vs))
    full, wl0, wtl0 = _gather_early(w)
    comm = Overlap(w)
    loss_local, grad_x, grads, layer_grads = local_step(x, p, loss_target, full, [wl0, None], [wtl0, None], comm)
    loss = lax.psum(loss_local, ("x", "y", "c"))
    return (loss, grad_x, *_reduce_and_update(grads, layer_grads, w, m, v, comm))


def _reduce_and_update(grads, layer_grads, w, m, v, comm):
    mx, my, mc = _mesh_pos()

    entries = [(i, key) for i in range(len(layer_grads)) for key in LAYER_MATRICES]
    left = [e for e in entries if e not in comm.from_chips]
    parts = [_to_shards(layer_grads[i][key], _layer_entry(i, key)[2]) for i, key in left]
    small = jnp.concatenate([_to_shards(grads[k], axis).reshape(N_DEV, -1) for k, axis in SMALL_SHARDED.items()],
                            axis=1)
    small_rows = _pack_small(w).shape[0]
    small = jnp.pad(small, ((0, 0), (0, small_rows * LANES - small.shape[1]))).reshape(N_DEV, small_rows, LANES)
    parts.append(small)
    from_sibling = scatter_to_sibling(parts, name="scatter_sibling")
    core = jnp.reshape(mc, (1,)).astype(jnp.int32)
    sums = [(a, b, add_sibling_parts(a, b, core, name=f"add_sibling_{i}"))
            for i, (a, b) in enumerate(zip(parts, from_sibling))]
    from_chips = scatter_to_chips([s[2] for s in sums], name="scatter_chips")
    all_sums = {**comm.sums, **dict(zip(left, sums[:-1]))}
    all_from_chips = {**comm.from_chips, **dict(zip(left, from_chips[:-1]))}
    place = jnp.stack([4 * mx + 2 * my + mc, 2 * mx + my]).astype(jnp.int32)
    per_layer = []
    for i in range(len(layer_grads)):
        ws, ms, vs = _layer_shards(w, i), _layer_shards(m, i), _layer_shards(v, i)
        per_layer.append({key: adamw_sharded(*all_sums[i, key][:2], all_from_chips[i, key], place, ws[key], ms[key],
                                             vs[key], name=f"adamw_l{i}_{key}") for key in LAYER_MATRICES})
    g_sh, d_sh, m_sh, v_sh = {}, {}, {}, {}
    for which, tree in enumerate((g_sh, d_sh, m_sh, v_sh)):
        for i in range(len(per_layer)):
            for key in LAYER_MATRICES:
                name, idx, _ = _layer_entry(i, key)
                tree.setdefault(name, {})[idx] = per_layer[i][key][which]
        for name in BIG_SHARDED:
            tree[name] = jnp.stack([tree[name][idx] for idx in sorted(tree[name])], axis=0)
    outs = adamw_sharded(*sums[-1][:2], from_chips[-1], place, _pack_small(w), _pack_small(m), _pack_small(v),
                         name="adamw_small")
    small_shapes = {k: w[k].shape for k in SMALL_SHARDED}
    for tree, o in zip((g_sh, d_sh, m_sh, v_sh), outs):
        tree.update(_unpack(o.reshape(-1), small_shapes))

    rep_parts, = all_gather([_pack_replicated(grads)], name="gather_replicated_grads")
    outs = adamw_replicated(rep_parts, _pack_replicated(w), _pack_replicated(m), _pack_replicated(v),
                            name="adamw_replicated")
    rep_shapes = {k: w[k].shape for k in REPLICATED}
    g_rp, d_rp, m_rp, v_rp = [_unpack(o.reshape(-1), rep_shapes) for o in outs]

    pick = lambda sh, rp: [sh[k] if k in SHARDED else rp[k] for k in W_NAMES]
    return [*pick(g_sh, g_rp), *pick(d_sh, d_rp), *pick(m_sh, m_rp), *pick(v_sh, v_rp)]
```

```python
import math

import jax
import jax.numpy as jnp
from jax import lax
from jax.experimental import pallas as pl
from jax.experimental.pallas import tpu as pltpu

f32 = jnp.float32
MXU_DTYPE = jnp.bfloat16

EPS = 1e-6
LRU_WIDTH = 512
LRU_HEADS = 8
LRU_CONV = 4
LRU_C = 8.0
SB_WIDTH = 512
SB_HEAD_DIM = 64
SSM_WIDTH = 1024
SSM_HEADS = 16
SSM_HEAD_DIM = 64
SSM_GROUPS = 2
SSM_STATE = 128
SSM_CONV = 4
SSM_CHUNK = 128
SSM_XBC = SSM_WIDTH + 2 * SSM_GROUPS * SSM_STATE
CONF_WIDTH = 512
CONF_KERNEL = 31
LANES = 128
N_DEV = 8

ADAM_LR = 0.001
ADAM_B1 = 0.9
ADAM_B2 = 0.999
ADAM_EPS = 1e-08
ADAM_WD = 0.01
ADAM_STEP = 10

VMEM_LIMIT = 56 * 1024 * 1024


def _cparams(*sem):
    return pltpu.CompilerParams(dimension_semantics=sem, vmem_limit_bytes=VMEM_LIMIT)


def _mm(a, b):
    return jnp.dot(a.astype(MXU_DTYPE), b.astype(MXU_DTYPE), preferred_element_type=f32)


def _mm_nt(a, b):
    return lax.dot_general(a.astype(MXU_DTYPE), b.astype(MXU_DTYPE), (((1,), (1,)), ((), ())),
                           preferred_element_type=f32)


def _mm_tn(a, b):
    return lax.dot_general(a.astype(MXU_DTYPE), b.astype(MXU_DTYPE), (((0,), (0,)), ((), ())),
                           preferred_element_type=f32)


def _mm_exact(a, b):
    return jnp.dot(a, b, preferred_element_type=f32, precision=lax.Precision.HIGHEST)


@jax.custom_vjp
def dmm(a, b):
    return _mm(a, b)


def _dmm_fwd(a, b):
    return _mm(a, b), (a, b)


def _dmm_bwd(res, g):
    a, b = res
    return _mm_nt(g, b), _mm_tn(a, g)


dmm.defvjp(_dmm_fwd, _dmm_bwd)


@jax.custom_vjp
def dmm_nt(a, b):
    return _mm_nt(a, b)


def _dmm_nt_fwd(a, b):
    return _mm_nt(a, b), (a, b)


def _dmm_nt_bwd(res, g):
    a, b = res
    return _mm(g, b), _mm_tn(g, a)


dmm_nt.defvjp(_dmm_nt_fwd, _dmm_nt_bwd)


@jax.custom_vjp
def dmm_tn(a, b):
    return _mm_tn(a, b)


def _dmm_tn_fwd(a, b):
    return _mm_tn(a, b), (a, b)


def _dmm_tn_bwd(res, g):
    a, b = res
    return _mm_nt(b, g), _mm(a, g)


dmm_tn.defvjp(_dmm_tn_fwd, _dmm_tn_bwd)


def _rms(x, g):
    r = lax.rsqrt(jnp.mean(x * x, axis=-1, keepdims=True) + EPS)
    return x * r * g


def _rms_bwd(dy, x, g):
    r = lax.rsqrt(jnp.mean(x * x, axis=-1, keepdims=True) + EPS)
    dyg = dy * g
    dx = r * dyg - x * (r * r * r * jnp.mean(dyg * x, axis=-1, keepdims=True))
    return dx, dy * x * r


def _tok(tm, n):
    return pl.BlockSpec((tm, n), lambda i: (i, 0))


def _whole(shape):
    nd = len(shape)
    return pl.BlockSpec(tuple(shape), lambda i: (0,) * nd)


def _acc_rows(ref, val):
    s = jnp.sum(val, axis=0, keepdims=True)

    @pl.when(pl.program_id(0) == 0)
    def _():
        ref[...] = s

    @pl.when(pl.program_id(0) != 0)
    def _():
        ref[...] += s


TOKEN_TILE = 512
WEIGHT_GRAD_TOKENS = 1024


class Part:
    def __init__(self, whole, start, size, axis):
        self.whole, self.start, self.size, self.axis = whole, start, size, axis
        self.shape = tuple(size if a == axis else n for a, n in enumerate(whole.shape))


def _weights(ws):
    wholes, readers = [], []
    for w in ws:
        arr = w.whole if isinstance(w, Part) else w
        idx = next((i for i, a in enumerate(wholes) if a is arr), None)
        if idx is None:
            wholes.append(arr)
            idx = len(wholes) - 1
        if isinstance(w, Part):
            rows = pl.ds(w.start, w.size) if w.axis == 0 else slice(None)
            cols = pl.ds(w.start, w.size) if w.axis == 1 else slice(None)
            readers.append(lambda refs, idx=idx, rows=rows, cols=cols: refs[idx][rows, cols])
        else:
            readers.append(lambda refs, idx=idx: refs[idx][...])
    return wholes, readers


def norm_matmul(h, g, ws, out_dtypes, name):
    t, d = h.shape
    tm = TOKEN_TILE
    wholes, readers = _weights(ws)
    nw = len(wholes)

    def body(h_ref, g_ref, *refs):
        hn = _rms(h_ref[...], g_ref[...]).astype(MXU_DTYPE)
        for read, o_ref in zip(readers, refs[nw:]):
            o_ref[...] = jnp.dot(hn, read(refs[:nw]), preferred_element_type=f32).astype(o_ref.dtype)

    return pl.pallas_call(
        body, name=name, grid=(t // tm,),
        in_specs=[_tok(tm, d), _whole(g.shape)] + [_whole(w.shape) for w in wholes],
        out_specs=[_tok(tm, w.shape[1]) for w in ws],
        out_shape=[jax.ShapeDtypeStruct((t, w.shape[1]), dt) for w, dt in zip(ws, out_dtypes)],
        compiler_params=_cparams("parallel"),
    )(h, g, *wholes)


def matmul_residual_norm(xs, ws, h, g, name, relu2=False):
    t, d = h.shape
    tm = TOKEN_TILE
    nx = len(xs)
    wholes, readers = _weights(ws)
    nw = len(wholes)

    def body(*refs):
        x_refs, w_refs = refs[:nx], refs[nx:nx + nw]
        h_ref, g_ref, ho_ref, m_ref = refs[nx + nw:]
        m = None
        for x_ref, read in zip(x_refs, readers):
            x = x_ref[...]
            if relu2:
                x = jnp.square(jnp.maximum(x.astype(f32), 0.0))
            part = jnp.dot(x.astype(MXU_DTYPE), read(w_refs), preferred_element_type=f32)
            m = part if m is None else m + part
        m_ref[...] = m.astype(m_ref.dtype)
        ho_ref[...] = h_ref[...] + _rms(m, g_ref[...])

    return pl.pallas_call(
        body, name=name, grid=(t // tm,),
        in_specs=[_tok(tm, x.shape[1]) for x in xs] + [_whole(w.shape) for w in wholes]
        + [_tok(tm, d), _whole(g.shape)],
        out_specs=[_tok(tm, d), _tok(tm, d)],
        out_shape=[jax.ShapeDtypeStruct((t, d), f32), jax.ShapeDtypeStruct((t, d), MXU_DTYPE)],
        compiler_params=_cparams("parallel"),
    )(*xs, *wholes, h, g)


def ple_forward(h, p, w_gate, w_proj, g, name, target=None):
    t, d = h.shape
    tm = TOKEN_TILE
    last = target is not None

    def body(h_ref, p_ref, wg_ref, wp_ref, g_ref, *refs):
        hh = h_ref[...]
        gl = jnp.dot(hh.astype(MXU_DTYPE), wg_ref[...], preferred_element_type=f32)
        emb = jnp.dot(p_ref[...].astype(MXU_DTYPE), wp_ref[...], preferred_element_type=f32)
        y = hh + _rms(jax.nn.sigmoid(gl) * emb, g_ref[...])
        if last:
            t_ref, l_ref, dy_ref, gl_ref, emb_ref = refs
            e = y - t_ref[...]
            dy_ref[...] = e * (1.0 / d)
            part = jnp.sum(jnp.sum(e * e, axis=1, keepdims=True), axis=0, keepdims=True) * (0.5 / d)
            _acc_rows(l_ref, jnp.broadcast_to(part, (1, LANES)))
        else:
            y_ref, gl_ref, emb_ref = refs
            y_ref[...] = y
        gl_ref[...] = gl.astype(gl_ref.dtype)
        emb_ref[...] = emb.astype(emb_ref.dtype)

    saved = [jax.ShapeDtypeStruct((t, d), MXU_DTYPE)] * 2
    in_specs = [_tok(tm, d), _tok(tm, p.shape[1]), _whole(w_gate.shape), _whole(w_proj.shape), _whole(g.shape)]
    if last:
        return pl.pallas_call(
            body, name=name, grid=(t // tm,),
            in_specs=in_specs + [_tok(tm, d)],
            out_specs=[_whole((1, LANES))] + [_tok(tm, d)] * 3,
            out_shape=[jax.ShapeDtypeStruct((1, LANES), f32), jax.ShapeDtypeStruct((t, d), f32)] + saved,
            compiler_params=_cparams("arbitrary"),
        )(h, p, w_gate, w_proj, g, target)
    return pl.pallas_call(
        body, name=name, grid=(t // tm,),
        in_specs=in_specs,
        out_specs=[_tok(tm, d)] * 3,
        out_shape=[jax.ShapeDtypeStruct((t, d), f32)] + saved,
        compiler_params=_cparams("parallel"),
    )(h, p, w_gate, w_proj, g)


def bwd_through_norm_in(dh, gs, wts, h, g, name):
    t, d = h.shape
    tm = TOKEN_TILE
    ng = len(gs)
    wholes, readers = _weights(wts)
    nw = len(wholes)

    def body(*refs):
        dh_ref = refs[0]
        g_refs, w_refs = refs[1:1 + ng], refs[1 + ng:1 + ng + nw]
        h_ref, gain_ref, dho_ref, dg_ref = refs[1 + ng + nw:]
        dhn = None
        for g_ref, read in zip(g_refs, readers):
            part = jnp.dot(g_ref[...].astype(MXU_DTYPE), read(w_refs), preferred_element_type=f32)
            dhn = part if dhn is None else dhn + part
        dx, dgr = _rms_bwd(dhn, h_ref[...], gain_ref[...])
        dho_ref[...] = dh_ref[...] + dx
        _acc_rows(dg_ref, dgr)

    return pl.pallas_call(
        body, name=name, grid=(t // tm,),
        in_specs=[_tok(tm, d)] + [_tok(tm, x.shape[1]) for x in gs] + [_whole(w.shape) for w in wholes]
        + [_tok(tm, d), _whole(g.shape)],
        out_specs=[_tok(tm, d), _whole((1, d))],
        out_shape=[jax.ShapeDtypeStruct((t, d), f32), jax.ShapeDtypeStruct((1, d), f32)],
        compiler_params=_cparams("arbitrary"),
    )(dh, *gs, *wholes, h, g)


def bwd_through_norm_out(dh, n, g, wts, out_dtypes, name, relu2_of=None, rider=None):
    t, d = n.shape
    tm = TOKEN_TILE
    nw = len(wts)
    wholes, readers = _weights(wts)
    nwh = len(wholes)
    has_a = relu2_of is not None

    def body(*refs):
        dh_ref, n_ref, gain_ref = refs[:3]
        w_refs = refs[3:3 + nwh]
        rest = refs[3 + nwh:]
        if has_a:
            a_ref, rest = rest[0], rest[1:]
        dn_ref, dx_refs, dg_ref = rest[0], rest[1:1 + nw], rest[1 + nw]
        dn, dgr = _rms_bwd(dh_ref[...], n_ref[...].astype(f32), gain_ref[...])
        dnb = dn.astype(MXU_DTYPE)
        dn_ref[...] = dnb.astype(dn_ref.dtype)
        for read, dx_ref in zip(readers, dx_refs):
            dx = jnp.dot(dnb, read(w_refs), preferred_element_type=f32)
            if has_a:
                dx = dx * (2.0 * jnp.maximum(a_ref[...].astype(f32), 0.0))
            dx_ref[...] = dx.astype(dx_ref.dtype)
        _acc_rows(dg_ref, dgr)

    ins = [dh, n, g, *wholes] + ([relu2_of] if has_a else [])
    in_specs = [_tok(tm, d), _tok(tm, d), _whole(g.shape)] + [_whole(w.shape) for w in wholes]
    if has_a:
        in_specs.append(_tok(tm, relu2_of.shape[1]))
    grid = (t // tm,)
    body, in_specs, out_specs, out_shape, scratch = _ride(
        rider, body, in_specs, [_tok(tm, d)] + [_tok(tm, w.shape[1]) for w in wts] + [_whole((1, d))],
        [jax.ShapeDtypeStruct((t, d), MXU_DTYPE)]
        + [jax.ShapeDtypeStruct((t, w.shape[1]), dt) for w, dt in zip(wts, out_dtypes)]
        + [jax.ShapeDtypeStruct((1, d), f32)], [], grid)
    outs = pl.pallas_call(
        body, name=name, grid=grid, in_specs=in_specs, out_specs=out_specs, out_shape=out_shape,
        scratch_shapes=scratch, compiler_params=_cparams("arbitrary"),
    )(*ins, *(rider.inputs if rider else []))
    if rider:
        return outs[0], list(outs[1:1 + nw]), outs[1 + nw], list(outs[2 + nw:])
    return outs[0], list(outs[1:1 + nw]), outs[1 + nw]


def ple_backward(dh3, h2, gl, emb, g, w_gate_t, name):
    t, d = h2.shape
    tm = TOKEN_TILE

    def body(dh_ref, gl_ref, emb_ref, gain_ref, wt_ref, dho_ref, dgl_ref, demb_ref, dg_ref):
        gate = jax.nn.sigmoid(gl_ref[...].astype(f32))
        emb = emb_ref[...].astype(f32)
        dge, dgr = _rms_bwd(dh_ref[...], gate * emb, gain_ref[...])
        demb_ref[...] = (dge * gate).astype(demb_ref.dtype)
        dgl = (dge * emb * gate * (1.0 - gate)).astype(MXU_DTYPE)
        dgl_ref[...] = dgl.astype(dgl_ref.dtype)
        dho_ref[...] = dh_ref[...] + jnp.dot(dgl, wt_ref[...], preferred_element_type=f32)
        _acc_rows(dg_ref, dgr)

    return pl.pallas_call(
        body, name=name, grid=(t // tm,),
        in_specs=[_tok(tm, d), _tok(tm, d), _tok(tm, d), _whole(g.shape), _whole(w_gate_t.shape)],
        out_specs=[_tok(tm, d), _tok(tm, d), _tok(tm, d), _whole((1, d))],
        out_shape=[jax.ShapeDtypeStruct((t, d), f32), jax.ShapeDtypeStruct((t, d), MXU_DTYPE),
                   jax.ShapeDtypeStruct((t, d), MXU_DTYPE), jax.ShapeDtypeStruct((1, d), f32)],
        compiler_params=_cparams("arbitrary"),
    )(dh3, gl, emb, g, w_gate_t)


def _largest_tile(n, cap):
    if n <= cap:
        return n
    return max(c for c in range(LANES, cap + 1, LANES) if n % c == 0)


def weight_grad(x, gout, name, prologue="none", gain=None):
    t, k = x.shape
    n = gout.shape[1]
    tt = WEIGHT_GRAD_TOKENS
    tn = _largest_tile(n, 1024)
    tk = k if prologue == "rms" else _largest_tile(k, 1024)
    has_gain = prologue == "rms"

    def body(*refs):
        if has_gain:
            x_ref, gain_ref, g_ref, o_ref = refs
        else:
            x_ref, g_ref, o_ref = refs
        x = x_ref[...].astype(f32)
        if prologue == "relu2":
            x = jnp.square(jnp.maximum(x, 0.0))
        elif prologue == "rms":
            x = _rms(x, gain_ref[...])
        part = _mm_tn(x, g_ref[...])

        @pl.when(pl.program_id(2) == 0)
        def _():
            o_ref[...] = part

        @pl.when(pl.program_id(2) != 0)
        def _():
            o_ref[...] += part

    in_specs = [pl.BlockSpec((tt, tk), lambda i, j, s: (s, i))]
    ins = [x]
    if has_gain:
        in_specs.append(pl.BlockSpec(gain.shape, lambda i, j, s: (0, 0)))
        ins.append(gain)
    in_specs.append(pl.BlockSpec((tt, tn), lambda i, j, s: (s, j)))
    ins.append(gout)
    return pl.pallas_call(
        body, name=name, grid=(k // tk, n // tn, t // tt),
        in_specs=in_specs,
        out_specs=pl.BlockSpec((tk, tn), lambda i, j, s: (i, j)),
        out_shape=jax.ShapeDtypeStruct((k, n), f32),
        compiler_params=_cparams("parallel", "parallel", "arbitrary"),
    )(*ins)


def weight_grads_of_norm(x, gain, gouts, name):
    t, k = x.shape
    tt = TOKEN_TILE
    ng = len(gouts)

    def body(x_ref, gain_ref, *refs):
        xn = _rms(x_ref[...], gain_ref[...]).astype(MXU_DTYPE)
        first = pl.program_id(0) == 0
        for g_ref, o_ref in zip(refs[:ng], refs[ng:]):
            _accum(o_ref, _mm_tn(xn, g_ref[...]), first)

    return pl.pallas_call(
        body, name=name, grid=(t // tt,),
        in_specs=[_tok(tt, k), _whole(gain.shape)] + [_tok(tt, g.shape[1]) for g in gouts],
        out_specs=[_whole((k, g.shape[1])) for g in gouts],
        out_shape=[jax.ShapeDtypeStruct((k, g.shape[1]), f32) for g in gouts],
        compiler_params=_cparams("arbitrary"),
    )(x, gain, *gouts)


SEQ_TILE = 256
HALO = 8


def _first_step():
    return jnp.logical_and(pl.program_id(0) == 0, pl.program_id(1) == 0)


def _accum(ref, val, first):
    @pl.when(first)
    def _():
        ref[...] = val

    @pl.when(jnp.logical_not(first))
    def _():
        ref[...] += val


def _softplus(x):
    return jnp.maximum(x, 0.0) + jnp.log1p(jnp.exp(-jnp.abs(x)))


def _neg_expm1(z):
    series = -z * (1.0 + z * (0.5 + z * (1.0 / 6.0 + z * (1.0 / 24.0 + z * (1.0 / 120.0)))))
    return jnp.where(z > -0.05, series, 1.0 - jnp.exp(z))


def _lru_gates(xc, ga, gab, gx, gxb, lam):
    r = jax.nn.sigmoid(dmm(xc, ga) + gab)
    i = jax.nn.sigmoid(dmm(xc, gx) + gxb)
    log_a = -LRU_C * r * _softplus(-lam)
    a = jnp.exp(log_a)
    u = jnp.sqrt(_neg_expm1(2.0 * log_a)) * (i * xc)
    return a, u


def _scan_down(a, u):
    n = a.shape[0]
    rows = lax.broadcasted_iota(jnp.int32, a.shape, 0)
    d = 1
    while d < n:
        keep = rows >= d
        a_s = jnp.where(keep, pltpu.roll(a, d, 0), 1.0)
        u_s = jnp.where(keep, pltpu.roll(u, d, 0), 0.0)
        u = a * u_s + u
        a = a * a_s
        d *= 2
    return a, u


def _scan_up(b, g):
    n = b.shape[0]
    rows = lax.broadcasted_iota(jnp.int32, b.shape, 0)
    d = 1
    while d < n:
        keep = rows < n - d
        b_s = jnp.where(keep, pltpu.roll(b, n - d, 0), 1.0)
        g_s = jnp.where(keep, pltpu.roll(g, n - d, 0), 0.0)
        g = g + b * g_s
        b = b * b_s
        d *= 2
    return g


def _seq_specs(ts, c, nt, reverse=False):
    per = ts // HALO

    def jj(j):
        return (nt - 1 - j) if reverse else j

    tile = pl.BlockSpec((1, ts, c), lambda b, j: (b, jj(j), 0))
    before = pl.BlockSpec((1, HALO, c), lambda b, j: (b, jnp.maximum(jj(j) * per - 1, 0), 0))
    after = pl.BlockSpec((1, HALO, c), lambda b, j: (b, jnp.minimum((jj(j) + 1) * per, nt * per - 1), 0))
    return tile, before, after


def _const2(shape):
    nd = len(shape)
    return pl.BlockSpec(tuple(shape), lambda b, j: (0,) * nd)


def lru_forward(xpre, gate, cw, cb, ga, gab, gx, gxb, lam, name):
    nb, ns, w = xpre.shape
    ts = SEQ_TILE
    nt = ns // ts
    tile, _, _ = _seq_specs(ts, w, nt)

    def body(xp_ref, gt_ref, cw_ref, cb_ref, ga_ref, gab_ref, gx_ref, gxb_ref, lam_ref,
             y_ref, xc_ref, hs_ref, xin, hcar):
        @pl.when(pl.program_id(1) == 0)
        def _():
            xin[0:HALO, :] = jnp.zeros((HALO, w), f32)
            hcar[...] = jnp.zeros_like(hcar)

        xin[HALO:HALO + ts, :] = xp_ref[0]
        xc = jnp.broadcast_to(cb_ref[...], (ts, w))
        for k in range(LRU_CONV):
            xc = xc + cw_ref[k:k + 1, :] * xin[pl.ds(HALO - LRU_CONV + 1 + k, ts), :]
        xin[0:HALO, :] = xin[ts:ts + HALO, :]
        a, u = _lru_gates(xc, ga_ref[...], gab_ref[...], gx_ref[...], gxb_ref[...], lam_ref[...])
        acum, h = _scan_down(a, u)
        h = h + acum * hcar[0:1, :]
        hcar[0:1, :] = h[ts - 1:ts, :]
        xc_ref[0] = xc
        hs_ref[0] = h
        y_ref[0] = (h * jax.nn.gelu(gt_ref[0])).astype(y_ref.dtype)

    params = [cw, cb, ga, gab, gx, gxb, lam]
    return pl.pallas_call(
        body, name=name, grid=(nb, nt),
        in_specs=[tile, tile] + [_const2(p.shape) for p in params],
        out_specs=[tile, tile, tile],
        out_shape=[jax.ShapeDtypeStruct((nb, ns, w), MXU_DTYPE), jax.ShapeDtypeStruct((nb, ns, w), f32),
                   jax.ShapeDtypeStruct((nb, ns, w), f32)],
        scratch_shapes=[pltpu.VMEM((ts + HALO, w), f32), pltpu.VMEM((HALO, w), f32)],
        compiler_params=_cparams("arbitrary", "arbitrary"),
    )(xpre, gate, *params)


def lru_backward(dy, xpre, gate, xc, hs, cw, cb, ga, gab, gx, gxb, lam, name, rider=None):
    nb, ns, w = xpre.shape
    ts = SEQ_TILE
    nt = ns // ts
    tile, before, _ = _seq_specs(ts, w, nt, reverse=True)

    def body(dy_ref, xp_ref, xpb_ref, gt_ref, xc_ref, hs_ref, hsb_ref,
             cw_ref, cb_ref, ga_ref, gab_ref, gx_ref, gxb_ref, lam_ref,
             dxp_ref, dgt_ref, dcw_ref, dcb_ref, dga_ref, dgab_ref, dgx_ref, dgxb_ref, dlam_ref,
             dxc_ext, gcar, xin):
        j = pl.program_id(1)
        first = _first_step()
        at_seq_start = j == nt - 1

        @pl.when(j == 0)
        def _():
            dxc_ext[ts:ts + HALO, :] = jnp.zeros((HALO, w), f32)
            gcar[...] = jnp.zeros_like(gcar)

        gt = gt_ref[0]
        h = hs_ref[0]
        dyv = dy_ref[0].astype(f32)
        gl, gelu_vjp = jax.vjp(jax.nn.gelu, gt)
        dgt_ref[0] = gelu_vjp(dyv * h)[0].astype(dgt_ref.dtype)
        dh = dyv * gl

        (a, _), gates_vjp = jax.vjp(_lru_gates, xc_ref[0], ga_ref[...], gab_ref[...], gx_ref[...], gxb_ref[...],
                                    lam_ref[...])
        rows = lax.broadcasted_iota(jnp.int32, (ts, w), 0)
        dh = dh + jnp.where(rows == ts - 1, gcar[0:1, :], 0.0)
        b = pltpu.roll(a, ts - 1, 0)
        g = _scan_up(b, dh)
        gcar[0:1, :] = a[0:1, :] * g[0:1, :]
        hprev_row = jnp.where(at_seq_start, 0.0, hsb_ref[0][HALO - 1:HALO, :])
        hprev = jnp.where(rows == 0, hprev_row, pltpu.roll(h, 1, 0))
        dxc, dga, dgab, dgx, dgxb, dlam = gates_vjp((g * hprev, g))

        _accum(dga_ref, dga, first)
        _accum(dgx_ref, dgx, first)
        _accum(dgab_ref, dgab, first)
        _accum(dgxb_ref, dgxb, first)
        _accum(dlam_ref, dlam, first)
        _accum(dcb_ref, jnp.sum(dxc, axis=0, keepdims=True), first)

        dxc_ext[0:ts, :] = dxc
        dxp = jnp.zeros((ts, w), f32)
        for k in range(LRU_CONV):
            dxp = dxp + cw_ref[k:k + 1, :] * dxc_ext[pl.ds(LRU_CONV - 1 - k, ts), :]
        dxp_ref[0] = dxp.astype(dxp_ref.dtype)
        dxc_ext[ts:ts + HALO, :] = dxc[0:HALO, :]

        xin[0:HALO, :] = jnp.where(at_seq_start, 0.0, xpb_ref[0])
        xin[HALO:HALO + ts, :] = xp_ref[0]
        dcw_rows = [jnp.sum(dxc * xin[pl.ds(HALO - LRU_CONV + 1 + k, ts), :], axis=0, keepdims=True)
                    for k in range(LRU_CONV)]
        dcw_rows += [jnp.zeros((1, w), f32)] * (HALO - LRU_CONV)
        _accum(dcw_ref, jnp.concatenate(dcw_rows, axis=0), first)

    params = [cw, cb, ga, gab, gx, gxb, lam]
    pshape = lambda p: jax.ShapeDtypeStruct(p.shape, f32)
    grid = (nb, nt)
    n_main = 3 + len(params) - 1
    body, in_specs, out_specs, out_shape, scratch = _ride(
        rider, body, [tile, tile, before, tile, tile, tile, before] + [_const2(p.shape) for p in params],
        [tile, tile, _const2((HALO, w))] + [_const2(p.shape) for p in params[1:]],
        [jax.ShapeDtypeStruct((nb, ns, w), MXU_DTYPE), jax.ShapeDtypeStruct((nb, ns, w), MXU_DTYPE),
         jax.ShapeDtypeStruct((HALO, w), f32)] + [pshape(p) for p in params[1:]],
        [pltpu.VMEM((ts + HALO, w), f32), pltpu.VMEM((HALO, w), f32), pltpu.VMEM((ts + HALO, w), f32)], grid)
    outs = pl.pallas_call(
        body, name=name, grid=grid, in_specs=in_specs, out_specs=out_specs, out_shape=out_shape,
        scratch_shapes=scratch, compiler_params=_cparams("arbitrary", "arbitrary"),
    )(dy, xpre, xpre, gate, xc, hs, hs, *params, *(rider.inputs if rider else []))
    return (list(outs[:n_main]), list(outs[n_main:])) if rider else outs


SB_TILE = 256


def _split_dot(x, m):
    hi = x.astype(MXU_DTYPE)
    lo = (x - hi.astype(f32)).astype(MXU_DTYPE)
    return jnp.dot(hi, m, preferred_element_type=f32) + jnp.dot(lo, m, preferred_element_type=f32)


def _suffix_matrices(n):
    r = lax.broadcasted_iota(jnp.int32, (n, n), 0)
    c = lax.broadcasted_iota(jnp.int32, (n, n), 1)
    return (r > c).astype(MXU_DTYPE), (r >= c).astype(MXU_DTYPE)


LOG2E = 1.4426950408889634


def _sb_logits(qh, kb, strict):
    z = _mm_nt(qh, kb)
    ls = jnp.minimum(z, 0.0) - jnp.log2(1.0 + jnp.exp2(-jnp.abs(z)))
    lk = ls - z
    if strict is not None:
        lk = jnp.where(strict, lk, 0.0)
    return ls, lk


def _head_masked(x, dtype):
    lane = lax.broadcasted_iota(jnp.int32, x.shape, 1)
    return (jnp.where(lane < SB_HEAD_DIM, x, 0.0).astype(dtype), jnp.where(lane >= SB_HEAD_DIM, x, 0.0).astype(dtype))


def _stack_heads(dst, x, tq):
    x0, x1 = _head_masked(x, dst.dtype)
    for blk in range(dst.shape[0]):
        dst[blk, 0:tq, :] = x0[blk * tq:(blk + 1) * tq]
        dst[blk, tq:2 * tq, :] = x1[blk * tq:(blk + 1) * tq]


def _strict_mask(tq):
    rr = lax.broadcasted_iota(jnp.int32, (2 * tq, tq), 0)
    cc = lax.broadcasted_iota(jnp.int32, (2 * tq, tq), 1)
    return cc < jnp.where(rr >= tq, rr - tq, rr)


def _sb_specs(ns):
    npair = SB_WIDTH // LANES
    q = pl.BlockSpec((1, ns, LANES), lambda b, p: (b, 0, p))
    k = pl.BlockSpec((1, ns, LANES), lambda b, p: (b, 0, npair + p))
    v = pl.BlockSpec((1, ns, LANES), lambda b, p: (b, 0, 2 * npair + p))
    return q, k, v, npair


def sb_forward(qkv, name, rider=None):
    nb, ns, _ = qkv.shape
    tq = SB_TILE
    nq = ns // tq
    qspec, kspec, vspec, npair = _sb_specs(ns)

    def body(q_ref, k_ref, v_ref, o_ref, qs, ks, vs, acc):
        scale = 1.0 / math.sqrt(SB_HEAD_DIM)
        _stack_heads(qs, q_ref[0] * (scale * LOG2E), tq)
        ks[...] = k_ref[0].astype(MXU_DTYPE)
        _stack_heads(vs, v_ref[0], tq)
        mx, _ = _suffix_matrices(tq)
        strict = _strict_mask(tq)

        def step(q2, blks, r2, masked):
            kbs = [ks[pl.ds(pl.multiple_of(b * tq, tq), tq), :] for b in blks]
            lg = [_sb_logits(q2, kb, strict if masked else None) for kb in kbs]
            sums = [jnp.dot(lk.astype(MXU_DTYPE), mx, preferred_element_type=f32) for _, lk in lg]
            total = None
            for (ls, lk), s, b in zip(lg, sums, blks):
                a = r2 + s
                w = jnp.exp2(ls + a)
                if masked:
                    w = jnp.where(strict, w, 0.0)
                wb = w.astype(MXU_DTYPE)
                part = (jnp.dot(wb[:tq], vs[b, 0:tq, :], preferred_element_type=f32)
                        + jnp.dot(wb[tq:], vs[b, tq:2 * tq, :], preferred_element_type=f32))
                total = part if total is None else total + part
                r2 = a[:, 0:1] + lk[:, 0:1]
            acc[...] += total
            return r2

        def q_block(qi, carry):
            acc[...] = jnp.zeros_like(acc)
            q2 = qs[qi]
            r2 = step(q2, [qi], jnp.zeros((2 * tq, 1), f32), True)
            r2 = lax.fori_loop(0, lax.shift_right_logical(qi, 2),
                               lambda i, r: step(q2, [qi - 1 - 4 * i - u for u in range(4)], r, False), r2)
            r2 = lax.cond(jnp.bitwise_and(qi, 2) == 2,
                          lambda r: step(q2, [jnp.bitwise_and(qi, 3) - 1, jnp.bitwise_and(qi, 3) - 2], r, False),
                          lambda r: r, r2)
            lax.cond(jnp.bitwise_and(qi, 1) == 1, lambda r: step(q2, [0], r, False), lambda r: r, r2)
            o_ref[0, pl.ds(pl.multiple_of(qi * tq, tq), tq), :] = acc[...]
            return carry

        lax.fori_loop(0, nq, q_block, 0)

    grid = (nb, npair)
    body, in_specs, out_specs, out_shape, scratch = _ride(
        rider, body, [qspec, kspec, vspec], [pl.BlockSpec((1, ns, LANES), lambda b, p: (b, 0, p))],
        [jax.ShapeDtypeStruct((nb, ns, SB_WIDTH), f32)],
        [pltpu.VMEM((nq, 2 * tq, LANES), MXU_DTYPE), pltpu.VMEM((ns, LANES), MXU_DTYPE),
         pltpu.VMEM((nq, 2 * tq, LANES), MXU_DTYPE), pltpu.VMEM((tq, LANES), f32)], grid)
    outs = pl.pallas_call(
        body, name=name, grid=grid, in_specs=in_specs, out_specs=out_specs, out_shape=out_shape,
        scratch_shapes=scratch,
        compiler_params=_cparams("arbitrary", "arbitrary") if rider else _cparams("parallel", "parallel"),
    )(qkv, qkv, qkv, *(rider.inputs if rider else []))
    return (outs[0], list(outs[1:])) if rider else outs[0]


def sb_backward(qkv, o, do, name, rider=None):
    nb, ns, _ = qkv.shape
    tq = SB_TILE
    nq = ns // tq
    qspec, kspec, vspec, npair = _sb_specs(ns)
    ospec = pl.BlockSpec((1, ns, LANES), lambda b, p: (b, 0, p))

    def body(q_ref, k_ref, v_ref, o_ref, do_ref, dq_ref, dk_ref, dv_ref, qs, ks, kcat, vs, dos, dqacc, dkacc, dvacc):
        scale = 1.0 / math.sqrt(SB_HEAD_DIM)
        _stack_heads(qs, q_ref[0] * (scale * LOG2E), tq)
        ks[...] = k_ref[0].astype(MXU_DTYPE)
        _stack_heads(kcat, k_ref[0], tq)
        vs[...] = v_ref[0].astype(MXU_DTYPE)
        _stack_heads(dos, do_ref[0].astype(f32), tq)
        dkacc[...] = jnp.zeros_like(dkacc)
        dvacc[...] = jnp.zeros_like(dvacc)
        mx, mi = _suffix_matrices(tq)
        strict = _strict_mask(tq)

        def step(q2, do2, q2t, do2t, dtot2, blks, carry, masked):
            r2, g2 = carry
            k0s = [pl.multiple_of(b * tq, tq) for b in blks]
            lg = [_sb_logits(q2, ks[pl.ds(k0, tq), :], strict if masked else None) for k0 in k0s]
            dws = [_mm_nt(do2, vs[pl.ds(k0, tq), :]) for k0 in k0s]
            sums = [jnp.dot(lk.astype(MXU_DTYPE), mx, preferred_element_type=f32) for _, lk in lg]
            wbs, es = [], []
            for (ls, lk), s in zip(lg, sums):
                a = r2 + s
                w = jnp.exp2(ls + a)
                if masked:
                    w = jnp.where(strict, w, 0.0)
                wbs.append(w.astype(MXU_DTYPE))
                r2 = a[:, 0:1] + lk[:, 0:1]
            es = [wb.astype(f32) * dw for wb, dw in zip(wbs, dws)]
            esums = [_split_dot(e, mi) for e in es]
            dq = None
            for (ls, _), e, esum, wb, b, k0 in zip(lg, es, esums, wbs, blks, k0s):
                esuf = g2 + esum
                beta = jnp.exp2(ls)
                dz = e - beta * (e + (dtot2 - esuf))
                if masked:
                    dz = jnp.where(strict, dz, 0.0)
                dzb = dz.astype(MXU_DTYPE)
                part = (jnp.dot(dzb[:tq], kcat[b, 0:tq, :], preferred_element_type=f32)
                        + jnp.dot(dzb[tq:], kcat[b, tq:2 * tq, :], preferred_element_type=f32))
                dq = part if dq is None else dq + part
                dkacc[:, pl.ds(k0, tq)] += jnp.dot(q2t, dzb, preferred_element_type=f32)
                dvacc[:, pl.ds(k0, tq)] += jnp.dot(do2t, wb, preferred_element_type=f32)
                g2 = esuf[:, 0:1]
            dqacc[...] += dq
            return r2, g2

        def q_block(qi, carry):
            dqacc[...] = jnp.zeros_like(dqacc)
            q2, do2 = qs[qi], dos[qi]
            q2t, do2t = q2.T, do2.T
            ov = o_ref[0, pl.ds(pl.multiple_of(qi * tq, tq), tq), :]
            dtot2 = jnp.sum(do2.astype(f32) * jnp.concatenate([ov, ov], axis=0), axis=1, keepdims=True)
            zero = jnp.zeros((2 * tq, 1), f32)
            args = (q2, do2, q2t, do2t, dtot2)
            c = step(*args, [qi], (zero, zero), True)
            c = lax.fori_loop(0, lax.shift_right_logical(qi, 2),
                              lambda i, c: step(*args, [qi - 1 - 4 * i - u for u in range(4)], c, False), c)
            c = lax.cond(jnp.bitwise_and(qi, 2) == 2,
                         lambda c: step(*args, [jnp.bitwise_and(qi, 3) - 1, jnp.bitwise_and(qi, 3) - 2], c, False),
                         lambda c: c, c)
            lax.cond(jnp.bitwise_and(qi, 1) == 1, lambda c: step(*args, [0], c, False), lambda c: c, c)
            dq_ref[0, pl.ds(pl.multiple_of(qi * tq, tq), tq), :] = (dqacc[...] * scale).astype(dq_ref.dtype)
            return carry

        lax.fori_loop(0, nq, q_block, 0)
        dk_ref[0] = (dkacc[...].T * (1.0 / LOG2E)).astype(dk_ref.dtype)
        dv_ref[0] = dvacc[...].T.astype(dv_ref.dtype)

    dshape = jax.ShapeDtypeStruct((nb, ns, SB_WIDTH), MXU_DTYPE)
    stacked = pltpu.VMEM((nq, 2 * tq, LANES), MXU_DTYPE)
    flat = pltpu.VMEM((ns, LANES), MXU_DTYPE)
    grid = (nb, npair)
    body, in_specs, out_specs, out_shape, scratch = _ride(
        rider, body, [qspec, kspec, vspec, ospec, ospec], [ospec, ospec, ospec], [dshape, dshape, dshape],
        [stacked, flat, stacked, flat, stacked,
         pltpu.VMEM((tq, LANES), f32), pltpu.VMEM((LANES, ns), f32), pltpu.VMEM((LANES, ns), f32)], grid)
    outs = pl.pallas_call(
        body, name=name, grid=grid, in_specs=in_specs, out_specs=out_specs, out_shape=out_shape,
        scratch_shapes=scratch,
        compiler_params=_cparams("arbitrary", "arbitrary") if rider else _cparams("parallel", "parallel"),
    )(qkv, qkv, qkv, o, do, *(rider.inputs if rider else []))
    return (list(outs[:3]), list(outs[3:])) if rider else list(outs)


SSM_PAIRS = SSM_HEADS // 2
PAIRS_PER_GROUP = SSM_PAIRS // SSM_GROUPS
GROUP_WIDTH = SSM_WIDTH // SSM_GROUPS


def _silu(x):
    return x * jax.nn.sigmoid(x)


def _ssd_chunk(xs_pre, b_pre, c_pre, dt_raw, dt_raw_t, z, st, dt_bias_r, dt_bias_c, a_log_r, a_log_c, d_skip,
               gains):
    n = dt_raw.shape[0]
    rows = lax.broadcasted_iota(jnp.int32, (n, n), 0)
    cols = lax.broadcasted_iota(jnp.int32, (n, n), 1)
    tril = cols <= rows
    tri_l = tril.astype(f32)
    tri_u = (rows <= cols).astype(f32)
    lane = lax.broadcasted_iota(jnp.int32, (n, LANES), 1)
    sub = lax.broadcasted_iota(jnp.int32, (LANES, n), 0)

    dt = _softplus(dt_raw + dt_bias_r)
    a_r = -jnp.exp(a_log_r)
    da = dt * a_r
    acs = _mm_exact(tri_l, da)
    dt_t = _softplus(dt_raw_t + dt_bias_c)
    acs_t = _mm_exact(dt_t * (-jnp.exp(a_log_c)), tri_u)

    bs = [_silu(b) for b in b_pre]
    cs = [_silu(c) for c in c_pre]
    cb = [dmm_nt(cs[g], bs[g]) for g in range(SSM_GROUPS)]

    end = jnp.sum(da, axis=0, keepdims=True)
    lane_row = lax.broadcasted_iota(jnp.int32, (1, LANES), 1)
    first_head = lane < SSM_HEAD_DIM
    first_head_row = lane_row < SSM_HEAD_DIM

    def head_col(v, h):
        return jnp.sum(jnp.where((lane if v.shape[0] == n else lane_row) == h, v, 0.0), axis=1, keepdims=True)

    ys, st_new = [], []
    for p in range(SSM_PAIRS):
        g = p // PAIRS_PER_GROUP
        h0, h1 = 2 * p, 2 * p + 1
        xs = _silu(xs_pre[p])
        acols = [head_col(acs, h0), head_col(acs, h1)]
        dt_p = jnp.where(first_head, head_col(dt, h0), head_col(dt, h1))
        acs_p = jnp.where(first_head, acols[0], acols[1])
        end_p = jnp.where(first_head_row, head_col(end, h0), head_col(end, h1))
        dsk_p = jnp.where(first_head_row, head_col(d_skip, h0), head_col(d_skip, h1))
        xdt = xs * dt_p
        y = jnp.exp(acs_p) * dmm(cs[g], st[p])
        for hh in range(2):
            row = jnp.sum(jnp.where(sub == 2 * p + hh, acs_t, 0.0), axis=0, keepdims=True)
            decay = jnp.where(tril, jnp.exp(jnp.where(tril, acols[hh] - row, 0.0)), 0.0)
            head = first_head if hh == 0 else jnp.logical_not(first_head)
            y = y + dmm(cb[g] * decay, jnp.where(head, xdt, 0.0))
        st_new.append(jnp.exp(end_p) * st[p] + dmm_tn(bs[g], xdt * jnp.exp(end_p - acs_p)))
        ys.append(y + dsk_p * xs)
    out = []
    for g in range(SSM_GROUPS):
        yg = jnp.concatenate(ys[g * PAIRS_PER_GROUP:(g + 1) * PAIRS_PER_GROUP], axis=1) * _silu(z[g])
        out.append(_rms(yg, gains[g]))
    return out, st_new


def _ssd_chunk_inputs(xconv, dtr, z, st_ref, gain):
    xs_pre = [xconv[:, LANES * p:LANES * (p + 1)] for p in range(SSM_PAIRS)]
    b0 = SSM_WIDTH
    c0 = SSM_WIDTH + SSM_GROUPS * SSM_STATE
    b_pre = [xconv[:, b0 + SSM_STATE * g:b0 + SSM_STATE * (g + 1)] for g in range(SSM_GROUPS)]
    c_pre = [xconv[:, c0 + SSM_STATE * g:c0 + SSM_STATE * (g + 1)] for g in range(SSM_GROUPS)]
    zs = [z[:, GROUP_WIDTH * g:GROUP_WIDTH * (g + 1)] for g in range(SSM_GROUPS)]
    sts = [st_ref[p] for p in range(SSM_PAIRS)]
    gains = [gain[:, GROUP_WIDTH * g:GROUP_WIDTH * (g + 1)] for g in range(SSM_GROUPS)]
    return xs_pre, b_pre, c_pre, dtr, dtr.T, zs, sts, gains


def ssd_forward(xbc, dt_raw, z, cw, cb, dbr, dbc, alr, alc, dsk, gain, name):
    nb, ns, wx = xbc.shape
    ln = SSM_CHUNK
    nt = ns // ln
    tile = lambda c: pl.BlockSpec((1, ln, c), lambda b, j: (b, j, 0))
    st_spec = pl.BlockSpec((1, 1, SSM_PAIRS, SSM_STATE, LANES), lambda b, j: (b, j, 0, 0, 0))

    def body(xbc_ref, dt_ref, z_ref, cw_ref, cb_ref, dbr_ref, dbc_ref, alr_ref, alc_ref, dsk_ref, gain_ref,
             y_ref, xconv_ref, stp_ref, xin, st):
        @pl.when(pl.program_id(1) == 0)
        def _():
            xin[0:HALO, :] = jnp.zeros((HALO, wx), f32)
            st[...] = jnp.zeros_like(st)

        xin[HALO:HALO + ln, :] = xbc_ref[0]
        xconv = jnp.broadcast_to(cb_ref[...], (ln, wx))
        for k in range(SSM_CONV):
            xconv = xconv + cw_ref[k:k + 1, :] * xin[pl.ds(HALO - SSM_CONV + 1 + k, ln), :]
        xin[0:HALO, :] = xin[ln:ln + HALO, :]
        xconv_ref[0] = xconv
        stp_ref[0, 0] = st[...]
        xs_pre, b_pre, c_pre, dtr, dtr_t, zs, sts, gains = _ssd_chunk_inputs(xconv, dt_ref[0], z_ref[0], st,
                                                                             gain_ref[...])
        out, st_new = _ssd_chunk(xs_pre, b_pre, c_pre, dtr, dtr_t, zs, sts, dbr_ref[...], dbc_ref[...],
                                 alr_ref[...], alc_ref[...], dsk_ref[...], gains)
        y_ref[0] = jnp.concatenate(out, axis=1).astype(y_ref.dtype)
        for p in range(SSM_PAIRS):
            st[p] = st_new[p]

    params = [cw, cb, dbr, dbc, alr, alc, dsk, gain]
    return pl.pallas_call(
        body, name=name, grid=(nb, nt),
        in_specs=[tile(wx), tile(LANES), tile(SSM_WIDTH)] + [_const2(p.shape) for p in params],
        out_specs=[tile(SSM_WIDTH), tile(wx), st_spec],
        out_shape=[jax.ShapeDtypeStruct((nb, ns, SSM_WIDTH), MXU_DTYPE), jax.ShapeDtypeStruct((nb, ns, wx), f32),
                   jax.ShapeDtypeStruct((nb, nt, SSM_PAIRS, SSM_STATE, LANES), f32)],
        scratch_shapes=[pltpu.VMEM((ln + HALO, wx), f32), pltpu.VMEM((SSM_PAIRS, SSM_STATE, LANES), f32)],
        compiler_params=_cparams("arbitrary", "arbitrary"),
    )(xbc, dt_raw, z, *params)


def ssd_backward(dy, xbc, xconv, dt_raw, z, stp, cw, cb, dbr, dbc, alr, alc, dsk, gain, name):
    nb, ns, wx = xbc.shape
    ln = SSM_CHUNK
    nt = ns // ln
    per = ln // HALO
    rj = lambda j: nt - 1 - j
    tile = lambda c: pl.BlockSpec((1, ln, c), lambda b, j: (b, rj(j), 0))
    before = pl.BlockSpec((1, HALO, wx), lambda b, j: (b, jnp.maximum(rj(j) * per - 1, 0), 0))
    st_spec = pl.BlockSpec((1, 1, SSM_PAIRS, SSM_STATE, LANES), lambda b, j: (b, rj(j), 0, 0, 0))

    def body(dy_ref, xbc_ref, xbcb_ref, xconv_ref, dt_ref, z_ref, stp_ref,
             cw_ref, cb_ref, dbr_ref, dbc_ref, alr_ref, alc_ref, dsk_ref, gain_ref,
             dxbc_ref, ddt_ref, dz_ref, dcw_ref, dcb_ref, ddbr_ref, ddbc_ref, dalr_ref, dalc_ref, ddsk_ref, dgain_ref,
             dxc_ext, dst, xin):
        j = pl.program_id(1)
        first = _first_step()
        at_seq_start = j == nt - 1

        @pl.when(j == 0)
        def _():
            dxc_ext[ln:ln + HALO, :] = jnp.zeros((HALO, wx), f32)
            dst[...] = jnp.zeros_like(dst)

        xs_pre, b_pre, c_pre, dtr, dtr_t, zs, sts, gains = _ssd_chunk_inputs(xconv_ref[0], dt_ref[0], z_ref[0],
                                                                             stp_ref.at[0, 0], gain_ref[...])
        _, vjp = jax.vjp(_ssd_chunk, xs_pre, b_pre, c_pre, dtr, dtr_t, zs, sts, dbr_ref[...], dbc_ref[...],
                         alr_ref[...], alc_ref[...], dsk_ref[...], gains)
        dyv = dy_ref[0].astype(f32)
        cot = ([dyv[:, GROUP_WIDTH * g:GROUP_WIDTH * (g + 1)] for g in range(SSM_GROUPS)],
               [dst[p] for p in range(SSM_PAIRS)])
        dxs, db, dc, ddt, ddt_t, dzs, dsts, ddbr, ddbc, dalr, dalc, ddsk, dgains = vjp(cot)
        for p in range(SSM_PAIRS):
            dst[p] = dsts[p]
        ddt_ref[0] = (ddt + ddt_t.T).astype(ddt_ref.dtype)
        dz_ref[0] = jnp.concatenate(dzs, axis=1).astype(dz_ref.dtype)
        _accum(ddbr_ref, ddbr, first)
        _accum(ddbc_ref, ddbc, first)
        _accum(dalr_ref, dalr, first)
        _accum(dalc_ref, dalc, first)
        _accum(ddsk_ref, ddsk, first)
        _accum(dgain_ref, jnp.concatenate(dgains, axis=1), first)

        dxc = jnp.concatenate(dxs + db + dc, axis=1)
        _accum(dcb_ref, jnp.sum(dxc, axis=0, keepdims=True), first)
        dxc_ext[0:ln, :] = dxc
        dxp = jnp.zeros((ln, wx), f32)
        for k in range(SSM_CONV):
            dxp = dxp + cw_ref[k:k + 1, :] * dxc_ext[pl.ds(SSM_CONV - 1 - k, ln), :]
        dxbc_ref[0] = dxp.astype(dxbc_ref.dtype)
        dxc_ext[ln:ln + HALO, :] = dxc[0:HALO, :]

        xin[0:HALO, :] = jnp.where(at_seq_start, 0.0, xbcb_ref[0])
        xin[HALO:HALO + ln, :] = xbc_ref[0]
        dcw_rows = [jnp.sum(dxc * xin[pl.ds(HALO - SSM_CONV + 1 + k, ln), :], axis=0, keepdims=True)
                    for k in range(SSM_CONV)]
        dcw_rows += [jnp.zeros((1, wx), f32)] * (HALO - SSM_CONV)
        _accum(dcw_ref, jnp.concatenate(dcw_rows, axis=0), first)

    params = [cw, cb, dbr, dbc, alr, alc, dsk, gain]
    pshape = lambda p: jax.ShapeDtypeStruct(p.shape, f32)
    return pl.pallas_call(
        body, name=name, grid=(nb, nt),
        in_specs=[tile(SSM_WIDTH), tile(wx), before, tile(wx), tile(LANES), tile(SSM_WIDTH), st_spec]
        + [_const2(p.shape) for p in params],
        out_specs=[tile(wx), tile(LANES), tile(SSM_WIDTH), _const2((HALO, wx))] + [_const2(p.shape) for p in params[1:]],
        out_shape=[jax.ShapeDtypeStruct((nb, ns, wx), MXU_DTYPE), jax.ShapeDtypeStruct((nb, ns, LANES), MXU_DTYPE),
                   jax.ShapeDtypeStruct((nb, ns, SSM_WIDTH), MXU_DTYPE), jax.ShapeDtypeStruct((HALO, wx), f32)]
        + [pshape(p) for p in params[1:]],
        scratch_shapes=[pltpu.VMEM((ln + HALO, wx), f32), pltpu.VMEM((SSM_PAIRS, SSM_STATE, LANES), f32),
                        pltpu.VMEM((ln + HALO, wx), f32)],
        compiler_params=_cparams("arbitrary", "arbitrary"),
    )(dy, xbc, xbc, xconv, dt_raw, z, stp, *params)


CONF_HALO = 32
CONF_OFF = CONF_HALO - CONF_KERNEL + 1


def _conf_specs(ts, c, nt):
    per = ts // CONF_HALO
    tile = pl.BlockSpec((1, ts, c), lambda b, j: (b, j, 0))
    before = pl.BlockSpec((1, CONF_HALO, c), lambda b, j: (b, jnp.maximum(j * per - 1, 0), 0))
    after = pl.BlockSpec((1, CONF_HALO, c), lambda b, j: (b, jnp.minimum((j + 1) * per, nt * per - 1), 0))
    return tile, before, after


SUBLANES = 8


def _shifted_copies(dst, x):
    rows = x.shape[0]
    dst[0] = x
    for b in range(1, SUBLANES):
        dst[b] = pltpu.roll(x, rows - b, 0)


def _window(copies, off, size):
    b = off % SUBLANES
    return copies[b, pl.ds(off - b, size), :]


CONV_ROW_BLOCK = 32


def _taps_blocked(out_ref, copies, cw_ref, offsets, bias=None):
    rows, w = out_ref.shape
    rb = CONV_ROW_BLOCK

    def block(i, carry):
        r0 = pl.multiple_of(i * rb, rb)
        acc = jnp.zeros((rb, w), f32) if bias is None else jnp.broadcast_to(bias, (rb, w))
        for k, off in enumerate(offsets):
            b = off % SUBLANES
            acc = acc + cw_ref[k:k + 1, :] * copies[b, pl.ds(r0 + (off - b), rb), :]
        out_ref[pl.ds(r0, rb), :] = acc
        return carry

    lax.fori_loop(0, rows // rb, block, 0)


def _glu(x):
    return x[:, :CONF_WIDTH] * jax.nn.sigmoid(x[:, CONF_WIDTH:])


def _layernorm_parts(c):
    xc = c - jnp.mean(c, axis=-1, keepdims=True)
    r = lax.rsqrt(jnp.mean(xc * xc, axis=-1, keepdims=True) + EPS)
    return xc * r, r


def conf_forward(glu, cw, cb, ln_g, ln_b, name):
    nb, ns, wg = glu.shape
    w = CONF_WIDTH
    ts = SEQ_TILE
    nt = ns // ts
    tile, before, _ = _conf_specs(ts, wg, nt)

    def body(x_ref, xb_ref, cw_ref, cb_ref, g_ref, b_ref, y_ref, u_rot, conv_out):
        _shifted_copies(u_rot, jnp.concatenate(
            [jnp.where(pl.program_id(1) == 0, 0.0, _glu(xb_ref[0])), _glu(x_ref[0])], axis=0))
        _taps_blocked(conv_out, u_rot, cw_ref, [CONF_OFF + k for k in range(CONF_KERNEL)], cb_ref[...])
        xhat, _ = _layernorm_parts(conv_out[...])
        y_ref[0] = _silu(xhat * g_ref[...] + b_ref[...]).astype(y_ref.dtype)

    params = [cw, cb, ln_g, ln_b]
    return pl.pallas_call(
        body, name=name, grid=(nb, nt),
        in_specs=[tile, before] + [_const2(p.shape) for p in params],
        out_specs=pl.BlockSpec((1, ts, w), lambda b, j: (b, j, 0)),
        out_shape=jax.ShapeDtypeStruct((nb, ns, w), MXU_DTYPE),
        scratch_shapes=[pltpu.VMEM((SUBLANES, ts + CONF_HALO, w), f32), pltpu.VMEM((ts, w), f32)],
        compiler_params=_cparams("parallel", "parallel"),
    )(glu, glu, *params)


def conf_backward(dy, glu, cw, cb, ln_g, ln_b, name):
    nb, ns, wg = glu.shape
    w = CONF_WIDTH
    ts = SEQ_TILE
    nt = ns // ts
    te = ts + CONF_HALO
    tile, before, after = _conf_specs(ts, wg, nt)
    dtile, _, dafter = _conf_specs(ts, w, nt)

    def body(dy_ref, dya_ref, x_ref, xb_ref, xa_ref, cw_ref, cb_ref, g_ref, b_ref,
             dx_ref, dcw_ref, dcb_ref, dg_ref, db_ref, u_ext, dc_ext, conv_out, du_out):
        j = pl.program_id(1)
        first = _first_step()
        x = x_ref[0]
        _shifted_copies(u_ext, jnp.concatenate(
            [jnp.where(j == 0, 0.0, _glu(xb_ref[0])), _glu(x), _glu(xa_ref[0])], axis=0))
        _taps_blocked(conv_out, u_ext, cw_ref, [CONF_OFF + k for k in range(CONF_KERNEL)], cb_ref[...])
        xhat, r = _layernorm_parts(conv_out[...])
        lnout = xhat * g_ref[...] + b_ref[...]
        sg = jax.nn.sigmoid(lnout)
        rows = lax.broadcasted_iota(jnp.int32, (te, w), 0)
        dyv = jnp.concatenate([dy_ref[0].astype(f32), dya_ref[0].astype(f32)], axis=0)
        dyv = jnp.where(jnp.logical_and(j == nt - 1, rows >= ts), 0.0, dyv)
        dln = dyv * sg * (1.0 + lnout * (1.0 - sg))
        in_tile = rows < ts
        _accum(dg_ref, jnp.sum(jnp.where(in_tile, dln * xhat, 0.0), axis=0, keepdims=True), first)
        _accum(db_ref, jnp.sum(jnp.where(in_tile, dln, 0.0), axis=0, keepdims=True), first)
        dxh = dln * g_ref[...]
        dconv = r * (dxh - jnp.mean(dxh, axis=-1, keepdims=True) - xhat * jnp.mean(dxh * xhat, axis=-1, keepdims=True))
        _shifted_copies(dc_ext, dconv)
        dct = dconv[0:ts, :]
        _accum(dcb_ref, jnp.sum(dct, axis=0, keepdims=True), first)
        _taps_blocked(du_out, dc_ext, cw_ref, [CONF_KERNEL - 1 - k for k in range(CONF_KERNEL)])
        du = du_out[...]
        dcw_rows = []
        for k in range(CONF_KERNEL):
            dcw_rows.append(jnp.sum(dct * _window(u_ext, CONF_OFF + k, ts), axis=0, keepdims=True))
        dcw_rows.append(jnp.zeros((1, w), f32))
        _accum(dcw_ref, jnp.concatenate(dcw_rows, axis=0), first)
        sb = jax.nn.sigmoid(x[:, w:])
        dx_ref[0] = jnp.concatenate([du * sb, du * x[:, :w] * sb * (1.0 - sb)], axis=1).astype(dx_ref.dtype)

    params = [cw, cb, ln_g, ln_b]
    return pl.pallas_call(
        body, name=name, grid=(nb, nt),
        in_specs=[dtile, dafter, tile, before, after] + [_const2(p.shape) for p in params],
        out_specs=[tile] + [_const2(p.shape) for p in params],
        out_shape=[jax.ShapeDtypeStruct((nb, ns, wg), MXU_DTYPE)] + [jax.ShapeDtypeStruct(p.shape, f32) for p in params],
        scratch_shapes=[pltpu.VMEM((SUBLANES, te + CONF_HALO, w), f32), pltpu.VMEM((SUBLANES, te, w), f32),
                        pltpu.VMEM((te, w), f32), pltpu.VMEM((ts, w), f32)],
        compiler_params=_cparams("arbitrary", "arbitrary"),
    )(dy, dy, glu, glu, glu, *params)


def _row(v):
    return v.reshape(1, -1).astype(f32)


def _pad_to(v, n, axis):
    pads = [(0, 0)] * v.ndim
    pads[axis] = (0, n - v.shape[axis])
    return jnp.pad(v, pads)


def _block_diag(w):
    nh, d, _ = w.shape
    eye = jnp.eye(nh, dtype=w.dtype)
    return (eye[:, None, :, None] * w[:, :, None, :]).reshape(nh * d, nh * d)


def _diag_blocks(m, nh):
    d = m.shape[0] // nh
    idx = jnp.arange(nh)
    return m.reshape(nh, d, nh, d)[idx, :, idx, :]


def _mix_even_fwd(h, gpre, w, wl, nb, ns, rider=None):
    t = nb * ns
    w_in = wl["w_in"]
    w_lx, w_lg = Part(w_in, 0, LRU_WIDTH, 1), Part(w_in, LRU_WIDTH, LRU_WIDTH, 1)
    w_qkv = Part(w_in, 2 * LRU_WIDTH, 3 * SB_WIDTH, 1)
    xpre, gate, qkv = norm_matmul(h, gpre, [w_lx, w_lg, w_qkv], [f32, f32, f32], name="ev_in_proj")
    lru_p = [w["ev_lru_conv_w"][0], _row(w["ev_lru_conv_b"][0]),
             _block_diag(w["ev_lru_gate_a_w"][0]).astype(MXU_DTYPE), _row(w["ev_lru_gate_a_b"][0]),
             _block_diag(w["ev_lru_gate_x_w"][0]).astype(MXU_DTYPE), _row(w["ev_lru_gate_x_b"][0]),
             _row(w["ev_lru_lambda"][0])]
    xpre3, gate3, qkv3 = xpre.reshape(nb, ns, -1), gate.reshape(nb, ns, -1), qkv.reshape(nb, ns, -1)
    y_a, xc, hs = lru_forward(xpre3, gate3, *lru_p, name="ev_lru_fwd")
    o = sb_forward(qkv3, name="ev_sb_fwd", rider=rider)
    carried = None
    if rider is not None:
        o, carried = o
    ys = [y_a.reshape(t, -1), o.reshape(t, -1)]
    saved = dict(xpre=xpre3, gate=gate3, qkv=qkv3, xc=xc, hs=hs, o=o, lru_p=lru_p)
    return ys, saved, carried


def _mix_even_bwd(dys, saved, wtl, nb, ns, lru_rider=None, attention_rider=None):
    t = nb * ns
    dy_a, dy_b = [d.reshape(nb, ns, -1) for d in dys]
    outs = lru_backward(dy_a, saved["xpre"], saved["gate"], saved["xc"], saved["hs"], *saved["lru_p"],
                        name="ev_lru_bwd", rider=lru_rider)
    lru_carried = None
    if lru_rider is not None:
        outs, lru_carried = outs
    dxp, dgt, dcw, dcb, dga, dgab, dgx, dgxb, dlam = outs
    rider = attention_rider(lru_carried) if attention_rider is not None else None
    carried = None
    if rider is None:
        dq, dk, dv = sb_backward(saved["qkv"], saved["o"], dy_b, name="ev_sb_bwd")
    else:
        (dq, dk, dv), carried = sb_backward(saved["qkv"], saved["o"], dy_b, name="ev_sb_bwd", rider=rider)
    w_in_t = wtl["w_in"]
    pieces = [dxp, dgt, dq, dk, dv]
    gs = [d.reshape(t, -1) for d in pieces]
    wts = [Part(w_in_t, LRU_WIDTH * i, LRU_WIDTH, 0) for i in range(5)]
    grads = {
        "ev_lru_conv_w": dcw[:LRU_CONV][None], "ev_lru_conv_b": dcb,
        "ev_lru_gate_a_w": _diag_blocks(dga, LRU_HEADS)[None], "ev_lru_gate_a_b": dgab,
        "ev_lru_gate_x_w": _diag_blocks(dgx, LRU_HEADS)[None], "ev_lru_gate_x_b": dgxb,
        "ev_lru_lambda": dlam,
    }
    return gs, wts, grads, carried


def _odd_params(w):
    ssd_p = [w["od_ssm_conv_w"][0], _row(w["od_ssm_conv_b"][0]),
             _pad_to(_row(w["od_ssm_dt_bias"][0]), LANES, 1), _pad_to(_row(w["od_ssm_dt_bias"][0]), LANES, 1).T,
             _pad_to(_row(w["od_ssm_a_log"][0]), LANES, 1), _pad_to(_row(w["od_ssm_a_log"][0]), LANES, 1).T,
             _pad_to(_row(w["od_ssm_d"][0]), LANES, 1), _row(w["od_ssm_norm"][0])]
    conf_p = [_pad_to(w["od_cm_conv_w"][0], CONF_HALO, 0), _row(w["od_cm_conv_b"][0]),
              _row(w["od_cm_ln_g"][0]), _row(w["od_cm_ln_b"][0])]
    return ssd_p, conf_p


ODD_SPLITS = (SSM_WIDTH, SSM_WIDTH + SSM_XBC, SSM_WIDTH + SSM_XBC + SSM_HEADS)


def _mix_odd_fwd(h, gpre, w, wl, nb, ns, rider=None):
    assert rider is None
    t = nb * ns
    w_in = wl["w_in"]
    s0, s1, s2 = ODD_SPLITS
    w_al = jnp.concatenate([w_in[:, :s1], w_in[:, s2:], _pad_to(w_in[:, s1:s2], LANES, 1)], axis=1)
    widths = (s0, s1 - s0, w_in.shape[1] - s2, LANES)
    starts = (0, s0, s1, s1 + widths[2])
    zz, xbc, glu, dtr = norm_matmul(h, gpre, [Part(w_al, a, n, 1) for a, n in zip(starts, widths)], [f32] * 4,
                                    name="od_in_proj")
    ssd_p, conf_p = _odd_params(w)
    zz3, xbc3, dtr3, glu3 = [a.reshape(nb, ns, -1) for a in (zz, xbc, dtr, glu)]
    y_c, xconv, stp = ssd_forward(xbc3, dtr3, zz3, *ssd_p, name="od_ssd_fwd")
    y_d = conf_forward(glu3, *conf_p, name="od_conf_fwd")
    ys = [y_c.reshape(t, -1), y_d.reshape(t, -1)]
    saved = dict(z=zz3, xbc=xbc3, dtr=dtr3, glu=glu3, xconv=xconv, stp=stp, ssd_p=ssd_p, conf_p=conf_p)
    return ys, saved, None


def _mix_odd_bwd(dys, saved, wtl, nb, ns, lru_rider=None, attention_rider=None):
    assert lru_rider is None and attention_rider is None
    t = nb * ns
    dy_c, dy_d = [d.reshape(nb, ns, -1) for d in dys]
    outs = ssd_backward(dy_c, saved["xbc"], saved["xconv"], saved["dtr"], saved["z"], saved["stp"], *saved["ssd_p"],
                        name="od_ssd_bwd")
    dxbc, ddt, dz, dcw, dcb, ddbr, ddbc, dalr, dalc, ddsk, dgain = outs
    dglu, ccw, ccb, clg, clb = conf_backward(dy_d, saved["glu"], *saved["conf_p"], name="od_conf_bwd")
    w_in_t = wtl["w_in"]
    s0, s1, s2 = ODD_SPLITS
    carried = None
    gs = [d.reshape(t, -1) for d in (dz, dxbc, dglu, ddt)]
    wt_al = jnp.concatenate([w_in_t[:s1], w_in_t[s2:], _pad_to(w_in_t[s1:s2], LANES, 0)], axis=0)
    widths = (s0, s1 - s0, w_in_t.shape[0] - s2, LANES)
    starts = (0, s0, s1, s1 + widths[2])
    wts = [Part(wt_al, a, n, 0) for a, n in zip(starts, widths)]
    nh = SSM_HEADS
    grads = {
        "od_ssm_conv_w": dcw[:SSM_CONV][None], "od_ssm_conv_b": dcb,
        "od_ssm_dt_bias": ddbr[:, :nh] + ddbc[:nh, 0][None], "od_ssm_a_log": dalr[:, :nh] + dalc[:nh, 0][None],
        "od_ssm_d": ddsk[:, :nh], "od_ssm_norm": dgain,
        "od_cm_conv_w": ccw[:CONF_KERNEL][None], "od_cm_conv_b": ccb, "od_cm_ln_g": clg, "od_cm_ln_b": clb,
    }
    return gs, wts, grads, carried


LAYER_MATRICES = ("w_in", "w_out", "mlp_w1", "mlp_w2", "ple_w_proj", "ple_w_gate")
NORM_NAMES = ("norm_mix_pre", "norm_mix_post", "norm_mlp_pre", "norm_mlp_post", "norm_ple")


class NoOverlap:
    sums, from_chips = {}, {}

    def attention_fwd_rider(self):
        return None

    def weights_arrived(self, carried, wl, wtl):
        raise NotImplementedError

    def mlp_bwd_rider(self, layer1_grads):
        return None

    def after_mlp_bwd(self, carried):
        pass

    def lru_bwd_rider(self, layer0_grads):
        return None

    def attention_bwd_rider(self, carried):
        return None

    def after_attention_bwd(self, carried):
        pass


OUT_SPLIT = (LRU_WIDTH, SSM_WIDTH)


def local_step(x, p, target, w, wl, wtl, comm=NoOverlap()):
    nb, ns, d = x.shape
    t = nb * ns
    h = x.reshape(t, d)
    depth = p.shape[0]
    wl, wtl = list(wl), list(wtl)
    tapes = []
    for i in range(depth):
        even = i % 2 == 0
        tag = f"l{i}_"
        gpre = _row(w["norm_mix_pre"][i])
        rider = comm.attention_fwd_rider() if i == 0 else None
        ys, saved, carried = (_mix_even_fwd if even else _mix_odd_fwd)(h, gpre, w, wl[i], nb, ns, rider)
        if rider is not None:
            wl, wtl = comm.weights_arrived(carried, wl, wtl)
        w_out = wl[i]["w_out"]
        split = OUT_SPLIT[i % 2]
        w_outs = [Part(w_out, 0, split, 0), Part(w_out, split, w_out.shape[0] - split, 0)]
        h1, m = matmul_residual_norm(ys, w_outs, h, _row(w["norm_mix_post"][i]), name=tag + "out_proj")
        a1, = norm_matmul(h1, _row(w["norm_mlp_pre"][i]), [wl[i]["mlp_w1"]], [MXU_DTYPE], name=tag + "mlp_up")
        h2, f = matmul_residual_norm([a1], [wl[i]["mlp_w2"]], h1, _row(w["norm_mlp_post"][i]), name=tag + "mlp_down",
                                     relu2=True)
        pi = p[i].reshape(t, -1)
        ple_args = (h2, pi, wl[i]["ple_w_gate"], wl[i]["ple_w_proj"], _row(w["norm_ple"][i]))
        if i < depth - 1:
            h3, gl, emb = ple_forward(*ple_args, name=tag + "ple")
        else:
            loss_row, dh, gl, emb = ple_forward(*ple_args, name=tag + "ple_loss", target=target.reshape(t, d))
            h3 = None
        tapes.append(dict(h=h, ys=ys, w_outs=w_outs, saved=saved, h1=h1, m=m, a1=a1, h2=h2, f=f, pi=pi, gl=gl,
                          emb=emb))
        h = h3

    grads = {}
    norm_grads = {k: [None] * depth for k in NORM_NAMES}
    layer_grads = [None] * depth
    for i in reversed(range(depth)):
        even = i % 2 == 0
        tag = f"l{i}_"
        tp = tapes[i]
        lg = {}
        to_sibling = comm.mlp_bwd_rider(layer_grads[1]) if i == 0 else None
        dh2, dgl, demb, dg = ple_backward(dh, tp["h2"], tp["gl"], tp["emb"], _row(w["norm_ple"][i]),
                                          wtl[i]["ple_w_gate"], name=tag + "ple_bwd")
        norm_grads["norm_ple"][i] = dg
        lg["ple_w_gate"] = weight_grad(tp["h2"], dgl, name=tag + "dw_gate")
        lg["ple_w_proj"] = weight_grad(tp["pi"], demb, name=tag + "dw_proj")
        outs = bwd_through_norm_out(dh2, tp["f"], _row(w["norm_mlp_post"][i]), [wtl[i]["mlp_w2"]], [MXU_DTYPE],
                                    name=tag + "mlp_down_bwd", relu2_of=tp["a1"], rider=to_sibling)
        d_f, (da1,), dg = outs[:3]
        if to_sibling is not None:
            comm.after_mlp_bwd(outs[3])
        norm_grads["norm_mlp_post"][i] = dg
        lg["mlp_w2"] = weight_grad(tp["a1"], d_f, name=tag + "dw2", prologue="relu2")
        gpre = _row(w["norm_mlp_pre"][i])
        dh1, dg = bwd_through_norm_in(dh2, [da1], [wtl[i]["mlp_w1"]], tp["h1"], gpre, name=tag + "mlp_up_bwd")
        norm_grads["norm_mlp_pre"][i] = dg
        lg["mlp_w1"] = weight_grad(tp["h1"], da1, name=tag + "dw1", prologue="rms", gain=gpre)
        wt_out = wtl[i]["w_out"]
        split = tp["w_outs"][0].shape[0]
        dm, dys, dg = bwd_through_norm_out(dh1, tp["m"], _row(w["norm_mix_post"][i]),
                                           [Part(wt_out, 0, split, 1),
                                            Part(wt_out, split, wt_out.shape[1] - split, 1)],
                                           [f32, MXU_DTYPE if even else f32],
                                           name=tag + "out_proj_bwd")
        norm_grads["norm_mix_post"][i] = dg
        lg["w_out"] = jnp.concatenate([weight_grad(y, dm, name=tag + f"dw_out{k}") for k, y in enumerate(tp["ys"])],
                                      axis=0)
        lru_rider = comm.lru_bwd_rider(lg) if i == 0 else None
        gs, wts, mix_grads, carried = (_mix_even_bwd if even else _mix_odd_bwd)(
            dys, tp["saved"], wtl[i], nb, ns, lru_rider, comm.attention_bwd_rider if lru_rider is not None else None)
        if carried is not None:
            comm.after_attention_bwd(carried)
        grads.update(mix_grads)
        gpre = _row(w["norm_mix_pre"][i])
        dh, dg = bwd_through_norm_in(dh1, gs, wts, tp["h"], gpre, name=tag + "in_proj_bwd")
        norm_grads["norm_mix_pre"][i] = dg
        dw_in = weight_grads_of_norm(tp["h"], gpre, gs, name=tag + "dw_in")
        if not even:
            dw_in = [dw_in[0], dw_in[1], dw_in[3][:, :SSM_HEADS], dw_in[2]]
        lg["w_in"] = jnp.concatenate(dw_in, axis=1)
        layer_grads[i] = lg
    for k, v in norm_grads.items():
        grads[k] = jnp.concatenate(v, axis=0)
    return loss_row[0, 0], dh.reshape(nb, ns, d), grads, layer_grads


MESH_ID = pl.DeviceIdType.MESH
ANY = pl.BlockSpec(memory_space=pl.ANY)


def _mesh_pos():
    return lax.axis_index("x"), lax.axis_index("y"), lax.axis_index("c")


def all_gather(shards, name):
    return _run_alone(gather_rider(shards), name)


class Rider:
    def __init__(self, inputs, out_shapes, scratch_shapes, start, finish, middle=None):
        self.inputs, self.out_shapes, self.scratch_shapes = list(inputs), list(out_shapes), list(scratch_shapes)
        self.start, self.finish, self.middle = start, finish, middle


def _run_alone(rider, name):
    ni, no = len(rider.inputs), len(rider.out_shapes)

    def body(*refs):
        args = (refs[:ni], refs[ni:ni + no], refs[ni + no:])
        rider.start(*args)
        if rider.middle is not None:
            rider.middle(*args)
        rider.finish(*args)

    return pl.pallas_call(
        body, name=name, out_shape=rider.out_shapes, in_specs=[ANY] * ni, out_specs=[ANY] * no,
        scratch_shapes=rider.scratch_shapes,
    )(*rider.inputs)


def _ride(rider, body, in_specs, out_specs, out_shape, scratch_shapes, grid):
    in_specs, out_specs, out_shape = list(in_specs), list(out_specs), list(out_shape)
    scratch_shapes = list(scratch_shapes)
    if rider is None:
        return body, in_specs, out_specs, out_shape, scratch_shapes
    n_in, n_out, n_scr = len(in_specs), len(out_specs), len(scratch_shapes)
    ri, ro = len(rider.inputs), len(rider.out_shapes)
    total = math.prod(grid)

    def carrying(*refs):
        ins, r_ins = refs[:n_in], refs[n_in:n_in + ri]
        o0 = n_in + ri
        outs, r_outs = refs[o0:o0 + n_out], refs[o0 + n_out:o0 + n_out + ro]
        s0 = o0 + n_out + ro
        scr, r_scr = refs[s0:s0 + n_scr], refs[s0 + n_scr:]
        step = pl.program_id(0)
        for ax in range(1, len(grid)):
            step = step * grid[ax] + pl.program_id(ax)
        args = (r_ins, r_outs, r_scr)
        pl.when(step == 0)(lambda: rider.start(*args))
        if rider.middle is not None:
            pl.when(step == total // 2)(lambda: rider.middle(*args))
        body(*ins, *outs, *scr)
        pl.when(step == total - 1)(lambda: rider.finish(*args))

    return (carrying, in_specs + [ANY] * ri, out_specs + [ANY] * ro, out_shape + rider.out_shapes,
            scratch_shapes + rider.scratch_shapes)


def gather_rider(shards):
    n = len(shards)

    def parts(x_refs, out_refs, scr):
        send_sems, recv_sems, local_sems = scr
        x, y, c = _mesh_pos()
        chips = [(1 - x, y), (x, 1 - y), (1 - x, 1 - y)]

        def slot(a, px, py, pc):
            return out_refs[a].at[4 * px + 2 * py + pc]

        def copy(a, k, block, to, src=None):
            return pltpu.make_async_remote_copy(
                src_ref=slot(a, *block) if src is None else src, dst_ref=slot(a, *block),
                send_sem=send_sems.at[7 * a + k], recv_sem=recv_sems.at[7 * a + k], device_id=to,
                device_id_type=MESH_ID)

        me, sibling = (x, y, c), (x, y, 1 - c)
        def mine():
            return [pltpu.make_async_copy(x_refs[a], slot(a, *me), local_sems.at[a]) for a in range(n)]

        def first():
            out = []
            for j, chip in enumerate(chips):
                out += [copy(a, 1 + j, me, (*chip, c), src=x_refs[a]) for a in range(n)]
            return out + [copy(a, 0, me, sibling, src=x_refs[a]) for a in range(n)]

        def passed(j):
            return [copy(a, 4 + j, (*chips[j], c), sibling) for a in range(n)]

        return me, sibling, chips, c, copy, mine, first, passed

    def start(x_refs, out_refs, scr):
        _, _, _, _, _, mine, first, _ = parts(x_refs, out_refs, scr)
        for cp in mine() + first():
            cp.start()

    def middle(x_refs, out_refs, scr):
        me, _, chips, c, copy, _, _, passed = parts(x_refs, out_refs, scr)
        for j, chip in enumerate(chips):
            for a, fwd in enumerate(passed(j)):
                copy(a, 1 + j, (*chip, c), me).wait_recv()
                fwd.start()

    def finish(x_refs, out_refs, scr):
        me, sibling, chips, c, copy, mine, first, passed = parts(x_refs, out_refs, scr)
        for a in range(n):
            copy(a, 0, sibling, me).wait_recv()
        for j, chip in enumerate(chips):
            for a in range(n):
                copy(a, 4 + j, (*chip, 1 - c), me).wait_recv()
        for cp in first() + [cp for j in range(len(chips)) for cp in passed(j)]:
            cp.wait_send()
        for cp in mine():
            cp.wait()

    return Rider(shards, [jax.ShapeDtypeStruct((N_DEV,) + s.shape, s.dtype) for s in shards],
                 [pltpu.SemaphoreType.DMA((7 * n,)), pltpu.SemaphoreType.DMA((7 * n,)), pltpu.SemaphoreType.DMA((n,))],
                 start, finish, middle)


def scatter_to_sibling(parts, name):
    return _run_alone(sibling_rider(parts), name)


def sibling_rider(parts):
    n = len(parts)

    def copies(g_refs, out_refs, scr):
        send_sems, recv_sems = scr
        x, y, c = _mesh_pos()
        return [pltpu.make_async_remote_copy(
            src_ref=g_refs[a].at[2 * chip + (1 - c)], dst_ref=out_refs[a].at[chip],
            send_sem=send_sems.at[4 * a + chip], recv_sem=recv_sems.at[4 * a + chip], device_id=(x, y, 1 - c),
            device_id_type=MESH_ID) for a in range(n) for chip in range(4)]

    def start(*refs):
        for cp in copies(*refs):
            cp.start()

    def finish(*refs):
        cps = copies(*refs)
        for cp in cps:
            cp.wait_recv()
        for cp in cps:
            cp.wait_send()

    return Rider(parts, [jax.ShapeDtypeStruct((4,) + p.shape[1:], p.dtype) for p in parts],
                 [pltpu.SemaphoreType.DMA((4 * n,)), pltpu.SemaphoreType.DMA((4 * n,))], start, finish)


def scatter_to_chips(partials, name):
    return _run_alone(chips_rider(partials), name)


def chips_rider(partials):
    n = len(partials)

    def copies(p_refs, out_refs, scr):
        send_sems, recv_sems = scr
        x, y, c = _mesh_pos()
        chips = [(1 - x, y), (x, 1 - y), (1 - x, 1 - y)]
        return [pltpu.make_async_remote_copy(
            src_ref=p_refs[a].at[2 * px + py], dst_ref=out_refs[a].at[j],
            send_sem=send_sems.at[3 * a + j], recv_sem=recv_sems.at[3 * a + j], device_id=(px, py, c),
            device_id_type=MESH_ID) for a in range(n) for j, (px, py) in enumerate(chips)]

    def start(*refs):
        for cp in copies(*refs):
            cp.start()

    def finish(*refs):
        cps = copies(*refs)
        for cp in cps:
            cp.wait_recv()
        for cp in cps:
            cp.wait_send()

    return Rider(partials, [jax.ShapeDtypeStruct((3,) + p.shape[1:], p.dtype) for p in partials],
                 [pltpu.SemaphoreType.DMA((3 * n,)), pltpu.SemaphoreType.DMA((3 * n,))], start, finish)


ICI_DTYPE = jnp.bfloat16
ELEMENTWISE_BLOCK_BYTES = 1 << 20


def _row_tile(rows, cols):
    cap = max(16, ELEMENTWISE_BLOCK_BYTES // (4 * cols))
    best = [t for t in range(16, min(rows, cap) + 1, 16) if rows % t == 0]
    return best[-1] if best else rows


def add_sibling_parts(parts, received, core, name):
    _, r, n = parts.shape
    tr = _row_tile(r, n)

    def body(c_ref, a_ref, b_ref, o_ref, ob_ref):
        s = a_ref[...] + b_ref[...]
        o_ref[...] = s
        ob_ref[...] = s.astype(ob_ref.dtype)

    blk = pl.BlockSpec((1, tr, n), lambda i, j, c_ref: (i, j, 0))
    return pl.pallas_call(
        body, name=name,
        grid_spec=pltpu.PrefetchScalarGridSpec(
            num_scalar_prefetch=1, grid=(4, r // tr),
            in_specs=[pl.BlockSpec((1, tr, n), lambda i, j, c_ref: (2 * i + c_ref[0], j, 0)), blk],
            out_specs=[blk, blk]),
        out_shape=[jax.ShapeDtypeStruct((4, r, n), f32), jax.ShapeDtypeStruct((4, r, n), ICI_DTYPE)],
        compiler_params=_cparams("parallel", "parallel"),
    )(core, parts, received)


def _adamw(w, g, m, v):
    m = ADAM_B1 * m + (1.0 - ADAM_B1) * g
    v = ADAM_B2 * v + (1.0 - ADAM_B2) * jnp.square(g)
    m_hat = m / (1.0 - ADAM_B1 ** ADAM_STEP)
    v_hat = v / (1.0 - ADAM_B2 ** ADAM_STEP)
    delta = -ADAM_LR * (m_hat / (jnp.sqrt(v_hat) + ADAM_EPS) + ADAM_WD * w)
    return delta, m, v


def adamw_sharded(partial, received, chip, w, m, v, name):
    _, r, n = partial.shape

    def body(k_ref, p_ref, r_ref, w_ref, m_ref, v_ref, g_out, d_out, m_out, v_out):
        g = p_ref[0] + r_ref[0].astype(f32)
        g = g + r_ref[1].astype(f32)
        g = g + r_ref[2].astype(f32)
        delta, mn, vn = _adamw(w_ref[...], g, m_ref[...], v_ref[...])
        g_out[...] = g
        d_out[...] = delta
        m_out[...] = mn
        v_out[...] = vn

    tr = _row_tile(r, n)
    flat = pl.BlockSpec((tr, n), lambda j, k_ref: (j, 0))
    return pl.pallas_call(
        body, name=name,
        grid_spec=pltpu.PrefetchScalarGridSpec(
            num_scalar_prefetch=1, grid=(r // tr,),
            in_specs=[pl.BlockSpec((1, tr, n), lambda j, k_ref: (k_ref[0], j, 0)),
                      pl.BlockSpec((3, tr, n), lambda j, k_ref: (0, j, 0)), flat, flat, flat],
            out_specs=[flat] * 4),
        out_shape=[jax.ShapeDtypeStruct((r, n), f32)] * 4,
        compiler_params=_cparams("parallel"),
    )(chip, partial, received, w, m, v)


def adamw_replicated(gathered, w, m, v, name):
    _, r, n = gathered.shape

    def body(g_ref, w_ref, m_ref, v_ref, g_out, d_out, m_out, v_out):
        g = g_ref[0]
        for k in range(1, N_DEV):
            g = g + g_ref[k]
        delta, mn, vn = _adamw(w_ref[...], g, m_ref[...], v_ref[...])
        g_out[...] = g
        d_out[...] = delta
        m_out[...] = mn
        v_out[...] = vn

    return pl.pallas_call(
        body, name=name,
        out_shape=[jax.ShapeDtypeStruct((r, n), f32)] * 4,
        compiler_params=pltpu.CompilerParams(vmem_limit_bytes=VMEM_LIMIT),
    )(gathered, w, m, v)


W_NAMES = ['ev_w_in', 'ev_lru_conv_w', 'ev_lru_conv_b', 'ev_lru_gate_a_w', 'ev_lru_gate_a_b', 'ev_lru_gate_x_w',
           'ev_lru_gate_x_b', 'ev_lru_lambda', 'ev_w_out', 'od_w_in', 'od_ssm_conv_w', 'od_ssm_conv_b',
           'od_ssm_dt_bias', 'od_ssm_a_log', 'od_ssm_d', 'od_ssm_norm', 'od_cm_conv_w', 'od_cm_conv_b', 'od_cm_ln_g',
           'od_cm_ln_b', 'od_w_out', 'norm_mix_pre', 'norm_mix_post', 'norm_mlp_pre', 'norm_mlp_post', 'norm_ple',
           'mlp_w1', 'mlp_w2', 'ple_w_proj', 'ple_w_gate']
BIG_SHARDED = {'ev_w_in': 2, 'ev_w_out': 1, 'od_w_in': 2, 'od_w_out': 1, 'mlp_w1': 2, 'mlp_w2': 1, 'ple_w_proj': 2,
               'ple_w_gate': 1}
SMALL_SHARDED = {'ev_lru_conv_w': 2, 'od_ssm_conv_w': 2, 'od_ssm_conv_b': 1, 'od_ssm_norm': 1, 'od_cm_conv_w': 2,
                 'od_cm_conv_b': 1, 'od_cm_ln_g': 1, 'od_cm_ln_b': 1}
SHARDED = {**BIG_SHARDED, **SMALL_SHARDED}
REPLICATED = [n for n in W_NAMES if n not in SHARDED]


def _round_up(n, k):
    return -(-n // k) * k


def _pack_rows(flat, rows_multiple):
    n = flat.shape[0]
    total = _round_up(n, LANES * rows_multiple)
    return jnp.pad(flat, (0, total - n)).reshape(-1, LANES)


def _unpack(flat, shapes):
    out, off = {}, 0
    for name, shape in shapes.items():
        size = math.prod(shape)
        out[name] = flat[off:off + size].reshape(shape)
        off += size
    return out


def _unshard(g8, shape, axis):
    g = jnp.moveaxis(g8.reshape((N_DEV,) + tuple(shape)), 0, axis)
    return g.reshape(tuple(shape[:axis]) + (N_DEV * shape[axis],) + tuple(shape[axis + 1:]))


def _to_shards(g, axis):
    shard = g.shape[axis] // N_DEV
    g = g.reshape(g.shape[:axis] + (N_DEV, shard) + g.shape[axis + 1:])
    return jnp.moveaxis(g, axis, 0)


def _pack_small(tree):
    return _pack_rows(jnp.concatenate([tree[k].astype(f32).reshape(-1) for k in SMALL_SHARDED]), 16)


def _layer_entry(i, key):
    mixer = "ev" if i % 2 == 0 else "od"
    return {"w_in": (mixer + "_w_in", i // 2, 1), "w_out": (mixer + "_w_out", i // 2, 0),
            "mlp_w1": ("mlp_w1", i, 1), "mlp_w2": ("mlp_w2", i, 0),
            "ple_w_proj": ("ple_w_proj", i, 1), "ple_w_gate": ("ple_w_gate", i, 0)}[key]


def _layer_shards(tree, i):
    out = {}
    for key in LAYER_MATRICES:
        name, idx, _ = _layer_entry(i, key)
        out[key] = tree[name][idx]
    return out


def _gather_early(w):
    small = _pack_small(w)
    terms, rest = [], small
    for _ in range(3):
        term = rest.astype(MXU_DTYPE)
        terms.append(term)
        rest = rest - term.astype(f32)
    outs = all_gather([_layer_shards(w, 0)["w_in"].astype(MXU_DTYPE), jnp.concatenate(terms, axis=0)],
                      name="gather_weights")
    w_in = _unshard(outs[0], outs[0].shape[1:], _layer_entry(0, "w_in")[2])
    wl, wtl = {"w_in": w_in}, {"w_in": w_in.T}
    full = {k: w[k] for k in REPLICATED}
    t = outs[-1].astype(f32)
    nr = small.shape[0]
    vals = (t[:, :nr] + t[:, nr:2 * nr] + t[:, 2 * nr:]).reshape(N_DEV, -1)
    off = 0
    for k, axis in SMALL_SHARDED.items():
        size = math.prod(w[k].shape)
        full[k] = _unshard(vals[:, off:off + size], w[k].shape, axis)
        off += size
    return full, wl, wtl


class Overlap:
    LATE = [(0, key) for key in LAYER_MATRICES if key != "w_in"] + [(1, key) for key in LAYER_MATRICES]
    EARLY_GRADS = [(0, key) for key in LAYER_MATRICES if key != "w_in"]

    def __init__(self, w):
        self.w = w
        self.sums, self.from_chips, self.parts = {}, {}, {}

    def attention_fwd_rider(self):
        shards = [_layer_shards(self.w, 0), _layer_shards(self.w, 1)]
        return gather_rider([shards[i][key].astype(MXU_DTYPE) for i, key in self.LATE])

    def weights_arrived(self, carried, wl, wtl):
        wl = [dict(wl[0]), {}]
        wtl = [dict(wtl[0]), {}]
        for (i, key), g8 in zip(self.LATE, carried):
            wl[i][key] = _unshard(g8, g8.shape[1:], _layer_entry(i, key)[2])
            wtl[i][key] = wl[i][key].T
        return wl, wtl

    def _to_sibling(self, ids, layer_grads):
        for i, key in ids:
            self.parts[i, key] = _to_shards(layer_grads[key], _layer_entry(i, key)[2])
        return sibling_rider([self.parts[e] for e in ids])

    def _add(self, ids, carried):
        core = jnp.reshape(lax.axis_index("c"), (1,)).astype(jnp.int32)
        for (i, key), got in zip(ids, carried):
            self.sums[i, key] = add_sibling_parts(self.parts[i, key], got, core, name=f"add_sibling_l{i}_{key}")

    def mlp_bwd_rider(self, layer1_grads):
        return self._to_sibling([(1, key) for key in LAYER_MATRICES], layer1_grads)

    def after_mlp_bwd(self, carried):
        self._add([(1, key) for key in LAYER_MATRICES], carried)

    def lru_bwd_rider(self, layer0_grads):
        return self._to_sibling(self.EARLY_GRADS, layer0_grads)

    def attention_bwd_rider(self, carried):
        self._add(self.EARLY_GRADS, carried)
        self.travelling = list(self.sums)
        return chips_rider([self.sums[e][1] for e in self.travelling])

    def after_attention_bwd(self, carried):
        for e, got in zip(self.travelling, carried):
            self.from_chips[e] = got


def _pack_replicated(tree):
    return _pack_rows(jnp.concatenate([tree[k].astype(f32).reshape(-1) for k in REPLICATED]), 8)


def kernel(x, p, ev_w_in, ev_lru_conv_w, ev_lru_conv_b, ev_lru_gate_a_w, ev_lru_gate_a_b, ev_lru_gate_x_w, ev_lru_gate_x_b, ev_lru_lambda, ev_w_out, od_w_in, od_ssm_conv_w, od_ssm_conv_b, od_ssm_dt_bias, od_ssm_a_log, od_ssm_d, od_ssm_norm, od_cm_conv_w, od_cm_conv_b, od_cm_ln_g, od_cm_ln_b, od_w_out, norm_mix_pre, norm_mix_post, norm_mlp_pre, norm_mlp_post, norm_ple, mlp_w1, mlp_w2, ple_w_proj, ple_w_gate, loss_target, m_ev_w_in, m_ev_lru_conv_w, m_ev_lru_conv_b, m_ev_lru_gate_a_w, m_ev_lru_gate_a_b, m_ev_lru_gate_x_w, m_ev_lru_gate_x_b, m_ev_lru_lambda, m_ev_w_out, m_od_w_in, m_od_ssm_conv_w, m_od_ssm_conv_b, m_od_ssm_dt_bias, m_od_ssm_a_log, m_od_ssm_d, m_od_ssm_norm, m_od_cm_conv_w, m_od_cm_conv_b, m_od_cm_ln_g, m_od_cm_ln_b, m_od_w_out, m_norm_mix_pre, m_norm_mix_post, m_norm_mlp_pre, m_norm_mlp_post, m_norm_ple, m_mlp_w1, m_mlp_w2, m_ple_w_proj, m_ple_w_gate, v_ev_w_in, v_ev_lru_conv_w, v_ev_lru_conv_b, v_ev_lru_gate_a_w, v_ev_lru_gate_a_b, v_ev_lru_gate_x_w, v_ev_lru_gate_x_b, v_ev_lru_lambda, v_ev_w_out, v_od_w_in, v_od_ssm_conv_w, v_od_ssm_conv_b, v_od_ssm_dt_bias, v_od_ssm_a_log, v_od_ssm_d, v_od_ssm_norm, v_od_cm_conv_w, v_od_cm_conv_b, v_od_cm_ln_g, v_od_cm_ln_b, v_od_w_out, v_norm_mix_pre, v_norm_mix_post, v_norm_mlp_pre, v_norm_mlp_post, v_norm_ple, v_mlp_w1, v_mlp_w2, v_ple_w_proj, v_ple_w_gate):
    ws = [ev_w_in, ev_lru_conv_w, ev_lru_conv_b, ev_lru_gate_a_w, ev_lru_gate_a_b, ev_lru_gate_x_w, ev_lru_gate_x_b, ev_lru_lambda, ev_w_out, od_w_in, od_ssm_conv_w, od_ssm_conv_b, od_ssm_dt_bias, od_ssm_a_log, od_ssm_d, od_ssm_norm, od_cm_conv_w, od_cm_conv_b, od_cm_ln_g, od_cm_ln_b, od_w_out, norm_mix_pre, norm_mix_post, norm_mlp_pre, norm_mlp_post, norm_ple, mlp_w1, mlp_w2, ple_w_proj, ple_w_gate]
    ms = [m_ev_w_in, m_ev_lru_conv_w, m_ev_lru_conv_b, m_ev_lru_gate_a_w, m_ev_lru_gate_a_b, m_ev_lru_gate_x_w, m_ev_lru_gate_x_b, m_ev_lru_lambda, m_ev_w_out, m_od_w_in, m_od_ssm_conv_w, m_od_ssm_conv_b, m_od_ssm_dt_bias, m_od_ssm_a_log, m_od_ssm_d, m_od_ssm_norm, m_od_cm_conv_w, m_od_cm_conv_b, m_od_cm_ln_g, m_od_cm_ln_b, m_od_w_out, m_norm_mix_pre, m_norm_mix_post, m_norm_mlp_pre, m_norm_mlp_post, m_norm_ple, m_mlp_w1, m_mlp_w2, m_ple_w_proj, m_ple_w_gate]
    vs = [v_ev_w_in, v_ev_lru_conv_w, v_ev_lru_conv_b, v_ev_lru_gate_a_w, v_ev_lru_gate_a_b, v_ev_lru_gate_x_w, v_ev_lru_gate_x_b, v_ev_lru_lambda, v_ev_w_out, v_od_w_in, v_od_ssm_conv_w, v_od_ssm_conv_b, v_od_ssm_dt_bias, v_od_ssm_a_log, v_od_ssm_d, v_od_ssm_norm, v_od_cm_conv_w, v_od_cm_conv_b, v_od_cm_ln_g, v_od_cm_ln_b, v_od_w_out, v_norm_mix_pre, v_norm_mix_post, v_norm_mlp_pre, v_norm_mlp_post, v_norm_ple, v_mlp_w1, v_mlp_w2, v_ple_w_proj, v_ple_w_gate]
    w = dict(zip(W_NAMES, ws))
    m = dict(zip(W_NAMES, ms))
    v = dict(zip(W_NAMES, vs))
    full, wl0, wtl0 = _gather_early(w)
    comm = Overlap(w)
    loss_local, grad_x, grads, layer_grads = local_step(x, p, loss_target, full, [wl0, None], [wtl0, None], comm)
    loss = lax.psum(loss_local, ("x", "y", "c"))
    return (loss, grad_x, *_reduce_and_update(grads, layer_grads, w, m, v, comm))


def _reduce_and_update(grads, layer_grads, w, m, v, comm):
    mx, my, mc = _mesh_pos()

    entries = [(i, key) for i in range(len(layer_grads)) for key in LAYER_MATRICES]
    left = [e for e in entries if e not in comm.from_chips]
    parts = [_to_shards(layer_grads[i][key], _layer_entry(i, key)[2]) for i, key in left]
    small = jnp.concatenate([_to_shards(grads[k], axis).reshape(N_DEV, -1) for k, axis in SMALL_SHARDED.items()],
                            axis=1)
    small_rows = _pack_small(w).shape[0]
    small = jnp.pad(small, ((0, 0), (0, small_rows * LANES - small.shape[1]))).reshape(N_DEV, small_rows, LANES)
    parts.append(small)
    from_sibling = scatter_to_sibling(parts, name="scatter_sibling")
    core = jnp.reshape(mc, (1,)).astype(jnp.int32)
    sums = [add_sibling_parts(a, b, core, name=f"add_sibling_{i}") for i, (a, b) in enumerate(zip(parts, from_sibling))]
    from_chips = scatter_to_chips([s[1] for s in sums], name="scatter_chips")
    all_sums = {**comm.sums, **dict(zip(left, sums[:-1]))}
    all_from_chips = {**comm.from_chips, **dict(zip(left, from_chips[:-1]))}
    chip = jnp.reshape(2 * mx + my, (1,)).astype(jnp.int32)
    per_layer = []
    for i in range(len(layer_grads)):
        ws, ms, vs = _layer_shards(w, i), _layer_shards(m, i), _layer_shards(v, i)
        per_layer.append({key: adamw_sharded(all_sums[i, key][0], all_from_chips[i, key], chip, ws[key], ms[key],
                                             vs[key], name=f"adamw_l{i}_{key}") for key in LAYER_MATRICES})
    g_sh, d_sh, m_sh, v_sh = {}, {}, {}, {}
    for which, tree in enumerate((g_sh, d_sh, m_sh, v_sh)):
        for i in range(len(per_layer)):
            for key in LAYER_MATRICES:
                name, idx, _ = _layer_entry(i, key)
                tree.setdefault(name, {})[idx] = per_layer[i][key][which]
        for name in BIG_SHARDED:
            tree[name] = jnp.stack([tree[name][idx] for idx in sorted(tree[name])], axis=0)
    outs = adamw_sharded(sums[-1][0], from_chips[-1], chip, _pack_small(w), _pack_small(m), _pack_small(v),
                         name="adamw_small")
    small_shapes = {k: w[k].shape for k in SMALL_SHARDED}
    for tree, o in zip((g_sh, d_sh, m_sh, v_sh), outs):
        tree.update(_unpack(o.reshape(-1), small_shapes))

    rep_parts, = all_gather([_pack_replicated(grads)], name="gather_replicated_grads")
    outs = adamw_replicated(rep_parts, _pack_replicated(w), _pack_replicated(m), _pack_replicated(v),
                            name="adamw_replicated")
    rep_shapes = {k: w[k].shape for k in REPLICATED}
    g_rp, d_rp, m_rp, v_rp = [_unpack(o.reshape(-1), rep_shapes) for o in outs]

    pick = lambda sh, rp: [sh[k] if k in SHARDED else rp[k] for k in W_NAMES]
    return [*pick(g_sh, g_rp), *pick(d_sh, d_rp), *pick(m_sh, m_rp), *pick(v_sh, v_rp)]
```

```python
import math

import jax
import jax.numpy as jnp
from jax import lax
from jax.experimental import pallas as pl
from jax.experimental.pallas import tpu as pltpu

f32 = jnp.float32
MXU_DTYPE = jnp.bfloat16

EPS = 1e-6
LRU_WIDTH = 512
LRU_HEADS = 8
LRU_CONV = 4
LRU_C = 8.0
SB_WIDTH = 512
SB_HEAD_DIM = 64
SSM_WIDTH = 1024
SSM_HEADS = 16
SSM_HEAD_DIM = 64
SSM_GROUPS = 2
SSM_STATE = 128
SSM_CONV = 4
SSM_CHUNK = 128
SSM_XBC = SSM_WIDTH + 2 * SSM_GROUPS * SSM_STATE
CONF_WIDTH = 512
CONF_KERNEL = 31
LANES = 128
N_DEV = 8

ADAM_LR = 0.001
ADAM_B1 = 0.9
ADAM_B2 = 0.999
ADAM_EPS = 1e-08
ADAM_WD = 0.01
ADAM_STEP = 10

VMEM_LIMIT = 56 * 1024 * 1024


def _cparams(*sem):
    return pltpu.CompilerParams(dimension_semantics=sem, vmem_limit_bytes=VMEM_LIMIT)


def _mm(a, b):
    return jnp.dot(a.astype(MXU_DTYPE), b.astype(MXU_DTYPE), preferred_element_type=f32)


def _mm_nt(a, b):
    return lax.dot_general(a.astype(MXU_DTYPE), b.astype(MXU_DTYPE), (((1,), (1,)), ((), ())),
                           preferred_element_type=f32)


def _mm_tn(a, b):
    return lax.dot_general(a.astype(MXU_DTYPE), b.astype(MXU_DTYPE), (((0,), (0,)), ((), ())),
                           preferred_element_type=f32)


def _mm_exact(a, b):
    return jnp.dot(a, b, preferred_element_type=f32, precision=lax.Precision.HIGHEST)


@jax.custom_vjp
def dmm(a, b):
    return _mm(a, b)


def _dmm_fwd(a, b):
    return _mm(a, b), (a, b)


def _dmm_bwd(res, g):
    a, b = res
    return _mm_nt(g, b), _mm_tn(a, g)


dmm.defvjp(_dmm_fwd, _dmm_bwd)


@jax.custom_vjp
def dmm_nt(a, b):
    return _mm_nt(a, b)


def _dmm_nt_fwd(a, b):
    return _mm_nt(a, b), (a, b)


def _dmm_nt_bwd(res, g):
    a, b = res
    return _mm(g, b), _mm_tn(g, a)


dmm_nt.defvjp(_dmm_nt_fwd, _dmm_nt_bwd)


@jax.custom_vjp
def dmm_tn(a, b):
    return _mm_tn(a, b)


def _dmm_tn_fwd(a, b):
    return _mm_tn(a, b), (a, b)


def _dmm_tn_bwd(res, g):
    a, b = res
    return _mm_nt(b, g), _mm(a, g)


dmm_tn.defvjp(_dmm_tn_fwd, _dmm_tn_bwd)


def _rms(x, g):
    r = lax.rsqrt(jnp.mean(x * x, axis=-1, keepdims=True) + EPS)
    return x * r * g


def _rms_bwd(dy, x, g):
    r = lax.rsqrt(jnp.mean(x * x, axis=-1, keepdims=True) + EPS)
    dyg = dy * g
    dx = r * dyg - x * (r * r * r * jnp.mean(dyg * x, axis=-1, keepdims=True))
    return dx, dy * x * r


def _tok(tm, n):
    return pl.BlockSpec((tm, n), lambda i: (i, 0))


def _whole(shape):
    nd = len(shape)
    return pl.BlockSpec(tuple(shape), lambda i: (0,) * nd)


def _acc_rows(ref, val):
    s = jnp.sum(val, axis=0, keepdims=True)

    @pl.when(pl.program_id(0) == 0)
    def _():
        ref[...] = s

    @pl.when(pl.program_id(0) != 0)
    def _():
        ref[...] += s


TOKEN_TILE = 512
WEIGHT_GRAD_TOKENS = 1024


class Part:
    def __init__(self, whole, start, size, axis):
        self.whole, self.start, self.size, self.axis = whole, start, size, axis
        self.shape = tuple(size if a == axis else n for a, n in enumerate(whole.shape))


def _weights(ws):
    wholes, readers = [], []
    for w in ws:
        arr = w.whole if isinstance(w, Part) else w
        idx = next((i for i, a in enumerate(wholes) if a is arr), None)
        if idx is None:
            wholes.append(arr)
            idx = len(wholes) - 1
        if isinstance(w, Part):
            rows = pl.ds(w.start, w.size) if w.axis == 0 else slice(None)
            cols = pl.ds(w.start, w.size) if w.axis == 1 else slice(None)
            readers.append(lambda refs, idx=idx, rows=rows, cols=cols: refs[idx][rows, cols])
        else:
            readers.append(lambda refs, idx=idx: refs[idx][...])
    return wholes, readers


def norm_matmul(h, g, ws, out_dtypes, name):
    t, d = h.shape
    tm = TOKEN_TILE
    wholes, readers = _weights(ws)
    nw = len(wholes)

    def body(h_ref, g_ref, *refs):
        hn = _rms(h_ref[...], g_ref[...]).astype(MXU_DTYPE)
        for read, o_ref in zip(readers, refs[nw:]):
            o_ref[...] = jnp.dot(hn, read(refs[:nw]), preferred_element_type=f32).astype(o_ref.dtype)

    return pl.pallas_call(
        body, name=name, grid=(t // tm,),
        in_specs=[_tok(tm, d), _whole(g.shape)] + [_whole(w.shape) for w in wholes],
        out_specs=[_tok(tm, w.shape[1]) for w in ws],
        out_shape=[jax.ShapeDtypeStruct((t, w.shape[1]), dt) for w, dt in zip(ws, out_dtypes)],
        compiler_params=_cparams("parallel"),
    )(h, g, *wholes)


def matmul_residual_norm(xs, ws, h, g, name, relu2=False):
    t, d = h.shape
    tm = TOKEN_TILE
    nx = len(xs)
    wholes, readers = _weights(ws)
    nw = len(wholes)

    def body(*refs):
        x_refs, w_refs = refs[:nx], refs[nx:nx + nw]
        h_ref, g_ref, ho_ref, m_ref = refs[nx + nw:]
        m = None
        for x_ref, read in zip(x_refs, readers):
            x = x_ref[...]
            if relu2:
                x = jnp.square(jnp.maximum(x.astype(f32), 0.0))
            part = jnp.dot(x.astype(MXU_DTYPE), read(w_refs), preferred_element_type=f32)
            m = part if m is None else m + part
        m_ref[...] = m.astype(m_ref.dtype)
        ho_ref[...] = h_ref[...] + _rms(m, g_ref[...])

    return pl.pallas_call(
        body, name=name, grid=(t // tm,),
        in_specs=[_tok(tm, x.shape[1]) for x in xs] + [_whole(w.shape) for w in wholes]
        + [_tok(tm, d), _whole(g.shape)],
        out_specs=[_tok(tm, d), _tok(tm, d)],
        out_shape=[jax.ShapeDtypeStruct((t, d), f32), jax.ShapeDtypeStruct((t, d), MXU_DTYPE)],
        compiler_params=_cparams("parallel"),
    )(*xs, *wholes, h, g)


def ple_forward(h, p, w_gate, w_proj, g, name, target=None):
    t, d = h.shape
    tm = TOKEN_TILE
    last = target is not None

    def body(h_ref, p_ref, wg_ref, wp_ref, g_ref, *refs):
        hh = h_ref[...]
        gl = jnp.dot(hh.astype(MXU_DTYPE), wg_ref[...], preferred_element_type=f32)
        emb = jnp.dot(p_ref[...].astype(MXU_DTYPE), wp_ref[...], preferred_element_type=f32)
        y = hh + _rms(jax.nn.sigmoid(gl) * emb, g_ref[...])
        if last:
            t_ref, l_ref, dy_ref, gl_ref, emb_ref = refs
            e = y - t_ref[...]
            dy_ref[...] = e * (1.0 / d)
            part = jnp.sum(jnp.sum(e * e, axis=1, keepdims=True), axis=0, keepdims=True) * (0.5 / d)
            _acc_rows(l_ref, jnp.broadcast_to(part, (1, LANES)))
        else:
            y_ref, gl_ref, emb_ref = refs
            y_ref[...] = y
        gl_ref[...] = gl.astype(gl_ref.dtype)
        emb_ref[...] = emb.astype(emb_ref.dtype)

    saved = [jax.ShapeDtypeStruct((t, d), MXU_DTYPE)] * 2
    in_specs = [_tok(tm, d), _tok(tm, p.shape[1]), _whole(w_gate.shape), _whole(w_proj.shape), _whole(g.shape)]
    if last:
        return pl.pallas_call(
            body, name=name, grid=(t // tm,),
            in_specs=in_specs + [_tok(tm, d)],
            out_specs=[_whole((1, LANES))] + [_tok(tm, d)] * 3,
            out_shape=[jax.ShapeDtypeStruct((1, LANES), f32), jax.ShapeDtypeStruct((t, d), f32)] + saved,
            compiler_params=_cparams("arbitrary"),
        )(h, p, w_gate, w_proj, g, target)
    return pl.pallas_call(
        body, name=name, grid=(t // tm,),
        in_specs=in_specs,
        out_specs=[_tok(tm, d)] * 3,
        out_shape=[jax.ShapeDtypeStruct((t, d), f32)] + saved,
        compiler_params=_cparams("parallel"),
    )(h, p, w_gate, w_proj, g)


def bwd_through_norm_in(dh, gs, wts, h, g, name):
    t, d = h.shape
    tm = TOKEN_TILE
    ng = len(gs)
    wholes, readers = _weights(wts)
    nw = len(wholes)

    def body(*refs):
        dh_ref = refs[0]
        g_refs, w_refs = refs[1:1 + ng], refs[1 + ng:1 + ng + nw]
        h_ref, gain_ref, dho_ref, dg_ref = refs[1 + ng + nw:]
        dhn = None
        for g_ref, read in zip(g_refs, readers):
            part = jnp.dot(g_ref[...].astype(MXU_DTYPE), read(w_refs), preferred_element_type=f32)
            dhn = part if dhn is None else dhn + part
        dx, dgr = _rms_bwd(dhn, h_ref[...], gain_ref[...])
        dho_ref[...] = dh_ref[...] + dx
        _acc_rows(dg_ref, dgr)

    return pl.pallas_call(
        body, name=name, grid=(t // tm,),
        in_specs=[_tok(tm, d)] + [_tok(tm, x.shape[1]) for x in gs] + [_whole(w.shape) for w in wholes]
        + [_tok(tm, d), _whole(g.shape)],
        out_specs=[_tok(tm, d), _whole((1, d))],
        out_shape=[jax.ShapeDtypeStruct((t, d), f32), jax.ShapeDtypeStruct((1, d), f32)],
        compiler_params=_cparams("arbitrary"),
    )(dh, *gs, *wholes, h, g)


def bwd_through_norm_out(dh, n, g, wts, out_dtypes, name, relu2_of=None, rider=None):
    t, d = n.shape
    tm = TOKEN_TILE
    nw = len(wts)
    wholes, readers = _weights(wts)
    nwh = len(wholes)
    has_a = relu2_of is not None

    def body(*refs):
        dh_ref, n_ref, gain_ref = refs[:3]
        w_refs = refs[3:3 + nwh]
        rest = refs[3 + nwh:]
        if has_a:
            a_ref, rest = rest[0], rest[1:]
        dn_ref, dx_refs, dg_ref = rest[0], rest[1:1 + nw], rest[1 + nw]
        dn, dgr = _rms_bwd(dh_ref[...], n_ref[...].astype(f32), gain_ref[...])
        dnb = dn.astype(MXU_DTYPE)
        dn_ref[...] = dnb.astype(dn_ref.dtype)
        for read, dx_ref in zip(readers, dx_refs):
            dx = jnp.dot(dnb, read(w_refs), preferred_element_type=f32)
            if has_a:
                dx = dx * (2.0 * jnp.maximum(a_ref[...].astype(f32), 0.0))
            dx_ref[...] = dx.astype(dx_ref.dtype)
        _acc_rows(dg_ref, dgr)

    ins = [dh, n, g, *wholes] + ([relu2_of] if has_a else [])
    in_specs = [_tok(tm, d), _tok(tm, d), _whole(g.shape)] + [_whole(w.shape) for w in wholes]
    if has_a:
        in_specs.append(_tok(tm, relu2_of.shape[1]))
    grid = (t // tm,)
    body, in_specs, out_specs, out_shape, scratch = _ride(
        rider, body, in_specs, [_tok(tm, d)] + [_tok(tm, w.shape[1]) for w in wts] + [_whole((1, d))],
        [jax.ShapeDtypeStruct((t, d), MXU_DTYPE)]
        + [jax.ShapeDtypeStruct((t, w.shape[1]), dt) for w, dt in zip(wts, out_dtypes)]
        + [jax.ShapeDtypeStruct((1, d), f32)], [], grid)
    outs = pl.pallas_call(
        body, name=name, grid=grid, in_specs=in_specs, out_specs=out_specs, out_shape=out_shape,
        scratch_shapes=scratch, compiler_params=_cparams("arbitrary"),
    )(*ins, *(rider.inputs if rider else []))
    if rider:
        return outs[0], list(outs[1:1 + nw]), outs[1 + nw], list(outs[2 + nw:])
    return outs[0], list(outs[1:1 + nw]), outs[1 + nw]


def ple_backward(dh3, h2, gl, emb, g, w_gate_t, name):
    t, d = h2.shape
    tm = TOKEN_TILE

    def body(dh_ref, gl_ref, emb_ref, gain_ref, wt_ref, dho_ref, dgl_ref, demb_ref, dg_ref):
        gate = jax.nn.sigmoid(gl_ref[...].astype(f32))
        emb = emb_ref[...].astype(f32)
        dge, dgr = _rms_bwd(dh_ref[...], gate * emb, gain_ref[...])
        demb_ref[...] = (dge * gate).astype(demb_ref.dtype)
        dgl = (dge * emb * gate * (1.0 - gate)).astype(MXU_DTYPE)
        dgl_ref[...] = dgl.astype(dgl_ref.dtype)
        dho_ref[...] = dh_ref[...] + jnp.dot(dgl, wt_ref[...], preferred_element_type=f32)
        _acc_rows(dg_ref, dgr)

    return pl.pallas_call(
        body, name=name, grid=(t // tm,),
        in_specs=[_tok(tm, d), _tok(tm, d), _tok(tm, d), _whole(g.shape), _whole(w_gate_t.shape)],
        out_specs=[_tok(tm, d), _tok(tm, d), _tok(tm, d), _whole((1, d))],
        out_shape=[jax.ShapeDtypeStruct((t, d), f32), jax.ShapeDtypeStruct((t, d), MXU_DTYPE),
                   jax.ShapeDtypeStruct((t, d), MXU_DTYPE), jax.ShapeDtypeStruct((1, d), f32)],
        compiler_params=_cparams("arbitrary"),
    )(dh3, gl, emb, g, w_gate_t)


def _largest_tile(n, cap):
    if n <= cap:
        return n
    return max(c for c in range(LANES, cap + 1, LANES) if n % c == 0)


def weight_grad(x, gout, name, prologue="none", gain=None):
    t, k = x.shape
    n = gout.shape[1]
    tt = WEIGHT_GRAD_TOKENS
    tn = _largest_tile(n, 1024)
    tk = k if prologue == "rms" else _largest_tile(k, 1024)
    has_gain = prologue == "rms"

    def body(*refs):
        if has_gain:
            x_ref, gain_ref, g_ref, o_ref = refs
        else:
            x_ref, g_ref, o_ref = refs
        x = x_ref[...].astype(f32)
        if prologue == "relu2":
            x = jnp.square(jnp.maximum(x, 0.0))
        elif prologue == "rms":
            x = _rms(x, gain_ref[...])
        part = _mm_tn(x, g_ref[...])

        @pl.when(pl.program_id(2) == 0)
        def _():
            o_ref[...] = part

        @pl.when(pl.program_id(2) != 0)
        def _():
            o_ref[...] += part

    in_specs = [pl.BlockSpec((tt, tk), lambda i, j, s: (s, i))]
    ins = [x]
    if has_gain:
        in_specs.append(pl.BlockSpec(gain.shape, lambda i, j, s: (0, 0)))
        ins.append(gain)
    in_specs.append(pl.BlockSpec((tt, tn), lambda i, j, s: (s, j)))
    ins.append(gout)
    return pl.pallas_call(
        body, name=name, grid=(k // tk, n // tn, t // tt),
        in_specs=in_specs,
        out_specs=pl.BlockSpec((tk, tn), lambda i, j, s: (i, j)),
        out_shape=jax.ShapeDtypeStruct((k, n), f32),
        compiler_params=_cparams("parallel", "parallel", "arbitrary"),
    )(*ins)


def weight_grads_of_norm(x, gain, gouts, name):
    t, k = x.shape
    tt = TOKEN_TILE
    ng = len(gouts)

    def body(x_ref, gain_ref, *refs):
        xn = _rms(x_ref[...], gain_ref[...]).astype(MXU_DTYPE)
        first = pl.program_id(0) == 0
        for g_ref, o_ref in zip(refs[:ng], refs[ng:]):
            _accum(o_ref, _mm_tn(xn, g_ref[...]), first)

    return pl.pallas_call(
        body, name=name, grid=(t // tt,),
        in_specs=[_tok(tt, k), _whole(gain.shape)] + [_tok(tt, g.shape[1]) for g in gouts],
        out_specs=[_whole((k, g.shape[1])) for g in gouts],
        out_shape=[jax.ShapeDtypeStruct((k, g.shape[1]), f32) for g in gouts],
        compiler_params=_cparams("arbitrary"),
    )(x, gain, *gouts)


SEQ_TILE = 256
HALO = 8


def _first_step():
    return jnp.logical_and(pl.program_id(0) == 0, pl.program_id(1) == 0)


def _accum(ref, val, first):
    @pl.when(first)
    def _():
        ref[...] = val

    @pl.when(jnp.logical_not(first))
    def _():
        ref[...] += val


def _softplus(x):
    return jnp.maximum(x, 0.0) + jnp.log1p(jnp.exp(-jnp.abs(x)))


def _neg_expm1(z):
    series = -z * (1.0 + z * (0.5 + z * (1.0 / 6.0 + z * (1.0 / 24.0 + z * (1.0 / 120.0)))))
    return jnp.where(z > -0.05, series, 1.0 - jnp.exp(z))


def _lru_gates(xc, ga, gab, gx, gxb, lam):
    r = jax.nn.sigmoid(dmm(xc, ga) + gab)
    i = jax.nn.sigmoid(dmm(xc, gx) + gxb)
    log_a = -LRU_C * r * _softplus(-lam)
    a = jnp.exp(log_a)
    u = jnp.sqrt(_neg_expm1(2.0 * log_a)) * (i * xc)
    return a, u


def _scan_down(a, u):
    n = a.shape[0]
    rows = lax.broadcasted_iota(jnp.int32, a.shape, 0)
    d = 1
    while d < n:
        keep = rows >= d
        a_s = jnp.where(keep, pltpu.roll(a, d, 0), 1.0)
        u_s = jnp.where(keep, pltpu.roll(u, d, 0), 0.0)
        u = a * u_s + u
        a = a * a_s
        d *= 2
    return a, u


def _scan_up(b, g):
    n = b.shape[0]
    rows = lax.broadcasted_iota(jnp.int32, b.shape, 0)
    d = 1
    while d < n:
        keep = rows < n - d
        b_s = jnp.where(keep, pltpu.roll(b, n - d, 0), 1.0)
        g_s = jnp.where(keep, pltpu.roll(g, n - d, 0), 0.0)
        g = g + b * g_s
        b = b * b_s
        d *= 2
    return g


def _seq_specs(ts, c, nt, reverse=False):
    per = ts // HALO

    def jj(j):
        return (nt - 1 - j) if reverse else j

    tile = pl.BlockSpec((1, ts, c), lambda b, j: (b, jj(j), 0))
    before = pl.BlockSpec((1, HALO, c), lambda b, j: (b, jnp.maximum(jj(j) * per - 1, 0), 0))
    after = pl.BlockSpec((1, HALO, c), lambda b, j: (b, jnp.minimum((jj(j) + 1) * per, nt * per - 1), 0))
    return tile, before, after


def _const2(shape):
    nd = len(shape)
    return pl.BlockSpec(tuple(shape), lambda b, j: (0,) * nd)


def lru_forward(xpre, gate, cw, cb, ga, gab, gx, gxb, lam, name):
    nb, ns, w = xpre.shape
    ts = SEQ_TILE
    nt = ns // ts
    tile, _, _ = _seq_specs(ts, w, nt)

    def body(xp_ref, gt_ref, cw_ref, cb_ref, ga_ref, gab_ref, gx_ref, gxb_ref, lam_ref,
             y_ref, xc_ref, hs_ref, xin, hcar):
        @pl.when(pl.program_id(1) == 0)
        def _():
            xin[0:HALO, :] = jnp.zeros((HALO, w), f32)
            hcar[...] = jnp.zeros_like(hcar)

        xin[HALO:HALO + ts, :] = xp_ref[0]
        xc = jnp.broadcast_to(cb_ref[...], (ts, w))
        for k in range(LRU_CONV):
            xc = xc + cw_ref[k:k + 1, :] * xin[pl.ds(HALO - LRU_CONV + 1 + k, ts), :]
        xin[0:HALO, :] = xin[ts:ts + HALO, :]
        a, u = _lru_gates(xc, ga_ref[...], gab_ref[...], gx_ref[...], gxb_ref[...], lam_ref[...])
        acum, h = _scan_down(a, u)
        h = h + acum * hcar[0:1, :]
        hcar[0:1, :] = h[ts - 1:ts, :]
        xc_ref[0] = xc
        hs_ref[0] = h
        y_ref[0] = (h * jax.nn.gelu(gt_ref[0])).astype(y_ref.dtype)

    params = [cw, cb, ga, gab, gx, gxb, lam]
    return pl.pallas_call(
        body, name=name, grid=(nb, nt),
        in_specs=[tile, tile] + [_const2(p.shape) for p in params],
        out_specs=[tile, tile, tile],
        out_shape=[jax.ShapeDtypeStruct((nb, ns, w), MXU_DTYPE), jax.ShapeDtypeStruct((nb, ns, w), f32),
                   jax.ShapeDtypeStruct((nb, ns, w), f32)],
        scratch_shapes=[pltpu.VMEM((ts + HALO, w), f32), pltpu.VMEM((HALO, w), f32)],
        compiler_params=_cparams("arbitrary", "arbitrary"),
    )(xpre, gate, *params)


def lru_backward(dy, xpre, gate, xc, hs, cw, cb, ga, gab, gx, gxb, lam, name, rider=None):
    nb, ns, w = xpre.shape
    ts = SEQ_TILE
    nt = ns // ts
    tile, before, _ = _seq_specs(ts, w, nt, reverse=True)

    def body(dy_ref, xp_ref, xpb_ref, gt_ref, xc_ref, hs_ref, hsb_ref,
             cw_ref, cb_ref, ga_ref, gab_ref, gx_ref, gxb_ref, lam_ref,
             dxp_ref, dgt_ref, dcw_ref, dcb_ref, dga_ref, dgab_ref, dgx_ref, dgxb_ref, dlam_ref,
             dxc_ext, gcar, xin):
        j = pl.program_id(1)
        first = _first_step()
        at_seq_start = j == nt - 1

        @pl.when(j == 0)
        def _():
            dxc_ext[ts:ts + HALO, :] = jnp.zeros((HALO, w), f32)
            gcar[...] = jnp.zeros_like(gcar)

        gt = gt_ref[0]
        h = hs_ref[0]
        dyv = dy_ref[0].astype(f32)
        gl, gelu_vjp = jax.vjp(jax.nn.gelu, gt)
        dgt_ref[0] = gelu_vjp(dyv * h)[0].astype(dgt_ref.dtype)
        dh = dyv * gl

        (a, _), gates_vjp = jax.vjp(_lru_gates, xc_ref[0], ga_ref[...], gab_ref[...], gx_ref[...], gxb_ref[...],
                                    lam_ref[...])
        rows = lax.broadcasted_iota(jnp.int32, (ts, w), 0)
        dh = dh + jnp.where(rows == ts - 1, gcar[0:1, :], 0.0)
        b = pltpu.roll(a, ts - 1, 0)
        g = _scan_up(b, dh)
        gcar[0:1, :] = a[0:1, :] * g[0:1, :]
        hprev_row = jnp.where(at_seq_start, 0.0, hsb_ref[0][HALO - 1:HALO, :])
        hprev = jnp.where(rows == 0, hprev_row, pltpu.roll(h, 1, 0))
        dxc, dga, dgab, dgx, dgxb, dlam = gates_vjp((g * hprev, g))

        _accum(dga_ref, dga, first)
        _accum(dgx_ref, dgx, first)
        _accum(dgab_ref, dgab, first)
        _accum(dgxb_ref, dgxb, first)
        _accum(dlam_ref, dlam, first)
        _accum(dcb_ref, jnp.sum(dxc, axis=0, keepdims=True), first)

        dxc_ext[0:ts, :] = dxc
        dxp = jnp.zeros((ts, w), f32)
        for k in range(LRU_CONV):
            dxp = dxp + cw_ref[k:k + 1, :] * dxc_ext[pl.ds(LRU_CONV - 1 - k, ts), :]
        dxp_ref[0] = dxp.astype(dxp_ref.dtype)
        dxc_ext[ts:ts + HALO, :] = dxc[0:HALO, :]

        xin[0:HALO, :] = jnp.where(at_seq_start, 0.0, xpb_ref[0])
        xin[HALO:HALO + ts, :] = xp_ref[0]
        dcw_rows = [jnp.sum(dxc * xin[pl.ds(HALO - LRU_CONV + 1 + k, ts), :], axis=0, keepdims=True)
                    for k in range(LRU_CONV)]
        dcw_rows += [jnp.zeros((1, w), f32)] * (HALO - LRU_CONV)
        _accum(dcw_ref, jnp.concatenate(dcw_rows, axis=0), first)

    params = [cw, cb, ga, gab, gx, gxb, lam]
    pshape = lambda p: jax.ShapeDtypeStruct(p.shape, f32)
    grid = (nb, nt)
    n_main = 3 + len(params) - 1
    body, in_specs, out_specs, out_shape, scratch = _ride(
        rider, body, [tile, tile, before, tile, tile, tile, before] + [_const2(p.shape) for p in params],
        [tile, tile, _const2((HALO, w))] + [_const2(p.shape) for p in params[1:]],
        [jax.ShapeDtypeStruct((nb, ns, w), MXU_DTYPE), jax.ShapeDtypeStruct((nb, ns, w), MXU_DTYPE),
         jax.ShapeDtypeStruct((HALO, w), f32)] + [pshape(p) for p in params[1:]],
        [pltpu.VMEM((ts + HALO, w), f32), pltpu.VMEM((HALO, w), f32), pltpu.VMEM((ts + HALO, w), f32)], grid)
    outs = pl.pallas_call(
        body, name=name, grid=grid, in_specs=in_specs, out_specs=out_specs, out_shape=out_shape,
        scratch_shapes=scratch, compiler_params=_cparams("arbitrary", "arbitrary"),
    )(dy, xpre, xpre, gate, xc, hs, hs, *params, *(rider.inputs if rider else []))
    return (list(outs[:n_main]), list(outs[n_main:])) if rider else outs


SB_TILE = 256


def _split_dot(x, m):
    hi = x.astype(MXU_DTYPE)
    lo = (x - hi.astype(f32)).astype(MXU_DTYPE)
    return jnp.dot(hi, m, preferred_element_type=f32) + jnp.dot(lo, m, preferred_element_type=f32)


def _suffix_matrices(n):
    r = lax.broadcasted_iota(jnp.int32, (n, n), 0)
    c = lax.broadcasted_iota(jnp.int32, (n, n), 1)
    return (r > c).astype(MXU_DTYPE), (r >= c).astype(MXU_DTYPE)


LOG2E = 1.4426950408889634


def _sb_logits(qh, kb, strict):
    z = _mm_nt(qh, kb)
    ls = jnp.minimum(z, 0.0) - jnp.log2(1.0 + jnp.exp2(-jnp.abs(z)))
    lk = ls - z
    if strict is not None:
        lk = jnp.where(strict, lk, 0.0)
    return ls, lk


def _head_masked(x, dtype):
    lane = lax.broadcasted_iota(jnp.int32, x.shape, 1)
    return (jnp.where(lane < SB_HEAD_DIM, x, 0.0).astype(dtype), jnp.where(lane >= SB_HEAD_DIM, x, 0.0).astype(dtype))


def _stack_heads(dst, x, tq):
    x0, x1 = _head_masked(x, dst.dtype)
    for blk in range(dst.shape[0]):
        dst[blk, 0:tq, :] = x0[blk * tq:(blk + 1) * tq]
        dst[blk, tq:2 * tq, :] = x1[blk * tq:(blk + 1) * tq]


def _strict_mask(tq):
    rr = lax.broadcasted_iota(jnp.int32, (2 * tq, tq), 0)
    cc = lax.broadcasted_iota(jnp.int32, (2 * tq, tq), 1)
    return cc < jnp.where(rr >= tq, rr - tq, rr)


def _sb_specs(ns):
    npair = SB_WIDTH // LANES
    q = pl.BlockSpec((1, ns, LANES), lambda b, p: (b, 0, p))
    k = pl.BlockSpec((1, ns, LANES), lambda b, p: (b, 0, npair + p))
    v = pl.BlockSpec((1, ns, LANES), lambda b, p: (b, 0, 2 * npair + p))
    return q, k, v, npair


def sb_forward(qkv, name, rider=None):
    nb, ns, _ = qkv.shape
    tq = SB_TILE
    nq = ns // tq
    qspec, kspec, vspec, npair = _sb_specs(ns)

    def body(q_ref, k_ref, v_ref, o_ref, qs, ks, vs, acc):
        scale = 1.0 / math.sqrt(SB_HEAD_DIM)
        _stack_heads(qs, q_ref[0] * (scale * LOG2E), tq)
        ks[...] = k_ref[0].astype(MXU_DTYPE)
        _stack_heads(vs, v_ref[0], tq)
        mx, _ = _suffix_matrices(tq)
        strict = _strict_mask(tq)

        def step(q2, blks, r2, masked):
            kbs = [ks[pl.ds(pl.multiple_of(b * tq, tq), tq), :] for b in blks]
            lg = [_sb_logits(q2, kb, strict if masked else None) for kb in kbs]
            sums = [jnp.dot(lk.astype(MXU_DTYPE), mx, preferred_element_type=f32) for _, lk in lg]
            total = None
            for (ls, lk), s, b in zip(lg, sums, blks):
                a = r2 + s
                w = jnp.exp2(ls + a)
                if masked:
                    w = jnp.where(strict, w, 0.0)
                wb = w.astype(MXU_DTYPE)
                part = (jnp.dot(wb[:tq], vs[b, 0:tq, :], preferred_element_type=f32)
                        + jnp.dot(wb[tq:], vs[b, tq:2 * tq, :], preferred_element_type=f32))
                total = part if total is None else total + part
                r2 = a[:, 0:1] + lk[:, 0:1]
            acc[...] += total
            return r2

        def q_block(qi, carry):
            acc[...] = jnp.zeros_like(acc)
            q2 = qs[qi]
            r2 = step(q2, [qi], jnp.zeros((2 * tq, 1), f32), True)
            r2 = lax.fori_loop(0, lax.shift_right_logical(qi, 2),
                               lambda i, r: step(q2, [qi - 1 - 4 * i - u for u in range(4)], r, False), r2)
            r2 = lax.cond(jnp.bitwise_and(qi, 2) == 2,
                          lambda r: step(q2, [jnp.bitwise_and(qi, 3) - 1, jnp.bitwise_and(qi, 3) - 2], r, False),
                          lambda r: r, r2)
            lax.cond(jnp.bitwise_and(qi, 1) == 1, lambda r: step(q2, [0], r, False), lambda r: r, r2)
            o_ref[0, pl.ds(pl.multiple_of(qi * tq, tq), tq), :] = acc[...]
            return carry

        lax.fori_loop(0, nq, q_block, 0)

    grid = (nb, npair)
    body, in_specs, out_specs, out_shape, scratch = _ride(
        rider, body, [qspec, kspec, vspec], [pl.BlockSpec((1, ns, LANES), lambda b, p: (b, 0, p))],
        [jax.ShapeDtypeStruct((nb, ns, SB_WIDTH), f32)],
        [pltpu.VMEM((nq, 2 * tq, LANES), MXU_DTYPE), pltpu.VMEM((ns, LANES), MXU_DTYPE),
         pltpu.VMEM((nq, 2 * tq, LANES), MXU_DTYPE), pltpu.VMEM((tq, LANES), f32)], grid)
    outs = pl.pallas_call(
        body, name=name, grid=grid, in_specs=in_specs, out_specs=out_specs, out_shape=out_shape,
        scratch_shapes=scratch,
        compiler_params=_cparams("arbitrary", "arbitrary") if rider else _cparams("parallel", "parallel"),
    )(qkv, qkv, qkv, *(rider.inputs if rider else []))
    return (outs[0], list(outs[1:])) if rider else outs[0]


def sb_backward(qkv, o, do, name, rider=None):
    nb, ns, _ = qkv.shape
    tq = SB_TILE
    nq = ns // tq
    qspec, kspec, vspec, npair = _sb_specs(ns)
    ospec = pl.BlockSpec((1, ns, LANES), lambda b, p: (b, 0, p))

    def body(q_ref, k_ref, v_ref, o_ref, do_ref, dq_ref, dk_ref, dv_ref, qs, ks, kcat, vs, dos, dqacc, dkacc, dvacc):
        scale = 1.0 / math.sqrt(SB_HEAD_DIM)
        _stack_heads(qs, q_ref[0] * (scale * LOG2E), tq)
        ks[...] = k_ref[0].astype(MXU_DTYPE)
        _stack_heads(kcat, k_ref[0], tq)
        vs[...] = v_ref[0].astype(MXU_DTYPE)
        _stack_heads(dos, do_ref[0].astype(f32), tq)
        dkacc[...] = jnp.zeros_like(dkacc)
        dvacc[...] = jnp.zeros_like(dvacc)
        mx, mi = _suffix_matrices(tq)
        strict = _strict_mask(tq)

        def step(q2, do2, q2t, do2t, dtot2, blks, carry, masked):
            r2, g2 = carry
            k0s = [pl.multiple_of(b * tq, tq) for b in blks]
            lg = [_sb_logits(q2, ks[pl.ds(k0, tq), :], strict if masked else None) for k0 in k0s]
            dws = [_mm_nt(do2, vs[pl.ds(k0, tq), :]) for k0 in k0s]
            sums = [jnp.dot(lk.astype(MXU_DTYPE), mx, preferred_element_type=f32) for _, lk in lg]
            wbs, es = [], []
            for (ls, lk), s in zip(lg, sums):
                a = r2 + s
                w = jnp.exp2(ls + a)
                if masked:
                    w = jnp.where(strict, w, 0.0)
                wbs.append(w.astype(MXU_DTYPE))
                r2 = a[:, 0:1] + lk[:, 0:1]
            es = [wb.astype(f32) * dw for wb, dw in zip(wbs, dws)]
            esums = [_split_dot(e, mi) for e in es]
            dq = None
            for (ls, _), e, esum, wb, b, k0 in zip(lg, es, esums, wbs, blks, k0s):
                esuf = g2 + esum
                beta = jnp.exp2(ls)
                dz = e - beta * (e + (dtot2 - esuf))
                if masked:
                    dz = jnp.where(strict, dz, 0.0)
                dzb = dz.astype(MXU_DTYPE)
                part = (jnp.dot(dzb[:tq], kcat[b, 0:tq, :], preferred_element_type=f32)
                        + jnp.dot(dzb[tq:], kcat[b, tq:2 * tq, :], preferred_element_type=f32))
                dq = part if dq is None else dq + part
                dkacc[:, pl.ds(k0, tq)] += jnp.dot(q2t, dzb, preferred_element_type=f32)
                dvacc[:, pl.ds(k0, tq)] += jnp.dot(do2t, wb, preferred_element_type=f32)
                g2 = esuf[:, 0:1]
            dqacc[...] += dq
            return r2, g2

        def q_block(qi, carry):
            dqacc[...] = jnp.zeros_like(dqacc)
            q2, do2 = qs[qi], dos[qi]
            q2t, do2t = q2.T, do2.T
            ov = o_ref[0, pl.ds(pl.multiple_of(qi * tq, tq), tq), :]
            dtot2 = jnp.sum(do2.astype(f32) * jnp.concatenate([ov, ov], axis=0), axis=1, keepdims=True)
            zero = jnp.zeros((2 * tq, 1), f32)
            args = (q2, do2, q2t, do2t, dtot2)
            c = step(*args, [qi], (zero, zero), True)
            c = lax.fori_loop(0, lax.shift_right_logical(qi, 2),
                              lambda i, c: step(*args, [qi - 1 - 4 * i - u for u in range(4)], c, False), c)
            c = lax.cond(jnp.bitwise_and(qi, 2) == 2,
                         lambda c: step(*args, [jnp.bitwise_and(qi, 3) - 1, jnp.bitwise_and(qi, 3) - 2], c, False),
                         lambda c: c, c)
            lax.cond(jnp.bitwise_and(qi, 1) == 1, lambda c: step(*args, [0], c, False), lambda c: c, c)
            dq_ref[0, pl.ds(pl.multiple_of(qi * tq, tq), tq), :] = (dqacc[...] * scale).astype(dq_ref.dtype)
            return carry

        lax.fori_loop(0, nq, q_block, 0)
        dk_ref[0] = (dkacc[...].T * (1.0 / LOG2E)).astype(dk_ref.dtype)
        dv_ref[0] = dvacc[...].T.astype(dv_ref.dtype)

    dshape = jax.ShapeDtypeStruct((nb, ns, SB_WIDTH), MXU_DTYPE)
    stacked = pltpu.VMEM((nq, 2 * tq, LANES), MXU_DTYPE)
    flat = pltpu.VMEM((ns, LANES), MXU_DTYPE)
    grid = (nb, npair)
    body, in_specs, out_specs, out_shape, scratch = _ride(
        rider, body, [qspec, kspec, vspec, ospec, ospec], [ospec, ospec, ospec], [dshape, dshape, dshape],
        [stacked, flat, stacked, flat, stacked,
         pltpu.VMEM((tq, LANES), f32), pltpu.VMEM((LANES, ns), f32), pltpu.VMEM((LANES, ns), f32)], grid)
    outs = pl.pallas_call(
        body, name=name, grid=grid, in_specs=in_specs, out_specs=out_specs, out_shape=out_shape,
        scratch_shapes=scratch,
        compiler_params=_cparams("arbitrary", "arbitrary") if rider else _cparams("parallel", "parallel"),
    )(qkv, qkv, qkv, o, do, *(rider.inputs if rider else []))
    return (list(outs[:3]), list(outs[3:])) if rider else list(outs)


SSM_PAIRS = SSM_HEADS // 2
PAIRS_PER_GROUP = SSM_PAIRS // SSM_GROUPS
GROUP_WIDTH = SSM_WIDTH // SSM_GROUPS


def _silu(x):
    return x * jax.nn.sigmoid(x)


def _ssd_chunk(xs_pre, b_pre, c_pre, dt_raw, dt_raw_t, z, st, dt_bias_r, dt_bias_c, a_log_r, a_log_c, d_skip,
               gains):
    n = dt_raw.shape[0]
    rows = lax.broadcasted_iota(jnp.int32, (n, n), 0)
    cols = lax.broadcasted_iota(jnp.int32, (n, n), 1)
    tril = cols <= rows
    tri_l = tril.astype(f32)
    tri_u = (rows <= cols).astype(f32)
    lane = lax.broadcasted_iota(jnp.int32, (n, LANES), 1)
    sub = lax.broadcasted_iota(jnp.int32, (LANES, n), 0)

    dt = _softplus(dt_raw + dt_bias_r)
    a_r = -jnp.exp(a_log_r)
    da = dt * a_r
    acs = _mm_exact(tri_l, da)
    dt_t = _softplus(dt_raw_t + dt_bias_c)
    acs_t = _mm_exact(dt_t * (-jnp.exp(a_log_c)), tri_u)

    bs = [_silu(b) for b in b_pre]
    cs = [_silu(c) for c in c_pre]
    cb = [dmm_nt(cs[g], bs[g]) for g in range(SSM_GROUPS)]

    end = jnp.sum(da, axis=0, keepdims=True)
    lane_row = lax.broadcasted_iota(jnp.int32, (1, LANES), 1)
    first_head = lane < SSM_HEAD_DIM
    first_head_row = lane_row < SSM_HEAD_DIM

    def head_col(v, h):
        return jnp.sum(jnp.where((lane if v.shape[0] == n else lane_row) == h, v, 0.0), axis=1, keepdims=True)

    ys, st_new = [], []
    for p in range(SSM_PAIRS):
        g = p // PAIRS_PER_GROUP
        h0, h1 = 2 * p, 2 * p + 1
        xs = _silu(xs_pre[p])
        acols = [head_col(acs, h0), head_col(acs, h1)]
        dt_p = jnp.where(first_head, head_col(dt, h0), head_col(dt, h1))
        acs_p = jnp.where(first_head, acols[0], acols[1])
        end_p = jnp.where(first_head_row, head_col(end, h0), head_col(end, h1))
        dsk_p = jnp.where(first_head_row, head_col(d_skip, h0), head_col(d_skip, h1))
        xdt = xs * dt_p
        y = jnp.exp(acs_p) * dmm(cs[g], st[p])
        for hh in range(2):
            row = jnp.sum(jnp.where(sub == 2 * p + hh, acs_t, 0.0), axis=0, keepdims=True)
            decay = jnp.where(tril, jnp.exp(jnp.where(tril, acols[hh] - row, 0.0)), 0.0)
            head = first_head if hh == 0 else jnp.logical_not(first_head)
            y = y + dmm(cb[g] * decay, jnp.where(head, xdt, 0.0))
        st_new.append(jnp.exp(end_p) * st[p] + dmm_tn(bs[g], xdt * jnp.exp(end_p - acs_p)))
        ys.append(y + dsk_p * xs)
    out = []
    for g in range(SSM_GROUPS):
        yg = jnp.concatenate(ys[g * PAIRS_PER_GROUP:(g + 1) * PAIRS_PER_GROUP], axis=1) * _silu(z[g])
        out.append(_rms(yg, gains[g]))
    return out, st_new


def _ssd_chunk_inputs(xconv, dtr, z, st_ref, gain):
    xs_pre = [xconv[:, LANES * p:LANES * (p + 1)] for p in range(SSM_PAIRS)]
    b0 = SSM_WIDTH
    c0 = SSM_WIDTH + SSM_GROUPS * SSM_STATE
    b_pre = [xconv[:, b0 + SSM_STATE * g:b0 + SSM_STATE * (g + 1)] for g in range(SSM_GROUPS)]
    c_pre = [xconv[:, c0 + SSM_STATE * g:c0 + SSM_STATE * (g + 1)] for g in range(SSM_GROUPS)]
    zs = [z[:, GROUP_WIDTH * g:GROUP_WIDTH * (g + 1)] for g in range(SSM_GROUPS)]
    sts = [st_ref[p] for p in range(SSM_PAIRS)]
    gains = [gain[:, GROUP_WIDTH * g:GROUP_WIDTH * (g + 1)] for g in range(SSM_GROUPS)]
    return xs_pre, b_pre, c_pre, dtr, dtr.T, zs, sts, gains


def ssd_forward(xbc, dt_raw, z, cw, cb, dbr, dbc, alr, alc, dsk, gain, name):
    nb, ns, wx = xbc.shape
    ln = SSM_CHUNK
    nt = ns // ln
    tile = lambda c: pl.BlockSpec((1, ln, c), lambda b, j: (b, j, 0))
    st_spec = pl.BlockSpec((1, 1, SSM_PAIRS, SSM_STATE, LANES), lambda b, j: (b, j, 0, 0, 0))

    def body(xbc_ref, dt_ref, z_ref, cw_ref, cb_ref, dbr_ref, dbc_ref, alr_ref, alc_ref, dsk_ref, gain_ref,
             y_ref, xconv_ref, stp_ref, xin, st):
        @pl.when(pl.program_id(1) == 0)
        def _():
            xin[0:HALO, :] = jnp.zeros((HALO, wx), f32)
            st[...] = jnp.zeros_like(st)

        xin[HALO:HALO + ln, :] = xbc_ref[0]
        xconv = jnp.broadcast_to(cb_ref[...], (ln, wx))
        for k in range(SSM_CONV):
            xconv = xconv + cw_ref[k:k + 1, :] * xin[pl.ds(HALO - SSM_CONV + 1 + k, ln), :]
        xin[0:HALO, :] = xin[ln:ln + HALO, :]
        xconv_ref[0] = xconv
        stp_ref[0, 0] = st[...]
        xs_pre, b_pre, c_pre, dtr, dtr_t, zs, sts, gains = _ssd_chunk_inputs(xconv, dt_ref[0], z_ref[0], st,
                                                                             gain_ref[...])
        out, st_new = _ssd_chunk(xs_pre, b_pre, c_pre, dtr, dtr_t, zs, sts, dbr_ref[...], dbc_ref[...],
                                 alr_ref[...], alc_ref[...], dsk_ref[...], gains)
        y_ref[0] = jnp.concatenate(out, axis=1).astype(y_ref.dtype)
        for p in range(SSM_PAIRS):
            st[p] = st_new[p]

    params = [cw, cb, dbr, dbc, alr, alc, dsk, gain]
    return pl.pallas_call(
        body, name=name, grid=(nb, nt),
        in_specs=[tile(wx), tile(LANES), tile(SSM_WIDTH)] + [_const2(p.shape) for p in params],
        out_specs=[tile(SSM_WIDTH), tile(wx), st_spec],
        out_shape=[jax.ShapeDtypeStruct((nb, ns, SSM_WIDTH), MXU_DTYPE), jax.ShapeDtypeStruct((nb, ns, wx), f32),
                   jax.ShapeDtypeStruct((nb, nt, SSM_PAIRS, SSM_STATE, LANES), f32)],
        scratch_shapes=[pltpu.VMEM((ln + HALO, wx), f32), pltpu.VMEM((SSM_PAIRS, SSM_STATE, LANES), f32)],
        compiler_params=_cparams("arbitrary", "arbitrary"),
    )(xbc, dt_raw, z, *params)


def ssd_backward(dy, xbc, xconv, dt_raw, z, stp, cw, cb, dbr, dbc, alr, alc, dsk, gain, name):
    nb, ns, wx = xbc.shape
    ln = SSM_CHUNK
    nt = ns // ln
    per = ln // HALO
    rj = lambda j: nt - 1 - j
    tile = lambda c: pl.BlockSpec((1, ln, c), lambda b, j: (b, rj(j), 0))
    before = pl.BlockSpec((1, HALO, wx), lambda b, j: (b, jnp.maximum(rj(j) * per - 1, 0), 0))
    st_spec = pl.BlockSpec((1, 1, SSM_PAIRS, SSM_STATE, LANES), lambda b, j: (b, rj(j), 0, 0, 0))

    def body(dy_ref, xbc_ref, xbcb_ref, xconv_ref, dt_ref, z_ref, stp_ref,
             cw_ref, cb_ref, dbr_ref, dbc_ref, alr_ref, alc_ref, dsk_ref, gain_ref,
             dxbc_ref, ddt_ref, dz_ref, dcw_ref, dcb_ref, ddbr_ref, ddbc_ref, dalr_ref, dalc_ref, ddsk_ref, dgain_ref,
             dxc_ext, dst, xin):
        j = pl.program_id(1)
        first = _first_step()
        at_seq_start = j == nt - 1

        @pl.when(j == 0)
        def _():
            dxc_ext[ln:ln + HALO, :] = jnp.zeros((HALO, wx), f32)
            dst[...] = jnp.zeros_like(dst)

        xs_pre, b_pre, c_pre, dtr, dtr_t, zs, sts, gains = _ssd_chunk_inputs(xconv_ref[0], dt_ref[0], z_ref[0],
                                                                             stp_ref.at[0, 0], gain_ref[...])
        _, vjp = jax.vjp(_ssd_chunk, xs_pre, b_pre, c_pre, dtr, dtr_t, zs, sts, dbr_ref[...], dbc_ref[...],
                         alr_ref[...], alc_ref[...], dsk_ref[...], gains)
        dyv = dy_ref[0].astype(f32)
        cot = ([dyv[:, GROUP_WIDTH * g:GROUP_WIDTH * (g + 1)] for g in range(SSM_GROUPS)],
               [dst[p] for p in range(SSM_PAIRS)])
        dxs, db, dc, ddt, ddt_t, dzs, dsts, ddbr, ddbc, dalr, dalc, ddsk, dgains = vjp(cot)
        for p in range(SSM_PAIRS):
            dst[p] = dsts[p]
        ddt_ref[0] = (ddt + ddt_t.T).astype(ddt_ref.dtype)
        dz_ref[0] = jnp.concatenate(dzs, axis=1).astype(dz_ref.dtype)
        _accum(ddbr_ref, ddbr, first)
        _accum(ddbc_ref, ddbc, first)
        _accum(dalr_ref, dalr, first)
        _accum(dalc_ref, dalc, first)
        _accum(ddsk_ref, ddsk, first)
        _accum(dgain_ref, jnp.concatenate(dgains, axis=1), first)

        dxc = jnp.concatenate(dxs + db + dc, axis=1)
        _accum(dcb_ref, jnp.sum(dxc, axis=0, keepdims=True), first)
        dxc_ext[0:ln, :] = dxc
        dxp = jnp.zeros((ln, wx), f32)
        for k in range(SSM_CONV):
            dxp = dxp + cw_ref[k:k + 1, :] * dxc_ext[pl.ds(SSM_CONV - 1 - k, ln), :]
        dxbc_ref[0] = dxp.astype(dxbc_ref.dtype)
        dxc_ext[ln:ln + HALO, :] = dxc[0:HALO, :]

        xin[0:HALO, :] = jnp.where(at_seq_start, 0.0, xbcb_ref[0])
        xin[HALO:HALO + ln, :] = xbc_ref[0]
        dcw_rows = [jnp.sum(dxc * xin[pl.ds(HALO - SSM_CONV + 1 + k, ln), :], axis=0, keepdims=True)
                    for k in range(SSM_CONV)]
        dcw_rows += [jnp.zeros((1, wx), f32)] * (HALO - SSM_CONV)
        _accum(dcw_ref, jnp.concatenate(dcw_rows, axis=0), first)

    params = [cw, cb, dbr, dbc, alr, alc, dsk, gain]
    pshape = lambda p: jax.ShapeDtypeStruct(p.shape, f32)
    return pl.pallas_call(
        body, name=name, grid=(nb, nt),
        in_specs=[tile(SSM_WIDTH), tile(wx), before, tile(wx), tile(LANES), tile(SSM_WIDTH), st_spec]
        + [_const2(p.shape) for p in params],
        out_specs=[tile(wx), tile(LANES), tile(SSM_WIDTH), _const2((HALO, wx))] + [_const2(p.shape) for p in params[1:]],
        out_shape=[jax.ShapeDtypeStruct((nb, ns, wx), MXU_DTYPE), jax.ShapeDtypeStruct((nb, ns, LANES), MXU_DTYPE),
                   jax.ShapeDtypeStruct((nb, ns, SSM_WIDTH), MXU_DTYPE), jax.ShapeDtypeStruct((HALO, wx), f32)]
        + [pshape(p) for p in params[1:]],
        scratch_shapes=[pltpu.VMEM((ln + HALO, wx), f32), pltpu.VMEM((SSM_PAIRS, SSM_STATE, LANES), f32),
                        pltpu.VMEM((ln + HALO, wx), f32)],
        compiler_params=_cparams("arbitrary", "arbitrary"),
    )(dy, xbc, xbc, xconv, dt_raw, z, stp, *params)


CONF_HALO = 32
CONF_OFF = CONF_HALO - CONF_KERNEL + 1


def _conf_specs(ts, c, nt):
    per = ts // CONF_HALO
    tile = pl.BlockSpec((1, ts, c), lambda b, j: (b, j, 0))
    before = pl.BlockSpec((1, CONF_HALO, c), lambda b, j: (b, jnp.maximum(j * per - 1, 0), 0))
    after = pl.BlockSpec((1, CONF_HALO, c), lambda b, j: (b, jnp.minimum((j + 1) * per, nt * per - 1), 0))
    return tile, before, after


SUBLANES = 8


def _shifted_copies(dst, x):
    rows = x.shape[0]
    dst[0] = x
    for b in range(1, SUBLANES):
        dst[b] = pltpu.roll(x, rows - b, 0)


def _window(copies, off, size):
    b = off % SUBLANES
    return copies[b, pl.ds(off - b, size), :]


CONV_ROW_BLOCK = 32


def _taps_blocked(out_ref, copies, cw_ref, offsets, bias=None):
    rows, w = out_ref.shape
    rb = CONV_ROW_BLOCK

    def block(i, carry):
        r0 = pl.multiple_of(i * rb, rb)
        acc = jnp.zeros((rb, w), f32) if bias is None else jnp.broadcast_to(bias, (rb, w))
        for k, off in enumerate(offsets):
            b = off % SUBLANES
            acc = acc + cw_ref[k:k + 1, :] * copies[b, pl.ds(r0 + (off - b), rb), :]
        out_ref[pl.ds(r0, rb), :] = acc
        return carry

    lax.fori_loop(0, rows // rb, block, 0)


def _glu(x):
    return x[:, :CONF_WIDTH] * jax.nn.sigmoid(x[:, CONF_WIDTH:])


def _layernorm_parts(c):
    xc = c - jnp.mean(c, axis=-1, keepdims=True)
    r = lax.rsqrt(jnp.mean(xc * xc, axis=-1, keepdims=True) + EPS)
    return xc * r, r


def conf_forward(glu, cw, cb, ln_g, ln_b, name):
    nb, ns, wg = glu.shape
    w = CONF_WIDTH
    ts = SEQ_TILE
    nt = ns // ts
    tile, before, _ = _conf_specs(ts, wg, nt)

    def body(x_ref, xb_ref, cw_ref, cb_ref, g_ref, b_ref, y_ref, u_rot):
        _shifted_copies(u_rot, jnp.concatenate(
            [jnp.where(pl.program_id(1) == 0, 0.0, _glu(xb_ref[0])), _glu(x_ref[0])], axis=0))
        conv = jnp.broadcast_to(cb_ref[...], (ts, w))
        for k in range(CONF_KERNEL):
            conv = conv + cw_ref[k:k + 1, :] * _window(u_rot, CONF_OFF + k, ts)
        xhat, _ = _layernorm_parts(conv)
        y_ref[0] = _silu(xhat * g_ref[...] + b_ref[...]).astype(y_ref.dtype)

    params = [cw, cb, ln_g, ln_b]
    return pl.pallas_call(
        body, name=name, grid=(nb, nt),
        in_specs=[tile, before] + [_const2(p.shape) for p in params],
        out_specs=pl.BlockSpec((1, ts, w), lambda b, j: (b, j, 0)),
        out_shape=jax.ShapeDtypeStruct((nb, ns, w), MXU_DTYPE),
        scratch_shapes=[pltpu.VMEM((SUBLANES, ts + CONF_HALO, w), f32)],
        compiler_params=_cparams("parallel", "parallel"),
    )(glu, glu, *params)


def conf_backward(dy, glu, cw, cb, ln_g, ln_b, name):
    nb, ns, wg = glu.shape
    w = CONF_WIDTH
    ts = SEQ_TILE
    nt = ns // ts
    te = ts + CONF_HALO
    tile, before, after = _conf_specs(ts, wg, nt)
    dtile, _, dafter = _conf_specs(ts, w, nt)

    def body(dy_ref, dya_ref, x_ref, xb_ref, xa_ref, cw_ref, cb_ref, g_ref, b_ref,
             dx_ref, dcw_ref, dcb_ref, dg_ref, db_ref, u_ext, dc_ext, conv_out, du_out):
        j = pl.program_id(1)
        first = _first_step()
        x = x_ref[0]
        _shifted_copies(u_ext, jnp.concatenate(
            [jnp.where(j == 0, 0.0, _glu(xb_ref[0])), _glu(x), _glu(xa_ref[0])], axis=0))
        _taps_blocked(conv_out, u_ext, cw_ref, [CONF_OFF + k for k in range(CONF_KERNEL)], cb_ref[...])
        xhat, r = _layernorm_parts(conv_out[...])
        lnout = xhat * g_ref[...] + b_ref[...]
        sg = jax.nn.sigmoid(lnout)
        rows = lax.broadcasted_iota(jnp.int32, (te, w), 0)
        dyv = jnp.concatenate([dy_ref[0].astype(f32), dya_ref[0].astype(f32)], axis=0)
        dyv = jnp.where(jnp.logical_and(j == nt - 1, rows >= ts), 0.0, dyv)
        dln = dyv * sg * (1.0 + lnout * (1.0 - sg))
        in_tile = rows < ts
        _accum(dg_ref, jnp.sum(jnp.where(in_tile, dln * xhat, 0.0), axis=0, keepdims=True), first)
        _accum(db_ref, jnp.sum(jnp.where(in_tile, dln, 0.0), axis=0, keepdims=True), first)
        dxh = dln * g_ref[...]
        dconv = r * (dxh - jnp.mean(dxh, axis=-1, keepdims=True) - xhat * jnp.mean(dxh * xhat, axis=-1, keepdims=True))
        _shifted_copies(dc_ext, dconv)
        dct = dconv[0:ts, :]
        _accum(dcb_ref, jnp.sum(dct, axis=0, keepdims=True), first)
        _taps_blocked(du_out, dc_ext, cw_ref, [CONF_KERNEL - 1 - k for k in range(CONF_KERNEL)])
        du = du_out[...]
        dcw_rows = []
        for k in range(CONF_KERNEL):
            dcw_rows.append(jnp.sum(dct * _window(u_ext, CONF_OFF + k, ts), axis=0, keepdims=True))
        dcw_rows.append(jnp.zeros((1, w), f32))
        _accum(dcw_ref, jnp.concatenate(dcw_rows, axis=0), first)
        sb = jax.nn.sigmoid(x[:, w:])
        dx_ref[0] = jnp.concatenate([du * sb, du * x[:, :w] * sb * (1.0 - sb)], axis=1).astype(dx_ref.dtype)

    params = [cw, cb, ln_g, ln_b]
    return pl.pallas_call(
        body, name=name, grid=(nb, nt),
        in_specs=[dtile, dafter, tile, before, after] + [_const2(p.shape) for p in params],
        out_specs=[tile] + [_const2(p.shape) for p in params],
        out_shape=[jax.ShapeDtypeStruct((nb, ns, wg), MXU_DTYPE)] + [jax.ShapeDtypeStruct(p.shape, f32) for p in params],
        scratch_shapes=[pltpu.VMEM((SUBLANES, te + CONF_HALO, w), f32), pltpu.VMEM((SUBLANES, te, w), f32),
                        pltpu.VMEM((te, w), f32), pltpu.VMEM((ts, w), f32)],
        compiler_params=_cparams("arbitrary", "arbitrary"),
    )(dy, dy, glu, glu, glu, *params)


def _row(v):
    return v.reshape(1, -1).astype(f32)


def _pad_to(v, n, axis):
    pads = [(0, 0)] * v.ndim
    pads[axis] = (0, n - v.shape[axis])
    return jnp.pad(v, pads)


def _block_diag(w):
    nh, d, _ = w.shape
    eye = jnp.eye(nh, dtype=w.dtype)
    return (eye[:, None, :, None] * w[:, :, None, :]).reshape(nh * d, nh * d)


def _diag_blocks(m, nh):
    d = m.shape[0] // nh
    idx = jnp.arange(nh)
    return m.reshape(nh, d, nh, d)[idx, :, idx, :]


def _mix_even_fwd(h, gpre, w, wl, nb, ns, rider=None):
    t = nb * ns
    w_in = wl["w_in"]
    w_lx, w_lg = Part(w_in, 0, LRU_WIDTH, 1), Part(w_in, LRU_WIDTH, LRU_WIDTH, 1)
    w_qkv = Part(w_in, 2 * LRU_WIDTH, 3 * SB_WIDTH, 1)
    xpre, gate, qkv = norm_matmul(h, gpre, [w_lx, w_lg, w_qkv], [f32, f32, f32], name="ev_in_proj")
    lru_p = [w["ev_lru_conv_w"][0], _row(w["ev_lru_conv_b"][0]),
             _block_diag(w["ev_lru_gate_a_w"][0]).astype(MXU_DTYPE), _row(w["ev_lru_gate_a_b"][0]),
             _block_diag(w["ev_lru_gate_x_w"][0]).astype(MXU_DTYPE), _row(w["ev_lru_gate_x_b"][0]),
             _row(w["ev_lru_lambda"][0])]
    xpre3, gate3, qkv3 = xpre.reshape(nb, ns, -1), gate.reshape(nb, ns, -1), qkv.reshape(nb, ns, -1)
    y_a, xc, hs = lru_forward(xpre3, gate3, *lru_p, name="ev_lru_fwd")
    o = sb_forward(qkv3, name="ev_sb_fwd", rider=rider)
    carried = None
    if rider is not None:
        o, carried = o
    ys = [y_a.reshape(t, -1), o.reshape(t, -1)]
    saved = dict(xpre=xpre3, gate=gate3, qkv=qkv3, xc=xc, hs=hs, o=o, lru_p=lru_p)
    return ys, saved, carried


def _mix_even_bwd(dys, saved, wtl, nb, ns, lru_rider=None, attention_rider=None):
    t = nb * ns
    dy_a, dy_b = [d.reshape(nb, ns, -1) for d in dys]
    outs = lru_backward(dy_a, saved["xpre"], saved["gate"], saved["xc"], saved["hs"], *saved["lru_p"],
                        name="ev_lru_bwd", rider=lru_rider)
    lru_carried = None
    if lru_rider is not None:
        outs, lru_carried = outs
    dxp, dgt, dcw, dcb, dga, dgab, dgx, dgxb, dlam = outs
    rider = attention_rider(lru_carried) if attention_rider is not None else None
    carried = None
    if rider is None:
        dq, dk, dv = sb_backward(saved["qkv"], saved["o"], dy_b, name="ev_sb_bwd")
    else:
        (dq, dk, dv), carried = sb_backward(saved["qkv"], saved["o"], dy_b, name="ev_sb_bwd", rider=rider)
    w_in_t = wtl["w_in"]
    pieces = [dxp, dgt, dq, dk, dv]
    gs = [d.reshape(t, -1) for d in pieces]
    wts = [Part(w_in_t, LRU_WIDTH * i, LRU_WIDTH, 0) for i in range(5)]
    grads = {
        "ev_lru_conv_w": dcw[:LRU_CONV][None], "ev_lru_conv_b": dcb,
        "ev_lru_gate_a_w": _diag_blocks(dga, LRU_HEADS)[None], "ev_lru_gate_a_b": dgab,
        "ev_lru_gate_x_w": _diag_blocks(dgx, LRU_HEADS)[None], "ev_lru_gate_x_b": dgxb,
        "ev_lru_lambda": dlam,
    }
    return gs, wts, grads, carried


def _odd_params(w):
    ssd_p = [w["od_ssm_conv_w"][0], _row(w["od_ssm_conv_b"][0]),
             _pad_to(_row(w["od_ssm_dt_bias"][0]), LANES, 1), _pad_to(_row(w["od_ssm_dt_bias"][0]), LANES, 1).T,
             _pad_to(_row(w["od_ssm_a_log"][0]), LANES, 1), _pad_to(_row(w["od_ssm_a_log"][0]), LANES, 1).T,
             _pad_to(_row(w["od_ssm_d"][0]), LANES, 1), _row(w["od_ssm_norm"][0])]
    conf_p = [_pad_to(w["od_cm_conv_w"][0], CONF_HALO, 0), _row(w["od_cm_conv_b"][0]),
              _row(w["od_cm_ln_g"][0]), _row(w["od_cm_ln_b"][0])]
    return ssd_p, conf_p


ODD_SPLITS = (SSM_WIDTH, SSM_WIDTH + SSM_XBC, SSM_WIDTH + SSM_XBC + SSM_HEADS)


def _mix_odd_fwd(h, gpre, w, wl, nb, ns, rider=None):
    assert rider is None
    t = nb * ns
    w_in = wl["w_in"]
    s0, s1, s2 = ODD_SPLITS
    w_al = jnp.concatenate([w_in[:, :s1], w_in[:, s2:], _pad_to(w_in[:, s1:s2], LANES, 1)], axis=1)
    widths = (s0, s1 - s0, w_in.shape[1] - s2, LANES)
    starts = (0, s0, s1, s1 + widths[2])
    zz, xbc, glu, dtr = norm_matmul(h, gpre, [Part(w_al, a, n, 1) for a, n in zip(starts, widths)], [f32] * 4,
                                    name="od_in_proj")
    ssd_p, conf_p = _odd_params(w)
    zz3, xbc3, dtr3, glu3 = [a.reshape(nb, ns, -1) for a in (zz, xbc, dtr, glu)]
    y_c, xconv, stp = ssd_forward(xbc3, dtr3, zz3, *ssd_p, name="od_ssd_fwd")
    y_d = conf_forward(glu3, *conf_p, name="od_conf_fwd")
    ys = [y_c.reshape(t, -1), y_d.reshape(t, -1)]
    saved = dict(z=zz3, xbc=xbc3, dtr=dtr3, glu=glu3, xconv=xconv, stp=stp, ssd_p=ssd_p, conf_p=conf_p)
    return ys, saved, None


def _mix_odd_bwd(dys, saved, wtl, nb, ns, lru_rider=None, attention_rider=None):
    assert lru_rider is None and attention_rider is None
    t = nb * ns
    dy_c, dy_d = [d.reshape(nb, ns, -1) for d in dys]
    outs = ssd_backward(dy_c, saved["xbc"], saved["xconv"], saved["dtr"], saved["z"], saved["stp"], *saved["ssd_p"],
                        name="od_ssd_bwd")
    dxbc, ddt, dz, dcw, dcb, ddbr, ddbc, dalr, dalc, ddsk, dgain = outs
    dglu, ccw, ccb, clg, clb = conf_backward(dy_d, saved["glu"], *saved["conf_p"], name="od_conf_bwd")
    w_in_t = wtl["w_in"]
    s0, s1, s2 = ODD_SPLITS
    carried = None
    gs = [d.reshape(t, -1) for d in (dz, dxbc, dglu, ddt)]
    wt_al = jnp.concatenate([w_in_t[:s1], w_in_t[s2:], _pad_to(w_in_t[s1:s2], LANES, 0)], axis=0)
    widths = (s0, s1 - s0, w_in_t.shape[0] - s2, LANES)
    starts = (0, s0, s1, s1 + widths[2])
    wts = [Part(wt_al, a, n, 0) for a, n in zip(starts, widths)]
    nh = SSM_HEADS
    grads = {
        "od_ssm_conv_w": dcw[:SSM_CONV][None], "od_ssm_conv_b": dcb,
        "od_ssm_dt_bias": ddbr[:, :nh] + ddbc[:nh, 0][None], "od_ssm_a_log": dalr[:, :nh] + dalc[:nh, 0][None],
        "od_ssm_d": ddsk[:, :nh], "od_ssm_norm": dgain,
        "od_cm_conv_w": ccw[:CONF_KERNEL][None], "od_cm_conv_b": ccb, "od_cm_ln_g": clg, "od_cm_ln_b": clb,
    }
    return gs, wts, grads, carried


LAYER_MATRICES = ("w_in", "w_out", "mlp_w1", "mlp_w2", "ple_w_proj", "ple_w_gate")
NORM_NAMES = ("norm_mix_pre", "norm_mix_post", "norm_mlp_pre", "norm_mlp_post", "norm_ple")


class NoOverlap:
    sums, from_chips = {}, {}

    def attention_fwd_rider(self):
        return None

    def weights_arrived(self, carried, wl, wtl):
        raise NotImplementedError

    def mlp_bwd_rider(self, layer1_grads):
        return None

    def after_mlp_bwd(self, carried):
        pass

    def lru_bwd_rider(self, layer0_grads):
        return None

    def attention_bwd_rider(self, carried):
        return None

    def after_attention_bwd(self, carried):
        pass


OUT_SPLIT = (LRU_WIDTH, SSM_WIDTH)


def local_step(x, p, target, w, wl, wtl, comm=NoOverlap()):
    nb, ns, d = x.shape
    t = nb * ns
    h = x.reshape(t, d)
    depth = p.shape[0]
    wl, wtl = list(wl), list(wtl)
    tapes = []
    for i in range(depth):
        even = i % 2 == 0
        tag = f"l{i}_"
        gpre = _row(w["norm_mix_pre"][i])
        rider = comm.attention_fwd_rider() if i == 0 else None
        ys, saved, carried = (_mix_even_fwd if even else _mix_odd_fwd)(h, gpre, w, wl[i], nb, ns, rider)
        if rider is not None:
            wl, wtl = comm.weights_arrived(carried, wl, wtl)
        w_out = wl[i]["w_out"]
        split = OUT_SPLIT[i % 2]
        w_outs = [Part(w_out, 0, split, 0), Part(w_out, split, w_out.shape[0] - split, 0)]
        h1, m = matmul_residual_norm(ys, w_outs, h, _row(w["norm_mix_post"][i]), name=tag + "out_proj")
        a1, = norm_matmul(h1, _row(w["norm_mlp_pre"][i]), [wl[i]["mlp_w1"]], [MXU_DTYPE], name=tag + "mlp_up")
        h2, f = matmul_residual_norm([a1], [wl[i]["mlp_w2"]], h1, _row(w["norm_mlp_post"][i]), name=tag + "mlp_down",
                                     relu2=True)
        pi = p[i].reshape(t, -1)
        ple_args = (h2, pi, wl[i]["ple_w_gate"], wl[i]["ple_w_proj"], _row(w["norm_ple"][i]))
        if i < depth - 1:
            h3, gl, emb = ple_forward(*ple_args, name=tag + "ple")
        else:
            loss_row, dh, gl, emb = ple_forward(*ple_args, name=tag + "ple_loss", target=target.reshape(t, d))
            h3 = None
        tapes.append(dict(h=h, ys=ys, w_outs=w_outs, saved=saved, h1=h1, m=m, a1=a1, h2=h2, f=f, pi=pi, gl=gl,
                          emb=emb))
        h = h3

    grads = {}
    norm_grads = {k: [None] * depth for k in NORM_NAMES}
    layer_grads = [None] * depth
    for i in reversed(range(depth)):
        even = i % 2 == 0
        tag = f"l{i}_"
        tp = tapes[i]
        lg = {}
        to_sibling = comm.mlp_bwd_rider(layer_grads[1]) if i == 0 else None
        dh2, dgl, demb, dg = ple_backward(dh, tp["h2"], tp["gl"], tp["emb"], _row(w["norm_ple"][i]),
                                          wtl[i]["ple_w_gate"], name=tag + "ple_bwd")
        norm_grads["norm_ple"][i] = dg
        lg["ple_w_gate"] = weight_grad(tp["h2"], dgl, name=tag + "dw_gate")
        lg["ple_w_proj"] = weight_grad(tp["pi"], demb, name=tag + "dw_proj")
        outs = bwd_through_norm_out(dh2, tp["f"], _row(w["norm_mlp_post"][i]), [wtl[i]["mlp_w2"]], [MXU_DTYPE],
                                    name=tag + "mlp_down_bwd", relu2_of=tp["a1"], rider=to_sibling)
        d_f, (da1,), dg = outs[:3]
        if to_sibling is not None:
            comm.after_mlp_bwd(outs[3])
        norm_grads["norm_mlp_post"][i] = dg
        lg["mlp_w2"] = weight_grad(tp["a1"], d_f, name=tag + "dw2", prologue="relu2")
        gpre = _row(w["norm_mlp_pre"][i])
        dh1, dg = bwd_through_norm_in(dh2, [da1], [wtl[i]["mlp_w1"]], tp["h1"], gpre, name=tag + "mlp_up_bwd")
        norm_grads["norm_mlp_pre"][i] = dg
        lg["mlp_w1"] = weight_grad(tp["h1"], da1, name=tag + "dw1", prologue="rms", gain=gpre)
        wt_out = wtl[i]["w_out"]
        split = tp["w_outs"][0].shape[0]
        dm, dys, dg = bwd_through_norm_out(dh1, tp["m"], _row(w["norm_mix_post"][i]),
                                           [Part(wt_out, 0, split, 1),
                                            Part(wt_out, split, wt_out.shape[1] - split, 1)],
                                           [f32, MXU_DTYPE if even else f32],
                                           name=tag + "out_proj_bwd")
        norm_grads["norm_mix_post"][i] = dg
        lg["w_out"] = jnp.concatenate([weight_grad(y, dm, name=tag + f"dw_out{k}") for k, y in enumerate(tp["ys"])],
                                      axis=0)
        lru_rider = comm.lru_bwd_rider(lg) if i == 0 else None
        gs, wts, mix_grads, carried = (_mix_even_bwd if even else _mix_odd_bwd)(
            dys, tp["saved"], wtl[i], nb, ns, lru_rider, comm.attention_bwd_rider if lru_rider is not None else None)
        if carried is not None:
            comm.after_attention_bwd(carried)
        grads.update(mix_grads)
        gpre = _row(w["norm_mix_pre"][i])
        dh, dg = bwd_through_norm_in(dh1, gs, wts, tp["h"], gpre, name=tag + "in_proj_bwd")
        norm_grads["norm_mix_pre"][i] = dg
        dw_in = weight_grads_of_norm(tp["h"], gpre, gs, name=tag + "dw_in")
        if not even:
            dw_in = [dw_in[0], dw_in[1], dw_in[3][:, :SSM_HEADS], dw_in[2]]
        lg["w_in"] = jnp.concatenate(dw_in, axis=1)
        layer_grads[i] = lg
    for k, v in norm_grads.items():
        grads[k] = jnp.concatenate(v, axis=0)
    return loss_row[0, 0], dh.reshape(nb, ns, d), grads, layer_grads


MESH_ID = pl.DeviceIdType.MESH
ANY = pl.BlockSpec(memory_space=pl.ANY)


def _mesh_pos():
    return lax.axis_index("x"), lax.axis_index("y"), lax.axis_index("c")


def all_gather(shards, name):
    return _run_alone(gather_rider(shards), name)


class Rider:
    def __init__(self, inputs, out_shapes, scratch_shapes, start, finish, middle=None):
        self.inputs, self.out_shapes, self.scratch_shapes = list(inputs), list(out_shapes), list(scratch_shapes)
        self.start, self.finish, self.middle = start, finish, middle


def _run_alone(rider, name):
    ni, no = len(rider.inputs), len(rider.out_shapes)

    def body(*refs):
        args = (refs[:ni], refs[ni:ni + no], refs[ni + no:])
        rider.start(*args)
        if rider.middle is not None:
            rider.middle(*args)
        rider.finish(*args)

    return pl.pallas_call(
        body, name=name, out_shape=rider.out_shapes, in_specs=[ANY] * ni, out_specs=[ANY] * no,
        scratch_shapes=rider.scratch_shapes,
    )(*rider.inputs)


def _ride(rider, body, in_specs, out_specs, out_shape, scratch_shapes, grid):
    in_specs, out_specs, out_shape = list(in_specs), list(out_specs), list(out_shape)
    scratch_shapes = list(scratch_shapes)
    if rider is None:
        return body, in_specs, out_specs, out_shape, scratch_shapes
    n_in, n_out, n_scr = len(in_specs), len(out_specs), len(scratch_shapes)
    ri, ro = len(rider.inputs), len(rider.out_shapes)
    total = math.prod(grid)

    def carrying(*refs):
        ins, r_ins = refs[:n_in], refs[n_in:n_in + ri]
        o0 = n_in + ri
        outs, r_outs = refs[o0:o0 + n_out], refs[o0 + n_out:o0 + n_out + ro]
        s0 = o0 + n_out + ro
        scr, r_scr = refs[s0:s0 + n_scr], refs[s0 + n_scr:]
        step = pl.program_id(0)
        for ax in range(1, len(grid)):
            step = step * grid[ax] + pl.program_id(ax)
        args = (r_ins, r_outs, r_scr)
        pl.when(step == 0)(lambda: rider.start(*args))
        if rider.middle is not None:
            pl.when(step == total // 2)(lambda: rider.middle(*args))
        body(*ins, *outs, *scr)
        pl.when(step == total - 1)(lambda: rider.finish(*args))

    return (carrying, in_specs + [ANY] * ri, out_specs + [ANY] * ro, out_shape + rider.out_shapes,
            scratch_shapes + rider.scratch_shapes)


def gather_rider(shards):
    n = len(shards)

    def parts(x_refs, out_refs, scr):
        send_sems, recv_sems, local_sems = scr
        x, y, c = _mesh_pos()
        chips = [(1 - x, y), (x, 1 - y), (1 - x, 1 - y)]

        def slot(a, px, py, pc):
            return out_refs[a].at[4 * px + 2 * py + pc]

        def copy(a, k, block, to, src=None):
            return pltpu.make_async_remote_copy(
                src_ref=slot(a, *block) if src is None else src, dst_ref=slot(a, *block),
                send_sem=send_sems.at[7 * a + k], recv_sem=recv_sems.at[7 * a + k], device_id=to,
                device_id_type=MESH_ID)

        me, sibling = (x, y, c), (x, y, 1 - c)
        def mine():
            return [pltpu.make_async_copy(x_refs[a], slot(a, *me), local_sems.at[a]) for a in range(n)]

        def first():
            out = []
            for j, chip in enumerate(chips):
                out += [copy(a, 1 + j, me, (*chip, c), src=x_refs[a]) for a in range(n)]
            return out + [copy(a, 0, me, sibling, src=x_refs[a]) for a in range(n)]

        def passed(j):
            return [copy(a, 4 + j, (*chips[j], c), sibling) for a in range(n)]

        return me, sibling, chips, c, copy, mine, first, passed

    def start(x_refs, out_refs, scr):
        _, _, _, _, _, mine, first, _ = parts(x_refs, out_refs, scr)
        for cp in mine() + first():
            cp.start()

    def middle(x_refs, out_refs, scr):
        me, _, chips, c, copy, _, _, passed = parts(x_refs, out_refs, scr)
        for j, chip in enumerate(chips):
            for a, fwd in enumerate(passed(j)):
                copy(a, 1 + j, (*chip, c), me).wait_recv()
                fwd.start()

    def finish(x_refs, out_refs, scr):
        me, sibling, chips, c, copy, mine, first, passed = parts(x_refs, out_refs, scr)
        for a in range(n):
            copy(a, 0, sibling, me).wait_recv()
        for j, chip in enumerate(chips):
            for a in range(n):
                copy(a, 4 + j, (*chip, 1 - c), me).wait_recv()
        for cp in first() + [cp for j in range(len(chips)) for cp in passed(j)]:
            cp.wait_send()
        for cp in mine():
            cp.wait()

    return Rider(shards, [jax.ShapeDtypeStruct((N_DEV,) + s.shape, s.dtype) for s in shards],
                 [pltpu.SemaphoreType.DMA((7 * n,)), pltpu.SemaphoreType.DMA((7 * n,)), pltpu.SemaphoreType.DMA((n,))],
                 start, finish, middle)


def scatter_to_sibling(parts, name):
    return _run_alone(sibling_rider(parts), name)


def sibling_rider(parts):
    n = len(parts)

    def copies(g_refs, out_refs, scr):
        send_sems, recv_sems = scr
        x, y, c = _mesh_pos()
        return [pltpu.make_async_remote_copy(
            src_ref=g_refs[a].at[2 * chip + (1 - c)], dst_ref=out_refs[a].at[chip],
            send_sem=send_sems.at[4 * a + chip], recv_sem=recv_sems.at[4 * a + chip], device_id=(x, y, 1 - c),
            device_id_type=MESH_ID) for a in range(n) for chip in range(4)]

    def start(*refs):
        for cp in copies(*refs):
            cp.start()

    def finish(*refs):
        cps = copies(*refs)
        for cp in cps:
            cp.wait_recv()
        for cp in cps:
            cp.wait_send()

    return Rider(parts, [jax.ShapeDtypeStruct((4,) + p.shape[1:], p.dtype) for p in parts],
                 [pltpu.SemaphoreType.DMA((4 * n,)), pltpu.SemaphoreType.DMA((4 * n,))], start, finish)


def scatter_to_chips(partials, name):
    return _run_alone(chips_rider(partials), name)


def chips_rider(partials):
    n = len(partials)

    def copies(p_refs, out_refs, scr):
        send_sems, recv_sems = scr
        x, y, c = _mesh_pos()
        chips = [(1 - x, y), (x, 1 - y), (1 - x, 1 - y)]
        return [pltpu.make_async_remote_copy(
            src_ref=p_refs[a].at[2 * px + py], dst_ref=out_refs[a].at[j],
            send_sem=send_sems.at[3 * a + j], recv_sem=recv_sems.at[3 * a + j], device_id=(px, py, c),
            device_id_type=MESH_ID) for a in range(n) for j, (px, py) in enumerate(chips)]

    def start(*refs):
        for cp in copies(*refs):
            cp.start()

    def finish(*refs):
        cps = copies(*refs)
        for cp in cps:
            cp.wait_recv()
        for cp in cps:
            cp.wait_send()

    return Rider(partials, [jax.ShapeDtypeStruct((3,) + p.shape[1:], p.dtype) for p in partials],
                 [pltpu.SemaphoreType.DMA((3 * n,)), pltpu.SemaphoreType.DMA((3 * n,))], start, finish)


ICI_DTYPE = jnp.bfloat16
ELEMENTWISE_BLOCK_BYTES = 1 << 20


def _row_tile(rows, cols):
    cap = max(16, ELEMENTWISE_BLOCK_BYTES // (4 * cols))
    best = [t for t in range(16, min(rows, cap) + 1, 16) if rows % t == 0]
    return best[-1] if best else rows


def add_sibling_parts(parts, received, core, name):
    _, r, n = parts.shape
    tr = _row_tile(r, n)

    def body(c_ref, a_ref, b_ref, o_ref, ob_ref):
        s = a_ref[...] + b_ref[...]
        o_ref[...] = s
        ob_ref[...] = s.astype(ob_ref.dtype)

    blk = pl.BlockSpec((1, tr, n), lambda i, j, c_ref: (i, j, 0))
    return pl.pallas_call(
        body, name=name,
        grid_spec=pltpu.PrefetchScalarGridSpec(
            num_scalar_prefetch=1, grid=(4, r // tr),
            in_specs=[pl.BlockSpec((1, tr, n), lambda i, j, c_ref: (2 * i + c_ref[0], j, 0)), blk],
            out_specs=[blk, blk]),
        out_shape=[jax.ShapeDtypeStruct((4, r, n), f32), jax.ShapeDtypeStruct((4, r, n), ICI_DTYPE)],
        compiler_params=_cparams("parallel", "parallel"),
    )(core, parts, received)


def _adamw(w, g, m, v):
    m = ADAM_B1 * m + (1.0 - ADAM_B1) * g
    v = ADAM_B2 * v + (1.0 - ADAM_B2) * jnp.square(g)
    m_hat = m / (1.0 - ADAM_B1 ** ADAM_STEP)
    v_hat = v / (1.0 - ADAM_B2 ** ADAM_STEP)
    delta = -ADAM_LR * (m_hat / (jnp.sqrt(v_hat) + ADAM_EPS) + ADAM_WD * w)
    return delta, m, v


def adamw_sharded(partial, received, chip, w, m, v, name):
    _, r, n = partial.shape

    def body(k_ref, p_ref, r_ref, w_ref, m_ref, v_ref, g_out, d_out, m_out, v_out):
        g = p_ref[0] + r_ref[0].astype(f32)
        g = g + r_ref[1].astype(f32)
        g = g + r_ref[2].astype(f32)
        delta, mn, vn = _adamw(w_ref[...], g, m_ref[...], v_ref[...])
        g_out[...] = g
        d_out[...] = delta
        m_out[...] = mn
        v_out[...] = vn

    tr = _row_tile(r, n)
    flat = pl.BlockSpec((tr, n), lambda j, k_ref: (j, 0))
    return pl.pallas_call(
        body, name=name,
        grid_spec=pltpu.PrefetchScalarGridSpec(
            num_scalar_prefetch=1, grid=(r // tr,),
            in_specs=[pl.BlockSpec((1, tr, n), lambda j, k_ref: (k_ref[0], j, 0)),
                      pl.BlockSpec((3, tr, n), lambda j, k_ref: (0, j, 0)), flat, flat, flat],
            out_specs=[flat] * 4),
        out_shape=[jax.ShapeDtypeStruct((r, n), f32)] * 4,
        compiler_params=_cparams("parallel"),
    )(chip, partial, received, w, m, v)


def adamw_replicated(gathered, w, m, v, name):
    _, r, n = gathered.shape

    def body(g_ref, w_ref, m_ref, v_ref, g_out, d_out, m_out, v_out):
        g = g_ref[0]
        for k in range(1, N_DEV):
            g = g + g_ref[k]
        delta, mn, vn = _adamw(w_ref[...], g, m_ref[...], v_ref[...])
        g_out[...] = g
        d_out[...] = delta
        m_out[...] = mn
        v_out[...] = vn

    return pl.pallas_call(
        body, name=name,
        out_shape=[jax.ShapeDtypeStruct((r, n), f32)] * 4,
        compiler_params=pltpu.CompilerParams(vmem_limit_bytes=VMEM_LIMIT),
    )(gathered, w, m, v)


W_NAMES = ['ev_w_in', 'ev_lru_conv_w', 'ev_lru_conv_b', 'ev_lru_gate_a_w', 'ev_lru_gate_a_b', 'ev_lru_gate_x_w',
           'ev_lru_gate_x_b', 'ev_lru_lambda', 'ev_w_out', 'od_w_in', 'od_ssm_conv_w', 'od_ssm_conv_b',
           'od_ssm_dt_bias', 'od_ssm_a_log', 'od_ssm_d', 'od_ssm_norm', 'od_cm_conv_w', 'od_cm_conv_b', 'od_cm_ln_g',
           'od_cm_ln_b', 'od_w_out', 'norm_mix_pre', 'norm_mix_post', 'norm_mlp_pre', 'norm_mlp_post', 'norm_ple',
           'mlp_w1', 'mlp_w2', 'ple_w_proj', 'ple_w_gate']
BIG_SHARDED = {'ev_w_in': 2, 'ev_w_out': 1, 'od_w_in': 2, 'od_w_out': 1, 'mlp_w1': 2, 'mlp_w2': 1, 'ple_w_proj': 2,
               'ple_w_gate': 1}
SMALL_SHARDED = {'ev_lru_conv_w': 2, 'od_ssm_conv_w': 2, 'od_ssm_conv_b': 1, 'od_ssm_norm': 1, 'od_cm_conv_w': 2,
                 'od_cm_conv_b': 1, 'od_cm_ln_g': 1, 'od_cm_ln_b': 1}
SHARDED = {**BIG_SHARDED, **SMALL_SHARDED}
REPLICATED = [n for n in W_NAMES if n not in SHARDED]


def _round_up(n, k):
    return -(-n // k) * k


def _pack_rows(flat, rows_multiple):
    n = flat.shape[0]
    total = _round_up(n, LANES * rows_multiple)
    return jnp.pad(flat, (0, total - n)).reshape(-1, LANES)


def _unpack(flat, shapes):
    out, off = {}, 0
    for name, shape in shapes.items():
        size = math.prod(shape)
        out[name] = flat[off:off + size].reshape(shape)
        off += size
    return out


def _unshard(g8, shape, axis):
    g = jnp.moveaxis(g8.reshape((N_DEV,) + tuple(shape)), 0, axis)
    return g.reshape(tuple(shape[:axis]) + (N_DEV * shape[axis],) + tuple(shape[axis + 1:]))


def _to_shards(g, axis):
    shard = g.shape[axis] // N_DEV
    g = g.reshape(g.shape[:axis] + (N_DEV, shard) + g.shape[axis + 1:])
    return jnp.moveaxis(g, axis, 0)


def _pack_small(tree):
    return _pack_rows(jnp.concatenate([tree[k].astype(f32).reshape(-1) for k in SMALL_SHARDED]), 16)


def _layer_entry(i, key):
    mixer = "ev" if i % 2 == 0 else "od"
    return {"w_in": (mixer + "_w_in", i // 2, 1), "w_out": (mixer + "_w_out", i // 2, 0),
            "mlp_w1": ("mlp_w1", i, 1), "mlp_w2": ("mlp_w2", i, 0),
            "ple_w_proj": ("ple_w_proj", i, 1), "ple_w_gate": ("ple_w_gate", i, 0)}[key]


def _layer_shards(tree, i):
    out = {}
    for key in LAYER_MATRICES:
        name, idx, _ = _layer_entry(i, key)
        out[key] = tree[name][idx]
    return out


def _gather_early(w):
    small = _pack_small(w)
    terms, rest = [], small
    for _ in range(3):
        term = rest.astype(MXU_DTYPE)
        terms.append(term)
        rest = rest - term.astype(f32)
    outs = all_gather([_layer_shards(w, 0)["w_in"].astype(MXU_DTYPE), jnp.concatenate(terms, axis=0)],
                      name="gather_weights")
    w_in = _unshard(outs[0], outs[0].shape[1:], _layer_entry(0, "w_in")[2])
    wl, wtl = {"w_in": w_in}, {"w_in": w_in.T}
    full = {k: w[k] for k in REPLICATED}
    t = outs[-1].astype(f32)
    nr = small.shape[0]
    vals = (t[:, :nr] + t[:, nr:2 * nr] + t[:, 2 * nr:]).reshape(N_DEV, -1)
    off = 0
    for k, axis in SMALL_SHARDED.items():
        size = math.prod(w[k].shape)
        full[k] = _unshard(vals[:, off:off + size], w[k].shape, axis)
        off += size
    return full, wl, wtl


class Overlap:
    LATE = [(0, key) for key in LAYER_MATRICES if key != "w_in"] + [(1, key) for key in LAYER_MATRICES]
    EARLY_GRADS = [(0, key) for key in LAYER_MATRICES if key != "w_in"]

    def __init__(self, w):
        self.w = w
        self.sums, self.from_chips, self.parts = {}, {}, {}

    def attention_fwd_rider(self):
        shards = [_layer_shards(self.w, 0), _layer_shards(self.w, 1)]
        return gather_rider([shards[i][key].astype(MXU_DTYPE) for i, key in self.LATE])

    def weights_arrived(self, carried, wl, wtl):
        wl = [dict(wl[0]), {}]
        wtl = [dict(wtl[0]), {}]
        for (i, key), g8 in zip(self.LATE, carried):
            wl[i][key] = _unshard(g8, g8.shape[1:], _layer_entry(i, key)[2])
            wtl[i][key] = wl[i][key].T
        return wl, wtl

    def _to_sibling(self, ids, layer_grads):
        for i, key in ids:
            self.parts[i, key] = _to_shards(layer_grads[key], _layer_entry(i, key)[2])
        return sibling_rider([self.parts[e] for e in ids])

    def _add(self, ids, carried):
        core = jnp.reshape(lax.axis_index("c"), (1,)).astype(jnp.int32)
        for (i, key), got in zip(ids, carried):
            self.sums[i, key] = add_sibling_parts(self.parts[i, key], got, core, name=f"add_sibling_l{i}_{key}")

    def mlp_bwd_rider(self, layer1_grads):
        return self._to_sibling([(1, key) for key in LAYER_MATRICES], layer1_grads)

    def after_mlp_bwd(self, carried):
        self._add([(1, key) for key in LAYER_MATRICES], carried)

    def lru_bwd_rider(self, layer0_grads):
        return self._to_sibling(self.EARLY_GRADS, layer0_grads)

    def attention_bwd_rider(self, carried):
        self._add(self.EARLY_GRADS, carried)
        self.travelling = list(self.sums)
        return chips_rider([self.sums[e][1] for e in self.travelling])

    def after_attention_bwd(self, carried):
        for e, got in zip(self.travelling, carried):
            self.from_chips[e] = got


def _pack_replicated(tree):
    return _pack_rows(jnp.concatenate([tree[k].astype(f32).reshape(-1) for k in REPLICATED]), 8)


def kernel(x, p, ev_w_in, ev_lru_conv_w, ev_lru_conv_b, ev_lru_gate_a_w, ev_lru_gate_a_b, ev_lru_gate_x_w, ev_lru_gate_x_b, ev_lru_lambda, ev_w_out, od_w_in, od_ssm_conv_w, od_ssm_conv_b, od_ssm_dt_bias, od_ssm_a_log, od_ssm_d, od_ssm_norm, od_cm_conv_w, od_cm_conv_b, od_cm_ln_g, od_cm_ln_b, od_w_out, norm_mix_pre, norm_mix_post, norm_mlp_pre, norm_mlp_post, norm_ple, mlp_w1, mlp_w2, ple_w_proj, ple_w_gate, loss_target, m_ev_w_in, m_ev_lru_conv_w, m_ev_lru_conv_b, m_ev_lru_gate_a_w, m_ev_lru_gate_a_b, m_ev_lru_gate_x_w, m_ev_lru_gate_x_b, m_ev_lru_lambda, m_ev_w_out, m_od_w_in, m_od_ssm_conv_w, m_od_ssm_conv_b, m_od_ssm_dt_bias, m_od_ssm_a_log, m_od_ssm_d, m_od_ssm_norm, m_od_cm_conv_w, m_od_cm_conv_b, m_od_cm_ln_g, m_od_cm_ln_b, m_od_w_out, m_norm_mix_pre, m_norm_mix_post, m_norm_mlp_pre, m_norm_mlp_post, m_norm_ple, m_mlp_w1, m_mlp_w2, m_ple_w_proj, m_ple_w_gate, v_ev_w_in, v_ev_lru_conv_w, v_ev_lru_conv_b, v_ev_lru_gate_a_w, v_ev_lru_gate_a_b, v_ev_lru_gate_x_w, v_ev_lru_gate_x_b, v_ev_lru_lambda, v_ev_w_out, v_od_w_in, v_od_ssm_conv_w, v_od_ssm_conv_b, v_od_ssm_dt_bias, v_od_ssm_a_log, v_od_ssm_d, v_od_ssm_norm, v_od_cm_conv_w, v_od_cm_conv_b, v_od_cm_ln_g, v_od_cm_ln_b, v_od_w_out, v_norm_mix_pre, v_norm_mix_post, v_norm_mlp_pre, v_norm_mlp_post, v_norm_ple, v_mlp_w1, v_mlp_w2, v_ple_w_proj, v_ple_w_gate):
    ws = [ev_w_in, ev_lru_conv_w, ev_lru_conv_b, ev_lru_gate_a_w, ev_lru_gate_a_b, ev_lru_gate_x_w, ev_lru_gate_x_b, ev_lru_lambda, ev_w_out, od_w_in, od_ssm_conv_w, od_ssm_conv_b, od_ssm_dt_bias, od_ssm_a_log, od_ssm_d, od_ssm_norm, od_cm_conv_w, od_cm_conv_b, od_cm_ln_g, od_cm_ln_b, od_w_out, norm_mix_pre, norm_mix_post, norm_mlp_pre, norm_mlp_post, norm_ple, mlp_w1, mlp_w2, ple_w_proj, ple_w_gate]
    ms = [m_ev_w_in, m_ev_lru_conv_w, m_ev_lru_conv_b, m_ev_lru_gate_a_w, m_ev_lru_gate_a_b, m_ev_lru_gate_x_w, m_ev_lru_gate_x_b, m_ev_lru_lambda, m_ev_w_out, m_od_w_in, m_od_ssm_conv_w, m_od_ssm_conv_b, m_od_ssm_dt_bias, m_od_ssm_a_log, m_od_ssm_d, m_od_ssm_norm, m_od_cm_conv_w, m_od_cm_conv_b, m_od_cm_ln_g, m_od_cm_ln_b, m_od_w_out, m_norm_mix_pre, m_norm_mix_post, m_norm_mlp_pre, m_norm_mlp_post, m_norm_ple, m_mlp_w1, m_mlp_w2, m_ple_w_proj, m_ple_w_gate]
    vs = [v_ev_w_in, v_ev_lru_conv_w, v_ev_lru_conv_b, v_ev_lru_gate_a_w, v_ev_lru_gate_a_b, v_ev_lru_gate_x_w, v_ev_lru_gate_x_b, v_ev_lru_lambda, v_ev_w_out, v_od_w_in, v_od_ssm_conv_w, v_od_ssm_conv_b, v_od_ssm_dt_bias, v_od_ssm_a_log, v_od_ssm_d, v_od_ssm_norm, v_od_cm_conv_w, v_od_cm_conv_b, v_od_cm_ln_g, v_od_cm_ln_b, v_od_w_out, v_norm_mix_pre, v_norm_mix_post, v_norm_mlp_pre, v_norm_mlp_post, v_norm_ple, v_mlp_w1, v_mlp_w2, v_ple_w_proj, v_ple_w_gate]
    w = dict(zip(W_NAMES, ws))
    m = dict(zip(W_NAMES, ms))
    v = dict(zip(W_NAMES, vs))
    full, wl0, wtl0 = _gather_early(w)
    comm = Overlap(w)
    loss_local, grad_x, grads, layer_grads = local_step(x, p, loss_target, full, [wl0, None], [wtl0, None], comm)
    loss = lax.psum(loss_local, ("x", "y", "c"))
    return (loss, grad_x, *_reduce_and_update(grads, layer_grads, w, m, v, comm))


def _reduce_and_update(grads, layer_grads, w, m, v, comm):
    mx, my, mc = _mesh_pos()

    entries = [(i, key) for i in range(len(layer_grads)) for key in LAYER_MATRICES]
    left = [e for e in entries if e not in comm.from_chips]
    parts = [_to_shards(layer_grads[i][key], _layer_entry(i, key)[2]) for i, key in left]
    small = jnp.concatenate([_to_shards(grads[k], axis).reshape(N_DEV, -1) for k, axis in SMALL_SHARDED.items()],
                            axis=1)
    small_rows = _pack_small(w).shape[0]
    small = jnp.pad(small, ((0, 0), (0, small_rows * LANES - small.shape[1]))).reshape(N_DEV, small_rows, LANES)
    parts.append(small)
    from_sibling = scatter_to_sibling(parts, name="scatter_sibling")
    core = jnp.reshape(mc, (1,)).astype(jnp.int32)
    sums = [add_sibling_parts(a, b, core, name=f"add_sibling_{i}") for i, (a, b) in enumerate(zip(parts, from_sibling))]
    from_chips = scatter_to_chips([s[1] for s in sums], name="scatter_chips")
    all_sums = {**comm.sums, **dict(zip(left, sums[:-1]))}
    all_from_chips = {**comm.from_chips, **dict(zip(left, from_chips[:-1]))}
    chip = jnp.reshape(2 * mx + my, (1,)).astype(jnp.int32)
    per_layer = []
    for i in range(len(layer_grads)):
        ws, ms, vs = _layer_shards(w, i), _layer_shards(m, i), _layer_shards(v, i)
        per_layer.append({key: adamw_sharded(all_sums[i, key][0], all_from_chips[i, key], chip, ws[key], ms[key],
                                             vs[key], name=f"adamw_l{i}_{key}") for key in LAYER_MATRICES})
    g_sh, d_sh, m_sh, v_sh = {}, {}, {}, {}
    for which, tree in enumerate((g_sh, d_sh, m_sh, v_sh)):
        for i in range(len(per_layer)):
            for key in LAYER_MATRICES:
                name, idx, _ = _layer_entry(i, key)
                tree.setdefault(name, {})[idx] = per_layer[i][key][which]
        for name in BIG_SHARDED:
            tree[name] = jnp.stack([tree[name][idx] for idx in sorted(tree[name])], axis=0)
    outs = adamw_sharded(sums[-1][0], from_chips[-1], chip, _pack_small(w), _pack_small(m), _pack_small(v),
                         name="adamw_small")
    small_shapes = {k: w[k].shape for k in SMALL_SHARDED}
    for tree, o in zip((g_sh, d_sh, m_sh, v_sh), outs):
        tree.update(_unpack(o.reshape(-1), small_shapes))

    rep_parts, = all_gather([_pack_replicated(grads)], name="gather_replicated_grads")
    outs = adamw_replicated(rep_parts, _pack_replicated(w), _pack_replicated(m), _pack_replicated(v),
                            name="adamw_replicated")
    rep_shapes = {k: w[k].shape for k in REPLICATED}
    g_rp, d_rp, m_rp, v_rp = [_unpack(o.reshape(-1), rep_shapes) for o in outs]

    pick = lambda sh, rp: [sh[k] if k in SHARDED else rp[k] for k in W_NAMES]
    return [*pick(g_sh, g_rp), *pick(d_sh, d_rp), *pick(m_sh, m_rp), *pick(v_sh, v_rp)]
```
